```python
import jax, jax.numpy as jnp
from jax import lax
import numpy as np

D_MODEL = 1024
BATCH = 8
SEQ = 8192
DEPTH = 2

ATT_HEADS = 16
ATT_KV_HEADS = 2
ATT_HEAD_DIM = 64
WINDOW = 128
ATT_BLOCK = 128
SSD_EXPAND = 2
SSD_D_INNER = SSD_EXPAND * D_MODEL
SSD_HEAD_DIM = 64
SSD_HEADS = SSD_D_INNER // SSD_HEAD_DIM
SSD_GROUPS = 4
SSD_STATE = 128
SSD_CONV = 4
SSD_CHUNK = 128
FFN_HIDDEN = -(-8 * D_MODEL // (3 * 256)) * 256

LN_EPS = 1e-5
RMS_EPS = 1e-5
DEEPNORM_ALPHA = (2 * DEPTH) ** 0.25
DEEPNORM_BETA = (8 * DEPTH) ** -0.25

Q_DIM = ATT_HEADS * ATT_HEAD_DIM
KV_DIM = ATT_KV_HEADS * ATT_HEAD_DIM
BC_DIM = SSD_GROUPS * SSD_STATE
CONV_DIM = SSD_D_INNER + 2 * BC_DIM
IN_SIZES = (Q_DIM, KV_DIM, KV_DIM, SSD_D_INNER, SSD_D_INNER, BC_DIM, BC_DIM, SSD_HEADS, 2 * D_MODEL)
IN_DIM = sum(IN_SIZES)

kernel_name = 'hybrid_ssd_swa_sink_alibi_deepnorm'


def _split(t, sizes):
    offs = np.cumsum(sizes)[:-1].tolist()
    return jnp.split(t, offs, axis=-1)


def layer_norm(x, g, b):
    xf = x.astype(jnp.float32)
    mu = jnp.mean(xf, axis=-1, keepdims=True)
    var = jnp.mean(jnp.square(xf - mu), axis=-1, keepdims=True)
    return ((xf - mu) * lax.rsqrt(var + LN_EPS) * g + b).astype(x.dtype)


def grouped_rms_norm(y, w):
    yg = y.reshape(*y.shape[:-1], SSD_GROUPS, -1)
    yg = yg * lax.rsqrt(jnp.mean(jnp.square(yg), axis=-1, keepdims=True) + RMS_EPS)
    return yg.reshape(y.shape) * w


def causal_depthwise_conv(u, w, b):
    c = u.shape[-1]
    out = lax.conv_general_dilated(
        u, w[:, None, :].astype(u.dtype), window_strides=(1,),
        padding=[(SSD_CONV - 1, 0)], dimension_numbers=('NWC', 'WIO', 'NWC'),
        feature_group_count=c)
    return out + b


def segsum(a):
    t = a.shape[-1]
    cs = jnp.cumsum(a, axis=-1)
    diff = cs[..., :, None] - cs[..., None, :]
    mask = jnp.tril(jnp.ones((t, t), dtype=bool))
    return jnp.where(mask, diff, -jnp.inf)


def ssd_chunked_scan(xh, dt, a, b_ssm, c_ssm):
    bsz, seqlen, nh, hp = xh.shape
    ng, ns = b_ssm.shape[2], b_ssm.shape[3]
    ne = nh // ng
    nc = seqlen // SSD_CHUNK
    T = SSD_CHUNK
    X = (xh.astype(jnp.float32) * dt[..., None]).reshape(bsz, nc, T, ng, ne, hp)
    dA = jnp.moveaxis((dt * a).reshape(bsz, nc, T, ng, ne), 2, -1)
    a_cum = jnp.cumsum(dA, axis=-1)
    Bc = b_ssm.astype(jnp.float32).reshape(bsz, nc, T, ng, ns)
    Cc = c_ssm.astype(jnp.float32).reshape(bsz, nc, T, ng, ns)
    Lmat = jnp.exp(segsum(dA))
    CB = jnp.einsum('bclgn,bcsgn->bcgls', Cc, Bc)
    y_diag = jnp.einsum('bcgels,bcsgep->bclgep', Lmat * CB[:, :, :, None], X)
    decay_states = jnp.exp(a_cum[..., -1:] - a_cum)
    states = jnp.einsum('bclgn,bcgel,bclgep->bcgepn', Bc, decay_states, X)
    chunk_decay = jnp.exp(a_cum[..., -1])

    def step(h, inp):
        s_c, d_c = inp
        return h * d_c[..., None, None] + s_c, h

    h0 = jnp.zeros((bsz, ng, ne, hp, ns), jnp.float32)
    _, h_in = lax.scan(step, h0, (jnp.moveaxis(states, 1, 0), jnp.moveaxis(chunk_decay, 1, 0)))
    h_in = jnp.moveaxis(h_in, 0, 1)
    y_off = jnp.einsum('bclgn,bcgepn,bcgel->bclgep', Cc, h_in, jnp.exp(a_cum))
    return (y_diag + y_off).reshape(bsz, seqlen, nh, hp)


def alibi_slopes(n_heads):
    return jnp.exp2(-8.0 * jnp.arange(1, n_heads + 1, dtype=jnp.float32) / n_heads)


def sliding_window_sink_attention(q, k, v, sinks):
    bsz, seqlen, nh, hd = q.shape
    nkv = k.shape[2]
    ng = nh // nkv
    nb = seqlen // ATT_BLOCK
    qb = (q * (hd ** -0.5)).reshape(bsz, nb, ATT_BLOCK, nkv, ng, hd)
    pad = ((0, 0), (ATT_BLOCK, 0), (0, 0), (0, 0))
    kb = jnp.pad(k, pad).reshape(bsz, nb + 1, ATT_BLOCK, nkv, hd)
    vb = jnp.pad(v, pad).reshape(bsz, nb + 1, ATT_BLOCK, nkv, hd)
    k_band = jnp.concatenate([kb[:, :-1], kb[:, 1:]], axis=2)
    v_band = jnp.concatenate([vb[:, :-1], vb[:, 1:]], axis=2)
    scores = jnp.einsum('bnqkgd,bnskd->bnkgqs', qb, k_band).astype(jnp.float32)
    qi = jnp.arange(ATT_BLOCK)
    kj = jnp.arange(2 * ATT_BLOCK)
    rel = qi[:, None] + ATT_BLOCK - kj[None, :]
    key_pos = jnp.arange(nb)[:, None] * ATT_BLOCK - ATT_BLOCK + kj[None, :]
    valid = (rel >= 0)[None] & (rel < WINDOW)[None] & (key_pos >= 0)[:, None, :]
    slopes = alibi_slopes(nh).reshape(nkv, ng)
    bias = -slopes[:, :, None, None] * rel.astype(jnp.float32)
    scores = jnp.where(valid[None, :, None, None], scores + bias, -jnp.inf)
    sink = sinks.astype(jnp.float32).reshape(1, 1, nkv, ng, 1, 1)
    m = jnp.maximum(jnp.max(scores, axis=-1, keepdims=True), sink)
    p = jnp.exp(scores - m)
    p = p / (jnp.sum(p, axis=-1, keepdims=True) + jnp.exp(sink - m))
    out = jnp.einsum('bnkgqs,bnskd->bnqkgd', p.astype(v.dtype), v_band)
    return out.reshape(bsz, seqlen, nh * hd)


def token_mixer(h, w_in, conv_w, conv_b, dt_bias, a_log, d_skip, ssd_norm_w, att_sinks,
                w_ssd_out, w_att_out, w_mix_out):
    bsz, seqlen, _ = h.shape
    proj = h @ w_in
    q, k, v, z, xs, b_ssm, c_ssm, dt_raw, gate_logits = _split(proj, IN_SIZES)
    xbc = jax.nn.silu(causal_depthwise_conv(jnp.concatenate([xs, b_ssm, c_ssm], axis=-1), conv_w, conv_b))
    xs, b_ssm, c_ssm = _split(xbc, (SSD_D_INNER, BC_DIM, BC_DIM))
    xh = xs.reshape(bsz, seqlen, SSD_HEADS, SSD_HEAD_DIM)
    dt = jax.nn.softplus(dt_raw.astype(jnp.float32) + dt_bias)
    a = -jnp.exp(a_log.astype(jnp.float32))
    y = ssd_chunked_scan(xh, dt, a,
                         b_ssm.reshape(bsz, seqlen, SSD_GROUPS, SSD_STATE),
                         c_ssm.reshape(bsz, seqlen, SSD_GROUPS, SSD_STATE))
    y = y + d_skip[:, None] * xh
    y = y.reshape(bsz, seqlen, SSD_D_INNER) * jax.nn.silu(z.astype(jnp.float32))
    y_a = grouped_rms_norm(y, ssd_norm_w).astype(h.dtype) @ w_ssd_out
    att = sliding_window_sink_attention(
        q.reshape(bsz, seqlen, ATT_HEADS, ATT_HEAD_DIM),
        k.reshape(bsz, seqlen, ATT_KV_HEADS, ATT_HEAD_DIM),
        v.reshape(bsz, seqlen, ATT_KV_HEADS, ATT_HEAD_DIM), att_sinks)
    y_b = att @ w_att_out
    g_a, g_b = jnp.split(jax.nn.sigmoid(gate_logits), 2, axis=-1)
    return (g_a * y_a + g_b * y_b) @ w_mix_out


def swiglu_ffn(h, w_gate, w_up, w_down):
    return (jax.nn.silu(h @ w_gate) * (h @ w_up)) @ w_down


def _fwd_setup_inputs(seed: int = 0) -> dict:
    key = jax.random.key(seed)
    ks = jax.random.split(key, 24)
    f32 = jnp.float32

    def nrm(k, shape, scale):
        return jax.random.normal(k, shape, f32) * scale

    x = nrm(ks[0], (BATCH, SEQ, D_MODEL), 1.0)
    ln_in_g = 1.0 + nrm(ks[1], (D_MODEL,), 0.02)
    ln_in_b = nrm(ks[2], (D_MODEL,), 0.02)
    col_scale = jnp.concatenate([
        jnp.ones((Q_DIM + KV_DIM,), f32), jnp.full((KV_DIM,), DEEPNORM_BETA, f32),
        jnp.ones((SSD_D_INNER,), f32), jnp.full((SSD_D_INNER,), DEEPNORM_BETA, f32),
        jnp.ones((2 * BC_DIM + SSD_HEADS + 2 * D_MODEL,), f32)])
    w_in = nrm(ks[3], (DEPTH, D_MODEL, IN_DIM), D_MODEL ** -0.5) * col_scale
    conv_w = nrm(ks[4], (DEPTH, SSD_CONV, CONV_DIM), SSD_CONV ** -0.5)
    conv_b = nrm(ks[5], (DEPTH, CONV_DIM), 0.01)
    dt0 = jnp.exp(jax.random.uniform(ks[6], (DEPTH, SSD_HEADS), f32)
                  * (jnp.log(0.1) - jnp.log(0.001)) + jnp.log(0.001))
    dt_bias = dt0 + jnp.log(-jnp.expm1(-dt0))
    a_log = jnp.log(jax.random.uniform(ks[7], (DEPTH, SSD_HEADS), f32, 1.0, 16.0))
    d_skip = 1.0 + nrm(ks[8], (DEPTH, SSD_HEADS), 0.1)
    ssd_norm_w = 1.0 + nrm(ks[9], (DEPTH, SSD_D_INNER), 0.02)
    att_sinks = nrm(ks[10], (DEPTH, ATT_HEADS), 0.5)
    w_ssd_out = nrm(ks[11], (DEPTH, SSD_D_INNER, D_MODEL), SSD_D_INNER ** -0.5 * DEEPNORM_BETA)
    w_att_out = nrm(ks[12], (DEPTH, Q_DIM, D_MODEL), Q_DIM ** -0.5 * DEEPNORM_BETA)
    w_mix_out = nrm(ks[13], (DEPTH, D_MODEL, D_MODEL), D_MODEL ** -0.5 * DEEPNORM_BETA)
    ln_mix_g = 1.0 + nrm(ks[14], (DEPTH, D_MODEL), 0.02)
    ln_mix_b = nrm(ks[15], (DEPTH, D_MODEL), 0.02)
    w_ffn_gate = nrm(ks[16], (DEPTH, D_MODEL, FFN_HIDDEN), D_MODEL ** -0.5 * DEEPNORM_BETA)
    w_ffn_up = nrm(ks[17], (DEPTH, D_MODEL, FFN_HIDDEN), D_MODEL ** -0.5 * DEEPNORM_BETA)
    w_ffn_down = nrm(ks[18], (DEPTH, FFN_HIDDEN, D_MODEL), FFN_HIDDEN ** -0.5 * DEEPNORM_BETA)
    ln_ffn_g = 1.0 + nrm(ks[19], (DEPTH, D_MODEL), 0.02)
    ln_ffn_b = nrm(ks[20], (DEPTH, D_MODEL), 0.02)
    return {'x': x, 'ln_in_g': ln_in_g, 'ln_in_b': ln_in_b, 'w_in': w_in,
            'conv_w': conv_w, 'conv_b': conv_b, 'dt_bias': dt_bias, 'a_log': a_log,
            'd_skip': d_skip, 'ssd_norm_w': ssd_norm_w, 'att_sinks': att_sinks,
            'w_ssd_out': w_ssd_out, 'w_att_out': w_att_out, 'w_mix_out': w_mix_out,
            'ln_mix_g': ln_mix_g, 'ln_mix_b': ln_mix_b, 'w_ffn_gate': w_ffn_gate,
            'w_ffn_up': w_ffn_up, 'w_ffn_down': w_ffn_down,
            'ln_ffn_g': ln_ffn_g, 'ln_ffn_b': ln_ffn_b}


def _fwd_reference(x, ln_in_g, ln_in_b, w_in, conv_w, conv_b, dt_bias, a_log, d_skip, ssd_norm_w,
              att_sinks, w_ssd_out, w_att_out, w_mix_out, ln_mix_g, ln_mix_b,
              w_ffn_gate, w_ffn_up, w_ffn_down, ln_ffn_g, ln_ffn_b):
    h = layer_norm(x, ln_in_g, ln_in_b)
    for l in range(DEPTH):
        mix = token_mixer(h, w_in[l], conv_w[l], conv_b[l], dt_bias[l], a_log[l], d_skip[l],
                          ssd_norm_w[l], att_sinks[l], w_ssd_out[l], w_att_out[l], w_mix_out[l])
        h = layer_norm(DEEPNORM_ALPHA * h + mix, ln_mix_g[l], ln_mix_b[l])
        ffn = swiglu_ffn(h, w_ffn_gate[l], w_ffn_up[l], w_ffn_down[l])
        h = layer_norm(DEEPNORM_ALPHA * h + ffn, ln_ffn_g[l], ln_ffn_b[l])
    return h


import jax as _jax
import jax.numpy as _jnp

TWIN_FORMAT = 'train_step'
FWD_PARAMS = ['x', 'ln_in_g', 'ln_in_b', 'w_in', 'conv_w', 'conv_b', 'dt_bias', 'a_log', 'd_skip', 'ssd_norm_w', 'att_sinks', 'w_ssd_out', 'w_att_out', 'w_mix_out', 'ln_mix_g', 'ln_mix_b', 'w_ffn_gate', 'w_ffn_up', 'w_ffn_down', 'ln_ffn_g', 'ln_ffn_b']
TWIN_WEIGHTS = ['ln_in_g', 'ln_in_b', 'w_in', 'conv_w', 'conv_b', 'dt_bias', 'a_log', 'd_skip', 'ssd_norm_w', 'att_sinks', 'w_ssd_out', 'w_att_out', 'w_mix_out', 'ln_mix_g', 'ln_mix_b', 'w_ffn_gate', 'w_ffn_up', 'w_ffn_down', 'ln_ffn_g', 'ln_ffn_b']
TWIN_DIFF_INPUT = 'x'
TWIN_INPUTS = ['x', 'ln_in_g', 'ln_in_b', 'w_in', 'conv_w', 'conv_b', 'dt_bias', 'a_log', 'd_skip', 'ssd_norm_w', 'att_sinks', 'w_ssd_out', 'w_att_out', 'w_mix_out', 'ln_mix_g', 'ln_mix_b', 'w_ffn_gate', 'w_ffn_up', 'w_ffn_down', 'ln_ffn_g', 'ln_ffn_b', 'loss_target', 'm_ln_in_g', 'm_ln_in_b', 'm_w_in', 'm_conv_w', 'm_conv_b', 'm_dt_bias', 'm_a_log', 'm_d_skip', 'm_ssd_norm_w', 'm_att_sinks', 'm_w_ssd_out', 'm_w_att_out', 'm_w_mix_out', 'm_ln_mix_g', 'm_ln_mix_b', 'm_w_ffn_gate', 'm_w_ffn_up', 'm_w_ffn_down', 'm_ln_ffn_g', 'm_ln_ffn_b', 'v_ln_in_g', 'v_ln_in_b', 'v_w_in', 'v_conv_w', 'v_conv_b', 'v_dt_bias', 'v_a_log', 'v_d_skip', 'v_ssd_norm_w', 'v_att_sinks', 'v_w_ssd_out', 'v_w_att_out', 'v_w_mix_out', 'v_ln_mix_g', 'v_ln_mix_b', 'v_w_ffn_gate', 'v_w_ffn_up', 'v_w_ffn_down', 'v_ln_ffn_g', 'v_ln_ffn_b']
TWIN_OUTPUTS = ['loss', 'grad_x', 'grad_ln_in_g', 'grad_ln_in_b', 'grad_w_in', 'grad_conv_w', 'grad_conv_b', 'grad_dt_bias', 'grad_a_log', 'grad_d_skip', 'grad_ssd_norm_w', 'grad_att_sinks', 'grad_w_ssd_out', 'grad_w_att_out', 'grad_w_mix_out', 'grad_ln_mix_g', 'grad_ln_mix_b', 'grad_w_ffn_gate', 'grad_w_ffn_up', 'grad_w_ffn_down', 'grad_ln_ffn_g', 'grad_ln_ffn_b', 'delta_ln_in_g', 'delta_ln_in_b', 'delta_w_in', 'delta_conv_w', 'delta_conv_b', 'delta_dt_bias', 'delta_a_log', 'delta_d_skip', 'delta_ssd_norm_w', 'delta_att_sinks', 'delta_w_ssd_out', 'delta_w_att_out', 'delta_w_mix_out', 'delta_ln_mix_g', 'delta_ln_mix_b', 'delta_w_ffn_gate', 'delta_w_ffn_up', 'delta_w_ffn_down', 'delta_ln_ffn_g', 'delta_ln_ffn_b', 'new_m_ln_in_g', 'new_m_ln_in_b', 'new_m_w_in', 'new_m_conv_w', 'new_m_conv_b', 'new_m_dt_bias', 'new_m_a_log', 'new_m_d_skip', 'new_m_ssd_norm_w', 'new_m_att_sinks', 'new_m_w_ssd_out', 'new_m_w_att_out', 'new_m_w_mix_out', 'new_m_ln_mix_g', 'new_m_ln_mix_b', 'new_m_w_ffn_gate', 'new_m_w_ffn_up', 'new_m_w_ffn_down', 'new_m_ln_ffn_g', 'new_m_ln_ffn_b', 'new_v_ln_in_g', 'new_v_ln_in_b', 'new_v_w_in', 'new_v_conv_w', 'new_v_conv_b', 'new_v_dt_bias', 'new_v_a_log', 'new_v_d_skip', 'new_v_ssd_norm_w', 'new_v_att_sinks', 'new_v_w_ssd_out', 'new_v_w_att_out', 'new_v_w_mix_out', 'new_v_ln_mix_g', 'new_v_ln_mix_b', 'new_v_w_ffn_gate', 'new_v_w_ffn_up', 'new_v_w_ffn_down', 'new_v_ln_ffn_g', 'new_v_ln_ffn_b']
TWIN_LEAF_KINDS = {'loss': 'loss', 'grad_x': 'grad_x', 'grad_ln_in_g': 'grad_w', 'grad_ln_in_b': 'grad_w', 'grad_w_in': 'grad_w', 'grad_conv_w': 'grad_w', 'grad_conv_b': 'grad_w', 'grad_dt_bias': 'grad_w', 'grad_a_log': 'grad_w', 'grad_d_skip': 'grad_w', 'grad_ssd_norm_w': 'grad_w', 'grad_att_sinks': 'grad_w', 'grad_w_ssd_out': 'grad_w', 'grad_w_att_out': 'grad_w', 'grad_w_mix_out': 'grad_w', 'grad_ln_mix_g': 'grad_w', 'grad_ln_mix_b': 'grad_w', 'grad_w_ffn_gate': 'grad_w', 'grad_w_ffn_up': 'grad_w', 'grad_w_ffn_down': 'grad_w', 'grad_ln_ffn_g': 'grad_w', 'grad_ln_ffn_b': 'grad_w', 'delta_ln_in_g': 'delta_w', 'delta_ln_in_b': 'delta_w', 'delta_w_in': 'delta_w', 'delta_conv_w': 'delta_w', 'delta_conv_b': 'delta_w', 'delta_dt_bias': 'delta_w', 'delta_a_log': 'delta_w', 'delta_d_skip': 'delta_w', 'delta_ssd_norm_w': 'delta_w', 'delta_att_sinks': 'delta_w', 'delta_w_ssd_out': 'delta_w', 'delta_w_att_out': 'delta_w', 'delta_w_mix_out': 'delta_w', 'delta_ln_mix_g': 'delta_w', 'delta_ln_mix_b': 'delta_w', 'delta_w_ffn_gate': 'delta_w', 'delta_w_ffn_up': 'delta_w', 'delta_w_ffn_down': 'delta_w', 'delta_ln_ffn_g': 'delta_w', 'delta_ln_ffn_b': 'delta_w', 'new_m_ln_in_g': 'new_m', 'new_m_ln_in_b': 'new_m', 'new_m_w_in': 'new_m', 'new_m_conv_w': 'new_m', 'new_m_conv_b': 'new_m', 'new_m_dt_bias': 'new_m', 'new_m_a_log': 'new_m', 'new_m_d_skip': 'new_m', 'new_m_ssd_norm_w': 'new_m', 'new_m_att_sinks': 'new_m', 'new_m_w_ssd_out': 'new_m', 'new_m_w_att_out': 'new_m', 'new_m_w_mix_out': 'new_m', 'new_m_ln_mix_g': 'new_m', 'new_m_ln_mix_b': 'new_m', 'new_m_w_ffn_gate': 'new_m', 'new_m_w_ffn_up': 'new_m', 'new_m_w_ffn_down': 'new_m', 'new_m_ln_ffn_g': 'new_m', 'new_m_ln_ffn_b': 'new_m', 'new_v_ln_in_g': 'new_v', 'new_v_ln_in_b': 'new_v', 'new_v_w_in': 'new_v', 'new_v_conv_w': 'new_v', 'new_v_conv_b': 'new_v', 'new_v_dt_bias': 'new_v', 'new_v_a_log': 'new_v', 'new_v_d_skip': 'new_v', 'new_v_ssd_norm_w': 'new_v', 'new_v_att_sinks': 'new_v', 'new_v_w_ssd_out': 'new_v', 'new_v_w_att_out': 'new_v', 'new_v_w_mix_out': 'new_v', 'new_v_ln_mix_g': 'new_v', 'new_v_ln_mix_b': 'new_v', 'new_v_w_ffn_gate': 'new_v', 'new_v_w_ffn_up': 'new_v', 'new_v_w_ffn_down': 'new_v', 'new_v_ln_ffn_g': 'new_v', 'new_v_ln_ffn_b': 'new_v'}


def _forward(args):
    return _fwd_reference(*[args[k] for k in FWD_PARAMS])


def _output_shape():
    def fwd():
        inp = _fwd_setup_inputs(0)
        return _fwd_reference(*[inp[k] for k in FWD_PARAMS])
    out = _jax.eval_shape(fwd)
    return out.shape, out.dtype

N_MICROBATCH = 1
ADAM_LR = 0.001
ADAM_B1 = 0.9
ADAM_B2 = 0.999
ADAM_EPS = 1e-08
ADAM_WD = 0.01
ADAM_STEP = 10
PER_EXAMPLE_BATCH_AXIS = {'x': 0, 'loss_target': 0}
SHARED_INPUTS = []
_WEIGHT_DTYPES = {'ln_in_g': _jnp.float32, 'ln_in_b': _jnp.float32, 'w_in': _jnp.float32, 'conv_w': _jnp.float32, 'conv_b': _jnp.float32, 'dt_bias': _jnp.float32, 'a_log': _jnp.float32, 'd_skip': _jnp.float32, 'ssd_norm_w': _jnp.float32, 'att_sinks': _jnp.float32, 'w_ssd_out': _jnp.float32, 'w_att_out': _jnp.float32, 'w_mix_out': _jnp.float32, 'ln_mix_g': _jnp.float32, 'ln_mix_b': _jnp.float32, 'w_ffn_gate': _jnp.float32, 'w_ffn_up': _jnp.float32, 'w_ffn_down': _jnp.float32, 'ln_ffn_g': _jnp.float32, 'ln_ffn_b': _jnp.float32}
MOMENT_SCALE = {'ln_in_g': 2.137010e+00, 'ln_in_b': 1.032919e+00, 'w_in': 1.980297e-02, 'conv_w': 1.527500e-02, 'conv_b': 5.715786e-02, 'dt_bias': 5.470289e-02, 'a_log': 5.039520e-02, 'd_skip': 9.607400e-02, 'ssd_norm_w': 1.742331e-02, 'att_sinks': 5.979847e-03, 'w_ssd_out': 5.048295e-02, 'w_att_out': 7.591588e-03, 'w_mix_out': 5.133494e-02, 'ln_mix_g': 2.185426e+00, 'ln_mix_b': 1.058568e+00, 'w_ffn_gate': 1.467145e-02, 'w_ffn_up': 1.444114e-02, 'w_ffn_down': 2.392340e-02, 'ln_ffn_g': 4.541940e+01, 'ln_ffn_b': 2.106434e+00}


def _to_microbatches(a, axis):
    t = _jnp.moveaxis(a, axis, 0)
    t = t.reshape((N_MICROBATCH, t.shape[0] // N_MICROBATCH) + t.shape[1:])
    return _jnp.moveaxis(t, 1, axis + 1)


def setup_inputs(seed: int = 0) -> dict:
    inp = _fwd_setup_inputs(seed)
    key = _jax.random.fold_in(_jax.random.key(seed), 7919)
    shape, _ = _output_shape()
    out = dict(inp)
    out["loss_target"] = _jax.random.normal(_jax.random.fold_in(key, 0), shape, _jnp.float32)
    for i, name in enumerate(TWIN_WEIGHTS):
        w = inp[name].astype(_jnp.float32)
        if MOMENT_SCALE is None:
            s = _jnp.sqrt(_jnp.mean(_jnp.square(w)) + 1e-30)
        else:
            s = MOMENT_SCALE[name]
        km, kv = _jax.random.split(_jax.random.fold_in(key, i + 1))
        out[name] = w
        out["m_" + name] = s * _jax.random.normal(km, w.shape, _jnp.float32)
        out["v_" + name] = (s * s) * _jax.random.uniform(kv, w.shape, _jnp.float32, 0.5, 1.5)
    if N_MICROBATCH > 1:
        for name, axis in PER_EXAMPLE_BATCH_AXIS.items():
            out[name] = _to_microbatches(out[name], axis)
    return {'x': out['x'], 'ln_in_g': out['ln_in_g'], 'ln_in_b': out['ln_in_b'], 'w_in': out['w_in'], 'conv_w': out['conv_w'], 'conv_b': out['conv_b'], 'dt_bias': out['dt_bias'], 'a_log': out['a_log'], 'd_skip': out['d_skip'], 'ssd_norm_w': out['ssd_norm_w'], 'att_sinks': out['att_sinks'], 'w_ssd_out': out['w_ssd_out'], 'w_att_out': out['w_att_out'], 'w_mix_out': out['w_mix_out'], 'ln_mix_g': out['ln_mix_g'], 'ln_mix_b': out['ln_mix_b'], 'w_ffn_gate': out['w_ffn_gate'], 'w_ffn_up': out['w_ffn_up'], 'w_ffn_down': out['w_ffn_down'], 'ln_ffn_g': out['ln_ffn_g'], 'ln_ffn_b': out['ln_ffn_b'], 'loss_target': out['loss_target'], 'm_ln_in_g': out['m_ln_in_g'], 'm_ln_in_b': out['m_ln_in_b'], 'm_w_in': out['m_w_in'], 'm_conv_w': out['m_conv_w'], 'm_conv_b': out['m_conv_b'], 'm_dt_bias': out['m_dt_bias'], 'm_a_log': out['m_a_log'], 'm_d_skip': out['m_d_skip'], 'm_ssd_norm_w': out['m_ssd_norm_w'], 'm_att_sinks': out['m_att_sinks'], 'm_w_ssd_out': out['m_w_ssd_out'], 'm_w_att_out': out['m_w_att_out'], 'm_w_mix_out': out['m_w_mix_out'], 'm_ln_mix_g': out['m_ln_mix_g'], 'm_ln_mix_b': out['m_ln_mix_b'], 'm_w_ffn_gate': out['m_w_ffn_gate'], 'm_w_ffn_up': out['m_w_ffn_up'], 'm_w_ffn_down': out['m_w_ffn_down'], 'm_ln_ffn_g': out['m_ln_ffn_g'], 'm_ln_ffn_b': out['m_ln_ffn_b'], 'v_ln_in_g': out['v_ln_in_g'], 'v_ln_in_b': out['v_ln_in_b'], 'v_w_in': out['v_w_in'], 'v_conv_w': out['v_conv_w'], 'v_conv_b': out['v_conv_b'], 'v_dt_bias': out['v_dt_bias'], 'v_a_log': out['v_a_log'], 'v_d_skip': out['v_d_skip'], 'v_ssd_norm_w': out['v_ssd_norm_w'], 'v_att_sinks': out['v_att_sinks'], 'v_w_ssd_out': out['v_w_ssd_out'], 'v_w_att_out': out['v_w_att_out'], 'v_w_mix_out': out['v_w_mix_out'], 'v_ln_mix_g': out['v_ln_mix_g'], 'v_ln_mix_b': out['v_ln_mix_b'], 'v_w_ffn_gate': out['v_w_ffn_gate'], 'v_w_ffn_up': out['v_w_ffn_up'], 'v_w_ffn_down': out['v_w_ffn_down'], 'v_ln_ffn_g': out['v_ln_ffn_g'], 'v_ln_ffn_b': out['v_ln_ffn_b']}


def _loss(weights, diff, rest, loss_target):
    with _jax.named_scope("forward"):
        args = {**rest, TWIN_DIFF_INPUT: diff, **{k: w.astype(_WEIGHT_DTYPES[k]) for k, w in weights.items()}}
        y = _forward(args)
    with _jax.named_scope("loss_head"):
        err = _jnp.square(y.astype(_jnp.float32) - loss_target)
        return 0.5 * _jnp.sum(_jnp.mean(err, axis=-1)) if err.ndim else 0.5 * err


def _adamw(w, g, m, v):
    m = ADAM_B1 * m + (1.0 - ADAM_B1) * g
    v = ADAM_B2 * v + (1.0 - ADAM_B2) * _jnp.square(g)
    m_hat = m / (1.0 - ADAM_B1 ** ADAM_STEP)
    v_hat = v / (1.0 - ADAM_B2 ** ADAM_STEP)
    delta = -ADAM_LR * (m_hat / (_jnp.sqrt(v_hat) + ADAM_EPS) + ADAM_WD * w)
    return delta, m, v


def reference(x, ln_in_g, ln_in_b, w_in, conv_w, conv_b, dt_bias, a_log, d_skip, ssd_norm_w, att_sinks, w_ssd_out, w_att_out, w_mix_out, ln_mix_g, ln_mix_b, w_ffn_gate, w_ffn_up, w_ffn_down, ln_ffn_g, ln_ffn_b, loss_target, m_ln_in_g, m_ln_in_b, m_w_in, m_conv_w, m_conv_b, m_dt_bias, m_a_log, m_d_skip, m_ssd_norm_w, m_att_sinks, m_w_ssd_out, m_w_att_out, m_w_mix_out, m_ln_mix_g, m_ln_mix_b, m_w_ffn_gate, m_w_ffn_up, m_w_ffn_down, m_ln_ffn_g, m_ln_ffn_b, v_ln_in_g, v_ln_in_b, v_w_in, v_conv_w, v_conv_b, v_dt_bias, v_a_log, v_d_skip, v_ssd_norm_w, v_att_sinks, v_w_ssd_out, v_w_att_out, v_w_mix_out, v_ln_mix_g, v_ln_mix_b, v_w_ffn_gate, v_w_ffn_up, v_w_ffn_down, v_ln_ffn_g, v_ln_ffn_b):
    given = dict(x=x, ln_in_g=ln_in_g, ln_in_b=ln_in_b, w_in=w_in, conv_w=conv_w, conv_b=conv_b, dt_bias=dt_bias, a_log=a_log, d_skip=d_skip, ssd_norm_w=ssd_norm_w, att_sinks=att_sinks, w_ssd_out=w_ssd_out, w_att_out=w_att_out, w_mix_out=w_mix_out, ln_mix_g=ln_mix_g, ln_mix_b=ln_mix_b, w_ffn_gate=w_ffn_gate, w_ffn_up=w_ffn_up, w_ffn_down=w_ffn_down, ln_ffn_g=ln_ffn_g, ln_ffn_b=ln_ffn_b, loss_target=loss_target, m_ln_in_g=m_ln_in_g, m_ln_in_b=m_ln_in_b, m_w_in=m_w_in, m_conv_w=m_conv_w, m_conv_b=m_conv_b, m_dt_bias=m_dt_bias, m_a_log=m_a_log, m_d_skip=m_d_skip, m_ssd_norm_w=m_ssd_norm_w, m_att_sinks=m_att_sinks, m_w_ssd_out=m_w_ssd_out, m_w_att_out=m_w_att_out, m_w_mix_out=m_w_mix_out, m_ln_mix_g=m_ln_mix_g, m_ln_mix_b=m_ln_mix_b, m_w_ffn_gate=m_w_ffn_gate, m_w_ffn_up=m_w_ffn_up, m_w_ffn_down=m_w_ffn_down, m_ln_ffn_g=m_ln_ffn_g, m_ln_ffn_b=m_ln_ffn_b, v_ln_in_g=v_ln_in_g, v_ln_in_b=v_ln_in_b, v_w_in=v_w_in, v_conv_w=v_conv_w, v_conv_b=v_conv_b, v_dt_bias=v_dt_bias, v_a_log=v_a_log, v_d_skip=v_d_skip, v_ssd_norm_w=v_ssd_norm_w, v_att_sinks=v_att_sinks, v_w_ssd_out=v_w_ssd_out, v_w_att_out=v_w_att_out, v_w_mix_out=v_w_mix_out, v_ln_mix_g=v_ln_mix_g, v_ln_mix_b=v_ln_mix_b, v_w_ffn_gate=v_w_ffn_gate, v_w_ffn_up=v_w_ffn_up, v_w_ffn_down=v_w_ffn_down, v_ln_ffn_g=v_ln_ffn_g, v_ln_ffn_b=v_ln_ffn_b)
    weights = {n: given[n] for n in TWIN_WEIGHTS}
    shared = {n: given[n] for n in SHARED_INPUTS}
    per_example = {n: given[n] for n in ['x']}
    grad_fn = _jax.value_and_grad(_loss, argnums=(0, 1))

    def one_microbatch(ex, loss_target):
        ex = dict(ex)
        diff = ex.pop(TWIN_DIFF_INPUT)
        return grad_fn(weights, diff, {**shared, **ex}, loss_target)

    if N_MICROBATCH == 1:
        loss, (grad_w, grad_x) = one_microbatch(per_example, given["loss_target"])
    else:
        def body(carry, xs):
            loss_sum, grad_sum = carry
            l_k, (gw_k, gx_k) = one_microbatch(xs[0], xs[1])
            with _jax.named_scope("update"):
                return (loss_sum + l_k, _jax.tree.map(_jnp.add, grad_sum, gw_k)), gx_k

        init = (_jnp.zeros((), _jnp.float32), _jax.tree.map(_jnp.zeros_like, weights))
        (loss, grad_w), grad_x = _jax.lax.scan(body, init, (per_example, given["loss_target"]))
    with _jax.named_scope("update"):
        delta_w, new_m, new_v = {}, {}, {}
        for n in TWIN_WEIGHTS:
            delta_w[n], new_m[n], new_v[n] = _adamw(weights[n], grad_w[n], given["m_" + n], given["v_" + n])
    return (loss, grad_x, *[grad_w[n] for n in TWIN_WEIGHTS], *[delta_w[n] for n in TWIN_WEIGHTS],
            *[new_m[n] for n in TWIN_WEIGHTS], *[new_v[n] for n in TWIN_WEIGHTS])
```

```python
import functools
import math

import jax
import jax.numpy as jnp
from jax import lax
from jax.experimental import pallas as pl
from jax.experimental.pallas import tpu as pltpu

F32 = jnp.float32
BF = jnp.bfloat16

D_MODEL = 1024
DEPTH = 2
ATT_HEADS = 16
ATT_HEAD_DIM = 64
BLK = 128
SSD_D_INNER = 2048
SSD_HEADS = 32
SSD_GROUPS = 4
SSD_STATE = 128
SSD_CONV = 4
BC_DIM = 512
CONV_DIM = 3072
FFN_HIDDEN = 2816
IN_DIM = 8480
LN_EPS = 1e-5
RMS_EPS = 1e-5
ALPHA = (2 * DEPTH) ** 0.25
ADAM_LR = 0.001
ADAM_B1 = 0.9
ADAM_B2 = 0.999
ADAM_EPS = 1e-08
ADAM_WD = 0.01
ADAM_STEP = 10

_PACK = (("q", 0, 1024), ("z", 1280, 2048), ("gates", 6432, 2048), ("xs", 3328, 2048), ("B", 5376, 512),
         ("C", 5888, 512), ("k", 1024, 128), ("v", 1152, 128), ("dt", 6400, 32))
NP = 8704
_OFF = {}
_o = 0
for _n, _, _w in _PACK:
    _OFF[_n] = _o
    _o += _w
_PAD = NP - _o

VMEM_LIMIT_BYTES = 56 * 1024 * 1024
BIG = ("w_in", "w_ssd_out", "w_att_out", "w_mix_out", "w_ffn_gate", "w_ffn_up", "w_ffn_down")
WEIGHTS = ("ln_in_g", "ln_in_b", "w_in", "conv_w", "conv_b", "dt_bias", "a_log", "d_skip", "ssd_norm_w", "att_sinks",
           "w_ssd_out", "w_att_out", "w_mix_out", "ln_mix_g", "ln_mix_b", "w_ffn_gate", "w_ffn_up", "w_ffn_down",
           "ln_ffn_g", "ln_ffn_b")
SMALL = tuple(n for n in WEIGHTS if n not in BIG)


def _params(n_grid):
    return pltpu.CompilerParams(dimension_semantics=("arbitrary",) * n_grid, vmem_limit_bytes=VMEM_LIMIT_BYTES)


def _dot(a, b, ca, cb):
    return lax.dot_general(a.astype(BF), b.astype(BF), (((ca,), (cb,)), ((), ())), preferred_element_type=F32)


@jax.custom_vjp
def _nn(a, b):
    return _dot(a, b, 1, 0)


def _nn_f(a, b):
    return _dot(a, b, 1, 0), (a, b)


def _nn_b(res, g):
    a, b = res
    return _dot(g, b, 1, 1).astype(a.dtype), _dot(a, g, 0, 0).astype(b.dtype)


_nn.defvjp(_nn_f, _nn_b)


@jax.custom_vjp
def _nt(a, b):
    return _dot(a, b, 1, 1)


def _nt_f(a, b):
    return _dot(a, b, 1, 1), (a, b)


def _nt_b(res, g):
    a, b = res
    return _dot(g, b, 1, 0).astype(a.dtype), _dot(g, a, 0, 0).astype(b.dtype)


_nt.defvjp(_nt_f, _nt_b)


@jax.custom_vjp
def _tn(a, b):
    return _dot(a, b, 0, 0)


def _tn_f(a, b):
    return _dot(a, b, 0, 0), (a, b)


def _tn_b(res, g):
    a, b = res
    return _dot(b, g, 1, 1).astype(a.dtype), _dot(a, g, 1, 0).astype(b.dtype)


_tn.defvjp(_tn_f, _tn_b)


def _silu(x):
    return x * jax.nn.sigmoid(x)


def _layer_norm(s, g, b):
    mu = jnp.mean(s, axis=-1, keepdims=True)
    sc = s - mu
    var = jnp.mean(sc * sc, axis=-1, keepdims=True)
    return sc * lax.rsqrt(var + LN_EPS) * g + b


def _ssd_group(pre_x, pre_b, pre_c, z, dtr, dtb, alog, dsk, nw, state):
    t = pre_x.shape[0]
    xs, bm, cm = _silu(pre_x), _silu(pre_b), _silu(pre_c)
    dt = jax.nn.softplus(dtr + dtb)
    da = dt * (-jnp.exp(alog))
    row = lax.broadcasted_iota(jnp.int32, (t, t), 0)
    col = lax.broadcasted_iota(jnp.int32, (t, t), 1)
    upper = (row <= col).astype(F32)
    cum = jnp.dot(da, upper, precision=lax.Precision.HIGHEST, preferred_element_type=F32)
    tot = jnp.sum(da, axis=1, keepdims=True)
    cb = _nt(cm, bm)
    tril = row >= col
    first = lax.broadcasted_iota(jnp.int32, (t, BLK), 1) < 64
    first_row = lax.broadcasted_iota(jnp.int32, (1, BLK), 1) < 64

    def col_form(r):
        return jnp.broadcast_to(r, (t, t)).T

    ys, new_state = [], []
    for p in range(4):
        h0, h1 = 2 * p, 2 * p + 1
        sl = slice(BLK * p, BLK * (p + 1))
        x_pair = xs[:, sl] * jnp.where(first, col_form(dt[h0:h0 + 1]), col_form(dt[h1:h1 + 1]))
        y_pair = None
        cc = []
        for h, keep in ((h0, first), (h1, jnp.logical_not(first))):
            cr = jnp.broadcast_to(cum[h:h + 1], (t, t))
            cc.append(cr.T)
            decay = jnp.exp(jnp.where(tril, cc[-1] - cr, -1e30))
            y_h = _nn(decay * cb, jnp.where(keep, x_pair, 0.0))
            y_pair = y_h if y_pair is None else y_pair + y_h
        s_pair = state[:, sl]
        y_pair = y_pair + _nn(cm, s_pair) * jnp.where(first, jnp.exp(cc[0]), jnp.exp(cc[1]))
        to_end = jnp.where(first, jnp.exp(tot[h0:h0 + 1] - cc[0]), jnp.exp(tot[h1:h1 + 1] - cc[1]))
        chunk_decay = jnp.where(first_row, jnp.exp(tot[h0:h0 + 1]), jnp.exp(tot[h1:h1 + 1]))
        new_state.append(s_pair * chunk_decay + _tn(bm, x_pair * to_end))
        ys.append(y_pair + dsk[:, sl] * xs[:, sl])
    y = jnp.concatenate(ys, axis=1) * _silu(z)
    y = y * lax.rsqrt(jnp.mean(y * y, axis=-1, keepdims=True) + RMS_EPS) * nw
    return y, jnp.concatenate(new_state, axis=1)


def _attn_block(q, kp, kc, vp, vc, sinks, n, kv):
    t = q.shape[0]
    kb = jnp.concatenate([kp, kc], axis=0)
    vb = jnp.concatenate([vp, vc], axis=0)
    qi = lax.broadcasted_iota(jnp.int32, (t, 2 * t), 0)
    kj = lax.broadcasted_iota(jnp.int32, (t, 2 * t), 1)
    rel = qi + t - kj
    valid = (rel >= 0) & (rel < t) & ((n * t - t + kj) >= 0)
    relf = rel.astype(F32)
    first = lax.broadcasted_iota(jnp.int32, (t, BLK), 1) < 64
    lane16 = lax.broadcasted_iota(jnp.int32, (1, ATT_HEADS), 1)
    outs = []
    for p in range(4):
        qp = q[:, BLK * p:BLK * (p + 1)] * (ATT_HEAD_DIM ** -0.5)
        o_pair = None
        for half, keep in enumerate((first, jnp.logical_not(first))):
            h = kv * 8 + 2 * p + half
            s = _nt(jnp.where(keep, qp, 0.0), kb)
            slope = jnp.exp((-8.0 * math.log(2.0) / ATT_HEADS) * (h + 1).astype(F32))
            s = jnp.where(valid, s - slope * relf, -1e30)
            sink = jnp.sum(jnp.where(lane16 == h, sinks, 0.0), axis=1, keepdims=True)
            m = jnp.maximum(jnp.max(s, axis=1, keepdims=True), sink)
            e = jnp.exp(s - m)
            den = jnp.sum(e, axis=1, keepdims=True) + jnp.exp(sink - m)
            o = _nn(e / den, vb)
            o_pair = o if o_pair is None else jnp.where(first, o_pair, o)
        outs.append(o_pair)
    return jnp.concatenate(outs, axis=1)


def _mm(a, b, mode, out_dtype, tm, tn, tk, name):
    if mode == "nn":
        (m, k), (k2, n) = a.shape, b.shape
    elif mode == "nt":
        (m, k), (n, k2) = a.shape, b.shape
    else:
        (k, m), (k2, n) = a.shape, b.shape
    assert k == k2, (a.shape, b.shape, mode)
    tm, tn, tk = min(tm, m), min(tn, n), min(tk, k)
    assert m % tm == 0 and n % tn == 0 and k % tk == 0, (m, n, k, tm, tn, tk)
    nk = k // tk
    ca, cb = {"nn": (1, 0), "nt": (1, 1), "tn": (0, 0)}[mode]

    def body(a_ref, b_ref, o_ref, *acc):
        part = _dot(a_ref[...], b_ref[...], ca, cb)
        if nk == 1:
            o_ref[...] = part.astype(o_ref.dtype)
            return
        acc_ref, = acc
        kk = pl.program_id(2)

        @pl.when(kk == 0)
        def _():
            acc_ref[...] = part

        @pl.when(kk > 0)
        def _():
            acc_ref[...] += part

        @pl.when(kk == nk - 1)
        def _():
            o_ref[...] = acc_ref[...].astype(o_ref.dtype)

    a_spec = pl.BlockSpec((tk, tm), lambda i, j, kk: (kk, i)) if mode == "tn" else pl.BlockSpec((tm, tk), lambda i, j, kk: (i, kk))
    b_spec = pl.BlockSpec((tn, tk), lambda i, j, kk: (j, kk)) if mode == "nt" else pl.BlockSpec((tk, tn), lambda i, j, kk: (kk, j))
    return pl.pallas_call(
        body, name=name, grid=(m // tm, n // tn, nk),
        in_specs=[a_spec, b_spec], out_specs=pl.BlockSpec((tm, tn), lambda i, j, kk: (i, j)),
        out_shape=jax.ShapeDtypeStruct((m, n), out_dtype),
        scratch_shapes=[] if nk == 1 else [pltpu.VMEM((tm, tn), F32)],
        compiler_params=_params(3),
    )(a, b)


def _rows(fn, rows, params, outs, accs, tile, name):
    length = rows[0][0].shape[0]
    tile = min(tile, length)
    assert length % tile == 0
    nr, npar, no = len(rows), len(params), len(outs)

    def body(*refs):
        vals = [r[...] for r in refs[:nr + npar]]
        o, a = fn(*vals)
        for ref, val in zip(refs[nr + npar:nr + npar + no], o):
            ref[...] = val.astype(ref.dtype)
        i = pl.program_id(0)
        for ref, val in zip(refs[nr + npar + no:], a):
            @pl.when(i == 0)
            def _(ref=ref, val=val):
                ref[...] = val

            @pl.when(i > 0)
            def _(ref=ref, val=val):
                ref[...] += val

    in_specs = [pl.BlockSpec((tile, w), functools.partial(lambda i, cb: (i, cb), cb=cb)) for _, cb, w in rows]
    in_specs += [pl.BlockSpec(p.shape, lambda i: (0, 0)) for p in params]
    out_specs = [pl.BlockSpec((tile, w), lambda i: (i, 0)) for w, _ in outs]
    out_specs += [pl.BlockSpec((r, w), lambda i: (0, 0)) for r, w in accs]
    out_shape = [jax.ShapeDtypeStruct((length, w), dt) for w, dt in outs]
    out_shape += [jax.ShapeDtypeStruct((r, w), F32) for r, w in accs]
    res = pl.pallas_call(
        body, name=name, grid=(length // tile,), in_specs=in_specs, out_specs=out_specs, out_shape=out_shape,
        compiler_params=_params(1),
    )(*[r[0] for r in rows], *params)
    return res[:no], res[no:]


def _conv_fwd(proj, conv_w, conv_b, name):
    length = proj.shape[0]
    tl, cw = min(512, length), 512
    cb0 = _OFF["xs"] // cw

    def body(u_ref, halo_ref, w_ref, b_ref, o_ref, win):
        i = pl.program_id(1)
        win[0:8, :] = jnp.where(i > 0, halo_ref[...], 0.0)
        win[8:8 + tl, :] = u_ref[...]
        acc = b_ref[...] + w_ref[0:1, :] * win[5:5 + tl, :]
        for kk in range(1, SSD_CONV):
            acc = acc + w_ref[kk:kk + 1, :] * win[5 + kk:5 + kk + tl, :]
        o_ref[...] = acc

    return pl.pallas_call(
        body, name=name, grid=(CONV_DIM // cw, length // tl),
        in_specs=[pl.BlockSpec((tl, cw), lambda j, i: (i, cb0 + j)),
                  pl.BlockSpec((8, cw), lambda j, i: (jnp.maximum(i * (tl // 8) - 1, 0), cb0 + j)),
                  pl.BlockSpec((SSD_CONV, cw), lambda j, i: (0, j)),
                  pl.BlockSpec((1, cw), lambda j, i: (0, j))],
        out_specs=pl.BlockSpec((tl, cw), lambda j, i: (i, j)),
        out_shape=jax.ShapeDtypeStruct((length, CONV_DIM), F32),
        scratch_shapes=[pltpu.VMEM((8 + tl, cw), F32)],
        compiler_params=_params(2),
    )(proj, proj, conv_w, conv_b)


def _conv_bwd(dpre, proj, conv_w, name):
    length = proj.shape[0]
    tl, cw = min(512, length), 512
    cb0 = _OFF["xs"] // cw
    n_t = length // tl

    def body(d_ref, dnext_ref, u_ref, halo_ref, w_ref, du_ref, dw_ref, db_ref, dwin, uwin):
        i = pl.program_id(1)
        d = d_ref[...]
        dwin[0:tl, :] = d
        dwin[tl:tl + 8, :] = jnp.where(i < n_t - 1, dnext_ref[...], 0.0)
        uwin[0:8, :] = jnp.where(i > 0, halo_ref[...], 0.0)
        uwin[8:8 + tl, :] = u_ref[...]
        du = w_ref[3:4, :] * d
        for kk in range(SSD_CONV - 1):
            du = du + w_ref[kk:kk + 1, :] * dwin[3 - kk:3 - kk + tl, :]
        du_ref[...] = du.astype(du_ref.dtype)
        dws = [jnp.sum(d * uwin[5 + kk:5 + kk + tl, :], axis=0, keepdims=True) for kk in range(SSD_CONV)]
        dw = jnp.concatenate(dws, axis=0)
        db = jnp.sum(d, axis=0, keepdims=True)

        @pl.when(i == 0)
        def _():
            dw_ref[...] = dw
            db_ref[...] = db

        @pl.when(i > 0)
        def _():
            dw_ref[...] += dw
            db_ref[...] += db

    return pl.pallas_call(
        body, name=name, grid=(CONV_DIM // cw, n_t),
        in_specs=[pl.BlockSpec((tl, cw), lambda j, i: (i, j)),
                  pl.BlockSpec((8, cw), lambda j, i: (jnp.minimum((i + 1) * (tl // 8), length // 8 - 1), j)),
                  pl.BlockSpec((tl, cw), lambda j, i: (i, cb0 + j)),
                  pl.BlockSpec((8, cw), lambda j, i: (jnp.maximum(i * (tl // 8) - 1, 0), cb0 + j)),
                  pl.BlockSpec((SSD_CONV, cw), lambda j, i: (0, j))],
        out_specs=[pl.BlockSpec((tl, cw), lambda j, i: (i, j)),
                   pl.BlockSpec((SSD_CONV, cw), lambda j, i: (0, j)),
                   pl.BlockSpec((1, cw), lambda j, i: (0, j))],
        out_shape=[jax.ShapeDtypeStruct((length, CONV_DIM), BF),
                   jax.ShapeDtypeStruct((SSD_CONV, CONV_DIM), F32),
                   jax.ShapeDtypeStruct((1, CONV_DIM), F32)],
        scratch_shapes=[pltpu.VMEM((tl + 8, cw), F32), pltpu.VMEM((8 + tl, cw), F32)],
        compiler_params=_params(2),
    )(dpre, dpre, proj, proj, conv_w)


def _ssd_in_specs(rev, nc):
    def cidx(c):
        return nc - 1 - c if rev else c
    zb = _OFF["z"] // 512
    return [pl.BlockSpec((BLK, 512), lambda g, c: (cidx(c), g)),
            pl.BlockSpec((BLK, BLK), lambda g, c: (cidx(c), 16 + g)),
            pl.BlockSpec((BLK, BLK), lambda g, c: (cidx(c), 20 + g)),
            pl.BlockSpec((BLK, 512), lambda g, c: (cidx(c), zb + g)),
            pl.BlockSpec((8, BLK), lambda g, c: (g, cidx(c))),
            pl.BlockSpec((8, 1), lambda g, c: (g, 0)),
            pl.BlockSpec((8, 1), lambda g, c: (g, 0)),
            pl.BlockSpec((1, 512), lambda g, c: (0, g)),
            pl.BlockSpec((1, 512), lambda g, c: (0, g))]


def _ssd_fwd(pre, proj, dt_t, dtb, alog, dsk, nw, name):
    length = pre.shape[0]
    nc = length // BLK

    def body(px, pb, pc, z, dtr, dtb_r, al_r, dsk_r, nw_r, y_ref, sin_ref, st):
        @pl.when(pl.program_id(1) == 0)
        def _():
            st[...] = jnp.zeros_like(st)

        s_in = st[...]
        sin_ref[...] = s_in
        y, s_out = _ssd_group(px[...], pb[...], pc[...], z[...], dtr[...], dtb_r[...], al_r[...], dsk_r[...], nw_r[...], s_in)
        y_ref[...] = y.astype(y_ref.dtype)
        st[...] = s_out

    return pl.pallas_call(
        body, name=name, grid=(SSD_GROUPS, nc), in_specs=_ssd_in_specs(False, nc),
        out_specs=[pl.BlockSpec((BLK, 512), lambda g, c: (c, g)),
                   pl.BlockSpec((None, None, SSD_STATE, 512), lambda g, c: (g, c, 0, 0))],
        out_shape=[jax.ShapeDtypeStruct((length, SSD_D_INNER), BF),
                   jax.ShapeDtypeStruct((SSD_GROUPS, nc, SSD_STATE, 512), F32)],
        scratch_shapes=[pltpu.VMEM((SSD_STATE, 512), F32)],
        compiler_params=_params(2),
    )(pre, pre, pre, proj, dt_t, dtb, alog, dsk, nw)


def _ssd_bwd(pre, proj, dt_t, dtb, alog, dsk, nw, s_in, dy, name):
    length = pre.shape[0]
    nc = length // BLK

    def body(px, pb, pc, z, dtr, dtb_r, al_r, dsk_r, nw_r, sin_r, dy_r,
             dpx, dpb, dpc, dz, ddt, ddtb, dal, ddsk, dnw, dst):
        c = pl.program_id(1)

        @pl.when(c == 0)
        def _():
            dst[...] = jnp.zeros_like(dst)

        _, vjp = jax.vjp(_ssd_group, px[...], pb[...], pc[...], z[...], dtr[...], dtb_r[...], al_r[...], dsk_r[...],
                         nw_r[...], sin_r[...])
        g = vjp((dy_r[...], dst[...]))
        dpx[...] = g[0]
        dpb[...] = g[1]
        dpc[...] = g[2]
        dz[...] = g[3].astype(dz.dtype)
        ddt[...] = g[4]
        dst[...] = g[9]
        for ref, val in ((ddtb, g[5]), (dal, g[6]), (ddsk, g[7]), (dnw, g[8])):
            @pl.when(c == 0)
            def _(ref=ref, val=val):
                ref[...] = val

            @pl.when(c > 0)
            def _(ref=ref, val=val):
                ref[...] += val

    in_specs = _ssd_in_specs(True, nc) + [
        pl.BlockSpec((None, None, SSD_STATE, 512), lambda g, c: (g, nc - 1 - c, 0, 0)),
        pl.BlockSpec((BLK, 512), lambda g, c: (nc - 1 - c, g))]
    return pl.pallas_call(
        body, name=name, grid=(SSD_GROUPS, nc), in_specs=in_specs,
        out_specs=[pl.BlockSpec((BLK, 512), lambda g, c: (nc - 1 - c, g)),
                   pl.BlockSpec((BLK, BLK), lambda g, c: (nc - 1 - c, g)),
                   pl.BlockSpec((BLK, BLK), lambda g, c: (nc - 1 - c, g)),
                   pl.BlockSpec((BLK, 512), lambda g, c: (nc - 1 - c, g)),
                   pl.BlockSpec((8, BLK), lambda g, c: (g, nc - 1 - c)),
                   pl.BlockSpec((8, 1), lambda g, c: (g, 0)),
                   pl.BlockSpec((8, 1), lambda g, c: (g, 0)),
                   pl.BlockSpec((1, 512), lambda g, c: (0, g)),
                   pl.BlockSpec((1, 512), lambda g, c: (0, g))],
        out_shape=[jax.ShapeDtypeStruct((length, SSD_D_INNER), F32),
                   jax.ShapeDtypeStruct((length, BC_DIM), F32),
                   jax.ShapeDtypeStruct((length, BC_DIM), F32),
                   jax.ShapeDtypeStruct((length, SSD_D_INNER), BF),
                   jax.ShapeDtypeStruct((SSD_HEADS, length), F32),
                   jax.ShapeDtypeStruct((SSD_HEADS, 1), F32),
                   jax.ShapeDtypeStruct((SSD_HEADS, 1), F32),
                   jax.ShapeDtypeStruct((1, SSD_D_INNER), F32),
                   jax.ShapeDtypeStruct((1, SSD_D_INNER), F32)],
        scratch_shapes=[pltpu.VMEM((SSD_STATE, 512), F32)],
        compiler_params=_params(2),
    )(pre, pre, pre, proj, dt_t, dtb, alog, dsk, nw, s_in, dy)


def _attn_in_specs():
    return [pl.BlockSpec((BLK, 512), lambda kv, n: (n, kv)),
            pl.BlockSpec((None, BLK, BLK), lambda kv, n: (kv, jnp.maximum(n - 1, 0), 0)),
            pl.BlockSpec((None, BLK, BLK), lambda kv, n: (kv, n, 0)),
            pl.BlockSpec((None, BLK, BLK), lambda kv, n: (kv, jnp.maximum(n - 1, 0), 0)),
            pl.BlockSpec((None, BLK, BLK), lambda kv, n: (kv, n, 0)),
            pl.BlockSpec((1, ATT_HEADS), lambda kv, n: (0, 0))]


def _attn_fwd(proj, k2, v2, sinks, name):
    length = proj.shape[0]

    def body(q, kp, kc, vp, vc, sk, o_ref):
        o = _attn_block(q[...], kp[...], kc[...], vp[...], vc[...], sk[...], pl.program_id(1), pl.program_id(0))
        o_ref[...] = o.astype(o_ref.dtype)

    return pl.pallas_call(
        body, name=name, grid=(2, length // BLK), in_specs=_attn_in_specs(),
        out_specs=pl.BlockSpec((BLK, 512), lambda kv, n: (n, kv)),
        out_shape=jax.ShapeDtypeStruct((length, D_MODEL), BF),
        compiler_params=_params(2),
    )(proj, k2, k2, v2, v2, sinks)


def _attn_bwd(proj, k2, v2, sinks, datt, name):
    length = proj.shape[0]

    def body(q, kp, kc, vp, vc, sk, do, dq, dkp, dkc, dvp, dvc, dsk):
        n, kv = pl.program_id(1), pl.program_id(0)
        _, vjp = jax.vjp(lambda *a: _attn_block(*a, n, kv), q[...], kp[...], kc[...], vp[...], vc[...], sk[...])
        g = vjp(do[...])
        dq[...] = g[0].astype(dq.dtype)
        dkp[...] = g[1]
        dkc[...] = g[2]
        dvp[...] = g[3]
        dvc[...] = g[4]
        is_first = (n == 0) & (kv == 0)

        @pl.when(is_first)
        def _():
            dsk[...] = g[5]

        @pl.when(jnp.logical_not(is_first))
        def _():
            dsk[...] += g[5]

    blk3 = pl.BlockSpec((None, BLK, BLK), lambda kv, n: (kv, n, 0))
    kv_shape = jax.ShapeDtypeStruct((2, length, BLK), F32)
    return pl.pallas_call(
        body, name=name, grid=(2, length // BLK),
        in_specs=_attn_in_specs() + [pl.BlockSpec((BLK, 512), lambda kv, n: (n, kv))],
        out_specs=[pl.BlockSpec((BLK, 512), lambda kv, n: (n, kv)), blk3, blk3, blk3, blk3,
                   pl.BlockSpec((1, ATT_HEADS), lambda kv, n: (0, 0))],
        out_shape=[jax.ShapeDtypeStruct((length, D_MODEL), BF), kv_shape, kv_shape, kv_shape, kv_shape,
                   jax.ShapeDtypeStruct((1, ATT_HEADS), F32)],
        compiler_params=_params(2),
    )(proj, k2, k2, v2, v2, sinks, datt)


def _all_to_all(srcs, group, bcast, name):
    size = {"c": 2, "xy": 4, "all": 8}[group]
    n = len(srcs)
    for s in srcs:
        assert bcast or s.shape[0] == size

    def body(*refs):
        src, out = refs[:n], refs[n:2 * n]
        send_sems, recv_sems, local_sems = refs[2 * n:]
        x, y, c = lax.axis_index("x"), lax.axis_index("y"), lax.axis_index("c")
        if group == "c":
            me = c
        elif group == "xy":
            me = 2 * x + y
        else:
            me = 4 * x + 2 * y + c

        def device(j):
            if group == "c":
                return (x, y, j)
            if group == "xy":
                return (j // 2, j % 2, c)
            return (j // 4, (j // 2) % 2, j % 2)

        def remote(a, d):
            j = jnp.bitwise_xor(me, d)
            return pltpu.make_async_remote_copy(
                src_ref=src[a] if bcast else src[a].at[j], dst_ref=out[a].at[me],
                send_sem=send_sems.at[a * size + d], recv_sem=recv_sems.at[a * size + d],
                device_id=device(j), device_id_type=pl.DeviceIdType.MESH)

        def arrival(a, d):
            j = jnp.bitwise_xor(me, d)
            return pltpu.make_async_remote_copy(
                src_ref=src[a] if bcast else src[a].at[j], dst_ref=out[a].at[j],
                send_sem=send_sems.at[a * size + d], recv_sem=recv_sems.at[a * size + d],
                device_id=device(j), device_id_type=pl.DeviceIdType.MESH)

        local = [pltpu.make_async_copy(src[a] if bcast else src[a].at[me], out[a].at[me], local_sems.at[a]) for a in range(n)]
        for cp in local:
            cp.start()
        sends = [remote(a, d) for a in range(n) for d in range(1, size)]
        for cp in sends:
            cp.start()
        for a in range(n):
            for d in range(1, size):
                arrival(a, d).wait_recv()
        for cp in sends:
            cp.wait_send()
        for cp in local:
            cp.wait()

    any_spec = pl.BlockSpec(memory_space=pl.ANY)
    return pl.pallas_call(
        body, name=name, in_specs=[any_spec] * n, out_specs=[any_spec] * n,
        out_shape=[jax.ShapeDtypeStruct(((size,) + s.shape) if bcast else s.shape, s.dtype) for s in srcs],
        scratch_shapes=[pltpu.SemaphoreType.DMA((n * size,)), pltpu.SemaphoreType.DMA((n * size,)),
                        pltpu.SemaphoreType.DMA((n,))],
    )(*srcs)


def _sum_slots(arr, name):
    k = arr.shape[0]
    rest = arr.shape[1:]
    width = rest[-1]
    rows_per = math.prod(rest[:-1])
    a2 = arr.reshape(k * rows_per, width)
    tile = rows_per
    for cand in (256, 128, 64, 32, 16, 8):
        if rows_per % cand == 0:
            tile = cand
            break
    nt = rows_per // tile

    def body(*refs):
        acc = refs[0][...]
        for r in refs[1:k]:
            acc = acc + r[...]
        refs[k][...] = acc

    in_specs = [pl.BlockSpec((tile, width), functools.partial(lambda i, s: (s * nt + i, 0), s=s)) for s in range(k)]
    out = pl.pallas_call(
        body, name=name, grid=(nt,), in_specs=in_specs, out_specs=pl.BlockSpec((tile, width), lambda i: (i, 0)),
        out_shape=jax.ShapeDtypeStruct((rows_per, width), arr.dtype), compiler_params=_params(1),
    )(*([a2] * k))
    return out.reshape(rest)


def _adamw(w, g, m, v, name):
    def fn(w_, g_, m_, v_):
        m1 = ADAM_B1 * m_ + (1.0 - ADAM_B1) * g_
        v1 = ADAM_B2 * v_ + (1.0 - ADAM_B2) * (g_ * g_)
        m_hat = m1 / (1.0 - ADAM_B1 ** ADAM_STEP)
        v_hat = v1 / (1.0 - ADAM_B2 ** ADAM_STEP)
        delta = -ADAM_LR * (m_hat / (jnp.sqrt(v_hat) + ADAM_EPS) + ADAM_WD * w_)
        return (delta, m1, v1), ()

    rows, width = w.shape
    tile = rows
    for cand in (256, 128, 64, 32, 16, 8):
        if rows % cand == 0:
            tile = cand
            break
    o, _ = _rows(fn, [(w, 0, width), (g, 0, width), (m, 0, width), (v, 0, width)], [],
                 [(width, F32)] * 3, [], tile, name)
    return o


def _pack_cols(w):
    parts = [w[..., o:o + wd] for _, o, wd in _PACK]
    parts.append(jnp.zeros(w.shape[:-1] + (_PAD,), w.dtype))
    return jnp.concatenate(parts, axis=-1)


def _unpack_cols(wp):
    order = sorted(_PACK, key=lambda e: e[1])
    return jnp.concatenate([wp[..., _OFF[nm]:_OFF[nm] + wd] for nm, _, wd in order], axis=-1)


def _col_sharded_full(g):
    return jnp.transpose(g, (1, 0, 2)).reshape(g.shape[1], -1)


def _col_sharded_split(full):
    r, cdim = full.shape
    return jnp.transpose(full.reshape(r, 4, cdim // 4), (1, 0, 2))


def _double_heads(t):
    length = t.shape[0]
    h = jnp.transpose(t.reshape(length, 2, 64), (1, 0, 2))
    return jnp.concatenate([h, h], axis=-1)


def _fold_heads(d_cur, d_prev):
    length = d_cur.shape[1]
    d = d_cur + jnp.concatenate([d_prev[:, BLK:], jnp.zeros((2, BLK, BLK), F32)], axis=1)
    d = d[..., :64] + d[..., 64:]
    return jnp.transpose(d, (1, 0, 2)).reshape(length, 128)


def _ln_fwd(a, r, g, b, name):
    def fn(a_, r_, g_, b_):
        y = _layer_norm(ALPHA * a_ + r_, g_, b_)
        return (y, y), ()
    o, _ = _rows(fn, [(a, 0, D_MODEL), (r, 0, D_MODEL)], [g, b], [(D_MODEL, F32), (D_MODEL, BF)], [], 256, name)
    return o


def _ln_bwd(a, r, d1, d2, g, b, name):
    def fn(a_, r_, d1_, d2_, g_, b_):
        _, vjp = jax.vjp(_layer_norm, ALPHA * a_ + r_, g_, b_)
        ds, dg, db = vjp(ALPHA * d1_ + d2_)
        return (ds, ds), (dg, db)
    o, acc = _rows(fn, [(a, 0, D_MODEL), (r, 0, D_MODEL), (d1, 0, D_MODEL), (d2, 0, D_MODEL)], [g, b],
                   [(D_MODEL, F32), (D_MODEL, BF)], [(1, D_MODEL), (1, D_MODEL)], 256, name)
    return o[0], o[1], acc[0], acc[1]


def _layer_fwd(l, h, hb, w, p):
    sv = {"h": h, "hb": hb}
    proj = _mm(hb, w["win"], "nn", F32, 1024, 512, 1024, f"proj{l}")
    sv["proj"] = proj
    pre = _conv_fwd(proj, p["conv_w"], p["conv_b"], f"conv{l}")
    sv["pre"] = pre
    dt_t = jnp.transpose(proj[:, _OFF["dt"]:_OFF["dt"] + SSD_HEADS])
    sv["dt_t"] = dt_t
    yn, s_in = _ssd_fwd(pre, proj, dt_t, p["dt_bias"], p["a_log"], p["d_skip_c"], p["ssd_norm_w"], f"ssd{l}")
    sv["yn"], sv["s_in"] = yn, s_in
    ya = _mm(yn, w["wso"], "nn", F32, 1024, 1024, 2048, f"ssdout{l}")
    k2 = _double_heads(proj[:, _OFF["k"]:_OFF["k"] + 128])
    v2 = _double_heads(proj[:, _OFF["v"]:_OFF["v"] + 128])
    sv["k2"], sv["v2"] = k2, v2
    att = _attn_fwd(proj, k2, v2, p["att_sinks"], f"attn{l}")
    sv["att"] = att
    yb = _mm(att, w["wao"], "nn", F32, 1024, 1024, 1024, f"attout{l}")
    sv["ya"], sv["yb"] = ya, yb
    gcb = _OFF["gates"] // 1024

    def gate_fn(ga, gb, ya_, yb_):
        return (jax.nn.sigmoid(ga) * ya_ + jax.nn.sigmoid(gb) * yb_,), ()
    (ub,), _ = _rows(gate_fn, [(proj, gcb, 1024), (proj, gcb + 1, 1024), (ya, 0, 1024), (yb, 0, 1024)], [],
                     [(D_MODEL, BF)], [], 256, f"gate{l}")
    sv["ub"] = ub
    mix = _mm(ub, w["wmo"], "nn", F32, 1024, 1024, 1024, f"mixout{l}")
    sv["mix"] = mix
    h1, h1b = _ln_fwd(h, mix, p["ln_mix_g"], p["ln_mix_b"], f"lnmix{l}")
    sv["h1"], sv["h1b"] = h1, h1b
    gu = _mm(h1b, w["wgu"], "nn", F32, 1024, 512, 1024, f"ffnin{l}")
    sv["gu"] = gu

    def act_fn(g_, u_):
        return (_silu(g_) * u_,), ()
    (act,), _ = _rows(act_fn, [(gu, 0, FFN_HIDDEN), (gu, 1, FFN_HIDDEN)], [], [(FFN_HIDDEN, BF)], [], 256, f"swiglu{l}")
    sv["act"] = act
    ffn = _mm(act, w["wd"], "nn", F32, 1024, 1024, FFN_HIDDEN, f"ffnout{l}")
    sv["ffn"] = ffn
    return sv


def _layer_bwd(l, sv, w, p, ds_f, ds_b):
    g = {}
    g["w_ffn_down"] = _mm(sv["act"], ds_b, "tn", F32, 1408, 1024, 512, f"dwd{l}")
    dact = _mm(ds_b, w["wd"], "nt", F32, 1024, FFN_HIDDEN, 1024, f"dact{l}")

    def act_bwd(g_, u_, d_):
        _, vjp = jax.vjp(lambda a, b: _silu(a) * b, g_, u_)
        dg_, du_ = vjp(d_)
        return (jnp.concatenate([dg_, du_], axis=1),), ()
    (dgu,), _ = _rows(act_bwd, [(sv["gu"], 0, FFN_HIDDEN), (sv["gu"], 1, FFN_HIDDEN), (dact, 0, FFN_HIDDEN)], [],
                      [(2 * FFN_HIDDEN, BF)], [], 128, f"dswiglu{l}")
    dwgu = _mm(sv["h1b"], dgu, "tn", F32, 1024, 2816, 512, f"dwgu{l}")
    g["w_ffn_gate"], g["w_ffn_up"] = dwgu[:, :FFN_HIDDEN], dwgu[:, FFN_HIDDEN:]
    dh1 = _mm(dgu, w["wgu"], "nt", F32, 1024, 1024, 512, f"dh1{l}")
    ds2_f, ds2_b, g["ln_mix_g"], g["ln_mix_b"] = _ln_bwd(sv["h"], sv["mix"], ds_f, dh1, p["ln_mix_g"], p["ln_mix_b"], f"dlnmix{l}")
    g["w_mix_out"] = _mm(sv["ub"], ds2_b, "tn", F32, 1024, 1024, 512, f"dwmo{l}")
    du = _mm(ds2_b, w["wmo"], "nt", F32, 1024, 1024, 1024, f"du{l}")
    proj = sv["proj"]
    gcb = _OFF["gates"] // 1024

    def gate_bwd(ga, gb, ya_, yb_, du_):
        _, vjp = jax.vjp(lambda a, b, c, d: jax.nn.sigmoid(a) * c + jax.nn.sigmoid(b) * d, ga, gb, ya_, yb_)
        dga, dgb, dya, dyb = vjp(du_)
        return (jnp.concatenate([dga, dgb], axis=1), dya, dyb), ()
    (dgates, dya, dyb), _ = _rows(
        gate_bwd, [(proj, gcb, 1024), (proj, gcb + 1, 1024), (sv["ya"], 0, 1024), (sv["yb"], 0, 1024), (du, 0, 1024)], [],
        [(2048, BF), (D_MODEL, BF), (D_MODEL, BF)], [], 256, f"dgate{l}")
    g["w_att_out"] = _mm(sv["att"], dyb, "tn", F32, 1024, 1024, 512, f"dwao{l}")
    datt = _mm(dyb, w["wao"], "nt", F32, 1024, 1024, 1024, f"datt{l}")
    g["w_ssd_out"] = _mm(sv["yn"], dya, "tn", F32, 1024, 1024, 512, f"dwso{l}")
    dyn = _mm(dya, w["wso"], "nt", F32, 1024, 1024, 1024, f"dyn{l}")
    dq, dkp, dkc, dvp, dvc, g["att_sinks"] = _attn_bwd(proj, sv["k2"], sv["v2"], p["att_sinks"], datt, f"dattn{l}")
    dk = _fold_heads(dkc, dkp).astype(BF)
    dv = _fold_heads(dvc, dvp).astype(BF)
    (dpx, dpb, dpc, dz, ddt_t, g["dt_bias"], g["a_log"], ddsk, g["ssd_norm_w"]) = _ssd_bwd(
        sv["pre"], proj, sv["dt_t"], p["dt_bias"], p["a_log"], p["d_skip_c"], p["ssd_norm_w"], sv["s_in"], dyn, f"dssd{l}")
    g["d_skip"] = jnp.sum(ddsk.reshape(SSD_HEADS, 64), axis=1)
    dpre = jnp.concatenate([dpx, dpb, dpc], axis=1)
    dxbc, g["conv_w"], g["conv_b"] = _conv_bwd(dpre, proj, p["conv_w"], f"dconv{l}")
    length = proj.shape[0]
    pieces = {"q": dq, "z": dz, "gates": dgates, "xs": dxbc[:, :2048], "B": dxbc[:, 2048:2560], "C": dxbc[:, 2560:],
              "k": dk, "v": dv, "dt": jnp.transpose(ddt_t).astype(BF)}
    dproj = jnp.concatenate([pieces[nm] for nm, _, _ in _PACK] + [jnp.zeros((length, _PAD), BF)], axis=1)
    g["w_in"] = _unpack_cols(_mm(sv["hb"], dproj, "tn", F32, 1024, 2176, 512, f"dwin{l}"))
    dh = _mm(dproj, w["win"], "nt", F32, 1024, 1024, 512, f"dh{l}")
    return g, ds2_f, dh


def kernel(x, ln_in_g, ln_in_b, w_in, conv_w, conv_b, dt_bias, a_log, d_skip, ssd_norm_w, att_sinks, w_ssd_out, w_att_out, w_mix_out, ln_mix_g, ln_mix_b, w_ffn_gate, w_ffn_up, w_ffn_down, ln_ffn_g, ln_ffn_b, loss_target, m_ln_in_g, m_ln_in_b, m_w_in, m_conv_w, m_conv_b, m_dt_bias, m_a_log, m_d_skip, m_ssd_norm_w, m_att_sinks, m_w_ssd_out, m_w_att_out, m_w_mix_out, m_ln_mix_g, m_ln_mix_b, m_w_ffn_gate, m_w_ffn_up, m_w_ffn_down, m_ln_ffn_g, m_ln_ffn_b, v_ln_in_g, v_ln_in_b, v_w_in, v_conv_w, v_conv_b, v_dt_bias, v_a_log, v_d_skip, v_ssd_norm_w, v_att_sinks, v_w_ssd_out, v_w_att_out, v_w_mix_out, v_ln_mix_g, v_ln_mix_b, v_w_ffn_gate, v_w_ffn_up, v_w_ffn_down, v_ln_ffn_g, v_ln_ffn_b):
    env = dict(locals())
    wts = {n: env[n] for n in WEIGHTS}
    mom1 = {n: env["m_" + n] for n in WEIGHTS}
    mom2 = {n: env["v_" + n] for n in WEIGHTS}
    ci = lax.axis_index("c")
    chip = 2 * lax.axis_index("x") + lax.axis_index("y")
    xs_ = x[0]
    tgt = loss_target[0]

    shards = [wts[n].astype(BF) for n in BIG] + [conv_w]
    mine = [lax.dynamic_index_in_dim(s, ci, 0, keepdims=False) for s in shards]
    by_chip = _all_to_all(mine, "xy", True, "gather_chips")
    by_layer = _all_to_all(by_chip, "c", True, "gather_cores")
    full = dict(zip(BIG + ("conv_w",), by_layer))

    layer_w, layer_p = [], []
    for l in range(DEPTH):
        layer_w.append({
            "win": _pack_cols(_col_sharded_full(full["w_in"][l])),
            "wso": full["w_ssd_out"][l].reshape(SSD_D_INNER, D_MODEL),
            "wao": full["w_att_out"][l].reshape(D_MODEL, D_MODEL),
            "wmo": full["w_mix_out"][l].reshape(D_MODEL, D_MODEL),
            "wgu": jnp.concatenate([_col_sharded_full(full["w_ffn_gate"][l]), _col_sharded_full(full["w_ffn_up"][l])], axis=1),
            "wd": full["w_ffn_down"][l].reshape(FFN_HIDDEN, D_MODEL),
        })
        layer_p.append({
            "conv_w": _col_sharded_full(full["conv_w"][l]), "conv_b": conv_b[l][None],
            "dt_bias": dt_bias[l][:, None], "a_log": a_log[l][:, None],
            "d_skip_c": jnp.repeat(d_skip[l], 64)[None], "ssd_norm_w": ssd_norm_w[l][None],
            "att_sinks": att_sinks[l][None], "ln_mix_g": ln_mix_g[l][None], "ln_mix_b": ln_mix_b[l][None],
            "ln_ffn_g": ln_ffn_g[l][None], "ln_ffn_b": ln_ffn_b[l][None],
        })

    def ln_in_fn(x_, g_, b_):
        y = _layer_norm(x_, g_, b_)
        return (y, y), ()
    (h, hb), _ = _rows(ln_in_fn, [(xs_, 0, D_MODEL)], [ln_in_g[None], ln_in_b[None]], [(D_MODEL, F32), (D_MODEL, BF)], [], 256, "ln_in")
    saved = []
    for l in range(DEPTH):
        sv = _layer_fwd(l, h, hb, layer_w[l], layer_p[l])
        saved.append(sv)
        if l < DEPTH - 1:
            h, hb = _ln_fwd(sv["h1"], sv["ffn"], layer_p[l]["ln_ffn_g"], layer_p[l]["ln_ffn_b"], f"lnffn{l}")

    def loss_fn(a_, r_, t_, g_, b_):
        y, vjp = jax.vjp(_layer_norm, ALPHA * a_ + r_, g_, b_)
        err = y - t_
        ds, dg, db = vjp(err * (1.0 / D_MODEL))
        part = 0.5 * jnp.sum(jnp.mean(err * err, axis=-1, keepdims=True), axis=0, keepdims=True)
        return (ds, ds), (dg, db, jnp.broadcast_to(part, (1, BLK)))
    sv = saved[-1]
    (ds_f, ds_b), (dg_last, db_last, loss_part) = _rows(
        loss_fn, [(sv["h1"], 0, D_MODEL), (sv["ffn"], 0, D_MODEL), (tgt, 0, D_MODEL)],
        [layer_p[-1]["ln_ffn_g"], layer_p[-1]["ln_ffn_b"]], [(D_MODEL, F32), (D_MODEL, BF)],
        [(1, D_MODEL), (1, D_MODEL), (1, BLK)], 256, "loss")
    loss = lax.psum(loss_part[0, 0], ("x", "y", "c"))

    grads = [None] * DEPTH
    for l in reversed(range(DEPTH)):
        sv = saved[l]
        if l == DEPTH - 1:
            lg, lb = dg_last, db_last
        else:
            ds_f, ds_b, lg, lb = _ln_bwd(sv["h1"], sv["ffn"], d1, d2, layer_p[l]["ln_ffn_g"], layer_p[l]["ln_ffn_b"], f"dlnffn{l}")
        g, d1, d2 = _layer_bwd(l, sv, layer_w[l], layer_p[l], ds_f, ds_b)
        g["ln_ffn_g"], g["ln_ffn_b"] = lg, lb
        grads[l] = g

    def ln_in_bwd(x_, d1_, d2_, g_, b_):
        _, vjp = jax.vjp(_layer_norm, x_, g_, b_)
        dx, dg, db = vjp(ALPHA * d1_ + d2_)
        return (dx,), (dg, db)
    (grad_x,), (g_ln_in_g, g_ln_in_b) = _rows(
        ln_in_bwd, [(xs_, 0, D_MODEL), (d1, 0, D_MODEL), (d2, 0, D_MODEL)], [ln_in_g[None], ln_in_b[None]],
        [(D_MODEL, F32)], [(1, D_MODEL), (1, D_MODEL)], 256, "dln_in")

    def by_shard(name, gfull):
        if name in ("w_in", "w_ffn_gate", "w_ffn_up"):
            return _col_sharded_split(gfull)
        return gfull.reshape((4, gfull.shape[0] // 4) + gfull.shape[1:])
    big_g = [jnp.stack([by_shard(n, grads[l][n]) for l in range(DEPTH)]) for n in BIG]
    from_cores = _all_to_all(big_g, "c", False, "reduce_cores")
    chip_sum = [_sum_slots(a, f"sum_cores{i}") for i, a in enumerate(from_cores)]
    from_chips = _all_to_all(chip_sum, "xy", False, "reduce_chips")
    reduced = [_sum_slots(a, f"sum_chips{i}") for i, a in enumerate(from_chips)]
    both = _all_to_all(reduced, "c", True, "swap_layers")
    big_grad = dict(zip(BIG, both))

    small_g = {"ln_in_g": g_ln_in_g[0], "ln_in_b": g_ln_in_b[0]}
    for n in SMALL[2:]:
        small_g[n] = jnp.stack([grads[l][n].reshape(wts[n].shape[1:] if n != "conv_w" else (SSD_CONV, CONV_DIM)) for l in range(DEPTH)])
    flat = jnp.concatenate([small_g[n].reshape(-1) for n in SMALL])
    n_small = flat.shape[0]
    width = -(-n_small // 1024) * 1024
    flat = jnp.pad(flat, (0, width - n_small)).reshape(8, width // 8)
    gathered, = _all_to_all([flat], "all", True, "gather_small")
    total = _sum_slots(gathered, "sum_small").reshape(-1)
    small_grad, off = {}, 0
    for n in SMALL:
        shp = small_g[n].shape
        cnt = math.prod(shp)
        small_grad[n] = total[off:off + cnt].reshape(shp)
        off += cnt
    small_grad["conv_w"] = lax.dynamic_slice_in_dim(small_grad["conv_w"], chip * (CONV_DIM // 4), CONV_DIM // 4, axis=2)

    out_g, out_d, out_m, out_v = {}, {}, {}, {}
    for n in BIG:
        shp = wts[n].shape
        two_d = lambda t: t.reshape(shp[0] * shp[1], shp[2])
        d_, m_, v_ = _adamw(two_d(wts[n]), two_d(big_grad[n]), two_d(mom1[n]), two_d(mom2[n]), f"adamw_{n}")
        out_g[n], out_d[n], out_m[n], out_v[n] = big_grad[n], d_.reshape(shp), m_.reshape(shp), v_.reshape(shp)

    def flat_small(d):
        f = jnp.concatenate([d[n].reshape(-1) for n in SMALL])
        return jnp.pad(f, (0, swidth - f.shape[0])).reshape(8, swidth // 8)
    n_sw = sum(math.prod(wts[n].shape) for n in SMALL)
    swidth = -(-n_sw // 1024) * 1024
    d_, m_, v_ = _adamw(flat_small(wts), flat_small(small_grad), flat_small(mom1), flat_small(mom2), "adamw_small")
    off = 0
    for n in SMALL:
        shp = wts[n].shape
        cnt = math.prod(shp)
        out_g[n] = small_grad[n]
        out_d[n] = d_.reshape(-1)[off:off + cnt].reshape(shp)
        out_m[n] = m_.reshape(-1)[off:off + cnt].reshape(shp)
        out_v[n] = v_.reshape(-1)[off:off + cnt].reshape(shp)
        off += cnt

    return (loss, grad_x[None], *[out_g[n] for n in WEIGHTS], *[out_d[n] for n in WEIGHTS],
            *[out_m[n] for n in WEIGHTS], *[out_v[n] for n in WEIGHTS])
```

```python
import functools
import itertools
import math

import jax
import jax.numpy as jnp
from jax import lax
from jax.experimental import pallas as pl
from jax.experimental.pallas import tpu as pltpu

F32 = jnp.float32
BF = jnp.bfloat16

D_MODEL = 1024
DEPTH = 2
ATT_HEADS = 16
ATT_HEAD_DIM = 64
BLK = 128
SSD_D_INNER = 2048
SSD_HEADS = 32
SSD_GROUPS = 4
SSD_STATE = 128
SSD_CONV = 4
BC_DIM = 512
CONV_DIM = 3072
FFN_HIDDEN = 2816
IN_DIM = 8480
LN_EPS = 1e-5
RMS_EPS = 1e-5
ALPHA = (2 * DEPTH) ** 0.25
ADAM_LR = 0.001
ADAM_B1 = 0.9
ADAM_B2 = 0.999
ADAM_EPS = 1e-08
ADAM_WD = 0.01
ADAM_STEP = 10

_PACK = (("q", 0, 1024), ("z", 1280, 2048), ("gates", 6432, 2048), ("xs", 3328, 2048), ("B", 5376, 512),
         ("C", 5888, 512), ("k", 1024, 128), ("v", 1152, 128), ("dt", 6400, 32))
NP = 8704
_OFF = {}
_o = 0
for _n, _, _w in _PACK:
    _OFF[_n] = _o
    _o += _w
_PAD = NP - _o

VMEM_LIMIT_BYTES = 56 * 1024 * 1024
BIG = ("w_in", "w_ssd_out", "w_att_out", "w_mix_out", "w_ffn_gate", "w_ffn_up", "w_ffn_down")
WEIGHTS = ("ln_in_g", "ln_in_b", "w_in", "conv_w", "conv_b", "dt_bias", "a_log", "d_skip", "ssd_norm_w", "att_sinks",
           "w_ssd_out", "w_att_out", "w_mix_out", "ln_mix_g", "ln_mix_b", "w_ffn_gate", "w_ffn_up", "w_ffn_down",
           "ln_ffn_g", "ln_ffn_b")
SMALL = tuple(n for n in WEIGHTS if n not in BIG)


def _params(n_grid):
    return pltpu.CompilerParams(dimension_semantics=("arbitrary",) * n_grid, vmem_limit_bytes=VMEM_LIMIT_BYTES)


def _dot(a, b, ca, cb):
    return lax.dot_general(a.astype(BF), b.astype(BF), (((ca,), (cb,)), ((), ())), preferred_element_type=F32)


@jax.custom_vjp
def _nn(a, b):
    return _dot(a, b, 1, 0)


def _nn_f(a, b):
    return _dot(a, b, 1, 0), (a, b)


def _nn_b(res, g):
    a, b = res
    return _dot(g, b, 1, 1).astype(a.dtype), _dot(a, g, 0, 0).astype(b.dtype)


_nn.defvjp(_nn_f, _nn_b)


@jax.custom_vjp
def _nt(a, b):
    return _dot(a, b, 1, 1)


def _nt_f(a, b):
    return _dot(a, b, 1, 1), (a, b)


def _nt_b(res, g):
    a, b = res
    return _dot(g, b, 1, 0).astype(a.dtype), _dot(g, a, 0, 0).astype(b.dtype)


_nt.defvjp(_nt_f, _nt_b)


@jax.custom_vjp
def _tn(a, b):
    return _dot(a, b, 0, 0)


def _tn_f(a, b):
    return _dot(a, b, 0, 0), (a, b)


def _tn_b(res, g):
    a, b = res
    return _dot(b, g, 1, 1).astype(a.dtype), _dot(a, g, 1, 0).astype(b.dtype)


_tn.defvjp(_tn_f, _tn_b)


def _silu(x):
    return x * jax.nn.sigmoid(x)


def _layer_norm(s, g, b):
    mu = jnp.mean(s, axis=-1, keepdims=True)
    sc = s - mu
    var = jnp.mean(sc * sc, axis=-1, keepdims=True)
    return sc * lax.rsqrt(var + LN_EPS) * g + b


def _ssd_group(pre_x, pre_b, pre_c, z, dtr, dtb, alog, dsk, nw, state):
    t = pre_x.shape[0]
    xs, bm, cm = _silu(pre_x), _silu(pre_b), _silu(pre_c)
    dt = jax.nn.softplus(dtr + dtb)
    da = dt * (-jnp.exp(alog))
    row = lax.broadcasted_iota(jnp.int32, (t, t), 0)
    col = lax.broadcasted_iota(jnp.int32, (t, t), 1)
    upper = (row <= col).astype(F32)
    cum = jnp.dot(da, upper, precision=lax.Precision.HIGHEST, preferred_element_type=F32)
    tot = jnp.sum(da, axis=1, keepdims=True)
    cb = _nt(cm, bm)
    tril = row >= col
    first = lax.broadcasted_iota(jnp.int32, (t, BLK), 1) < 64
    first_row = lax.broadcasted_iota(jnp.int32, (1, BLK), 1) < 64

    def col_form(r):
        return jnp.broadcast_to(r, (t, t)).T

    ys, new_state = [], []
    for p in range(4):
        h0, h1 = 2 * p, 2 * p + 1
        sl = slice(BLK * p, BLK * (p + 1))
        x_pair = xs[:, sl] * jnp.where(first, col_form(dt[h0:h0 + 1]), col_form(dt[h1:h1 + 1]))
        y_pair = None
        cc = []
        for h, keep in ((h0, first), (h1, jnp.logical_not(first))):
            cr = jnp.broadcast_to(cum[h:h + 1], (t, t))
            cc.append(cr.T)
            decay = jnp.exp(jnp.where(tril, cc[-1] - cr, -1e30))
            y_h = _nn(decay * cb, jnp.where(keep, x_pair, 0.0))
            y_pair = y_h if y_pair is None else y_pair + y_h
        s_pair = state[:, sl]
        y_pair = y_pair + _nn(cm, s_pair) * jnp.where(first, jnp.exp(cc[0]), jnp.exp(cc[1]))
        to_end = jnp.where(first, jnp.exp(tot[h0:h0 + 1] - cc[0]), jnp.exp(tot[h1:h1 + 1] - cc[1]))
        chunk_decay = jnp.where(first_row, jnp.exp(tot[h0:h0 + 1]), jnp.exp(tot[h1:h1 + 1]))
        new_state.append(s_pair * chunk_decay + _tn(bm, x_pair * to_end))
        ys.append(y_pair + dsk[:, sl] * xs[:, sl])
    y = jnp.concatenate(ys, axis=1) * _silu(z)
    y = y * lax.rsqrt(jnp.mean(y * y, axis=-1, keepdims=True) + RMS_EPS) * nw
    return y, jnp.concatenate(new_state, axis=1)


def _attn_block(q, kp, kc, vp, vc, sinks, n, kv):
    t = q.shape[0]
    kb = jnp.concatenate([kp, kc], axis=0)
    vb = jnp.concatenate([vp, vc], axis=0)
    qi = lax.broadcasted_iota(jnp.int32, (t, 2 * t), 0)
    kj = lax.broadcasted_iota(jnp.int32, (t, 2 * t), 1)
    rel = qi + t - kj
    valid = (rel >= 0) & (rel < t) & ((n * t - t + kj) >= 0)
    relf = rel.astype(F32)
    first = lax.broadcasted_iota(jnp.int32, (t, BLK), 1) < 64
    lane16 = lax.broadcasted_iota(jnp.int32, (1, ATT_HEADS), 1)
    outs = []
    for p in range(4):
        qp = q[:, BLK * p:BLK * (p + 1)] * (ATT_HEAD_DIM ** -0.5)
        o_pair = None
        for half, keep in enumerate((first, jnp.logical_not(first))):
            h = kv * 8 + 2 * p + half
            s = _nt(jnp.where(keep, qp, 0.0), kb)
            slope = jnp.exp((-8.0 * math.log(2.0) / ATT_HEADS) * (h + 1).astype(F32))
            s = jnp.where(valid, s - slope * relf, -1e30)
            sink = jnp.sum(jnp.where(lane16 == h, sinks, 0.0), axis=1, keepdims=True)
            m = jnp.maximum(jnp.max(s, axis=1, keepdims=True), sink)
            e = jnp.exp(s - m)
            den = jnp.sum(e, axis=1, keepdims=True) + jnp.exp(sink - m)
            o = _nn(e / den, vb)
            o_pair = o if o_pair is None else jnp.where(first, o_pair, o)
        outs.append(o_pair)
    return jnp.concatenate(outs, axis=1)


def _mm(a, b, mode, out_dtype, tm, tn, tk, name):
    if mode == "nn":
        (m, k), (k2, n) = a.shape, b.shape
    elif mode == "nt":
        (m, k), (n, k2) = a.shape, b.shape
    else:
        (k, m), (k2, n) = a.shape, b.shape
    assert k == k2, (a.shape, b.shape, mode)
    tm, tn, tk = min(tm, m), min(tn, n), min(tk, k)
    assert m % tm == 0 and n % tn == 0 and k % tk == 0, (m, n, k, tm, tn, tk)
    nk = k // tk
    ca, cb = {"nn": (1, 0), "nt": (1, 1), "tn": (0, 0)}[mode]

    def body(a_ref, b_ref, o_ref, *acc):
        part = _dot(a_ref[...], b_ref[...], ca, cb)
        if nk == 1:
            o_ref[...] = part.astype(o_ref.dtype)
            return
        acc_ref, = acc
        kk = pl.program_id(2)

        @pl.when(kk == 0)
        def _():
            acc_ref[...] = part

        @pl.when(kk > 0)
        def _():
            acc_ref[...] += part

        @pl.when(kk == nk - 1)
        def _():
            o_ref[...] = acc_ref[...].astype(o_ref.dtype)

    a_spec = pl.BlockSpec((tk, tm), lambda i, j, kk: (kk, i)) if mode == "tn" else pl.BlockSpec((tm, tk), lambda i, j, kk: (i, kk))
    b_spec = pl.BlockSpec((tn, tk), lambda i, j, kk: (j, kk)) if mode == "nt" else pl.BlockSpec((tk, tn), lambda i, j, kk: (kk, j))
    return pl.pallas_call(
        body, name=name, grid=(m // tm, n // tn, nk),
        in_specs=[a_spec, b_spec], out_specs=pl.BlockSpec((tm, tn), lambda i, j, kk: (i, j)),
        out_shape=jax.ShapeDtypeStruct((m, n), out_dtype),
        scratch_shapes=[] if nk == 1 else [pltpu.VMEM((tm, tn), F32)],
        compiler_params=_params(3),
    )(a, b)


def _rows(fn, rows, params, outs, accs, tile, name):
    length = rows[0][0].shape[0]
    tile = min(tile, length)
    assert length % tile == 0
    nr, npar, no = len(rows), len(params), len(outs)

    def body(*refs):
        vals = [r[...] for r in refs[:nr + npar]]
        o, a = fn(*vals)
        for ref, val in zip(refs[nr + npar:nr + npar + no], o):
            ref[...] = val.astype(ref.dtype)
        i = pl.program_id(0)
        for ref, val in zip(refs[nr + npar + no:], a):
            @pl.when(i == 0)
            def _(ref=ref, val=val):
                ref[...] = val

            @pl.when(i > 0)
            def _(ref=ref, val=val):
                ref[...] += val

    in_specs = [pl.BlockSpec((tile, w), functools.partial(lambda i, cb: (i, cb), cb=cb)) for _, cb, w in rows]
    in_specs += [pl.BlockSpec(p.shape, lambda i: (0, 0)) for p in params]
    out_specs = [pl.BlockSpec((tile, w), lambda i: (i, 0)) for w, _ in outs]
    out_specs += [pl.BlockSpec((r, w), lambda i: (0, 0)) for r, w in accs]
    out_shape = [jax.ShapeDtypeStruct((length, w), dt) for w, dt in outs]
    out_shape += [jax.ShapeDtypeStruct((r, w), F32) for r, w in accs]
    res = pl.pallas_call(
        body, name=name, grid=(length // tile,), in_specs=in_specs, out_specs=out_specs, out_shape=out_shape,
        compiler_params=_params(1),
    )(*[r[0] for r in rows], *params)
    return res[:no], res[no:]


def _conv_fwd(proj, conv_w, conv_b, name):
    length = proj.shape[0]
    tl, cw = min(512, length), 512
    cb0 = _OFF["xs"] // cw

    def body(u_ref, halo_ref, w_ref, b_ref, o_ref, win):
        i = pl.program_id(1)
        win[0:8, :] = jnp.where(i > 0, halo_ref[...], 0.0)
        win[8:8 + tl, :] = u_ref[...]
        acc = b_ref[...] + w_ref[0:1, :] * win[5:5 + tl, :]
        for kk in range(1, SSD_CONV):
            acc = acc + w_ref[kk:kk + 1, :] * win[5 + kk:5 + kk + tl, :]
        o_ref[...] = acc

    return pl.pallas_call(
        body, name=name, grid=(CONV_DIM // cw, length // tl),
        in_specs=[pl.BlockSpec((tl, cw), lambda j, i: (i, cb0 + j)),
                  pl.BlockSpec((8, cw), lambda j, i: (jnp.maximum(i * (tl // 8) - 1, 0), cb0 + j)),
                  pl.BlockSpec((SSD_CONV, cw), lambda j, i: (0, j)),
                  pl.BlockSpec((1, cw), lambda j, i: (0, j))],
        out_specs=pl.BlockSpec((tl, cw), lambda j, i: (i, j)),
        out_shape=jax.ShapeDtypeStruct((length, CONV_DIM), F32),
        scratch_shapes=[pltpu.VMEM((8 + tl, cw), F32)],
        compiler_params=_params(2),
    )(proj, proj, conv_w, conv_b)


def _conv_bwd(dpre, proj, conv_w, name):
    length = proj.shape[0]
    tl, cw = min(512, length), 512
    cb0 = _OFF["xs"] // cw
    n_t = length // tl

    def body(d_ref, dnext_ref, u_ref, halo_ref, w_ref, du_ref, dw_ref, db_ref, dwin, uwin):
        i = pl.program_id(1)
        d = d_ref[...]
        dwin[0:tl, :] = d
        dwin[tl:tl + 8, :] = jnp.where(i < n_t - 1, dnext_ref[...], 0.0)
        uwin[0:8, :] = jnp.where(i > 0, halo_ref[...], 0.0)
        uwin[8:8 + tl, :] = u_ref[...]
        du = w_ref[3:4, :] * d
        for kk in range(SSD_CONV - 1):
            du = du + w_ref[kk:kk + 1, :] * dwin[3 - kk:3 - kk + tl, :]
        du_ref[...] = du.astype(du_ref.dtype)
        dws = [jnp.sum(d * uwin[5 + kk:5 + kk + tl, :], axis=0, keepdims=True) for kk in range(SSD_CONV)]
        dw = jnp.concatenate(dws, axis=0)
        db = jnp.sum(d, axis=0, keepdims=True)

        @pl.when(i == 0)
        def _():
            dw_ref[...] = dw
            db_ref[...] = db

        @pl.when(i > 0)
        def _():
            dw_ref[...] += dw
            db_ref[...] += db

    return pl.pallas_call(
        body, name=name, grid=(CONV_DIM // cw, n_t),
        in_specs=[pl.BlockSpec((tl, cw), lambda j, i: (i, j)),
                  pl.BlockSpec((8, cw), lambda j, i: (jnp.minimum((i + 1) * (tl // 8), length // 8 - 1), j)),
                  pl.BlockSpec((tl, cw), lambda j, i: (i, cb0 + j)),
                  pl.BlockSpec((8, cw), lambda j, i: (jnp.maximum(i * (tl // 8) - 1, 0), cb0 + j)),
                  pl.BlockSpec((SSD_CONV, cw), lambda j, i: (0, j))],
        out_specs=[pl.BlockSpec((tl, cw), lambda j, i: (i, j)),
                   pl.BlockSpec((SSD_CONV, cw), lambda j, i: (0, j)),
                   pl.BlockSpec((1, cw), lambda j, i: (0, j))],
        out_shape=[jax.ShapeDtypeStruct((length, CONV_DIM), BF),
                   jax.ShapeDtypeStruct((SSD_CONV, CONV_DIM), F32),
                   jax.ShapeDtypeStruct((1, CONV_DIM), F32)],
        scratch_shapes=[pltpu.VMEM((tl + 8, cw), F32), pltpu.VMEM((8 + tl, cw), F32)],
        compiler_params=_params(2),
    )(dpre, dpre, proj, proj, conv_w)


def _ssd_in_specs(rev, nc):
    def cidx(c):
        return nc - 1 - c if rev else c
    zb = _OFF["z"] // 512
    return [pl.BlockSpec((BLK, 512), lambda g, c: (cidx(c), g)),
            pl.BlockSpec((BLK, BLK), lambda g, c: (cidx(c), 16 + g)),
            pl.BlockSpec((BLK, BLK), lambda g, c: (cidx(c), 20 + g)),
            pl.BlockSpec((BLK, 512), lambda g, c: (cidx(c), zb + g)),
            pl.BlockSpec((8, BLK), lambda g, c: (g, cidx(c))),
            pl.BlockSpec((8, 1), lambda g, c: (g, 0)),
            pl.BlockSpec((8, 1), lambda g, c: (g, 0)),
            pl.BlockSpec((1, 512), lambda g, c: (0, g)),
            pl.BlockSpec((1, 512), lambda g, c: (0, g))]


def _ssd_fwd(pre, proj, dt_t, dtb, alog, dsk, nw, name):
    length = pre.shape[0]
    nc = length // BLK

    def body(px, pb, pc, z, dtr, dtb_r, al_r, dsk_r, nw_r, y_ref, sin_ref, st):
        @pl.when(pl.program_id(1) == 0)
        def _():
            st[...] = jnp.zeros_like(st)

        s_in = st[...]
        sin_ref[...] = s_in
        y, s_out = _ssd_group(px[...], pb[...], pc[...], z[...], dtr[...], dtb_r[...], al_r[...], dsk_r[...], nw_r[...], s_in)
        y_ref[...] = y.astype(y_ref.dtype)
        st[...] = s_out

    return pl.pallas_call(
        body, name=name, grid=(SSD_GROUPS, nc), in_specs=_ssd_in_specs(False, nc),
        out_specs=[pl.BlockSpec((BLK, 512), lambda g, c: (c, g)),
                   pl.BlockSpec((None, None, SSD_STATE, 512), lambda g, c: (g, c, 0, 0))],
        out_shape=[jax.ShapeDtypeStruct((length, SSD_D_INNER), BF),
                   jax.ShapeDtypeStruct((SSD_GROUPS, nc, SSD_STATE, 512), F32)],
        scratch_shapes=[pltpu.VMEM((SSD_STATE, 512), F32)],
        compiler_params=_params(2),
    )(pre, pre, pre, proj, dt_t, dtb, alog, dsk, nw)


def _ssd_bwd(pre, proj, dt_t, dtb, alog, dsk, nw, s_in, dy, name):
    length = pre.shape[0]
    nc = length // BLK

    def body(px, pb, pc, z, dtr, dtb_r, al_r, dsk_r, nw_r, sin_r, dy_r,
             dpx, dpb, dpc, dz, ddt, ddtb, dal, ddsk, dnw, dst):
        c = pl.program_id(1)

        @pl.when(c == 0)
        def _():
            dst[...] = jnp.zeros_like(dst)

        _, vjp = jax.vjp(_ssd_group, px[...], pb[...], pc[...], z[...], dtr[...], dtb_r[...], al_r[...], dsk_r[...],
                         nw_r[...], sin_r[...])
        g = vjp((dy_r[...], dst[...]))
        dpx[...] = g[0]
        dpb[...] = g[1]
        dpc[...] = g[2]
        dz[...] = g[3].astype(dz.dtype)
        ddt[...] = g[4]
        dst[...] = g[9]
        for ref, val in ((ddtb, g[5]), (dal, g[6]), (ddsk, g[7]), (dnw, g[8])):
            @pl.when(c == 0)
            def _(ref=ref, val=val):
                ref[...] = val

            @pl.when(c > 0)
            def _(ref=ref, val=val):
                ref[...] += val

    in_specs = _ssd_in_specs(True, nc) + [
        pl.BlockSpec((None, None, SSD_STATE, 512), lambda g, c: (g, nc - 1 - c, 0, 0)),
        pl.BlockSpec((BLK, 512), lambda g, c: (nc - 1 - c, g))]
    return pl.pallas_call(
        body, name=name, grid=(SSD_GROUPS, nc), in_specs=in_specs,
        out_specs=[pl.BlockSpec((BLK, 512), lambda g, c: (nc - 1 - c, g)),
                   pl.BlockSpec((BLK, BLK), lambda g, c: (nc - 1 - c, g)),
                   pl.BlockSpec((BLK, BLK), lambda g, c: (nc - 1 - c, g)),
                   pl.BlockSpec((BLK, 512), lambda g, c: (nc - 1 - c, g)),
                   pl.BlockSpec((8, BLK), lambda g, c: (g, nc - 1 - c)),
                   pl.BlockSpec((8, 1), lambda g, c: (g, 0)),
                   pl.BlockSpec((8, 1), lambda g, c: (g, 0)),
                   pl.BlockSpec((1, 512), lambda g, c: (0, g)),
                   pl.BlockSpec((1, 512), lambda g, c: (0, g))],
        out_shape=[jax.ShapeDtypeStruct((length, SSD_D_INNER), F32),
                   jax.ShapeDtypeStruct((length, BC_DIM), F32),
                   jax.ShapeDtypeStruct((length, BC_DIM), F32),
                   jax.ShapeDtypeStruct((length, SSD_D_INNER), BF),
                   jax.ShapeDtypeStruct((SSD_HEADS, length), F32),
                   jax.ShapeDtypeStruct((SSD_HEADS, 1), F32),
                   jax.ShapeDtypeStruct((SSD_HEADS, 1), F32),
                   jax.ShapeDtypeStruct((1, SSD_D_INNER), F32),
                   jax.ShapeDtypeStruct((1, SSD_D_INNER), F32)],
        scratch_shapes=[pltpu.VMEM((SSD_STATE, 512), F32)],
        compiler_params=_params(2),
    )(pre, pre, pre, proj, dt_t, dtb, alog, dsk, nw, s_in, dy)


def _attn_in_specs():
    return [pl.BlockSpec((BLK, 512), lambda kv, n: (n, kv)),
            pl.BlockSpec((None, BLK, BLK), lambda kv, n: (kv, jnp.maximum(n - 1, 0), 0)),
            pl.BlockSpec((None, BLK, BLK), lambda kv, n: (kv, n, 0)),
            pl.BlockSpec((None, BLK, BLK), lambda kv, n: (kv, jnp.maximum(n - 1, 0), 0)),
            pl.BlockSpec((None, BLK, BLK), lambda kv, n: (kv, n, 0)),
            pl.BlockSpec((1, ATT_HEADS), lambda kv, n: (0, 0))]


def _attn_fwd(proj, k2, v2, sinks, name):
    length = proj.shape[0]

    def body(q, kp, kc, vp, vc, sk, o_ref):
        o = _attn_block(q[...], kp[...], kc[...], vp[...], vc[...], sk[...], pl.program_id(1), pl.program_id(0))
        o_ref[...] = o.astype(o_ref.dtype)

    return pl.pallas_call(
        body, name=name, grid=(2, length // BLK), in_specs=_attn_in_specs(),
        out_specs=pl.BlockSpec((BLK, 512), lambda kv, n: (n, kv)),
        out_shape=jax.ShapeDtypeStruct((length, D_MODEL), BF),
        compiler_params=_params(2),
    )(proj, k2, k2, v2, v2, sinks)


def _attn_bwd(proj, k2, v2, sinks, datt, name):
    length = proj.shape[0]

    def body(q, kp, kc, vp, vc, sk, do, dq, dkp, dkc, dvp, dvc, dsk):
        n, kv = pl.program_id(1), pl.program_id(0)
        _, vjp = jax.vjp(lambda *a: _attn_block(*a, n, kv), q[...], kp[...], kc[...], vp[...], vc[...], sk[...])
        g = vjp(do[...])
        dq[...] = g[0].astype(dq.dtype)
        dkp[...] = g[1]
        dkc[...] = g[2]
        dvp[...] = g[3]
        dvc[...] = g[4]
        is_first = (n == 0) & (kv == 0)

        @pl.when(is_first)
        def _():
            dsk[...] = g[5]

        @pl.when(jnp.logical_not(is_first))
        def _():
            dsk[...] += g[5]

    blk3 = pl.BlockSpec((None, BLK, BLK), lambda kv, n: (kv, n, 0))
    kv_shape = jax.ShapeDtypeStruct((2, length, BLK), F32)
    return pl.pallas_call(
        body, name=name, grid=(2, length // BLK),
        in_specs=_attn_in_specs() + [pl.BlockSpec((BLK, 512), lambda kv, n: (n, kv))],
        out_specs=[pl.BlockSpec((BLK, 512), lambda kv, n: (n, kv)), blk3, blk3, blk3, blk3,
                   pl.BlockSpec((1, ATT_HEADS), lambda kv, n: (0, 0))],
        out_shape=[jax.ShapeDtypeStruct((length, D_MODEL), BF), kv_shape, kv_shape, kv_shape, kv_shape,
                   jax.ShapeDtypeStruct((1, ATT_HEADS), F32)],
        compiler_params=_params(2),
    )(proj, k2, k2, v2, v2, sinks, datt)


DMA_CHUNK_BYTES = 1 << 20


def _piece_chunks(shape, itemsize):
    if len(shape) < 2 or shape[-2] % 16 != 0:
        return [()]
    rows, cols = shape[-2:]
    step = max(16, DMA_CHUNK_BYTES // (cols * itemsize) // 16 * 16)
    out = []
    for lead in itertools.product(*[range(d) for d in shape[:-2]]):
        for r0 in range(0, rows, step):
            out.append(lead + (pl.ds(r0, min(step, rows - r0)),))
    return out


def _all_to_all(srcs, group, bcast, name):
    size = {"c": 2, "xy": 4, "all": 8}[group]
    n = len(srcs)
    for s in srcs:
        assert bcast or s.shape[0] == size
    chunks = [_piece_chunks(s.shape if bcast else s.shape[1:], s.dtype.itemsize) for s in srcs]

    def body(*refs):
        src, out = refs[:n], refs[n:2 * n]
        send_sems, recv_sems, local_sems = refs[2 * n:]
        x, y, c = lax.axis_index("x"), lax.axis_index("y"), lax.axis_index("c")
        if group == "c":
            me = c
        elif group == "xy":
            me = 2 * x + y
        else:
            me = 4 * x + 2 * y + c

        def device(j):
            if group == "c":
                return (x, y, j)
            if group == "xy":
                return (j // 2, j % 2, c)
            return (j // 4, (j // 2) % 2, j % 2)

        def part(ref, idx):
            return ref.at[idx] if idx else ref

        def piece(a, j):
            return src[a] if bcast else src[a].at[j]

        def remote(a, d, idx, slot):
            j = jnp.bitwise_xor(me, d)
            return pltpu.make_async_remote_copy(
                src_ref=part(piece(a, j), idx), dst_ref=part(out[a].at[slot], idx),
                send_sem=send_sems.at[a * size + d], recv_sem=recv_sems.at[a * size + d],
                device_id=device(j), device_id_type=pl.DeviceIdType.MESH)

        def local(a, idx):
            return pltpu.make_async_copy(part(piece(a, me), idx), part(out[a].at[me], idx), local_sems.at[a])

        for a in range(n):
            for idx in chunks[a]:
                local(a, idx).start()
                for d in range(1, size):
                    remote(a, d, idx, me).start()
        for a in range(n):
            for d in range(1, size):
                remote(a, d, (), jnp.bitwise_xor(me, d)).wait_recv()
        for a in range(n):
            for d in range(1, size):
                remote(a, d, (), me).wait_send()
            local(a, ()).wait()

    any_spec = pl.BlockSpec(memory_space=pl.ANY)
    return pl.pallas_call(
        body, name=name, in_specs=[any_spec] * n, out_specs=[any_spec] * n,
        out_shape=[jax.ShapeDtypeStruct(((size,) + s.shape) if bcast else s.shape, s.dtype) for s in srcs],
        scratch_shapes=[pltpu.SemaphoreType.DMA((n * size,)), pltpu.SemaphoreType.DMA((n * size,)),
                        pltpu.SemaphoreType.DMA((n,))],
    )(*srcs)


def _sum_slots(arr, name):
    k = arr.shape[0]
    rest = arr.shape[1:]
    width = rest[-1]
    rows_per = math.prod(rest[:-1])
    a2 = arr.reshape(k * rows_per, width)
    tile = rows_per
    for cand in (256, 128, 64, 32, 16, 8):
        if rows_per % cand == 0:
            tile = cand
            break
    nt = rows_per // tile

    def body(*refs):
        acc = refs[0][...]
        for r in refs[1:k]:
            acc = acc + r[...]
        refs[k][...] = acc

    in_specs = [pl.BlockSpec((tile, width), functools.partial(lambda i, s: (s * nt + i, 0), s=s)) for s in range(k)]
    out = pl.pallas_call(
        body, name=name, grid=(nt,), in_specs=in_specs, out_specs=pl.BlockSpec((tile, width), lambda i: (i, 0)),
        out_shape=jax.ShapeDtypeStruct((rows_per, width), arr.dtype), compiler_params=_params(1),
    )(*([a2] * k))
    return out.reshape(rest)


def _adamw(w, g, m, v, name):
    def fn(w_, g_, m_, v_):
        m1 = ADAM_B1 * m_ + (1.0 - ADAM_B1) * g_
        v1 = ADAM_B2 * v_ + (1.0 - ADAM_B2) * (g_ * g_)
        m_hat = m1 / (1.0 - ADAM_B1 ** ADAM_STEP)
        v_hat = v1 / (1.0 - ADAM_B2 ** ADAM_STEP)
        delta = -ADAM_LR * (m_hat / (jnp.sqrt(v_hat) + ADAM_EPS) + ADAM_WD * w_)
        return (delta, m1, v1), ()

    rows, width = w.shape
    tile = rows
    for cand in (256, 128, 64, 32, 16, 8):
        if rows % cand == 0:
            tile = cand
            break
    o, _ = _rows(fn, [(w, 0, width), (g, 0, width), (m, 0, width), (v, 0, width)], [],
                 [(width, F32)] * 3, [], tile, name)
    return o


def _pack_cols(w):
    parts = [w[..., o:o + wd] for _, o, wd in _PACK]
    parts.append(jnp.zeros(w.shape[:-1] + (_PAD,), w.dtype))
    return jnp.concatenate(parts, axis=-1)


def _unpack_cols(wp):
    order = sorted(_PACK, key=lambda e: e[1])
    return jnp.concatenate([wp[..., _OFF[nm]:_OFF[nm] + wd] for nm, _, wd in order], axis=-1)


def _col_sharded_full(g):
    return jnp.transpose(g, (1, 0, 2)).reshape(g.shape[1], -1)


def _col_sharded_split(full):
    r, cdim = full.shape
    return jnp.transpose(full.reshape(r, 4, cdim // 4), (1, 0, 2))


def _double_heads(t):
    length = t.shape[0]
    h = jnp.transpose(t.reshape(length, 2, 64), (1, 0, 2))
    return jnp.concatenate([h, h], axis=-1)


def _fold_heads(d_cur, d_prev):
    length = d_cur.shape[1]
    d = d_cur + jnp.concatenate([d_prev[:, BLK:], jnp.zeros((2, BLK, BLK), F32)], axis=1)
    d = d[..., :64] + d[..., 64:]
    return jnp.transpose(d, (1, 0, 2)).reshape(length, 128)


def _ln_fwd(a, r, g, b, name):
    def fn(a_, r_, g_, b_):
        y = _layer_norm(ALPHA * a_ + r_, g_, b_)
        return (y, y), ()
    o, _ = _rows(fn, [(a, 0, D_MODEL), (r, 0, D_MODEL)], [g, b], [(D_MODEL, F32), (D_MODEL, BF)], [], 256, name)
    return o


def _ln_bwd(a, r, d1, d2, g, b, name):
    def fn(a_, r_, d1_, d2_, g_, b_):
        _, vjp = jax.vjp(_layer_norm, ALPHA * a_ + r_, g_, b_)
        ds, dg, db = vjp(ALPHA * d1_ + d2_)
        return (ds, ds), (dg, db)
    o, acc = _rows(fn, [(a, 0, D_MODEL), (r, 0, D_MODEL), (d1, 0, D_MODEL), (d2, 0, D_MODEL)], [g, b],
                   [(D_MODEL, F32), (D_MODEL, BF)], [(1, D_MODEL), (1, D_MODEL)], 256, name)
    return o[0], o[1], acc[0], acc[1]


def _layer_fwd(l, h, hb, w, p):
    sv = {"h": h, "hb": hb}
    proj = _mm(hb, w["win"], "nn", F32, 1024, 512, 1024, f"proj{l}")
    sv["proj"] = proj
    pre = _conv_fwd(proj, p["conv_w"], p["conv_b"], f"conv{l}")
    sv["pre"] = pre
    dt_t = jnp.transpose(proj[:, _OFF["dt"]:_OFF["dt"] + SSD_HEADS])
    sv["dt_t"] = dt_t
    yn, s_in = _ssd_fwd(pre, proj, dt_t, p["dt_bias"], p["a_log"], p["d_skip_c"], p["ssd_norm_w"], f"ssd{l}")
    sv["yn"], sv["s_in"] = yn, s_in
    ya = _mm(yn, w["wso"], "nn", F32, 1024, 1024, 2048, f"ssdout{l}")
    k2 = _double_heads(proj[:, _OFF["k"]:_OFF["k"] + 128])
    v2 = _double_heads(proj[:, _OFF["v"]:_OFF["v"] + 128])
    sv["k2"], sv["v2"] = k2, v2
    att = _attn_fwd(proj, k2, v2, p["att_sinks"], f"attn{l}")
    sv["att"] = att
    yb = _mm(att, w["wao"], "nn", F32, 1024, 1024, 1024, f"attout{l}")
    sv["ya"], sv["yb"] = ya, yb
    gcb = _OFF["gates"] // 1024

    def gate_fn(ga, gb, ya_, yb_):
        return (jax.nn.sigmoid(ga) * ya_ + jax.nn.sigmoid(gb) * yb_,), ()
    (ub,), _ = _rows(gate_fn, [(proj, gcb, 1024), (proj, gcb + 1, 1024), (ya, 0, 1024), (yb, 0, 1024)], [],
                     [(D_MODEL, BF)], [], 256, f"gate{l}")
    sv["ub"] = ub
    mix = _mm(ub, w["wmo"], "nn", F32, 1024, 1024, 1024, f"mixout{l}")
    sv["mix"] = mix
    h1, h1b = _ln_fwd(h, mix, p["ln_mix_g"], p["ln_mix_b"], f"lnmix{l}")
    sv["h1"], sv["h1b"] = h1, h1b
    gu = _mm(h1b, w["wgu"], "nn", F32, 1024, 512, 1024, f"ffnin{l}")
    sv["gu"] = gu

    def act_fn(g_, u_):
        return (_silu(g_) * u_,), ()
    (act,), _ = _rows(act_fn, [(gu, 0, FFN_HIDDEN), (gu, 1, FFN_HIDDEN)], [], [(FFN_HIDDEN, BF)], [], 256, f"swiglu{l}")
    sv["act"] = act
    ffn = _mm(act, w["wd"], "nn", F32, 1024, 1024, FFN_HIDDEN, f"ffnout{l}")
    sv["ffn"] = ffn
    return sv


def _layer_bwd(l, sv, w, p, ds_f, ds_b):
    g = {}
    g["w_ffn_down"] = _mm(sv["act"], ds_b, "tn", F32, 1408, 1024, 512, f"dwd{l}")
    dact = _mm(ds_b, w["wd"], "nt", F32, 1024, FFN_HIDDEN, 1024, f"dact{l}")

    def act_bwd(g_, u_, d_):
        _, vjp = jax.vjp(lambda a, b: _silu(a) * b, g_, u_)
        dg_, du_ = vjp(d_)
        return (jnp.concatenate([dg_, du_], axis=1),), ()
    (dgu,), _ = _rows(act_bwd, [(sv["gu"], 0, FFN_HIDDEN), (sv["gu"], 1, FFN_HIDDEN), (dact, 0, FFN_HIDDEN)], [],
                      [(2 * FFN_HIDDEN, BF)], [], 128, f"dswiglu{l}")
    dwgu = _mm(sv["h1b"], dgu, "tn", F32, 1024, 2816, 512, f"dwgu{l}")
    g["w_ffn_gate"], g["w_ffn_up"] = dwgu[:, :FFN_HIDDEN], dwgu[:, FFN_HIDDEN:]
    dh1 = _mm(dgu, w["wgu"], "nt", F32, 1024, 1024, 512, f"dh1{l}")
    ds2_f, ds2_b, g["ln_mix_g"], g["ln_mix_b"] = _ln_bwd(sv["h"], sv["mix"], ds_f, dh1, p["ln_mix_g"], p["ln_mix_b"], f"dlnmix{l}")
    g["w_mix_out"] = _mm(sv["ub"], ds2_b, "tn", F32, 1024, 1024, 512, f"dwmo{l}")
    du = _mm(ds2_b, w["wmo"], "nt", F32, 1024, 1024, 1024, f"du{l}")
    proj = sv["proj"]
    gcb = _OFF["gates"] // 1024

    def gate_bwd(ga, gb, ya_, yb_, du_):
        _, vjp = jax.vjp(lambda a, b, c, d: jax.nn.sigmoid(a) * c + jax.nn.sigmoid(b) * d, ga, gb, ya_, yb_)
        dga, dgb, dya, dyb = vjp(du_)
        return (jnp.concatenate([dga, dgb], axis=1), dya, dyb), ()
    (dgates, dya, dyb), _ = _rows(
        gate_bwd, [(proj, gcb, 1024), (proj, gcb + 1, 1024), (sv["ya"], 0, 1024), (sv["yb"], 0, 1024), (du, 0, 1024)], [],
        [(2048, BF), (D_MODEL, BF), (D_MODEL, BF)], [], 256, f"dgate{l}")
    g["w_att_out"] = _mm(sv["att"], dyb, "tn", F32, 1024, 1024, 512, f"dwao{l}")
    datt = _mm(dyb, w["wao"], "nt", F32, 1024, 1024, 1024, f"datt{l}")
    g["w_ssd_out"] = _mm(sv["yn"], dya, "tn", F32, 1024, 1024, 512, f"dwso{l}")
    dyn = _mm(dya, w["wso"], "nt", F32, 1024, 1024, 1024, f"dyn{l}")
    dq, dkp, dkc, dvp, dvc, g["att_sinks"] = _attn_bwd(proj, sv["k2"], sv["v2"], p["att_sinks"], datt, f"dattn{l}")
    dk = _fold_heads(dkc, dkp).astype(BF)
    dv = _fold_heads(dvc, dvp).astype(BF)
    (dpx, dpb, dpc, dz, ddt_t, g["dt_bias"], g["a_log"], ddsk, g["ssd_norm_w"]) = _ssd_bwd(
        sv["pre"], proj, sv["dt_t"], p["dt_bias"], p["a_log"], p["d_skip_c"], p["ssd_norm_w"], sv["s_in"], dyn, f"dssd{l}")
    g["d_skip"] = jnp.sum(ddsk.reshape(SSD_HEADS, 64), axis=1)
    dpre = jnp.concatenate([dpx, dpb, dpc], axis=1)
    dxbc, g["conv_w"], g["conv_b"] = _conv_bwd(dpre, proj, p["conv_w"], f"dconv{l}")
    length = proj.shape[0]
    pieces = {"q": dq, "z": dz, "gates": dgates, "xs": dxbc[:, :2048], "B": dxbc[:, 2048:2560], "C": dxbc[:, 2560:],
              "k": dk, "v": dv, "dt": jnp.transpose(ddt_t).astype(BF)}
    dproj = jnp.concatenate([pieces[nm] for nm, _, _ in _PACK] + [jnp.zeros((length, _PAD), BF)], axis=1)
    g["w_in"] = _unpack_cols(_mm(sv["hb"], dproj, "tn", F32, 1024, 2176, 512, f"dwin{l}"))
    dh = _mm(dproj, w["win"], "nt", F32, 1024, 1024, 512, f"dh{l}")
    return g, ds2_f, dh


def kernel(x, ln_in_g, ln_in_b, w_in, conv_w, conv_b, dt_bias, a_log, d_skip, ssd_norm_w, att_sinks, w_ssd_out, w_att_out, w_mix_out, ln_mix_g, ln_mix_b, w_ffn_gate, w_ffn_up, w_ffn_down, ln_ffn_g, ln_ffn_b, loss_target, m_ln_in_g, m_ln_in_b, m_w_in, m_conv_w, m_conv_b, m_dt_bias, m_a_log, m_d_skip, m_ssd_norm_w, m_att_sinks, m_w_ssd_out, m_w_att_out, m_w_mix_out, m_ln_mix_g, m_ln_mix_b, m_w_ffn_gate, m_w_ffn_up, m_w_ffn_down, m_ln_ffn_g, m_ln_ffn_b, v_ln_in_g, v_ln_in_b, v_w_in, v_conv_w, v_conv_b, v_dt_bias, v_a_log, v_d_skip, v_ssd_norm_w, v_att_sinks, v_w_ssd_out, v_w_att_out, v_w_mix_out, v_ln_mix_g, v_ln_mix_b, v_w_ffn_gate, v_w_ffn_up, v_w_ffn_down, v_ln_ffn_g, v_ln_ffn_b):
    env = dict(locals())
    wts = {n: env[n] for n in WEIGHTS}
    mom1 = {n: env["m_" + n] for n in WEIGHTS}
    mom2 = {n: env["v_" + n] for n in WEIGHTS}
    ci = lax.axis_index("c")
    chip = 2 * lax.axis_index("x") + lax.axis_index("y")
    xs_ = x[0]
    tgt = loss_target[0]

    shards = [wts[n].astype(BF) for n in BIG] + [conv_w]
    mine = [lax.dynamic_index_in_dim(s, ci, 0, keepdims=False) for s in shards]
    by_chip = _all_to_all(mine, "xy", True, "gather_chips")
    by_layer = _all_to_all(by_chip, "c", True, "gather_cores")
    full = dict(zip(BIG + ("conv_w",), by_layer))

    layer_w, layer_p = [], []
    for l in range(DEPTH):
        layer_w.append({
            "win": _pack_cols(_col_sharded_full(full["w_in"][l])),
            "wso": full["w_ssd_out"][l].reshape(SSD_D_INNER, D_MODEL),
            "wao": full["w_att_out"][l].reshape(D_MODEL, D_MODEL),
            "wmo": full["w_mix_out"][l].reshape(D_MODEL, D_MODEL),
            "wgu": jnp.concatenate([_col_sharded_full(full["w_ffn_gate"][l]), _col_sharded_full(full["w_ffn_up"][l])], axis=1),
            "wd": full["w_ffn_down"][l].reshape(FFN_HIDDEN, D_MODEL),
        })
        layer_p.append({
            "conv_w": _col_sharded_full(full["conv_w"][l]), "conv_b": conv_b[l][None],
            "dt_bias": dt_bias[l][:, None], "a_log": a_log[l][:, None],
            "d_skip_c": jnp.repeat(d_skip[l], 64)[None], "ssd_norm_w": ssd_norm_w[l][None],
            "att_sinks": att_sinks[l][None], "ln_mix_g": ln_mix_g[l][None], "ln_mix_b": ln_mix_b[l][None],
            "ln_ffn_g": ln_ffn_g[l][None], "ln_ffn_b": ln_ffn_b[l][None],
        })

    def ln_in_fn(x_, g_, b_):
        y = _layer_norm(x_, g_, b_)
        return (y, y), ()
    (h, hb), _ = _rows(ln_in_fn, [(xs_, 0, D_MODEL)], [ln_in_g[None], ln_in_b[None]], [(D_MODEL, F32), (D_MODEL, BF)], [], 256, "ln_in")
    saved = []
    for l in range(DEPTH):
        sv = _layer_fwd(l, h, hb, layer_w[l], layer_p[l])
        saved.append(sv)
        if l < DEPTH - 1:
            h, hb = _ln_fwd(sv["h1"], sv["ffn"], layer_p[l]["ln_ffn_g"], layer_p[l]["ln_ffn_b"], f"lnffn{l}")

    def loss_fn(a_, r_, t_, g_, b_):
        y, vjp = jax.vjp(_layer_norm, ALPHA * a_ + r_, g_, b_)
        err = y - t_
        ds, dg, db = vjp(err * (1.0 / D_MODEL))
        part = 0.5 * jnp.sum(jnp.mean(err * err, axis=-1, keepdims=True), axis=0, keepdims=True)
        return (ds, ds), (dg, db, jnp.broadcast_to(part, (1, BLK)))
    sv = saved[-1]
    (ds_f, ds_b), (dg_last, db_last, loss_part) = _rows(
        loss_fn, [(sv["h1"], 0, D_MODEL), (sv["ffn"], 0, D_MODEL), (tgt, 0, D_MODEL)],
        [layer_p[-1]["ln_ffn_g"], layer_p[-1]["ln_ffn_b"]], [(D_MODEL, F32), (D_MODEL, BF)],
        [(1, D_MODEL), (1, D_MODEL), (1, BLK)], 256, "loss")
    loss = lax.psum(loss_part[0, 0], ("x", "y", "c"))

    grads = [None] * DEPTH
    for l in reversed(range(DEPTH)):
        sv = saved[l]
        if l == DEPTH - 1:
            lg, lb = dg_last, db_last
        else:
            ds_f, ds_b, lg, lb = _ln_bwd(sv["h1"], sv["ffn"], d1, d2, layer_p[l]["ln_ffn_g"], layer_p[l]["ln_ffn_b"], f"dlnffn{l}")
        g, d1, d2 = _layer_bwd(l, sv, layer_w[l], layer_p[l], ds_f, ds_b)
        g["ln_ffn_g"], g["ln_ffn_b"] = lg, lb
        grads[l] = g

    def ln_in_bwd(x_, d1_, d2_, g_, b_):
        _, vjp = jax.vjp(_layer_norm, x_, g_, b_)
        dx, dg, db = vjp(ALPHA * d1_ + d2_)
        return (dx,), (dg, db)
    (grad_x,), (g_ln_in_g, g_ln_in_b) = _rows(
        ln_in_bwd, [(xs_, 0, D_MODEL), (d1, 0, D_MODEL), (d2, 0, D_MODEL)], [ln_in_g[None], ln_in_b[None]],
        [(D_MODEL, F32)], [(1, D_MODEL), (1, D_MODEL)], 256, "dln_in")

    def by_shard(name, gfull):
        if name in ("w_in", "w_ffn_gate", "w_ffn_up"):
            return _col_sharded_split(gfull)
        return gfull.reshape((4, gfull.shape[0] // 4) + gfull.shape[1:])
    big_g = [jnp.stack([by_shard(n, grads[l][n]) for l in range(DEPTH)]) for n in BIG]
    from_cores = _all_to_all(big_g, "c", False, "reduce_cores")
    chip_sum = [_sum_slots(a, f"sum_cores{i}") for i, a in enumerate(from_cores)]
    from_chips = _all_to_all(chip_sum, "xy", False, "reduce_chips")
    reduced = [_sum_slots(a, f"sum_chips{i}") for i, a in enumerate(from_chips)]
    both = _all_to_all(reduced, "c", True, "swap_layers")
    big_grad = dict(zip(BIG, both))

    small_g = {"ln_in_g": g_ln_in_g[0], "ln_in_b": g_ln_in_b[0]}
    for n in SMALL[2:]:
        small_g[n] = jnp.stack([grads[l][n].reshape(wts[n].shape[1:] if n != "conv_w" else (SSD_CONV, CONV_DIM)) for l in range(DEPTH)])
    flat = jnp.concatenate([small_g[n].reshape(-1) for n in SMALL])
    n_small = flat.shape[0]
    width = -(-n_small // 1024) * 1024
    flat = jnp.pad(flat, (0, width - n_small)).reshape(8, width // 8)
    gathered, = _all_to_all([flat], "all", True, "gather_small")
    total = _sum_slots(gathered, "sum_small").reshape(-1)
    small_grad, off = {}, 0
    for n in SMALL:
        shp = small_g[n].shape
        cnt = math.prod(shp)
        small_grad[n] = total[off:off + cnt].reshape(shp)
        off += cnt
    small_grad["conv_w"] = lax.dynamic_slice_in_dim(small_grad["conv_w"], chip * (CONV_DIM // 4), CONV_DIM // 4, axis=2)

    out_g, out_d, out_m, out_v = {}, {}, {}, {}
    for n in BIG:
        shp = wts[n].shape
        two_d = lambda t: t.reshape(shp[0] * shp[1], shp[2])
        d_, m_, v_ = _adamw(two_d(wts[n]), two_d(big_grad[n]), two_d(mom1[n]), two_d(mom2[n]), f"adamw_{n}")
        out_g[n], out_d[n], out_m[n], out_v[n] = big_grad[n], d_.reshape(shp), m_.reshape(shp), v_.reshape(shp)

    def flat_small(d):
        f = jnp.concatenate([d[n].reshape(-1) for n in SMALL])
        return jnp.pad(f, (0, swidth - f.shape[0])).reshape(8, swidth // 8)
    n_sw = sum(math.prod(wts[n].shape) for n in SMALL)
    swidth = -(-n_sw // 1024) * 1024
    d_, m_, v_ = _adamw(flat_small(wts), flat_small(small_grad), flat_small(mom1), flat_small(mom2), "adamw_small")
    off = 0
    for n in SMALL:
        shp = wts[n].shape
        cnt = math.prod(shp)
        out_g[n] = small_grad[n]
        out_d[n] = d_.reshape(-1)[off:off + cnt].reshape(shp)
        out_m[n] = m_.reshape(-1)[off:off + cnt].reshape(shp)
        out_v[n] = v_.reshape(-1)[off:off + cnt].reshape(shp)
        off += cnt

    return (loss, grad_x[None], *[out_g[n] for n in WEIGHTS], *[out_d[n] for n in WEIGHTS],
            *[out_m[n] for n in WEIGHTS], *[out_v[n] for n in WEIGHTS])
```

```python
import functools
import itertools
import math

import jax
import jax.numpy as jnp
from jax import lax
from jax.experimental import pallas as pl
from jax.experimental.pallas import tpu as pltpu

F32 = jnp.float32
BF = jnp.bfloat16

D_MODEL = 1024
DEPTH = 2
ATT_HEADS = 16
ATT_HEAD_DIM = 64
BLK = 128
SSD_D_INNER = 2048
SSD_HEADS = 32
SSD_GROUPS = 4
SSD_STATE = 128
SSD_CONV = 4
BC_DIM = 512
CONV_DIM = 3072
FFN_HIDDEN = 2816
IN_DIM = 8480
LN_EPS = 1e-5
RMS_EPS = 1e-5
ALPHA = (2 * DEPTH) ** 0.25
ADAM_LR = 0.001
ADAM_B1 = 0.9
ADAM_B2 = 0.999
ADAM_EPS = 1e-08
ADAM_WD = 0.01
ADAM_STEP = 10

_PACK = (("q", 0, 1024), ("z", 1280, 2048), ("gates", 6432, 2048), ("xs", 3328, 2048), ("B", 5376, 512),
         ("C", 5888, 512), ("k", 1024, 128), ("v", 1152, 128), ("dt", 6400, 32))
NP = 8704
_OFF = {}
_o = 0
for _n, _, _w in _PACK:
    _OFF[_n] = _o
    _o += _w
_PAD = NP - _o

VMEM_LIMIT_BYTES = 56 * 1024 * 1024
BIG = ("w_in", "w_ssd_out", "w_att_out", "w_mix_out", "w_ffn_gate", "w_ffn_up", "w_ffn_down")
WEIGHTS = ("ln_in_g", "ln_in_b", "w_in", "conv_w", "conv_b", "dt_bias", "a_log", "d_skip", "ssd_norm_w", "att_sinks",
           "w_ssd_out", "w_att_out", "w_mix_out", "ln_mix_g", "ln_mix_b", "w_ffn_gate", "w_ffn_up", "w_ffn_down",
           "ln_ffn_g", "ln_ffn_b")
SMALL = tuple(n for n in WEIGHTS if n not in BIG)


def _params(n_grid):
    return pltpu.CompilerParams(dimension_semantics=("arbitrary",) * n_grid, vmem_limit_bytes=VMEM_LIMIT_BYTES)


def _dot(a, b, ca, cb):
    return lax.dot_general(a.astype(BF), b.astype(BF), (((ca,), (cb,)), ((), ())), preferred_element_type=F32)


@jax.custom_vjp
def _nn(a, b):
    return _dot(a, b, 1, 0)


def _nn_f(a, b):
    return _dot(a, b, 1, 0), (a, b)


def _nn_b(res, g):
    a, b = res
    return _dot(g, b, 1, 1).astype(a.dtype), _dot(a, g, 0, 0).astype(b.dtype)


_nn.defvjp(_nn_f, _nn_b)


@jax.custom_vjp
def _nt(a, b):
    return _dot(a, b, 1, 1)


def _nt_f(a, b):
    return _dot(a, b, 1, 1), (a, b)


def _nt_b(res, g):
    a, b = res
    return _dot(g, b, 1, 0).astype(a.dtype), _dot(g, a, 0, 0).astype(b.dtype)


_nt.defvjp(_nt_f, _nt_b)


@jax.custom_vjp
def _tn(a, b):
    return _dot(a, b, 0, 0)


def _tn_f(a, b):
    return _dot(a, b, 0, 0), (a, b)


def _tn_b(res, g):
    a, b = res
    return _dot(b, g, 1, 1).astype(a.dtype), _dot(a, g, 1, 0).astype(b.dtype)


_tn.defvjp(_tn_f, _tn_b)


def _silu(x):
    return x * jax.nn.sigmoid(x)


def _layer_norm(s, g, b):
    mu = jnp.mean(s, axis=-1, keepdims=True)
    sc = s - mu
    var = jnp.mean(sc * sc, axis=-1, keepdims=True)
    return sc * lax.rsqrt(var + LN_EPS) * g + b


def _ssd_group(pre_x, pre_b, pre_c, z, dtr, dtb, alog, dsk, nw, state):
    t = pre_x.shape[0]
    xs, bm, cm = _silu(pre_x), _silu(pre_b), _silu(pre_c)
    dt = jax.nn.softplus(dtr + dtb)
    da = dt * (-jnp.exp(alog))
    row = lax.broadcasted_iota(jnp.int32, (t, t), 0)
    col = lax.broadcasted_iota(jnp.int32, (t, t), 1)
    upper = (row <= col).astype(F32)
    cum = jnp.dot(da, upper, precision=lax.Precision.HIGHEST, preferred_element_type=F32)
    tot = jnp.sum(da, axis=1, keepdims=True)
    cb = _nt(cm, bm)
    tril = row >= col
    first = lax.broadcasted_iota(jnp.int32, (t, BLK), 1) < 64
    first_row = lax.broadcasted_iota(jnp.int32, (1, BLK), 1) < 64

    def col_form(r):
        return jnp.broadcast_to(r, (t, t)).T

    ys, new_state = [], []
    for p in range(4):
        h0, h1 = 2 * p, 2 * p + 1
        sl = slice(BLK * p, BLK * (p + 1))
        x_pair = xs[:, sl] * jnp.where(first, col_form(dt[h0:h0 + 1]), col_form(dt[h1:h1 + 1]))
        y_pair = None
        cc = []
        for h, keep in ((h0, first), (h1, jnp.logical_not(first))):
            cr = jnp.broadcast_to(cum[h:h + 1], (t, t))
            cc.append(cr.T)
            decay = jnp.exp(jnp.where(tril, cc[-1] - cr, -1e30))
            y_h = _nn(decay * cb, jnp.where(keep, x_pair, 0.0))
            y_pair = y_h if y_pair is None else y_pair + y_h
        s_pair = state[:, sl]
        y_pair = y_pair + _nn(cm, s_pair) * jnp.where(first, jnp.exp(cc[0]), jnp.exp(cc[1]))
        to_end = jnp.where(first, jnp.exp(tot[h0:h0 + 1] - cc[0]), jnp.exp(tot[h1:h1 + 1] - cc[1]))
        chunk_decay = jnp.where(first_row, jnp.exp(tot[h0:h0 + 1]), jnp.exp(tot[h1:h1 + 1]))
        new_state.append(s_pair * chunk_decay + _tn(bm, x_pair * to_end))
        ys.append(y_pair + dsk[:, sl] * xs[:, sl])
    y = jnp.concatenate(ys, axis=1) * _silu(z)
    y = y * lax.rsqrt(jnp.mean(y * y, axis=-1, keepdims=True) + RMS_EPS) * nw
    return y, jnp.concatenate(new_state, axis=1)


def _attn_block(q, kp, kc, vp, vc, sinks, n, kv):
    t = q.shape[0]
    kb = jnp.concatenate([kp, kc], axis=0)
    vb = jnp.concatenate([vp, vc], axis=0)
    qi = lax.broadcasted_iota(jnp.int32, (t, 2 * t), 0)
    kj = lax.broadcasted_iota(jnp.int32, (t, 2 * t), 1)
    rel = qi + t - kj
    valid = (rel >= 0) & (rel < t) & ((n * t - t + kj) >= 0)
    relf = rel.astype(F32)
    first = lax.broadcasted_iota(jnp.int32, (t, BLK), 1) < 64
    lane16 = lax.broadcasted_iota(jnp.int32, (1, ATT_HEADS), 1)
    outs = []
    for p in range(4):
        qp = q[:, BLK * p:BLK * (p + 1)] * (ATT_HEAD_DIM ** -0.5)
        o_pair = None
        for half, keep in enumerate((first, jnp.logical_not(first))):
            h = kv * 8 + 2 * p + half
            s = _nt(jnp.where(keep, qp, 0.0), kb)
            slope = jnp.exp((-8.0 * math.log(2.0) / ATT_HEADS) * (h + 1).astype(F32))
            s = jnp.where(valid, s - slope * relf, -1e30)
            sink = jnp.sum(jnp.where(lane16 == h, sinks, 0.0), axis=1, keepdims=True)
            m = jnp.maximum(jnp.max(s, axis=1, keepdims=True), sink)
            e = jnp.exp(s - m)
            den = jnp.sum(e, axis=1, keepdims=True) + jnp.exp(sink - m)
            o = _nn(e / den, vb)
            o_pair = o if o_pair is None else jnp.where(first, o_pair, o)
        outs.append(o_pair)
    return jnp.concatenate(outs, axis=1)


def _mm(a, b, mode, out_dtype, tm, tn, tk, name):
    if mode == "nn":
        (m, k), (k2, n) = a.shape, b.shape
    elif mode == "nt":
        (m, k), (n, k2) = a.shape, b.shape
    else:
        (k, m), (k2, n) = a.shape, b.shape
    assert k == k2, (a.shape, b.shape, mode)
    tm, tn, tk = min(tm, m), min(tn, n), min(tk, k)
    assert m % tm == 0 and n % tn == 0 and k % tk == 0, (m, n, k, tm, tn, tk)
    nk = k // tk
    ca, cb = {"nn": (1, 0), "nt": (1, 1), "tn": (0, 0)}[mode]

    def body(a_ref, b_ref, o_ref, *acc):
        part = _dot(a_ref[...], b_ref[...], ca, cb)
        if nk == 1:
            o_ref[...] = part.astype(o_ref.dtype)
            return
        acc_ref, = acc
        kk = pl.program_id(2)

        @pl.when(kk == 0)
        def _():
            acc_ref[...] = part

        @pl.when(kk > 0)
        def _():
            acc_ref[...] += part

        @pl.when(kk == nk - 1)
        def _():
            o_ref[...] = acc_ref[...].astype(o_ref.dtype)

    a_spec = pl.BlockSpec((tk, tm), lambda i, j, kk: (kk, i)) if mode == "tn" else pl.BlockSpec((tm, tk), lambda i, j, kk: (i, kk))
    b_spec = pl.BlockSpec((tn, tk), lambda i, j, kk: (j, kk)) if mode == "nt" else pl.BlockSpec((tk, tn), lambda i, j, kk: (kk, j))
    return pl.pallas_call(
        body, name=name, grid=(m // tm, n // tn, nk),
        in_specs=[a_spec, b_spec], out_specs=pl.BlockSpec((tm, tn), lambda i, j, kk: (i, j)),
        out_shape=jax.ShapeDtypeStruct((m, n), out_dtype),
        scratch_shapes=[] if nk == 1 else [pltpu.VMEM((tm, tn), F32)],
        compiler_params=_params(3),
    )(a, b)


def _rows(fn, rows, params, outs, accs, tile, name):
    length = rows[0][0].shape[0]
    tile = min(tile, length)
    assert length % tile == 0
    nr, npar, no = len(rows), len(params), len(outs)

    def body(*refs):
        vals = [r[...] for r in refs[:nr + npar]]
        o, a = fn(*vals)
        for ref, val in zip(refs[nr + npar:nr + npar + no], o):
            ref[...] = val.astype(ref.dtype)
        i = pl.program_id(0)
        for ref, val in zip(refs[nr + npar + no:], a):
            @pl.when(i == 0)
            def _(ref=ref, val=val):
                ref[...] = val

            @pl.when(i > 0)
            def _(ref=ref, val=val):
                ref[...] += val

    in_specs = [pl.BlockSpec((tile, w), functools.partial(lambda i, cb: (i, cb), cb=cb)) for _, cb, w in rows]
    in_specs += [pl.BlockSpec(p.shape, lambda i: (0, 0)) for p in params]
    out_specs = [pl.BlockSpec((tile, w), lambda i: (i, 0)) for w, _ in outs]
    out_specs += [pl.BlockSpec((r, w), lambda i: (0, 0)) for r, w in accs]
    out_shape = [jax.ShapeDtypeStruct((length, w), dt) for w, dt in outs]
    out_shape += [jax.ShapeDtypeStruct((r, w), F32) for r, w in accs]
    res = pl.pallas_call(
        body, name=name, grid=(length // tile,), in_specs=in_specs, out_specs=out_specs, out_shape=out_shape,
        compiler_params=_params(1),
    )(*[r[0] for r in rows], *params)
    return res[:no], res[no:]


def _conv_fwd(proj, conv_w, conv_b, name):
    length = proj.shape[0]
    tl, cw = min(512, length), 512
    cb0 = _OFF["xs"] // cw

    def body(u_ref, halo_ref, w_ref, b_ref, o_ref, win):
        i = pl.program_id(1)
        win[0:8, :] = jnp.where(i > 0, halo_ref[...], 0.0)
        win[8:8 + tl, :] = u_ref[...]
        acc = b_ref[...] + w_ref[0:1, :] * win[5:5 + tl, :]
        for kk in range(1, SSD_CONV):
            acc = acc + w_ref[kk:kk + 1, :] * win[5 + kk:5 + kk + tl, :]
        o_ref[...] = acc

    return pl.pallas_call(
        body, name=name, grid=(CONV_DIM // cw, length // tl),
        in_specs=[pl.BlockSpec((tl, cw), lambda j, i: (i, cb0 + j)),
                  pl.BlockSpec((8, cw), lambda j, i: (jnp.maximum(i * (tl // 8) - 1, 0), cb0 + j)),
                  pl.BlockSpec((SSD_CONV, cw), lambda j, i: (0, j)),
                  pl.BlockSpec((1, cw), lambda j, i: (0, j))],
        out_specs=pl.BlockSpec((tl, cw), lambda j, i: (i, j)),
        out_shape=jax.ShapeDtypeStruct((length, CONV_DIM), F32),
        scratch_shapes=[pltpu.VMEM((8 + tl, cw), F32)],
        compiler_params=_params(2),
    )(proj, proj, conv_w, conv_b)


def _conv_bwd(dpre, proj, conv_w, name):
    length = proj.shape[0]
    tl, cw = min(512, length), 512
    cb0 = _OFF["xs"] // cw
    n_t = length // tl

    def body(d_ref, dnext_ref, u_ref, halo_ref, w_ref, du_ref, dw_ref, db_ref, dwin, uwin):
        i = pl.program_id(1)
        d = d_ref[...]
        dwin[0:tl, :] = d
        dwin[tl:tl + 8, :] = jnp.where(i < n_t - 1, dnext_ref[...], 0.0)
        uwin[0:8, :] = jnp.where(i > 0, halo_ref[...], 0.0)
        uwin[8:8 + tl, :] = u_ref[...]
        du = w_ref[3:4, :] * d
        for kk in range(SSD_CONV - 1):
            du = du + w_ref[kk:kk + 1, :] * dwin[3 - kk:3 - kk + tl, :]
        du_ref[...] = du.astype(du_ref.dtype)
        dws = [jnp.sum(d * uwin[5 + kk:5 + kk + tl, :], axis=0, keepdims=True) for kk in range(SSD_CONV)]
        dw = jnp.concatenate(dws, axis=0)
        db = jnp.sum(d, axis=0, keepdims=True)

        @pl.when(i == 0)
        def _():
            dw_ref[...] = dw
            db_ref[...] = db

        @pl.when(i > 0)
        def _():
            dw_ref[...] += dw
            db_ref[...] += db

    return pl.pallas_call(
        body, name=name, grid=(CONV_DIM // cw, n_t),
        in_specs=[pl.BlockSpec((tl, cw), lambda j, i: (i, j)),
                  pl.BlockSpec((8, cw), lambda j, i: (jnp.minimum((i + 1) * (tl // 8), length // 8 - 1), j)),
                  pl.BlockSpec((tl, cw), lambda j, i: (i, cb0 + j)),
                  pl.BlockSpec((8, cw), lambda j, i: (jnp.maximum(i * (tl // 8) - 1, 0), cb0 + j)),
                  pl.BlockSpec((SSD_CONV, cw), lambda j, i: (0, j))],
        out_specs=[pl.BlockSpec((tl, cw), lambda j, i: (i, j)),
                   pl.BlockSpec((SSD_CONV, cw), lambda j, i: (0, j)),
                   pl.BlockSpec((1, cw), lambda j, i: (0, j))],
        out_shape=[jax.ShapeDtypeStruct((length, CONV_DIM), BF),
                   jax.ShapeDtypeStruct((SSD_CONV, CONV_DIM), F32),
                   jax.ShapeDtypeStruct((1, CONV_DIM), F32)],
        scratch_shapes=[pltpu.VMEM((tl + 8, cw), F32), pltpu.VMEM((8 + tl, cw), F32)],
        compiler_params=_params(2),
    )(dpre, dpre, proj, proj, conv_w)


def _ssd_in_specs(rev, nc):
    def cidx(c):
        return nc - 1 - c if rev else c
    zb = _OFF["z"] // 512
    return [pl.BlockSpec((BLK, 512), lambda g, c: (cidx(c), g)),
            pl.BlockSpec((BLK, BLK), lambda g, c: (cidx(c), 16 + g)),
            pl.BlockSpec((BLK, BLK), lambda g, c: (cidx(c), 20 + g)),
            pl.BlockSpec((BLK, 512), lambda g, c: (cidx(c), zb + g)),
            pl.BlockSpec((8, BLK), lambda g, c: (g, cidx(c))),
            pl.BlockSpec((8, 1), lambda g, c: (g, 0)),
            pl.BlockSpec((8, 1), lambda g, c: (g, 0)),
            pl.BlockSpec((1, 512), lambda g, c: (0, g)),
            pl.BlockSpec((1, 512), lambda g, c: (0, g))]


def _ssd_fwd(pre, proj, dt_t, dtb, alog, dsk, nw, name):
    length = pre.shape[0]
    nc = length // BLK

    def body(px, pb, pc, z, dtr, dtb_r, al_r, dsk_r, nw_r, y_ref, sin_ref, st):
        @pl.when(pl.program_id(1) == 0)
        def _():
            st[...] = jnp.zeros_like(st)

        s_in = st[...]
        sin_ref[...] = s_in
        y, s_out = _ssd_group(px[...], pb[...], pc[...], z[...], dtr[...], dtb_r[...], al_r[...], dsk_r[...], nw_r[...], s_in)
        y_ref[...] = y.astype(y_ref.dtype)
        st[...] = s_out

    return pl.pallas_call(
        body, name=name, grid=(SSD_GROUPS, nc), in_specs=_ssd_in_specs(False, nc),
        out_specs=[pl.BlockSpec((BLK, 512), lambda g, c: (c, g)),
                   pl.BlockSpec((None, None, SSD_STATE, 512), lambda g, c: (g, c, 0, 0))],
        out_shape=[jax.ShapeDtypeStruct((length, SSD_D_INNER), BF),
                   jax.ShapeDtypeStruct((SSD_GROUPS, nc, SSD_STATE, 512), F32)],
        scratch_shapes=[pltpu.VMEM((SSD_STATE, 512), F32)],
        compiler_params=_params(2),
    )(pre, pre, pre, proj, dt_t, dtb, alog, dsk, nw)


def _ssd_bwd(pre, proj, dt_t, dtb, alog, dsk, nw, s_in, dy, name):
    length = pre.shape[0]
    nc = length // BLK

    def body(px, pb, pc, z, dtr, dtb_r, al_r, dsk_r, nw_r, sin_r, dy_r,
             dpx, dpb, dpc, dz, ddt, ddtb, dal, ddsk, dnw, dst):
        c = pl.program_id(1)

        @pl.when(c == 0)
        def _():
            dst[...] = jnp.zeros_like(dst)

        _, vjp = jax.vjp(_ssd_group, px[...], pb[...], pc[...], z[...], dtr[...], dtb_r[...], al_r[...], dsk_r[...],
                         nw_r[...], sin_r[...])
        g = vjp((dy_r[...], dst[...]))
        dpx[...] = g[0]
        dpb[...] = g[1]
        dpc[...] = g[2]
        dz[...] = g[3].astype(dz.dtype)
        ddt[...] = g[4]
        dst[...] = g[9]
        for ref, val in ((ddtb, g[5]), (dal, g[6]), (ddsk, g[7]), (dnw, g[8])):
            @pl.when(c == 0)
            def _(ref=ref, val=val):
                ref[...] = val

            @pl.when(c > 0)
            def _(ref=ref, val=val):
                ref[...] += val

    in_specs = _ssd_in_specs(True, nc) + [
        pl.BlockSpec((None, None, SSD_STATE, 512), lambda g, c: (g, nc - 1 - c, 0, 0)),
        pl.BlockSpec((BLK, 512), lambda g, c: (nc - 1 - c, g))]
    return pl.pallas_call(
        body, name=name, grid=(SSD_GROUPS, nc), in_specs=in_specs,
        out_specs=[pl.BlockSpec((BLK, 512), lambda g, c: (nc - 1 - c, g)),
                   pl.BlockSpec((BLK, BLK), lambda g, c: (nc - 1 - c, g)),
                   pl.BlockSpec((BLK, BLK), lambda g, c: (nc - 1 - c, g)),
                   pl.BlockSpec((BLK, 512), lambda g, c: (nc - 1 - c, g)),
                   pl.BlockSpec((8, BLK), lambda g, c: (g, nc - 1 - c)),
                   pl.BlockSpec((8, 1), lambda g, c: (g, 0)),
                   pl.BlockSpec((8, 1), lambda g, c: (g, 0)),
                   pl.BlockSpec((1, 512), lambda g, c: (0, g)),
                   pl.BlockSpec((1, 512), lambda g, c: (0, g))],
        out_shape=[jax.ShapeDtypeStruct((length, SSD_D_INNER), F32),
                   jax.ShapeDtypeStruct((length, BC_DIM), F32),
                   jax.ShapeDtypeStruct((length, BC_DIM), F32),
                   jax.ShapeDtypeStruct((length, SSD_D_INNER), BF),
                   jax.ShapeDtypeStruct((SSD_HEADS, length), F32),
                   jax.ShapeDtypeStruct((SSD_HEADS, 1), F32),
                   jax.ShapeDtypeStruct((SSD_HEADS, 1), F32),
                   jax.ShapeDtypeStruct((1, SSD_D_INNER), F32),
                   jax.ShapeDtypeStruct((1, SSD_D_INNER), F32)],
        scratch_shapes=[pltpu.VMEM((SSD_STATE, 512), F32)],
        compiler_params=_params(2),
    )(pre, pre, pre, proj, dt_t, dtb, alog, dsk, nw, s_in, dy)


def _attn_in_specs():
    return [pl.BlockSpec((BLK, 512), lambda kv, n: (n, kv)),
            pl.BlockSpec((None, BLK, BLK), lambda kv, n: (kv, jnp.maximum(n - 1, 0), 0)),
            pl.BlockSpec((None, BLK, BLK), lambda kv, n: (kv, n, 0)),
            pl.BlockSpec((None, BLK, BLK), lambda kv, n: (kv, jnp.maximum(n - 1, 0), 0)),
            pl.BlockSpec((None, BLK, BLK), lambda kv, n: (kv, n, 0)),
            pl.BlockSpec((1, ATT_HEADS), lambda kv, n: (0, 0))]


def _attn_fwd(proj, k2, v2, sinks, name):
    length = proj.shape[0]

    def body(q, kp, kc, vp, vc, sk, o_ref):
        o = _attn_block(q[...], kp[...], kc[...], vp[...], vc[...], sk[...], pl.program_id(1), pl.program_id(0))
        o_ref[...] = o.astype(o_ref.dtype)

    return pl.pallas_call(
        body, name=name, grid=(2, length // BLK), in_specs=_attn_in_specs(),
        out_specs=pl.BlockSpec((BLK, 512), lambda kv, n: (n, kv)),
        out_shape=jax.ShapeDtypeStruct((length, D_MODEL), BF),
        compiler_params=_params(2),
    )(proj, k2, k2, v2, v2, sinks)


def _attn_bwd(proj, k2, v2, sinks, datt, name):
    length = proj.shape[0]

    def body(q, kp, kc, vp, vc, sk, do, dq, dkp, dkc, dvp, dvc, dsk):
        n, kv = pl.program_id(1), pl.program_id(0)
        _, vjp = jax.vjp(lambda *a: _attn_block(*a, n, kv), q[...], kp[...], kc[...], vp[...], vc[...], sk[...])
        g = vjp(do[...])
        dq[...] = g[0].astype(dq.dtype)
        dkp[...] = g[1]
        dkc[...] = g[2]
        dvp[...] = g[3]
        dvc[...] = g[4]
        is_first = (n == 0) & (kv == 0)

        @pl.when(is_first)
        def _():
            dsk[...] = g[5]

        @pl.when(jnp.logical_not(is_first))
        def _():
            dsk[...] += g[5]

    blk3 = pl.BlockSpec((None, BLK, BLK), lambda kv, n: (kv, n, 0))
    kv_shape = jax.ShapeDtypeStruct((2, length, BLK), F32)
    return pl.pallas_call(
        body, name=name, grid=(2, length // BLK),
        in_specs=_attn_in_specs() + [pl.BlockSpec((BLK, 512), lambda kv, n: (n, kv))],
        out_specs=[pl.BlockSpec((BLK, 512), lambda kv, n: (n, kv)), blk3, blk3, blk3, blk3,
                   pl.BlockSpec((1, ATT_HEADS), lambda kv, n: (0, 0))],
        out_shape=[jax.ShapeDtypeStruct((length, D_MODEL), BF), kv_shape, kv_shape, kv_shape, kv_shape,
                   jax.ShapeDtypeStruct((1, ATT_HEADS), F32)],
        compiler_params=_params(2),
    )(proj, k2, k2, v2, v2, sinks, datt)


DMA_CHUNK_BYTES = 1 << 20
N_STAGE = 8
LOOKAHEAD = 3


def _piece_chunks(shape, itemsize):
    if len(shape) < 2 or shape[-2] % 16 != 0:
        return [()]
    rows, cols = shape[-2:]
    step = min(rows, max(16, DMA_CHUNK_BYTES // (cols * itemsize) // 16 * 16))
    out = []
    for lead in itertools.product(*[range(d) for d in shape[:-2]]):
        for r0 in range(0, rows, step):
            out.append(lead + (pl.ds(r0, min(step, rows - r0)),))
    return out


def _exchange(srcs, group, bcast, name):
    size = {"c": 2, "xy": 4, "all": 8}[group]
    npeer = size - 1
    n = len(srcs)
    for s in srcs:
        assert bcast or s.shape[0] == size
    pieces = [s.shape if bcast else s.shape[1:] for s in srcs]
    chunks = [_piece_chunks(p, s.dtype.itemsize) for p, s in zip(pieces, srcs)]

    def body(*refs):
        src, out = refs[:n], refs[n:2 * n]
        load_sems, send_sems, recv_sems = refs[2 * n:]
        x, y, c = lax.axis_index("x"), lax.axis_index("y"), lax.axis_index("c")
        if group == "c":
            me = c
        elif group == "xy":
            me = 2 * x + y
        else:
            me = 4 * x + 2 * y + c

        def device(j):
            if group == "c":
                return (x, y, j)
            if group == "xy":
                return (j // 2, j % 2, c)
            return (j // 4, (j // 2) % 2, j % 2)

        def part(ref, idx):
            return ref.at[idx] if idx else ref

        def piece(a, j):
            return src[a] if bcast else src[a].at[j]

        for a in range(n):
            jobs = [(None, idx) for idx in chunks[a]] if bcast else [(d, idx) for idx in chunks[a] for d in range(1, size)]
            full_rows = chunks[a][0][-1].size if chunks[a][0] else None
            slot_shape = (full_rows, pieces[a][-1]) if chunks[a][0] else pieces[a]

            def stream(stage, a=a, jobs=jobs, full_rows=full_rows):
                def slot(q):
                    idx = jobs[q][1]
                    view = stage.at[q % N_STAGE]
                    if idx and idx[-1].size != full_rows:
                        view = view.at[pl.ds(0, idx[-1].size)]
                    return view

                def load(q):
                    d, idx = jobs[q]
                    j = me if d is None else jnp.bitwise_xor(me, d)
                    return pltpu.make_async_copy(part(piece(a, j), idx), slot(q), load_sems.at[q % N_STAGE])

                def sends(q):
                    d, idx = jobs[q]
                    return [pltpu.make_async_remote_copy(
                        src_ref=slot(q), dst_ref=part(out[a].at[me], idx),
                        send_sem=send_sems.at[(q % N_STAGE) * npeer + dd - 1], recv_sem=recv_sems.at[a * size + dd],
                        device_id=device(jnp.bitwise_xor(me, dd)), device_id_type=pl.DeviceIdType.MESH)
                        for dd in (range(1, size) if d is None else (d,))]

                nq = len(jobs)
                for q in range(nq + LOOKAHEAD):
                    if q < nq:
                        if q >= N_STAGE:
                            for cp in sends(q - N_STAGE):
                                cp.wait_send()
                        load(q).start()
                    if q >= LOOKAHEAD:
                        load(q - LOOKAHEAD).wait()
                        for cp in sends(q - LOOKAHEAD):
                            cp.start()
                for q in range(max(0, nq - N_STAGE), nq):
                    for cp in sends(q):
                        cp.wait_send()

            pl.run_scoped(stream, pltpu.VMEM((N_STAGE,) + tuple(slot_shape), srcs[a].dtype))

        for a in range(n):
            for d in range(1, size):
                j = jnp.bitwise_xor(me, d)
                pltpu.make_async_remote_copy(
                    src_ref=piece(a, j), dst_ref=out[a].at[j], send_sem=send_sems.at[0], recv_sem=recv_sems.at[a * size + d],
                    device_id=device(j), device_id_type=pl.DeviceIdType.MESH).wait_recv()

    any_spec = pl.BlockSpec(memory_space=pl.ANY)
    return pl.pallas_call(
        body, name=name, in_specs=[any_spec] * n, out_specs=[any_spec] * n,
        out_shape=[jax.ShapeDtypeStruct((size,) + tuple(p), s.dtype) for p, s in zip(pieces, srcs)],
        scratch_shapes=[pltpu.SemaphoreType.DMA((N_STAGE,)), pltpu.SemaphoreType.DMA((N_STAGE * npeer,)),
                        pltpu.SemaphoreType.DMA((n * size,))],
        compiler_params=pltpu.CompilerParams(vmem_limit_bytes=VMEM_LIMIT_BYTES),
    )(*srcs)


def _with_own(outs, owns, me):
    return [lax.dynamic_update_index_in_dim(o, w, me, 0) for o, w in zip(outs, owns)]


def _sum_slots(arr, name):
    k = arr.shape[0]
    rest = arr.shape[1:]
    width = rest[-1]
    rows_per = math.prod(rest[:-1])
    a2 = arr.reshape(k * rows_per, width)
    tile = rows_per
    for cand in (256, 128, 64, 32, 16, 8):
        if rows_per % cand == 0:
            tile = cand
            break
    nt = rows_per // tile

    def body(*refs):
        acc = refs[0][...]
        for r in refs[1:k]:
            acc = acc + r[...]
        refs[k][...] = acc

    in_specs = [pl.BlockSpec((tile, width), functools.partial(lambda i, s: (s * nt + i, 0), s=s)) for s in range(k)]
    out = pl.pallas_call(
        body, name=name, grid=(nt,), in_specs=in_specs, out_specs=pl.BlockSpec((tile, width), lambda i: (i, 0)),
        out_shape=jax.ShapeDtypeStruct((rows_per, width), arr.dtype), compiler_params=_params(1),
    )(*([a2] * k))
    return out.reshape(rest)


def _adamw(w, g, m, v, name):
    def fn(w_, g_, m_, v_):
        m1 = ADAM_B1 * m_ + (1.0 - ADAM_B1) * g_
        v1 = ADAM_B2 * v_ + (1.0 - ADAM_B2) * (g_ * g_)
        m_hat = m1 / (1.0 - ADAM_B1 ** ADAM_STEP)
        v_hat = v1 / (1.0 - ADAM_B2 ** ADAM_STEP)
        delta = -ADAM_LR * (m_hat / (jnp.sqrt(v_hat) + ADAM_EPS) + ADAM_WD * w_)
        return (delta, m1, v1), ()

    rows, width = w.shape
    tile = rows
    for cand in (256, 128, 64, 32, 16, 8):
        if rows % cand == 0:
            tile = cand
            break
    o, _ = _rows(fn, [(w, 0, width), (g, 0, width), (m, 0, width), (v, 0, width)], [],
                 [(width, F32)] * 3, [], tile, name)
    return o


def _pack_cols(w):
    parts = [w[..., o:o + wd] for _, o, wd in _PACK]
    parts.append(jnp.zeros(w.shape[:-1] + (_PAD,), w.dtype))
    return jnp.concatenate(parts, axis=-1)


def _unpack_cols(wp):
    order = sorted(_PACK, key=lambda e: e[1])
    return jnp.concatenate([wp[..., _OFF[nm]:_OFF[nm] + wd] for nm, _, wd in order], axis=-1)


def _col_sharded_full(g):
    return jnp.transpose(g, (1, 0, 2)).reshape(g.shape[1], -1)


def _col_sharded_split(full):
    r, cdim = full.shape
    return jnp.transpose(full.reshape(r, 4, cdim // 4), (1, 0, 2))


def _double_heads(t):
    length = t.shape[0]
    h = jnp.transpose(t.reshape(length, 2, 64), (1, 0, 2))
    return jnp.concatenate([h, h], axis=-1)


def _fold_heads(d_cur, d_prev):
    length = d_cur.shape[1]
    d = d_cur + jnp.concatenate([d_prev[:, BLK:], jnp.zeros((2, BLK, BLK), F32)], axis=1)
    d = d[..., :64] + d[..., 64:]
    return jnp.transpose(d, (1, 0, 2)).reshape(length, 128)


def _ln_fwd(a, r, g, b, name):
    def fn(a_, r_, g_, b_):
        y = _layer_norm(ALPHA * a_ + r_, g_, b_)
        return (y, y), ()
    o, _ = _rows(fn, [(a, 0, D_MODEL), (r, 0, D_MODEL)], [g, b], [(D_MODEL, F32), (D_MODEL, BF)], [], 256, name)
    return o


def _ln_bwd(a, r, d1, d2, g, b, name):
    def fn(a_, r_, d1_, d2_, g_, b_):
        _, vjp = jax.vjp(_layer_norm, ALPHA * a_ + r_, g_, b_)
        ds, dg, db = vjp(ALPHA * d1_ + d2_)
        return (ds, ds), (dg, db)
    o, acc = _rows(fn, [(a, 0, D_MODEL), (r, 0, D_MODEL), (d1, 0, D_MODEL), (d2, 0, D_MODEL)], [g, b],
                   [(D_MODEL, F32), (D_MODEL, BF)], [(1, D_MODEL), (1, D_MODEL)], 256, name)
    return o[0], o[1], acc[0], acc[1]


def _layer_fwd(l, h, hb, w, p):
    sv = {"h": h, "hb": hb}
    proj = _mm(hb, w["win"], "nn", F32, 1024, 512, 1024, f"proj{l}")
    sv["proj"] = proj
    pre = _conv_fwd(proj, p["conv_w"], p["conv_b"], f"conv{l}")
    sv["pre"] = pre
    dt_t = jnp.transpose(proj[:, _OFF["dt"]:_OFF["dt"] + SSD_HEADS])
    sv["dt_t"] = dt_t
    yn, s_in = _ssd_fwd(pre, proj, dt_t, p["dt_bias"], p["a_log"], p["d_skip_c"], p["ssd_norm_w"], f"ssd{l}")
    sv["yn"], sv["s_in"] = yn, s_in
    ya = _mm(yn, w["wso"], "nn", F32, 1024, 1024, 2048, f"ssdout{l}")
    k2 = _double_heads(proj[:, _OFF["k"]:_OFF["k"] + 128])
    v2 = _double_heads(proj[:, _OFF["v"]:_OFF["v"] + 128])
    sv["k2"], sv["v2"] = k2, v2
    att = _attn_fwd(proj, k2, v2, p["att_sinks"], f"attn{l}")
    sv["att"] = att
    yb = _mm(att, w["wao"], "nn", F32, 1024, 1024, 1024, f"attout{l}")
    sv["ya"], sv["yb"] = ya, yb
    gcb = _OFF["gates"] // 1024

    def gate_fn(ga, gb, ya_, yb_):
        return (jax.nn.sigmoid(ga) * ya_ + jax.nn.sigmoid(gb) * yb_,), ()
    (ub,), _ = _rows(gate_fn, [(proj, gcb, 1024), (proj, gcb + 1, 1024), (ya, 0, 1024), (yb, 0, 1024)], [],
                     [(D_MODEL, BF)], [], 256, f"gate{l}")
    sv["ub"] = ub
    mix = _mm(ub, w["wmo"], "nn", F32, 1024, 1024, 1024, f"mixout{l}")
    sv["mix"] = mix
    h1, h1b = _ln_fwd(h, mix, p["ln_mix_g"], p["ln_mix_b"], f"lnmix{l}")
    sv["h1"], sv["h1b"] = h1, h1b
    gu = _mm(h1b, w["wgu"], "nn", F32, 1024, 512, 1024, f"ffnin{l}")
    sv["gu"] = gu

    def act_fn(g_, u_):
        return (_silu(g_) * u_,), ()
    (act,), _ = _rows(act_fn, [(gu, 0, FFN_HIDDEN), (gu, 1, FFN_HIDDEN)], [], [(FFN_HIDDEN, BF)], [], 256, f"swiglu{l}")
    sv["act"] = act
    ffn = _mm(act, w["wd"], "nn", F32, 1024, 1024, FFN_HIDDEN, f"ffnout{l}")
    sv["ffn"] = ffn
    return sv


def _layer_bwd(l, sv, w, p, ds_f, ds_b):
    g = {}
    g["w_ffn_down"] = _mm(sv["act"], ds_b, "tn", F32, 1408, 1024, 512, f"dwd{l}")
    dact = _mm(ds_b, w["wd"], "nt", F32, 1024, FFN_HIDDEN, 1024, f"dact{l}")

    def act_bwd(g_, u_, d_):
        _, vjp = jax.vjp(lambda a, b: _silu(a) * b, g_, u_)
        dg_, du_ = vjp(d_)
        return (jnp.concatenate([dg_, du_], axis=1),), ()
    (dgu,), _ = _rows(act_bwd, [(sv["gu"], 0, FFN_HIDDEN), (sv["gu"], 1, FFN_HIDDEN), (dact, 0, FFN_HIDDEN)], [],
                      [(2 * FFN_HIDDEN, BF)], [], 128, f"dswiglu{l}")
    dwgu = _mm(sv["h1b"], dgu, "tn", F32, 1024, 2816, 512, f"dwgu{l}")
    g["w_ffn_gate"], g["w_ffn_up"] = dwgu[:, :FFN_HIDDEN], dwgu[:, FFN_HIDDEN:]
    dh1 = _mm(dgu, w["wgu"], "nt", F32, 1024, 1024, 512, f"dh1{l}")
    ds2_f, ds2_b, g["ln_mix_g"], g["ln_mix_b"] = _ln_bwd(sv["h"], sv["mix"], ds_f, dh1, p["ln_mix_g"], p["ln_mix_b"], f"dlnmix{l}")
    g["w_mix_out"] = _mm(sv["ub"], ds2_b, "tn", F32, 1024, 1024, 512, f"dwmo{l}")
    du = _mm(ds2_b, w["wmo"], "nt", F32, 1024, 1024, 1024, f"du{l}")
    proj = sv["proj"]
    gcb = _OFF["gates"] // 1024

    def gate_bwd(ga, gb, ya_, yb_, du_):
        _, vjp = jax.vjp(lambda a, b, c, d: jax.nn.sigmoid(a) * c + jax.nn.sigmoid(b) * d, ga, gb, ya_, yb_)
        dga, dgb, dya, dyb = vjp(du_)
        return (jnp.concatenate([dga, dgb], axis=1), dya, dyb), ()
    (dgates, dya, dyb), _ = _rows(
        gate_bwd, [(proj, gcb, 1024), (proj, gcb + 1, 1024), (sv["ya"], 0, 1024), (sv["yb"], 0, 1024), (du, 0, 1024)], [],
        [(2048, BF), (D_MODEL, BF), (D_MODEL, BF)], [], 256, f"dgate{l}")
    g["w_att_out"] = _mm(sv["att"], dyb, "tn", F32, 1024, 1024, 512, f"dwao{l}")
    datt = _mm(dyb, w["wao"], "nt", F32, 1024, 1024, 1024, f"datt{l}")
    g["w_ssd_out"] = _mm(sv["yn"], dya, "tn", F32, 1024, 1024, 512, f"dwso{l}")
    dyn = _mm(dya, w["wso"], "nt", F32, 1024, 1024, 1024, f"dyn{l}")
    dq, dkp, dkc, dvp, dvc, g["att_sinks"] = _attn_bwd(proj, sv["k2"], sv["v2"], p["att_sinks"], datt, f"dattn{l}")
    dk = _fold_heads(dkc, dkp).astype(BF)
    dv = _fold_heads(dvc, dvp).astype(BF)
    (dpx, dpb, dpc, dz, ddt_t, g["dt_bias"], g["a_log"], ddsk, g["ssd_norm_w"]) = _ssd_bwd(
        sv["pre"], proj, sv["dt_t"], p["dt_bias"], p["a_log"], p["d_skip_c"], p["ssd_norm_w"], sv["s_in"], dyn, f"dssd{l}")
    g["d_skip"] = jnp.sum(ddsk.reshape(SSD_HEADS, 64), axis=1)
    dpre = jnp.concatenate([dpx, dpb, dpc], axis=1)
    dxbc, g["conv_w"], g["conv_b"] = _conv_bwd(dpre, proj, p["conv_w"], f"dconv{l}")
    length = proj.shape[0]
    pieces = {"q": dq, "z": dz, "gates": dgates, "xs": dxbc[:, :2048], "B": dxbc[:, 2048:2560], "C": dxbc[:, 2560:],
              "k": dk, "v": dv, "dt": jnp.transpose(ddt_t).astype(BF)}
    dproj = jnp.concatenate([pieces[nm] for nm, _, _ in _PACK] + [jnp.zeros((length, _PAD), BF)], axis=1)
    g["w_in"] = _unpack_cols(_mm(sv["hb"], dproj, "tn", F32, 1024, 2176, 512, f"dwin{l}"))
    dh = _mm(dproj, w["win"], "nt", F32, 1024, 1024, 512, f"dh{l}")
    return g, ds2_f, dh


def kernel(x, ln_in_g, ln_in_b, w_in, conv_w, conv_b, dt_bias, a_log, d_skip, ssd_norm_w, att_sinks, w_ssd_out, w_att_out, w_mix_out, ln_mix_g, ln_mix_b, w_ffn_gate, w_ffn_up, w_ffn_down, ln_ffn_g, ln_ffn_b, loss_target, m_ln_in_g, m_ln_in_b, m_w_in, m_conv_w, m_conv_b, m_dt_bias, m_a_log, m_d_skip, m_ssd_norm_w, m_att_sinks, m_w_ssd_out, m_w_att_out, m_w_mix_out, m_ln_mix_g, m_ln_mix_b, m_w_ffn_gate, m_w_ffn_up, m_w_ffn_down, m_ln_ffn_g, m_ln_ffn_b, v_ln_in_g, v_ln_in_b, v_w_in, v_conv_w, v_conv_b, v_dt_bias, v_a_log, v_d_skip, v_ssd_norm_w, v_att_sinks, v_w_ssd_out, v_w_att_out, v_w_mix_out, v_ln_mix_g, v_ln_mix_b, v_w_ffn_gate, v_w_ffn_up, v_w_ffn_down, v_ln_ffn_g, v_ln_ffn_b):
    env = dict(locals())
    wts = {n: env[n] for n in WEIGHTS}
    mom1 = {n: env["m_" + n] for n in WEIGHTS}
    mom2 = {n: env["v_" + n] for n in WEIGHTS}
    ci = lax.axis_index("c")
    chip = 2 * lax.axis_index("x") + lax.axis_index("y")
    xs_ = x[0]
    tgt = loss_target[0]

    shards = [wts[n].astype(BF) for n in BIG] + [conv_w]
    mine = [lax.dynamic_index_in_dim(s, ci, 0, keepdims=False) for s in shards]
    by_chip = _with_own(_exchange(mine, "xy", True, "gather_chips"), mine, chip)
    by_layer = _with_own(_exchange(by_chip, "c", True, "gather_cores"), by_chip, ci)
    full = dict(zip(BIG + ("conv_w",), by_layer))

    layer_w, layer_p = [], []
    for l in range(DEPTH):
        layer_w.append({
            "win": _pack_cols(_col_sharded_full(full["w_in"][l])),
            "wso": full["w_ssd_out"][l].reshape(SSD_D_INNER, D_MODEL),
            "wao": full["w_att_out"][l].reshape(D_MODEL, D_MODEL),
            "wmo": full["w_mix_out"][l].reshape(D_MODEL, D_MODEL),
            "wgu": jnp.concatenate([_col_sharded_full(full["w_ffn_gate"][l]), _col_sharded_full(full["w_ffn_up"][l])], axis=1),
            "wd": full["w_ffn_down"][l].reshape(FFN_HIDDEN, D_MODEL),
        })
        layer_p.append({
            "conv_w": _col_sharded_full(full["conv_w"][l]), "conv_b": conv_b[l][None],
            "dt_bias": dt_bias[l][:, None], "a_log": a_log[l][:, None],
            "d_skip_c": jnp.repeat(d_skip[l], 64)[None], "ssd_norm_w": ssd_norm_w[l][None],
            "att_sinks": att_sinks[l][None], "ln_mix_g": ln_mix_g[l][None], "ln_mix_b": ln_mix_b[l][None],
            "ln_ffn_g": ln_ffn_g[l][None], "ln_ffn_b": ln_ffn_b[l][None],
        })

    def ln_in_fn(x_, g_, b_):
        y = _layer_norm(x_, g_, b_)
        return (y, y), ()
    (h, hb), _ = _rows(ln_in_fn, [(xs_, 0, D_MODEL)], [ln_in_g[None], ln_in_b[None]], [(D_MODEL, F32), (D_MODEL, BF)], [], 256, "ln_in")
    saved = []
    for l in range(DEPTH):
        sv = _layer_fwd(l, h, hb, layer_w[l], layer_p[l])
        saved.append(sv)
        if l < DEPTH - 1:
            h, hb = _ln_fwd(sv["h1"], sv["ffn"], layer_p[l]["ln_ffn_g"], layer_p[l]["ln_ffn_b"], f"lnffn{l}")

    def loss_fn(a_, r_, t_, g_, b_):
        y, vjp = jax.vjp(_layer_norm, ALPHA * a_ + r_, g_, b_)
        err = y - t_
        ds, dg, db = vjp(err * (1.0 / D_MODEL))
        part = 0.5 * jnp.sum(jnp.mean(err * err, axis=-1, keepdims=True), axis=0, keepdims=True)
        return (ds, ds), (dg, db, jnp.broadcast_to(part, (1, BLK)))
    sv = saved[-1]
    (ds_f, ds_b), (dg_last, db_last, loss_part) = _rows(
        loss_fn, [(sv["h1"], 0, D_MODEL), (sv["ffn"], 0, D_MODEL), (tgt, 0, D_MODEL)],
        [layer_p[-1]["ln_ffn_g"], layer_p[-1]["ln_ffn_b"]], [(D_MODEL, F32), (D_MODEL, BF)],
        [(1, D_MODEL), (1, D_MODEL), (1, BLK)], 256, "loss")
    loss = lax.psum(loss_part[0, 0], ("x", "y", "c"))

    grads = [None] * DEPTH
    for l in reversed(range(DEPTH)):
        sv = saved[l]
        if l == DEPTH - 1:
            lg, lb = dg_last, db_last
        else:
            ds_f, ds_b, lg, lb = _ln_bwd(sv["h1"], sv["ffn"], d1, d2, layer_p[l]["ln_ffn_g"], layer_p[l]["ln_ffn_b"], f"dlnffn{l}")
        g, d1, d2 = _layer_bwd(l, sv, layer_w[l], layer_p[l], ds_f, ds_b)
        g["ln_ffn_g"], g["ln_ffn_b"] = lg, lb
        grads[l] = g

    def ln_in_bwd(x_, d1_, d2_, g_, b_):
        _, vjp = jax.vjp(_layer_norm, x_, g_, b_)
        dx, dg, db = vjp(ALPHA * d1_ + d2_)
        return (dx,), (dg, db)
    (grad_x,), (g_ln_in_g, g_ln_in_b) = _rows(
        ln_in_bwd, [(xs_, 0, D_MODEL), (d1, 0, D_MODEL), (d2, 0, D_MODEL)], [ln_in_g[None], ln_in_b[None]],
        [(D_MODEL, F32)], [(1, D_MODEL), (1, D_MODEL)], 256, "dln_in")

    def by_shard(name, gfull):
        if name in ("w_in", "w_ffn_gate", "w_ffn_up"):
            return _col_sharded_split(gfull)
        return gfull.reshape((4, gfull.shape[0] // 4) + gfull.shape[1:])
    big_g = [jnp.stack([by_shard(n, grads[l][n]) for l in range(DEPTH)]) for n in BIG]
    def own(arrs, me):
        return [lax.dynamic_index_in_dim(a, me, 0, keepdims=False) for a in arrs]
    from_cores = _with_own(_exchange(big_g, "c", False, "reduce_cores"), own(big_g, ci), ci)
    chip_sum = [_sum_slots(a, f"sum_cores{i}") for i, a in enumerate(from_cores)]
    from_chips = _with_own(_exchange(chip_sum, "xy", False, "reduce_chips"), own(chip_sum, chip), chip)
    reduced = [_sum_slots(a, f"sum_chips{i}") for i, a in enumerate(from_chips)]
    both = _with_own(_exchange(reduced, "c", True, "swap_layers"), reduced, ci)
    big_grad = dict(zip(BIG, both))

    small_g = {"ln_in_g": g_ln_in_g[0], "ln_in_b": g_ln_in_b[0]}
    for n in SMALL[2:]:
        small_g[n] = jnp.stack([grads[l][n].reshape(wts[n].shape[1:] if n != "conv_w" else (SSD_CONV, CONV_DIM)) for l in range(DEPTH)])
    flat = jnp.concatenate([small_g[n].reshape(-1) for n in SMALL])
    n_small = flat.shape[0]
    width = -(-n_small // 1024) * 1024
    flat = jnp.pad(flat, (0, width - n_small)).reshape(8, width // 8)
    gathered, = _with_own(_exchange([flat], "all", True, "gather_small"), [flat], 2 * chip + ci)
    total = _sum_slots(gathered, "sum_small").reshape(-1)
    small_grad, off = {}, 0
    for n in SMALL:
        shp = small_g[n].shape
        cnt = math.prod(shp)
        small_grad[n] = total[off:off + cnt].reshape(shp)
        off += cnt
    small_grad["conv_w"] = lax.dynamic_slice_in_dim(small_grad["conv_w"], chip * (CONV_DIM // 4), CONV_DIM // 4, axis=2)

    out_g, out_d, out_m, out_v = {}, {}, {}, {}
    for n in BIG:
        shp = wts[n].shape
        two_d = lambda t: t.reshape(shp[0] * shp[1], shp[2])
        d_, m_, v_ = _adamw(two_d(wts[n]), two_d(big_grad[n]), two_d(mom1[n]), two_d(mom2[n]), f"adamw_{n}")
        out_g[n], out_d[n], out_m[n], out_v[n] = big_grad[n], d_.reshape(shp), m_.reshape(shp), v_.reshape(shp)

    def flat_small(d):
        f = jnp.concatenate([d[n].reshape(-1) for n in SMALL])
        return jnp.pad(f, (0, swidth - f.shape[0])).reshape(8, swidth // 8)
    n_sw = sum(math.prod(wts[n].shape) for n in SMALL)
    swidth = -(-n_sw // 1024) * 1024
    d_, m_, v_ = _adamw(flat_small(wts), flat_small(small_grad), flat_small(mom1), flat_small(mom2), "adamw_small")
    off = 0
    for n in SMALL:
        shp = wts[n].shape
        cnt = math.prod(shp)
        out_g[n] = small_grad[n]
        out_d[n] = d_.reshape(-1)[off:off + cnt].reshape(shp)
        out_m[n] = m_.reshape(-1)[off:off + cnt].reshape(shp)
        out_v[n] = v_.reshape(-1)[off:off + cnt].reshape(shp)
        off += cnt

    return (loss, grad_x[None], *[out_g[n] for n in WEIGHTS], *[out_d[n] for n in WEIGHTS],
            *[out_m[n] for n in WEIGHTS], *[out_v[n] for n in WEIGHTS])
```

```python
import functools
import itertools
import math

import jax
import jax.numpy as jnp
from jax import lax
from jax.experimental import pallas as pl
from jax.experimental.pallas import tpu as pltpu

F32 = jnp.float32
BF = jnp.bfloat16

D_MODEL = 1024
DEPTH = 2
ATT_HEADS = 16
ATT_HEAD_DIM = 64
BLK = 128
SSD_D_INNER = 2048
SSD_HEADS = 32
SSD_GROUPS = 4
SSD_STATE = 128
SSD_CONV = 4
BC_DIM = 512
CONV_DIM = 3072
FFN_HIDDEN = 2816
IN_DIM = 8480
LN_EPS = 1e-5
RMS_EPS = 1e-5
ALPHA = (2 * DEPTH) ** 0.25
ADAM_LR = 0.001
ADAM_B1 = 0.9
ADAM_B2 = 0.999
ADAM_EPS = 1e-08
ADAM_WD = 0.01
ADAM_STEP = 10

_PACK = (("gates", 6432, 2048), ("z", 1280, 2048), ("q", 0, 1024), ("xs", 3328, 2048), ("B", 5376, 512),
         ("C", 5888, 512), ("k", 1024, 128), ("v", 1152, 128), ("dt", 6400, 32))
NP = 8704
_OFF = {}
_o = 0
for _n, _, _w in _PACK:
    _OFF[_n] = _o
    _o += _w
_PAD = NP - _o

VMEM_LIMIT_BYTES = 56 * 1024 * 1024
BIG = ("w_in", "w_ssd_out", "w_att_out", "w_mix_out", "w_ffn_gate", "w_ffn_up", "w_ffn_down")
WEIGHTS = ("ln_in_g", "ln_in_b", "w_in", "conv_w", "conv_b", "dt_bias", "a_log", "d_skip", "ssd_norm_w", "att_sinks",
           "w_ssd_out", "w_att_out", "w_mix_out", "ln_mix_g", "ln_mix_b", "w_ffn_gate", "w_ffn_up", "w_ffn_down",
           "ln_ffn_g", "ln_ffn_b")
SMALL = tuple(n for n in WEIGHTS if n not in BIG)


def _params(n_grid):
    return pltpu.CompilerParams(dimension_semantics=("arbitrary",) * n_grid, vmem_limit_bytes=VMEM_LIMIT_BYTES)


def _dot(a, b, ca, cb):
    return lax.dot_general(a.astype(BF), b.astype(BF), (((ca,), (cb,)), ((), ())), preferred_element_type=F32)


@jax.custom_vjp
def _nn(a, b):
    return _dot(a, b, 1, 0)


def _nn_f(a, b):
    return _dot(a, b, 1, 0), (a, b)


def _nn_b(res, g):
    a, b = res
    return _dot(g, b, 1, 1).astype(a.dtype), _dot(a, g, 0, 0).astype(b.dtype)


_nn.defvjp(_nn_f, _nn_b)


@jax.custom_vjp
def _nt(a, b):
    return _dot(a, b, 1, 1)


def _nt_f(a, b):
    return _dot(a, b, 1, 1), (a, b)


def _nt_b(res, g):
    a, b = res
    return _dot(g, b, 1, 0).astype(a.dtype), _dot(g, a, 0, 0).astype(b.dtype)


_nt.defvjp(_nt_f, _nt_b)


@jax.custom_vjp
def _tn(a, b):
    return _dot(a, b, 0, 0)


def _tn_f(a, b):
    return _dot(a, b, 0, 0), (a, b)


def _tn_b(res, g):
    a, b = res
    return _dot(b, g, 1, 1).astype(a.dtype), _dot(a, g, 1, 0).astype(b.dtype)


_tn.defvjp(_tn_f, _tn_b)


def _silu(x):
    return x * jax.nn.sigmoid(x)


def _layer_norm(s, g, b):
    mu = jnp.mean(s, axis=-1, keepdims=True)
    sc = s - mu
    var = jnp.mean(sc * sc, axis=-1, keepdims=True)
    return sc * lax.rsqrt(var + LN_EPS) * g + b


def _ssd_group(pre_x, pre_b, pre_c, z, dtr, dtb, alog, dsk, nw, state):
    t = pre_x.shape[0]
    xs, bm, cm = _silu(pre_x), _silu(pre_b), _silu(pre_c)
    dt = jax.nn.softplus(dtr + dtb)
    da = dt * (-jnp.exp(alog))
    row = lax.broadcasted_iota(jnp.int32, (t, t), 0)
    col = lax.broadcasted_iota(jnp.int32, (t, t), 1)
    upper = (row <= col).astype(F32)
    cum = jnp.dot(da, upper, precision=lax.Precision.HIGHEST, preferred_element_type=F32)
    tot = jnp.sum(da, axis=1, keepdims=True)
    cb = _nt(cm, bm)
    tril = row >= col
    first = lax.broadcasted_iota(jnp.int32, (t, BLK), 1) < 64
    first_row = lax.broadcasted_iota(jnp.int32, (1, BLK), 1) < 64

    def col_form(r):
        return jnp.broadcast_to(r, (t, t)).T

    ys, new_state = [], []
    for p in range(4):
        h0, h1 = 2 * p, 2 * p + 1
        sl = slice(BLK * p, BLK * (p + 1))
        x_pair = xs[:, sl] * jnp.where(first, col_form(dt[h0:h0 + 1]), col_form(dt[h1:h1 + 1]))
        y_pair = None
        cc = []
        for h, keep in ((h0, first), (h1, jnp.logical_not(first))):
            cr = jnp.broadcast_to(cum[h:h + 1], (t, t))
            cc.append(cr.T)
            decay = jnp.exp(jnp.where(tril, cc[-1] - cr, -1e30))
            y_h = _nn(decay * cb, jnp.where(keep, x_pair, 0.0))
            y_pair = y_h if y_pair is None else y_pair + y_h
        s_pair = state[:, sl]
        y_pair = y_pair + _nn(cm, s_pair) * jnp.where(first, jnp.exp(cc[0]), jnp.exp(cc[1]))
        to_end = jnp.where(first, jnp.exp(tot[h0:h0 + 1] - cc[0]), jnp.exp(tot[h1:h1 + 1] - cc[1]))
        chunk_decay = jnp.where(first_row, jnp.exp(tot[h0:h0 + 1]), jnp.exp(tot[h1:h1 + 1]))
        new_state.append(s_pair * chunk_decay + _tn(bm, x_pair * to_end))
        ys.append(y_pair + dsk[:, sl] * xs[:, sl])
    y = jnp.concatenate(ys, axis=1) * _silu(z)
    y = y * lax.rsqrt(jnp.mean(y * y, axis=-1, keepdims=True) + RMS_EPS) * nw
    return y, jnp.concatenate(new_state, axis=1)


def _attn_block(q, kp, kc, vp, vc, sinks, n, kv):
    t = q.shape[0]
    kb = jnp.concatenate([kp, kc], axis=0)
    vb = jnp.concatenate([vp, vc], axis=0)
    qi = lax.broadcasted_iota(jnp.int32, (t, 2 * t), 0)
    kj = lax.broadcasted_iota(jnp.int32, (t, 2 * t), 1)
    rel = qi + t - kj
    valid = (rel >= 0) & (rel < t) & ((n * t - t + kj) >= 0)
    relf = rel.astype(F32)
    first = lax.broadcasted_iota(jnp.int32, (t, BLK), 1) < 64
    lane16 = lax.broadcasted_iota(jnp.int32, (1, ATT_HEADS), 1)
    outs = []
    for p in range(4):
        qp = q[:, BLK * p:BLK * (p + 1)] * (ATT_HEAD_DIM ** -0.5)
        o_pair = None
        for half, keep in enumerate((first, jnp.logical_not(first))):
            h = kv * 8 + 2 * p + half
            s = _nt(jnp.where(keep, qp, 0.0), kb)
            slope = jnp.exp((-8.0 * math.log(2.0) / ATT_HEADS) * (h + 1).astype(F32))
            s = jnp.where(valid, s - slope * relf, -1e30)
            sink = jnp.sum(jnp.where(lane16 == h, sinks, 0.0), axis=1, keepdims=True)
            m = jnp.maximum(jnp.max(s, axis=1, keepdims=True), sink)
            e = jnp.exp(s - m)
            den = jnp.sum(e, axis=1, keepdims=True) + jnp.exp(sink - m)
            o = _nn(e / den, vb)
            o_pair = o if o_pair is None else jnp.where(first, o_pair, o)
        outs.append(o_pair)
    return jnp.concatenate(outs, axis=1)


def _mm(a, b, mode, out_dtype, tm, tn, tk, name):
    if mode == "nn":
        (m, k), (k2, n) = a.shape, b.shape
    elif mode == "nt":
        (m, k), (n, k2) = a.shape, b.shape
    else:
        (k, m), (k2, n) = a.shape, b.shape
    assert k == k2, (a.shape, b.shape, mode)
    tm, tn, tk = min(tm, m), min(tn, n), min(tk, k)
    assert m % tm == 0 and n % tn == 0 and k % tk == 0, (m, n, k, tm, tn, tk)
    nk = k // tk
    ca, cb = {"nn": (1, 0), "nt": (1, 1), "tn": (0, 0)}[mode]

    def body(a_ref, b_ref, o_ref, *acc):
        part = _dot(a_ref[...], b_ref[...], ca, cb)
        if nk == 1:
            o_ref[...] = part.astype(o_ref.dtype)
            return
        acc_ref, = acc
        kk = pl.program_id(2)

        @pl.when(kk == 0)
        def _():
            acc_ref[...] = part

        @pl.when(kk > 0)
        def _():
            acc_ref[...] += part

        @pl.when(kk == nk - 1)
        def _():
            o_ref[...] = acc_ref[...].astype(o_ref.dtype)

    a_spec = pl.BlockSpec((tk, tm), lambda i, j, kk: (kk, i)) if mode == "tn" else pl.BlockSpec((tm, tk), lambda i, j, kk: (i, kk))
    b_spec = pl.BlockSpec((tn, tk), lambda i, j, kk: (j, kk)) if mode == "nt" else pl.BlockSpec((tk, tn), lambda i, j, kk: (kk, j))
    return pl.pallas_call(
        body, name=name, grid=(m // tm, n // tn, nk),
        in_specs=[a_spec, b_spec], out_specs=pl.BlockSpec((tm, tn), lambda i, j, kk: (i, j)),
        out_shape=jax.ShapeDtypeStruct((m, n), out_dtype),
        scratch_shapes=[] if nk == 1 else [pltpu.VMEM((tm, tn), F32)],
        compiler_params=_params(3),
    )(a, b)


def _rows(fn, rows, params, outs, accs, tile, name):
    length = rows[0][0].shape[0]
    tile = min(tile, length)
    assert length % tile == 0
    nr, npar, no = len(rows), len(params), len(outs)

    def body(*refs):
        vals = [r[...] for r in refs[:nr + npar]]
        o, a = fn(*vals)
        for ref, val in zip(refs[nr + npar:nr + npar + no], o):
            ref[...] = val.astype(ref.dtype)
        i = pl.program_id(0)
        for ref, val in zip(refs[nr + npar + no:], a):
            @pl.when(i == 0)
            def _(ref=ref, val=val):
                ref[...] = val

            @pl.when(i > 0)
            def _(ref=ref, val=val):
                ref[...] += val

    in_specs = [pl.BlockSpec((tile, w), functools.partial(lambda i, cb: (i, cb), cb=cb)) for _, cb, w in rows]
    in_specs += [pl.BlockSpec(p.shape, lambda i: (0, 0)) for p in params]
    outs = [o if len(o) == 3 else (o[0], o[1], o[0]) for o in outs]
    out_specs = [pl.BlockSpec((tile, w), lambda i: (i, 0)) for w, _, _ in outs]
    out_specs += [pl.BlockSpec((r, w), lambda i: (0, 0)) for r, w in accs]
    out_shape = [jax.ShapeDtypeStruct((length, full), dt) for _, dt, full in outs]
    out_shape += [jax.ShapeDtypeStruct((r, w), F32) for r, w in accs]
    res = pl.pallas_call(
        body, name=name, grid=(length // tile,), in_specs=in_specs, out_specs=out_specs, out_shape=out_shape,
        compiler_params=_params(1),
    )(*[r[0] for r in rows], *params)
    return res[:no], res[no:]


def _conv_fwd(proj, conv_w, conv_b, name):
    length = proj.shape[0]
    tl, cw = min(512, length), 512
    cb0 = _OFF["xs"] // cw

    def body(u_ref, halo_ref, w_ref, b_ref, o_ref, win):
        i = pl.program_id(1)
        win[0:8, :] = jnp.where(i > 0, halo_ref[...], 0.0)
        win[8:8 + tl, :] = u_ref[...]
        acc = b_ref[...] + w_ref[0:1, :] * win[5:5 + tl, :]
        for kk in range(1, SSD_CONV):
            acc = acc + w_ref[kk:kk + 1, :] * win[5 + kk:5 + kk + tl, :]
        o_ref[...] = acc

    return pl.pallas_call(
        body, name=name, grid=(CONV_DIM // cw, length // tl),
        in_specs=[pl.BlockSpec((tl, cw), lambda j, i: (i, cb0 + j)),
                  pl.BlockSpec((8, cw), lambda j, i: (jnp.maximum(i * (tl // 8) - 1, 0), cb0 + j)),
                  pl.BlockSpec((SSD_CONV, cw), lambda j, i: (0, j)),
                  pl.BlockSpec((1, cw), lambda j, i: (0, j))],
        out_specs=pl.BlockSpec((tl, cw), lambda j, i: (i, j)),
        out_shape=jax.ShapeDtypeStruct((length, CONV_DIM), F32),
        scratch_shapes=[pltpu.VMEM((8 + tl, cw), F32)],
        compiler_params=_params(2),
    )(proj, proj, conv_w, conv_b)


def _conv_bwd(dpre, proj, conv_w, dproj, part, name):
    length = proj.shape[0]
    width = dpre.shape[1]
    tl, cw = min(512, length), 512
    cb0 = _OFF["xs"] // cw + part
    n_t = length // tl

    def body(d_ref, dnext_ref, u_ref, halo_ref, w_ref, _, du_ref, dw_ref, db_ref, dwin, uwin):
        i = pl.program_id(1)
        d = d_ref[...]
        dwin[0:tl, :] = d
        dwin[tl:tl + 8, :] = jnp.where(i < n_t - 1, dnext_ref[...], 0.0)
        uwin[0:8, :] = jnp.where(i > 0, halo_ref[...], 0.0)
        uwin[8:8 + tl, :] = u_ref[...]
        du = w_ref[3:4, :] * d
        for kk in range(SSD_CONV - 1):
            du = du + w_ref[kk:kk + 1, :] * dwin[3 - kk:3 - kk + tl, :]
        du_ref[...] = du.astype(du_ref.dtype)
        dws = [jnp.sum(d * uwin[5 + kk:5 + kk + tl, :], axis=0, keepdims=True) for kk in range(SSD_CONV)]
        dw = jnp.concatenate(dws, axis=0)
        db = jnp.sum(d, axis=0, keepdims=True)

        @pl.when(i == 0)
        def _():
            dw_ref[...] = dw
            db_ref[...] = db

        @pl.when(i > 0)
        def _():
            dw_ref[...] += dw
            db_ref[...] += db

    return pl.pallas_call(
        body, name=name, grid=(width // cw, n_t), input_output_aliases={5: 0},
        in_specs=[pl.BlockSpec((tl, cw), lambda j, i: (i, j)),
                  pl.BlockSpec((8, cw), lambda j, i: (jnp.minimum((i + 1) * (tl // 8), length // 8 - 1), j)),
                  pl.BlockSpec((tl, cw), lambda j, i: (i, cb0 + j)),
                  pl.BlockSpec((8, cw), lambda j, i: (jnp.maximum(i * (tl // 8) - 1, 0), cb0 + j)),
                  pl.BlockSpec((SSD_CONV, cw), lambda j, i: (0, part + j)),
                  pl.BlockSpec(memory_space=pl.ANY)],
        out_specs=[pl.BlockSpec((tl, cw), lambda j, i: (i, cb0 + j)),
                   pl.BlockSpec((SSD_CONV, cw), lambda j, i: (0, j)),
                   pl.BlockSpec((1, cw), lambda j, i: (0, j))],
        out_shape=[jax.ShapeDtypeStruct((length, NP), BF),
                   jax.ShapeDtypeStruct((SSD_CONV, width), F32),
                   jax.ShapeDtypeStruct((1, width), F32)],
        scratch_shapes=[pltpu.VMEM((tl + 8, cw), F32), pltpu.VMEM((8 + tl, cw), F32)],
        compiler_params=_params(2),
    )(dpre, dpre, proj, proj, conv_w, dproj)


def _ssd_in_specs(rev, nc):
    def cidx(c):
        return nc - 1 - c if rev else c
    return [pl.BlockSpec((BLK, 512), lambda g, c: (cidx(c), g)),
            pl.BlockSpec((BLK, BLK), lambda g, c: (cidx(c), 16 + g)),
            pl.BlockSpec((BLK, BLK), lambda g, c: (cidx(c), 20 + g)),
            pl.BlockSpec((BLK, 512), lambda g, c: (cidx(c), ZB + g)),
            pl.BlockSpec((8, BLK), lambda g, c: (g, cidx(c))),
            pl.BlockSpec((8, 1), lambda g, c: (g, 0)),
            pl.BlockSpec((8, 1), lambda g, c: (g, 0)),
            pl.BlockSpec((1, 512), lambda g, c: (0, g)),
            pl.BlockSpec((1, 512), lambda g, c: (0, g))]


def _ssd_fwd(pre, proj, dt_t, dtb, alog, dsk, nw, name):
    length = pre.shape[0]
    nc = length // BLK

    def body(px, pb, pc, z, dtr, dtb_r, al_r, dsk_r, nw_r, y_ref, sin_ref, st):
        @pl.when(pl.program_id(1) == 0)
        def _():
            st[...] = jnp.zeros_like(st)

        s_in = st[...]
        sin_ref[...] = s_in
        y, s_out = _ssd_group(px[...], pb[...], pc[...], z[...], dtr[...], dtb_r[...], al_r[...], dsk_r[...], nw_r[...], s_in)
        y_ref[...] = y.astype(y_ref.dtype)
        st[...] = s_out

    return pl.pallas_call(
        body, name=name, grid=(SSD_GROUPS, nc), in_specs=_ssd_in_specs(False, nc),
        out_specs=[pl.BlockSpec((BLK, 512), lambda g, c: (c, g)),
                   pl.BlockSpec((None, None, SSD_STATE, 512), lambda g, c: (g, c, 0, 0))],
        out_shape=[jax.ShapeDtypeStruct((length, SSD_D_INNER), BF),
                   jax.ShapeDtypeStruct((SSD_GROUPS, nc, SSD_STATE, 512), F32)],
        scratch_shapes=[pltpu.VMEM((SSD_STATE, 512), F32)],
        compiler_params=_params(2),
    )(pre, pre, pre, proj, dt_t, dtb, alog, dsk, nw)


def _ssd_bwd(pre, proj, dt_t, dtb, alog, dsk, nw, s_in, dy, dproj, name):
    length = pre.shape[0]
    nc = length // BLK

    def body(px, pb, pc, z, dtr, dtb_r, al_r, dsk_r, nw_r, sin_r, dy_r, _,
             dpx, dpb, dpc, dz, ddt, ddtb, dal, ddsk, dnw, dst):
        c = pl.program_id(1)

        @pl.when(c == 0)
        def _():
            dst[...] = jnp.zeros_like(dst)

        _, vjp = jax.vjp(_ssd_group, px[...], pb[...], pc[...], z[...], dtr[...], dtb_r[...], al_r[...], dsk_r[...],
                         nw_r[...], sin_r[...])
        g = vjp((dy_r[...], dst[...]))
        dpx[...] = g[0]
        dpb[...] = g[1]
        dpc[...] = g[2]
        dz[...] = g[3].astype(dz.dtype)
        ddt[...] = g[4]
        dst[...] = g[9]
        for ref, val in ((ddtb, g[5]), (dal, g[6]), (ddsk, g[7]), (dnw, g[8])):
            @pl.when(c == 0)
            def _(ref=ref, val=val):
                ref[...] = val

            @pl.when(c > 0)
            def _(ref=ref, val=val):
                ref[...] += val

    in_specs = _ssd_in_specs(True, nc) + [
        pl.BlockSpec((None, None, SSD_STATE, 512), lambda g, c: (g, nc - 1 - c, 0, 0)),
        pl.BlockSpec((BLK, 512), lambda g, c: (nc - 1 - c, g)),
        pl.BlockSpec(memory_space=pl.ANY)]
    return pl.pallas_call(
        body, name=name, grid=(SSD_GROUPS, nc), in_specs=in_specs, input_output_aliases={11: 3},
        out_specs=[pl.BlockSpec((BLK, 512), lambda g, c: (nc - 1 - c, g)),
                   pl.BlockSpec((BLK, BLK), lambda g, c: (nc - 1 - c, g)),
                   pl.BlockSpec((BLK, BLK), lambda g, c: (nc - 1 - c, g)),
                   pl.BlockSpec((BLK, 512), lambda g, c: (nc - 1 - c, ZB + g)),
                   pl.BlockSpec((8, BLK), lambda g, c: (g, nc - 1 - c)),
                   pl.BlockSpec((8, 1), lambda g, c: (g, 0)),
                   pl.BlockSpec((8, 1), lambda g, c: (g, 0)),
                   pl.BlockSpec((1, 512), lambda g, c: (0, g)),
                   pl.BlockSpec((1, 512), lambda g, c: (0, g))],
        out_shape=[jax.ShapeDtypeStruct((length, SSD_D_INNER), F32),
                   jax.ShapeDtypeStruct((length, BC_DIM), F32),
                   jax.ShapeDtypeStruct((length, BC_DIM), F32),
                   jax.ShapeDtypeStruct((length, NP), BF),
                   jax.ShapeDtypeStruct((SSD_HEADS, length), F32),
                   jax.ShapeDtypeStruct((SSD_HEADS, 1), F32),
                   jax.ShapeDtypeStruct((SSD_HEADS, 1), F32),
                   jax.ShapeDtypeStruct((1, SSD_D_INNER), F32),
                   jax.ShapeDtypeStruct((1, SSD_D_INNER), F32)],
        scratch_shapes=[pltpu.VMEM((SSD_STATE, 512), F32)],
        compiler_params=_params(2),
    )(pre, pre, pre, proj, dt_t, dtb, alog, dsk, nw, s_in, dy, dproj)


QB = _OFF["q"] // 512
ZB = _OFF["z"] // 512
KVB = _OFF["k"] // 512
assert _OFF["gates"] == 0 and _OFF["k"] % 512 == 0 and _OFF["k"] + 512 == NP


def _attn_in_specs():
    return [pl.BlockSpec((BLK, 512), lambda kv, n: (n, QB + kv)),
            pl.BlockSpec((None, BLK, BLK), lambda kv, n: (kv, jnp.maximum(n - 1, 0), 0)),
            pl.BlockSpec((None, BLK, BLK), lambda kv, n: (kv, n, 0)),
            pl.BlockSpec((None, BLK, BLK), lambda kv, n: (kv, jnp.maximum(n - 1, 0), 0)),
            pl.BlockSpec((None, BLK, BLK), lambda kv, n: (kv, n, 0)),
            pl.BlockSpec((1, ATT_HEADS), lambda kv, n: (0, 0))]


def _attn_fwd(proj, k2, v2, sinks, name):
    length = proj.shape[0]

    def body(q, kp, kc, vp, vc, sk, o_ref):
        o = _attn_block(q[...], kp[...], kc[...], vp[...], vc[...], sk[...], pl.program_id(1), pl.program_id(0))
        o_ref[...] = o.astype(o_ref.dtype)

    return pl.pallas_call(
        body, name=name, grid=(2, length // BLK), in_specs=_attn_in_specs(),
        out_specs=pl.BlockSpec((BLK, 512), lambda kv, n: (n, kv)),
        out_shape=jax.ShapeDtypeStruct((length, D_MODEL), BF),
        compiler_params=_params(2),
    )(proj, k2, k2, v2, v2, sinks)


def _attn_bwd(proj, k2, v2, sinks, datt, dproj, name):
    length = proj.shape[0]

    def body(q, kp, kc, vp, vc, sk, do, _, dq, dkp, dkc, dvp, dvc, dsk):
        n, kv = pl.program_id(1), pl.program_id(0)
        _, vjp = jax.vjp(lambda *a: _attn_block(*a, n, kv), q[...], kp[...], kc[...], vp[...], vc[...], sk[...])
        g = vjp(do[...])
        dq[...] = g[0].astype(dq.dtype)
        dkp[...] = g[1]
        dkc[...] = g[2]
        dvp[...] = g[3]
        dvc[...] = g[4]
        is_first = (n == 0) & (kv == 0)

        @pl.when(is_first)
        def _():
            dsk[...] = g[5]

        @pl.when(jnp.logical_not(is_first))
        def _():
            dsk[...] += g[5]

    blk3 = pl.BlockSpec((None, BLK, BLK), lambda kv, n: (kv, n, 0))
    kv_shape = jax.ShapeDtypeStruct((2, length, BLK), F32)
    return pl.pallas_call(
        body, name=name, grid=(2, length // BLK),
        in_specs=_attn_in_specs() + [pl.BlockSpec((BLK, 512), lambda kv, n: (n, kv)), pl.BlockSpec(memory_space=pl.ANY)],
        out_specs=[pl.BlockSpec((BLK, 512), lambda kv, n: (n, QB + kv)), blk3, blk3, blk3, blk3,
                   pl.BlockSpec((1, ATT_HEADS), lambda kv, n: (0, 0))],
        out_shape=[jax.ShapeDtypeStruct((length, NP), BF), kv_shape, kv_shape, kv_shape, kv_shape,
                   jax.ShapeDtypeStruct((1, ATT_HEADS), F32)],
        input_output_aliases={7: 0},
        compiler_params=_params(2),
    )(proj, k2, k2, v2, v2, sinks, datt, dproj)


DMA_CHUNK_BYTES = 1 << 20
N_STAGE = 8
LOOKAHEAD = 3


def _piece_chunks(shape, itemsize):
    if len(shape) < 2 or shape[-2] % 16 != 0:
        return [()]
    rows, cols = shape[-2:]
    step = min(rows, max(16, DMA_CHUNK_BYTES // (cols * itemsize) // 16 * 16))
    out = []
    for lead in itertools.product(*[range(d) for d in shape[:-2]]):
        for r0 in range(0, rows, step):
            out.append(lead + (pl.ds(r0, min(step, rows - r0)),))
    return out


def _exchange(srcs, group, bcast, name):
    size = {"c": 2, "xy": 4, "all": 8}[group]
    npeer = size - 1
    n = len(srcs)
    for s in srcs:
        assert bcast or s.shape[0] == size
    pieces = [s.shape if bcast else s.shape[1:] for s in srcs]
    chunks = [_piece_chunks(p, s.dtype.itemsize) for p, s in zip(pieces, srcs)]

    def body(*refs):
        src, out = refs[:n], refs[n:2 * n]
        load_sems, send_sems, recv_sems = refs[2 * n:]
        x, y, c = lax.axis_index("x"), lax.axis_index("y"), lax.axis_index("c")
        if group == "c":
            me = c
        elif group == "xy":
            me = 2 * x + y
        else:
            me = 4 * x + 2 * y + c

        def device(j):
            if group == "c":
                return (x, y, j)
            if group == "xy":
                return (j // 2, j % 2, c)
            return (j // 4, (j // 2) % 2, j % 2)

        def part(ref, idx):
            return ref.at[idx] if idx else ref

        def piece(a, j):
            return src[a] if bcast else src[a].at[j]

        for a in range(n):
            jobs = [(None, idx) for idx in chunks[a]] if bcast else [(d, idx) for idx in chunks[a] for d in range(1, size)]
            full_rows = chunks[a][0][-1].size if chunks[a][0] else None
            slot_shape = (full_rows, pieces[a][-1]) if chunks[a][0] else pieces[a]

            def stream(stage, a=a, jobs=jobs, full_rows=full_rows):
                def slot(q):
                    idx = jobs[q][1]
                    view = stage.at[q % N_STAGE]
                    if idx and idx[-1].size != full_rows:
                        view = view.at[pl.ds(0, idx[-1].size)]
                    return view

                def load(q):
                    d, idx = jobs[q]
                    j = me if d is None else jnp.bitwise_xor(me, d)
                    return pltpu.make_async_copy(part(piece(a, j), idx), slot(q), load_sems.at[q % N_STAGE])

                def sends(q):
                    d, idx = jobs[q]
                    return [pltpu.make_async_remote_copy(
                        src_ref=slot(q), dst_ref=part(out[a].at[me], idx),
                        send_sem=send_sems.at[(q % N_STAGE) * npeer + dd - 1], recv_sem=recv_sems.at[a * size + dd],
                        device_id=device(jnp.bitwise_xor(me, dd)), device_id_type=pl.DeviceIdType.MESH)
                        for dd in (range(1, size) if d is None else (d,))]

                nq = len(jobs)
                for q in range(nq + LOOKAHEAD):
                    if q < nq:
                        if q >= N_STAGE:
                            for cp in sends(q - N_STAGE):
                                cp.wait_send()
                        load(q).start()
                    if q >= LOOKAHEAD:
                        load(q - LOOKAHEAD).wait()
                        for cp in sends(q - LOOKAHEAD):
                            cp.start()
                for q in range(max(0, nq - N_STAGE), nq):
                    for cp in sends(q):
                        cp.wait_send()

            pl.run_scoped(stream, pltpu.VMEM((N_STAGE,) + tuple(slot_shape), srcs[a].dtype))

        for a in range(n):
            for d in range(1, size):
                j = jnp.bitwise_xor(me, d)
                pltpu.make_async_remote_copy(
                    src_ref=piece(a, j), dst_ref=out[a].at[j], send_sem=send_sems.at[0], recv_sem=recv_sems.at[a * size + d],
                    device_id=device(j), device_id_type=pl.DeviceIdType.MESH).wait_recv()

    any_spec = pl.BlockSpec(memory_space=pl.ANY)
    return pl.pallas_call(
        body, name=name, in_specs=[any_spec] * n, out_specs=[any_spec] * n,
        out_shape=[jax.ShapeDtypeStruct((size,) + tuple(p), s.dtype) for p, s in zip(pieces, srcs)],
        scratch_shapes=[pltpu.SemaphoreType.DMA((N_STAGE,)), pltpu.SemaphoreType.DMA((N_STAGE * npeer,)),
                        pltpu.SemaphoreType.DMA((n * size,))],
        compiler_params=pltpu.CompilerParams(vmem_limit_bytes=VMEM_LIMIT_BYTES),
    )(*srcs)


def _with_own(outs, owns, me):
    return [lax.dynamic_update_index_in_dim(o, w, me, 0) for o, w in zip(outs, owns)]


def _sum_slots(arr, name):
    k = arr.shape[0]
    rest = arr.shape[1:]
    width = rest[-1]
    rows_per = math.prod(rest[:-1])
    a2 = arr.reshape(k * rows_per, width)
    tile = rows_per
    for cand in (256, 128, 64, 32, 16, 8):
        if rows_per % cand == 0:
            tile = cand
            break
    nt = rows_per // tile

    def body(*refs):
        acc = refs[0][...]
        for r in refs[1:k]:
            acc = acc + r[...]
        refs[k][...] = acc

    in_specs = [pl.BlockSpec((tile, width), functools.partial(lambda i, s: (s * nt + i, 0), s=s)) for s in range(k)]
    out = pl.pallas_call(
        body, name=name, grid=(nt,), in_specs=in_specs, out_specs=pl.BlockSpec((tile, width), lambda i: (i, 0)),
        out_shape=jax.ShapeDtypeStruct((rows_per, width), arr.dtype), compiler_params=_params(1),
    )(*([a2] * k))
    return out.reshape(rest)


def _sum_pieces(recv, own, own_slotted, me, name):
    k = recv.shape[0]
    rest = recv.shape[1:]
    width = rest[-1]
    rows_per = math.prod(rest[:-1])
    r2 = recv.reshape(k * rows_per, width)
    o2 = own.reshape(-1, width)
    tile = rows_per
    for cand in (256, 128, 64, 32, 16, 8):
        if rows_per % cand == 0:
            tile = cand
            break
    nt = rows_per // tile

    def body(me_ref, *refs):
        mine = refs[k][...].astype(F32)
        acc = None
        for j in range(k):
            term = jnp.where(me_ref[0] == j, mine, refs[j][...].astype(F32))
            acc = term if acc is None else acc + term
        refs[k + 1][...] = acc

    in_specs = [pl.BlockSpec((tile, width), functools.partial(lambda i, m, s: (s * nt + i, 0), s=s)) for s in range(k)]
    in_specs.append(pl.BlockSpec((tile, width), (lambda i, m: (m[0] * nt + i, 0)) if own_slotted else (lambda i, m: (i, 0))))
    out = pl.pallas_call(
        body, name=name,
        grid_spec=pltpu.PrefetchScalarGridSpec(
            num_scalar_prefetch=1, grid=(nt,), in_specs=in_specs,
            out_specs=pl.BlockSpec((tile, width), lambda i, m: (i, 0))),
        out_shape=jax.ShapeDtypeStruct((rows_per, width), F32), compiler_params=_params(1),
    )(jnp.reshape(me, (1,)).astype(jnp.int32), *([r2] * k), o2)
    return out.reshape(rest)


def _adamw(w, g, m, v, name):
    def fn(w_, g_, m_, v_):
        m1 = ADAM_B1 * m_ + (1.0 - ADAM_B1) * g_
        v1 = ADAM_B2 * v_ + (1.0 - ADAM_B2) * (g_ * g_)
        m_hat = m1 / (1.0 - ADAM_B1 ** ADAM_STEP)
        v_hat = v1 / (1.0 - ADAM_B2 ** ADAM_STEP)
        delta = -ADAM_LR * (m_hat / (jnp.sqrt(v_hat) + ADAM_EPS) + ADAM_WD * w_)
        return (delta, m1, v1), ()

    rows, width = w.shape
    tile = rows
    for cand in (256, 128, 64, 32, 16, 8):
        if rows % cand == 0:
            tile = cand
            break
    o, _ = _rows(fn, [(w, 0, width), (g, 0, width), (m, 0, width), (v, 0, width)], [],
                 [(width, F32)] * 3, [], tile, name)
    return o


def _pack_cols(w):
    parts = [w[..., o:o + wd] for _, o, wd in _PACK]
    parts.append(jnp.zeros(w.shape[:-1] + (_PAD,), w.dtype))
    return jnp.concatenate(parts, axis=-1)


def _unpack_cols(wp):
    order = sorted(_PACK, key=lambda e: e[1])
    return jnp.concatenate([wp[..., _OFF[nm]:_OFF[nm] + wd] for nm, _, wd in order], axis=-1)


def _col_sharded_full(g):
    return jnp.transpose(g, (1, 0, 2)).reshape(g.shape[1], -1)


def _col_sharded_split(full):
    r, cdim = full.shape
    return jnp.transpose(full.reshape(r, 4, cdim // 4), (1, 0, 2))


def _double_heads(t):
    length = t.shape[0]
    h = jnp.transpose(t.reshape(length, 2, 64), (1, 0, 2))
    return jnp.concatenate([h, h], axis=-1)


def _fold_heads(d_cur, d_prev):
    length = d_cur.shape[1]
    d = d_cur + jnp.concatenate([d_prev[:, BLK:], jnp.zeros((2, BLK, BLK), F32)], axis=1)
    d = d[..., :64] + d[..., 64:]
    return jnp.transpose(d, (1, 0, 2)).reshape(length, 128)


def _ln_fwd(a, r, g, b, name):
    def fn(a_, r_, g_, b_):
        y = _layer_norm(ALPHA * a_ + r_, g_, b_)
        return (y, y), ()
    o, _ = _rows(fn, [(a, 0, D_MODEL), (r, 0, D_MODEL)], [g, b], [(D_MODEL, F32), (D_MODEL, BF)], [], 256, name)
    return o


def _ln_bwd(a, r, d1, d2, g, b, name):
    def fn(a_, r_, d1_, d2_, g_, b_):
        _, vjp = jax.vjp(_layer_norm, ALPHA * a_ + r_, g_, b_)
        ds, dg, db = vjp(ALPHA * d1_ + d2_)
        return (ds, ds), (dg, db)
    o, acc = _rows(fn, [(a, 0, D_MODEL), (r, 0, D_MODEL), (d1, 0, D_MODEL), (d2, 0, D_MODEL)], [g, b],
                   [(D_MODEL, F32), (D_MODEL, BF)], [(1, D_MODEL), (1, D_MODEL)], 256, name)
    return o[0], o[1], acc[0], acc[1]


def _layer_fwd(l, h, hb, w, p):
    sv = {"h": h, "hb": hb}
    proj = _mm(hb, w["win"], "nn", F32, 2048, 512, 1024, f"proj{l}")
    sv["proj"] = proj
    pre = _conv_fwd(proj, p["conv_w"], p["conv_b"], f"conv{l}")
    sv["pre"] = pre
    dt_t = jnp.transpose(proj[:, _OFF["dt"]:_OFF["dt"] + SSD_HEADS])
    sv["dt_t"] = dt_t
    yn, s_in = _ssd_fwd(pre, proj, dt_t, p["dt_bias"], p["a_log"], p["d_skip_c"], p["ssd_norm_w"], f"ssd{l}")
    sv["yn"], sv["s_in"] = yn, s_in
    ya = _mm(yn, w["wso"], "nn", F32, 1024, 1024, 2048, f"ssdout{l}")
    k2 = _double_heads(proj[:, _OFF["k"]:_OFF["k"] + 128])
    v2 = _double_heads(proj[:, _OFF["v"]:_OFF["v"] + 128])
    sv["k2"], sv["v2"] = k2, v2
    att = _attn_fwd(proj, k2, v2, p["att_sinks"], f"attn{l}")
    sv["att"] = att
    yb = _mm(att, w["wao"], "nn", F32, 1024, 1024, 1024, f"attout{l}")
    sv["ya"], sv["yb"] = ya, yb
    gcb = _OFF["gates"] // 1024

    def gate_fn(ga, gb, ya_, yb_):
        return (jax.nn.sigmoid(ga) * ya_ + jax.nn.sigmoid(gb) * yb_,), ()
    (ub,), _ = _rows(gate_fn, [(proj, gcb, 1024), (proj, gcb + 1, 1024), (ya, 0, 1024), (yb, 0, 1024)], [],
                     [(D_MODEL, BF)], [], 256, f"gate{l}")
    sv["ub"] = ub
    mix = _mm(ub, w["wmo"], "nn", F32, 1024, 1024, 1024, f"mixout{l}")
    sv["mix"] = mix
    h1, h1b = _ln_fwd(h, mix, p["ln_mix_g"], p["ln_mix_b"], f"lnmix{l}")
    sv["h1"], sv["h1b"] = h1, h1b
    gu = _mm(h1b, w["wgu"], "nn", F32, 2048, 512, 1024, f"ffnin{l}")
    sv["gu"] = gu

    def act_fn(g_, u_):
        return (_silu(g_) * u_,), ()
    (act,), _ = _rows(act_fn, [(gu, 0, FFN_HIDDEN), (gu, 1, FFN_HIDDEN)], [], [(FFN_HIDDEN, BF)], [], 256, f"swiglu{l}")
    sv["act"] = act
    ffn = _mm(act, w["wd"], "nn", F32, 1024, 1024, FFN_HIDDEN, f"ffnout{l}")
    sv["ffn"] = ffn
    return sv


def _layer_bwd(l, sv, w, p, ds_f, ds_b):
    g = {}
    g["w_ffn_down"] = _mm(sv["act"], ds_b, "tn", F32, 1408, 1024, 1024, f"dwd{l}")
    dact = _mm(ds_b, w["wd"], "nt", F32, 1024, FFN_HIDDEN, 1024, f"dact{l}")

    def act_bwd(g_, u_, d_):
        _, vjp = jax.vjp(lambda a, b: _silu(a) * b, g_, u_)
        dg_, du_ = vjp(d_)
        return (jnp.concatenate([dg_, du_], axis=1),), ()
    (dgu,), _ = _rows(act_bwd, [(sv["gu"], 0, FFN_HIDDEN), (sv["gu"], 1, FFN_HIDDEN), (dact, 0, FFN_HIDDEN)], [],
                      [(2 * FFN_HIDDEN, BF)], [], 128, f"dswiglu{l}")
    dwgu = _mm(sv["h1b"], dgu, "tn", F32, 1024, 2816, 512, f"dwgu{l}")
    g["w_ffn_gate"], g["w_ffn_up"] = dwgu[:, :FFN_HIDDEN], dwgu[:, FFN_HIDDEN:]
    dh1 = _mm(dgu, w["wgu"], "nt", F32, 1024, 1024, 2816, f"dh1{l}")
    ds2_f, ds2_b, g["ln_mix_g"], g["ln_mix_b"] = _ln_bwd(sv["h"], sv["mix"], ds_f, dh1, p["ln_mix_g"], p["ln_mix_b"], f"dlnmix{l}")
    g["w_mix_out"] = _mm(sv["ub"], ds2_b, "tn", F32, 1024, 1024, 512, f"dwmo{l}")
    du = _mm(ds2_b, w["wmo"], "nt", F32, 1024, 1024, 1024, f"du{l}")
    proj = sv["proj"]
    gcb = _OFF["gates"] // 1024

    def gate_bwd(ga, gb, ya_, yb_, du_):
        _, vjp = jax.vjp(lambda a, b, c, d: jax.nn.sigmoid(a) * c + jax.nn.sigmoid(b) * d, ga, gb, ya_, yb_)
        dga, dgb, dya, dyb = vjp(du_)
        return (jnp.concatenate([dga, dgb], axis=1), dya, dyb), ()
    (dproj, dya, dyb), _ = _rows(
        gate_bwd, [(proj, gcb, 1024), (proj, gcb + 1, 1024), (sv["ya"], 0, 1024), (sv["yb"], 0, 1024), (du, 0, 1024)], [],
        [(2048, BF, NP), (D_MODEL, BF), (D_MODEL, BF)], [], 256, f"dgate{l}")
    g["w_att_out"] = _mm(sv["att"], dyb, "tn", F32, 1024, 1024, 512, f"dwao{l}")
    datt = _mm(dyb, w["wao"], "nt", F32, 1024, 1024, 1024, f"datt{l}")
    g["w_ssd_out"] = _mm(sv["yn"], dya, "tn", F32, 1024, 1024, 512, f"dwso{l}")
    dyn = _mm(dya, w["wso"], "nt", F32, 1024, 1024, 1024, f"dyn{l}")
    dproj, dkp, dkc, dvp, dvc, g["att_sinks"] = _attn_bwd(proj, sv["k2"], sv["v2"], p["att_sinks"], datt, dproj, f"dattn{l}")
    (dpx, dpb, dpc, dproj, ddt_t, g["dt_bias"], g["a_log"], ddsk, g["ssd_norm_w"]) = _ssd_bwd(
        sv["pre"], proj, sv["dt_t"], p["dt_bias"], p["a_log"], p["d_skip_c"], p["ssd_norm_w"], sv["s_in"], dyn, dproj, f"dssd{l}")
    g["d_skip"] = jnp.sum(ddsk.reshape(SSD_HEADS, 64), axis=1)
    dws, dbs = [], []
    for nm, dpart, part in (("x", dpx, 0), ("b", dpb, 4), ("c", dpc, 5)):
        dproj, dw_, db_ = _conv_bwd(dpart, proj, p["conv_w"], dproj, part, f"dconv{nm}{l}")
        dws.append(dw_)
        dbs.append(db_)
    g["conv_w"], g["conv_b"] = jnp.concatenate(dws, axis=1), jnp.concatenate(dbs, axis=1)
    length = proj.shape[0]
    tail = jnp.concatenate([_fold_heads(dkc, dkp).astype(BF), _fold_heads(dvc, dvp).astype(BF),
                            jnp.transpose(ddt_t).astype(BF), jnp.zeros((length, _PAD), BF)], axis=1)
    dproj = lax.dynamic_update_slice(dproj, tail, (0, _OFF["k"]))
    g["w_in"] = _unpack_cols(_mm(sv["hb"], dproj, "tn", F32, 1024, 2176, 1024, f"dwin{l}"))
    dh = _mm(dproj, w["win"], "nt", F32, 1024, 1024, 2176, f"dh{l}")
    return g, ds2_f, dh


def kernel(x, ln_in_g, ln_in_b, w_in, conv_w, conv_b, dt_bias, a_log, d_skip, ssd_norm_w, att_sinks, w_ssd_out, w_att_out, w_mix_out, ln_mix_g, ln_mix_b, w_ffn_gate, w_ffn_up, w_ffn_down, ln_ffn_g, ln_ffn_b, loss_target, m_ln_in_g, m_ln_in_b, m_w_in, m_conv_w, m_conv_b, m_dt_bias, m_a_log, m_d_skip, m_ssd_norm_w, m_att_sinks, m_w_ssd_out, m_w_att_out, m_w_mix_out, m_ln_mix_g, m_ln_mix_b, m_w_ffn_gate, m_w_ffn_up, m_w_ffn_down, m_ln_ffn_g, m_ln_ffn_b, v_ln_in_g, v_ln_in_b, v_w_in, v_conv_w, v_conv_b, v_dt_bias, v_a_log, v_d_skip, v_ssd_norm_w, v_att_sinks, v_w_ssd_out, v_w_att_out, v_w_mix_out, v_ln_mix_g, v_ln_mix_b, v_w_ffn_gate, v_w_ffn_up, v_w_ffn_down, v_ln_ffn_g, v_ln_ffn_b):
    env = dict(locals())
    wts = {n: env[n] for n in WEIGHTS}
    mom1 = {n: env["m_" + n] for n in WEIGHTS}
    mom2 = {n: env["v_" + n] for n in WEIGHTS}
    ci = lax.axis_index("c")
    chip = 2 * lax.axis_index("x") + lax.axis_index("y")
    xs_ = x[0]
    tgt = loss_target[0]

    shards = [wts[n].astype(BF) for n in BIG] + [conv_w]
    mine = [lax.dynamic_index_in_dim(s, ci, 0, keepdims=False) for s in shards]
    by_chip = _with_own(_exchange(mine, "xy", True, "gather_chips"), mine, chip)
    by_layer = _with_own(_exchange(by_chip, "c", True, "gather_cores"), by_chip, ci)
    full = dict(zip(BIG + ("conv_w",), by_layer))

    layer_w, layer_p = [], []
    for l in range(DEPTH):
        layer_w.append({
            "win": _pack_cols(_col_sharded_full(full["w_in"][l])),
            "wso": full["w_ssd_out"][l].reshape(SSD_D_INNER, D_MODEL),
            "wao": full["w_att_out"][l].reshape(D_MODEL, D_MODEL),
            "wmo": full["w_mix_out"][l].reshape(D_MODEL, D_MODEL),
            "wgu": jnp.concatenate([_col_sharded_full(full["w_ffn_gate"][l]), _col_sharded_full(full["w_ffn_up"][l])], axis=1),
            "wd": full["w_ffn_down"][l].reshape(FFN_HIDDEN, D_MODEL),
        })
        layer_p.append({
            "conv_w": _col_sharded_full(full["conv_w"][l]), "conv_b": conv_b[l][None],
            "dt_bias": dt_bias[l][:, None], "a_log": a_log[l][:, None],
            "d_skip_c": jnp.repeat(d_skip[l], 64)[None], "ssd_norm_w": ssd_norm_w[l][None],
            "att_sinks": att_sinks[l][None], "ln_mix_g": ln_mix_g[l][None], "ln_mix_b": ln_mix_b[l][None],
            "ln_ffn_g": ln_ffn_g[l][None], "ln_ffn_b": ln_ffn_b[l][None],
        })

    def ln_in_fn(x_, g_, b_):
        y = _layer_norm(x_, g_, b_)
        return (y, y), ()
    (h, hb), _ = _rows(ln_in_fn, [(xs_, 0, D_MODEL)], [ln_in_g[None], ln_in_b[None]], [(D_MODEL, F32), (D_MODEL, BF)], [], 256, "ln_in")
    saved = []
    for l in range(DEPTH):
        sv = _layer_fwd(l, h, hb, layer_w[l], layer_p[l])
        saved.append(sv)
        if l < DEPTH - 1:
            h, hb = _ln_fwd(sv["h1"], sv["ffn"], layer_p[l]["ln_ffn_g"], layer_p[l]["ln_ffn_b"], f"lnffn{l}")

    def loss_fn(a_, r_, t_, g_, b_):
        y, vjp = jax.vjp(_layer_norm, ALPHA * a_ + r_, g_, b_)
        err = y - t_
        ds, dg, db = vjp(err * (1.0 / D_MODEL))
        part = 0.5 * jnp.sum(jnp.mean(err * err, axis=-1, keepdims=True), axis=0, keepdims=True)
        return (ds, ds), (dg, db, jnp.broadcast_to(part, (1, BLK)))
    sv = saved[-1]
    (ds_f, ds_b), (dg_last, db_last, loss_part) = _rows(
        loss_fn, [(sv["h1"], 0, D_MODEL), (sv["ffn"], 0, D_MODEL), (tgt, 0, D_MODEL)],
        [layer_p[-1]["ln_ffn_g"], layer_p[-1]["ln_ffn_b"]], [(D_MODEL, F32), (D_MODEL, BF)],
        [(1, D_MODEL), (1, D_MODEL), (1, BLK)], 256, "loss")
    loss = lax.psum(loss_part[0, 0], ("x", "y", "c"))

    grads = [None] * DEPTH
    for l in reversed(range(DEPTH)):
        sv = saved[l]
        if l == DEPTH - 1:
            lg, lb = dg_last, db_last
        else:
            ds_f, ds_b, lg, lb = _ln_bwd(sv["h1"], sv["ffn"], d1, d2, layer_p[l]["ln_ffn_g"], layer_p[l]["ln_ffn_b"], f"dlnffn{l}")
        g, d1, d2 = _layer_bwd(l, sv, layer_w[l], layer_p[l], ds_f, ds_b)
        g["ln_ffn_g"], g["ln_ffn_b"] = lg, lb
        grads[l] = g

    def ln_in_bwd(x_, d1_, d2_, g_, b_):
        _, vjp = jax.vjp(_layer_norm, x_, g_, b_)
        dx, dg, db = vjp(ALPHA * d1_ + d2_)
        return (dx,), (dg, db)
    (grad_x,), (g_ln_in_g, g_ln_in_b) = _rows(
        ln_in_bwd, [(xs_, 0, D_MODEL), (d1, 0, D_MODEL), (d2, 0, D_MODEL)], [ln_in_g[None], ln_in_b[None]],
        [(D_MODEL, F32)], [(1, D_MODEL), (1, D_MODEL)], 256, "dln_in")

    def by_shard(name, gfull):
        if name in ("w_in", "w_ffn_gate", "w_ffn_up"):
            return _col_sharded_split(gfull)
        return gfull.reshape((4, gfull.shape[0] // 4) + gfull.shape[1:])
    big_g = [jnp.stack([by_shard(n, grads[l][n]) for l in range(DEPTH)]) for n in BIG]
    from_cores = _exchange(big_g, "c", False, "reduce_cores")
    chip_sum = [_sum_pieces(r, o, True, ci, f"sum_cores{i}") for i, (r, o) in enumerate(zip(from_cores, big_g))]
    from_chips = _exchange([a.astype(BF) for a in chip_sum], "xy", False, "reduce_chips")
    reduced = [_sum_pieces(r, o, True, chip, f"sum_chips{i}") for i, (r, o) in enumerate(zip(from_chips, chip_sum))]
    both = _with_own(_exchange(reduced, "c", True, "swap_layers"), reduced, ci)
    big_grad = dict(zip(BIG, both))

    small_g = {"ln_in_g": g_ln_in_g[0], "ln_in_b": g_ln_in_b[0]}
    for n in SMALL[2:]:
        small_g[n] = jnp.stack([grads[l][n].reshape(wts[n].shape[1:] if n != "conv_w" else (SSD_CONV, CONV_DIM)) for l in range(DEPTH)])
    flat = jnp.concatenate([small_g[n].reshape(-1) for n in SMALL])
    n_small = flat.shape[0]
    width = -(-n_small // 1024) * 1024
    flat = jnp.pad(flat, (0, width - n_small)).reshape(8, width // 8)
    gathered, = _with_own(_exchange([flat], "all", True, "gather_small"), [flat], 2 * chip + ci)
    total = _sum_slots(gathered, "sum_small").reshape(-1)
    small_grad, off = {}, 0
    for n in SMALL:
        shp = small_g[n].shape
        cnt = math.prod(shp)
        small_grad[n] = total[off:off + cnt].reshape(shp)
        off += cnt
    small_grad["conv_w"] = lax.dynamic_slice_in_dim(small_grad["conv_w"], chip * (CONV_DIM // 4), CONV_DIM // 4, axis=2)

    out_g, out_d, out_m, out_v = {}, {}, {}, {}
    for n in BIG:
        shp = wts[n].shape
        two_d = lambda t: t.reshape(shp[0] * shp[1], shp[2])
        d_, m_, v_ = _adamw(two_d(wts[n]), two_d(big_grad[n]), two_d(mom1[n]), two_d(mom2[n]), f"adamw_{n}")
        out_g[n], out_d[n], out_m[n], out_v[n] = big_grad[n], d_.reshape(shp), m_.reshape(shp), v_.reshape(shp)

    def flat_small(d):
        f = jnp.concatenate([d[n].reshape(-1) for n in SMALL])
        return jnp.pad(f, (0, swidth - f.shape[0])).reshape(8, swidth // 8)
    n_sw = sum(math.prod(wts[n].shape) for n in SMALL)
    swidth = -(-n_sw // 1024) * 1024
    d_, m_, v_ = _adamw(flat_small(wts), flat_small(small_grad), flat_small(mom1), flat_small(mom2), "adamw_small")
    off = 0
    for n in SMALL:
        shp = wts[n].shape
        cnt = math.prod(shp)
        out_g[n] = small_grad[n]
        out_d[n] = d_.reshape(-1)[off:off + cnt].reshape(shp)
        out_m[n] = m_.reshape(-1)[off:off + cnt].reshape(shp)
        out_v[n] = v_.reshape(-1)[off:off + cnt].reshape(shp)
        off += cnt

    return (loss, grad_x[None], *[out_g[n] for n in WEIGHTS], *[out_d[n] for n in WEIGHTS],
            *[out_m[n] for n in WEIGHTS], *[out_v[n] for n in WEIGHTS])
```

```python
import functools
import itertools
import math

import jax
import jax.numpy as jnp
from jax import lax
from jax.experimental import pallas as pl
from jax.experimental.pallas import tpu as pltpu

F32 = jnp.float32
BF = jnp.bfloat16

D_MODEL = 1024
DEPTH = 2
ATT_HEADS = 16
ATT_HEAD_DIM = 64
BLK = 128
SSD_D_INNER = 2048
SSD_HEADS = 32
SSD_GROUPS = 4
SSD_STATE = 128
SSD_CONV = 4
BC_DIM = 512
CONV_DIM = 3072
FFN_HIDDEN = 2816
IN_DIM = 8480
LN_EPS = 1e-5
RMS_EPS = 1e-5
ALPHA = (2 * DEPTH) ** 0.25
ADAM_LR = 0.001
ADAM_B1 = 0.9
ADAM_B2 = 0.999
ADAM_EPS = 1e-08
ADAM_WD = 0.01
ADAM_STEP = 10

_PACK = (("gates", 6432, 2048), ("z", 1280, 2048), ("q", 0, 1024), ("xs", 3328, 2048), ("B", 5376, 512),
         ("C", 5888, 512), ("k", 1024, 128), ("v", 1152, 128), ("dt", 6400, 32))
NP = 8704
_OFF = {}
_o = 0
for _n, _, _w in _PACK:
    _OFF[_n] = _o
    _o += _w
_PAD = NP - _o

VMEM_LIMIT_BYTES = 56 * 1024 * 1024
BIG = ("w_in", "w_ssd_out", "w_att_out", "w_mix_out", "w_ffn_gate", "w_ffn_up", "w_ffn_down")
WEIGHTS = ("ln_in_g", "ln_in_b", "w_in", "conv_w", "conv_b", "dt_bias", "a_log", "d_skip", "ssd_norm_w", "att_sinks",
           "w_ssd_out", "w_att_out", "w_mix_out", "ln_mix_g", "ln_mix_b", "w_ffn_gate", "w_ffn_up", "w_ffn_down",
           "ln_ffn_g", "ln_ffn_b")
SMALL = tuple(n for n in WEIGHTS if n not in BIG)


def _params(n_grid):
    return pltpu.CompilerParams(dimension_semantics=("arbitrary",) * n_grid, vmem_limit_bytes=VMEM_LIMIT_BYTES)


def _dot(a, b, ca, cb):
    return lax.dot_general(a.astype(BF), b.astype(BF), (((ca,), (cb,)), ((), ())), preferred_element_type=F32)


@jax.custom_vjp
def _nn(a, b):
    return _dot(a, b, 1, 0)


def _nn_f(a, b):
    return _dot(a, b, 1, 0), (a, b)


def _nn_b(res, g):
    a, b = res
    return _dot(g, b, 1, 1).astype(a.dtype), _dot(a, g, 0, 0).astype(b.dtype)


_nn.defvjp(_nn_f, _nn_b)


@jax.custom_vjp
def _nt(a, b):
    return _dot(a, b, 1, 1)


def _nt_f(a, b):
    return _dot(a, b, 1, 1), (a, b)


def _nt_b(res, g):
    a, b = res
    return _dot(g, b, 1, 0).astype(a.dtype), _dot(g, a, 0, 0).astype(b.dtype)


_nt.defvjp(_nt_f, _nt_b)


@jax.custom_vjp
def _tn(a, b):
    return _dot(a, b, 0, 0)


def _tn_f(a, b):
    return _dot(a, b, 0, 0), (a, b)


def _tn_b(res, g):
    a, b = res
    return _dot(b, g, 1, 1).astype(a.dtype), _dot(a, g, 1, 0).astype(b.dtype)


_tn.defvjp(_tn_f, _tn_b)


def _silu(x):
    return x * jax.nn.sigmoid(x)


def _layer_norm(s, g, b):
    mu = jnp.mean(s, axis=-1, keepdims=True)
    sc = s - mu
    var = jnp.mean(sc * sc, axis=-1, keepdims=True)
    return sc * lax.rsqrt(var + LN_EPS) * g + b


def _ssd_group(pre_x, pre_b, pre_c, z, dtr, dtb, alog, dsk, nw, state):
    t = pre_x.shape[0]
    xs, bm, cm = _silu(pre_x), _silu(pre_b), _silu(pre_c)
    dt = jax.nn.softplus(dtr + dtb)
    da = dt * (-jnp.exp(alog))
    row = lax.broadcasted_iota(jnp.int32, (t, t), 0)
    col = lax.broadcasted_iota(jnp.int32, (t, t), 1)
    upper = (row <= col).astype(F32)
    cum = jnp.dot(da, upper, precision=lax.Precision.HIGHEST, preferred_element_type=F32)
    tot = jnp.sum(da, axis=1, keepdims=True)
    cb = _nt(cm, bm)
    tril = row >= col
    first = lax.broadcasted_iota(jnp.int32, (t, BLK), 1) < 64
    first_row = lax.broadcasted_iota(jnp.int32, (1, BLK), 1) < 64

    def col_form(r):
        return jnp.broadcast_to(r, (t, t)).T

    ys, new_state = [], []
    for p in range(4):
        h0, h1 = 2 * p, 2 * p + 1
        sl = slice(BLK * p, BLK * (p + 1))
        x_pair = xs[:, sl] * jnp.where(first, col_form(dt[h0:h0 + 1]), col_form(dt[h1:h1 + 1]))
        y_pair = None
        cc = []
        for h, keep in ((h0, first), (h1, jnp.logical_not(first))):
            cr = jnp.broadcast_to(cum[h:h + 1], (t, t))
            cc.append(cr.T)
            decay = jnp.exp(jnp.where(tril, cc[-1] - cr, -1e30))
            y_h = _nn(decay * cb, jnp.where(keep, x_pair, 0.0))
            y_pair = y_h if y_pair is None else y_pair + y_h
        s_pair = state[:, sl]
        y_pair = y_pair + _nn(cm, s_pair) * jnp.where(first, jnp.exp(cc[0]), jnp.exp(cc[1]))
        to_end = jnp.where(first, jnp.exp(tot[h0:h0 + 1] - cc[0]), jnp.exp(tot[h1:h1 + 1] - cc[1]))
        chunk_decay = jnp.where(first_row, jnp.exp(tot[h0:h0 + 1]), jnp.exp(tot[h1:h1 + 1]))
        new_state.append(s_pair * chunk_decay + _tn(bm, x_pair * to_end))
        ys.append(y_pair + dsk[:, sl] * xs[:, sl])
    y = jnp.concatenate(ys, axis=1) * _silu(z)
    y = y * lax.rsqrt(jnp.mean(y * y, axis=-1, keepdims=True) + RMS_EPS) * nw
    return y, jnp.concatenate(new_state, axis=1)


def _attn_block(q, kp, kc, vp, vc, sinks, n, kv):
    t = q.shape[0]
    kb = jnp.concatenate([kp, kc], axis=0)
    vb = jnp.concatenate([vp, vc], axis=0)
    qi = lax.broadcasted_iota(jnp.int32, (t, 2 * t), 0)
    kj = lax.broadcasted_iota(jnp.int32, (t, 2 * t), 1)
    rel = qi + t - kj
    valid = (rel >= 0) & (rel < t) & ((n * t - t + kj) >= 0)
    relf = rel.astype(F32)
    first = lax.broadcasted_iota(jnp.int32, (t, BLK), 1) < 64
    lane16 = lax.broadcasted_iota(jnp.int32, (1, ATT_HEADS), 1)
    outs = []
    for p in range(4):
        qp = q[:, BLK * p:BLK * (p + 1)] * (ATT_HEAD_DIM ** -0.5)
        o_pair = None
        for half, keep in enumerate((first, jnp.logical_not(first))):
            h = kv * 8 + 2 * p + half
            s = _nt(jnp.where(keep, qp, 0.0), kb)
            slope = jnp.exp((-8.0 * math.log(2.0) / ATT_HEADS) * jnp.asarray(h + 1, F32))
            s = jnp.where(valid, s - slope * relf, -1e30)
            sink = jnp.sum(jnp.where(lane16 == h, sinks, 0.0), axis=1, keepdims=True)
            m = jnp.maximum(jnp.max(s, axis=1, keepdims=True), sink)
            e = jnp.exp(s - m)
            den = jnp.sum(e, axis=1, keepdims=True) + jnp.exp(sink - m)
            o = _nn(e / den, vb)
            o_pair = o if o_pair is None else jnp.where(first, o_pair, o)
        outs.append(o_pair)
    return jnp.concatenate(outs, axis=1)


def _mm(a, b, mode, out_dtype, tm, tn, tk, name):
    if mode == "nn":
        (m, k), (k2, n) = a.shape, b.shape
    elif mode == "nt":
        (m, k), (n, k2) = a.shape, b.shape
    else:
        (k, m), (k2, n) = a.shape, b.shape
    assert k == k2, (a.shape, b.shape, mode)
    tm, tn, tk = min(tm, m), min(tn, n), min(tk, k)
    assert m % tm == 0 and n % tn == 0 and k % tk == 0, (m, n, k, tm, tn, tk)
    nk = k // tk
    ca, cb = {"nn": (1, 0), "nt": (1, 1), "tn": (0, 0)}[mode]

    def body(a_ref, b_ref, o_ref, *acc):
        part = _dot(a_ref[...], b_ref[...], ca, cb)
        if nk == 1:
            o_ref[...] = part.astype(o_ref.dtype)
            return
        acc_ref, = acc
        kk = pl.program_id(2)

        @pl.when(kk == 0)
        def _():
            acc_ref[...] = part

        @pl.when(kk > 0)
        def _():
            acc_ref[...] += part

        @pl.when(kk == nk - 1)
        def _():
            o_ref[...] = acc_ref[...].astype(o_ref.dtype)

    a_spec = pl.BlockSpec((tk, tm), lambda i, j, kk: (kk, i)) if mode == "tn" else pl.BlockSpec((tm, tk), lambda i, j, kk: (i, kk))
    b_spec = pl.BlockSpec((tn, tk), lambda i, j, kk: (j, kk)) if mode == "nt" else pl.BlockSpec((tk, tn), lambda i, j, kk: (kk, j))
    return pl.pallas_call(
        body, name=name, grid=(m // tm, n // tn, nk),
        in_specs=[a_spec, b_spec], out_specs=pl.BlockSpec((tm, tn), lambda i, j, kk: (i, j)),
        out_shape=jax.ShapeDtypeStruct((m, n), out_dtype),
        scratch_shapes=[] if nk == 1 else [pltpu.VMEM((tm, tn), F32)],
        compiler_params=_params(3),
    )(a, b)


def _rows(fn, rows, params, outs, accs, tile, name):
    length = rows[0][0].shape[0]
    tile = min(tile, length)
    assert length % tile == 0
    nr, npar, no = len(rows), len(params), len(outs)

    def body(*refs):
        vals = [r[...] for r in refs[:nr + npar]]
        o, a = fn(*vals)
        for ref, val in zip(refs[nr + npar:nr + npar + no], o):
            ref[...] = val.astype(ref.dtype)
        i = pl.program_id(0)
        for ref, val in zip(refs[nr + npar + no:], a):
            @pl.when(i == 0)
            def _(ref=ref, val=val):
                ref[...] = val

            @pl.when(i > 0)
            def _(ref=ref, val=val):
                ref[...] += val

    in_specs = [pl.BlockSpec((tile, w), functools.partial(lambda i, cb: (i, cb), cb=cb)) for _, cb, w in rows]
    in_specs += [pl.BlockSpec(p.shape, lambda i: (0, 0)) for p in params]
    outs = [o if len(o) == 3 else (o[0], o[1], o[0]) for o in outs]
    out_specs = [pl.BlockSpec((tile, w), lambda i: (i, 0)) for w, _, _ in outs]
    out_specs += [pl.BlockSpec((r, w), lambda i: (0, 0)) for r, w in accs]
    out_shape = [jax.ShapeDtypeStruct((length, full), dt) for _, dt, full in outs]
    out_shape += [jax.ShapeDtypeStruct((r, w), F32) for r, w in accs]
    res = pl.pallas_call(
        body, name=name, grid=(length // tile,), in_specs=in_specs, out_specs=out_specs, out_shape=out_shape,
        compiler_params=_params(1),
    )(*[r[0] for r in rows], *params)
    return res[:no], res[no:]


def _conv_fwd(proj, conv_w, conv_b, name):
    length = proj.shape[0]
    tl, cw = min(512, length), 512
    cb0 = _OFF["xs"] // cw

    def body(u_ref, halo_ref, w_ref, b_ref, o_ref, win):
        i = pl.program_id(1)
        win[0:8, :] = jnp.where(i > 0, halo_ref[...], 0.0)
        win[8:8 + tl, :] = u_ref[...]
        acc = b_ref[...] + w_ref[0:1, :] * win[5:5 + tl, :]
        for kk in range(1, SSD_CONV):
            acc = acc + w_ref[kk:kk + 1, :] * win[5 + kk:5 + kk + tl, :]
        o_ref[...] = acc

    return pl.pallas_call(
        body, name=name, grid=(CONV_DIM // cw, length // tl),
        in_specs=[pl.BlockSpec((tl, cw), lambda j, i: (i, cb0 + j)),
                  pl.BlockSpec((8, cw), lambda j, i: (jnp.maximum(i * (tl // 8) - 1, 0), cb0 + j)),
                  pl.BlockSpec((SSD_CONV, cw), lambda j, i: (0, j)),
                  pl.BlockSpec((1, cw), lambda j, i: (0, j))],
        out_specs=pl.BlockSpec((tl, cw), lambda j, i: (i, j)),
        out_shape=jax.ShapeDtypeStruct((length, CONV_DIM), F32),
        scratch_shapes=[pltpu.VMEM((8 + tl, cw), F32)],
        compiler_params=_params(2),
    )(proj, proj, conv_w, conv_b)


def _conv_bwd(dpre, proj, conv_w, dproj, part, name):
    length = proj.shape[0]
    width = dpre.shape[1]
    tl, cw = min(512, length), 512
    cb0 = _OFF["xs"] // cw + part
    n_t = length // tl

    def body(d_ref, dnext_ref, u_ref, halo_ref, w_ref, _, du_ref, dw_ref, db_ref, dwin, uwin):
        i = pl.program_id(1)
        d = d_ref[...]
        dwin[0:tl, :] = d
        dwin[tl:tl + 8, :] = jnp.where(i < n_t - 1, dnext_ref[...], 0.0)
        uwin[0:8, :] = jnp.where(i > 0, halo_ref[...], 0.0)
        uwin[8:8 + tl, :] = u_ref[...]
        du = w_ref[3:4, :] * d
        for kk in range(SSD_CONV - 1):
            du = du + w_ref[kk:kk + 1, :] * dwin[3 - kk:3 - kk + tl, :]
        du_ref[...] = du.astype(du_ref.dtype)
        dws = [jnp.sum(d * uwin[5 + kk:5 + kk + tl, :], axis=0, keepdims=True) for kk in range(SSD_CONV)]
        dw = jnp.concatenate(dws, axis=0)
        db = jnp.sum(d, axis=0, keepdims=True)

        @pl.when(i == 0)
        def _():
            dw_ref[...] = dw
            db_ref[...] = db

        @pl.when(i > 0)
        def _():
            dw_ref[...] += dw
            db_ref[...] += db

    return pl.pallas_call(
        body, name=name, grid=(width // cw, n_t), input_output_aliases={5: 0},
        in_specs=[pl.BlockSpec((tl, cw), lambda j, i: (i, j)),
                  pl.BlockSpec((8, cw), lambda j, i: (jnp.minimum((i + 1) * (tl // 8), length // 8 - 1), j)),
                  pl.BlockSpec((tl, cw), lambda j, i: (i, cb0 + j)),
                  pl.BlockSpec((8, cw), lambda j, i: (jnp.maximum(i * (tl // 8) - 1, 0), cb0 + j)),
                  pl.BlockSpec((SSD_CONV, cw), lambda j, i: (0, part + j)),
                  pl.BlockSpec(memory_space=pl.ANY)],
        out_specs=[pl.BlockSpec((tl, cw), lambda j, i: (i, cb0 + j)),
                   pl.BlockSpec((SSD_CONV, cw), lambda j, i: (0, j)),
                   pl.BlockSpec((1, cw), lambda j, i: (0, j))],
        out_shape=[jax.ShapeDtypeStruct((length, NP), BF),
                   jax.ShapeDtypeStruct((SSD_CONV, width), F32),
                   jax.ShapeDtypeStruct((1, width), F32)],
        scratch_shapes=[pltpu.VMEM((tl + 8, cw), F32), pltpu.VMEM((8 + tl, cw), F32)],
        compiler_params=_params(2),
    )(dpre, dpre, proj, proj, conv_w, dproj)


def _ssd_in_specs(rev, nc):
    def cidx(c):
        return nc - 1 - c if rev else c
    whole = lambda c: (0, 0)
    return [pl.BlockSpec((BLK, SSD_D_INNER), lambda c: (cidx(c), 0)),
            pl.BlockSpec((BLK, BC_DIM), lambda c: (cidx(c), SSD_D_INNER // BC_DIM)),
            pl.BlockSpec((BLK, BC_DIM), lambda c: (cidx(c), SSD_D_INNER // BC_DIM + 1)),
            pl.BlockSpec((BLK, SSD_D_INNER), lambda c: (cidx(c), _OFF["z"] // SSD_D_INNER)),
            pl.BlockSpec((SSD_HEADS, BLK), lambda c: (0, cidx(c))),
            pl.BlockSpec((SSD_HEADS, 1), whole),
            pl.BlockSpec((SSD_HEADS, 1), whole),
            pl.BlockSpec((1, SSD_D_INNER), whole),
            pl.BlockSpec((1, SSD_D_INNER), whole)]


def _group_args(g, px, pb, pc, z, dtr, dtb, alog, dsk, nw):
    wide, narrow, heads = slice(512 * g, 512 * (g + 1)), slice(BLK * g, BLK * (g + 1)), slice(8 * g, 8 * (g + 1))
    return (px[:, wide], pb[:, narrow], pc[:, narrow], z[:, wide], dtr[heads, :], dtb[heads, :], alog[heads, :],
            dsk[:, wide], nw[:, wide])


def _ssd_fwd(pre, proj, dt_t, dtb, alog, dsk, nw, name):
    length = pre.shape[0]
    nc = length // BLK

    def body(px, pb, pc, z, dtr, dtb_r, al_r, dsk_r, nw_r, y_ref, sin_ref, st):
        @pl.when(pl.program_id(0) == 0)
        def _():
            st[...] = jnp.zeros_like(st)

        for g in range(SSD_GROUPS):
            s_in = st[g]
            sin_ref[g] = s_in
            y, s_out = _ssd_group(*_group_args(g, px, pb, pc, z, dtr, dtb_r, al_r, dsk_r, nw_r), s_in)
            y_ref[:, 512 * g:512 * (g + 1)] = y.astype(y_ref.dtype)
            st[g] = s_out

    return pl.pallas_call(
        body, name=name, grid=(nc,), in_specs=_ssd_in_specs(False, nc),
        out_specs=[pl.BlockSpec((BLK, SSD_D_INNER), lambda c: (c, 0)),
                   pl.BlockSpec((SSD_GROUPS, None, SSD_STATE, 512), lambda c: (0, c, 0, 0))],
        out_shape=[jax.ShapeDtypeStruct((length, SSD_D_INNER), BF),
                   jax.ShapeDtypeStruct((SSD_GROUPS, nc, SSD_STATE, 512), F32)],
        scratch_shapes=[pltpu.VMEM((SSD_GROUPS, SSD_STATE, 512), F32)],
        compiler_params=_params(1),
    )(pre, pre, pre, proj, dt_t, dtb, alog, dsk, nw)


def _ssd_bwd(pre, proj, dt_t, dtb, alog, dsk, nw, s_in, dy, dproj, name):
    length = pre.shape[0]
    nc = length // BLK

    def body(px, pb, pc, z, dtr, dtb_r, al_r, dsk_r, nw_r, sin_r, dy_r, _,
             dpx, dpb, dpc, dz, ddt, ddtb, dal, ddsk, dnw, dst):
        c = pl.program_id(0)

        @pl.when(c == 0)
        def _():
            dst[...] = jnp.zeros_like(dst)

        for grp in range(SSD_GROUPS):
            wide, narrow, heads = slice(512 * grp, 512 * (grp + 1)), slice(BLK * grp, BLK * (grp + 1)), slice(8 * grp, 8 * (grp + 1))
            _, vjp = jax.vjp(_ssd_group, *_group_args(grp, px, pb, pc, z, dtr, dtb_r, al_r, dsk_r, nw_r), sin_r[grp])
            g = vjp((dy_r[:, wide], dst[grp]))
            dpx[:, wide] = g[0]
            dpb[:, narrow] = g[1]
            dpc[:, narrow] = g[2]
            dz[:, wide] = g[3].astype(dz.dtype)
            ddt[heads, :] = g[4]
            dst[grp] = g[9]
            for ref, val, idx in ((ddtb, g[5], (heads, slice(None))), (dal, g[6], (heads, slice(None))),
                                  (ddsk, g[7], (slice(None), wide)), (dnw, g[8], (slice(None), wide))):
                @pl.when(c == 0)
                def _(ref=ref, val=val, idx=idx):
                    ref[idx] = val

                @pl.when(c > 0)
                def _(ref=ref, val=val, idx=idx):
                    ref[idx] += val

    rev = lambda c: nc - 1 - c
    whole = lambda c: (0, 0)
    in_specs = _ssd_in_specs(True, nc) + [
        pl.BlockSpec((SSD_GROUPS, None, SSD_STATE, 512), lambda c: (0, rev(c), 0, 0)),
        pl.BlockSpec((BLK, SSD_D_INNER), lambda c: (rev(c), 0)),
        pl.BlockSpec(memory_space=pl.ANY)]
    return pl.pallas_call(
        body, name=name, grid=(nc,), in_specs=in_specs, input_output_aliases={11: 3},
        out_specs=[pl.BlockSpec((BLK, SSD_D_INNER), lambda c: (rev(c), 0)),
                   pl.BlockSpec((BLK, BC_DIM), lambda c: (rev(c), 0)),
                   pl.BlockSpec((BLK, BC_DIM), lambda c: (rev(c), 0)),
                   pl.BlockSpec((BLK, SSD_D_INNER), lambda c: (rev(c), _OFF["z"] // SSD_D_INNER)),
                   pl.BlockSpec((SSD_HEADS, BLK), lambda c: (0, rev(c))),
                   pl.BlockSpec((SSD_HEADS, 1), whole),
                   pl.BlockSpec((SSD_HEADS, 1), whole),
                   pl.BlockSpec((1, SSD_D_INNER), whole),
                   pl.BlockSpec((1, SSD_D_INNER), whole)],
        out_shape=[jax.ShapeDtypeStruct((length, SSD_D_INNER), F32),
                   jax.ShapeDtypeStruct((length, BC_DIM), F32),
                   jax.ShapeDtypeStruct((length, BC_DIM), F32),
                   jax.ShapeDtypeStruct((length, NP), BF),
                   jax.ShapeDtypeStruct((SSD_HEADS, length), F32),
                   jax.ShapeDtypeStruct((SSD_HEADS, 1), F32),
                   jax.ShapeDtypeStruct((SSD_HEADS, 1), F32),
                   jax.ShapeDtypeStruct((1, SSD_D_INNER), F32),
                   jax.ShapeDtypeStruct((1, SSD_D_INNER), F32)],
        scratch_shapes=[pltpu.VMEM((SSD_GROUPS, SSD_STATE, 512), F32)],
        compiler_params=_params(1),
    )(pre, pre, pre, proj, dt_t, dtb, alog, dsk, nw, s_in, dy, dproj)


assert _OFF["gates"] == 0 and _OFF["z"] % SSD_D_INNER == 0 and _OFF["q"] % D_MODEL == 0 and _OFF["k"] + 512 == NP


def _attn_in_specs():
    prev = lambda n: (0, jnp.maximum(n - 1, 0), 0)
    cur = lambda n: (0, n, 0)
    return [pl.BlockSpec((BLK, D_MODEL), lambda n: (n, _OFF["q"] // D_MODEL)),
            pl.BlockSpec((2, BLK, BLK), prev),
            pl.BlockSpec((2, BLK, BLK), cur),
            pl.BlockSpec((2, BLK, BLK), prev),
            pl.BlockSpec((2, BLK, BLK), cur),
            pl.BlockSpec((1, ATT_HEADS), lambda n: (0, 0))]


def _attn_fwd(proj, k2, v2, sinks, name):
    length = proj.shape[0]

    def body(q, kp, kc, vp, vc, sk, o_ref):
        for kv in range(2):
            cols = slice(512 * kv, 512 * (kv + 1))
            o = _attn_block(q[:, cols], kp[kv], kc[kv], vp[kv], vc[kv], sk[...], pl.program_id(0), kv)
            o_ref[:, cols] = o.astype(o_ref.dtype)

    return pl.pallas_call(
        body, name=name, grid=(length // BLK,), in_specs=_attn_in_specs(),
        out_specs=pl.BlockSpec((BLK, D_MODEL), lambda n: (n, 0)),
        out_shape=jax.ShapeDtypeStruct((length, D_MODEL), BF),
        compiler_params=_params(1),
    )(proj, k2, k2, v2, v2, sinks)


def _attn_bwd(proj, k2, v2, sinks, datt, dproj, name):
    length = proj.shape[0]

    def body(q, kp, kc, vp, vc, sk, do, _, dq, dkp, dkc, dvp, dvc, dsk):
        n = pl.program_id(0)
        dsinks = None
        for kv in range(2):
            cols = slice(512 * kv, 512 * (kv + 1))
            _, vjp = jax.vjp(lambda *a: _attn_block(*a, n, kv), q[:, cols], kp[kv], kc[kv], vp[kv], vc[kv], sk[...])
            g = vjp(do[:, cols])
            dq[:, cols] = g[0].astype(dq.dtype)
            dkp[kv] = g[1]
            dkc[kv] = g[2]
            dvp[kv] = g[3]
            dvc[kv] = g[4]
            dsinks = g[5] if dsinks is None else dsinks + g[5]

        @pl.when(n == 0)
        def _():
            dsk[...] = dsinks

        @pl.when(n > 0)
        def _():
            dsk[...] += dsinks

    blk3 = pl.BlockSpec((2, BLK, BLK), lambda n: (0, n, 0))
    kv_shape = jax.ShapeDtypeStruct((2, length, BLK), F32)
    return pl.pallas_call(
        body, name=name, grid=(length // BLK,),
        in_specs=_attn_in_specs() + [pl.BlockSpec((BLK, D_MODEL), lambda n: (n, 0)), pl.BlockSpec(memory_space=pl.ANY)],
        out_specs=[pl.BlockSpec((BLK, D_MODEL), lambda n: (n, _OFF["q"] // D_MODEL)), blk3, blk3, blk3, blk3,
                   pl.BlockSpec((1, ATT_HEADS), lambda n: (0, 0))],
        out_shape=[jax.ShapeDtypeStruct((length, NP), BF), kv_shape, kv_shape, kv_shape, kv_shape,
                   jax.ShapeDtypeStruct((1, ATT_HEADS), F32)],
        input_output_aliases={7: 0},
        compiler_params=_params(1),
    )(proj, k2, k2, v2, v2, sinks, datt, dproj)


DMA_CHUNK_BYTES = 1 << 20
N_STAGE = 8
LOOKAHEAD = 3


def _piece_chunks(shape, itemsize):
    if len(shape) < 2 or shape[-2] % 16 != 0:
        return [()]
    rows, cols = shape[-2:]
    step = min(rows, max(16, DMA_CHUNK_BYTES // (cols * itemsize) // 16 * 16))
    out = []
    for lead in itertools.product(*[range(d) for d in shape[:-2]]):
        for r0 in range(0, rows, step):
            out.append(lead + (pl.ds(r0, min(step, rows - r0)),))
    return out


def _exchange(srcs, group, bcast, name):
    size = {"c": 2, "xy": 4, "all": 8}[group]
    npeer = size - 1
    n = len(srcs)
    for s in srcs:
        assert bcast or s.shape[0] == size
    pieces = [s.shape if bcast else s.shape[1:] for s in srcs]
    chunks = [_piece_chunks(p, s.dtype.itemsize) for p, s in zip(pieces, srcs)]

    def body(*refs):
        src, out = refs[:n], refs[n:2 * n]
        load_sems, send_sems, recv_sems = refs[2 * n:]
        x, y, c = lax.axis_index("x"), lax.axis_index("y"), lax.axis_index("c")
        if group == "c":
            me = c
        elif group == "xy":
            me = 2 * x + y
        else:
            me = 4 * x + 2 * y + c

        def device(j):
            if group == "c":
                return (x, y, j)
            if group == "xy":
                return (j // 2, j % 2, c)
            return (j // 4, (j // 2) % 2, j % 2)

        def part(ref, idx):
            return ref.at[idx] if idx else ref

        def piece(a, j):
            return src[a] if bcast else src[a].at[j]

        for a in range(n):
            jobs = [(None, idx) for idx in chunks[a]] if bcast else [(d, idx) for idx in chunks[a] for d in range(1, size)]
            full_rows = chunks[a][0][-1].size if chunks[a][0] else None
            slot_shape = (full_rows, pieces[a][-1]) if chunks[a][0] else pieces[a]

            def stream(stage, a=a, jobs=jobs, full_rows=full_rows):
                def slot(q):
                    idx = jobs[q][1]
                    view = stage.at[q % N_STAGE]
                    if idx and idx[-1].size != full_rows:
                        view = view.at[pl.ds(0, idx[-1].size)]
                    return view

                def load(q):
                    d, idx = jobs[q]
                    j = me if d is None else jnp.bitwise_xor(me, d)
                    return pltpu.make_async_copy(part(piece(a, j), idx), slot(q), load_sems.at[q % N_STAGE])

                def sends(q):
                    d, idx = jobs[q]
                    return [pltpu.make_async_remote_copy(
                        src_ref=slot(q), dst_ref=part(out[a].at[me], idx),
                        send_sem=send_sems.at[(q % N_STAGE) * npeer + dd - 1], recv_sem=recv_sems.at[a * size + dd],
                        device_id=device(jnp.bitwise_xor(me, dd)), device_id_type=pl.DeviceIdType.MESH)
                        for dd in (range(1, size) if d is None else (d,))]

                nq = len(jobs)
                for q in range(nq + LOOKAHEAD):
                    if q < nq:
                        if q >= N_STAGE:
                            for cp in sends(q - N_STAGE):
                                cp.wait_send()
                        load(q).start()
                    if q >= LOOKAHEAD:
                        load(q - LOOKAHEAD).wait()
                        for cp in sends(q - LOOKAHEAD):
                            cp.start()
                for q in range(max(0, nq - N_STAGE), nq):
                    for cp in sends(q):
                        cp.wait_send()

            pl.run_scoped(stream, pltpu.VMEM((N_STAGE,) + tuple(slot_shape), srcs[a].dtype))

        for a in range(n):
            for d in range(1, size):
                j = jnp.bitwise_xor(me, d)
                pltpu.make_async_remote_copy(
                    src_ref=piece(a, j), dst_ref=out[a].at[j], send_sem=send_sems.at[0], recv_sem=recv_sems.at[a * size + d],
                    device_id=device(j), device_id_type=pl.DeviceIdType.MESH).wait_recv()

    any_spec = pl.BlockSpec(memory_space=pl.ANY)
    return pl.pallas_call(
        body, name=name, in_specs=[any_spec] * n, out_specs=[any_spec] * n,
        out_shape=[jax.ShapeDtypeStruct((size,) + tuple(p), s.dtype) for p, s in zip(pieces, srcs)],
        scratch_shapes=[pltpu.SemaphoreType.DMA((N_STAGE,)), pltpu.SemaphoreType.DMA((N_STAGE * npeer,)),
                        pltpu.SemaphoreType.DMA((n * size,))],
        compiler_params=pltpu.CompilerParams(vmem_limit_bytes=VMEM_LIMIT_BYTES),
    )(*srcs)


def _with_own(outs, owns, me):
    return [lax.dynamic_update_index_in_dim(o, w, me, 0) for o, w in zip(outs, owns)]


def _sum_slots(arr, name):
    k = arr.shape[0]
    rest = arr.shape[1:]
    width = rest[-1]
    rows_per = math.prod(rest[:-1])
    a2 = arr.reshape(k * rows_per, width)
    tile = rows_per
    for cand in (256, 128, 64, 32, 16, 8):
        if rows_per % cand == 0:
            tile = cand
            break
    nt = rows_per // tile

    def body(*refs):
        acc = refs[0][...]
        for r in refs[1:k]:
            acc = acc + r[...]
        refs[k][...] = acc

    in_specs = [pl.BlockSpec((tile, width), functools.partial(lambda i, s: (s * nt + i, 0), s=s)) for s in range(k)]
    out = pl.pallas_call(
        body, name=name, grid=(nt,), in_specs=in_specs, out_specs=pl.BlockSpec((tile, width), lambda i: (i, 0)),
        out_shape=jax.ShapeDtypeStruct((rows_per, width), arr.dtype), compiler_params=_params(1),
    )(*([a2] * k))
    return out.reshape(rest)


def _sum_pieces(recv, own, own_slotted, me, name):
    k = recv.shape[0]
    rest = recv.shape[1:]
    width = rest[-1]
    rows_per = math.prod(rest[:-1])
    r2 = recv.reshape(k * rows_per, width)
    o2 = own.reshape(-1, width)
    tile = rows_per
    for cand in (256, 128, 64, 32, 16, 8):
        if rows_per % cand == 0:
            tile = cand
            break
    nt = rows_per // tile

    def body(me_ref, *refs):
        mine = refs[k][...].astype(F32)
        acc = None
        for j in range(k):
            term = jnp.where(me_ref[0] == j, mine, refs[j][...].astype(F32))
            acc = term if acc is None else acc + term
        refs[k + 1][...] = acc

    in_specs = [pl.BlockSpec((tile, width), functools.partial(lambda i, m, s: (s * nt + i, 0), s=s)) for s in range(k)]
    in_specs.append(pl.BlockSpec((tile, width), (lambda i, m: (m[0] * nt + i, 0)) if own_slotted else (lambda i, m: (i, 0))))
    out = pl.pallas_call(
        body, name=name,
        grid_spec=pltpu.PrefetchScalarGridSpec(
            num_scalar_prefetch=1, grid=(nt,), in_specs=in_specs,
            out_specs=pl.BlockSpec((tile, width), lambda i, m: (i, 0))),
        out_shape=jax.ShapeDtypeStruct((rows_per, width), F32), compiler_params=_params(1),
    )(jnp.reshape(me, (1,)).astype(jnp.int32), *([r2] * k), o2)
    return out.reshape(rest)


def _adamw(w, g, m, v, name):
    def fn(w_, g_, m_, v_):
        m1 = ADAM_B1 * m_ + (1.0 - ADAM_B1) * g_
        v1 = ADAM_B2 * v_ + (1.0 - ADAM_B2) * (g_ * g_)
        m_hat = m1 / (1.0 - ADAM_B1 ** ADAM_STEP)
        v_hat = v1 / (1.0 - ADAM_B2 ** ADAM_STEP)
        delta = -ADAM_LR * (m_hat / (jnp.sqrt(v_hat) + ADAM_EPS) + ADAM_WD * w_)
        return (delta, m1, v1), ()

    rows, width = w.shape
    tile = rows
    for cand in (256, 128, 64, 32, 16, 8):
        if rows % cand == 0:
            tile = cand
            break
    o, _ = _rows(fn, [(w, 0, width), (g, 0, width), (m, 0, width), (v, 0, width)], [],
                 [(width, F32)] * 3, [], tile, name)
    return o


def _pack_cols(w):
    parts = [w[..., o:o + wd] for _, o, wd in _PACK]
    parts.append(jnp.zeros(w.shape[:-1] + (_PAD,), w.dtype))
    return jnp.concatenate(parts, axis=-1)


def _unpack_cols(wp):
    order = sorted(_PACK, key=lambda e: e[1])
    return jnp.concatenate([wp[..., _OFF[nm]:_OFF[nm] + wd] for nm, _, wd in order], axis=-1)


def _col_sharded_full(g):
    return jnp.transpose(g, (1, 0, 2)).reshape(g.shape[1], -1)


def _col_sharded_split(full):
    r, cdim = full.shape
    return jnp.transpose(full.reshape(r, 4, cdim // 4), (1, 0, 2))


def _double_heads(t):
    length = t.shape[0]
    h = jnp.transpose(t.reshape(length, 2, 64), (1, 0, 2))
    return jnp.concatenate([h, h], axis=-1)


def _fold_heads(d_cur, d_prev):
    length = d_cur.shape[1]
    d = d_cur + jnp.concatenate([d_prev[:, BLK:], jnp.zeros((2, BLK, BLK), F32)], axis=1)
    d = d[..., :64] + d[..., 64:]
    return jnp.transpose(d, (1, 0, 2)).reshape(length, 128)


def _ln_fwd(a, r, g, b, name):
    def fn(a_, r_, g_, b_):
        y = _layer_norm(ALPHA * a_ + r_, g_, b_)
        return (y, y), ()
    o, _ = _rows(fn, [(a, 0, D_MODEL), (r, 0, D_MODEL)], [g, b], [(D_MODEL, F32), (D_MODEL, BF)], [], 256, name)
    return o


def _ln_bwd(a, r, d1, d2, g, b, name):
    def fn(a_, r_, d1_, d2_, g_, b_):
        _, vjp = jax.vjp(_layer_norm, ALPHA * a_ + r_, g_, b_)
        ds, dg, db = vjp(ALPHA * d1_ + d2_)
        return (ds, ds), (dg, db)
    o, acc = _rows(fn, [(a, 0, D_MODEL), (r, 0, D_MODEL), (d1, 0, D_MODEL), (d2, 0, D_MODEL)], [g, b],
                   [(D_MODEL, F32), (D_MODEL, BF)], [(1, D_MODEL), (1, D_MODEL)], 256, name)
    return o[0], o[1], acc[0], acc[1]


def _layer_fwd(l, h, hb, w, p):
    sv = {"h": h, "hb": hb}
    proj = _mm(hb, w["win"], "nn", F32, 2048, 512, 1024, f"proj{l}")
    sv["proj"] = proj
    pre = _conv_fwd(proj, p["conv_w"], p["conv_b"], f"conv{l}")
    sv["pre"] = pre
    dt_t = jnp.transpose(proj[:, _OFF["dt"]:_OFF["dt"] + SSD_HEADS])
    sv["dt_t"] = dt_t
    yn, s_in = _ssd_fwd(pre, proj, dt_t, p["dt_bias"], p["a_log"], p["d_skip_c"], p["ssd_norm_w"], f"ssd{l}")
    sv["yn"], sv["s_in"] = yn, s_in
    ya = _mm(yn, w["wso"], "nn", F32, 1024, 1024, 2048, f"ssdout{l}")
    k2 = _double_heads(proj[:, _OFF["k"]:_OFF["k"] + 128])
    v2 = _double_heads(proj[:, _OFF["v"]:_OFF["v"] + 128])
    sv["k2"], sv["v2"] = k2, v2
    att = _attn_fwd(proj, k2, v2, p["att_sinks"], f"attn{l}")
    sv["att"] = att
    yb = _mm(att, w["wao"], "nn", F32, 1024, 1024, 1024, f"attout{l}")
    sv["ya"], sv["yb"] = ya, yb
    gcb = _OFF["gates"] // 1024

    def gate_fn(ga, gb, ya_, yb_):
        return (jax.nn.sigmoid(ga) * ya_ + jax.nn.sigmoid(gb) * yb_,), ()
    (ub,), _ = _rows(gate_fn, [(proj, gcb, 1024), (proj, gcb + 1, 1024), (ya, 0, 1024), (yb, 0, 1024)], [],
                     [(D_MODEL, BF)], [], 256, f"gate{l}")
    sv["ub"] = ub
    mix = _mm(ub, w["wmo"], "nn", F32, 1024, 1024, 1024, f"mixout{l}")
    sv["mix"] = mix
    h1, h1b = _ln_fwd(h, mix, p["ln_mix_g"], p["ln_mix_b"], f"lnmix{l}")
    sv["h1"], sv["h1b"] = h1, h1b
    gu = _mm(h1b, w["wgu"], "nn", F32, 2048, 512, 1024, f"ffnin{l}")
    sv["gu"] = gu

    def act_fn(g_, u_):
        return (_silu(g_) * u_,), ()
    (act,), _ = _rows(act_fn, [(gu, 0, FFN_HIDDEN), (gu, 1, FFN_HIDDEN)], [], [(FFN_HIDDEN, BF)], [], 256, f"swiglu{l}")
    sv["act"] = act
    ffn = _mm(act, w["wd"], "nn", F32, 1024, 1024, FFN_HIDDEN, f"ffnout{l}")
    sv["ffn"] = ffn
    return sv


def _layer_bwd(l, sv, w, p, ds_f, ds_b):
    g = {}
    g["w_ffn_down"] = _mm(sv["act"], ds_b, "tn", F32, 1408, 1024, 1024, f"dwd{l}")
    dact = _mm(ds_b, w["wd"], "nt", F32, 1024, FFN_HIDDEN, 1024, f"dact{l}")

    def act_bwd(g_, u_, d_):
        _, vjp = jax.vjp(lambda a, b: _silu(a) * b, g_, u_)
        dg_, du_ = vjp(d_)
        return (jnp.concatenate([dg_, du_], axis=1),), ()
    (dgu,), _ = _rows(act_bwd, [(sv["gu"], 0, FFN_HIDDEN), (sv["gu"], 1, FFN_HIDDEN), (dact, 0, FFN_HIDDEN)], [],
                      [(2 * FFN_HIDDEN, BF)], [], 128, f"dswiglu{l}")
    dwgu = _mm(sv["h1b"], dgu, "tn", F32, 1024, 2816, 512, f"dwgu{l}")
    g["w_ffn_gate"], g["w_ffn_up"] = dwgu[:, :FFN_HIDDEN], dwgu[:, FFN_HIDDEN:]
    dh1 = _mm(dgu, w["wgu"], "nt", F32, 1024, 1024, 2816, f"dh1{l}")
    ds2_f, ds2_b, g["ln_mix_g"], g["ln_mix_b"] = _ln_bwd(sv["h"], sv["mix"], ds_f, dh1, p["ln_mix_g"], p["ln_mix_b"], f"dlnmix{l}")
    g["w_mix_out"] = _mm(sv["ub"], ds2_b, "tn", F32, 1024, 1024, 512, f"dwmo{l}")
    du = _mm(ds2_b, w["wmo"], "nt", F32, 1024, 1024, 1024, f"du{l}")
    proj = sv["proj"]
    gcb = _OFF["gates"] // 1024

    def gate_bwd(ga, gb, ya_, yb_, du_):
        _, vjp = jax.vjp(lambda a, b, c, d: jax.nn.sigmoid(a) * c + jax.nn.sigmoid(b) * d, ga, gb, ya_, yb_)
        dga, dgb, dya, dyb = vjp(du_)
        return (jnp.concatenate([dga, dgb], axis=1), dya, dyb), ()
    (dproj, dya, dyb), _ = _rows(
        gate_bwd, [(proj, gcb, 1024), (proj, gcb + 1, 1024), (sv["ya"], 0, 1024), (sv["yb"], 0, 1024), (du, 0, 1024)], [],
        [(2048, BF, NP), (D_MODEL, BF), (D_MODEL, BF)], [], 256, f"dgate{l}")
    g["w_att_out"] = _mm(sv["att"], dyb, "tn", F32, 1024, 1024, 512, f"dwao{l}")
    datt = _mm(dyb, w["wao"], "nt", F32, 1024, 1024, 1024, f"datt{l}")
    g["w_ssd_out"] = _mm(sv["yn"], dya, "tn", F32, 1024, 1024, 512, f"dwso{l}")
    dyn = _mm(dya, w["wso"], "nt", F32, 1024, 1024, 1024, f"dyn{l}")
    dproj, dkp, dkc, dvp, dvc, g["att_sinks"] = _attn_bwd(proj, sv["k2"], sv["v2"], p["att_sinks"], datt, dproj, f"dattn{l}")
    (dpx, dpb, dpc, dproj, ddt_t, g["dt_bias"], g["a_log"], ddsk, g["ssd_norm_w"]) = _ssd_bwd(
        sv["pre"], proj, sv["dt_t"], p["dt_bias"], p["a_log"], p["d_skip_c"], p["ssd_norm_w"], sv["s_in"], dyn, dproj, f"dssd{l}")
    g["d_skip"] = jnp.sum(ddsk.reshape(SSD_HEADS, 64), axis=1)
    dws, dbs = [], []
    for nm, dpart, part in (("x", dpx, 0), ("b", dpb, 4), ("c", dpc, 5)):
        dproj, dw_, db_ = _conv_bwd(dpart, proj, p["conv_w"], dproj, part, f"dconv{nm}{l}")
        dws.append(dw_)
        dbs.append(db_)
    g["conv_w"], g["conv_b"] = jnp.concatenate(dws, axis=1), jnp.concatenate(dbs, axis=1)
    length = proj.shape[0]
    tail = jnp.concatenate([_fold_heads(dkc, dkp).astype(BF), _fold_heads(dvc, dvp).astype(BF),
                            jnp.transpose(ddt_t).astype(BF), jnp.zeros((length, _PAD), BF)], axis=1)
    dproj = lax.dynamic_update_slice(dproj, tail, (0, _OFF["k"]))
    g["w_in"] = _unpack_cols(_mm(sv["hb"], dproj, "tn", F32, 1024, 2176, 1024, f"dwin{l}"))
    dh = _mm(dproj, w["win"], "nt", F32, 1024, 1024, 2176, f"dh{l}")
    return g, ds2_f, dh


def kernel(x, ln_in_g, ln_in_b, w_in, conv_w, conv_b, dt_bias, a_log, d_skip, ssd_norm_w, att_sinks, w_ssd_out, w_att_out, w_mix_out, ln_mix_g, ln_mix_b, w_ffn_gate, w_ffn_up, w_ffn_down, ln_ffn_g, ln_ffn_b, loss_target, m_ln_in_g, m_ln_in_b, m_w_in, m_conv_w, m_conv_b, m_dt_bias, m_a_log, m_d_skip, m_ssd_norm_w, m_att_sinks, m_w_ssd_out, m_w_att_out, m_w_mix_out, m_ln_mix_g, m_ln_mix_b, m_w_ffn_gate, m_w_ffn_up, m_w_ffn_down, m_ln_ffn_g, m_ln_ffn_b, v_ln_in_g, v_ln_in_b, v_w_in, v_conv_w, v_conv_b, v_dt_bias, v_a_log, v_d_skip, v_ssd_norm_w, v_att_sinks, v_w_ssd_out, v_w_att_out, v_w_mix_out, v_ln_mix_g, v_ln_mix_b, v_w_ffn_gate, v_w_ffn_up, v_w_ffn_down, v_ln_ffn_g, v_ln_ffn_b):
    env = dict(locals())
    wts = {n: env[n] for n in WEIGHTS}
    mom1 = {n: env["m_" + n] for n in WEIGHTS}
    mom2 = {n: env["v_" + n] for n in WEIGHTS}
    ci = lax.axis_index("c")
    chip = 2 * lax.axis_index("x") + lax.axis_index("y")
    xs_ = x[0]
    tgt = loss_target[0]

    shards = [wts[n].astype(BF) for n in BIG] + [conv_w]
    mine = [lax.dynamic_index_in_dim(s, ci, 0, keepdims=False) for s in shards]
    by_chip = _with_own(_exchange(mine, "xy", True, "gather_chips"), mine, chip)
    by_layer = _with_own(_exchange(by_chip, "c", True, "gather_cores"), by_chip, ci)
    full = dict(zip(BIG + ("conv_w",), by_layer))

    layer_w, layer_p = [], []
    for l in range(DEPTH):
        layer_w.append({
            "win": _pack_cols(_col_sharded_full(full["w_in"][l])),
            "wso": full["w_ssd_out"][l].reshape(SSD_D_INNER, D_MODEL),
            "wao": full["w_att_out"][l].reshape(D_MODEL, D_MODEL),
            "wmo": full["w_mix_out"][l].reshape(D_MODEL, D_MODEL),
            "wgu": jnp.concatenate([_col_sharded_full(full["w_ffn_gate"][l]), _col_sharded_full(full["w_ffn_up"][l])], axis=1),
            "wd": full["w_ffn_down"][l].reshape(FFN_HIDDEN, D_MODEL),
        })
        layer_p.append({
            "conv_w": _col_sharded_full(full["conv_w"][l]), "conv_b": conv_b[l][None],
            "dt_bias": dt_bias[l][:, None], "a_log": a_log[l][:, None],
            "d_skip_c": jnp.repeat(d_skip[l], 64)[None], "ssd_norm_w": ssd_norm_w[l][None],
            "att_sinks": att_sinks[l][None], "ln_mix_g": ln_mix_g[l][None], "ln_mix_b": ln_mix_b[l][None],
            "ln_ffn_g": ln_ffn_g[l][None], "ln_ffn_b": ln_ffn_b[l][None],
        })

    def ln_in_fn(x_, g_, b_):
        y = _layer_norm(x_, g_, b_)
        return (y, y), ()
    (h, hb), _ = _rows(ln_in_fn, [(xs_, 0, D_MODEL)], [ln_in_g[None], ln_in_b[None]], [(D_MODEL, F32), (D_MODEL, BF)], [], 256, "ln_in")
    saved = []
    for l in range(DEPTH):
        sv = _layer_fwd(l, h, hb, layer_w[l], layer_p[l])
        saved.append(sv)
        if l < DEPTH - 1:
            h, hb = _ln_fwd(sv["h1"], sv["ffn"], layer_p[l]["ln_ffn_g"], layer_p[l]["ln_ffn_b"], f"lnffn{l}")

    def loss_fn(a_, r_, t_, g_, b_):
        y, vjp = jax.vjp(_layer_norm, ALPHA * a_ + r_, g_, b_)
        err = y - t_
        ds, dg, db = vjp(err * (1.0 / D_MODEL))
        part = 0.5 * jnp.sum(jnp.mean(err * err, axis=-1, keepdims=True), axis=0, keepdims=True)
        return (ds, ds), (dg, db, jnp.broadcast_to(part, (1, BLK)))
    sv = saved[-1]
    (ds_f, ds_b), (dg_last, db_last, loss_part) = _rows(
        loss_fn, [(sv["h1"], 0, D_MODEL), (sv["ffn"], 0, D_MODEL), (tgt, 0, D_MODEL)],
        [layer_p[-1]["ln_ffn_g"], layer_p[-1]["ln_ffn_b"]], [(D_MODEL, F32), (D_MODEL, BF)],
        [(1, D_MODEL), (1, D_MODEL), (1, BLK)], 256, "loss")
    loss = lax.psum(loss_part[0, 0], ("x", "y", "c"))

    grads = [None] * DEPTH
    for l in reversed(range(DEPTH)):
        sv = saved[l]
        if l == DEPTH - 1:
            lg, lb = dg_last, db_last
        else:
            ds_f, ds_b, lg, lb = _ln_bwd(sv["h1"], sv["ffn"], d1, d2, layer_p[l]["ln_ffn_g"], layer_p[l]["ln_ffn_b"], f"dlnffn{l}")
        g, d1, d2 = _layer_bwd(l, sv, layer_w[l], layer_p[l], ds_f, ds_b)
        g["ln_ffn_g"], g["ln_ffn_b"] = lg, lb
        grads[l] = g

    def ln_in_bwd(x_, d1_, d2_, g_, b_):
        _, vjp = jax.vjp(_layer_norm, x_, g_, b_)
        dx, dg, db = vjp(ALPHA * d1_ + d2_)
        return (dx,), (dg, db)
    (grad_x,), (g_ln_in_g, g_ln_in_b) = _rows(
        ln_in_bwd, [(xs_, 0, D_MODEL), (d1, 0, D_MODEL), (d2, 0, D_MODEL)], [ln_in_g[None], ln_in_b[None]],
        [(D_MODEL, F32)], [(1, D_MODEL), (1, D_MODEL)], 256, "dln_in")

    def by_shard(name, gfull):
        if name in ("w_in", "w_ffn_gate", "w_ffn_up"):
            return _col_sharded_split(gfull)
        return gfull.reshape((4, gfull.shape[0] // 4) + gfull.shape[1:])
    big_g = [jnp.stack([by_shard(n, grads[l][n]) for l in range(DEPTH)]) for n in BIG]
    from_cores = _exchange(big_g, "c", False, "reduce_cores")
    chip_sum = [_sum_pieces(r, o, True, ci, f"sum_cores{i}") for i, (r, o) in enumerate(zip(from_cores, big_g))]
    from_chips = _exchange([a.astype(BF) for a in chip_sum], "xy", False, "reduce_chips")
    reduced = [_sum_pieces(r, o, True, chip, f"sum_chips{i}") for i, (r, o) in enumerate(zip(from_chips, chip_sum))]
    both = _with_own(_exchange(reduced, "c", True, "swap_layers"), reduced, ci)
    big_grad = dict(zip(BIG, both))

    small_g = {"ln_in_g": g_ln_in_g[0], "ln_in_b": g_ln_in_b[0]}
    for n in SMALL[2:]:
        small_g[n] = jnp.stack([grads[l][n].reshape(wts[n].shape[1:] if n != "conv_w" else (SSD_CONV, CONV_DIM)) for l in range(DEPTH)])
    flat = jnp.concatenate([small_g[n].reshape(-1) for n in SMALL])
    n_small = flat.shape[0]
    width = -(-n_small // 1024) * 1024
    flat = jnp.pad(flat, (0, width - n_small)).reshape(8, width // 8)
    gathered, = _with_own(_exchange([flat], "all", True, "gather_small"), [flat], 2 * chip + ci)
    total = _sum_slots(gathered, "sum_small").reshape(-1)
    small_grad, off = {}, 0
    for n in SMALL:
        shp = small_g[n].shape
        cnt = math.prod(shp)
        small_grad[n] = total[off:off + cnt].reshape(shp)
        off += cnt
    small_grad["conv_w"] = lax.dynamic_slice_in_dim(small_grad["conv_w"], chip * (CONV_DIM // 4), CONV_DIM // 4, axis=2)

    out_g, out_d, out_m, out_v = {}, {}, {}, {}
    for n in BIG:
        shp = wts[n].shape
        two_d = lambda t: t.reshape(shp[0] * shp[1], shp[2])
        d_, m_, v_ = _adamw(two_d(wts[n]), two_d(big_grad[n]), two_d(mom1[n]), two_d(mom2[n]), f"adamw_{n}")
        out_g[n], out_d[n], out_m[n], out_v[n] = big_grad[n], d_.reshape(shp), m_.reshape(shp), v_.reshape(shp)

    def flat_small(d):
        f = jnp.concatenate([d[n].reshape(-1) for n in SMALL])
        return jnp.pad(f, (0, swidth - f.shape[0])).reshape(8, swidth // 8)
    n_sw = sum(math.prod(wts[n].shape) for n in SMALL)
    swidth = -(-n_sw // 1024) * 1024
    d_, m_, v_ = _adamw(flat_small(wts), flat_small(small_grad), flat_small(mom1), flat_small(mom2), "adamw_small")
    off = 0
    for n in SMALL:
        shp = wts[n].shape
        cnt = math.prod(shp)
        out_g[n] = small_grad[n]
        out_d[n] = d_.reshape(-1)[off:off + cnt].reshape(shp)
        out_m[n] = m_.reshape(-1)[off:off + cnt].reshape(shp)
        out_v[n] = v_.reshape(-1)[off:off + cnt].reshape(shp)
        off += cnt

    return (loss, grad_x[None], *[out_g[n] for n in WEIGHTS], *[out_d[n] for n in WEIGHTS],
            *[out_m[n] for n in WEIGHTS], *[out_v[n] for n in WEIGHTS])
```

```python
import functools
import itertools
import math

import jax
import jax.numpy as jnp
from jax import lax
from jax.experimental import pallas as pl
from jax.experimental.pallas import tpu as pltpu

F32 = jnp.float32
BF = jnp.bfloat16

D_MODEL = 1024
DEPTH = 2
ATT_HEADS = 16
ATT_HEAD_DIM = 64
BLK = 128
SSD_D_INNER = 2048
SSD_HEADS = 32
SSD_GROUPS = 4
SSD_STATE = 128
SSD_CONV = 4
BC_DIM = 512
CONV_DIM = 3072
FFN_HIDDEN = 2816
IN_DIM = 8480
LN_EPS = 1e-5
RMS_EPS = 1e-5
ALPHA = (2 * DEPTH) ** 0.25
ADAM_LR = 0.001
ADAM_B1 = 0.9
ADAM_B2 = 0.999
ADAM_EPS = 1e-08
ADAM_WD = 0.01
ADAM_STEP = 10

_PACK = (("gates", 6432, 2048), ("z", 1280, 2048), ("q", 0, 1024), ("xs", 3328, 2048), ("B", 5376, 512),
         ("C", 5888, 512), ("k", 1024, 128), ("v", 1152, 128), ("dt", 6400, 32))
NP = 8704
_OFF = {}
_o = 0
for _n, _, _w in _PACK:
    _OFF[_n] = _o
    _o += _w
_PAD = NP - _o

VMEM_LIMIT_BYTES = 56 * 1024 * 1024
BIG = ("w_in", "w_ssd_out", "w_att_out", "w_mix_out", "w_ffn_gate", "w_ffn_up", "w_ffn_down")
WEIGHTS = ("ln_in_g", "ln_in_b", "w_in", "conv_w", "conv_b", "dt_bias", "a_log", "d_skip", "ssd_norm_w", "att_sinks",
           "w_ssd_out", "w_att_out", "w_mix_out", "ln_mix_g", "ln_mix_b", "w_ffn_gate", "w_ffn_up", "w_ffn_down",
           "ln_ffn_g", "ln_ffn_b")
SMALL = tuple(n for n in WEIGHTS if n not in BIG)


def _params(n_grid):
    return pltpu.CompilerParams(dimension_semantics=("arbitrary",) * n_grid, vmem_limit_bytes=VMEM_LIMIT_BYTES)


def _dot(a, b, ca, cb):
    return lax.dot_general(a.astype(BF), b.astype(BF), (((ca,), (cb,)), ((), ())), preferred_element_type=F32)


@jax.custom_vjp
def _nn(a, b):
    return _dot(a, b, 1, 0)


def _nn_f(a, b):
    return _dot(a, b, 1, 0), (a, b)


def _nn_b(res, g):
    a, b = res
    return _dot(g, b, 1, 1).astype(a.dtype), _dot(a, g, 0, 0).astype(b.dtype)


_nn.defvjp(_nn_f, _nn_b)


@jax.custom_vjp
def _nt(a, b):
    return _dot(a, b, 1, 1)


def _nt_f(a, b):
    return _dot(a, b, 1, 1), (a, b)


def _nt_b(res, g):
    a, b = res
    return _dot(g, b, 1, 0).astype(a.dtype), _dot(g, a, 0, 0).astype(b.dtype)


_nt.defvjp(_nt_f, _nt_b)


@jax.custom_vjp
def _tn(a, b):
    return _dot(a, b, 0, 0)


def _tn_f(a, b):
    return _dot(a, b, 0, 0), (a, b)


def _tn_b(res, g):
    a, b = res
    return _dot(b, g, 1, 1).astype(a.dtype), _dot(a, g, 1, 0).astype(b.dtype)


_tn.defvjp(_tn_f, _tn_b)


def _sigmoid(x):
    return 0.5 * jnp.tanh(0.5 * x) + 0.5


def _silu(x):
    return x * _sigmoid(x)


def _layer_norm(s, g, b):
    mu = jnp.mean(s, axis=-1, keepdims=True)
    sc = s - mu
    var = jnp.mean(sc * sc, axis=-1, keepdims=True)
    return sc * lax.rsqrt(var + LN_EPS) * g + b


def _ssd_group(pre_x, pre_b, pre_c, z, dtr, dtb, alog, dsk, nw, state):
    t = pre_x.shape[0]
    xs, bm, cm = _silu(pre_x), _silu(pre_b), _silu(pre_c)
    dt = jax.nn.softplus(dtr + dtb)
    da = dt * (-jnp.exp(alog))
    row = lax.broadcasted_iota(jnp.int32, (t, t), 0)
    col = lax.broadcasted_iota(jnp.int32, (t, t), 1)
    upper = (row <= col).astype(F32)
    cum = jnp.dot(da, upper, precision=lax.Precision.HIGHEST, preferred_element_type=F32)
    tot = jnp.sum(da, axis=1, keepdims=True)
    cb = _nt(cm, bm)
    tril = row >= col
    first = lax.broadcasted_iota(jnp.int32, (t, BLK), 1) < 64
    first_row = lax.broadcasted_iota(jnp.int32, (1, BLK), 1) < 64

    def col_form(r):
        return jnp.broadcast_to(r, (t, t)).T

    ys, new_state = [], []
    for p in range(4):
        h0, h1 = 2 * p, 2 * p + 1
        sl = slice(BLK * p, BLK * (p + 1))
        x_pair = xs[:, sl] * jnp.where(first, col_form(dt[h0:h0 + 1]), col_form(dt[h1:h1 + 1]))
        y_pair = None
        cc = []
        for h, keep in ((h0, first), (h1, jnp.logical_not(first))):
            cr = jnp.broadcast_to(cum[h:h + 1], (t, t))
            cc.append(cr.T)
            decay = jnp.exp(jnp.where(tril, cc[-1] - cr, -1e30))
            y_h = _nn(decay * cb, jnp.where(keep, x_pair, 0.0))
            y_pair = y_h if y_pair is None else y_pair + y_h
        s_pair = state[:, sl]
        y_pair = y_pair + _nn(cm, s_pair) * jnp.where(first, jnp.exp(cc[0]), jnp.exp(cc[1]))
        to_end = jnp.where(first, jnp.exp(tot[h0:h0 + 1] - cc[0]), jnp.exp(tot[h1:h1 + 1] - cc[1]))
        chunk_decay = jnp.where(first_row, jnp.exp(tot[h0:h0 + 1]), jnp.exp(tot[h1:h1 + 1]))
        new_state.append(s_pair * chunk_decay + _tn(bm, x_pair * to_end))
        ys.append(y_pair + dsk[:, sl] * xs[:, sl])
    y = jnp.concatenate(ys, axis=1) * _silu(z)
    y = y * lax.rsqrt(jnp.mean(y * y, axis=-1, keepdims=True) + RMS_EPS) * nw
    return y, jnp.concatenate(new_state, axis=1)


def _attn_block(q, kp, kc, vp, vc, sinks, n, kv):
    t = q.shape[0]
    kb = jnp.concatenate([kp, kc], axis=0)
    vb = jnp.concatenate([vp, vc], axis=0)
    qi = lax.broadcasted_iota(jnp.int32, (t, 2 * t), 0)
    kj = lax.broadcasted_iota(jnp.int32, (t, 2 * t), 1)
    rel = qi + t - kj
    valid = (rel >= 0) & (rel < t) & ((n * t - t + kj) >= 0)
    relf = rel.astype(F32)
    first = lax.broadcasted_iota(jnp.int32, (t, BLK), 1) < 64
    lane16 = lax.broadcasted_iota(jnp.int32, (1, ATT_HEADS), 1)
    outs = []
    for p in range(4):
        qp = q[:, BLK * p:BLK * (p + 1)] * (ATT_HEAD_DIM ** -0.5)
        o_pair = None
        for half, keep in enumerate((first, jnp.logical_not(first))):
            h = kv * 8 + 2 * p + half
            s = _nt(jnp.where(keep, qp, 0.0), kb)
            slope = jnp.exp((-8.0 * math.log(2.0) / ATT_HEADS) * jnp.asarray(h + 1, F32))
            s = jnp.where(valid, s - slope * relf, -1e30)
            sink = jnp.sum(jnp.where(lane16 == h, sinks, 0.0), axis=1, keepdims=True)
            m = lax.stop_gradient(jnp.maximum(jnp.max(s, axis=1, keepdims=True), sink))
            e = jnp.exp(s - m)
            den = jnp.sum(e, axis=1, keepdims=True) + jnp.exp(sink - m)
            o = _nn(e * (1.0 / den), vb)
            o_pair = o if o_pair is None else jnp.where(first, o_pair, o)
        outs.append(o_pair)
    return jnp.concatenate(outs, axis=1)


def _mm(a, b, mode, out_dtype, tm, tn, tk, name):
    if mode == "nn":
        (m, k), (k2, n) = a.shape, b.shape
    elif mode == "nt":
        (m, k), (n, k2) = a.shape, b.shape
    else:
        (k, m), (k2, n) = a.shape, b.shape
    assert k == k2, (a.shape, b.shape, mode)
    tm, tn, tk = min(tm, m), min(tn, n), min(tk, k)
    assert m % tm == 0 and n % tn == 0 and k % tk == 0, (m, n, k, tm, tn, tk)
    nk = k // tk
    ca, cb = {"nn": (1, 0), "nt": (1, 1), "tn": (0, 0)}[mode]

    def body(a_ref, b_ref, o_ref, *acc):
        part = _dot(a_ref[...], b_ref[...], ca, cb)
        if nk == 1:
            o_ref[...] = part.astype(o_ref.dtype)
            return
        acc_ref, = acc
        kk = pl.program_id(2)

        @pl.when(kk == 0)
        def _():
            acc_ref[...] = part

        @pl.when(kk > 0)
        def _():
            acc_ref[...] += part

        @pl.when(kk == nk - 1)
        def _():
            o_ref[...] = acc_ref[...].astype(o_ref.dtype)

    a_spec = pl.BlockSpec((tk, tm), lambda i, j, kk: (kk, i)) if mode == "tn" else pl.BlockSpec((tm, tk), lambda i, j, kk: (i, kk))
    b_spec = pl.BlockSpec((tn, tk), lambda i, j, kk: (j, kk)) if mode == "nt" else pl.BlockSpec((tk, tn), lambda i, j, kk: (kk, j))
    return pl.pallas_call(
        body, name=name, grid=(m // tm, n // tn, nk),
        in_specs=[a_spec, b_spec], out_specs=pl.BlockSpec((tm, tn), lambda i, j, kk: (i, j)),
        out_shape=jax.ShapeDtypeStruct((m, n), out_dtype),
        scratch_shapes=[] if nk == 1 else [pltpu.VMEM((tm, tn), F32)],
        compiler_params=_params(3),
    )(a, b)


def _rows(fn, rows, params, outs, accs, tile, name):
    length = rows[0][0].shape[0]
    tile = min(tile, length)
    assert length % tile == 0
    nr, npar, no = len(rows), len(params), len(outs)

    def body(*refs):
        vals = [r[...] for r in refs[:nr + npar]]
        o, a = fn(*vals)
        for ref, val in zip(refs[nr + npar:nr + npar + no], o):
            ref[...] = val.astype(ref.dtype)
        i = pl.program_id(0)
        for ref, val in zip(refs[nr + npar + no:], a):
            @pl.when(i == 0)
            def _(ref=ref, val=val):
                ref[...] = val

            @pl.when(i > 0)
            def _(ref=ref, val=val):
                ref[...] += val

    in_specs = [pl.BlockSpec((tile, w), functools.partial(lambda i, cb: (i, cb), cb=cb)) for _, cb, w in rows]
    in_specs += [pl.BlockSpec(p.shape, lambda i: (0, 0)) for p in params]
    outs = [o if len(o) == 3 else (o[0], o[1], o[0]) for o in outs]
    out_specs = [pl.BlockSpec((tile, w), lambda i: (i, 0)) for w, _, _ in outs]
    out_specs += [pl.BlockSpec((r, w), lambda i: (0, 0)) for r, w in accs]
    out_shape = [jax.ShapeDtypeStruct((length, full), dt) for _, dt, full in outs]
    out_shape += [jax.ShapeDtypeStruct((r, w), F32) for r, w in accs]
    res = pl.pallas_call(
        body, name=name, grid=(length // tile,), in_specs=in_specs, out_specs=out_specs, out_shape=out_shape,
        compiler_params=_params(1),
    )(*[r[0] for r in rows], *params)
    return res[:no], res[no:]


def _conv_fwd(proj, conv_w, conv_b, name):
    length = proj.shape[0]
    tl, cw = min(512, length), 512
    cb0 = _OFF["xs"] // cw

    def body(u_ref, halo_ref, w_ref, b_ref, o_ref, win):
        i = pl.program_id(1)
        win[0:8, :] = jnp.where(i > 0, halo_ref[...], 0.0)
        win[8:8 + tl, :] = u_ref[...]
        acc = b_ref[...] + w_ref[0:1, :] * win[5:5 + tl, :]
        for kk in range(1, SSD_CONV):
            acc = acc + w_ref[kk:kk + 1, :] * win[5 + kk:5 + kk + tl, :]
        o_ref[...] = acc

    return pl.pallas_call(
        body, name=name, grid=(CONV_DIM // cw, length // tl),
        in_specs=[pl.BlockSpec((tl, cw), lambda j, i: (i, cb0 + j)),
                  pl.BlockSpec((8, cw), lambda j, i: (jnp.maximum(i * (tl // 8) - 1, 0), cb0 + j)),
                  pl.BlockSpec((SSD_CONV, cw), lambda j, i: (0, j)),
                  pl.BlockSpec((1, cw), lambda j, i: (0, j))],
        out_specs=pl.BlockSpec((tl, cw), lambda j, i: (i, j)),
        out_shape=jax.ShapeDtypeStruct((length, CONV_DIM), F32),
        scratch_shapes=[pltpu.VMEM((8 + tl, cw), F32)],
        compiler_params=_params(2),
    )(proj, proj, conv_w, conv_b)


def _conv_bwd(dpre, proj, conv_w, dproj, part, name):
    length = proj.shape[0]
    width = dpre.shape[1]
    tl, cw = min(512, length), 512
    cb0 = _OFF["xs"] // cw + part
    n_t = length // tl

    def body(d_ref, dnext_ref, u_ref, halo_ref, w_ref, _, du_ref, dw_ref, db_ref, dwin, uwin):
        i = pl.program_id(1)
        d = d_ref[...]
        dwin[0:tl, :] = d
        dwin[tl:tl + 8, :] = jnp.where(i < n_t - 1, dnext_ref[...], 0.0)
        uwin[0:8, :] = jnp.where(i > 0, halo_ref[...], 0.0)
        uwin[8:8 + tl, :] = u_ref[...]
        du = w_ref[3:4, :] * d
        for kk in range(SSD_CONV - 1):
            du = du + w_ref[kk:kk + 1, :] * dwin[3 - kk:3 - kk + tl, :]
        du_ref[...] = du.astype(du_ref.dtype)
        dws = [jnp.sum(d * uwin[5 + kk:5 + kk + tl, :], axis=0, keepdims=True) for kk in range(SSD_CONV)]
        dw = jnp.concatenate(dws, axis=0)
        db = jnp.sum(d, axis=0, keepdims=True)

        @pl.when(i == 0)
        def _():
            dw_ref[...] = dw
            db_ref[...] = db

        @pl.when(i > 0)
        def _():
            dw_ref[...] += dw
            db_ref[...] += db

    return pl.pallas_call(
        body, name=name, grid=(width // cw, n_t), input_output_aliases={5: 0},
        in_specs=[pl.BlockSpec((tl, cw), lambda j, i: (i, j)),
                  pl.BlockSpec((8, cw), lambda j, i: (jnp.minimum((i + 1) * (tl // 8), length // 8 - 1), j)),
                  pl.BlockSpec((tl, cw), lambda j, i: (i, cb0 + j)),
                  pl.BlockSpec((8, cw), lambda j, i: (jnp.maximum(i * (tl // 8) - 1, 0), cb0 + j)),
                  pl.BlockSpec((SSD_CONV, cw), lambda j, i: (0, part + j)),
                  pl.BlockSpec(memory_space=pl.ANY)],
        out_specs=[pl.BlockSpec((tl, cw), lambda j, i: (i, cb0 + j)),
                   pl.BlockSpec((SSD_CONV, cw), lambda j, i: (0, j)),
                   pl.BlockSpec((1, cw), lambda j, i: (0, j))],
        out_shape=[jax.ShapeDtypeStruct((length, NP), BF),
                   jax.ShapeDtypeStruct((SSD_CONV, width), F32),
                   jax.ShapeDtypeStruct((1, width), F32)],
        scratch_shapes=[pltpu.VMEM((tl + 8, cw), F32), pltpu.VMEM((8 + tl, cw), F32)],
        compiler_params=_params(2),
    )(dpre, dpre, proj, proj, conv_w, dproj)


def _ssd_in_specs(rev, nc):
    def cidx(c):
        return nc - 1 - c if rev else c
    whole = lambda c: (0, 0)
    return [pl.BlockSpec((BLK, SSD_D_INNER), lambda c: (cidx(c), 0)),
            pl.BlockSpec((BLK, BC_DIM), lambda c: (cidx(c), SSD_D_INNER // BC_DIM)),
            pl.BlockSpec((BLK, BC_DIM), lambda c: (cidx(c), SSD_D_INNER // BC_DIM + 1)),
            pl.BlockSpec((BLK, SSD_D_INNER), lambda c: (cidx(c), _OFF["z"] // SSD_D_INNER)),
            pl.BlockSpec((SSD_HEADS, BLK), lambda c: (0, cidx(c))),
            pl.BlockSpec((SSD_HEADS, 1), whole),
            pl.BlockSpec((SSD_HEADS, 1), whole),
            pl.BlockSpec((1, SSD_D_INNER), whole),
            pl.BlockSpec((1, SSD_D_INNER), whole)]


def _group_args(g, px, pb, pc, z, dtr, dtb, alog, dsk, nw):
    wide, narrow, heads = slice(512 * g, 512 * (g + 1)), slice(BLK * g, BLK * (g + 1)), slice(8 * g, 8 * (g + 1))
    return (px[:, wide], pb[:, narrow], pc[:, narrow], z[:, wide], dtr[heads, :], dtb[heads, :], alog[heads, :],
            dsk[:, wide], nw[:, wide])


def _ssd_fwd(pre, proj, dt_t, dtb, alog, dsk, nw, name):
    length = pre.shape[0]
    nc = length // BLK

    def body(px, pb, pc, z, dtr, dtb_r, al_r, dsk_r, nw_r, y_ref, sin_ref, st):
        @pl.when(pl.program_id(0) == 0)
        def _():
            st[...] = jnp.zeros_like(st)

        for g in range(SSD_GROUPS):
            s_in = st[g]
            sin_ref[g] = s_in
            y, s_out = _ssd_group(*_group_args(g, px, pb, pc, z, dtr, dtb_r, al_r, dsk_r, nw_r), s_in)
            y_ref[:, 512 * g:512 * (g + 1)] = y.astype(y_ref.dtype)
            st[g] = s_out

    return pl.pallas_call(
        body, name=name, grid=(nc,), in_specs=_ssd_in_specs(False, nc),
        out_specs=[pl.BlockSpec((BLK, SSD_D_INNER), lambda c: (c, 0)),
                   pl.BlockSpec((SSD_GROUPS, None, SSD_STATE, 512), lambda c: (0, c, 0, 0))],
        out_shape=[jax.ShapeDtypeStruct((length, SSD_D_INNER), BF),
                   jax.ShapeDtypeStruct((SSD_GROUPS, nc, SSD_STATE, 512), F32)],
        scratch_shapes=[pltpu.VMEM((SSD_GROUPS, SSD_STATE, 512), F32)],
        compiler_params=_params(1),
    )(pre, pre, pre, proj, dt_t, dtb, alog, dsk, nw)


def _ssd_bwd(pre, proj, dt_t, dtb, alog, dsk, nw, s_in, dy, dproj, name):
    length = pre.shape[0]
    nc = length // BLK

    def body(px, pb, pc, z, dtr, dtb_r, al_r, dsk_r, nw_r, sin_r, dy_r, _,
             dpx, dpb, dpc, dz, ddt, ddtb, dal, ddsk, dnw, dst):
        c = pl.program_id(0)

        @pl.when(c == 0)
        def _():
            dst[...] = jnp.zeros_like(dst)

        for grp in range(SSD_GROUPS):
            wide, narrow, heads = slice(512 * grp, 512 * (grp + 1)), slice(BLK * grp, BLK * (grp + 1)), slice(8 * grp, 8 * (grp + 1))
            _, vjp = jax.vjp(_ssd_group, *_group_args(grp, px, pb, pc, z, dtr, dtb_r, al_r, dsk_r, nw_r), sin_r[grp])
            g = vjp((dy_r[:, wide], dst[grp]))
            dpx[:, wide] = g[0]
            dpb[:, narrow] = g[1]
            dpc[:, narrow] = g[2]
            dz[:, wide] = g[3].astype(dz.dtype)
            ddt[heads, :] = g[4]
            dst[grp] = g[9]
            for ref, val, idx in ((ddtb, g[5], (heads, slice(None))), (dal, g[6], (heads, slice(None))),
                                  (ddsk, g[7], (slice(None), wide)), (dnw, g[8], (slice(None), wide))):
                @pl.when(c == 0)
                def _(ref=ref, val=val, idx=idx):
                    ref[idx] = val

                @pl.when(c > 0)
                def _(ref=ref, val=val, idx=idx):
                    ref[idx] += val

    rev = lambda c: nc - 1 - c
    whole = lambda c: (0, 0)
    in_specs = _ssd_in_specs(True, nc) + [
        pl.BlockSpec((SSD_GROUPS, None, SSD_STATE, 512), lambda c: (0, rev(c), 0, 0)),
        pl.BlockSpec((BLK, SSD_D_INNER), lambda c: (rev(c), 0)),
        pl.BlockSpec(memory_space=pl.ANY)]
    return pl.pallas_call(
        body, name=name, grid=(nc,), in_specs=in_specs, input_output_aliases={11: 3},
        out_specs=[pl.BlockSpec((BLK, SSD_D_INNER), lambda c: (rev(c), 0)),
                   pl.BlockSpec((BLK, BC_DIM), lambda c: (rev(c), 0)),
                   pl.BlockSpec((BLK, BC_DIM), lambda c: (rev(c), 0)),
                   pl.BlockSpec((BLK, SSD_D_INNER), lambda c: (rev(c), _OFF["z"] // SSD_D_INNER)),
                   pl.BlockSpec((SSD_HEADS, BLK), lambda c: (0, rev(c))),
                   pl.BlockSpec((SSD_HEADS, 1), whole),
                   pl.BlockSpec((SSD_HEADS, 1), whole),
                   pl.BlockSpec((1, SSD_D_INNER), whole),
                   pl.BlockSpec((1, SSD_D_INNER), whole)],
        out_shape=[jax.ShapeDtypeStruct((length, SSD_D_INNER), F32),
                   jax.ShapeDtypeStruct((length, BC_DIM), F32),
                   jax.ShapeDtypeStruct((length, BC_DIM), F32),
                   jax.ShapeDtypeStruct((length, NP), BF),
                   jax.ShapeDtypeStruct((SSD_HEADS, length), F32),
                   jax.ShapeDtypeStruct((SSD_HEADS, 1), F32),
                   jax.ShapeDtypeStruct((SSD_HEADS, 1), F32),
                   jax.ShapeDtypeStruct((1, SSD_D_INNER), F32),
                   jax.ShapeDtypeStruct((1, SSD_D_INNER), F32)],
        scratch_shapes=[pltpu.VMEM((SSD_GROUPS, SSD_STATE, 512), F32)],
        compiler_params=_params(1),
    )(pre, pre, pre, proj, dt_t, dtb, alog, dsk, nw, s_in, dy, dproj)


assert _OFF["gates"] == 0 and _OFF["z"] % SSD_D_INNER == 0 and _OFF["q"] % D_MODEL == 0 and _OFF["k"] + 512 == NP


def _attn_in_specs():
    prev = lambda n: (0, jnp.maximum(n - 1, 0), 0)
    cur = lambda n: (0, n, 0)
    return [pl.BlockSpec((BLK, D_MODEL), lambda n: (n, _OFF["q"] // D_MODEL)),
            pl.BlockSpec((2, BLK, BLK), prev),
            pl.BlockSpec((2, BLK, BLK), cur),
            pl.BlockSpec((2, BLK, BLK), prev),
            pl.BlockSpec((2, BLK, BLK), cur),
            pl.BlockSpec((1, ATT_HEADS), lambda n: (0, 0))]


def _attn_fwd(proj, k2, v2, sinks, name):
    length = proj.shape[0]

    def body(q, kp, kc, vp, vc, sk, o_ref):
        for kv in range(2):
            cols = slice(512 * kv, 512 * (kv + 1))
            o = _attn_block(q[:, cols], kp[kv], kc[kv], vp[kv], vc[kv], sk[...], pl.program_id(0), kv)
            o_ref[:, cols] = o.astype(o_ref.dtype)

    return pl.pallas_call(
        body, name=name, grid=(length // BLK,), in_specs=_attn_in_specs(),
        out_specs=pl.BlockSpec((BLK, D_MODEL), lambda n: (n, 0)),
        out_shape=jax.ShapeDtypeStruct((length, D_MODEL), BF),
        compiler_params=_params(1),
    )(proj, k2, k2, v2, v2, sinks)


def _attn_bwd(proj, k2, v2, sinks, datt, dproj, name):
    length = proj.shape[0]

    def body(q, kp, kc, vp, vc, sk, do, _, dq, dkp, dkc, dvp, dvc, dsk):
        n = pl.program_id(0)
        dsinks = None
        for kv in range(2):
            cols = slice(512 * kv, 512 * (kv + 1))
            _, vjp = jax.vjp(lambda *a: _attn_block(*a, n, kv), q[:, cols], kp[kv], kc[kv], vp[kv], vc[kv], sk[...])
            g = vjp(do[:, cols])
            dq[:, cols] = g[0].astype(dq.dtype)
            dkp[kv] = g[1]
            dkc[kv] = g[2]
            dvp[kv] = g[3]
            dvc[kv] = g[4]
            dsinks = g[5] if dsinks is None else dsinks + g[5]

        @pl.when(n == 0)
        def _():
            dsk[...] = dsinks

        @pl.when(n > 0)
        def _():
            dsk[...] += dsinks

    blk3 = pl.BlockSpec((2, BLK, BLK), lambda n: (0, n, 0))
    kv_shape = jax.ShapeDtypeStruct((2, length, BLK), F32)
    return pl.pallas_call(
        body, name=name, grid=(length // BLK,),
        in_specs=_attn_in_specs() + [pl.BlockSpec((BLK, D_MODEL), lambda n: (n, 0)), pl.BlockSpec(memory_space=pl.ANY)],
        out_specs=[pl.BlockSpec((BLK, D_MODEL), lambda n: (n, _OFF["q"] // D_MODEL)), blk3, blk3, blk3, blk3,
                   pl.BlockSpec((1, ATT_HEADS), lambda n: (0, 0))],
        out_shape=[jax.ShapeDtypeStruct((length, NP), BF), kv_shape, kv_shape, kv_shape, kv_shape,
                   jax.ShapeDtypeStruct((1, ATT_HEADS), F32)],
        input_output_aliases={7: 0},
        compiler_params=_params(1),
    )(proj, k2, k2, v2, v2, sinks, datt, dproj)


DMA_CHUNK_BYTES = 1 << 20
N_STAGE = 8
LOOKAHEAD = 3


def _piece_chunks(shape, itemsize):
    if len(shape) < 2 or shape[-2] % 16 != 0:
        return [()]
    rows, cols = shape[-2:]
    step = min(rows, max(16, DMA_CHUNK_BYTES // (cols * itemsize) // 16 * 16))
    out = []
    for lead in itertools.product(*[range(d) for d in shape[:-2]]):
        for r0 in range(0, rows, step):
            out.append(lead + (pl.ds(r0, min(step, rows - r0)),))
    return out


def _exchange(srcs, group, bcast, name):
    size = {"c": 2, "xy": 4, "all": 8}[group]
    npeer = size - 1
    n = len(srcs)
    for s in srcs:
        assert bcast or s.shape[0] == size
    pieces = [s.shape if bcast else s.shape[1:] for s in srcs]
    chunks = [_piece_chunks(p, s.dtype.itemsize) for p, s in zip(pieces, srcs)]

    def body(*refs):
        src, out = refs[:n], refs[n:2 * n]
        load_sems, send_sems, recv_sems = refs[2 * n:]
        x, y, c = lax.axis_index("x"), lax.axis_index("y"), lax.axis_index("c")
        if group == "c":
            me = c
        elif group == "xy":
            me = 2 * x + y
        else:
            me = 4 * x + 2 * y + c

        def device(j):
            if group == "c":
                return (x, y, j)
            if group == "xy":
                return (j // 2, j % 2, c)
            return (j // 4, (j // 2) % 2, j % 2)

        def part(ref, idx):
            return ref.at[idx] if idx else ref

        def piece(a, j):
            return src[a] if bcast else src[a].at[j]

        for a in range(n):
            jobs = [(None, idx) for idx in chunks[a]] if bcast else [(d, idx) for idx in chunks[a] for d in range(1, size)]
            full_rows = chunks[a][0][-1].size if chunks[a][0] else None
            slot_shape = (full_rows, pieces[a][-1]) if chunks[a][0] else pieces[a]

            def stream(stage, a=a, jobs=jobs, full_rows=full_rows):
                def slot(q):
                    idx = jobs[q][1]
                    view = stage.at[q % N_STAGE]
                    if idx and idx[-1].size != full_rows:
                        view = view.at[pl.ds(0, idx[-1].size)]
                    return view

                def load(q):
                    d, idx = jobs[q]
                    j = me if d is None else jnp.bitwise_xor(me, d)
                    return pltpu.make_async_copy(part(piece(a, j), idx), slot(q), load_sems.at[q % N_STAGE])

                def sends(q):
                    d, idx = jobs[q]
                    return [pltpu.make_async_remote_copy(
                        src_ref=slot(q), dst_ref=part(out[a].at[me], idx),
                        send_sem=send_sems.at[(q % N_STAGE) * npeer + dd - 1], recv_sem=recv_sems.at[a * size + dd],
                        device_id=device(jnp.bitwise_xor(me, dd)), device_id_type=pl.DeviceIdType.MESH)
                        for dd in (range(1, size) if d is None else (d,))]

                nq = len(jobs)
                for q in range(nq + LOOKAHEAD):
                    if q < nq:
                        if q >= N_STAGE:
                            for cp in sends(q - N_STAGE):
                                cp.wait_send()
                        load(q).start()
                    if q >= LOOKAHEAD:
                        load(q - LOOKAHEAD).wait()
                        for cp in sends(q - LOOKAHEAD):
                            cp.start()
                for q in range(max(0, nq - N_STAGE), nq):
                    for cp in sends(q):
                        cp.wait_send()

            pl.run_scoped(stream, pltpu.VMEM((N_STAGE,) + tuple(slot_shape), srcs[a].dtype))

        for a in range(n):
            for d in range(1, size):
                j = jnp.bitwise_xor(me, d)
                pltpu.make_async_remote_copy(
                    src_ref=piece(a, j), dst_ref=out[a].at[j], send_sem=send_sems.at[0], recv_sem=recv_sems.at[a * size + d],
                    device_id=device(j), device_id_type=pl.DeviceIdType.MESH).wait_recv()

    any_spec = pl.BlockSpec(memory_space=pl.ANY)
    return pl.pallas_call(
        body, name=name, in_specs=[any_spec] * n, out_specs=[any_spec] * n,
        out_shape=[jax.ShapeDtypeStruct((size,) + tuple(p), s.dtype) for p, s in zip(pieces, srcs)],
        scratch_shapes=[pltpu.SemaphoreType.DMA((N_STAGE,)), pltpu.SemaphoreType.DMA((N_STAGE * npeer,)),
                        pltpu.SemaphoreType.DMA((n * size,))],
        compiler_params=pltpu.CompilerParams(vmem_limit_bytes=VMEM_LIMIT_BYTES),
    )(*srcs)


def _with_own(outs, owns, me):
    return [lax.dynamic_update_index_in_dim(o, w, me, 0) for o, w in zip(outs, owns)]


def _sum_slots(arr, name):
    k = arr.shape[0]
    rest = arr.shape[1:]
    width = rest[-1]
    rows_per = math.prod(rest[:-1])
    a2 = arr.reshape(k * rows_per, width)
    tile = rows_per
    for cand in (256, 128, 64, 32, 16, 8):
        if rows_per % cand == 0:
            tile = cand
            break
    nt = rows_per // tile

    def body(*refs):
        acc = refs[0][...]
        for r in refs[1:k]:
            acc = acc + r[...]
        refs[k][...] = acc

    in_specs = [pl.BlockSpec((tile, width), functools.partial(lambda i, s: (s * nt + i, 0), s=s)) for s in range(k)]
    out = pl.pallas_call(
        body, name=name, grid=(nt,), in_specs=in_specs, out_specs=pl.BlockSpec((tile, width), lambda i: (i, 0)),
        out_shape=jax.ShapeDtypeStruct((rows_per, width), arr.dtype), compiler_params=_params(1),
    )(*([a2] * k))
    return out.reshape(rest)


def _sum_pieces(recv, own, own_slotted, me, name):
    k = recv.shape[0]
    rest = recv.shape[1:]
    width = rest[-1]
    rows_per = math.prod(rest[:-1])
    r2 = recv.reshape(k * rows_per, width)
    o2 = own.reshape(-1, width)
    tile = rows_per
    for cand in (256, 128, 64, 32, 16, 8):
        if rows_per % cand == 0:
            tile = cand
            break
    nt = rows_per // tile

    def body(me_ref, *refs):
        mine = refs[k][...].astype(F32)
        acc = None
        for j in range(k):
            term = jnp.where(me_ref[0] == j, mine, refs[j][...].astype(F32))
            acc = term if acc is None else acc + term
        refs[k + 1][...] = acc

    in_specs = [pl.BlockSpec((tile, width), functools.partial(lambda i, m, s: (s * nt + i, 0), s=s)) for s in range(k)]
    in_specs.append(pl.BlockSpec((tile, width), (lambda i, m: (m[0] * nt + i, 0)) if own_slotted else (lambda i, m: (i, 0))))
    out = pl.pallas_call(
        body, name=name,
        grid_spec=pltpu.PrefetchScalarGridSpec(
            num_scalar_prefetch=1, grid=(nt,), in_specs=in_specs,
            out_specs=pl.BlockSpec((tile, width), lambda i, m: (i, 0))),
        out_shape=jax.ShapeDtypeStruct((rows_per, width), F32), compiler_params=_params(1),
    )(jnp.reshape(me, (1,)).astype(jnp.int32), *([r2] * k), o2)
    return out.reshape(rest)


def _adamw(w, g, m, v, name):
    def fn(w_, g_, m_, v_):
        m1 = ADAM_B1 * m_ + (1.0 - ADAM_B1) * g_
        v1 = ADAM_B2 * v_ + (1.0 - ADAM_B2) * (g_ * g_)
        m_hat = m1 / (1.0 - ADAM_B1 ** ADAM_STEP)
        v_hat = v1 / (1.0 - ADAM_B2 ** ADAM_STEP)
        delta = -ADAM_LR * (m_hat / (jnp.sqrt(v_hat) + ADAM_EPS) + ADAM_WD * w_)
        return (delta, m1, v1), ()

    rows, width = w.shape
    tile = rows
    for cand in (256, 128, 64, 32, 16, 8):
        if rows % cand == 0:
            tile = cand
            break
    o, _ = _rows(fn, [(w, 0, width), (g, 0, width), (m, 0, width), (v, 0, width)], [],
                 [(width, F32)] * 3, [], tile, name)
    return o


def _pack_cols(w):
    parts = [w[..., o:o + wd] for _, o, wd in _PACK]
    parts.append(jnp.zeros(w.shape[:-1] + (_PAD,), w.dtype))
    return jnp.concatenate(parts, axis=-1)


def _unpack_cols(wp):
    order = sorted(_PACK, key=lambda e: e[1])
    return jnp.concatenate([wp[..., _OFF[nm]:_OFF[nm] + wd] for nm, _, wd in order], axis=-1)


def _col_sharded_full(g):
    return jnp.transpose(g, (1, 0, 2)).reshape(g.shape[1], -1)


def _col_sharded_split(full):
    r, cdim = full.shape
    return jnp.transpose(full.reshape(r, 4, cdim // 4), (1, 0, 2))


def _double_heads(t):
    length = t.shape[0]
    h = jnp.transpose(t.reshape(length, 2, 64), (1, 0, 2))
    return jnp.concatenate([h, h], axis=-1)


def _fold_heads(d_cur, d_prev):
    length = d_cur.shape[1]
    d = d_cur + jnp.concatenate([d_prev[:, BLK:], jnp.zeros((2, BLK, BLK), F32)], axis=1)
    d = d[..., :64] + d[..., 64:]
    return jnp.transpose(d, (1, 0, 2)).reshape(length, 128)


def _ln_fwd(a, r, g, b, name):
    def fn(a_, r_, g_, b_):
        y = _layer_norm(ALPHA * a_ + r_, g_, b_)
        return (y, y), ()
    o, _ = _rows(fn, [(a, 0, D_MODEL), (r, 0, D_MODEL)], [g, b], [(D_MODEL, F32), (D_MODEL, BF)], [], 256, name)
    return o


def _ln_bwd(a, r, d1, d2, g, b, name):
    def fn(a_, r_, d1_, d2_, g_, b_):
        _, vjp = jax.vjp(_layer_norm, ALPHA * a_ + r_, g_, b_)
        ds, dg, db = vjp(ALPHA * d1_ + d2_)
        return (ds, ds), (dg, db)
    o, acc = _rows(fn, [(a, 0, D_MODEL), (r, 0, D_MODEL), (d1, 0, D_MODEL), (d2, 0, D_MODEL)], [g, b],
                   [(D_MODEL, F32), (D_MODEL, BF)], [(1, D_MODEL), (1, D_MODEL)], 256, name)
    return o[0], o[1], acc[0], acc[1]


def _layer_fwd(l, h, hb, w, p):
    sv = {"h": h, "hb": hb}
    proj = _mm(hb, w["win"], "nn", F32, 2048, 512, 1024, f"proj{l}")
    sv["proj"] = proj
    pre = _conv_fwd(proj, p["conv_w"], p["conv_b"], f"conv{l}")
    sv["pre"] = pre
    dt_t = jnp.transpose(proj[:, _OFF["dt"]:_OFF["dt"] + SSD_HEADS])
    sv["dt_t"] = dt_t
    yn, s_in = _ssd_fwd(pre, proj, dt_t, p["dt_bias"], p["a_log"], p["d_skip_c"], p["ssd_norm_w"], f"ssd{l}")
    sv["yn"], sv["s_in"] = yn, s_in
    ya = _mm(yn, w["wso"], "nn", F32, 1024, 1024, 2048, f"ssdout{l}")
    k2 = _double_heads(proj[:, _OFF["k"]:_OFF["k"] + 128])
    v2 = _double_heads(proj[:, _OFF["v"]:_OFF["v"] + 128])
    sv["k2"], sv["v2"] = k2, v2
    att = _attn_fwd(proj, k2, v2, p["att_sinks"], f"attn{l}")
    sv["att"] = att
    yb = _mm(att, w["wao"], "nn", F32, 1024, 1024, 1024, f"attout{l}")
    sv["ya"], sv["yb"] = ya, yb
    gcb = _OFF["gates"] // 1024

    def gate_fn(ga, gb, ya_, yb_):
        return (_sigmoid(ga) * ya_ + _sigmoid(gb) * yb_,), ()
    (ub,), _ = _rows(gate_fn, [(proj, gcb, 1024), (proj, gcb + 1, 1024), (ya, 0, 1024), (yb, 0, 1024)], [],
                     [(D_MODEL, BF)], [], 256, f"gate{l}")
    sv["ub"] = ub
    mix = _mm(ub, w["wmo"], "nn", F32, 1024, 1024, 1024, f"mixout{l}")
    sv["mix"] = mix
    h1, h1b = _ln_fwd(h, mix, p["ln_mix_g"], p["ln_mix_b"], f"lnmix{l}")
    sv["h1"], sv["h1b"] = h1, h1b
    gu = _mm(h1b, w["wgu"], "nn", F32, 2048, 512, 1024, f"ffnin{l}")
    sv["gu"] = gu

    def act_fn(g_, u_):
        return (_silu(g_) * u_,), ()
    (act,), _ = _rows(act_fn, [(gu, 0, FFN_HIDDEN), (gu, 1, FFN_HIDDEN)], [], [(FFN_HIDDEN, BF)], [], 256, f"swiglu{l}")
    sv["act"] = act
    ffn = _mm(act, w["wd"], "nn", F32, 1024, 1024, FFN_HIDDEN, f"ffnout{l}")
    sv["ffn"] = ffn
    return sv


def _layer_bwd(l, sv, w, p, ds_f, ds_b):
    g = {}
    g["w_ffn_down"] = _mm(sv["act"], ds_b, "tn", F32, 1408, 1024, 1024, f"dwd{l}")
    dact = _mm(ds_b, w["wd"], "nt", F32, 1024, FFN_HIDDEN, 1024, f"dact{l}")

    def act_bwd(g_, u_, d_):
        _, vjp = jax.vjp(lambda a, b: _silu(a) * b, g_, u_)
        dg_, du_ = vjp(d_)
        return (jnp.concatenate([dg_, du_], axis=1),), ()
    (dgu,), _ = _rows(act_bwd, [(sv["gu"], 0, FFN_HIDDEN), (sv["gu"], 1, FFN_HIDDEN), (dact, 0, FFN_HIDDEN)], [],
                      [(2 * FFN_HIDDEN, BF)], [], 128, f"dswiglu{l}")
    dwgu = _mm(sv["h1b"], dgu, "tn", F32, 1024, 2816, 512, f"dwgu{l}")
    g["w_ffn_gate"], g["w_ffn_up"] = dwgu[:, :FFN_HIDDEN], dwgu[:, FFN_HIDDEN:]
    dh1 = _mm(dgu, w["wgu"], "nt", F32, 1024, 1024, 2816, f"dh1{l}")
    ds2_f, ds2_b, g["ln_mix_g"], g["ln_mix_b"] = _ln_bwd(sv["h"], sv["mix"], ds_f, dh1, p["ln_mix_g"], p["ln_mix_b"], f"dlnmix{l}")
    g["w_mix_out"] = _mm(sv["ub"], ds2_b, "tn", F32, 1024, 1024, 512, f"dwmo{l}")
    du = _mm(ds2_b, w["wmo"], "nt", F32, 1024, 1024, 1024, f"du{l}")
    proj = sv["proj"]
    gcb = _OFF["gates"] // 1024

    def gate_bwd(ga, gb, ya_, yb_, du_):
        _, vjp = jax.vjp(lambda a, b, c, d: _sigmoid(a) * c + _sigmoid(b) * d, ga, gb, ya_, yb_)
        dga, dgb, dya, dyb = vjp(du_)
        return (jnp.concatenate([dga, dgb], axis=1), dya, dyb), ()
    (dproj, dya, dyb), _ = _rows(
        gate_bwd, [(proj, gcb, 1024), (proj, gcb + 1, 1024), (sv["ya"], 0, 1024), (sv["yb"], 0, 1024), (du, 0, 1024)], [],
        [(2048, BF, NP), (D_MODEL, BF), (D_MODEL, BF)], [], 256, f"dgate{l}")
    g["w_att_out"] = _mm(sv["att"], dyb, "tn", F32, 1024, 1024, 512, f"dwao{l}")
    datt = _mm(dyb, w["wao"], "nt", F32, 1024, 1024, 1024, f"datt{l}")
    g["w_ssd_out"] = _mm(sv["yn"], dya, "tn", F32, 1024, 1024, 512, f"dwso{l}")
    dyn = _mm(dya, w["wso"], "nt", F32, 1024, 1024, 1024, f"dyn{l}")
    dproj, dkp, dkc, dvp, dvc, g["att_sinks"] = _attn_bwd(proj, sv["k2"], sv["v2"], p["att_sinks"], datt, dproj, f"dattn{l}")
    (dpx, dpb, dpc, dproj, ddt_t, g["dt_bias"], g["a_log"], ddsk, g["ssd_norm_w"]) = _ssd_bwd(
        sv["pre"], proj, sv["dt_t"], p["dt_bias"], p["a_log"], p["d_skip_c"], p["ssd_norm_w"], sv["s_in"], dyn, dproj, f"dssd{l}")
    g["d_skip"] = jnp.sum(ddsk.reshape(SSD_HEADS, 64), axis=1)
    dws, dbs = [], []
    for nm, dpart, part in (("x", dpx, 0), ("b", dpb, 4), ("c", dpc, 5)):
        dproj, dw_, db_ = _conv_bwd(dpart, proj, p["conv_w"], dproj, part, f"dconv{nm}{l}")
        dws.append(dw_)
        dbs.append(db_)
    g["conv_w"], g["conv_b"] = jnp.concatenate(dws, axis=1), jnp.concatenate(dbs, axis=1)
    length = proj.shape[0]
    tail = jnp.concatenate([_fold_heads(dkc, dkp).astype(BF), _fold_heads(dvc, dvp).astype(BF),
                            jnp.transpose(ddt_t).astype(BF), jnp.zeros((length, _PAD), BF)], axis=1)
    dproj = lax.dynamic_update_slice(dproj, tail, (0, _OFF["k"]))
    g["w_in"] = _unpack_cols(_mm(sv["hb"], dproj, "tn", F32, 1024, 2176, 1024, f"dwin{l}"))
    dh = _mm(dproj, w["win"], "nt", F32, 1024, 1024, 2176, f"dh{l}")
    return g, ds2_f, dh


def kernel(x, ln_in_g, ln_in_b, w_in, conv_w, conv_b, dt_bias, a_log, d_skip, ssd_norm_w, att_sinks, w_ssd_out, w_att_out, w_mix_out, ln_mix_g, ln_mix_b, w_ffn_gate, w_ffn_up, w_ffn_down, ln_ffn_g, ln_ffn_b, loss_target, m_ln_in_g, m_ln_in_b, m_w_in, m_conv_w, m_conv_b, m_dt_bias, m_a_log, m_d_skip, m_ssd_norm_w, m_att_sinks, m_w_ssd_out, m_w_att_out, m_w_mix_out, m_ln_mix_g, m_ln_mix_b, m_w_ffn_gate, m_w_ffn_up, m_w_ffn_down, m_ln_ffn_g, m_ln_ffn_b, v_ln_in_g, v_ln_in_b, v_w_in, v_conv_w, v_conv_b, v_dt_bias, v_a_log, v_d_skip, v_ssd_norm_w, v_att_sinks, v_w_ssd_out, v_w_att_out, v_w_mix_out, v_ln_mix_g, v_ln_mix_b, v_w_ffn_gate, v_w_ffn_up, v_w_ffn_down, v_ln_ffn_g, v_ln_ffn_b):
    env = dict(locals())
    wts = {n: env[n] for n in WEIGHTS}
    mom1 = {n: env["m_" + n] for n in WEIGHTS}
    mom2 = {n: env["v_" + n] for n in WEIGHTS}
    ci = lax.axis_index("c")
    chip = 2 * lax.axis_index("x") + lax.axis_index("y")
    xs_ = x[0]
    tgt = loss_target[0]

    shards = [wts[n].astype(BF) for n in BIG] + [conv_w]
    mine = [lax.dynamic_index_in_dim(s, ci, 0, keepdims=False) for s in shards]
    by_chip = _with_own(_exchange(mine, "xy", True, "gather_chips"), mine, chip)
    by_layer = _with_own(_exchange(by_chip, "c", True, "gather_cores"), by_chip, ci)
    full = dict(zip(BIG + ("conv_w",), by_layer))

    layer_w, layer_p = [], []
    for l in range(DEPTH):
        layer_w.append({
            "win": _pack_cols(_col_sharded_full(full["w_in"][l])),
            "wso": full["w_ssd_out"][l].reshape(SSD_D_INNER, D_MODEL),
            "wao": full["w_att_out"][l].reshape(D_MODEL, D_MODEL),
            "wmo": full["w_mix_out"][l].reshape(D_MODEL, D_MODEL),
            "wgu": jnp.concatenate([_col_sharded_full(full["w_ffn_gate"][l]), _col_sharded_full(full["w_ffn_up"][l])], axis=1),
            "wd": full["w_ffn_down"][l].reshape(FFN_HIDDEN, D_MODEL),
        })
        layer_p.append({
            "conv_w": _col_sharded_full(full["conv_w"][l]), "conv_b": conv_b[l][None],
            "dt_bias": dt_bias[l][:, None], "a_log": a_log[l][:, None],
            "d_skip_c": jnp.repeat(d_skip[l], 64)[None], "ssd_norm_w": ssd_norm_w[l][None],
            "att_sinks": att_sinks[l][None], "ln_mix_g": ln_mix_g[l][None], "ln_mix_b": ln_mix_b[l][None],
            "ln_ffn_g": ln_ffn_g[l][None], "ln_ffn_b": ln_ffn_b[l][None],
        })

    def ln_in_fn(x_, g_, b_):
        y = _layer_norm(x_, g_, b_)
        return (y, y), ()
    (h, hb), _ = _rows(ln_in_fn, [(xs_, 0, D_MODEL)], [ln_in_g[None], ln_in_b[None]], [(D_MODEL, F32), (D_MODEL, BF)], [], 256, "ln_in")
    saved = []
    for l in range(DEPTH):
        sv = _layer_fwd(l, h, hb, layer_w[l], layer_p[l])
        saved.append(sv)
        if l < DEPTH - 1:
            h, hb = _ln_fwd(sv["h1"], sv["ffn"], layer_p[l]["ln_ffn_g"], layer_p[l]["ln_ffn_b"], f"lnffn{l}")

    def loss_fn(a_, r_, t_, g_, b_):
        y, vjp = jax.vjp(_layer_norm, ALPHA * a_ + r_, g_, b_)
        err = y - t_
        ds, dg, db = vjp(err * (1.0 / D_MODEL))
        part = 0.5 * jnp.sum(jnp.mean(err * err, axis=-1, keepdims=True), axis=0, keepdims=True)
        return (ds, ds), (dg, db, jnp.broadcast_to(part, (1, BLK)))
    sv = saved[-1]
    (ds_f, ds_b), (dg_last, db_last, loss_part) = _rows(
        loss_fn, [(sv["h1"], 0, D_MODEL), (sv["ffn"], 0, D_MODEL), (tgt, 0, D_MODEL)],
        [layer_p[-1]["ln_ffn_g"], layer_p[-1]["ln_ffn_b"]], [(D_MODEL, F32), (D_MODEL, BF)],
        [(1, D_MODEL), (1, D_MODEL), (1, BLK)], 256, "loss")
    loss = lax.psum(loss_part[0, 0], ("x", "y", "c"))

    grads = [None] * DEPTH
    for l in reversed(range(DEPTH)):
        sv = saved[l]
        if l == DEPTH - 1:
            lg, lb = dg_last, db_last
        else:
            ds_f, ds_b, lg, lb = _ln_bwd(sv["h1"], sv["ffn"], d1, d2, layer_p[l]["ln_ffn_g"], layer_p[l]["ln_ffn_b"], f"dlnffn{l}")
        g, d1, d2 = _layer_bwd(l, sv, layer_w[l], layer_p[l], ds_f, ds_b)
        g["ln_ffn_g"], g["ln_ffn_b"] = lg, lb
        grads[l] = g

    def ln_in_bwd(x_, d1_, d2_, g_, b_):
        _, vjp = jax.vjp(_layer_norm, x_, g_, b_)
        dx, dg, db = vjp(ALPHA * d1_ + d2_)
        return (dx,), (dg, db)
    (grad_x,), (g_ln_in_g, g_ln_in_b) = _rows(
        ln_in_bwd, [(xs_, 0, D_MODEL), (d1, 0, D_MODEL), (d2, 0, D_MODEL)], [ln_in_g[None], ln_in_b[None]],
        [(D_MODEL, F32)], [(1, D_MODEL), (1, D_MODEL)], 256, "dln_in")

    def by_shard(name, gfull):
        if name in ("w_in", "w_ffn_gate", "w_ffn_up"):
            return _col_sharded_split(gfull)
        return gfull.reshape((4, gfull.shape[0] // 4) + gfull.shape[1:])
    big_g = [jnp.stack([by_shard(n, grads[l][n]) for l in range(DEPTH)]) for n in BIG]
    from_cores = _exchange(big_g, "c", False, "reduce_cores")
    chip_sum = [_sum_pieces(r, o, True, ci, f"sum_cores{i}") for i, (r, o) in enumerate(zip(from_cores, big_g))]
    from_chips = _exchange([a.astype(BF) for a in chip_sum], "xy", False, "reduce_chips")
    reduced = [_sum_pieces(r, o, True, chip, f"sum_chips{i}") for i, (r, o) in enumerate(zip(from_chips, chip_sum))]
    both = _with_own(_exchange(reduced, "c", True, "swap_layers"), reduced, ci)
    big_grad = dict(zip(BIG, both))

    small_g = {"ln_in_g": g_ln_in_g[0], "ln_in_b": g_ln_in_b[0]}
    for n in SMALL[2:]:
        small_g[n] = jnp.stack([grads[l][n].reshape(wts[n].shape[1:] if n != "conv_w" else (SSD_CONV, CONV_DIM)) for l in range(DEPTH)])
    flat = jnp.concatenate([small_g[n].reshape(-1) for n in SMALL])
    n_small = flat.shape[0]
    width = -(-n_small // 1024) * 1024
    flat = jnp.pad(flat, (0, width - n_small)).reshape(8, width // 8)
    gathered, = _with_own(_exchange([flat], "all", True, "gather_small"), [flat], 2 * chip + ci)
    total = _sum_slots(gathered, "sum_small").reshape(-1)
    small_grad, off = {}, 0
    for n in SMALL:
        shp = small_g[n].shape
        cnt = math.prod(shp)
        small_grad[n] = total[off:off + cnt].reshape(shp)
        off += cnt
    small_grad["conv_w"] = lax.dynamic_slice_in_dim(small_grad["conv_w"], chip * (CONV_DIM // 4), CONV_DIM // 4, axis=2)

    out_g, out_d, out_m, out_v = {}, {}, {}, {}
    for n in BIG:
        shp = wts[n].shape
        two_d = lambda t: t.reshape(shp[0] * shp[1], shp[2])
        d_, m_, v_ = _adamw(two_d(wts[n]), two_d(big_grad[n]), two_d(mom1[n]), two_d(mom2[n]), f"adamw_{n}")
        out_g[n], out_d[n], out_m[n], out_v[n] = big_grad[n], d_.reshape(shp), m_.reshape(shp), v_.reshape(shp)

    def flat_small(d):
        f = jnp.concatenate([d[n].reshape(-1) for n in SMALL])
        return jnp.pad(f, (0, swidth - f.shape[0])).reshape(8, swidth // 8)
    n_sw = sum(math.prod(wts[n].shape) for n in SMALL)
    swidth = -(-n_sw // 1024) * 1024
    d_, m_, v_ = _adamw(flat_small(wts), flat_small(small_grad), flat_small(mom1), flat_small(mom2), "adamw_small")
    off = 0
    for n in SMALL:
        shp = wts[n].shape
        cnt = math.prod(shp)
        out_g[n] = small_grad[n]
        out_d[n] = d_.reshape(-1)[off:off + cnt].reshape(shp)
        out_m[n] = m_.reshape(-1)[off:off + cnt].reshape(shp)
        out_v[n] = v_.reshape(-1)[off:off + cnt].reshape(shp)
        off += cnt

    return (loss, grad_x[None], *[out_g[n] for n in WEIGHTS], *[out_d[n] for n in WEIGHTS],
            *[out_m[n] for n in WEIGHTS], *[out_v[n] for n in WEIGHTS])
```

```python
import functools
import itertools
import math

import jax
import jax.numpy as jnp
from jax import lax
from jax.experimental import pallas as pl
from jax.experimental.pallas import tpu as pltpu

F32 = jnp.float32
BF = jnp.bfloat16

D_MODEL = 1024
DEPTH = 2
ATT_HEADS = 16
ATT_HEAD_DIM = 64
BLK = 128
SSD_D_INNER = 2048
SSD_HEADS = 32
SSD_GROUPS = 4
SSD_STATE = 128
SSD_CONV = 4
BC_DIM = 512
CONV_DIM = 3072
FFN_HIDDEN = 2816
IN_DIM = 8480
LN_EPS = 1e-5
RMS_EPS = 1e-5
ALPHA = (2 * DEPTH) ** 0.25
ADAM_LR = 0.001
ADAM_B1 = 0.9
ADAM_B2 = 0.999
ADAM_EPS = 1e-08
ADAM_WD = 0.01
ADAM_STEP = 10

_PACK = (("gates", 6432, 2048), ("z", 1280, 2048), ("q", 0, 1024), ("xs", 3328, 2048), ("B", 5376, 512),
         ("C", 5888, 512), ("k", 1024, 128), ("v", 1152, 128), ("dt", 6400, 32))
NP = 8704
_OFF = {}
_o = 0
for _n, _, _w in _PACK:
    _OFF[_n] = _o
    _o += _w
_PAD = NP - _o

VMEM_LIMIT_BYTES = 56 * 1024 * 1024
BIG = ("w_in", "w_ssd_out", "w_att_out", "w_mix_out", "w_ffn_gate", "w_ffn_up", "w_ffn_down")
WEIGHTS = ("ln_in_g", "ln_in_b", "w_in", "conv_w", "conv_b", "dt_bias", "a_log", "d_skip", "ssd_norm_w", "att_sinks",
           "w_ssd_out", "w_att_out", "w_mix_out", "ln_mix_g", "ln_mix_b", "w_ffn_gate", "w_ffn_up", "w_ffn_down",
           "ln_ffn_g", "ln_ffn_b")
SMALL = tuple(n for n in WEIGHTS if n not in BIG)


def _params(n_grid):
    return pltpu.CompilerParams(dimension_semantics=("arbitrary",) * n_grid, vmem_limit_bytes=VMEM_LIMIT_BYTES)


def _dot(a, b, ca, cb):
    return lax.dot_general(a.astype(BF), b.astype(BF), (((ca,), (cb,)), ((), ())), preferred_element_type=F32)


@jax.custom_vjp
def _nn(a, b):
    return _dot(a, b, 1, 0)


def _nn_f(a, b):
    return _dot(a, b, 1, 0), (a, b)


def _nn_b(res, g):
    a, b = res
    return _dot(g, b, 1, 1).astype(a.dtype), _dot(a, g, 0, 0).astype(b.dtype)


_nn.defvjp(_nn_f, _nn_b)


@jax.custom_vjp
def _nt(a, b):
    return _dot(a, b, 1, 1)


def _nt_f(a, b):
    return _dot(a, b, 1, 1), (a, b)


def _nt_b(res, g):
    a, b = res
    return _dot(g, b, 1, 0).astype(a.dtype), _dot(g, a, 0, 0).astype(b.dtype)


_nt.defvjp(_nt_f, _nt_b)


@jax.custom_vjp
def _tn(a, b):
    return _dot(a, b, 0, 0)


def _tn_f(a, b):
    return _dot(a, b, 0, 0), (a, b)


def _tn_b(res, g):
    a, b = res
    return _dot(b, g, 1, 1).astype(a.dtype), _dot(a, g, 1, 0).astype(b.dtype)


_tn.defvjp(_tn_f, _tn_b)


def _sigmoid(x):
    return 0.5 * jnp.tanh(0.5 * x) + 0.5


def _silu(x):
    return x * _sigmoid(x)


def _layer_norm(s, g, b):
    mu = jnp.mean(s, axis=-1, keepdims=True)
    sc = s - mu
    var = jnp.mean(sc * sc, axis=-1, keepdims=True)
    return sc * lax.rsqrt(var + LN_EPS) * g + b


def _ssd_group(pre_x, pre_b, pre_c, z, dtr, dtb, alog, dsk, nw, state):
    t = pre_x.shape[0]
    xs, bm, cm = _silu(pre_x), _silu(pre_b), _silu(pre_c)
    dt = jax.nn.softplus(dtr + dtb)
    da = dt * (-jnp.exp(alog))
    row = lax.broadcasted_iota(jnp.int32, (t, t), 0)
    col = lax.broadcasted_iota(jnp.int32, (t, t), 1)
    upper = (row <= col).astype(F32)
    cum = jnp.dot(da, upper, precision=lax.Precision.HIGHEST, preferred_element_type=F32)
    tot = jnp.sum(da, axis=1, keepdims=True)
    cb = _nt(cm, bm)
    tril = row >= col
    first = lax.broadcasted_iota(jnp.int32, (t, BLK), 1) < 64
    first_row = lax.broadcasted_iota(jnp.int32, (1, BLK), 1) < 64

    def col_form(r):
        return jnp.broadcast_to(r, (t, t)).T

    ys, new_state = [], []
    for p in range(4):
        h0, h1 = 2 * p, 2 * p + 1
        sl = slice(BLK * p, BLK * (p + 1))
        x_pair = xs[:, sl] * jnp.where(first, col_form(dt[h0:h0 + 1]), col_form(dt[h1:h1 + 1]))
        y_pair = None
        cc = []
        for h, keep in ((h0, first), (h1, jnp.logical_not(first))):
            cr = jnp.broadcast_to(cum[h:h + 1], (t, t))
            cc.append(cr.T)
            decay = jnp.exp(jnp.where(tril, cc[-1] - cr, -1e30))
            y_h = _nn(decay * cb, jnp.where(keep, x_pair, 0.0))
            y_pair = y_h if y_pair is None else y_pair + y_h
        s_pair = state[:, sl]
        y_pair = y_pair + _nn(cm, s_pair) * jnp.where(first, jnp.exp(cc[0]), jnp.exp(cc[1]))
        to_end = jnp.where(first, jnp.exp(tot[h0:h0 + 1] - cc[0]), jnp.exp(tot[h1:h1 + 1] - cc[1]))
        chunk_decay = jnp.where(first_row, jnp.exp(tot[h0:h0 + 1]), jnp.exp(tot[h1:h1 + 1]))
        new_state.append(s_pair * chunk_decay + _tn(bm, x_pair * to_end))
        ys.append(y_pair + dsk[:, sl] * xs[:, sl])
    y = jnp.concatenate(ys, axis=1) * _silu(z)
    y = y * lax.rsqrt(jnp.mean(y * y, axis=-1, keepdims=True) + RMS_EPS) * nw
    return y, jnp.concatenate(new_state, axis=1)


def _attn_block(q, kp, kc, vp, vc, sinks, n, kv):
    t = q.shape[0]
    kb = jnp.concatenate([kp, kc], axis=0)
    vb = jnp.concatenate([vp, vc], axis=0)
    qi = lax.broadcasted_iota(jnp.int32, (t, 2 * t), 0)
    kj = lax.broadcasted_iota(jnp.int32, (t, 2 * t), 1)
    rel = qi + t - kj
    valid = (rel >= 0) & (rel < t) & ((n * t - t + kj) >= 0)
    relf = rel.astype(F32)
    first = lax.broadcasted_iota(jnp.int32, (t, BLK), 1) < 64
    lane16 = lax.broadcasted_iota(jnp.int32, (1, ATT_HEADS), 1)
    outs = []
    for p in range(4):
        qp = q[:, BLK * p:BLK * (p + 1)] * (ATT_HEAD_DIM ** -0.5)
        o_pair = None
        for half, keep in enumerate((first, jnp.logical_not(first))):
            h = kv * 8 + 2 * p + half
            s = _nt(jnp.where(keep, qp, 0.0), kb)
            slope = jnp.exp((-8.0 * math.log(2.0) / ATT_HEADS) * jnp.asarray(h + 1, F32))
            s = jnp.where(valid, s - slope * relf, -1e30)
            sink = jnp.sum(jnp.where(lane16 == h, sinks, 0.0), axis=1, keepdims=True)
            m = lax.stop_gradient(jnp.maximum(jnp.max(s, axis=1, keepdims=True), sink))
            e = jnp.exp(s - m)
            den = jnp.sum(e, axis=1, keepdims=True) + jnp.exp(sink - m)
            o = _nn(e * (1.0 / den), vb)
            o_pair = o if o_pair is None else jnp.where(first, o_pair, o)
        outs.append(o_pair)
    return jnp.concatenate(outs, axis=1)


def _mm(a, b, mode, out_dtype, tm, tn, tk, name):
    if mode == "nn":
        (m, k), (k2, n) = a.shape, b.shape
    elif mode == "nt":
        (m, k), (n, k2) = a.shape, b.shape
    else:
        (k, m), (k2, n) = a.shape, b.shape
    assert k == k2, (a.shape, b.shape, mode)
    tm, tn, tk = min(tm, m), min(tn, n), min(tk, k)
    assert m % tm == 0 and n % tn == 0 and k % tk == 0, (m, n, k, tm, tn, tk)
    nk = k // tk
    ca, cb = {"nn": (1, 0), "nt": (1, 1), "tn": (0, 0)}[mode]

    def body(a_ref, b_ref, o_ref, *acc):
        part = _dot(a_ref[...], b_ref[...], ca, cb)
        if nk == 1:
            o_ref[...] = part.astype(o_ref.dtype)
            return
        acc_ref, = acc
        kk = pl.program_id(2)

        @pl.when(kk == 0)
        def _():
            acc_ref[...] = part

        @pl.when(kk > 0)
        def _():
            acc_ref[...] += part

        @pl.when(kk == nk - 1)
        def _():
            o_ref[...] = acc_ref[...].astype(o_ref.dtype)

    a_spec = pl.BlockSpec((tk, tm), lambda i, j, kk: (kk, i)) if mode == "tn" else pl.BlockSpec((tm, tk), lambda i, j, kk: (i, kk))
    b_spec = pl.BlockSpec((tn, tk), lambda i, j, kk: (j, kk)) if mode == "nt" else pl.BlockSpec((tk, tn), lambda i, j, kk: (kk, j))
    return pl.pallas_call(
        body, name=name, grid=(m // tm, n // tn, nk),
        in_specs=[a_spec, b_spec], out_specs=pl.BlockSpec((tm, tn), lambda i, j, kk: (i, j)),
        out_shape=jax.ShapeDtypeStruct((m, n), out_dtype),
        scratch_shapes=[] if nk == 1 else [pltpu.VMEM((tm, tn), F32)],
        compiler_params=_params(3),
    )(a, b)


def _rows(fn, rows, params, outs, accs, tile, name):
    length = rows[0][0].shape[0]
    tile = min(tile, length)
    assert length % tile == 0
    nr, npar, no = len(rows), len(params), len(outs)

    def body(*refs):
        vals = [r[...] for r in refs[:nr + npar]]
        o, a = fn(*vals)
        for ref, val in zip(refs[nr + npar:nr + npar + no], o):
            ref[...] = val.astype(ref.dtype)
        i = pl.program_id(0)
        for ref, val in zip(refs[nr + npar + no:], a):
            @pl.when(i == 0)
            def _(ref=ref, val=val):
                ref[...] = val

            @pl.when(i > 0)
            def _(ref=ref, val=val):
                ref[...] += val

    in_specs = [pl.BlockSpec((tile, w), functools.partial(lambda i, cb: (i, cb), cb=cb)) for _, cb, w in rows]
    in_specs += [pl.BlockSpec(p.shape, lambda i: (0, 0)) for p in params]
    outs = [o if len(o) == 3 else (o[0], o[1], o[0]) for o in outs]
    out_specs = [pl.BlockSpec((tile, w), lambda i: (i, 0)) for w, _, _ in outs]
    out_specs += [pl.BlockSpec((r, w), lambda i: (0, 0)) for r, w in accs]
    out_shape = [jax.ShapeDtypeStruct((length, full), dt) for _, dt, full in outs]
    out_shape += [jax.ShapeDtypeStruct((r, w), F32) for r, w in accs]
    res = pl.pallas_call(
        body, name=name, grid=(length // tile,), in_specs=in_specs, out_specs=out_specs, out_shape=out_shape,
        compiler_params=_params(1),
    )(*[r[0] for r in rows], *params)
    return res[:no], res[no:]


CONV_SUB = 32


def _conv_fwd(proj, conv_w, conv_b, name):
    length = proj.shape[0]
    tl, cw = min(512, length), 512
    cb0 = _OFF["xs"] // cw

    def body(u_ref, halo_ref, w_ref, b_ref, o_ref, win):
        i = pl.program_id(1)
        win[0:8, :] = jnp.where(i > 0, halo_ref[...], 0.0)
        win[8:8 + tl, :] = u_ref[...]
        taps = [w_ref[kk:kk + 1, :] for kk in range(SSD_CONV)]
        bias = b_ref[...]
        for r in range(0, tl, CONV_SUB):
            acc = bias + taps[0] * win[5 + r:5 + r + CONV_SUB, :]
            for kk in range(1, SSD_CONV):
                acc = acc + taps[kk] * win[5 + kk + r:5 + kk + r + CONV_SUB, :]
            o_ref[r:r + CONV_SUB, :] = acc

    return pl.pallas_call(
        body, name=name, grid=(CONV_DIM // cw, length // tl),
        in_specs=[pl.BlockSpec((tl, cw), lambda j, i: (i, cb0 + j)),
                  pl.BlockSpec((8, cw), lambda j, i: (jnp.maximum(i * (tl // 8) - 1, 0), cb0 + j)),
                  pl.BlockSpec((SSD_CONV, cw), lambda j, i: (0, j)),
                  pl.BlockSpec((1, cw), lambda j, i: (0, j))],
        out_specs=pl.BlockSpec((tl, cw), lambda j, i: (i, j)),
        out_shape=jax.ShapeDtypeStruct((length, CONV_DIM), F32),
        scratch_shapes=[pltpu.VMEM((8 + tl, cw), F32)],
        compiler_params=_params(2),
    )(proj, proj, conv_w, conv_b)


def _conv_bwd(dpre, proj, conv_w, dproj, part, name):
    length = proj.shape[0]
    width = dpre.shape[1]
    tl, cw = min(512, length), 512
    cb0 = _OFF["xs"] // cw + part
    n_t = length // tl

    def body(d_ref, dnext_ref, u_ref, halo_ref, w_ref, _, du_ref, dw_ref, db_ref, dwin, uwin):
        i = pl.program_id(1)
        dwin[0:tl, :] = d_ref[...]
        dwin[tl:tl + 8, :] = jnp.where(i < n_t - 1, dnext_ref[...], 0.0)
        uwin[0:8, :] = jnp.where(i > 0, halo_ref[...], 0.0)
        uwin[8:8 + tl, :] = u_ref[...]
        taps = [w_ref[kk:kk + 1, :] for kk in range(SSD_CONV)]
        sub = CONV_SUB // 2
        acc_w = [jnp.zeros((sub, cw), F32) for _ in range(SSD_CONV)]
        acc_b = jnp.zeros((sub, cw), F32)
        for r in range(0, tl, sub):
            d = dwin[r:r + sub, :]
            du = taps[3] * d
            for kk in range(SSD_CONV - 1):
                du = du + taps[kk] * dwin[3 - kk + r:3 - kk + r + sub, :]
            du_ref[r:r + sub, :] = du.astype(du_ref.dtype)
            for kk in range(SSD_CONV):
                acc_w[kk] = acc_w[kk] + d * uwin[5 + kk + r:5 + kk + r + sub, :]
            acc_b = acc_b + d
        dw = jnp.concatenate([jnp.sum(a, axis=0, keepdims=True) for a in acc_w], axis=0)
        db = jnp.sum(acc_b, axis=0, keepdims=True)

        @pl.when(i == 0)
        def _():
            dw_ref[...] = dw
            db_ref[...] = db

        @pl.when(i > 0)
        def _():
            dw_ref[...] += dw
            db_ref[...] += db

    return pl.pallas_call(
        body, name=name, grid=(width // cw, n_t), input_output_aliases={5: 0},
        in_specs=[pl.BlockSpec((tl, cw), lambda j, i: (i, j)),
                  pl.BlockSpec((8, cw), lambda j, i: (jnp.minimum((i + 1) * (tl // 8), length // 8 - 1), j)),
                  pl.BlockSpec((tl, cw), lambda j, i: (i, cb0 + j)),
                  pl.BlockSpec((8, cw), lambda j, i: (jnp.maximum(i * (tl // 8) - 1, 0), cb0 + j)),
                  pl.BlockSpec((SSD_CONV, cw), lambda j, i: (0, part + j)),
                  pl.BlockSpec(memory_space=pl.ANY)],
        out_specs=[pl.BlockSpec((tl, cw), lambda j, i: (i, cb0 + j)),
                   pl.BlockSpec((SSD_CONV, cw), lambda j, i: (0, j)),
                   pl.BlockSpec((1, cw), lambda j, i: (0, j))],
        out_shape=[jax.ShapeDtypeStruct((length, NP), BF),
                   jax.ShapeDtypeStruct((SSD_CONV, width), F32),
                   jax.ShapeDtypeStruct((1, width), F32)],
        scratch_shapes=[pltpu.VMEM((tl + 8, cw), F32), pltpu.VMEM((8 + tl, cw), F32)],
        compiler_params=_params(2),
    )(dpre, dpre, proj, proj, conv_w, dproj)


def _ssd_in_specs(rev, nc):
    def cidx(c):
        return nc - 1 - c if rev else c
    whole = lambda c: (0, 0)
    return [pl.BlockSpec((BLK, SSD_D_INNER), lambda c: (cidx(c), 0)),
            pl.BlockSpec((BLK, BC_DIM), lambda c: (cidx(c), SSD_D_INNER // BC_DIM)),
            pl.BlockSpec((BLK, BC_DIM), lambda c: (cidx(c), SSD_D_INNER // BC_DIM + 1)),
            pl.BlockSpec((BLK, SSD_D_INNER), lambda c: (cidx(c), _OFF["z"] // SSD_D_INNER)),
            pl.BlockSpec((SSD_HEADS, BLK), lambda c: (0, cidx(c))),
            pl.BlockSpec((SSD_HEADS, 1), whole),
            pl.BlockSpec((SSD_HEADS, 1), whole),
            pl.BlockSpec((1, SSD_D_INNER), whole),
            pl.BlockSpec((1, SSD_D_INNER), whole)]


def _group_args(g, px, pb, pc, z, dtr, dtb, alog, dsk, nw):
    wide, narrow, heads = slice(512 * g, 512 * (g + 1)), slice(BLK * g, BLK * (g + 1)), slice(8 * g, 8 * (g + 1))
    return (px[:, wide], pb[:, narrow], pc[:, narrow], z[:, wide], dtr[heads, :], dtb[heads, :], alog[heads, :],
            dsk[:, wide], nw[:, wide])


def _ssd_fwd(pre, proj, dt_t, dtb, alog, dsk, nw, name):
    length = pre.shape[0]
    nc = length // BLK

    def body(px, pb, pc, z, dtr, dtb_r, al_r, dsk_r, nw_r, y_ref, sin_ref, st):
        @pl.when(pl.program_id(0) == 0)
        def _():
            st[...] = jnp.zeros_like(st)

        for g in range(SSD_GROUPS):
            s_in = st[g]
            sin_ref[g] = s_in
            y, s_out = _ssd_group(*_group_args(g, px, pb, pc, z, dtr, dtb_r, al_r, dsk_r, nw_r), s_in)
            y_ref[:, 512 * g:512 * (g + 1)] = y.astype(y_ref.dtype)
            st[g] = s_out

    return pl.pallas_call(
        body, name=name, grid=(nc,), in_specs=_ssd_in_specs(False, nc),
        out_specs=[pl.BlockSpec((BLK, SSD_D_INNER), lambda c: (c, 0)),
                   pl.BlockSpec((SSD_GROUPS, None, SSD_STATE, 512), lambda c: (0, c, 0, 0))],
        out_shape=[jax.ShapeDtypeStruct((length, SSD_D_INNER), BF),
                   jax.ShapeDtypeStruct((SSD_GROUPS, nc, SSD_STATE, 512), F32)],
        scratch_shapes=[pltpu.VMEM((SSD_GROUPS, SSD_STATE, 512), F32)],
        compiler_params=_params(1),
    )(pre, pre, pre, proj, dt_t, dtb, alog, dsk, nw)


def _ssd_bwd(pre, proj, dt_t, dtb, alog, dsk, nw, s_in, dy, dproj, name):
    length = pre.shape[0]
    nc = length // BLK

    def body(px, pb, pc, z, dtr, dtb_r, al_r, dsk_r, nw_r, sin_r, dy_r, _,
             dpx, dpb, dpc, dz, ddt, ddtb, dal, ddsk, dnw, dst):
        c = pl.program_id(0)

        @pl.when(c == 0)
        def _():
            dst[...] = jnp.zeros_like(dst)

        for grp in range(SSD_GROUPS):
            wide, narrow, heads = slice(512 * grp, 512 * (grp + 1)), slice(BLK * grp, BLK * (grp + 1)), slice(8 * grp, 8 * (grp + 1))
            _, vjp = jax.vjp(_ssd_group, *_group_args(grp, px, pb, pc, z, dtr, dtb_r, al_r, dsk_r, nw_r), sin_r[grp])
            g = vjp((dy_r[:, wide], dst[grp]))
            dpx[:, wide] = g[0]
            dpb[:, narrow] = g[1]
            dpc[:, narrow] = g[2]
            dz[:, wide] = g[3].astype(dz.dtype)
            ddt[heads, :] = g[4]
            dst[grp] = g[9]
            for ref, val, idx in ((ddtb, g[5], (heads, slice(None))), (dal, g[6], (heads, slice(None))),
                                  (ddsk, g[7], (slice(None), wide)), (dnw, g[8], (slice(None), wide))):
                @pl.when(c == 0)
                def _(ref=ref, val=val, idx=idx):
                    ref[idx] = val

                @pl.when(c > 0)
                def _(ref=ref, val=val, idx=idx):
                    ref[idx] += val

    rev = lambda c: nc - 1 - c
    whole = lambda c: (0, 0)
    in_specs = _ssd_in_specs(True, nc) + [
        pl.BlockSpec((SSD_GROUPS, None, SSD_STATE, 512), lambda c: (0, rev(c), 0, 0)),
        pl.BlockSpec((BLK, SSD_D_INNER), lambda c: (rev(c), 0)),
        pl.BlockSpec(memory_space=pl.ANY)]
    return pl.pallas_call(
        body, name=name, grid=(nc,), in_specs=in_specs, input_output_aliases={11: 3},
        out_specs=[pl.BlockSpec((BLK, SSD_D_INNER), lambda c: (rev(c), 0)),
                   pl.BlockSpec((BLK, BC_DIM), lambda c: (rev(c), 0)),
                   pl.BlockSpec((BLK, BC_DIM), lambda c: (rev(c), 0)),
                   pl.BlockSpec((BLK, SSD_D_INNER), lambda c: (rev(c), _OFF["z"] // SSD_D_INNER)),
                   pl.BlockSpec((SSD_HEADS, BLK), lambda c: (0, rev(c))),
                   pl.BlockSpec((SSD_HEADS, 1), whole),
                   pl.BlockSpec((SSD_HEADS, 1), whole),
                   pl.BlockSpec((1, SSD_D_INNER), whole),
                   pl.BlockSpec((1, SSD_D_INNER), whole)],
        out_shape=[jax.ShapeDtypeStruct((length, SSD_D_INNER), F32),
                   jax.ShapeDtypeStruct((length, BC_DIM), F32),
                   jax.ShapeDtypeStruct((length, BC_DIM), F32),
                   jax.ShapeDtypeStruct((length, NP), BF),
                   jax.ShapeDtypeStruct((SSD_HEADS, length), F32),
                   jax.ShapeDtypeStruct((SSD_HEADS, 1), F32),
                   jax.ShapeDtypeStruct((SSD_HEADS, 1), F32),
                   jax.ShapeDtypeStruct((1, SSD_D_INNER), F32),
                   jax.ShapeDtypeStruct((1, SSD_D_INNER), F32)],
        scratch_shapes=[pltpu.VMEM((SSD_GROUPS, SSD_STATE, 512), F32)],
        compiler_params=_params(1),
    )(pre, pre, pre, proj, dt_t, dtb, alog, dsk, nw, s_in, dy, dproj)


assert _OFF["gates"] == 0 and _OFF["z"] % SSD_D_INNER == 0 and _OFF["q"] % D_MODEL == 0 and _OFF["k"] + 512 == NP


def _attn_in_specs():
    prev = lambda n: (0, jnp.maximum(n - 1, 0), 0)
    cur = lambda n: (0, n, 0)
    return [pl.BlockSpec((BLK, D_MODEL), lambda n: (n, _OFF["q"] // D_MODEL)),
            pl.BlockSpec((2, BLK, BLK), prev),
            pl.BlockSpec((2, BLK, BLK), cur),
            pl.BlockSpec((2, BLK, BLK), prev),
            pl.BlockSpec((2, BLK, BLK), cur),
            pl.BlockSpec((1, ATT_HEADS), lambda n: (0, 0))]


def _attn_fwd(proj, k2, v2, sinks, name):
    length = proj.shape[0]

    def body(q, kp, kc, vp, vc, sk, o_ref):
        for kv in range(2):
            cols = slice(512 * kv, 512 * (kv + 1))
            o = _attn_block(q[:, cols], kp[kv], kc[kv], vp[kv], vc[kv], sk[...], pl.program_id(0), kv)
            o_ref[:, cols] = o.astype(o_ref.dtype)

    return pl.pallas_call(
        body, name=name, grid=(length // BLK,), in_specs=_attn_in_specs(),
        out_specs=pl.BlockSpec((BLK, D_MODEL), lambda n: (n, 0)),
        out_shape=jax.ShapeDtypeStruct((length, D_MODEL), BF),
        compiler_params=_params(1),
    )(proj, k2, k2, v2, v2, sinks)


def _attn_bwd(proj, k2, v2, sinks, datt, dproj, name):
    length = proj.shape[0]

    def body(q, kp, kc, vp, vc, sk, do, _, dq, dkp, dkc, dvp, dvc, dsk):
        n = pl.program_id(0)
        dsinks = None
        for kv in range(2):
            cols = slice(512 * kv, 512 * (kv + 1))
            _, vjp = jax.vjp(lambda *a: _attn_block(*a, n, kv), q[:, cols], kp[kv], kc[kv], vp[kv], vc[kv], sk[...])
            g = vjp(do[:, cols].astype(F32))
            dq[:, cols] = g[0].astype(dq.dtype)
            dkp[kv] = g[1]
            dkc[kv] = g[2]
            dvp[kv] = g[3]
            dvc[kv] = g[4]
            dsinks = g[5] if dsinks is None else dsinks + g[5]

        @pl.when(n == 0)
        def _():
            dsk[...] = dsinks

        @pl.when(n > 0)
        def _():
            dsk[...] += dsinks

    blk3 = pl.BlockSpec((2, BLK, BLK), lambda n: (0, n, 0))
    kv_shape = jax.ShapeDtypeStruct((2, length, BLK), F32)
    return pl.pallas_call(
        body, name=name, grid=(length // BLK,),
        in_specs=_attn_in_specs() + [pl.BlockSpec((BLK, D_MODEL), lambda n: (n, 0)), pl.BlockSpec(memory_space=pl.ANY)],
        out_specs=[pl.BlockSpec((BLK, D_MODEL), lambda n: (n, _OFF["q"] // D_MODEL)), blk3, blk3, blk3, blk3,
                   pl.BlockSpec((1, ATT_HEADS), lambda n: (0, 0))],
        out_shape=[jax.ShapeDtypeStruct((length, NP), BF), kv_shape, kv_shape, kv_shape, kv_shape,
                   jax.ShapeDtypeStruct((1, ATT_HEADS), F32)],
        input_output_aliases={7: 0},
        compiler_params=_params(1),
    )(proj, k2, k2, v2, v2, sinks, datt, dproj)


DMA_CHUNK_BYTES = 1 << 20
N_STAGE = 8
LOOKAHEAD = 3


def _piece_chunks(shape, itemsize):
    if len(shape) < 2 or shape[-2] % 16 != 0:
        return [()]
    rows, cols = shape[-2:]
    step = min(rows, max(16, DMA_CHUNK_BYTES // (cols * itemsize) // 16 * 16))
    out = []
    for lead in itertools.product(*[range(d) for d in shape[:-2]]):
        for r0 in range(0, rows, step):
            out.append(lead + (pl.ds(r0, min(step, rows - r0)),))
    return out


def _exchange(srcs, group, bcast, name):
    size = {"c": 2, "xy": 4, "all": 8}[group]
    npeer = size - 1
    n = len(srcs)
    for s in srcs:
        assert bcast or s.shape[0] == size
    pieces = [s.shape if bcast else s.shape[1:] for s in srcs]
    chunks = [_piece_chunks(p, s.dtype.itemsize) for p, s in zip(pieces, srcs)]

    def body(*refs):
        src, out = refs[:n], refs[n:2 * n]
        load_sems, send_sems, recv_sems = refs[2 * n:]
        x, y, c = lax.axis_index("x"), lax.axis_index("y"), lax.axis_index("c")
        if group == "c":
            me = c
        elif group == "xy":
            me = 2 * x + y
        else:
            me = 4 * x + 2 * y + c

        def device(j):
            if group == "c":
                return (x, y, j)
            if group == "xy":
                return (j // 2, j % 2, c)
            return (j // 4, (j // 2) % 2, j % 2)

        def part(ref, idx):
            return ref.at[idx] if idx else ref

        def piece(a, j):
            return src[a] if bcast else src[a].at[j]

        for a in range(n):
            jobs = [(None, idx) for idx in chunks[a]] if bcast else [(d, idx) for idx in chunks[a] for d in range(1, size)]
            full_rows = chunks[a][0][-1].size if chunks[a][0] else None
            slot_shape = (full_rows, pieces[a][-1]) if chunks[a][0] else pieces[a]

            def stream(stage, a=a, jobs=jobs, full_rows=full_rows):
                def slot(q):
                    idx = jobs[q][1]
                    view = stage.at[q % N_STAGE]
                    if idx and idx[-1].size != full_rows:
                        view = view.at[pl.ds(0, idx[-1].size)]
                    return view

                def load(q):
                    d, idx = jobs[q]
                    j = me if d is None else jnp.bitwise_xor(me, d)
                    return pltpu.make_async_copy(part(piece(a, j), idx), slot(q), load_sems.at[q % N_STAGE])

                def sends(q):
                    d, idx = jobs[q]
                    return [pltpu.make_async_remote_copy(
                        src_ref=slot(q), dst_ref=part(out[a].at[me], idx),
                        send_sem=send_sems.at[(q % N_STAGE) * npeer + dd - 1], recv_sem=recv_sems.at[a * size + dd],
                        device_id=device(jnp.bitwise_xor(me, dd)), device_id_type=pl.DeviceIdType.MESH)
                        for dd in (range(1, size) if d is None else (d,))]

                nq = len(jobs)
                for q in range(nq + LOOKAHEAD):
                    if q < nq:
                        if q >= N_STAGE:
                            for cp in sends(q - N_STAGE):
                                cp.wait_send()
                        load(q).start()
                    if q >= LOOKAHEAD:
                        load(q - LOOKAHEAD).wait()
                        for cp in sends(q - LOOKAHEAD):
                            cp.start()
                for q in range(max(0, nq - N_STAGE), nq):
                    for cp in sends(q):
                        cp.wait_send()

            pl.run_scoped(stream, pltpu.VMEM((N_STAGE,) + tuple(slot_shape), srcs[a].dtype))

        for a in range(n):
            for d in range(1, size):
                j = jnp.bitwise_xor(me, d)
                pltpu.make_async_remote_copy(
                    src_ref=piece(a, j), dst_ref=out[a].at[j], send_sem=send_sems.at[0], recv_sem=recv_sems.at[a * size + d],
                    device_id=device(j), device_id_type=pl.DeviceIdType.MESH).wait_recv()

    any_spec = pl.BlockSpec(memory_space=pl.ANY)
    return pl.pallas_call(
        body, name=name, in_specs=[any_spec] * n, out_specs=[any_spec] * n,
        out_shape=[jax.ShapeDtypeStruct((size,) + tuple(p), s.dtype) for p, s in zip(pieces, srcs)],
        scratch_shapes=[pltpu.SemaphoreType.DMA((N_STAGE,)), pltpu.SemaphoreType.DMA((N_STAGE * npeer,)),
                        pltpu.SemaphoreType.DMA((n * size,))],
        compiler_params=pltpu.CompilerParams(vmem_limit_bytes=VMEM_LIMIT_BYTES),
    )(*srcs)


def _with_own(outs, owns, me):
    return [lax.dynamic_update_index_in_dim(o, w, me, 0) for o, w in zip(outs, owns)]


def _sum_slots(arr, name):
    k = arr.shape[0]
    rest = arr.shape[1:]
    width = rest[-1]
    rows_per = math.prod(rest[:-1])
    a2 = arr.reshape(k * rows_per, width)
    tile = rows_per
    for cand in (256, 128, 64, 32, 16, 8):
        if rows_per % cand == 0:
            tile = cand
            break
    nt = rows_per // tile

    def body(*refs):
        acc = refs[0][...]
        for r in refs[1:k]:
            acc = acc + r[...]
        refs[k][...] = acc

    in_specs = [pl.BlockSpec((tile, width), functools.partial(lambda i, s: (s * nt + i, 0), s=s)) for s in range(k)]
    out = pl.pallas_call(
        body, name=name, grid=(nt,), in_specs=in_specs, out_specs=pl.BlockSpec((tile, width), lambda i: (i, 0)),
        out_shape=jax.ShapeDtypeStruct((rows_per, width), arr.dtype), compiler_params=_params(1),
    )(*([a2] * k))
    return out.reshape(rest)


def _sum_pieces(recv, own, own_slotted, me, name):
    k = recv.shape[0]
    rest = recv.shape[1:]
    width = rest[-1]
    rows_per = math.prod(rest[:-1])
    r2 = recv.reshape(k * rows_per, width)
    o2 = own.reshape(-1, width)
    tile = rows_per
    for cand in (256, 128, 64, 32, 16, 8):
        if rows_per % cand == 0:
            tile = cand
            break
    nt = rows_per // tile

    def body(me_ref, *refs):
        mine = refs[k][...].astype(F32)
        acc = None
        for j in range(k):
            term = jnp.where(me_ref[0] == j, mine, refs[j][...].astype(F32))
            acc = term if acc is None else acc + term
        refs[k + 1][...] = acc
        refs[k + 2][...] = acc.astype(BF)

    in_specs = [pl.BlockSpec((tile, width), functools.partial(lambda i, m, s: (s * nt + i, 0), s=s)) for s in range(k)]
    in_specs.append(pl.BlockSpec((tile, width), (lambda i, m: (m[0] * nt + i, 0)) if own_slotted else (lambda i, m: (i, 0))))
    out_spec = pl.BlockSpec((tile, width), lambda i, m: (i, 0))
    out, out_bf = pl.pallas_call(
        body, name=name,
        grid_spec=pltpu.PrefetchScalarGridSpec(num_scalar_prefetch=1, grid=(nt,), in_specs=in_specs, out_specs=[out_spec, out_spec]),
        out_shape=[jax.ShapeDtypeStruct((rows_per, width), F32), jax.ShapeDtypeStruct((rows_per, width), BF)],
        compiler_params=_params(1),
    )(jnp.reshape(me, (1,)).astype(jnp.int32), *([r2] * k), o2)
    return out.reshape(rest), out_bf.reshape(rest)


def _adamw(w, g, m, v, name):
    def fn(w_, g_, m_, v_):
        m1 = ADAM_B1 * m_ + (1.0 - ADAM_B1) * g_
        v1 = ADAM_B2 * v_ + (1.0 - ADAM_B2) * (g_ * g_)
        m_hat = m1 / (1.0 - ADAM_B1 ** ADAM_STEP)
        v_hat = v1 / (1.0 - ADAM_B2 ** ADAM_STEP)
        delta = -ADAM_LR * (m_hat / (jnp.sqrt(v_hat) + ADAM_EPS) + ADAM_WD * w_)
        return (delta, m1, v1), ()

    rows, width = w.shape
    tile = rows
    for cand in (256, 128, 64, 32, 16, 8):
        if rows % cand == 0:
            tile = cand
            break
    o, _ = _rows(fn, [(w, 0, width), (g, 0, width), (m, 0, width), (v, 0, width)], [],
                 [(width, F32)] * 3, [], tile, name)
    return o


def _pack_cols(w):
    parts = [w[..., o:o + wd] for _, o, wd in _PACK]
    parts.append(jnp.zeros(w.shape[:-1] + (_PAD,), w.dtype))
    return jnp.concatenate(parts, axis=-1)


def _unpack_cols(wp):
    order = sorted(_PACK, key=lambda e: e[1])
    return jnp.concatenate([wp[..., _OFF[nm]:_OFF[nm] + wd] for nm, _, wd in order], axis=-1)


def _col_sharded_full(g):
    return jnp.transpose(g, (1, 0, 2)).reshape(g.shape[1], -1)


def _col_sharded_split(full):
    r, cdim = full.shape
    return jnp.transpose(full.reshape(r, 4, cdim // 4), (1, 0, 2))


def _double_heads(t):
    length = t.shape[0]
    h = jnp.transpose(t.reshape(length, 2, 64), (1, 0, 2))
    return jnp.concatenate([h, h], axis=-1)


def _fold_heads(d_cur, d_prev):
    length = d_cur.shape[1]
    d = d_cur + jnp.concatenate([d_prev[:, BLK:], jnp.zeros((2, BLK, BLK), F32)], axis=1)
    d = d[..., :64] + d[..., 64:]
    return jnp.transpose(d, (1, 0, 2)).reshape(length, 128)


def _ln_fwd(a, r, g, b, name):
    def fn(a_, r_, g_, b_):
        y = _layer_norm(ALPHA * a_ + r_, g_, b_)
        return (y, y), ()
    o, _ = _rows(fn, [(a, 0, D_MODEL), (r, 0, D_MODEL)], [g, b], [(D_MODEL, F32), (D_MODEL, BF)], [], 256, name)
    return o


def _ln_bwd(a, r, d1, d2, g, b, name):
    def fn(a_, r_, d1_, d2_, g_, b_):
        _, vjp = jax.vjp(_layer_norm, ALPHA * a_ + r_, g_, b_)
        ds, dg, db = vjp(ALPHA * d1_ + d2_)
        return (ds, ds), (dg, db)
    o, acc = _rows(fn, [(a, 0, D_MODEL), (r, 0, D_MODEL), (d1, 0, D_MODEL), (d2, 0, D_MODEL)], [g, b],
                   [(D_MODEL, F32), (D_MODEL, BF)], [(1, D_MODEL), (1, D_MODEL)], 256, name)
    return o[0], o[1], acc[0], acc[1]


def _layer_fwd(l, h, hb, w, p):
    sv = {"h": h, "hb": hb}
    proj = _mm(hb, w["win"], "nn", F32, 2048, 512, 1024, f"proj{l}")
    sv["proj"] = proj
    pre = _conv_fwd(proj, p["conv_w"], p["conv_b"], f"conv{l}")
    sv["pre"] = pre
    dt_t = jnp.transpose(proj[:, _OFF["dt"]:_OFF["dt"] + SSD_HEADS])
    sv["dt_t"] = dt_t
    yn, s_in = _ssd_fwd(pre, proj, dt_t, p["dt_bias"], p["a_log"], p["d_skip_c"], p["ssd_norm_w"], f"ssd{l}")
    sv["yn"], sv["s_in"] = yn, s_in
    ya = _mm(yn, w["wso"], "nn", F32, 1024, 1024, 2048, f"ssdout{l}")
    k2 = _double_heads(proj[:, _OFF["k"]:_OFF["k"] + 128])
    v2 = _double_heads(proj[:, _OFF["v"]:_OFF["v"] + 128])
    sv["k2"], sv["v2"] = k2, v2
    att = _attn_fwd(proj, k2, v2, p["att_sinks"], f"attn{l}")
    sv["att"] = att
    yb = _mm(att, w["wao"], "nn", F32, 1024, 1024, 1024, f"attout{l}")
    sv["ya"], sv["yb"] = ya, yb
    gcb = _OFF["gates"] // 1024

    def gate_fn(ga, gb, ya_, yb_):
        return (_sigmoid(ga) * ya_ + _sigmoid(gb) * yb_,), ()
    (ub,), _ = _rows(gate_fn, [(proj, gcb, 1024), (proj, gcb + 1, 1024), (ya, 0, 1024), (yb, 0, 1024)], [],
                     [(D_MODEL, BF)], [], 256, f"gate{l}")
    sv["ub"] = ub
    mix = _mm(ub, w["wmo"], "nn", F32, 1024, 1024, 1024, f"mixout{l}")
    sv["mix"] = mix
    h1, h1b = _ln_fwd(h, mix, p["ln_mix_g"], p["ln_mix_b"], f"lnmix{l}")
    sv["h1"], sv["h1b"] = h1, h1b
    gu = _mm(h1b, w["wgu"], "nn", F32, 2048, 512, 1024, f"ffnin{l}")
    sv["gu"] = gu

    def act_fn(g_, u_):
        return (_silu(g_) * u_,), ()
    (act,), _ = _rows(act_fn, [(gu, 0, FFN_HIDDEN), (gu, 1, FFN_HIDDEN)], [], [(FFN_HIDDEN, BF)], [], 256, f"swiglu{l}")
    sv["act"] = act
    ffn = _mm(act, w["wd"], "nn", F32, 1024, 1024, FFN_HIDDEN, f"ffnout{l}")
    sv["ffn"] = ffn
    return sv


def _layer_bwd(l, sv, w, p, ds_f, ds_b):
    g = {}
    g["w_ffn_down"] = _mm(sv["act"], ds_b, "tn", F32, 1408, 1024, 1024, f"dwd{l}")
    dact = _mm(ds_b, w["wd"], "nt", BF, 1024, FFN_HIDDEN, 1024, f"dact{l}")

    def act_bwd(g_, u_, d_):
        _, vjp = jax.vjp(lambda a, b: _silu(a) * b, g_, u_)
        dg_, du_ = vjp(d_.astype(F32))
        return (jnp.concatenate([dg_, du_], axis=1),), ()
    (dgu,), _ = _rows(act_bwd, [(sv["gu"], 0, FFN_HIDDEN), (sv["gu"], 1, FFN_HIDDEN), (dact, 0, FFN_HIDDEN)], [],
                      [(2 * FFN_HIDDEN, BF)], [], 128, f"dswiglu{l}")
    dwgu = _mm(sv["h1b"], dgu, "tn", F32, 1024, 2816, 512, f"dwgu{l}")
    g["w_ffn_gate"], g["w_ffn_up"] = dwgu[:, :FFN_HIDDEN], dwgu[:, FFN_HIDDEN:]
    dh1 = _mm(dgu, w["wgu"], "nt", F32, 1024, 1024, 2816, f"dh1{l}")
    ds2_f, ds2_b, g["ln_mix_g"], g["ln_mix_b"] = _ln_bwd(sv["h"], sv["mix"], ds_f, dh1, p["ln_mix_g"], p["ln_mix_b"], f"dlnmix{l}")
    g["w_mix_out"] = _mm(sv["ub"], ds2_b, "tn", F32, 1024, 1024, 512, f"dwmo{l}")
    du = _mm(ds2_b, w["wmo"], "nt", BF, 1024, 1024, 1024, f"du{l}")
    proj = sv["proj"]
    gcb = _OFF["gates"] // 1024

    def gate_bwd(ga, gb, ya_, yb_, du_):
        _, vjp = jax.vjp(lambda a, b, c, d: _sigmoid(a) * c + _sigmoid(b) * d, ga, gb, ya_, yb_)
        dga, dgb, dya, dyb = vjp(du_.astype(F32))
        return (jnp.concatenate([dga, dgb], axis=1), dya, dyb), ()
    (dproj, dya, dyb), _ = _rows(
        gate_bwd, [(proj, gcb, 1024), (proj, gcb + 1, 1024), (sv["ya"], 0, 1024), (sv["yb"], 0, 1024), (du, 0, 1024)], [],
        [(2048, BF, NP), (D_MODEL, BF), (D_MODEL, BF)], [], 256, f"dgate{l}")
    g["w_att_out"] = _mm(sv["att"], dyb, "tn", F32, 1024, 1024, 512, f"dwao{l}")
    datt = _mm(dyb, w["wao"], "nt", BF, 1024, 1024, 1024, f"datt{l}")
    g["w_ssd_out"] = _mm(sv["yn"], dya, "tn", F32, 1024, 1024, 512, f"dwso{l}")
    dyn = _mm(dya, w["wso"], "nt", F32, 1024, 1024, 1024, f"dyn{l}")
    dproj, dkp, dkc, dvp, dvc, g["att_sinks"] = _attn_bwd(proj, sv["k2"], sv["v2"], p["att_sinks"], datt, dproj, f"dattn{l}")
    (dpx, dpb, dpc, dproj, ddt_t, g["dt_bias"], g["a_log"], ddsk, g["ssd_norm_w"]) = _ssd_bwd(
        sv["pre"], proj, sv["dt_t"], p["dt_bias"], p["a_log"], p["d_skip_c"], p["ssd_norm_w"], sv["s_in"], dyn, dproj, f"dssd{l}")
    g["d_skip"] = jnp.sum(ddsk.reshape(SSD_HEADS, 64), axis=1)
    dws, dbs = [], []
    for nm, dpart, part in (("x", dpx, 0), ("b", dpb, 4), ("c", dpc, 5)):
        dproj, dw_, db_ = _conv_bwd(dpart, proj, p["conv_w"], dproj, part, f"dconv{nm}{l}")
        dws.append(dw_)
        dbs.append(db_)
    g["conv_w"], g["conv_b"] = jnp.concatenate(dws, axis=1), jnp.concatenate(dbs, axis=1)
    length = proj.shape[0]
    tail = jnp.concatenate([_fold_heads(dkc, dkp).astype(BF), _fold_heads(dvc, dvp).astype(BF),
                            jnp.transpose(ddt_t).astype(BF), jnp.zeros((length, _PAD), BF)], axis=1)
    dproj = lax.dynamic_update_slice(dproj, tail, (0, _OFF["k"]))
    g["w_in"] = _unpack_cols(_mm(sv["hb"], dproj, "tn", F32, 1024, 2176, 1024, f"dwin{l}"))
    dh = _mm(dproj, w["win"], "nt", F32, 1024, 1024, 2176, f"dh{l}")
    return g, ds2_f, dh


def kernel(x, ln_in_g, ln_in_b, w_in, conv_w, conv_b, dt_bias, a_log, d_skip, ssd_norm_w, att_sinks, w_ssd_out, w_att_out, w_mix_out, ln_mix_g, ln_mix_b, w_ffn_gate, w_ffn_up, w_ffn_down, ln_ffn_g, ln_ffn_b, loss_target, m_ln_in_g, m_ln_in_b, m_w_in, m_conv_w, m_conv_b, m_dt_bias, m_a_log, m_d_skip, m_ssd_norm_w, m_att_sinks, m_w_ssd_out, m_w_att_out, m_w_mix_out, m_ln_mix_g, m_ln_mix_b, m_w_ffn_gate, m_w_ffn_up, m_w_ffn_down, m_ln_ffn_g, m_ln_ffn_b, v_ln_in_g, v_ln_in_b, v_w_in, v_conv_w, v_conv_b, v_dt_bias, v_a_log, v_d_skip, v_ssd_norm_w, v_att_sinks, v_w_ssd_out, v_w_att_out, v_w_mix_out, v_ln_mix_g, v_ln_mix_b, v_w_ffn_gate, v_w_ffn_up, v_w_ffn_down, v_ln_ffn_g, v_ln_ffn_b):
    env = dict(locals())
    wts = {n: env[n] for n in WEIGHTS}
    mom1 = {n: env["m_" + n] for n in WEIGHTS}
    mom2 = {n: env["v_" + n] for n in WEIGHTS}
    ci = lax.axis_index("c")
    chip = 2 * lax.axis_index("x") + lax.axis_index("y")
    xs_ = x[0]
    tgt = loss_target[0]

    shards = [wts[n].astype(BF) for n in BIG] + [conv_w]
    mine = [lax.dynamic_index_in_dim(s, ci, 0, keepdims=False) for s in shards]
    by_chip = _with_own(_exchange(mine, "xy", True, "gather_chips"), mine, chip)
    by_layer = _with_own(_exchange(by_chip, "c", True, "gather_cores"), by_chip, ci)
    full = dict(zip(BIG + ("conv_w",), by_layer))

    layer_w, layer_p = [], []
    for l in range(DEPTH):
        layer_w.append({
            "win": _pack_cols(_col_sharded_full(full["w_in"][l])),
            "wso": full["w_ssd_out"][l].reshape(SSD_D_INNER, D_MODEL),
            "wao": full["w_att_out"][l].reshape(D_MODEL, D_MODEL),
            "wmo": full["w_mix_out"][l].reshape(D_MODEL, D_MODEL),
            "wgu": jnp.concatenate([_col_sharded_full(full["w_ffn_gate"][l]), _col_sharded_full(full["w_ffn_up"][l])], axis=1),
            "wd": full["w_ffn_down"][l].reshape(FFN_HIDDEN, D_MODEL),
        })
        layer_p.append({
            "conv_w": _col_sharded_full(full["conv_w"][l]), "conv_b": conv_b[l][None],
            "dt_bias": dt_bias[l][:, None], "a_log": a_log[l][:, None],
            "d_skip_c": jnp.repeat(d_skip[l], 64)[None], "ssd_norm_w": ssd_norm_w[l][None],
            "att_sinks": att_sinks[l][None], "ln_mix_g": ln_mix_g[l][None], "ln_mix_b": ln_mix_b[l][None],
            "ln_ffn_g": ln_ffn_g[l][None], "ln_ffn_b": ln_ffn_b[l][None],
        })

    def ln_in_fn(x_, g_, b_):
        y = _layer_norm(x_, g_, b_)
        return (y, y), ()
    (h, hb), _ = _rows(ln_in_fn, [(xs_, 0, D_MODEL)], [ln_in_g[None], ln_in_b[None]], [(D_MODEL, F32), (D_MODEL, BF)], [], 256, "ln_in")
    saved = []
    for l in range(DEPTH):
        sv = _layer_fwd(l, h, hb, layer_w[l], layer_p[l])
        saved.append(sv)
        if l < DEPTH - 1:
            h, hb = _ln_fwd(sv["h1"], sv["ffn"], layer_p[l]["ln_ffn_g"], layer_p[l]["ln_ffn_b"], f"lnffn{l}")

    def loss_fn(a_, r_, t_, g_, b_):
        y, vjp = jax.vjp(_layer_norm, ALPHA * a_ + r_, g_, b_)
        err = y - t_
        ds, dg, db = vjp(err * (1.0 / D_MODEL))
        part = 0.5 * jnp.sum(jnp.mean(err * err, axis=-1, keepdims=True), axis=0, keepdims=True)
        return (ds, ds), (dg, db, jnp.broadcast_to(part, (1, BLK)))
    sv = saved[-1]
    (ds_f, ds_b), (dg_last, db_last, loss_part) = _rows(
        loss_fn, [(sv["h1"], 0, D_MODEL), (sv["ffn"], 0, D_MODEL), (tgt, 0, D_MODEL)],
        [layer_p[-1]["ln_ffn_g"], layer_p[-1]["ln_ffn_b"]], [(D_MODEL, F32), (D_MODEL, BF)],
        [(1, D_MODEL), (1, D_MODEL), (1, BLK)], 256, "loss")
    loss = lax.psum(loss_part[0, 0], ("x", "y", "c"))

    grads = [None] * DEPTH
    for l in reversed(range(DEPTH)):
        sv = saved[l]
        if l == DEPTH - 1:
            lg, lb = dg_last, db_last
        else:
            ds_f, ds_b, lg, lb = _ln_bwd(sv["h1"], sv["ffn"], d1, d2, layer_p[l]["ln_ffn_g"], layer_p[l]["ln_ffn_b"], f"dlnffn{l}")
        g, d1, d2 = _layer_bwd(l, sv, layer_w[l], layer_p[l], ds_f, ds_b)
        g["ln_ffn_g"], g["ln_ffn_b"] = lg, lb
        grads[l] = g

    def ln_in_bwd(x_, d1_, d2_, g_, b_):
        _, vjp = jax.vjp(_layer_norm, x_, g_, b_)
        dx, dg, db = vjp(ALPHA * d1_ + d2_)
        return (dx,), (dg, db)
    (grad_x,), (g_ln_in_g, g_ln_in_b) = _rows(
        ln_in_bwd, [(xs_, 0, D_MODEL), (d1, 0, D_MODEL), (d2, 0, D_MODEL)], [ln_in_g[None], ln_in_b[None]],
        [(D_MODEL, F32)], [(1, D_MODEL), (1, D_MODEL)], 256, "dln_in")

    def by_shard(name, gfull):
        if name in ("w_in", "w_ffn_gate", "w_ffn_up"):
            return _col_sharded_split(gfull)
        return gfull.reshape((4, gfull.shape[0] // 4) + gfull.shape[1:])
    big_g = [jnp.stack([by_shard(n, grads[l][n]) for l in range(DEPTH)]) for n in BIG]
    from_cores = _exchange(big_g, "c", False, "reduce_cores")
    chip_sum = [_sum_pieces(r, o, True, ci, f"sum_cores{i}") for i, (r, o) in enumerate(zip(from_cores, big_g))]
    from_chips = _exchange([bf for _, bf in chip_sum], "xy", False, "reduce_chips")
    reduced = [_sum_pieces(r, o, True, chip, f"sum_chips{i}")[0] for i, (r, (o, _)) in enumerate(zip(from_chips, chip_sum))]
    both = _with_own(_exchange(reduced, "c", True, "swap_layers"), reduced, ci)
    big_grad = dict(zip(BIG, both))

    small_g = {"ln_in_g": g_ln_in_g[0], "ln_in_b": g_ln_in_b[0]}
    for n in SMALL[2:]:
        small_g[n] = jnp.stack([grads[l][n].reshape(wts[n].shape[1:] if n != "conv_w" else (SSD_CONV, CONV_DIM)) for l in range(DEPTH)])
    flat = jnp.concatenate([small_g[n].reshape(-1) for n in SMALL])
    n_small = flat.shape[0]
    width = -(-n_small // 1024) * 1024
    flat = jnp.pad(flat, (0, width - n_small)).reshape(8, width // 8)
    gathered, = _with_own(_exchange([flat], "all", True, "gather_small"), [flat], 2 * chip + ci)
    total = _sum_slots(gathered, "sum_small").reshape(-1)
    small_grad, off = {}, 0
    for n in SMALL:
        shp = small_g[n].shape
        cnt = math.prod(shp)
        small_grad[n] = total[off:off + cnt].reshape(shp)
        off += cnt
    small_grad["conv_w"] = lax.dynamic_slice_in_dim(small_grad["conv_w"], chip * (CONV_DIM // 4), CONV_DIM // 4, axis=2)

    out_g, out_d, out_m, out_v = {}, {}, {}, {}
    for n in BIG:
        shp = wts[n].shape
        two_d = lambda t: t.reshape(shp[0] * shp[1], shp[2])
        d_, m_, v_ = _adamw(two_d(wts[n]), two_d(big_grad[n]), two_d(mom1[n]), two_d(mom2[n]), f"adamw_{n}")
        out_g[n], out_d[n], out_m[n], out_v[n] = big_grad[n], d_.reshape(shp), m_.reshape(shp), v_.reshape(shp)

    def flat_small(d):
        f = jnp.concatenate([d[n].reshape(-1) for n in SMALL])
        return jnp.pad(f, (0, swidth - f.shape[0])).reshape(8, swidth // 8)
    n_sw = sum(math.prod(wts[n].shape) for n in SMALL)
    swidth = -(-n_sw // 1024) * 1024
    d_, m_, v_ = _adamw(flat_small(wts), flat_small(small_grad), flat_small(mom1), flat_small(mom2), "adamw_small")
    off = 0
    for n in SMALL:
        shp = wts[n].shape
        cnt = math.prod(shp)
        out_g[n] = small_grad[n]
        out_d[n] = d_.reshape(-1)[off:off + cnt].reshape(shp)
        out_m[n] = m_.reshape(-1)[off:off + cnt].reshape(shp)
        out_v[n] = v_.reshape(-1)[off:off + cnt].reshape(shp)
        off += cnt

    return (loss, grad_x[None], *[out_g[n] for n in WEIGHTS], *[out_d[n] for n in WEIGHTS],
            *[out_m[n] for n in WEIGHTS], *[out_v[n] for n in WEIGHTS])
```

```python
import functools
import itertools
import math

import jax
import jax.numpy as jnp
from jax import lax
from jax.experimental import pallas as pl
from jax.experimental.pallas import tpu as pltpu

F32 = jnp.float32
BF = jnp.bfloat16

D_MODEL = 1024
DEPTH = 2
ATT_HEADS = 16
ATT_HEAD_DIM = 64
BLK = 128
SSD_D_INNER = 2048
SSD_HEADS = 32
SSD_GROUPS = 4
SSD_STATE = 128
SSD_CONV = 4
BC_DIM = 512
CONV_DIM = 3072
FFN_HIDDEN = 2816
IN_DIM = 8480
LN_EPS = 1e-5
RMS_EPS = 1e-5
ALPHA = (2 * DEPTH) ** 0.25
ADAM_LR = 0.001
ADAM_B1 = 0.9
ADAM_B2 = 0.999
ADAM_EPS = 1e-08
ADAM_WD = 0.01
ADAM_STEP = 10

_PACK = (("gates", 6432, 2048), ("z", 1280, 2048), ("q", 0, 1024), ("xs", 3328, 2048), ("B", 5376, 512),
         ("C", 5888, 512), ("k", 1024, 128), ("v", 1152, 128), ("dt", 6400, 32))
NP = 8704
_OFF = {}
_o = 0
for _n, _, _w in _PACK:
    _OFF[_n] = _o
    _o += _w
_PAD = NP - _o

VMEM_LIMIT_BYTES = 56 * 1024 * 1024
BIG = ("w_in", "w_ssd_out", "w_att_out", "w_mix_out", "w_ffn_gate", "w_ffn_up", "w_ffn_down")
WEIGHTS = ("ln_in_g", "ln_in_b", "w_in", "conv_w", "conv_b", "dt_bias", "a_log", "d_skip", "ssd_norm_w", "att_sinks",
           "w_ssd_out", "w_att_out", "w_mix_out", "ln_mix_g", "ln_mix_b", "w_ffn_gate", "w_ffn_up", "w_ffn_down",
           "ln_ffn_g", "ln_ffn_b")
SMALL = tuple(n for n in WEIGHTS if n not in BIG)


def _params(n_grid):
    return pltpu.CompilerParams(dimension_semantics=("arbitrary",) * n_grid, vmem_limit_bytes=VMEM_LIMIT_BYTES)


def _dot(a, b, ca, cb):
    return lax.dot_general(a.astype(BF), b.astype(BF), (((ca,), (cb,)), ((), ())), preferred_element_type=F32)


@jax.custom_vjp
def _nn(a, b):
    return _dot(a, b, 1, 0)


def _nn_f(a, b):
    return _dot(a, b, 1, 0), (a, b)


def _nn_b(res, g):
    a, b = res
    return _dot(g, b, 1, 1).astype(a.dtype), _dot(a, g, 0, 0).astype(b.dtype)


_nn.defvjp(_nn_f, _nn_b)


@jax.custom_vjp
def _nt(a, b):
    return _dot(a, b, 1, 1)


def _nt_f(a, b):
    return _dot(a, b, 1, 1), (a, b)


def _nt_b(res, g):
    a, b = res
    return _dot(g, b, 1, 0).astype(a.dtype), _dot(g, a, 0, 0).astype(b.dtype)


_nt.defvjp(_nt_f, _nt_b)


@jax.custom_vjp
def _tn(a, b):
    return _dot(a, b, 0, 0)


def _tn_f(a, b):
    return _dot(a, b, 0, 0), (a, b)


def _tn_b(res, g):
    a, b = res
    return _dot(b, g, 1, 1).astype(a.dtype), _dot(a, g, 1, 0).astype(b.dtype)


_tn.defvjp(_tn_f, _tn_b)


def _sigmoid(x):
    return 0.5 * jnp.tanh(0.5 * x) + 0.5


def _silu(x):
    return x * _sigmoid(x)


def _layer_norm(s, g, b):
    mu = jnp.mean(s, axis=-1, keepdims=True)
    sc = s - mu
    var = jnp.mean(sc * sc, axis=-1, keepdims=True)
    return sc * lax.rsqrt(var + LN_EPS) * g + b


def _ssd_group(pre_x, pre_b, pre_c, z, dtr, dtb, alog, dsk, nw, state):
    t = pre_x.shape[0]
    xs, bm, cm = _silu(pre_x), _silu(pre_b), _silu(pre_c)
    dt = jax.nn.softplus(dtr + dtb)
    da = dt * (-jnp.exp(alog))
    row = lax.broadcasted_iota(jnp.int32, (t, t), 0)
    col = lax.broadcasted_iota(jnp.int32, (t, t), 1)
    upper = (row <= col).astype(F32)
    cum = jnp.dot(da, upper, precision=lax.Precision.HIGHEST, preferred_element_type=F32)
    tot = jnp.sum(da, axis=1, keepdims=True)
    cb = _nt(cm, bm)
    tril = row >= col
    first = lax.broadcasted_iota(jnp.int32, (t, BLK), 1) < 64
    first_row = lax.broadcasted_iota(jnp.int32, (1, BLK), 1) < 64

    def col_form(r):
        return jnp.broadcast_to(r, (t, t)).T

    ys, new_state = [], []
    for p in range(4):
        h0, h1 = 2 * p, 2 * p + 1
        sl = slice(BLK * p, BLK * (p + 1))
        x_pair = xs[:, sl] * jnp.where(first, col_form(dt[h0:h0 + 1]), col_form(dt[h1:h1 + 1]))
        y_pair = None
        cc = []
        for h, keep in ((h0, first), (h1, jnp.logical_not(first))):
            cr = jnp.broadcast_to(cum[h:h + 1], (t, t))
            cc.append(cr.T)
            decay = jnp.exp(jnp.where(tril, cc[-1] - cr, -1e30))
            y_h = _nn(decay * cb, jnp.where(keep, x_pair, 0.0))
            y_pair = y_h if y_pair is None else y_pair + y_h
        s_pair = state[:, sl]
        y_pair = y_pair + _nn(cm, s_pair) * jnp.where(first, jnp.exp(cc[0]), jnp.exp(cc[1]))
        to_end = jnp.where(first, jnp.exp(tot[h0:h0 + 1] - cc[0]), jnp.exp(tot[h1:h1 + 1] - cc[1]))
        chunk_decay = jnp.where(first_row, jnp.exp(tot[h0:h0 + 1]), jnp.exp(tot[h1:h1 + 1]))
        new_state.append(s_pair * chunk_decay + _tn(bm, x_pair * to_end))
        ys.append(y_pair + dsk[:, sl] * xs[:, sl])
    y = jnp.concatenate(ys, axis=1) * _silu(z)
    y = y * lax.rsqrt(jnp.mean(y * y, axis=-1, keepdims=True) + RMS_EPS) * nw
    return y, jnp.concatenate(new_state, axis=1)


def _attn_block(q, kp, kc, vp, vc, sinks, n, kv):
    t = q.shape[0]
    kb = jnp.concatenate([kp, kc], axis=0)
    vb = jnp.concatenate([vp, vc], axis=0)
    qi = lax.broadcasted_iota(jnp.int32, (t, 2 * t), 0)
    kj = lax.broadcasted_iota(jnp.int32, (t, 2 * t), 1)
    rel = qi + t - kj
    valid = (rel >= 0) & (rel < t) & ((n * t - t + kj) >= 0)
    relf = rel.astype(F32)
    first = lax.broadcasted_iota(jnp.int32, (t, BLK), 1) < 64
    lane16 = lax.broadcasted_iota(jnp.int32, (1, ATT_HEADS), 1)
    outs = []
    for p in range(4):
        qp = q[:, BLK * p:BLK * (p + 1)] * (ATT_HEAD_DIM ** -0.5)
        o_pair = None
        for half, keep in enumerate((first, jnp.logical_not(first))):
            h = kv * 8 + 2 * p + half
            s = _nt(jnp.where(keep, qp, 0.0), kb)
            slope = jnp.exp((-8.0 * math.log(2.0) / ATT_HEADS) * jnp.asarray(h + 1, F32))
            s = jnp.where(valid, s - slope * relf, -1e30)
            sink = jnp.sum(jnp.where(lane16 == h, sinks, 0.0), axis=1, keepdims=True)
            m = lax.stop_gradient(jnp.maximum(jnp.max(s, axis=1, keepdims=True), sink))
            e = jnp.exp(s - m)
            den = jnp.sum(e, axis=1, keepdims=True) + jnp.exp(sink - m)
            o = _nn(e * (1.0 / den), vb)
            o_pair = o if o_pair is None else jnp.where(first, o_pair, o)
        outs.append(o_pair)
    return jnp.concatenate(outs, axis=1)


def _chip_exchange(src, out, send_sems, recv_sems, bcast):
    x, y, c = lax.axis_index("x"), lax.axis_index("y"), lax.axis_index("c")
    me = 2 * x + y

    def copy(a, d, slot):
        j = jnp.bitwise_xor(me, d)
        return pltpu.make_async_remote_copy(
            src_ref=src[a] if bcast else src[a].at[j], dst_ref=out[a].at[slot],
            send_sem=send_sems.at[a * 4 + d], recv_sem=recv_sems.at[a * 4 + d],
            device_id=(j // 2, j % 2, c), device_id_type=pl.DeviceIdType.MESH)

    pairs = [(a, d) for a in range(len(src)) for d in range(1, 4)]

    def start():
        for a, d in pairs:
            copy(a, d, me).start()

    def wait():
        for a, d in pairs:
            copy(a, d, jnp.bitwise_xor(me, d)).wait_recv()
        for a, d in pairs:
            copy(a, d, me).wait_send()

    return start, wait


def _mm(a, b, mode, out_dtype, tm, tn, tk, name, hosted=None):
    if mode == "nn":
        (m, k), (k2, n) = a.shape, b.shape
    elif mode == "nt":
        (m, k), (n, k2) = a.shape, b.shape
    else:
        (k, m), (k2, n) = a.shape, b.shape
    assert k == k2, (a.shape, b.shape, mode)
    tm, tn, tk = min(tm, m), min(tn, n), min(tk, k)
    assert m % tm == 0 and n % tn == 0 and k % tk == 0, (m, n, k, tm, tn, tk)
    nk = k // tk
    grid = (m // tm, n // tn, nk)
    ca, cb = {"nn": (1, 0), "nt": (1, 1), "tn": (0, 0)}[mode]
    h_arrays, h_bcast = hosted if hosted else ((), False)
    nh = len(h_arrays)

    def body(*refs):
        a_ref, b_ref = refs[:2]
        o_ref = refs[2 + nh]
        scratch = refs[3 + 2 * nh:]
        step = [pl.program_id(ax) for ax in range(3)]
        if nh:
            start, wait = _chip_exchange(refs[2:2 + nh], refs[3 + nh:3 + 2 * nh], scratch[-2], scratch[-1], h_bcast)

            @pl.when((step[0] == 0) & (step[1] == 0) & (step[2] == 0))
            def _():
                start()

        part = _dot(a_ref[...], b_ref[...], ca, cb)
        if nk == 1:
            o_ref[...] = part.astype(o_ref.dtype)
        else:
            acc_ref = scratch[0]

            @pl.when(step[2] == 0)
            def _():
                acc_ref[...] = part

            @pl.when(step[2] > 0)
            def _():
                acc_ref[...] += part

            @pl.when(step[2] == nk - 1)
            def _():
                o_ref[...] = acc_ref[...].astype(o_ref.dtype)

        if nh:
            @pl.when((step[0] == grid[0] - 1) & (step[1] == grid[1] - 1) & (step[2] == nk - 1))
            def _():
                wait()

    a_spec = pl.BlockSpec((tk, tm), lambda i, j, kk: (kk, i)) if mode == "tn" else pl.BlockSpec((tm, tk), lambda i, j, kk: (i, kk))
    b_spec = pl.BlockSpec((tn, tk), lambda i, j, kk: (j, kk)) if mode == "nt" else pl.BlockSpec((tk, tn), lambda i, j, kk: (kk, j))
    any_spec = pl.BlockSpec(memory_space=pl.ANY)
    res = pl.pallas_call(
        body, name=name, grid=grid,
        in_specs=[a_spec, b_spec] + [any_spec] * nh,
        out_specs=[pl.BlockSpec((tm, tn), lambda i, j, kk: (i, j))] + [any_spec] * nh,
        out_shape=[jax.ShapeDtypeStruct((m, n), out_dtype)]
        + [jax.ShapeDtypeStruct(((4,) + s.shape) if h_bcast else s.shape, s.dtype) for s in h_arrays],
        scratch_shapes=([] if nk == 1 else [pltpu.VMEM((tm, tn), F32)])
        + ([pltpu.SemaphoreType.DMA((4 * nh,)), pltpu.SemaphoreType.DMA((4 * nh,))] if nh else []),
        compiler_params=_params(3),
    )(a, b, *h_arrays)
    return (res[0], list(res[1:])) if nh else res[0]


def _rows(fn, rows, params, outs, accs, tile, name):
    length = rows[0][0].shape[0]
    tile = min(tile, length)
    assert length % tile == 0
    nr, npar, no = len(rows), len(params), len(outs)

    def body(*refs):
        vals = [r[...] for r in refs[:nr + npar]]
        o, a = fn(*vals)
        for ref, val in zip(refs[nr + npar:nr + npar + no], o):
            ref[...] = val.astype(ref.dtype)
        i = pl.program_id(0)
        for ref, val in zip(refs[nr + npar + no:], a):
            @pl.when(i == 0)
            def _(ref=ref, val=val):
                ref[...] = val

            @pl.when(i > 0)
            def _(ref=ref, val=val):
                ref[...] += val

    in_specs = [pl.BlockSpec((tile, w), functools.partial(lambda i, cb: (i, cb), cb=cb)) for _, cb, w in rows]
    in_specs += [pl.BlockSpec(p.shape, lambda i: (0, 0)) for p in params]
    outs = [o if len(o) == 3 else (o[0], o[1], o[0]) for o in outs]
    out_specs = [pl.BlockSpec((tile, w), lambda i: (i, 0)) for w, _, _ in outs]
    out_specs += [pl.BlockSpec((r, w), lambda i: (0, 0)) for r, w in accs]
    out_shape = [jax.ShapeDtypeStruct((length, full), dt) for _, dt, full in outs]
    out_shape += [jax.ShapeDtypeStruct((r, w), F32) for r, w in accs]
    res = pl.pallas_call(
        body, name=name, grid=(length // tile,), in_specs=in_specs, out_specs=out_specs, out_shape=out_shape,
        compiler_params=_params(1),
    )(*[r[0] for r in rows], *params)
    return res[:no], res[no:]


CONV_SUB = 32


def _conv_fwd(proj, conv_w, conv_b, name):
    length = proj.shape[0]
    tl, cw = min(512, length), 512
    cb0 = _OFF["xs"] // cw

    def body(u_ref, halo_ref, w_ref, b_ref, o_ref, win):
        i = pl.program_id(1)
        win[0:8, :] = jnp.where(i > 0, halo_ref[...], 0.0)
        win[8:8 + tl, :] = u_ref[...]
        taps = [w_ref[kk:kk + 1, :] for kk in range(SSD_CONV)]
        bias = b_ref[...]
        for r in range(0, tl, CONV_SUB):
            acc = bias + taps[0] * win[5 + r:5 + r + CONV_SUB, :]
            for kk in range(1, SSD_CONV):
                acc = acc + taps[kk] * win[5 + kk + r:5 + kk + r + CONV_SUB, :]
            o_ref[r:r + CONV_SUB, :] = acc

    return pl.pallas_call(
        body, name=name, grid=(CONV_DIM // cw, length // tl),
        in_specs=[pl.BlockSpec((tl, cw), lambda j, i: (i, cb0 + j)),
                  pl.BlockSpec((8, cw), lambda j, i: (jnp.maximum(i * (tl // 8) - 1, 0), cb0 + j)),
                  pl.BlockSpec((SSD_CONV, cw), lambda j, i: (0, j)),
                  pl.BlockSpec((1, cw), lambda j, i: (0, j))],
        out_specs=pl.BlockSpec((tl, cw), lambda j, i: (i, j)),
        out_shape=jax.ShapeDtypeStruct((length, CONV_DIM), F32),
        scratch_shapes=[pltpu.VMEM((8 + tl, cw), F32)],
        compiler_params=_params(2),
    )(proj, proj, conv_w, conv_b)


def _conv_bwd(dpre, proj, conv_w, dproj, part, name):
    length = proj.shape[0]
    width = dpre.shape[1]
    tl, cw = min(512, length), 512
    cb0 = _OFF["xs"] // cw + part
    n_t = length // tl

    def body(d_ref, dnext_ref, u_ref, halo_ref, w_ref, _, du_ref, dw_ref, db_ref, dwin, uwin):
        i = pl.program_id(1)
        dwin[0:tl, :] = d_ref[...]
        dwin[tl:tl + 8, :] = jnp.where(i < n_t - 1, dnext_ref[...], 0.0)
        uwin[0:8, :] = jnp.where(i > 0, halo_ref[...], 0.0)
        uwin[8:8 + tl, :] = u_ref[...]
        taps = [w_ref[kk:kk + 1, :] for kk in range(SSD_CONV)]
        sub = CONV_SUB // 2
        acc_w = [jnp.zeros((sub, cw), F32) for _ in range(SSD_CONV)]
        acc_b = jnp.zeros((sub, cw), F32)
        for r in range(0, tl, sub):
            d = dwin[r:r + sub, :]
            du = taps[3] * d
            for kk in range(SSD_CONV - 1):
                du = du + taps[kk] * dwin[3 - kk + r:3 - kk + r + sub, :]
            du_ref[r:r + sub, :] = du.astype(du_ref.dtype)
            for kk in range(SSD_CONV):
                acc_w[kk] = acc_w[kk] + d * uwin[5 + kk + r:5 + kk + r + sub, :]
            acc_b = acc_b + d
        dw = jnp.concatenate([jnp.sum(a, axis=0, keepdims=True) for a in acc_w], axis=0)
        db = jnp.sum(acc_b, axis=0, keepdims=True)

        @pl.when(i == 0)
        def _():
            dw_ref[...] = dw
            db_ref[...] = db

        @pl.when(i > 0)
        def _():
            dw_ref[...] += dw
            db_ref[...] += db

    return pl.pallas_call(
        body, name=name, grid=(width // cw, n_t), input_output_aliases={5: 0},
        in_specs=[pl.BlockSpec((tl, cw), lambda j, i: (i, j)),
                  pl.BlockSpec((8, cw), lambda j, i: (jnp.minimum((i + 1) * (tl // 8), length // 8 - 1), j)),
                  pl.BlockSpec((tl, cw), lambda j, i: (i, cb0 + j)),
                  pl.BlockSpec((8, cw), lambda j, i: (jnp.maximum(i * (tl // 8) - 1, 0), cb0 + j)),
                  pl.BlockSpec((SSD_CONV, cw), lambda j, i: (0, part + j)),
                  pl.BlockSpec(memory_space=pl.ANY)],
        out_specs=[pl.BlockSpec((tl, cw), lambda j, i: (i, cb0 + j)),
                   pl.BlockSpec((SSD_CONV, cw), lambda j, i: (0, j)),
                   pl.BlockSpec((1, cw), lambda j, i: (0, j))],
        out_shape=[jax.ShapeDtypeStruct((length, NP), BF),
                   jax.ShapeDtypeStruct((SSD_CONV, width), F32),
                   jax.ShapeDtypeStruct((1, width), F32)],
        scratch_shapes=[pltpu.VMEM((tl + 8, cw), F32), pltpu.VMEM((8 + tl, cw), F32)],
        compiler_params=_params(2),
    )(dpre, dpre, proj, proj, conv_w, dproj)


def _ssd_in_specs(rev, nc):
    def cidx(c):
        return nc - 1 - c if rev else c
    whole = lambda c: (0, 0)
    return [pl.BlockSpec((BLK, SSD_D_INNER), lambda c: (cidx(c), 0)),
            pl.BlockSpec((BLK, BC_DIM), lambda c: (cidx(c), SSD_D_INNER // BC_DIM)),
            pl.BlockSpec((BLK, BC_DIM), lambda c: (cidx(c), SSD_D_INNER // BC_DIM + 1)),
            pl.BlockSpec((BLK, SSD_D_INNER), lambda c: (cidx(c), _OFF["z"] // SSD_D_INNER)),
            pl.BlockSpec((SSD_HEADS, BLK), lambda c: (0, cidx(c))),
            pl.BlockSpec((SSD_HEADS, 1), whole),
            pl.BlockSpec((SSD_HEADS, 1), whole),
            pl.BlockSpec((1, SSD_D_INNER), whole),
            pl.BlockSpec((1, SSD_D_INNER), whole)]


def _group_args(g, px, pb, pc, z, dtr, dtb, alog, dsk, nw):
    wide, narrow, heads = slice(512 * g, 512 * (g + 1)), slice(BLK * g, BLK * (g + 1)), slice(8 * g, 8 * (g + 1))
    return (px[:, wide], pb[:, narrow], pc[:, narrow], z[:, wide], dtr[heads, :], dtb[heads, :], alog[heads, :],
            dsk[:, wide], nw[:, wide])


def _ssd_fwd(pre, proj, dt_t, dtb, alog, dsk, nw, name):
    length = pre.shape[0]
    nc = length // BLK

    def body(px, pb, pc, z, dtr, dtb_r, al_r, dsk_r, nw_r, y_ref, sin_ref, st):
        @pl.when(pl.program_id(0) == 0)
        def _():
            st[...] = jnp.zeros_like(st)

        for g in range(SSD_GROUPS):
            s_in = st[g]
            sin_ref[g] = s_in
            y, s_out = _ssd_group(*_group_args(g, px, pb, pc, z, dtr, dtb_r, al_r, dsk_r, nw_r), s_in)
            y_ref[:, 512 * g:512 * (g + 1)] = y.astype(y_ref.dtype)
            st[g] = s_out

    return pl.pallas_call(
        body, name=name, grid=(nc,), in_specs=_ssd_in_specs(False, nc),
        out_specs=[pl.BlockSpec((BLK, SSD_D_INNER), lambda c: (c, 0)),
                   pl.BlockSpec((SSD_GROUPS, None, SSD_STATE, 512), lambda c: (0, c, 0, 0))],
        out_shape=[jax.ShapeDtypeStruct((length, SSD_D_INNER), BF),
                   jax.ShapeDtypeStruct((SSD_GROUPS, nc, SSD_STATE, 512), F32)],
        scratch_shapes=[pltpu.VMEM((SSD_GROUPS, SSD_STATE, 512), F32)],
        compiler_params=_params(1),
    )(pre, pre, pre, proj, dt_t, dtb, alog, dsk, nw)


def _ssd_bwd(pre, proj, dt_t, dtb, alog, dsk, nw, s_in, dy, dproj, name):
    length = pre.shape[0]
    nc = length // BLK

    def body(px, pb, pc, z, dtr, dtb_r, al_r, dsk_r, nw_r, sin_r, dy_r, _,
             dpx, dpb, dpc, dz, ddt, ddtb, dal, ddsk, dnw, dst):
        c = pl.program_id(0)

        @pl.when(c == 0)
        def _():
            dst[...] = jnp.zeros_like(dst)

        for grp in range(SSD_GROUPS):
            wide, narrow, heads = slice(512 * grp, 512 * (grp + 1)), slice(BLK * grp, BLK * (grp + 1)), slice(8 * grp, 8 * (grp + 1))
            _, vjp = jax.vjp(_ssd_group, *_group_args(grp, px, pb, pc, z, dtr, dtb_r, al_r, dsk_r, nw_r), sin_r[grp])
            g = vjp((dy_r[:, wide], dst[grp]))
            dpx[:, wide] = g[0]
            dpb[:, narrow] = g[1]
            dpc[:, narrow] = g[2]
            dz[:, wide] = g[3].astype(dz.dtype)
            ddt[heads, :] = g[4]
            dst[grp] = g[9]
            for ref, val, idx in ((ddtb, g[5], (heads, slice(None))), (dal, g[6], (heads, slice(None))),
                                  (ddsk, g[7], (slice(None), wide)), (dnw, g[8], (slice(None), wide))):
                @pl.when(c == 0)
                def _(ref=ref, val=val, idx=idx):
                    ref[idx] = val

                @pl.when(c > 0)
                def _(ref=ref, val=val, idx=idx):
                    ref[idx] += val

    rev = lambda c: nc - 1 - c
    whole = lambda c: (0, 0)
    in_specs = _ssd_in_specs(True, nc) + [
        pl.BlockSpec((SSD_GROUPS, None, SSD_STATE, 512), lambda c: (0, rev(c), 0, 0)),
        pl.BlockSpec((BLK, SSD_D_INNER), lambda c: (rev(c), 0)),
        pl.BlockSpec(memory_space=pl.ANY)]
    return pl.pallas_call(
        body, name=name, grid=(nc,), in_specs=in_specs, input_output_aliases={11: 3},
        out_specs=[pl.BlockSpec((BLK, SSD_D_INNER), lambda c: (rev(c), 0)),
                   pl.BlockSpec((BLK, BC_DIM), lambda c: (rev(c), 0)),
                   pl.BlockSpec((BLK, BC_DIM), lambda c: (rev(c), 0)),
                   pl.BlockSpec((BLK, SSD_D_INNER), lambda c: (rev(c), _OFF["z"] // SSD_D_INNER)),
                   pl.BlockSpec((SSD_HEADS, BLK), lambda c: (0, rev(c))),
                   pl.BlockSpec((SSD_HEADS, 1), whole),
                   pl.BlockSpec((SSD_HEADS, 1), whole),
                   pl.BlockSpec((1, SSD_D_INNER), whole),
                   pl.BlockSpec((1, SSD_D_INNER), whole)],
        out_shape=[jax.ShapeDtypeStruct((length, SSD_D_INNER), F32),
                   jax.ShapeDtypeStruct((length, BC_DIM), F32),
                   jax.ShapeDtypeStruct((length, BC_DIM), F32),
                   jax.ShapeDtypeStruct((length, NP), BF),
                   jax.ShapeDtypeStruct((SSD_HEADS, length), F32),
                   jax.ShapeDtypeStruct((SSD_HEADS, 1), F32),
                   jax.ShapeDtypeStruct((SSD_HEADS, 1), F32),
                   jax.ShapeDtypeStruct((1, SSD_D_INNER), F32),
                   jax.ShapeDtypeStruct((1, SSD_D_INNER), F32)],
        scratch_shapes=[pltpu.VMEM((SSD_GROUPS, SSD_STATE, 512), F32)],
        compiler_params=_params(1),
    )(pre, pre, pre, proj, dt_t, dtb, alog, dsk, nw, s_in, dy, dproj)


assert _OFF["gates"] == 0 and _OFF["z"] % SSD_D_INNER == 0 and _OFF["q"] % D_MODEL == 0 and _OFF["k"] + 512 == NP


def _attn_in_specs():
    prev = lambda n: (0, jnp.maximum(n - 1, 0), 0)
    cur = lambda n: (0, n, 0)
    return [pl.BlockSpec((BLK, D_MODEL), lambda n: (n, _OFF["q"] // D_MODEL)),
            pl.BlockSpec((2, BLK, BLK), prev),
            pl.BlockSpec((2, BLK, BLK), cur),
            pl.BlockSpec((2, BLK, BLK), prev),
            pl.BlockSpec((2, BLK, BLK), cur),
            pl.BlockSpec((1, ATT_HEADS), lambda n: (0, 0))]


def _attn_fwd(proj, k2, v2, sinks, name):
    length = proj.shape[0]

    def body(q, kp, kc, vp, vc, sk, o_ref):
        for kv in range(2):
            cols = slice(512 * kv, 512 * (kv + 1))
            o = _attn_block(q[:, cols], kp[kv], kc[kv], vp[kv], vc[kv], sk[...], pl.program_id(0), kv)
            o_ref[:, cols] = o.astype(o_ref.dtype)

    return pl.pallas_call(
        body, name=name, grid=(length // BLK,), in_specs=_attn_in_specs(),
        out_specs=pl.BlockSpec((BLK, D_MODEL), lambda n: (n, 0)),
        out_shape=jax.ShapeDtypeStruct((length, D_MODEL), BF),
        compiler_params=_params(1),
    )(proj, k2, k2, v2, v2, sinks)


def _attn_bwd(proj, k2, v2, sinks, datt, dproj, name):
    length = proj.shape[0]

    def body(q, kp, kc, vp, vc, sk, do, _, dq, dkp, dkc, dvp, dvc, dsk):
        n = pl.program_id(0)
        dsinks = None
        for kv in range(2):
            cols = slice(512 * kv, 512 * (kv + 1))
            _, vjp = jax.vjp(lambda *a: _attn_block(*a, n, kv), q[:, cols], kp[kv], kc[kv], vp[kv], vc[kv], sk[...])
            g = vjp(do[:, cols].astype(F32))
            dq[:, cols] = g[0].astype(dq.dtype)
            dkp[kv] = g[1]
            dkc[kv] = g[2]
            dvp[kv] = g[3]
            dvc[kv] = g[4]
            dsinks = g[5] if dsinks is None else dsinks + g[5]

        @pl.when(n == 0)
        def _():
            dsk[...] = dsinks

        @pl.when(n > 0)
        def _():
            dsk[...] += dsinks

    blk3 = pl.BlockSpec((2, BLK, BLK), lambda n: (0, n, 0))
    kv_shape = jax.ShapeDtypeStruct((2, length, BLK), F32)
    return pl.pallas_call(
        body, name=name, grid=(length // BLK,),
        in_specs=_attn_in_specs() + [pl.BlockSpec((BLK, D_MODEL), lambda n: (n, 0)), pl.BlockSpec(memory_space=pl.ANY)],
        out_specs=[pl.BlockSpec((BLK, D_MODEL), lambda n: (n, _OFF["q"] // D_MODEL)), blk3, blk3, blk3, blk3,
                   pl.BlockSpec((1, ATT_HEADS), lambda n: (0, 0))],
        out_shape=[jax.ShapeDtypeStruct((length, NP), BF), kv_shape, kv_shape, kv_shape, kv_shape,
                   jax.ShapeDtypeStruct((1, ATT_HEADS), F32)],
        input_output_aliases={7: 0},
        compiler_params=_params(1),
    )(proj, k2, k2, v2, v2, sinks, datt, dproj)


DMA_CHUNK_BYTES = 1 << 20
N_STAGE = 8
LOOKAHEAD = 3


def _piece_chunks(shape, itemsize):
    if len(shape) < 2 or shape[-2] % 16 != 0:
        return [()]
    rows, cols = shape[-2:]
    step = min(rows, max(16, DMA_CHUNK_BYTES // (cols * itemsize) // 16 * 16))
    out = []
    for lead in itertools.product(*[range(d) for d in shape[:-2]]):
        for r0 in range(0, rows, step):
            out.append(lead + (pl.ds(r0, min(step, rows - r0)),))
    return out


def _exchange(srcs, group, bcast, name):
    size = {"c": 2, "xy": 4, "all": 8}[group]
    npeer = size - 1
    n = len(srcs)
    for s in srcs:
        assert bcast or s.shape[0] == size
    pieces = [s.shape if bcast else s.shape[1:] for s in srcs]
    chunks = [_piece_chunks(p, s.dtype.itemsize) for p, s in zip(pieces, srcs)]

    def body(*refs):
        src, out = refs[:n], refs[n:2 * n]
        load_sems, send_sems, recv_sems = refs[2 * n:]
        x, y, c = lax.axis_index("x"), lax.axis_index("y"), lax.axis_index("c")
        if group == "c":
            me = c
        elif group == "xy":
            me = 2 * x + y
        else:
            me = 4 * x + 2 * y + c

        def device(j):
            if group == "c":
                return (x, y, j)
            if group == "xy":
                return (j // 2, j % 2, c)
            return (j // 4, (j // 2) % 2, j % 2)

        def part(ref, idx):
            return ref.at[idx] if idx else ref

        def piece(a, j):
            return src[a] if bcast else src[a].at[j]

        for a in range(n):
            jobs = [(None, idx) for idx in chunks[a]] if bcast else [(d, idx) for idx in chunks[a] for d in range(1, size)]
            full_rows = chunks[a][0][-1].size if chunks[a][0] else None
            slot_shape = (full_rows, pieces[a][-1]) if chunks[a][0] else pieces[a]

            def stream(stage, a=a, jobs=jobs, full_rows=full_rows):
                def slot(q):
                    idx = jobs[q][1]
                    view = stage.at[q % N_STAGE]
                    if idx and idx[-1].size != full_rows:
                        view = view.at[pl.ds(0, idx[-1].size)]
                    return view

                def load(q):
                    d, idx = jobs[q]
                    j = me if d is None else jnp.bitwise_xor(me, d)
                    return pltpu.make_async_copy(part(piece(a, j), idx), slot(q), load_sems.at[q % N_STAGE])

                def sends(q):
                    d, idx = jobs[q]
                    return [pltpu.make_async_remote_copy(
                        src_ref=slot(q), dst_ref=part(out[a].at[me], idx),
                        send_sem=send_sems.at[(q % N_STAGE) * npeer + dd - 1], recv_sem=recv_sems.at[a * size + dd],
                        device_id=device(jnp.bitwise_xor(me, dd)), device_id_type=pl.DeviceIdType.MESH)
                        for dd in (range(1, size) if d is None else (d,))]

                nq = len(jobs)
                for q in range(nq + LOOKAHEAD):
                    if q < nq:
                        if q >= N_STAGE:
                            for cp in sends(q - N_STAGE):
                                cp.wait_send()
                        load(q).start()
                    if q >= LOOKAHEAD:
                        load(q - LOOKAHEAD).wait()
                        for cp in sends(q - LOOKAHEAD):
                            cp.start()
                for q in range(max(0, nq - N_STAGE), nq):
                    for cp in sends(q):
                        cp.wait_send()

            pl.run_scoped(stream, pltpu.VMEM((N_STAGE,) + tuple(slot_shape), srcs[a].dtype))

        for a in range(n):
            for d in range(1, size):
                j = jnp.bitwise_xor(me, d)
                pltpu.make_async_remote_copy(
                    src_ref=piece(a, j), dst_ref=out[a].at[j], send_sem=send_sems.at[0], recv_sem=recv_sems.at[a * size + d],
                    device_id=device(j), device_id_type=pl.DeviceIdType.MESH).wait_recv()

    any_spec = pl.BlockSpec(memory_space=pl.ANY)
    return pl.pallas_call(
        body, name=name, in_specs=[any_spec] * n, out_specs=[any_spec] * n,
        out_shape=[jax.ShapeDtypeStruct((size,) + tuple(p), s.dtype) for p, s in zip(pieces, srcs)],
        scratch_shapes=[pltpu.SemaphoreType.DMA((N_STAGE,)), pltpu.SemaphoreType.DMA((N_STAGE * npeer,)),
                        pltpu.SemaphoreType.DMA((n * size,))],
        compiler_params=pltpu.CompilerParams(vmem_limit_bytes=VMEM_LIMIT_BYTES),
    )(*srcs)


def _with_own(outs, owns, me):
    return [lax.dynamic_update_index_in_dim(o, w, me, 0) for o, w in zip(outs, owns)]


def _sum_slots(arr, name):
    k = arr.shape[0]
    rest = arr.shape[1:]
    width = rest[-1]
    rows_per = math.prod(rest[:-1])
    a2 = arr.reshape(k * rows_per, width)
    tile = rows_per
    for cand in (256, 128, 64, 32, 16, 8):
        if rows_per % cand == 0:
            tile = cand
            break
    nt = rows_per // tile

    def body(*refs):
        acc = refs[0][...]
        for r in refs[1:k]:
            acc = acc + r[...]
        refs[k][...] = acc

    in_specs = [pl.BlockSpec((tile, width), functools.partial(lambda i, s: (s * nt + i, 0), s=s)) for s in range(k)]
    out = pl.pallas_call(
        body, name=name, grid=(nt,), in_specs=in_specs, out_specs=pl.BlockSpec((tile, width), lambda i: (i, 0)),
        out_shape=jax.ShapeDtypeStruct((rows_per, width), arr.dtype), compiler_params=_params(1),
    )(*([a2] * k))
    return out.reshape(rest)


def _sum_pieces(recv, own, own_slotted, me, name):
    k = recv.shape[0]
    rest = recv.shape[1:]
    width = rest[-1]
    rows_per = math.prod(rest[:-1])
    r2 = recv.reshape(k * rows_per, width)
    o2 = own.reshape(-1, width)
    tile = rows_per
    for cand in (256, 128, 64, 32, 16, 8):
        if rows_per % cand == 0:
            tile = cand
            break
    nt = rows_per // tile

    def body(me_ref, *refs):
        mine = refs[k][...].astype(F32)
        acc = None
        for j in range(k):
            term = jnp.where(me_ref[0] == j, mine, refs[j][...].astype(F32))
            acc = term if acc is None else acc + term
        refs[k + 1][...] = acc
        refs[k + 2][...] = acc.astype(BF)

    in_specs = [pl.BlockSpec((tile, width), functools.partial(lambda i, m, s: (s * nt + i, 0), s=s)) for s in range(k)]
    in_specs.append(pl.BlockSpec((tile, width), (lambda i, m: (m[0] * nt + i, 0)) if own_slotted else (lambda i, m: (i, 0))))
    out_spec = pl.BlockSpec((tile, width), lambda i, m: (i, 0))
    out, out_bf = pl.pallas_call(
        body, name=name,
        grid_spec=pltpu.PrefetchScalarGridSpec(num_scalar_prefetch=1, grid=(nt,), in_specs=in_specs, out_specs=[out_spec, out_spec]),
        out_shape=[jax.ShapeDtypeStruct((rows_per, width), F32), jax.ShapeDtypeStruct((rows_per, width), BF)],
        compiler_params=_params(1),
    )(jnp.reshape(me, (1,)).astype(jnp.int32), *([r2] * k), o2)
    return out.reshape(rest), out_bf.reshape(rest)


def _adamw(w, g, m, v, name):
    def fn(w_, g_, m_, v_):
        m1 = ADAM_B1 * m_ + (1.0 - ADAM_B1) * g_
        v1 = ADAM_B2 * v_ + (1.0 - ADAM_B2) * (g_ * g_)
        m_hat = m1 / (1.0 - ADAM_B1 ** ADAM_STEP)
        v_hat = v1 / (1.0 - ADAM_B2 ** ADAM_STEP)
        delta = -ADAM_LR * (m_hat / (jnp.sqrt(v_hat) + ADAM_EPS) + ADAM_WD * w_)
        return (delta, m1, v1), ()

    rows, width = w.shape
    tile = rows
    for cand in (256, 128, 64, 32, 16, 8):
        if rows % cand == 0:
            tile = cand
            break
    o, _ = _rows(fn, [(w, 0, width), (g, 0, width), (m, 0, width), (v, 0, width)], [],
                 [(width, F32)] * 3, [], tile, name)
    return o


def _pack_cols(w):
    parts = [w[..., o:o + wd] for _, o, wd in _PACK]
    parts.append(jnp.zeros(w.shape[:-1] + (_PAD,), w.dtype))
    return jnp.concatenate(parts, axis=-1)


def _unpack_cols(wp):
    order = sorted(_PACK, key=lambda e: e[1])
    return jnp.concatenate([wp[..., _OFF[nm]:_OFF[nm] + wd] for nm, _, wd in order], axis=-1)


def _col_sharded_full(g):
    return jnp.transpose(g, (1, 0, 2)).reshape(g.shape[1], -1)


def _col_sharded_split(full):
    r, cdim = full.shape
    return jnp.transpose(full.reshape(r, 4, cdim // 4), (1, 0, 2))


def _double_heads(t):
    length = t.shape[0]
    h = jnp.transpose(t.reshape(length, 2, 64), (1, 0, 2))
    return jnp.concatenate([h, h], axis=-1)


def _fold_heads(d_cur, d_prev):
    length = d_cur.shape[1]
    d = d_cur + jnp.concatenate([d_prev[:, BLK:], jnp.zeros((2, BLK, BLK), F32)], axis=1)
    d = d[..., :64] + d[..., 64:]
    return jnp.transpose(d, (1, 0, 2)).reshape(length, 128)


def _ln_fwd(a, r, g, b, name):
    def fn(a_, r_, g_, b_):
        y = _layer_norm(ALPHA * a_ + r_, g_, b_)
        return (y, y), ()
    o, _ = _rows(fn, [(a, 0, D_MODEL), (r, 0, D_MODEL)], [g, b], [(D_MODEL, F32), (D_MODEL, BF)], [], 256, name)
    return o


def _ln_bwd(a, r, d1, d2, g, b, name):
    def fn(a_, r_, d1_, d2_, g_, b_):
        _, vjp = jax.vjp(_layer_norm, ALPHA * a_ + r_, g_, b_)
        ds, dg, db = vjp(ALPHA * d1_ + d2_)
        return (ds, ds), (dg, db)
    o, acc = _rows(fn, [(a, 0, D_MODEL), (r, 0, D_MODEL), (d1, 0, D_MODEL), (d2, 0, D_MODEL)], [g, b],
                   [(D_MODEL, F32), (D_MODEL, BF)], [(1, D_MODEL), (1, D_MODEL)], 256, name)
    return o[0], o[1], acc[0], acc[1]


def _layer_fwd(l, h, hb, w, p, hosted=None):
    sv = {"h": h, "hb": hb}
    proj = _mm(hb, w["win"], "nn", F32, 2048, 512, 1024, f"proj{l}", hosted)
    if hosted:
        proj, sv["hosted"] = proj
    sv["proj"] = proj
    pre = _conv_fwd(proj, p["conv_w"], p["conv_b"], f"conv{l}")
    sv["pre"] = pre
    dt_t = jnp.transpose(proj[:, _OFF["dt"]:_OFF["dt"] + SSD_HEADS])
    sv["dt_t"] = dt_t
    yn, s_in = _ssd_fwd(pre, proj, dt_t, p["dt_bias"], p["a_log"], p["d_skip_c"], p["ssd_norm_w"], f"ssd{l}")
    sv["yn"], sv["s_in"] = yn, s_in
    ya = _mm(yn, w["wso"], "nn", F32, 1024, 1024, 2048, f"ssdout{l}")
    k2 = _double_heads(proj[:, _OFF["k"]:_OFF["k"] + 128])
    v2 = _double_heads(proj[:, _OFF["v"]:_OFF["v"] + 128])
    sv["k2"], sv["v2"] = k2, v2
    att = _attn_fwd(proj, k2, v2, p["att_sinks"], f"attn{l}")
    sv["att"] = att
    yb = _mm(att, w["wao"], "nn", F32, 1024, 1024, 1024, f"attout{l}")
    sv["ya"], sv["yb"] = ya, yb
    gcb = _OFF["gates"] // 1024

    def gate_fn(ga, gb, ya_, yb_):
        return (_sigmoid(ga) * ya_ + _sigmoid(gb) * yb_,), ()
    (ub,), _ = _rows(gate_fn, [(proj, gcb, 1024), (proj, gcb + 1, 1024), (ya, 0, 1024), (yb, 0, 1024)], [],
                     [(D_MODEL, BF)], [], 256, f"gate{l}")
    sv["ub"] = ub
    mix = _mm(ub, w["wmo"], "nn", F32, 1024, 1024, 1024, f"mixout{l}")
    sv["mix"] = mix
    h1, h1b = _ln_fwd(h, mix, p["ln_mix_g"], p["ln_mix_b"], f"lnmix{l}")
    sv["h1"], sv["h1b"] = h1, h1b
    gu = _mm(h1b, w["wgu"], "nn", F32, 2048, 512, 1024, f"ffnin{l}")
    sv["gu"] = gu

    def act_fn(g_, u_):
        return (_silu(g_) * u_,), ()
    (act,), _ = _rows(act_fn, [(gu, 0, FFN_HIDDEN), (gu, 1, FFN_HIDDEN)], [], [(FFN_HIDDEN, BF)], [], 256, f"swiglu{l}")
    sv["act"] = act
    ffn = _mm(act, w["wd"], "nn", F32, 1024, 1024, FFN_HIDDEN, f"ffnout{l}")
    sv["ffn"] = ffn
    return sv


def _layer_bwd(l, sv, w, p, ds_f, ds_b, hosted=None):
    g = {}
    g["w_ffn_down"] = _mm(sv["act"], ds_b, "tn", F32, 1408, 1024, 1024, f"dwd{l}")
    dact = _mm(ds_b, w["wd"], "nt", BF, 1024, FFN_HIDDEN, 1024, f"dact{l}")

    def act_bwd(g_, u_, d_):
        _, vjp = jax.vjp(lambda a, b: _silu(a) * b, g_, u_)
        dg_, du_ = vjp(d_.astype(F32))
        return (jnp.concatenate([dg_, du_], axis=1),), ()
    (dgu,), _ = _rows(act_bwd, [(sv["gu"], 0, FFN_HIDDEN), (sv["gu"], 1, FFN_HIDDEN), (dact, 0, FFN_HIDDEN)], [],
                      [(2 * FFN_HIDDEN, BF)], [], 128, f"dswiglu{l}")
    dwgu = _mm(sv["h1b"], dgu, "tn", F32, 1024, 2816, 512, f"dwgu{l}")
    g["w_ffn_gate"], g["w_ffn_up"] = dwgu[:, :FFN_HIDDEN], dwgu[:, FFN_HIDDEN:]
    dh1 = _mm(dgu, w["wgu"], "nt", F32, 1024, 1024, 2816, f"dh1{l}")
    ds2_f, ds2_b, g["ln_mix_g"], g["ln_mix_b"] = _ln_bwd(sv["h"], sv["mix"], ds_f, dh1, p["ln_mix_g"], p["ln_mix_b"], f"dlnmix{l}")
    g["w_mix_out"] = _mm(sv["ub"], ds2_b, "tn", F32, 1024, 1024, 512, f"dwmo{l}")
    du = _mm(ds2_b, w["wmo"], "nt", BF, 1024, 1024, 1024, f"du{l}")
    proj = sv["proj"]
    gcb = _OFF["gates"] // 1024

    def gate_bwd(ga, gb, ya_, yb_, du_):
        _, vjp = jax.vjp(lambda a, b, c, d: _sigmoid(a) * c + _sigmoid(b) * d, ga, gb, ya_, yb_)
        dga, dgb, dya, dyb = vjp(du_.astype(F32))
        return (jnp.concatenate([dga, dgb], axis=1), dya, dyb), ()
    (dproj, dya, dyb), _ = _rows(
        gate_bwd, [(proj, gcb, 1024), (proj, gcb + 1, 1024), (sv["ya"], 0, 1024), (sv["yb"], 0, 1024), (du, 0, 1024)], [],
        [(2048, BF, NP), (D_MODEL, BF), (D_MODEL, BF)], [], 256, f"dgate{l}")
    g["w_att_out"] = _mm(sv["att"], dyb, "tn", F32, 1024, 1024, 512, f"dwao{l}")
    datt = _mm(dyb, w["wao"], "nt", BF, 1024, 1024, 1024, f"datt{l}")
    g["w_ssd_out"] = _mm(sv["yn"], dya, "tn", F32, 1024, 1024, 512, f"dwso{l}")
    dyn = _mm(dya, w["wso"], "nt", F32, 1024, 1024, 1024, f"dyn{l}")
    dproj, dkp, dkc, dvp, dvc, g["att_sinks"] = _attn_bwd(proj, sv["k2"], sv["v2"], p["att_sinks"], datt, dproj, f"dattn{l}")
    (dpx, dpb, dpc, dproj, ddt_t, g["dt_bias"], g["a_log"], ddsk, g["ssd_norm_w"]) = _ssd_bwd(
        sv["pre"], proj, sv["dt_t"], p["dt_bias"], p["a_log"], p["d_skip_c"], p["ssd_norm_w"], sv["s_in"], dyn, dproj, f"dssd{l}")
    g["d_skip"] = jnp.sum(ddsk.reshape(SSD_HEADS, 64), axis=1)
    dws, dbs = [], []
    for nm, dpart, part in (("x", dpx, 0), ("b", dpb, 4), ("c", dpc, 5)):
        dproj, dw_, db_ = _conv_bwd(dpart, proj, p["conv_w"], dproj, part, f"dconv{nm}{l}")
        dws.append(dw_)
        dbs.append(db_)
    g["conv_w"], g["conv_b"] = jnp.concatenate(dws, axis=1), jnp.concatenate(dbs, axis=1)
    length = proj.shape[0]
    tail = jnp.concatenate([_fold_heads(dkc, dkp).astype(BF), _fold_heads(dvc, dvp).astype(BF),
                            jnp.transpose(ddt_t).astype(BF), jnp.zeros((length, _PAD), BF)], axis=1)
    dproj = lax.dynamic_update_slice(dproj, tail, (0, _OFF["k"]))
    dwin = _mm(sv["hb"], dproj, "tn", F32, 1024, 2176, 1024, f"dwin{l}", hosted)
    if hosted:
        dwin, g["hosted"] = dwin
    g["w_in"] = _unpack_cols(dwin)
    dh = _mm(dproj, w["win"], "nt", F32, 1024, 1024, 2176, f"dh{l}")
    return g, ds2_f, dh


def kernel(x, ln_in_g, ln_in_b, w_in, conv_w, conv_b, dt_bias, a_log, d_skip, ssd_norm_w, att_sinks, w_ssd_out, w_att_out, w_mix_out, ln_mix_g, ln_mix_b, w_ffn_gate, w_ffn_up, w_ffn_down, ln_ffn_g, ln_ffn_b, loss_target, m_ln_in_g, m_ln_in_b, m_w_in, m_conv_w, m_conv_b, m_dt_bias, m_a_log, m_d_skip, m_ssd_norm_w, m_att_sinks, m_w_ssd_out, m_w_att_out, m_w_mix_out, m_ln_mix_g, m_ln_mix_b, m_w_ffn_gate, m_w_ffn_up, m_w_ffn_down, m_ln_ffn_g, m_ln_ffn_b, v_ln_in_g, v_ln_in_b, v_w_in, v_conv_w, v_conv_b, v_dt_bias, v_a_log, v_d_skip, v_ssd_norm_w, v_att_sinks, v_w_ssd_out, v_w_att_out, v_w_mix_out, v_ln_mix_g, v_ln_mix_b, v_w_ffn_gate, v_w_ffn_up, v_w_ffn_down, v_ln_ffn_g, v_ln_ffn_b):
    env = dict(locals())
    wts = {n: env[n] for n in WEIGHTS}
    mom1 = {n: env["m_" + n] for n in WEIGHTS}
    mom2 = {n: env["v_" + n] for n in WEIGHTS}
    ci = lax.axis_index("c")
    chip = 2 * lax.axis_index("x") + lax.axis_index("y")
    xs_ = x[0]
    tgt = loss_target[0]

    COLS = ("w_in", "w_ffn_gate", "w_ffn_up")

    def half(t):
        return lax.dynamic_slice_in_dim(t, ci * (t.shape[0] // 2), t.shape[0] // 2, axis=0)

    def assemble(name, parts):
        if name in COLS:
            return jnp.transpose(parts, (0, 2, 1, 3)).reshape(2 * parts.shape[2], 4 * parts.shape[3])
        return jnp.transpose(parts, (1, 0, 2, 3)).reshape(8 * parts.shape[2], parts.shape[3])

    def split(name, gfull):
        r, cdim = gfull.shape
        if name in COLS:
            return jnp.transpose(gfull.reshape(2, r // 2, 4, cdim // 4), (0, 2, 1, 3))
        return jnp.transpose(gfull.reshape(4, 2, r // 8, cdim), (1, 0, 2, 3))

    mine = [[half(wts[n][l].astype(BF)) for n in BIG] for l in range(DEPTH)]
    conv_all, = _with_own(_exchange([conv_w], "xy", True, "gather_conv"), [conv_w], chip)
    conv_full = jnp.transpose(conv_all, (1, 2, 0, 3)).reshape(DEPTH, SSD_CONV, CONV_DIM)

    def layer_weights(l, by_chip):
        by_chip = _with_own(by_chip, mine[l], chip)
        parts = dict(zip(BIG, _with_own(_exchange(by_chip, "c", True, f"gather_cores{l}"), by_chip, ci)))
        fw = {n: assemble(n, parts[n]) for n in BIG}
        return {"win": _pack_cols(fw["w_in"]), "wso": fw["w_ssd_out"], "wao": fw["w_att_out"], "wmo": fw["w_mix_out"],
                "wgu": jnp.concatenate([fw["w_ffn_gate"], fw["w_ffn_up"]], axis=1), "wd": fw["w_ffn_down"]}

    layer_p = []
    for l in range(DEPTH):
        layer_p.append({
            "conv_w": conv_full[l], "conv_b": conv_b[l][None],
            "dt_bias": dt_bias[l][:, None], "a_log": a_log[l][:, None],
            "d_skip_c": jnp.repeat(d_skip[l], 64)[None], "ssd_norm_w": ssd_norm_w[l][None],
            "att_sinks": att_sinks[l][None], "ln_mix_g": ln_mix_g[l][None], "ln_mix_b": ln_mix_b[l][None],
            "ln_ffn_g": ln_ffn_g[l][None], "ln_ffn_b": ln_ffn_b[l][None],
        })

    def ln_in_fn(x_, g_, b_):
        y = _layer_norm(x_, g_, b_)
        return (y, y), ()
    (h, hb), _ = _rows(ln_in_fn, [(xs_, 0, D_MODEL)], [ln_in_g[None], ln_in_b[None]], [(D_MODEL, F32), (D_MODEL, BF)], [], 256, "ln_in")
    layer_w = [layer_weights(0, _exchange(mine[0], "xy", True, "gather_chips0"))]
    saved = []
    for l in range(DEPTH):
        sv = _layer_fwd(l, h, hb, layer_w[l], layer_p[l], (mine[l + 1], True) if l < DEPTH - 1 else None)
        saved.append(sv)
        if l < DEPTH - 1:
            layer_w.append(layer_weights(l + 1, sv["hosted"]))
            h, hb = _ln_fwd(sv["h1"], sv["ffn"], layer_p[l]["ln_ffn_g"], layer_p[l]["ln_ffn_b"], f"lnffn{l}")

    def loss_fn(a_, r_, t_, g_, b_):
        y, vjp = jax.vjp(_layer_norm, ALPHA * a_ + r_, g_, b_)
        err = y - t_
        ds, dg, db = vjp(err * (1.0 / D_MODEL))
        part = 0.5 * jnp.sum(jnp.mean(err * err, axis=-1, keepdims=True), axis=0, keepdims=True)
        return (ds, ds), (dg, db, jnp.broadcast_to(part, (1, BLK)))
    sv = saved[-1]
    (ds_f, ds_b), (dg_last, db_last, loss_part) = _rows(
        loss_fn, [(sv["h1"], 0, D_MODEL), (sv["ffn"], 0, D_MODEL), (tgt, 0, D_MODEL)],
        [layer_p[-1]["ln_ffn_g"], layer_p[-1]["ln_ffn_b"]], [(D_MODEL, F32), (D_MODEL, BF)],
        [(1, D_MODEL), (1, D_MODEL), (1, BLK)], 256, "loss")
    loss = lax.psum(loss_part[0, 0], ("x", "y", "c"))

    def reduce_cores(l, g):
        src = [split(n, g[n]) for n in BIG]
        recv = _exchange(src, "c", False, f"reduce_cores{l}")
        return [_sum_pieces(r, o, True, ci, f"sum_cores{l}_{i}") for i, (r, o) in enumerate(zip(recv, src))]

    def finish_reduce(l, chip_sum, from_chips):
        reduced = [_sum_pieces(r, o, True, chip, f"sum_chips{l}_{i}")[0] for i, (r, (o, _)) in enumerate(zip(from_chips, chip_sum))]
        halves = _with_own(_exchange(reduced, "c", True, f"swap_halves{l}"), reduced, ci)
        return [t.reshape(2 * t.shape[1], t.shape[2]) for t in halves]

    grads = [None] * DEPTH
    chip_sums = [None] * DEPTH
    for l in reversed(range(DEPTH)):
        sv = saved[l]
        if l == DEPTH - 1:
            lg, lb = dg_last, db_last
        else:
            ds_f, ds_b, lg, lb = _ln_bwd(sv["h1"], sv["ffn"], d1, d2, layer_p[l]["ln_ffn_g"], layer_p[l]["ln_ffn_b"], f"dlnffn{l}")
        hosted = ([bf for _, bf in chip_sums[l + 1]], False) if l < DEPTH - 1 else None
        g, d1, d2 = _layer_bwd(l, sv, layer_w[l], layer_p[l], ds_f, ds_b, hosted)
        g["ln_ffn_g"], g["ln_ffn_b"] = lg, lb
        grads[l] = g
        chip_sums[l] = reduce_cores(l, g)

    def ln_in_bwd(x_, d1_, d2_, g_, b_):
        _, vjp = jax.vjp(_layer_norm, x_, g_, b_)
        dx, dg, db = vjp(ALPHA * d1_ + d2_)
        return (dx,), (dg, db)
    (grad_x,), (g_ln_in_g, g_ln_in_b) = _rows(
        ln_in_bwd, [(xs_, 0, D_MODEL), (d1, 0, D_MODEL), (d2, 0, D_MODEL)], [ln_in_g[None], ln_in_b[None]],
        [(D_MODEL, F32)], [(1, D_MODEL), (1, D_MODEL)], 256, "dln_in")

    shard_grads = [None] * DEPTH
    for l in range(1, DEPTH):
        shard_grads[l] = finish_reduce(l, chip_sums[l], grads[l - 1]["hosted"])
    shard_grads[0] = finish_reduce(0, chip_sums[0], _exchange([bf for _, bf in chip_sums[0]], "xy", False, "reduce_chips0"))
    big_grad = {n: jnp.stack([shard_grads[l][i] for l in range(DEPTH)]).reshape(wts[n].shape) for i, n in enumerate(BIG)}

    small_g = {"ln_in_g": g_ln_in_g[0], "ln_in_b": g_ln_in_b[0]}
    for n in SMALL[2:]:
        small_g[n] = jnp.stack([grads[l][n].reshape(wts[n].shape[1:] if n != "conv_w" else (SSD_CONV, CONV_DIM)) for l in range(DEPTH)])
    flat = jnp.concatenate([small_g[n].reshape(-1) for n in SMALL])
    n_small = flat.shape[0]
    width = -(-n_small // 1024) * 1024
    flat = jnp.pad(flat, (0, width - n_small)).reshape(8, width // 8)
    gathered, = _with_own(_exchange([flat], "all", True, "gather_small"), [flat], 2 * chip + ci)
    total = _sum_slots(gathered, "sum_small").reshape(-1)
    small_grad, off = {}, 0
    for n in SMALL:
        shp = small_g[n].shape
        cnt = math.prod(shp)
        small_grad[n] = total[off:off + cnt].reshape(shp)
        off += cnt
    small_grad["conv_w"] = lax.dynamic_slice_in_dim(small_grad["conv_w"], chip * (CONV_DIM // 4), CONV_DIM // 4, axis=2)

    out_g, out_d, out_m, out_v = {}, {}, {}, {}
    for n in BIG:
        shp = wts[n].shape
        two_d = lambda t: t.reshape(shp[0] * shp[1], shp[2])
        d_, m_, v_ = _adamw(two_d(wts[n]), two_d(big_grad[n]), two_d(mom1[n]), two_d(mom2[n]), f"adamw_{n}")
        out_g[n], out_d[n], out_m[n], out_v[n] = big_grad[n], d_.reshape(shp), m_.reshape(shp), v_.reshape(shp)

    def flat_small(d):
        f = jnp.concatenate([d[n].reshape(-1) for n in SMALL])
        return jnp.pad(f, (0, swidth - f.shape[0])).reshape(8, swidth // 8)
    n_sw = sum(math.prod(wts[n].shape) for n in SMALL)
    swidth = -(-n_sw // 1024) * 1024
    d_, m_, v_ = _adamw(flat_small(wts), flat_small(small_grad), flat_small(mom1), flat_small(mom2), "adamw_small")
    off = 0
    for n in SMALL:
        shp = wts[n].shape
        cnt = math.prod(shp)
        out_g[n] = small_grad[n]
        out_d[n] = d_.reshape(-1)[off:off + cnt].reshape(shp)
        out_m[n] = m_.reshape(-1)[off:off + cnt].reshape(shp)
        out_v[n] = v_.reshape(-1)[off:off + cnt].reshape(shp)
        off += cnt

    return (loss, grad_x[None], *[out_g[n] for n in WEIGHTS], *[out_d[n] for n in WEIGHTS],
            *[out_m[n] for n in WEIGHTS], *[out_v[n] for n in WEIGHTS])
```

```python
import functools
import itertools
import math

import jax
import jax.numpy as jnp
from jax import lax
from jax.experimental import pallas as pl
from jax.experimental.pallas import tpu as pltpu

F32 = jnp.float32
BF = jnp.bfloat16

D_MODEL = 1024
DEPTH = 2
ATT_HEADS = 16
ATT_HEAD_DIM = 64
BLK = 128
SSD_D_INNER = 2048
SSD_HEADS = 32
SSD_GROUPS = 4
SSD_STATE = 128
SSD_CONV = 4
BC_DIM = 512
CONV_DIM = 3072
FFN_HIDDEN = 2816
IN_DIM = 8480
LN_EPS = 1e-5
RMS_EPS = 1e-5
ALPHA = (2 * DEPTH) ** 0.25
ADAM_LR = 0.001
ADAM_B1 = 0.9
ADAM_B2 = 0.999
ADAM_EPS = 1e-08
ADAM_WD = 0.01
ADAM_STEP = 10

_PACK = (("gates", 6432, 2048), ("z", 1280, 2048), ("q", 0, 1024), ("xs", 3328, 2048), ("B", 5376, 512),
         ("C", 5888, 512), ("k", 1024, 128), ("v", 1152, 128), ("dt", 6400, 32))
NP = 8704
_OFF = {}
_o = 0
for _n, _, _w in _PACK:
    _OFF[_n] = _o
    _o += _w
_PAD = NP - _o

VMEM_LIMIT_BYTES = 56 * 1024 * 1024
BIG = ("w_in", "w_ssd_out", "w_att_out", "w_mix_out", "w_ffn_gate", "w_ffn_up", "w_ffn_down")
WEIGHTS = ("ln_in_g", "ln_in_b", "w_in", "conv_w", "conv_b", "dt_bias", "a_log", "d_skip", "ssd_norm_w", "att_sinks",
           "w_ssd_out", "w_att_out", "w_mix_out", "ln_mix_g", "ln_mix_b", "w_ffn_gate", "w_ffn_up", "w_ffn_down",
           "ln_ffn_g", "ln_ffn_b")
SMALL = tuple(n for n in WEIGHTS if n not in BIG)


def _params(n_grid):
    return pltpu.CompilerParams(dimension_semantics=("arbitrary",) * n_grid, vmem_limit_bytes=VMEM_LIMIT_BYTES)


def _dot(a, b, ca, cb):
    return lax.dot_general(a.astype(BF), b.astype(BF), (((ca,), (cb,)), ((), ())), preferred_element_type=F32)


@jax.custom_vjp
def _nn(a, b):
    return _dot(a, b, 1, 0)


def _nn_f(a, b):
    return _dot(a, b, 1, 0), (a, b)


def _nn_b(res, g):
    a, b = res
    return _dot(g, b, 1, 1).astype(a.dtype), _dot(a, g, 0, 0).astype(b.dtype)


_nn.defvjp(_nn_f, _nn_b)


@jax.custom_vjp
def _nt(a, b):
    return _dot(a, b, 1, 1)


def _nt_f(a, b):
    return _dot(a, b, 1, 1), (a, b)


def _nt_b(res, g):
    a, b = res
    return _dot(g, b, 1, 0).astype(a.dtype), _dot(g, a, 0, 0).astype(b.dtype)


_nt.defvjp(_nt_f, _nt_b)


@jax.custom_vjp
def _tn(a, b):
    return _dot(a, b, 0, 0)


def _tn_f(a, b):
    return _dot(a, b, 0, 0), (a, b)


def _tn_b(res, g):
    a, b = res
    return _dot(b, g, 1, 1).astype(a.dtype), _dot(a, g, 1, 0).astype(b.dtype)


_tn.defvjp(_tn_f, _tn_b)


def _sigmoid(x):
    return 0.5 * jnp.tanh(0.5 * x) + 0.5


def _silu(x):
    return x * _sigmoid(x)


def _layer_norm(s, g, b):
    mu = jnp.mean(s, axis=-1, keepdims=True)
    sc = s - mu
    var = jnp.mean(sc * sc, axis=-1, keepdims=True)
    return sc * lax.rsqrt(var + LN_EPS) * g + b


def _ssd_group(pre_x, pre_b, pre_c, z, dtr, dtb, alog, dsk, nw, state):
    t = pre_x.shape[0]
    xs, bm, cm = _silu(pre_x), _silu(pre_b), _silu(pre_c)
    dt = jax.nn.softplus(dtr + dtb)
    da = dt * (-jnp.exp(alog))
    row = lax.broadcasted_iota(jnp.int32, (t, t), 0)
    col = lax.broadcasted_iota(jnp.int32, (t, t), 1)
    upper = (row <= col).astype(F32)
    cum = jnp.dot(da, upper, precision=lax.Precision.HIGHEST, preferred_element_type=F32)
    tot = jnp.sum(da, axis=1, keepdims=True)
    cb = _nt(cm, bm)
    tril = row >= col
    first = lax.broadcasted_iota(jnp.int32, (t, BLK), 1) < 64
    first_row = lax.broadcasted_iota(jnp.int32, (1, BLK), 1) < 64

    def col_form(r):
        return jnp.broadcast_to(r, (t, t)).T

    ys, new_state = [], []
    for p in range(4):
        h0, h1 = 2 * p, 2 * p + 1
        sl = slice(BLK * p, BLK * (p + 1))
        x_pair = xs[:, sl] * jnp.where(first, col_form(dt[h0:h0 + 1]), col_form(dt[h1:h1 + 1]))
        y_pair = None
        cc = []
        for h, keep in ((h0, first), (h1, jnp.logical_not(first))):
            cr = jnp.broadcast_to(cum[h:h + 1], (t, t))
            cc.append(cr.T)
            decay = jnp.exp(jnp.where(tril, cc[-1] - cr, -1e30))
            y_h = _nn(decay * cb, jnp.where(keep, x_pair, 0.0))
            y_pair = y_h if y_pair is None else y_pair + y_h
        s_pair = state[:, sl]
        y_pair = y_pair + _nn(cm, s_pair) * jnp.where(first, jnp.exp(cc[0]), jnp.exp(cc[1]))
        to_end = jnp.where(first, jnp.exp(tot[h0:h0 + 1] - cc[0]), jnp.exp(tot[h1:h1 + 1] - cc[1]))
        chunk_decay = jnp.where(first_row, jnp.exp(tot[h0:h0 + 1]), jnp.exp(tot[h1:h1 + 1]))
        new_state.append(s_pair * chunk_decay + _tn(bm, x_pair * to_end))
        ys.append(y_pair + dsk[:, sl] * xs[:, sl])
    y = jnp.concatenate(ys, axis=1) * _silu(z)
    y = y * lax.rsqrt(jnp.mean(y * y, axis=-1, keepdims=True) + RMS_EPS) * nw
    return y, jnp.concatenate(new_state, axis=1)


def _attn_block(q, kp, kc, vp, vc, sinks, n, kv):
    t = q.shape[0]
    kb = jnp.concatenate([kp, kc], axis=0)
    vb = jnp.concatenate([vp, vc], axis=0)
    qi = lax.broadcasted_iota(jnp.int32, (t, 2 * t), 0)
    kj = lax.broadcasted_iota(jnp.int32, (t, 2 * t), 1)
    rel = qi + t - kj
    valid = (rel >= 0) & (rel < t) & ((n * t - t + kj) >= 0)
    relf = rel.astype(F32)
    first = lax.broadcasted_iota(jnp.int32, (t, BLK), 1) < 64
    lane16 = lax.broadcasted_iota(jnp.int32, (1, ATT_HEADS), 1)
    outs = []
    for p in range(4):
        qp = q[:, BLK * p:BLK * (p + 1)] * (ATT_HEAD_DIM ** -0.5)
        o_pair = None
        for half, keep in enumerate((first, jnp.logical_not(first))):
            h = kv * 8 + 2 * p + half
            s = _nt(jnp.where(keep, qp, 0.0), kb)
            slope = jnp.exp((-8.0 * math.log(2.0) / ATT_HEADS) * jnp.asarray(h + 1, F32))
            s = jnp.where(valid, s - slope * relf, -1e30)
            sink = jnp.sum(jnp.where(lane16 == h, sinks, 0.0), axis=1, keepdims=True)
            m = lax.stop_gradient(jnp.maximum(jnp.max(s, axis=1, keepdims=True), sink))
            e = jnp.exp(s - m)
            den = jnp.sum(e, axis=1, keepdims=True) + jnp.exp(sink - m)
            o = _nn(e * (1.0 / den), vb)
            o_pair = o if o_pair is None else jnp.where(first, o_pair, o)
        outs.append(o_pair)
    return jnp.concatenate(outs, axis=1)


def _chip_exchange(src, out, send_sems, recv_sems, bcast):
    x, y, c = lax.axis_index("x"), lax.axis_index("y"), lax.axis_index("c")
    me = 2 * x + y

    def copy(a, d, slot):
        j = jnp.bitwise_xor(me, d)
        return pltpu.make_async_remote_copy(
            src_ref=src[a] if bcast else src[a].at[j], dst_ref=out[a].at[slot],
            send_sem=send_sems.at[a * 4 + d], recv_sem=recv_sems.at[a * 4 + d],
            device_id=(j // 2, j % 2, c), device_id_type=pl.DeviceIdType.MESH)

    pairs = [(a, d) for a in range(len(src)) for d in range(1, 4)]

    def start():
        for a, d in pairs:
            copy(a, d, me).start()

    def wait():
        for a, d in pairs:
            copy(a, d, jnp.bitwise_xor(me, d)).wait_recv()
        for a, d in pairs:
            copy(a, d, me).wait_send()

    return start, wait


def _mm(a, b, mode, out_dtype, tm, tn, tk, name, hosted=None):
    if mode == "nn":
        (m, k), (k2, n) = a.shape, b.shape
    elif mode == "nt":
        (m, k), (n, k2) = a.shape, b.shape
    else:
        (k, m), (k2, n) = a.shape, b.shape
    assert k == k2, (a.shape, b.shape, mode)
    tm, tn, tk = min(tm, m), min(tn, n), min(tk, k)
    assert m % tm == 0 and n % tn == 0 and k % tk == 0, (m, n, k, tm, tn, tk)
    nk = k // tk
    grid = (m // tm, n // tn, nk)
    ca, cb = {"nn": (1, 0), "nt": (1, 1), "tn": (0, 0)}[mode]
    h_arrays, h_bcast = hosted if hosted else ((), False)
    nh = len(h_arrays)

    def body(*refs):
        a_ref, b_ref = refs[:2]
        o_ref = refs[2 + nh]
        scratch = refs[3 + 2 * nh:]
        step = [pl.program_id(ax) for ax in range(3)]
        if nh:
            start, wait = _chip_exchange(refs[2:2 + nh], refs[3 + nh:3 + 2 * nh], scratch[-2], scratch[-1], h_bcast)

            @pl.when((step[0] == 0) & (step[1] == 0) & (step[2] == 0))
            def _():
                start()

        part = _dot(a_ref[...], b_ref[...], ca, cb)
        if nk == 1:
            o_ref[...] = part.astype(o_ref.dtype)
        else:
            acc_ref = scratch[0]

            @pl.when(step[2] == 0)
            def _():
                acc_ref[...] = part

            @pl.when(step[2] > 0)
            def _():
                acc_ref[...] += part

            @pl.when(step[2] == nk - 1)
            def _():
                o_ref[...] = acc_ref[...].astype(o_ref.dtype)

        if nh:
            @pl.when((step[0] == grid[0] - 1) & (step[1] == grid[1] - 1) & (step[2] == nk - 1))
            def _():
                wait()

    a_spec = pl.BlockSpec((tk, tm), lambda i, j, kk: (kk, i)) if mode == "tn" else pl.BlockSpec((tm, tk), lambda i, j, kk: (i, kk))
    b_spec = pl.BlockSpec((tn, tk), lambda i, j, kk: (j, kk)) if mode == "nt" else pl.BlockSpec((tk, tn), lambda i, j, kk: (kk, j))
    any_spec = pl.BlockSpec(memory_space=pl.ANY)
    res = pl.pallas_call(
        body, name=name, grid=grid,
        in_specs=[a_spec, b_spec] + [any_spec] * nh,
        out_specs=[pl.BlockSpec((tm, tn), lambda i, j, kk: (i, j))] + [any_spec] * nh,
        out_shape=[jax.ShapeDtypeStruct((m, n), out_dtype)]
        + [jax.ShapeDtypeStruct(((4,) + s.shape) if h_bcast else s.shape, s.dtype) for s in h_arrays],
        scratch_shapes=([] if nk == 1 else [pltpu.VMEM((tm, tn), F32)])
        + ([pltpu.SemaphoreType.DMA((4 * nh,)), pltpu.SemaphoreType.DMA((4 * nh,))] if nh else []),
        compiler_params=_params(3),
    )(a, b, *h_arrays)
    return (res[0], list(res[1:])) if nh else res[0]


def _rows(fn, rows, params, outs, accs, tile, name):
    length = rows[0][0].shape[0]
    tile = min(tile, length)
    assert length % tile == 0
    nr, npar, no = len(rows), len(params), len(outs)

    def body(*refs):
        vals = [r[...] for r in refs[:nr + npar]]
        o, a = fn(*vals)
        for ref, val in zip(refs[nr + npar:nr + npar + no], o):
            ref[...] = val.astype(ref.dtype)
        i = pl.program_id(0)
        for ref, val in zip(refs[nr + npar + no:], a):
            @pl.when(i == 0)
            def _(ref=ref, val=val):
                ref[...] = val

            @pl.when(i > 0)
            def _(ref=ref, val=val):
                ref[...] += val

    in_specs = [pl.BlockSpec((tile, w), functools.partial(lambda i, cb: (i, cb), cb=cb)) for _, cb, w in rows]
    in_specs += [pl.BlockSpec(p.shape, lambda i: (0, 0)) for p in params]
    outs = [o if len(o) == 3 else (o[0], o[1], o[0]) for o in outs]
    out_specs = [pl.BlockSpec((tile, w), lambda i: (i, 0)) for w, _, _ in outs]
    out_specs += [pl.BlockSpec((r, w), lambda i: (0, 0)) for r, w in accs]
    out_shape = [jax.ShapeDtypeStruct((length, full), dt) for _, dt, full in outs]
    out_shape += [jax.ShapeDtypeStruct((r, w), F32) for r, w in accs]
    res = pl.pallas_call(
        body, name=name, grid=(length // tile,), in_specs=in_specs, out_specs=out_specs, out_shape=out_shape,
        compiler_params=_params(1),
    )(*[r[0] for r in rows], *params)
    return res[:no], res[no:]


CONV_SUB = 32


def _conv_fwd(proj, conv_w, conv_b, name):
    length = proj.shape[0]
    tl, cw = min(512, length), 512
    cb0 = _OFF["xs"] // cw

    def body(u_ref, halo_ref, w_ref, b_ref, o_ref, win):
        i = pl.program_id(1)
        win[0:8, :] = jnp.where(i > 0, halo_ref[...], 0.0)
        win[8:8 + tl, :] = u_ref[...]
        taps = [w_ref[kk:kk + 1, :] for kk in range(SSD_CONV)]
        bias = b_ref[...]
        for r in range(0, tl, CONV_SUB):
            acc = bias + taps[0] * win[5 + r:5 + r + CONV_SUB, :]
            for kk in range(1, SSD_CONV):
                acc = acc + taps[kk] * win[5 + kk + r:5 + kk + r + CONV_SUB, :]
            o_ref[r:r + CONV_SUB, :] = acc

    return pl.pallas_call(
        body, name=name, grid=(CONV_DIM // cw, length // tl),
        in_specs=[pl.BlockSpec((tl, cw), lambda j, i: (i, cb0 + j)),
                  pl.BlockSpec((8, cw), lambda j, i: (jnp.maximum(i * (tl // 8) - 1, 0), cb0 + j)),
                  pl.BlockSpec((SSD_CONV, cw), lambda j, i: (0, j)),
                  pl.BlockSpec((1, cw), lambda j, i: (0, j))],
        out_specs=pl.BlockSpec((tl, cw), lambda j, i: (i, j)),
        out_shape=jax.ShapeDtypeStruct((length, CONV_DIM), F32),
        scratch_shapes=[pltpu.VMEM((8 + tl, cw), F32)],
        compiler_params=_params(2),
    )(proj, proj, conv_w, conv_b)


def _conv_bwd(dpre, proj, conv_w, dproj, part, name):
    length = proj.shape[0]
    width = dpre.shape[1]
    tl, cw = min(512, length), 512
    cb0 = _OFF["xs"] // cw + part
    n_t = length // tl

    def body(d_ref, dnext_ref, u_ref, halo_ref, w_ref, _, du_ref, dw_ref, db_ref, dwin, uwin):
        i = pl.program_id(1)
        dwin[0:tl, :] = d_ref[...]
        dwin[tl:tl + 8, :] = jnp.where(i < n_t - 1, dnext_ref[...], 0.0)
        uwin[0:8, :] = jnp.where(i > 0, halo_ref[...], 0.0)
        uwin[8:8 + tl, :] = u_ref[...]
        taps = [w_ref[kk:kk + 1, :] for kk in range(SSD_CONV)]
        sub = CONV_SUB // 2
        acc_w = [jnp.zeros((sub, cw), F32) for _ in range(SSD_CONV)]
        acc_b = jnp.zeros((sub, cw), F32)
        for r in range(0, tl, sub):
            d = dwin[r:r + sub, :]
            du = taps[3] * d
            for kk in range(SSD_CONV - 1):
                du = du + taps[kk] * dwin[3 - kk + r:3 - kk + r + sub, :]
            du_ref[r:r + sub, :] = du.astype(du_ref.dtype)
            for kk in range(SSD_CONV):
                acc_w[kk] = acc_w[kk] + d * uwin[5 + kk + r:5 + kk + r + sub, :]
            acc_b = acc_b + d
        dw = jnp.concatenate([jnp.sum(a, axis=0, keepdims=True) for a in acc_w], axis=0)
        db = jnp.sum(acc_b, axis=0, keepdims=True)

        @pl.when(i == 0)
        def _():
            dw_ref[...] = dw
            db_ref[...] = db

        @pl.when(i > 0)
        def _():
            dw_ref[...] += dw
            db_ref[...] += db

    return pl.pallas_call(
        body, name=name, grid=(width // cw, n_t), input_output_aliases={5: 0},
        in_specs=[pl.BlockSpec((tl, cw), lambda j, i: (i, j)),
                  pl.BlockSpec((8, cw), lambda j, i: (jnp.minimum((i + 1) * (tl // 8), length // 8 - 1), j)),
                  pl.BlockSpec((tl, cw), lambda j, i: (i, cb0 + j)),
                  pl.BlockSpec((8, cw), lambda j, i: (jnp.maximum(i * (tl // 8) - 1, 0), cb0 + j)),
                  pl.BlockSpec((SSD_CONV, cw), lambda j, i: (0, part + j)),
                  pl.BlockSpec(memory_space=pl.ANY)],
        out_specs=[pl.BlockSpec((tl, cw), lambda j, i: (i, cb0 + j)),
                   pl.BlockSpec((SSD_CONV, cw), lambda j, i: (0, j)),
                   pl.BlockSpec((1, cw), lambda j, i: (0, j))],
        out_shape=[jax.ShapeDtypeStruct((length, NP), BF),
                   jax.ShapeDtypeStruct((SSD_CONV, width), F32),
                   jax.ShapeDtypeStruct((1, width), F32)],
        scratch_shapes=[pltpu.VMEM((tl + 8, cw), F32), pltpu.VMEM((8 + tl, cw), F32)],
        compiler_params=_params(2),
    )(dpre, dpre, proj, proj, conv_w, dproj)


def _ssd_in_specs(rev, nc):
    def cidx(c):
        return nc - 1 - c if rev else c
    whole = lambda c: (0, 0)
    return [pl.BlockSpec((BLK, SSD_D_INNER), lambda c: (cidx(c), 0)),
            pl.BlockSpec((BLK, BC_DIM), lambda c: (cidx(c), SSD_D_INNER // BC_DIM)),
            pl.BlockSpec((BLK, BC_DIM), lambda c: (cidx(c), SSD_D_INNER // BC_DIM + 1)),
            pl.BlockSpec((BLK, SSD_D_INNER), lambda c: (cidx(c), _OFF["z"] // SSD_D_INNER)),
            pl.BlockSpec((SSD_HEADS, BLK), lambda c: (0, cidx(c))),
            pl.BlockSpec((SSD_HEADS, 1), whole),
            pl.BlockSpec((SSD_HEADS, 1), whole),
            pl.BlockSpec((1, SSD_D_INNER), whole),
            pl.BlockSpec((1, SSD_D_INNER), whole)]


def _group_args(g, px, pb, pc, z, dtr, dtb, alog, dsk, nw):
    wide, narrow, heads = slice(512 * g, 512 * (g + 1)), slice(BLK * g, BLK * (g + 1)), slice(8 * g, 8 * (g + 1))
    return (px[:, wide], pb[:, narrow], pc[:, narrow], z[:, wide], dtr[heads, :], dtb[heads, :], alog[heads, :],
            dsk[:, wide], nw[:, wide])


def _ssd_fwd(pre, proj, dt_t, dtb, alog, dsk, nw, name, hosted=None):
    length = pre.shape[0]
    nc = length // BLK
    h_arrays, h_bcast = hosted if hosted else ((), False)
    nh = len(h_arrays)

    def body(*refs):
        px, pb, pc, z, dtr, dtb_r, al_r, dsk_r, nw_r = refs[:9]
        y_ref, sin_ref = refs[9 + nh:11 + nh]
        st = refs[11 + 2 * nh]
        c = pl.program_id(0)
        if nh:
            start, wait = _chip_exchange(refs[9:9 + nh], refs[11 + nh:11 + 2 * nh], refs[-2], refs[-1], h_bcast)

        @pl.when(c == 0)
        def _():
            st[...] = jnp.zeros_like(st)
            if nh:
                start()

        for g in range(SSD_GROUPS):
            s_in = st[g]
            sin_ref[g] = s_in
            y, s_out = _ssd_group(*_group_args(g, px, pb, pc, z, dtr, dtb_r, al_r, dsk_r, nw_r), s_in)
            y_ref[:, 512 * g:512 * (g + 1)] = y.astype(y_ref.dtype)
            st[g] = s_out

        if nh:
            @pl.when(c == nc - 1)
            def _():
                wait()

    any_spec = pl.BlockSpec(memory_space=pl.ANY)
    res = pl.pallas_call(
        body, name=name, grid=(nc,), in_specs=_ssd_in_specs(False, nc) + [any_spec] * nh,
        out_specs=[pl.BlockSpec((BLK, SSD_D_INNER), lambda c: (c, 0)),
                   pl.BlockSpec((SSD_GROUPS, None, SSD_STATE, 512), lambda c: (0, c, 0, 0))] + [any_spec] * nh,
        out_shape=[jax.ShapeDtypeStruct((length, SSD_D_INNER), BF),
                   jax.ShapeDtypeStruct((SSD_GROUPS, nc, SSD_STATE, 512), F32)]
        + [jax.ShapeDtypeStruct(((4,) + s.shape) if h_bcast else s.shape, s.dtype) for s in h_arrays],
        scratch_shapes=[pltpu.VMEM((SSD_GROUPS, SSD_STATE, 512), F32)]
        + ([pltpu.SemaphoreType.DMA((4 * nh,)), pltpu.SemaphoreType.DMA((4 * nh,))] if nh else []),
        compiler_params=_params(1),
    )(pre, pre, pre, proj, dt_t, dtb, alog, dsk, nw, *h_arrays)
    return (res[0], res[1], list(res[2:])) if nh else (res[0], res[1])


def _ssd_bwd(pre, proj, dt_t, dtb, alog, dsk, nw, s_in, dy, dproj, name):
    length = pre.shape[0]
    nc = length // BLK

    def body(px, pb, pc, z, dtr, dtb_r, al_r, dsk_r, nw_r, sin_r, dy_r, _,
             dpx, dpb, dpc, dz, ddt, ddtb, dal, ddsk, dnw, dst):
        c = pl.program_id(0)

        @pl.when(c == 0)
        def _():
            dst[...] = jnp.zeros_like(dst)

        for grp in range(SSD_GROUPS):
            wide, narrow, heads = slice(512 * grp, 512 * (grp + 1)), slice(BLK * grp, BLK * (grp + 1)), slice(8 * grp, 8 * (grp + 1))
            _, vjp = jax.vjp(_ssd_group, *_group_args(grp, px, pb, pc, z, dtr, dtb_r, al_r, dsk_r, nw_r), sin_r[grp])
            g = vjp((dy_r[:, wide], dst[grp]))
            dpx[:, wide] = g[0]
            dpb[:, narrow] = g[1]
            dpc[:, narrow] = g[2]
            dz[:, wide] = g[3].astype(dz.dtype)
            ddt[heads, :] = g[4]
            dst[grp] = g[9]
            for ref, val, idx in ((ddtb, g[5], (heads, slice(None))), (dal, g[6], (heads, slice(None))),
                                  (ddsk, g[7], (slice(None), wide)), (dnw, g[8], (slice(None), wide))):
                @pl.when(c == 0)
                def _(ref=ref, val=val, idx=idx):
                    ref[idx] = val

                @pl.when(c > 0)
                def _(ref=ref, val=val, idx=idx):
                    ref[idx] += val

    rev = lambda c: nc - 1 - c
    whole = lambda c: (0, 0)
    in_specs = _ssd_in_specs(True, nc) + [
        pl.BlockSpec((SSD_GROUPS, None, SSD_STATE, 512), lambda c: (0, rev(c), 0, 0)),
        pl.BlockSpec((BLK, SSD_D_INNER), lambda c: (rev(c), 0)),
        pl.BlockSpec(memory_space=pl.ANY)]
    return pl.pallas_call(
        body, name=name, grid=(nc,), in_specs=in_specs, input_output_aliases={11: 3},
        out_specs=[pl.BlockSpec((BLK, SSD_D_INNER), lambda c: (rev(c), 0)),
                   pl.BlockSpec((BLK, BC_DIM), lambda c: (rev(c), 0)),
                   pl.BlockSpec((BLK, BC_DIM), lambda c: (rev(c), 0)),
                   pl.BlockSpec((BLK, SSD_D_INNER), lambda c: (rev(c), _OFF["z"] // SSD_D_INNER)),
                   pl.BlockSpec((SSD_HEADS, BLK), lambda c: (0, rev(c))),
                   pl.BlockSpec((SSD_HEADS, 1), whole),
                   pl.BlockSpec((SSD_HEADS, 1), whole),
                   pl.BlockSpec((1, SSD_D_INNER), whole),
                   pl.BlockSpec((1, SSD_D_INNER), whole)],
        out_shape=[jax.ShapeDtypeStruct((length, SSD_D_INNER), F32),
                   jax.ShapeDtypeStruct((length, BC_DIM), F32),
                   jax.ShapeDtypeStruct((length, BC_DIM), F32),
                   jax.ShapeDtypeStruct((length, NP), BF),
                   jax.ShapeDtypeStruct((SSD_HEADS, length), F32),
                   jax.ShapeDtypeStruct((SSD_HEADS, 1), F32),
                   jax.ShapeDtypeStruct((SSD_HEADS, 1), F32),
                   jax.ShapeDtypeStruct((1, SSD_D_INNER), F32),
                   jax.ShapeDtypeStruct((1, SSD_D_INNER), F32)],
        scratch_shapes=[pltpu.VMEM((SSD_GROUPS, SSD_STATE, 512), F32)],
        compiler_params=_params(1),
    )(pre, pre, pre, proj, dt_t, dtb, alog, dsk, nw, s_in, dy, dproj)


assert _OFF["gates"] == 0 and _OFF["z"] % SSD_D_INNER == 0 and _OFF["q"] % D_MODEL == 0 and _OFF["k"] + 512 == NP


def _attn_in_specs():
    prev = lambda n: (0, jnp.maximum(n - 1, 0), 0)
    cur = lambda n: (0, n, 0)
    return [pl.BlockSpec((BLK, D_MODEL), lambda n: (n, _OFF["q"] // D_MODEL)),
            pl.BlockSpec((2, BLK, BLK), prev),
            pl.BlockSpec((2, BLK, BLK), cur),
            pl.BlockSpec((2, BLK, BLK), prev),
            pl.BlockSpec((2, BLK, BLK), cur),
            pl.BlockSpec((1, ATT_HEADS), lambda n: (0, 0))]


def _attn_fwd(proj, k2, v2, sinks, name):
    length = proj.shape[0]

    def body(q, kp, kc, vp, vc, sk, o_ref):
        for kv in range(2):
            cols = slice(512 * kv, 512 * (kv + 1))
            o = _attn_block(q[:, cols], kp[kv], kc[kv], vp[kv], vc[kv], sk[...], pl.program_id(0), kv)
            o_ref[:, cols] = o.astype(o_ref.dtype)

    return pl.pallas_call(
        body, name=name, grid=(length // BLK,), in_specs=_attn_in_specs(),
        out_specs=pl.BlockSpec((BLK, D_MODEL), lambda n: (n, 0)),
        out_shape=jax.ShapeDtypeStruct((length, D_MODEL), BF),
        compiler_params=_params(1),
    )(proj, k2, k2, v2, v2, sinks)


def _attn_bwd(proj, k2, v2, sinks, datt, dproj, name):
    length = proj.shape[0]

    def body(q, kp, kc, vp, vc, sk, do, _, dq, dkp, dkc, dvp, dvc, dsk):
        n = pl.program_id(0)
        dsinks = None
        for kv in range(2):
            cols = slice(512 * kv, 512 * (kv + 1))
            _, vjp = jax.vjp(lambda *a: _attn_block(*a, n, kv), q[:, cols], kp[kv], kc[kv], vp[kv], vc[kv], sk[...])
            g = vjp(do[:, cols].astype(F32))
            dq[:, cols] = g[0].astype(dq.dtype)
            dkp[kv] = g[1]
            dkc[kv] = g[2]
            dvp[kv] = g[3]
            dvc[kv] = g[4]
            dsinks = g[5] if dsinks is None else dsinks + g[5]

        @pl.when(n == 0)
        def _():
            dsk[...] = dsinks

        @pl.when(n > 0)
        def _():
            dsk[...] += dsinks

    blk3 = pl.BlockSpec((2, BLK, BLK), lambda n: (0, n, 0))
    kv_shape = jax.ShapeDtypeStruct((2, length, BLK), F32)
    return pl.pallas_call(
        body, name=name, grid=(length // BLK,),
        in_specs=_attn_in_specs() + [pl.BlockSpec((BLK, D_MODEL), lambda n: (n, 0)), pl.BlockSpec(memory_space=pl.ANY)],
        out_specs=[pl.BlockSpec((BLK, D_MODEL), lambda n: (n, _OFF["q"] // D_MODEL)), blk3, blk3, blk3, blk3,
                   pl.BlockSpec((1, ATT_HEADS), lambda n: (0, 0))],
        out_shape=[jax.ShapeDtypeStruct((length, NP), BF), kv_shape, kv_shape, kv_shape, kv_shape,
                   jax.ShapeDtypeStruct((1, ATT_HEADS), F32)],
        input_output_aliases={7: 0},
        compiler_params=_params(1),
    )(proj, k2, k2, v2, v2, sinks, datt, dproj)


DMA_CHUNK_BYTES = 1 << 20
N_STAGE = 8
LOOKAHEAD = 3


def _piece_chunks(shape, itemsize):
    if len(shape) < 2 or shape[-2] % 16 != 0:
        return [()]
    rows, cols = shape[-2:]
    step = min(rows, max(16, DMA_CHUNK_BYTES // (cols * itemsize) // 16 * 16))
    out = []
    for lead in itertools.product(*[range(d) for d in shape[:-2]]):
        for r0 in range(0, rows, step):
            out.append(lead + (pl.ds(r0, min(step, rows - r0)),))
    return out


def _exchange(srcs, group, bcast, name):
    size = {"c": 2, "xy": 4, "all": 8}[group]
    npeer = size - 1
    n = len(srcs)
    for s in srcs:
        assert bcast or s.shape[0] == size
    pieces = [s.shape if bcast else s.shape[1:] for s in srcs]
    chunks = [_piece_chunks(p, s.dtype.itemsize) for p, s in zip(pieces, srcs)]

    def body(*refs):
        src, out = refs[:n], refs[n:2 * n]
        load_sems, send_sems, recv_sems = refs[2 * n:]
        x, y, c = lax.axis_index("x"), lax.axis_index("y"), lax.axis_index("c")
        if group == "c":
            me = c
        elif group == "xy":
            me = 2 * x + y
        else:
            me = 4 * x + 2 * y + c

        def device(j):
            if group == "c":
                return (x, y, j)
            if group == "xy":
                return (j // 2, j % 2, c)
            return (j // 4, (j // 2) % 2, j % 2)

        def part(ref, idx):
            return ref.at[idx] if idx else ref

        def piece(a, j):
            return src[a] if bcast else src[a].at[j]

        for a in range(n):
            jobs = [(None, idx) for idx in chunks[a]] if bcast else [(d, idx) for idx in chunks[a] for d in range(1, size)]
            full_rows = chunks[a][0][-1].size if chunks[a][0] else None
            slot_shape = (full_rows, pieces[a][-1]) if chunks[a][0] else pieces[a]

            def stream(stage, a=a, jobs=jobs, full_rows=full_rows):
                def slot(q):
                    idx = jobs[q][1]
                    view = stage.at[q % N_STAGE]
                    if idx and idx[-1].size != full_rows:
                        view = view.at[pl.ds(0, idx[-1].size)]
                    return view

                def load(q):
                    d, idx = jobs[q]
                    j = me if d is None else jnp.bitwise_xor(me, d)
                    return pltpu.make_async_copy(part(piece(a, j), idx), slot(q), load_sems.at[q % N_STAGE])

                def sends(q):
                    d, idx = jobs[q]
                    return [pltpu.make_async_remote_copy(
                        src_ref=slot(q), dst_ref=part(out[a].at[me], idx),
                        send_sem=send_sems.at[(q % N_STAGE) * npeer + dd - 1], recv_sem=recv_sems.at[a * size + dd],
                        device_id=device(jnp.bitwise_xor(me, dd)), device_id_type=pl.DeviceIdType.MESH)
                        for dd in (range(1, size) if d is None else (d,))]

                nq = len(jobs)
                for q in range(nq + LOOKAHEAD):
                    if q < nq:
                        if q >= N_STAGE:
                            for cp in sends(q - N_STAGE):
                                cp.wait_send()
                        load(q).start()
                    if q >= LOOKAHEAD:
                        load(q - LOOKAHEAD).wait()
                        for cp in sends(q - LOOKAHEAD):
                            cp.start()
                for q in range(max(0, nq - N_STAGE), nq):
                    for cp in sends(q):
                        cp.wait_send()

            pl.run_scoped(stream, pltpu.VMEM((N_STAGE,) + tuple(slot_shape), srcs[a].dtype))

        for a in range(n):
            for d in range(1, size):
                j = jnp.bitwise_xor(me, d)
                pltpu.make_async_remote_copy(
                    src_ref=piece(a, j), dst_ref=out[a].at[j], send_sem=send_sems.at[0], recv_sem=recv_sems.at[a * size + d],
                    device_id=device(j), device_id_type=pl.DeviceIdType.MESH).wait_recv()

    any_spec = pl.BlockSpec(memory_space=pl.ANY)
    return pl.pallas_call(
        body, name=name, in_specs=[any_spec] * n, out_specs=[any_spec] * n,
        out_shape=[jax.ShapeDtypeStruct((size,) + tuple(p), s.dtype) for p, s in zip(pieces, srcs)],
        scratch_shapes=[pltpu.SemaphoreType.DMA((N_STAGE,)), pltpu.SemaphoreType.DMA((N_STAGE * npeer,)),
                        pltpu.SemaphoreType.DMA((n * size,))],
        compiler_params=pltpu.CompilerParams(vmem_limit_bytes=VMEM_LIMIT_BYTES),
    )(*srcs)


def _with_own(outs, owns, me):
    return [lax.dynamic_update_index_in_dim(o, w, me, 0) for o, w in zip(outs, owns)]


def _sum_slots(arr, name):
    k = arr.shape[0]
    rest = arr.shape[1:]
    width = rest[-1]
    rows_per = math.prod(rest[:-1])
    a2 = arr.reshape(k * rows_per, width)
    tile = rows_per
    for cand in (256, 128, 64, 32, 16, 8):
        if rows_per % cand == 0:
            tile = cand
            break
    nt = rows_per // tile

    def body(*refs):
        acc = refs[0][...]
        for r in refs[1:k]:
            acc = acc + r[...]
        refs[k][...] = acc

    in_specs = [pl.BlockSpec((tile, width), functools.partial(lambda i, s: (s * nt + i, 0), s=s)) for s in range(k)]
    out = pl.pallas_call(
        body, name=name, grid=(nt,), in_specs=in_specs, out_specs=pl.BlockSpec((tile, width), lambda i: (i, 0)),
        out_shape=jax.ShapeDtypeStruct((rows_per, width), arr.dtype), compiler_params=_params(1),
    )(*([a2] * k))
    return out.reshape(rest)


def _sum_pieces(recv, own, own_slotted, me, name):
    k = recv.shape[0]
    rest = recv.shape[1:]
    width = rest[-1]
    rows_per = math.prod(rest[:-1])
    r2 = recv.reshape(k * rows_per, width)
    o2 = own.reshape(-1, width)
    tile = rows_per
    for cand in (256, 128, 64, 32, 16, 8):
        if rows_per % cand == 0:
            tile = cand
            break
    nt = rows_per // tile

    def body(me_ref, *refs):
        mine = refs[k][...].astype(F32)
        acc = None
        for j in range(k):
            term = jnp.where(me_ref[0] == j, mine, refs[j][...].astype(F32))
            acc = term if acc is None else acc + term
        refs[k + 1][...] = acc
        refs[k + 2][...] = acc.astype(BF)

    in_specs = [pl.BlockSpec((tile, width), functools.partial(lambda i, m, s: (s * nt + i, 0), s=s)) for s in range(k)]
    in_specs.append(pl.BlockSpec((tile, width), (lambda i, m: (m[0] * nt + i, 0)) if own_slotted else (lambda i, m: (i, 0))))
    out_spec = pl.BlockSpec((tile, width), lambda i, m: (i, 0))
    out, out_bf = pl.pallas_call(
        body, name=name,
        grid_spec=pltpu.PrefetchScalarGridSpec(num_scalar_prefetch=1, grid=(nt,), in_specs=in_specs, out_specs=[out_spec, out_spec]),
        out_shape=[jax.ShapeDtypeStruct((rows_per, width), F32), jax.ShapeDtypeStruct((rows_per, width), BF)],
        compiler_params=_params(1),
    )(jnp.reshape(me, (1,)).astype(jnp.int32), *([r2] * k), o2)
    return out.reshape(rest), out_bf.reshape(rest)


def _adamw(w, g, m, v, name):
    def fn(w_, g_, m_, v_):
        m1 = ADAM_B1 * m_ + (1.0 - ADAM_B1) * g_
        v1 = ADAM_B2 * v_ + (1.0 - ADAM_B2) * (g_ * g_)
        m_hat = m1 / (1.0 - ADAM_B1 ** ADAM_STEP)
        v_hat = v1 / (1.0 - ADAM_B2 ** ADAM_STEP)
        delta = -ADAM_LR * (m_hat / (jnp.sqrt(v_hat) + ADAM_EPS) + ADAM_WD * w_)
        return (delta, m1, v1), ()

    rows, width = w.shape
    tile = rows
    for cand in (256, 128, 64, 32, 16, 8):
        if rows % cand == 0:
            tile = cand
            break
    o, _ = _rows(fn, [(w, 0, width), (g, 0, width), (m, 0, width), (v, 0, width)], [],
                 [(width, F32)] * 3, [], tile, name)
    return o


def _col_pieces(start, width, shard):
    out = []
    while width:
        chip, off = divmod(start, shard)
        take = min(width, shard - off)
        out.append((chip, off, take))
        start, width = start + take, width - take
    return out


def _gather_cols(parts, ranges, pad=0):
    shard = parts.shape[3]
    halves = []
    for h in range(2):
        cols = [parts[h, chip, :, off:off + w] for s, wd in ranges for chip, off, w in _col_pieces(s, wd, shard)]
        if pad:
            cols.append(jnp.zeros((parts.shape[2], pad), parts.dtype))
        halves.append(jnp.concatenate(cols, axis=1))
    return jnp.concatenate(halves, axis=0)


def _scatter_cols(g, ranges, shard):
    starts, pos = {}, 0
    for s, wd in ranges:
        starts[s] = (pos, wd)
        pos += wd
    order = sorted(starts)
    hr = g.shape[0] // 2
    out = []
    for h in range(2):
        per_chip = []
        for chip in range(4):
            cols = []
            for s in order:
                p0, wd = starts[s]
                lo, hi = max(s, chip * shard), min(s + wd, (chip + 1) * shard)
                if lo < hi:
                    cols.append(g[h * hr:(h + 1) * hr, p0 + lo - s:p0 + hi - s])
            per_chip.append(jnp.concatenate(cols, axis=1))
        out.append(jnp.stack(per_chip))
    return jnp.stack(out)


_IN_RANGES = tuple((o, wd) for _, o, wd in _PACK)


def _double_heads(t):
    length = t.shape[0]
    h = jnp.transpose(t.reshape(length, 2, 64), (1, 0, 2))
    return jnp.concatenate([h, h], axis=-1)


def _fold_heads(d_cur, d_prev):
    length = d_cur.shape[1]
    d = d_cur + jnp.concatenate([d_prev[:, BLK:], jnp.zeros((2, BLK, BLK), F32)], axis=1)
    d = d[..., :64] + d[..., 64:]
    return jnp.transpose(d, (1, 0, 2)).reshape(length, 128)


def _ln_fwd(a, r, g, b, name):
    def fn(a_, r_, g_, b_):
        y = _layer_norm(ALPHA * a_ + r_, g_, b_)
        return (y, y), ()
    o, _ = _rows(fn, [(a, 0, D_MODEL), (r, 0, D_MODEL)], [g, b], [(D_MODEL, F32), (D_MODEL, BF)], [], 256, name)
    return o


def _ln_bwd(a, r, d1, d2, g, b, name):
    def fn(a_, r_, d1_, d2_, g_, b_):
        _, vjp = jax.vjp(_layer_norm, ALPHA * a_ + r_, g_, b_)
        ds, dg, db = vjp(ALPHA * d1_ + d2_)
        return (ds, ds), (dg, db)
    o, acc = _rows(fn, [(a, 0, D_MODEL), (r, 0, D_MODEL), (d1, 0, D_MODEL), (d2, 0, D_MODEL)], [g, b],
                   [(D_MODEL, F32), (D_MODEL, BF)], [(1, D_MODEL), (1, D_MODEL)], 256, name)
    return o[0], o[1], acc[0], acc[1]


def _layer_fwd(l, h, hb, w, p, hosted=None):
    sv = {"h": h, "hb": hb}
    proj = _mm(hb, w["win"], "nn", F32, 2048, 512, 1024, f"proj{l}")
    sv["proj"] = proj
    pre = _conv_fwd(proj, p["conv_w"], p["conv_b"], f"conv{l}")
    sv["pre"] = pre
    dt_t = jnp.transpose(proj[:, _OFF["dt"]:_OFF["dt"] + SSD_HEADS])
    sv["dt_t"] = dt_t
    res = _ssd_fwd(pre, proj, dt_t, p["dt_bias"], p["a_log"], p["d_skip_c"], p["ssd_norm_w"], f"ssd{l}", hosted)
    yn, s_in = res[:2]
    if hosted:
        sv["hosted"] = res[2]
    sv["yn"], sv["s_in"] = yn, s_in
    ya = _mm(yn, w["wso"], "nn", F32, 1024, 1024, 2048, f"ssdout{l}")
    k2 = _double_heads(proj[:, _OFF["k"]:_OFF["k"] + 128])
    v2 = _double_heads(proj[:, _OFF["v"]:_OFF["v"] + 128])
    sv["k2"], sv["v2"] = k2, v2
    att = _attn_fwd(proj, k2, v2, p["att_sinks"], f"attn{l}")
    sv["att"] = att
    yb = _mm(att, w["wao"], "nn", F32, 1024, 1024, 1024, f"attout{l}")
    sv["ya"], sv["yb"] = ya, yb
    gcb = _OFF["gates"] // 1024

    def gate_fn(ga, gb, ya_, yb_):
        return (_sigmoid(ga) * ya_ + _sigmoid(gb) * yb_,), ()
    (ub,), _ = _rows(gate_fn, [(proj, gcb, 1024), (proj, gcb + 1, 1024), (ya, 0, 1024), (yb, 0, 1024)], [],
                     [(D_MODEL, BF)], [], 256, f"gate{l}")
    sv["ub"] = ub
    mix = _mm(ub, w["wmo"], "nn", F32, 1024, 1024, 1024, f"mixout{l}")
    sv["mix"] = mix
    h1, h1b = _ln_fwd(h, mix, p["ln_mix_g"], p["ln_mix_b"], f"lnmix{l}")
    sv["h1"], sv["h1b"] = h1, h1b
    gu = _mm(h1b, w["wgu"], "nn", F32, 2048, 512, 1024, f"ffnin{l}")
    sv["gu"] = gu

    def act_fn(g_, u_):
        return (_silu(g_) * u_,), ()
    (act,), _ = _rows(act_fn, [(gu, 0, FFN_HIDDEN), (gu, 1, FFN_HIDDEN)], [], [(FFN_HIDDEN, BF)], [], 256, f"swiglu{l}")
    sv["act"] = act
    ffn = _mm(act, w["wd"], "nn", F32, 1024, 1024, FFN_HIDDEN, f"ffnout{l}")
    sv["ffn"] = ffn
    return sv


def _layer_bwd(l, sv, w, p, ds_f, ds_b, hosted=None):
    g = {}
    g["w_ffn_down"] = _mm(sv["act"], ds_b, "tn", F32, 1408, 1024, 1024, f"dwd{l}")
    dact = _mm(ds_b, w["wd"], "nt", BF, 1024, FFN_HIDDEN, 1024, f"dact{l}")

    def act_bwd(g_, u_, d_):
        _, vjp = jax.vjp(lambda a, b: _silu(a) * b, g_, u_)
        dg_, du_ = vjp(d_.astype(F32))
        return (jnp.concatenate([dg_, du_], axis=1),), ()
    (dgu,), _ = _rows(act_bwd, [(sv["gu"], 0, FFN_HIDDEN), (sv["gu"], 1, FFN_HIDDEN), (dact, 0, FFN_HIDDEN)], [],
                      [(2 * FFN_HIDDEN, BF)], [], 128, f"dswiglu{l}")
    dwgu = _mm(sv["h1b"], dgu, "tn", F32, 1024, 2816, 512, f"dwgu{l}")
    g["w_ffn_gate"], g["w_ffn_up"] = dwgu[:, :FFN_HIDDEN], dwgu[:, FFN_HIDDEN:]
    dh1 = _mm(dgu, w["wgu"], "nt", F32, 1024, 1024, 2816, f"dh1{l}")
    ds2_f, ds2_b, g["ln_mix_g"], g["ln_mix_b"] = _ln_bwd(sv["h"], sv["mix"], ds_f, dh1, p["ln_mix_g"], p["ln_mix_b"], f"dlnmix{l}")
    g["w_mix_out"] = _mm(sv["ub"], ds2_b, "tn", F32, 1024, 1024, 512, f"dwmo{l}")
    du = _mm(ds2_b, w["wmo"], "nt", BF, 1024, 1024, 1024, f"du{l}")
    proj = sv["proj"]
    gcb = _OFF["gates"] // 1024

    def gate_bwd(ga, gb, ya_, yb_, du_):
        _, vjp = jax.vjp(lambda a, b, c, d: _sigmoid(a) * c + _sigmoid(b) * d, ga, gb, ya_, yb_)
        dga, dgb, dya, dyb = vjp(du_.astype(F32))
        return (jnp.concatenate([dga, dgb], axis=1), dya, dyb), ()
    (dproj, dya, dyb), _ = _rows(
        gate_bwd, [(proj, gcb, 1024), (proj, gcb + 1, 1024), (sv["ya"], 0, 1024), (sv["yb"], 0, 1024), (du, 0, 1024)], [],
        [(2048, BF, NP), (D_MODEL, BF), (D_MODEL, BF)], [], 256, f"dgate{l}")
    g["w_att_out"] = _mm(sv["att"], dyb, "tn", F32, 1024, 1024, 512, f"dwao{l}")
    datt = _mm(dyb, w["wao"], "nt", BF, 1024, 1024, 1024, f"datt{l}")
    g["w_ssd_out"] = _mm(sv["yn"], dya, "tn", F32, 1024, 1024, 512, f"dwso{l}")
    dyn = _mm(dya, w["wso"], "nt", F32, 1024, 1024, 1024, f"dyn{l}")
    dproj, dkp, dkc, dvp, dvc, g["att_sinks"] = _attn_bwd(proj, sv["k2"], sv["v2"], p["att_sinks"], datt, dproj, f"dattn{l}")
    (dpx, dpb, dpc, dproj, ddt_t, g["dt_bias"], g["a_log"], ddsk, g["ssd_norm_w"]) = _ssd_bwd(
        sv["pre"], proj, sv["dt_t"], p["dt_bias"], p["a_log"], p["d_skip_c"], p["ssd_norm_w"], sv["s_in"], dyn, dproj, f"dssd{l}")
    g["d_skip"] = jnp.sum(ddsk.reshape(SSD_HEADS, 64), axis=1)
    dws, dbs = [], []
    for nm, dpart, part in (("x", dpx, 0), ("b", dpb, 4), ("c", dpc, 5)):
        dproj, dw_, db_ = _conv_bwd(dpart, proj, p["conv_w"], dproj, part, f"dconv{nm}{l}")
        dws.append(dw_)
        dbs.append(db_)
    g["conv_w"], g["conv_b"] = jnp.concatenate(dws, axis=1), jnp.concatenate(dbs, axis=1)
    length = proj.shape[0]
    tail = jnp.concatenate([_fold_heads(dkc, dkp).astype(BF), _fold_heads(dvc, dvp).astype(BF),
                            jnp.transpose(ddt_t).astype(BF), jnp.zeros((length, _PAD), BF)], axis=1)
    dproj = lax.dynamic_update_slice(dproj, tail, (0, _OFF["k"]))
    g["w_in"] = _mm(sv["hb"], dproj, "tn", F32, 1024, 2176, 1024, f"dwin{l}")
    dh = _mm(dproj, w["win"], "nt", F32, 1024, 1024, 2176, f"dh{l}", hosted)
    if hosted:
        dh, g["hosted"] = dh
    return g, ds2_f, dh


def kernel(x, ln_in_g, ln_in_b, w_in, conv_w, conv_b, dt_bias, a_log, d_skip, ssd_norm_w, att_sinks, w_ssd_out, w_att_out, w_mix_out, ln_mix_g, ln_mix_b, w_ffn_gate, w_ffn_up, w_ffn_down, ln_ffn_g, ln_ffn_b, loss_target, m_ln_in_g, m_ln_in_b, m_w_in, m_conv_w, m_conv_b, m_dt_bias, m_a_log, m_d_skip, m_ssd_norm_w, m_att_sinks, m_w_ssd_out, m_w_att_out, m_w_mix_out, m_ln_mix_g, m_ln_mix_b, m_w_ffn_gate, m_w_ffn_up, m_w_ffn_down, m_ln_ffn_g, m_ln_ffn_b, v_ln_in_g, v_ln_in_b, v_w_in, v_conv_w, v_conv_b, v_dt_bias, v_a_log, v_d_skip, v_ssd_norm_w, v_att_sinks, v_w_ssd_out, v_w_att_out, v_w_mix_out, v_ln_mix_g, v_ln_mix_b, v_w_ffn_gate, v_w_ffn_up, v_w_ffn_down, v_ln_ffn_g, v_ln_ffn_b):
    env = dict(locals())
    wts = {n: env[n] for n in WEIGHTS}
    mom1 = {n: env["m_" + n] for n in WEIGHTS}
    mom2 = {n: env["v_" + n] for n in WEIGHTS}
    ci = lax.axis_index("c")
    chip = 2 * lax.axis_index("x") + lax.axis_index("y")
    xs_ = x[0]
    tgt = loss_target[0]

    COLS = ("w_in", "w_ffn_gate", "w_ffn_up")

    def half(t):
        return lax.dynamic_slice_in_dim(t, ci * (t.shape[0] // 2), t.shape[0] // 2, axis=0)

    def assemble(name, parts):
        if name == "w_in":
            return _gather_cols(parts, _IN_RANGES, _PAD)
        if name in COLS:
            return _gather_cols(parts, ((0, 4 * parts.shape[3]),))
        return jnp.concatenate([parts[h, s] for s in range(4) for h in range(2)], axis=0)

    def split(name, gfull):
        if name == "w_in":
            return _scatter_cols(gfull, _IN_RANGES, IN_DIM // 4)
        if name in COLS:
            return _scatter_cols(gfull, ((0, gfull.shape[1]),), gfull.shape[1] // 4)
        hr = gfull.shape[0] // 8
        return jnp.stack([jnp.stack([gfull[(2 * s + h) * hr:(2 * s + h + 1) * hr] for s in range(4)]) for h in range(2)])

    mine = [[half(wts[n][l].astype(BF)) for n in BIG] for l in range(DEPTH)]
    conv_all, = _with_own(_exchange([conv_w], "xy", True, "gather_conv"), [conv_w], chip)
    conv_full = jnp.transpose(conv_all, (1, 2, 0, 3)).reshape(DEPTH, SSD_CONV, CONV_DIM)

    def layer_weights(l, by_chip):
        by_chip = _with_own(by_chip, mine[l], chip)
        parts = dict(zip(BIG, _with_own(_exchange(by_chip, "c", True, f"gather_cores{l}"), by_chip, ci)))
        fw = {n: assemble(n, parts[n]) for n in BIG}
        return {"win": fw["w_in"], "wso": fw["w_ssd_out"], "wao": fw["w_att_out"], "wmo": fw["w_mix_out"],
                "wgu": jnp.concatenate([fw["w_ffn_gate"], fw["w_ffn_up"]], axis=1), "wd": fw["w_ffn_down"]}

    layer_p = []
    for l in range(DEPTH):
        layer_p.append({
            "conv_w": conv_full[l], "conv_b": conv_b[l][None],
            "dt_bias": dt_bias[l][:, None], "a_log": a_log[l][:, None],
            "d_skip_c": jnp.repeat(d_skip[l], 64)[None], "ssd_norm_w": ssd_norm_w[l][None],
            "att_sinks": att_sinks[l][None], "ln_mix_g": ln_mix_g[l][None], "ln_mix_b": ln_mix_b[l][None],
            "ln_ffn_g": ln_ffn_g[l][None], "ln_ffn_b": ln_ffn_b[l][None],
        })

    def ln_in_fn(x_, g_, b_):
        y = _layer_norm(x_, g_, b_)
        return (y, y), ()
    (h, hb), _ = _rows(ln_in_fn, [(xs_, 0, D_MODEL)], [ln_in_g[None], ln_in_b[None]], [(D_MODEL, F32), (D_MODEL, BF)], [], 256, "ln_in")
    layer_w = [layer_weights(0, _exchange(mine[0], "xy", True, "gather_chips0"))]
    saved = []
    for l in range(DEPTH):
        sv = _layer_fwd(l, h, hb, layer_w[l], layer_p[l], (mine[l + 1], True) if l < DEPTH - 1 else None)
        saved.append(sv)
        if l < DEPTH - 1:
            layer_w.append(layer_weights(l + 1, sv["hosted"]))
            h, hb = _ln_fwd(sv["h1"], sv["ffn"], layer_p[l]["ln_ffn_g"], layer_p[l]["ln_ffn_b"], f"lnffn{l}")

    def loss_fn(a_, r_, t_, g_, b_):
        y, vjp = jax.vjp(_layer_norm, ALPHA * a_ + r_, g_, b_)
        err = y - t_
        ds, dg, db = vjp(err * (1.0 / D_MODEL))
        part = 0.5 * jnp.sum(jnp.mean(err * err, axis=-1, keepdims=True), axis=0, keepdims=True)
        return (ds, ds), (dg, db, jnp.broadcast_to(part, (1, BLK)))
    sv = saved[-1]
    (ds_f, ds_b), (dg_last, db_last, loss_part) = _rows(
        loss_fn, [(sv["h1"], 0, D_MODEL), (sv["ffn"], 0, D_MODEL), (tgt, 0, D_MODEL)],
        [layer_p[-1]["ln_ffn_g"], layer_p[-1]["ln_ffn_b"]], [(D_MODEL, F32), (D_MODEL, BF)],
        [(1, D_MODEL), (1, D_MODEL), (1, BLK)], 256, "loss")
    loss = lax.psum(loss_part[0, 0], ("x", "y", "c"))

    def reduce_cores(l, g):
        src = [split(n, g[n]) for n in BIG]
        recv = _exchange(src, "c", False, f"reduce_cores{l}")
        return [_sum_pieces(r, o, True, ci, f"sum_cores{l}_{i}") for i, (r, o) in enumerate(zip(recv, src))]

    def finish_reduce(l, chip_sum, from_chips):
        reduced = [_sum_pieces(r, o, True, chip, f"sum_chips{l}_{i}")[0] for i, (r, (o, _)) in enumerate(zip(from_chips, chip_sum))]
        halves = _with_own(_exchange(reduced, "c", True, f"swap_halves{l}"), reduced, ci)
        return [t.reshape(2 * t.shape[1], t.shape[2]) for t in halves]

    grads = [None] * DEPTH
    chip_sums = [None] * DEPTH
    for l in reversed(range(DEPTH)):
        sv = saved[l]
        if l == DEPTH - 1:
            lg, lb = dg_last, db_last
        else:
            ds_f, ds_b, lg, lb = _ln_bwd(sv["h1"], sv["ffn"], d1, d2, layer_p[l]["ln_ffn_g"], layer_p[l]["ln_ffn_b"], f"dlnffn{l}")
        hosted = ([bf for _, bf in chip_sums[l + 1]], False) if l < DEPTH - 1 else None
        g, d1, d2 = _layer_bwd(l, sv, layer_w[l], layer_p[l], ds_f, ds_b, hosted)
        g["ln_ffn_g"], g["ln_ffn_b"] = lg, lb
        grads[l] = g
        chip_sums[l] = reduce_cores(l, g)

    def ln_in_bwd(x_, d1_, d2_, g_, b_):
        _, vjp = jax.vjp(_layer_norm, x_, g_, b_)
        dx, dg, db = vjp(ALPHA * d1_ + d2_)
        return (dx,), (dg, db)
    (grad_x,), (g_ln_in_g, g_ln_in_b) = _rows(
        ln_in_bwd, [(xs_, 0, D_MODEL), (d1, 0, D_MODEL), (d2, 0, D_MODEL)], [ln_in_g[None], ln_in_b[None]],
        [(D_MODEL, F32)], [(1, D_MODEL), (1, D_MODEL)], 256, "dln_in")

    shard_grads = [None] * DEPTH
    for l in range(1, DEPTH):
        shard_grads[l] = finish_reduce(l, chip_sums[l], grads[l - 1]["hosted"])
    shard_grads[0] = finish_reduce(0, chip_sums[0], _exchange([bf for _, bf in chip_sums[0]], "xy", False, "reduce_chips0"))
    big_grad = {n: jnp.stack([shard_grads[l][i] for l in range(DEPTH)]).reshape(wts[n].shape) for i, n in enumerate(BIG)}

    small_g = {"ln_in_g": g_ln_in_g[0], "ln_in_b": g_ln_in_b[0]}
    for n in SMALL[2:]:
        small_g[n] = jnp.stack([grads[l][n].reshape(wts[n].shape[1:] if n != "conv_w" else (SSD_CONV, CONV_DIM)) for l in range(DEPTH)])
    flat = jnp.concatenate([small_g[n].reshape(-1) for n in SMALL])
    n_small = flat.shape[0]
    width = -(-n_small // 1024) * 1024
    flat = jnp.pad(flat, (0, width - n_small)).reshape(8, width // 8)
    gathered, = _with_own(_exchange([flat], "all", True, "gather_small"), [flat], 2 * chip + ci)
    total = _sum_slots(gathered, "sum_small").reshape(-1)
    small_grad, off = {}, 0
    for n in SMALL:
        shp = small_g[n].shape
        cnt = math.prod(shp)
        small_grad[n] = total[off:off + cnt].reshape(shp)
        off += cnt
    small_grad["conv_w"] = lax.dynamic_slice_in_dim(small_grad["conv_w"], chip * (CONV_DIM // 4), CONV_DIM // 4, axis=2)

    out_g, out_d, out_m, out_v = {}, {}, {}, {}
    for n in BIG:
        shp = wts[n].shape
        two_d = lambda t: t.reshape(shp[0] * shp[1], shp[2])
        d_, m_, v_ = _adamw(two_d(wts[n]), two_d(big_grad[n]), two_d(mom1[n]), two_d(mom2[n]), f"adamw_{n}")
        out_g[n], out_d[n], out_m[n], out_v[n] = big_grad[n], d_.reshape(shp), m_.reshape(shp), v_.reshape(shp)

    def flat_small(d):
        f = jnp.concatenate([d[n].reshape(-1) for n in SMALL])
        return jnp.pad(f, (0, swidth - f.shape[0])).reshape(8, swidth // 8)
    n_sw = sum(math.prod(wts[n].shape) for n in SMALL)
    swidth = -(-n_sw // 1024) * 1024
    d_, m_, v_ = _adamw(flat_small(wts), flat_small(small_grad), flat_small(mom1), flat_small(mom2), "adamw_small")
    off = 0
    for n in SMALL:
        shp = wts[n].shape
        cnt = math.prod(shp)
        out_g[n] = small_grad[n]
        out_d[n] = d_.reshape(-1)[off:off + cnt].reshape(shp)
        out_m[n] = m_.reshape(-1)[off:off + cnt].reshape(shp)
        out_v[n] = v_.reshape(-1)[off:off + cnt].reshape(shp)
        off += cnt

    return (loss, grad_x[None], *[out_g[n] for n in WEIGHTS], *[out_d[n] for n in WEIGHTS],
            *[out_m[n] for n in WEIGHTS], *[out_v[n] for n in WEIGHTS])
```

```python
import functools
import itertools
import math

import jax
import jax.numpy as jnp
from jax import lax
from jax.experimental import pallas as pl
from jax.experimental.pallas import tpu as pltpu

F32 = jnp.float32
BF = jnp.bfloat16

D_MODEL = 1024
DEPTH = 2
ATT_HEADS = 16
ATT_HEAD_DIM = 64
BLK = 128
SSD_D_INNER = 2048
SSD_HEADS = 32
SSD_GROUPS = 4
SSD_STATE = 128
SSD_CONV = 4
BC_DIM = 512
CONV_DIM = 3072
FFN_HIDDEN = 2816
IN_DIM = 8480
LN_EPS = 1e-5
RMS_EPS = 1e-5
ALPHA = (2 * DEPTH) ** 0.25
ADAM_LR = 0.001
ADAM_B1 = 0.9
ADAM_B2 = 0.999
ADAM_EPS = 1e-08
ADAM_WD = 0.01
ADAM_STEP = 10

_PACK = (("gates", 6432, 2048), ("z", 1280, 2048), ("q", 0, 1024), ("xs", 3328, 2048), ("B", 5376, 512),
         ("C", 5888, 512), ("k", 1024, 128), ("v", 1152, 128), ("dt", 6400, 32))
NP = 8704
_OFF = {}
_o = 0
for _n, _, _w in _PACK:
    _OFF[_n] = _o
    _o += _w
_PAD = NP - _o

VMEM_LIMIT_BYTES = 56 * 1024 * 1024
BIG = ("w_in", "w_ssd_out", "w_att_out", "w_mix_out", "w_ffn_gate", "w_ffn_up", "w_ffn_down")
WEIGHTS = ("ln_in_g", "ln_in_b", "w_in", "conv_w", "conv_b", "dt_bias", "a_log", "d_skip", "ssd_norm_w", "att_sinks",
           "w_ssd_out", "w_att_out", "w_mix_out", "ln_mix_g", "ln_mix_b", "w_ffn_gate", "w_ffn_up", "w_ffn_down",
           "ln_ffn_g", "ln_ffn_b")
SMALL = tuple(n for n in WEIGHTS if n not in BIG)


def _params(n_grid):
    return pltpu.CompilerParams(dimension_semantics=("arbitrary",) * n_grid, vmem_limit_bytes=VMEM_LIMIT_BYTES)


def _dot(a, b, ca, cb):
    return lax.dot_general(a.astype(BF), b.astype(BF), (((ca,), (cb,)), ((), ())), preferred_element_type=F32)


@jax.custom_vjp
def _nn(a, b):
    return _dot(a, b, 1, 0)


def _nn_f(a, b):
    return _dot(a, b, 1, 0), (a, b)


def _nn_b(res, g):
    a, b = res
    return _dot(g, b, 1, 1).astype(a.dtype), _dot(a, g, 0, 0).astype(b.dtype)


_nn.defvjp(_nn_f, _nn_b)


@jax.custom_vjp
def _nt(a, b):
    return _dot(a, b, 1, 1)


def _nt_f(a, b):
    return _dot(a, b, 1, 1), (a, b)


def _nt_b(res, g):
    a, b = res
    return _dot(g, b, 1, 0).astype(a.dtype), _dot(g, a, 0, 0).astype(b.dtype)


_nt.defvjp(_nt_f, _nt_b)


@jax.custom_vjp
def _tn(a, b):
    return _dot(a, b, 0, 0)


def _tn_f(a, b):
    return _dot(a, b, 0, 0), (a, b)


def _tn_b(res, g):
    a, b = res
    return _dot(b, g, 1, 1).astype(a.dtype), _dot(a, g, 1, 0).astype(b.dtype)


_tn.defvjp(_tn_f, _tn_b)


def _sigmoid(x):
    return 0.5 * jnp.tanh(0.5 * x) + 0.5


def _silu(x):
    return x * _sigmoid(x)


def _layer_norm(s, g, b):
    mu = jnp.mean(s, axis=-1, keepdims=True)
    sc = s - mu
    var = jnp.mean(sc * sc, axis=-1, keepdims=True)
    return sc * lax.rsqrt(var + LN_EPS) * g + b


def _ssd_group(pre_x, pre_b, pre_c, z, dtr, dtb, alog, dsk, nw, state):
    t = pre_x.shape[0]
    xs, bm, cm = _silu(pre_x), _silu(pre_b), _silu(pre_c)
    dt = jax.nn.softplus(dtr + dtb)
    da = dt * (-jnp.exp(alog))
    row = lax.broadcasted_iota(jnp.int32, (t, t), 0)
    col = lax.broadcasted_iota(jnp.int32, (t, t), 1)
    upper = (row <= col).astype(F32)
    cum = jnp.dot(da, upper, precision=lax.Precision.HIGHEST, preferred_element_type=F32)
    tot = jnp.sum(da, axis=1, keepdims=True)
    cb = _nt(cm, bm)
    tril = row >= col
    first = lax.broadcasted_iota(jnp.int32, (t, BLK), 1) < 64
    first_row = lax.broadcasted_iota(jnp.int32, (1, BLK), 1) < 64

    def col_form(r):
        return jnp.broadcast_to(r, (t, t)).T

    ys, new_state = [], []
    for p in range(4):
        h0, h1 = 2 * p, 2 * p + 1
        sl = slice(BLK * p, BLK * (p + 1))
        x_pair = xs[:, sl] * jnp.where(first, col_form(dt[h0:h0 + 1]), col_form(dt[h1:h1 + 1]))
        y_pair = None
        cc = []
        for h, keep in ((h0, first), (h1, jnp.logical_not(first))):
            cr = jnp.broadcast_to(cum[h:h + 1], (t, t))
            cc.append(cr.T)
            decay = jnp.exp(jnp.where(tril, cc[-1] - cr, -1e30))
            y_h = _nn(decay * cb, jnp.where(keep, x_pair, 0.0))
            y_pair = y_h if y_pair is None else y_pair + y_h
        s_pair = state[:, sl]
        y_pair = y_pair + _nn(cm, s_pair) * jnp.where(first, jnp.exp(cc[0]), jnp.exp(cc[1]))
        to_end = jnp.where(first, jnp.exp(tot[h0:h0 + 1] - cc[0]), jnp.exp(tot[h1:h1 + 1] - cc[1]))
        chunk_decay = jnp.where(first_row, jnp.exp(tot[h0:h0 + 1]), jnp.exp(tot[h1:h1 + 1]))
        new_state.append(s_pair * chunk_decay + _tn(bm, x_pair * to_end))
        ys.append(y_pair + dsk[:, sl] * xs[:, sl])
    y = jnp.concatenate(ys, axis=1) * _silu(z)
    y = y * lax.rsqrt(jnp.mean(y * y, axis=-1, keepdims=True) + RMS_EPS) * nw
    return y, jnp.concatenate(new_state, axis=1)


def _attn_block(q, kp, kc, vp, vc, sinks, n, kv):
    t = q.shape[0]
    kb = jnp.concatenate([kp, kc], axis=0)
    vb = jnp.concatenate([vp, vc], axis=0)
    qi = lax.broadcasted_iota(jnp.int32, (t, 2 * t), 0)
    kj = lax.broadcasted_iota(jnp.int32, (t, 2 * t), 1)
    rel = qi + t - kj
    valid = (rel >= 0) & (rel < t) & ((n * t - t + kj) >= 0)
    relf = rel.astype(F32)
    first = lax.broadcasted_iota(jnp.int32, (t, BLK), 1) < 64
    lane16 = lax.broadcasted_iota(jnp.int32, (1, ATT_HEADS), 1)
    outs = []
    for p in range(4):
        qp = q[:, BLK * p:BLK * (p + 1)] * (ATT_HEAD_DIM ** -0.5)
        o_pair = None
        for half, keep in enumerate((first, jnp.logical_not(first))):
            h = kv * 8 + 2 * p + half
            s = _nt(jnp.where(keep, qp, 0.0), kb)
            slope = jnp.exp((-8.0 * math.log(2.0) / ATT_HEADS) * jnp.asarray(h + 1, F32))
            s = jnp.where(valid, s - slope * relf, -1e30)
            sink = jnp.sum(jnp.where(lane16 == h, sinks, 0.0), axis=1, keepdims=True)
            m = lax.stop_gradient(jnp.maximum(jnp.max(s, axis=1, keepdims=True), sink))
            e = jnp.exp(s - m)
            den = jnp.sum(e, axis=1, keepdims=True) + jnp.exp(sink - m)
            o = _nn(e * (1.0 / den), vb)
            o_pair = o if o_pair is None else jnp.where(first, o_pair, o)
        outs.append(o_pair)
    return jnp.concatenate(outs, axis=1)


def _chip_exchange(src, out, send_sems, recv_sems, bcast):
    x, y, c = lax.axis_index("x"), lax.axis_index("y"), lax.axis_index("c")
    me = 2 * x + y

    def copy(a, d, slot):
        j = jnp.bitwise_xor(me, d)
        return pltpu.make_async_remote_copy(
            src_ref=src[a] if bcast else src[a].at[j], dst_ref=out[a].at[slot],
            send_sem=send_sems.at[a * 4 + d], recv_sem=recv_sems.at[a * 4 + d],
            device_id=(j // 2, j % 2, c), device_id_type=pl.DeviceIdType.MESH)

    pairs = [(a, d) for a in range(len(src)) for d in range(1, 4)]

    def start():
        for a, d in pairs:
            copy(a, d, me).start()

    def wait():
        for a, d in pairs:
            copy(a, d, jnp.bitwise_xor(me, d)).wait_recv()
        for a, d in pairs:
            copy(a, d, me).wait_send()

    return start, wait


def _mm(a, b, mode, out_dtype, tm, tn, tk, name, hosted=None):
    if mode == "nn":
        (m, k), (k2, n) = a.shape, b.shape
    elif mode == "nt":
        (m, k), (n, k2) = a.shape, b.shape
    else:
        (k, m), (k2, n) = a.shape, b.shape
    assert k == k2, (a.shape, b.shape, mode)
    tm, tn, tk = min(tm, m), min(tn, n), min(tk, k)
    assert m % tm == 0 and n % tn == 0 and k % tk == 0, (m, n, k, tm, tn, tk)
    nk = k // tk
    grid = (m // tm, n // tn, nk)
    ca, cb = {"nn": (1, 0), "nt": (1, 1), "tn": (0, 0)}[mode]
    h_arrays, h_bcast = hosted if hosted else ((), False)
    nh = len(h_arrays)

    def body(*refs):
        a_ref, b_ref = refs[:2]
        o_ref = refs[2 + nh]
        scratch = refs[3 + 2 * nh:]
        step = [pl.program_id(ax) for ax in range(3)]
        if nh:
            start, wait = _chip_exchange(refs[2:2 + nh], refs[3 + nh:3 + 2 * nh], scratch[-2], scratch[-1], h_bcast)

            @pl.when((step[0] == 0) & (step[1] == 0) & (step[2] == 0))
            def _():
                start()

        part = _dot(a_ref[...], b_ref[...], ca, cb)
        if nk == 1:
            o_ref[...] = part.astype(o_ref.dtype)
        else:
            acc_ref = scratch[0]

            @pl.when(step[2] == 0)
            def _():
                acc_ref[...] = part

            @pl.when(step[2] > 0)
            def _():
                acc_ref[...] += part

            @pl.when(step[2] == nk - 1)
            def _():
                o_ref[...] = acc_ref[...].astype(o_ref.dtype)

        if nh:
            @pl.when((step[0] == grid[0] - 1) & (step[1] == grid[1] - 1) & (step[2] == nk - 1))
            def _():
                wait()

    a_spec = pl.BlockSpec((tk, tm), lambda i, j, kk: (kk, i)) if mode == "tn" else pl.BlockSpec((tm, tk), lambda i, j, kk: (i, kk))
    b_spec = pl.BlockSpec((tn, tk), lambda i, j, kk: (j, kk)) if mode == "nt" else pl.BlockSpec((tk, tn), lambda i, j, kk: (kk, j))
    any_spec = pl.BlockSpec(memory_space=pl.ANY)
    res = pl.pallas_call(
        body, name=name, grid=grid,
        in_specs=[a_spec, b_spec] + [any_spec] * nh,
        out_specs=[pl.BlockSpec((tm, tn), lambda i, j, kk: (i, j))] + [any_spec] * nh,
        out_shape=[jax.ShapeDtypeStruct((m, n), out_dtype)]
        + [jax.ShapeDtypeStruct(((4,) + s.shape) if h_bcast else s.shape, s.dtype) for s in h_arrays],
        scratch_shapes=([] if nk == 1 else [pltpu.VMEM((tm, tn), F32)])
        + ([pltpu.SemaphoreType.DMA((4 * nh,)), pltpu.SemaphoreType.DMA((4 * nh,))] if nh else []),
        compiler_params=_params(3),
    )(a, b, *h_arrays)
    return (res[0], list(res[1:])) if nh else res[0]


def _rows(fn, rows, params, outs, accs, tile, name):
    length = rows[0][0].shape[0]
    tile = min(tile, length)
    assert length % tile == 0
    nr, npar, no = len(rows), len(params), len(outs)

    def body(*refs):
        vals = [r[...] for r in refs[:nr + npar]]
        o, a = fn(*vals)
        for ref, val in zip(refs[nr + npar:nr + npar + no], o):
            ref[...] = val.astype(ref.dtype)
        i = pl.program_id(0)
        for ref, val in zip(refs[nr + npar + no:], a):
            @pl.when(i == 0)
            def _(ref=ref, val=val):
                ref[...] = val

            @pl.when(i > 0)
            def _(ref=ref, val=val):
                ref[...] += val

    in_specs = [pl.BlockSpec((tile, w), functools.partial(lambda i, cb: (i, cb), cb=cb)) for _, cb, w in rows]
    in_specs += [pl.BlockSpec(p.shape, lambda i: (0, 0)) for p in params]
    outs = [o if len(o) == 3 else (o[0], o[1], o[0]) for o in outs]
    out_specs = [pl.BlockSpec((tile, w), lambda i: (i, 0)) for w, _, _ in outs]
    out_specs += [pl.BlockSpec((r, w), lambda i: (0, 0)) for r, w in accs]
    out_shape = [jax.ShapeDtypeStruct((length, full), dt) for _, dt, full in outs]
    out_shape += [jax.ShapeDtypeStruct((r, w), F32) for r, w in accs]
    res = pl.pallas_call(
        body, name=name, grid=(length // tile,), in_specs=in_specs, out_specs=out_specs, out_shape=out_shape,
        compiler_params=_params(1),
    )(*[r[0] for r in rows], *params)
    return res[:no], res[no:]


CONV_SUB = 32


def _conv_fwd(proj, conv_w, conv_b, name):
    length = proj.shape[0]
    tl, cw = min(512, length), 512
    cb0 = _OFF["xs"] // cw

    def body(u_ref, halo_ref, w_ref, b_ref, o_ref, win):
        i = pl.program_id(1)
        win[0:8, :] = jnp.where(i > 0, halo_ref[...], 0.0)
        win[8:8 + tl, :] = u_ref[...]
        taps = [w_ref[kk:kk + 1, :] for kk in range(SSD_CONV)]
        bias = b_ref[...]
        for r in range(0, tl, CONV_SUB):
            acc = bias + taps[0] * win[5 + r:5 + r + CONV_SUB, :]
            for kk in range(1, SSD_CONV):
                acc = acc + taps[kk] * win[5 + kk + r:5 + kk + r + CONV_SUB, :]
            o_ref[r:r + CONV_SUB, :] = acc

    return pl.pallas_call(
        body, name=name, grid=(CONV_DIM // cw, length // tl),
        in_specs=[pl.BlockSpec((tl, cw), lambda j, i: (i, cb0 + j)),
                  pl.BlockSpec((8, cw), lambda j, i: (jnp.maximum(i * (tl // 8) - 1, 0), cb0 + j)),
                  pl.BlockSpec((SSD_CONV, cw), lambda j, i: (0, j)),
                  pl.BlockSpec((1, cw), lambda j, i: (0, j))],
        out_specs=pl.BlockSpec((tl, cw), lambda j, i: (i, j)),
        out_shape=jax.ShapeDtypeStruct((length, CONV_DIM), F32),
        scratch_shapes=[pltpu.VMEM((8 + tl, cw), F32)],
        compiler_params=_params(2),
    )(proj, proj, conv_w, conv_b)


def _conv_bwd(dpre, proj, conv_w, dproj, part, name):
    length = proj.shape[0]
    width = dpre.shape[1]
    tl, cw = min(512, length), 512
    cb0 = _OFF["xs"] // cw + part
    n_t = length // tl

    def body(d_ref, dnext_ref, u_ref, halo_ref, w_ref, _, du_ref, dw_ref, db_ref, dwin, uwin):
        i = pl.program_id(1)
        dwin[0:tl, :] = d_ref[...]
        dwin[tl:tl + 8, :] = jnp.where(i < n_t - 1, dnext_ref[...], 0.0)
        uwin[0:8, :] = jnp.where(i > 0, halo_ref[...], 0.0)
        uwin[8:8 + tl, :] = u_ref[...]
        taps = [w_ref[kk:kk + 1, :] for kk in range(SSD_CONV)]
        sub = CONV_SUB // 2
        acc_w = [jnp.zeros((sub, cw), F32) for _ in range(SSD_CONV)]
        acc_b = jnp.zeros((sub, cw), F32)
        for r in range(0, tl, sub):
            d = dwin[r:r + sub, :]
            du = taps[3] * d
            for kk in range(SSD_CONV - 1):
                du = du + taps[kk] * dwin[3 - kk + r:3 - kk + r + sub, :]
            du_ref[r:r + sub, :] = du.astype(du_ref.dtype)
            for kk in range(SSD_CONV):
                acc_w[kk] = acc_w[kk] + d * uwin[5 + kk + r:5 + kk + r + sub, :]
            acc_b = acc_b + d
        dw = jnp.concatenate([jnp.sum(a, axis=0, keepdims=True) for a in acc_w], axis=0)
        db = jnp.sum(acc_b, axis=0, keepdims=True)

        @pl.when(i == 0)
        def _():
            dw_ref[...] = dw
            db_ref[...] = db

        @pl.when(i > 0)
        def _():
            dw_ref[...] += dw
            db_ref[...] += db

    return pl.pallas_call(
        body, name=name, grid=(width // cw, n_t), input_output_aliases={5: 0},
        in_specs=[pl.BlockSpec((tl, cw), lambda j, i: (i, j)),
                  pl.BlockSpec((8, cw), lambda j, i: (jnp.minimum((i + 1) * (tl // 8), length // 8 - 1), j)),
                  pl.BlockSpec((tl, cw), lambda j, i: (i, cb0 + j)),
                  pl.BlockSpec((8, cw), lambda j, i: (jnp.maximum(i * (tl // 8) - 1, 0), cb0 + j)),
                  pl.BlockSpec((SSD_CONV, cw), lambda j, i: (0, part + j)),
                  pl.BlockSpec(memory_space=pl.ANY)],
        out_specs=[pl.BlockSpec((tl, cw), lambda j, i: (i, cb0 + j)),
                   pl.BlockSpec((SSD_CONV, cw), lambda j, i: (0, j)),
                   pl.BlockSpec((1, cw), lambda j, i: (0, j))],
        out_shape=[jax.ShapeDtypeStruct((length, NP), BF),
                   jax.ShapeDtypeStruct((SSD_CONV, width), F32),
                   jax.ShapeDtypeStruct((1, width), F32)],
        scratch_shapes=[pltpu.VMEM((tl + 8, cw), F32), pltpu.VMEM((8 + tl, cw), F32)],
        compiler_params=_params(2),
    )(dpre, dpre, proj, proj, conv_w, dproj)


def _ssd_in_specs(rev, nc):
    def cidx(c):
        return nc - 1 - c if rev else c
    whole = lambda c: (0, 0)
    return [pl.BlockSpec((BLK, SSD_D_INNER), lambda c: (cidx(c), 0)),
            pl.BlockSpec((BLK, BC_DIM), lambda c: (cidx(c), SSD_D_INNER // BC_DIM)),
            pl.BlockSpec((BLK, BC_DIM), lambda c: (cidx(c), SSD_D_INNER // BC_DIM + 1)),
            pl.BlockSpec((BLK, SSD_D_INNER), lambda c: (cidx(c), _OFF["z"] // SSD_D_INNER)),
            pl.BlockSpec((SSD_HEADS, BLK), lambda c: (0, cidx(c))),
            pl.BlockSpec((SSD_HEADS, 1), whole),
            pl.BlockSpec((SSD_HEADS, 1), whole),
            pl.BlockSpec((1, SSD_D_INNER), whole),
            pl.BlockSpec((1, SSD_D_INNER), whole)]


def _group_args(g, px, pb, pc, z, dtr, dtb, alog, dsk, nw):
    wide, narrow, heads = slice(512 * g, 512 * (g + 1)), slice(BLK * g, BLK * (g + 1)), slice(8 * g, 8 * (g + 1))
    return (px[:, wide], pb[:, narrow], pc[:, narrow], z[:, wide], dtr[heads, :], dtb[heads, :], alog[heads, :],
            dsk[:, wide], nw[:, wide])


def _ssd_fwd(pre, proj, dt_t, dtb, alog, dsk, nw, name, hosted=None):
    length = pre.shape[0]
    nc = length // BLK
    h_arrays, h_bcast = hosted if hosted else ((), False)
    nh = len(h_arrays)

    def body(*refs):
        px, pb, pc, z, dtr, dtb_r, al_r, dsk_r, nw_r = refs[:9]
        y_ref, sin_ref = refs[9 + nh:11 + nh]
        st = refs[11 + 2 * nh]
        c = pl.program_id(0)
        if nh:
            start, wait = _chip_exchange(refs[9:9 + nh], refs[11 + nh:11 + 2 * nh], refs[-2], refs[-1], h_bcast)

        @pl.when(c == 0)
        def _():
            st[...] = jnp.zeros_like(st)
            if nh:
                start()

        for g in range(SSD_GROUPS):
            s_in = st[g]
            sin_ref[g] = s_in
            y, s_out = _ssd_group(*_group_args(g, px, pb, pc, z, dtr, dtb_r, al_r, dsk_r, nw_r), s_in)
            y_ref[:, 512 * g:512 * (g + 1)] = y.astype(y_ref.dtype)
            st[g] = s_out

        if nh:
            @pl.when(c == nc - 1)
            def _():
                wait()

    any_spec = pl.BlockSpec(memory_space=pl.ANY)
    res = pl.pallas_call(
        body, name=name, grid=(nc,), in_specs=_ssd_in_specs(False, nc) + [any_spec] * nh,
        out_specs=[pl.BlockSpec((BLK, SSD_D_INNER), lambda c: (c, 0)),
                   pl.BlockSpec((SSD_GROUPS, None, SSD_STATE, 512), lambda c: (0, c, 0, 0))] + [any_spec] * nh,
        out_shape=[jax.ShapeDtypeStruct((length, SSD_D_INNER), BF),
                   jax.ShapeDtypeStruct((SSD_GROUPS, nc, SSD_STATE, 512), F32)]
        + [jax.ShapeDtypeStruct(((4,) + s.shape) if h_bcast else s.shape, s.dtype) for s in h_arrays],
        scratch_shapes=[pltpu.VMEM((SSD_GROUPS, SSD_STATE, 512), F32)]
        + ([pltpu.SemaphoreType.DMA((4 * nh,)), pltpu.SemaphoreType.DMA((4 * nh,))] if nh else []),
        compiler_params=_params(1),
    )(pre, pre, pre, proj, dt_t, dtb, alog, dsk, nw, *h_arrays)
    return (res[0], res[1], list(res[2:])) if nh else (res[0], res[1])


def _ssd_bwd(pre, proj, dt_t, dtb, alog, dsk, nw, s_in, dy, dproj, name, hosted=None):
    length = pre.shape[0]
    nc = length // BLK
    h_arrays, h_bcast = hosted if hosted else ((), False)
    nh = len(h_arrays)

    def body(*refs):
        px, pb, pc, z, dtr, dtb_r, al_r, dsk_r, nw_r, sin_r, dy_r = refs[:11]
        dpx, dpb, dpc, dz, ddt, ddtb, dal, ddsk, dnw = refs[12 + nh:21 + nh]
        dst = refs[21 + 2 * nh]
        c = pl.program_id(0)
        if nh:
            start, wait = _chip_exchange(refs[12:12 + nh], refs[21 + nh:21 + 2 * nh], refs[-2], refs[-1], h_bcast)

        @pl.when(c == 0)
        def _():
            dst[...] = jnp.zeros_like(dst)
            if nh:
                start()

        for grp in range(SSD_GROUPS):
            wide, narrow, heads = slice(512 * grp, 512 * (grp + 1)), slice(BLK * grp, BLK * (grp + 1)), slice(8 * grp, 8 * (grp + 1))
            _, vjp = jax.vjp(_ssd_group, *_group_args(grp, px, pb, pc, z, dtr, dtb_r, al_r, dsk_r, nw_r), sin_r[grp])
            g = vjp((dy_r[:, wide], dst[grp]))
            dpx[:, wide] = g[0]
            dpb[:, narrow] = g[1]
            dpc[:, narrow] = g[2]
            dz[:, wide] = g[3].astype(dz.dtype)
            ddt[heads, :] = g[4]
            dst[grp] = g[9]
            for ref, val, idx in ((ddtb, g[5], (heads, slice(None))), (dal, g[6], (heads, slice(None))),
                                  (ddsk, g[7], (slice(None), wide)), (dnw, g[8], (slice(None), wide))):
                @pl.when(c == 0)
                def _(ref=ref, val=val, idx=idx):
                    ref[idx] = val

                @pl.when(c > 0)
                def _(ref=ref, val=val, idx=idx):
                    ref[idx] += val

        if nh:
            @pl.when(c == nc - 1)
            def _():
                wait()

    rev = lambda c: nc - 1 - c
    whole = lambda c: (0, 0)
    any_spec = pl.BlockSpec(memory_space=pl.ANY)
    in_specs = _ssd_in_specs(True, nc) + [
        pl.BlockSpec((SSD_GROUPS, None, SSD_STATE, 512), lambda c: (0, rev(c), 0, 0)),
        pl.BlockSpec((BLK, SSD_D_INNER), lambda c: (rev(c), 0)),
        any_spec] + [any_spec] * nh
    res = pl.pallas_call(
        body, name=name, grid=(nc,), in_specs=in_specs, input_output_aliases={11: 3},
        out_specs=[pl.BlockSpec((BLK, SSD_D_INNER), lambda c: (rev(c), 0)),
                   pl.BlockSpec((BLK, BC_DIM), lambda c: (rev(c), 0)),
                   pl.BlockSpec((BLK, BC_DIM), lambda c: (rev(c), 0)),
                   pl.BlockSpec((BLK, SSD_D_INNER), lambda c: (rev(c), _OFF["z"] // SSD_D_INNER)),
                   pl.BlockSpec((SSD_HEADS, BLK), lambda c: (0, rev(c))),
                   pl.BlockSpec((SSD_HEADS, 1), whole),
                   pl.BlockSpec((SSD_HEADS, 1), whole),
                   pl.BlockSpec((1, SSD_D_INNER), whole),
                   pl.BlockSpec((1, SSD_D_INNER), whole)] + [any_spec] * nh,
        out_shape=[jax.ShapeDtypeStruct((length, SSD_D_INNER), F32),
                   jax.ShapeDtypeStruct((length, BC_DIM), F32),
                   jax.ShapeDtypeStruct((length, BC_DIM), F32),
                   jax.ShapeDtypeStruct((length, NP), BF),
                   jax.ShapeDtypeStruct((SSD_HEADS, length), F32),
                   jax.ShapeDtypeStruct((SSD_HEADS, 1), F32),
                   jax.ShapeDtypeStruct((SSD_HEADS, 1), F32),
                   jax.ShapeDtypeStruct((1, SSD_D_INNER), F32),
                   jax.ShapeDtypeStruct((1, SSD_D_INNER), F32)]
        + [jax.ShapeDtypeStruct(((4,) + s.shape) if h_bcast else s.shape, s.dtype) for s in h_arrays],
        scratch_shapes=[pltpu.VMEM((SSD_GROUPS, SSD_STATE, 512), F32)]
        + ([pltpu.SemaphoreType.DMA((4 * nh,)), pltpu.SemaphoreType.DMA((4 * nh,))] if nh else []),
        compiler_params=_params(1),
    )(pre, pre, pre, proj, dt_t, dtb, alog, dsk, nw, s_in, dy, dproj, *h_arrays)
    return tuple(res[:9]) + ((list(res[9:]),) if nh else ())


assert _OFF["gates"] == 0 and _OFF["z"] % SSD_D_INNER == 0 and _OFF["q"] % D_MODEL == 0 and _OFF["k"] + 512 == NP


def _attn_in_specs():
    prev = lambda n: (0, jnp.maximum(n - 1, 0), 0)
    cur = lambda n: (0, n, 0)
    return [pl.BlockSpec((BLK, D_MODEL), lambda n: (n, _OFF["q"] // D_MODEL)),
            pl.BlockSpec((2, BLK, BLK), prev),
            pl.BlockSpec((2, BLK, BLK), cur),
            pl.BlockSpec((2, BLK, BLK), prev),
            pl.BlockSpec((2, BLK, BLK), cur),
            pl.BlockSpec((1, ATT_HEADS), lambda n: (0, 0))]


def _attn_fwd(proj, k2, v2, sinks, name, hosted=None):
    length = proj.shape[0]
    nb = length // BLK
    h_arrays, h_bcast = hosted if hosted else ((), False)
    nh = len(h_arrays)

    def body(*refs):
        q, kp, kc, vp, vc, sk = refs[:6]
        o_ref = refs[6 + nh]
        n = pl.program_id(0)
        if nh:
            start, wait = _chip_exchange(refs[6:6 + nh], refs[7 + nh:7 + 2 * nh], refs[-2], refs[-1], h_bcast)

            @pl.when(n == 0)
            def _():
                start()

        for kv in range(2):
            cols = slice(512 * kv, 512 * (kv + 1))
            o = _attn_block(q[:, cols], kp[kv], kc[kv], vp[kv], vc[kv], sk[...], n, kv)
            o_ref[:, cols] = o.astype(o_ref.dtype)

        if nh:
            @pl.when(n == nb - 1)
            def _():
                wait()

    any_spec = pl.BlockSpec(memory_space=pl.ANY)
    res = pl.pallas_call(
        body, name=name, grid=(nb,), in_specs=_attn_in_specs() + [any_spec] * nh,
        out_specs=[pl.BlockSpec((BLK, D_MODEL), lambda n: (n, 0))] + [any_spec] * nh,
        out_shape=[jax.ShapeDtypeStruct((length, D_MODEL), BF)]
        + [jax.ShapeDtypeStruct(((4,) + s.shape) if h_bcast else s.shape, s.dtype) for s in h_arrays],
        scratch_shapes=[pltpu.SemaphoreType.DMA((4 * nh,)), pltpu.SemaphoreType.DMA((4 * nh,))] if nh else [],
        compiler_params=_params(1),
    )(proj, k2, k2, v2, v2, sinks, *h_arrays)
    return (res[0], list(res[1:])) if nh else res[0]


def _attn_bwd(proj, k2, v2, sinks, datt, dproj, name):
    length = proj.shape[0]

    def body(q, kp, kc, vp, vc, sk, do, _, dq, dkp, dkc, dvp, dvc, dsk):
        n = pl.program_id(0)
        dsinks = None
        for kv in range(2):
            cols = slice(512 * kv, 512 * (kv + 1))
            _, vjp = jax.vjp(lambda *a: _attn_block(*a, n, kv), q[:, cols], kp[kv], kc[kv], vp[kv], vc[kv], sk[...])
            g = vjp(do[:, cols].astype(F32))
            dq[:, cols] = g[0].astype(dq.dtype)
            dkp[kv] = g[1]
            dkc[kv] = g[2]
            dvp[kv] = g[3]
            dvc[kv] = g[4]
            dsinks = g[5] if dsinks is None else dsinks + g[5]

        @pl.when(n == 0)
        def _():
            dsk[...] = dsinks

        @pl.when(n > 0)
        def _():
            dsk[...] += dsinks

    blk3 = pl.BlockSpec((2, BLK, BLK), lambda n: (0, n, 0))
    kv_shape = jax.ShapeDtypeStruct((2, length, BLK), F32)
    return pl.pallas_call(
        body, name=name, grid=(length // BLK,),
        in_specs=_attn_in_specs() + [pl.BlockSpec((BLK, D_MODEL), lambda n: (n, 0)), pl.BlockSpec(memory_space=pl.ANY)],
        out_specs=[pl.BlockSpec((BLK, D_MODEL), lambda n: (n, _OFF["q"] // D_MODEL)), blk3, blk3, blk3, blk3,
                   pl.BlockSpec((1, ATT_HEADS), lambda n: (0, 0))],
        out_shape=[jax.ShapeDtypeStruct((length, NP), BF), kv_shape, kv_shape, kv_shape, kv_shape,
                   jax.ShapeDtypeStruct((1, ATT_HEADS), F32)],
        input_output_aliases={7: 0},
        compiler_params=_params(1),
    )(proj, k2, k2, v2, v2, sinks, datt, dproj)


DMA_CHUNK_BYTES = 1 << 20
N_STAGE = 8
LOOKAHEAD = 3


def _piece_chunks(shape, itemsize):
    if len(shape) < 2 or shape[-2] % 16 != 0:
        return [()]
    rows, cols = shape[-2:]
    step = min(rows, max(16, DMA_CHUNK_BYTES // (cols * itemsize) // 16 * 16))
    out = []
    for lead in itertools.product(*[range(d) for d in shape[:-2]]):
        for r0 in range(0, rows, step):
            out.append(lead + (pl.ds(r0, min(step, rows - r0)),))
    return out


def _exchange(srcs, group, bcast, name):
    size = {"c": 2, "xy": 4, "all": 8}[group]
    npeer = size - 1
    n = len(srcs)
    for s in srcs:
        assert bcast or s.shape[0] == size
    pieces = [s.shape if bcast else s.shape[1:] for s in srcs]
    chunks = [_piece_chunks(p, s.dtype.itemsize) for p, s in zip(pieces, srcs)]

    def body(*refs):
        src, out = refs[:n], refs[n:2 * n]
        load_sems, send_sems, recv_sems = refs[2 * n:]
        x, y, c = lax.axis_index("x"), lax.axis_index("y"), lax.axis_index("c")
        if group == "c":
            me = c
        elif group == "xy":
            me = 2 * x + y
        else:
            me = 4 * x + 2 * y + c

        def device(j):
            if group == "c":
                return (x, y, j)
            if group == "xy":
                return (j // 2, j % 2, c)
            return (j // 4, (j // 2) % 2, j % 2)

        def part(ref, idx):
            return ref.at[idx] if idx else ref

        def piece(a, j):
            return src[a] if bcast else src[a].at[j]

        for a in range(n):
            jobs = [(None, idx) for idx in chunks[a]] if bcast else [(d, idx) for idx in chunks[a] for d in range(1, size)]
            full_rows = chunks[a][0][-1].size if chunks[a][0] else None
            slot_shape = (full_rows, pieces[a][-1]) if chunks[a][0] else pieces[a]

            def stream(stage, a=a, jobs=jobs, full_rows=full_rows):
                def slot(q):
                    idx = jobs[q][1]
                    view = stage.at[q % N_STAGE]
                    if idx and idx[-1].size != full_rows:
                        view = view.at[pl.ds(0, idx[-1].size)]
                    return view

                def load(q):
                    d, idx = jobs[q]
                    j = me if d is None else jnp.bitwise_xor(me, d)
                    return pltpu.make_async_copy(part(piece(a, j), idx), slot(q), load_sems.at[q % N_STAGE])

                def sends(q):
                    d, idx = jobs[q]
                    return [pltpu.make_async_remote_copy(
                        src_ref=slot(q), dst_ref=part(out[a].at[me], idx),
                        send_sem=send_sems.at[(q % N_STAGE) * npeer + dd - 1], recv_sem=recv_sems.at[a * size + dd],
                        device_id=device(jnp.bitwise_xor(me, dd)), device_id_type=pl.DeviceIdType.MESH)
                        for dd in (range(1, size) if d is None else (d,))]

                nq = len(jobs)
                for q in range(nq + LOOKAHEAD):
                    if q < nq:
                        if q >= N_STAGE:
                            for cp in sends(q - N_STAGE):
                                cp.wait_send()
                        load(q).start()
                    if q >= LOOKAHEAD:
                        load(q - LOOKAHEAD).wait()
                        for cp in sends(q - LOOKAHEAD):
                            cp.start()
                for q in range(max(0, nq - N_STAGE), nq):
                    for cp in sends(q):
                        cp.wait_send()

            pl.run_scoped(stream, pltpu.VMEM((N_STAGE,) + tuple(slot_shape), srcs[a].dtype))

        for a in range(n):
            for d in range(1, size):
                j = jnp.bitwise_xor(me, d)
                pltpu.make_async_remote_copy(
                    src_ref=piece(a, j), dst_ref=out[a].at[j], send_sem=send_sems.at[0], recv_sem=recv_sems.at[a * size + d],
                    device_id=device(j), device_id_type=pl.DeviceIdType.MESH).wait_recv()

    any_spec = pl.BlockSpec(memory_space=pl.ANY)
    return pl.pallas_call(
        body, name=name, in_specs=[any_spec] * n, out_specs=[any_spec] * n,
        out_shape=[jax.ShapeDtypeStruct((size,) + tuple(p), s.dtype) for p, s in zip(pieces, srcs)],
        scratch_shapes=[pltpu.SemaphoreType.DMA((N_STAGE,)), pltpu.SemaphoreType.DMA((N_STAGE * npeer,)),
                        pltpu.SemaphoreType.DMA((n * size,))],
        compiler_params=pltpu.CompilerParams(vmem_limit_bytes=VMEM_LIMIT_BYTES),
    )(*srcs)


def _with_own(outs, owns, me):
    return [lax.dynamic_update_index_in_dim(o, w, me, 0) for o, w in zip(outs, owns)]


def _sum_slots(arr, name):
    k = arr.shape[0]
    rest = arr.shape[1:]
    width = rest[-1]
    rows_per = math.prod(rest[:-1])
    a2 = arr.reshape(k * rows_per, width)
    tile = rows_per
    for cand in (256, 128, 64, 32, 16, 8):
        if rows_per % cand == 0:
            tile = cand
            break
    nt = rows_per // tile

    def body(*refs):
        acc = refs[0][...]
        for r in refs[1:k]:
            acc = acc + r[...]
        refs[k][...] = acc

    in_specs = [pl.BlockSpec((tile, width), functools.partial(lambda i, s: (s * nt + i, 0), s=s)) for s in range(k)]
    out = pl.pallas_call(
        body, name=name, grid=(nt,), in_specs=in_specs, out_specs=pl.BlockSpec((tile, width), lambda i: (i, 0)),
        out_shape=jax.ShapeDtypeStruct((rows_per, width), arr.dtype), compiler_params=_params(1),
    )(*([a2] * k))
    return out.reshape(rest)


def _sum_pieces(recv, own, own_slotted, me, name):
    k = recv.shape[0]
    rest = recv.shape[1:]
    width = rest[-1]
    rows_per = math.prod(rest[:-1])
    r2 = recv.reshape(k * rows_per, width)
    o2 = own.reshape(-1, width)
    tile = rows_per
    for cand in (256, 128, 64, 32, 16, 8):
        if rows_per % cand == 0:
            tile = cand
            break
    nt = rows_per // tile

    def body(me_ref, *refs):
        mine = refs[k][...].astype(F32)
        acc = None
        for j in range(k):
            term = jnp.where(me_ref[0] == j, mine, refs[j][...].astype(F32))
            acc = term if acc is None else acc + term
        refs[k + 1][...] = acc
        refs[k + 2][...] = acc.astype(BF)

    in_specs = [pl.BlockSpec((tile, width), functools.partial(lambda i, m, s: (s * nt + i, 0), s=s)) for s in range(k)]
    in_specs.append(pl.BlockSpec((tile, width), (lambda i, m: (m[0] * nt + i, 0)) if own_slotted else (lambda i, m: (i, 0))))
    out_spec = pl.BlockSpec((tile, width), lambda i, m: (i, 0))
    out, out_bf = pl.pallas_call(
        body, name=name,
        grid_spec=pltpu.PrefetchScalarGridSpec(num_scalar_prefetch=1, grid=(nt,), in_specs=in_specs, out_specs=[out_spec, out_spec]),
        out_shape=[jax.ShapeDtypeStruct((rows_per, width), F32), jax.ShapeDtypeStruct((rows_per, width), BF)],
        compiler_params=_params(1),
    )(jnp.reshape(me, (1,)).astype(jnp.int32), *([r2] * k), o2)
    return out.reshape(rest), out_bf.reshape(rest)


def _adamw(w, g, m, v, name):
    def fn(w_, g_, m_, v_):
        m1 = ADAM_B1 * m_ + (1.0 - ADAM_B1) * g_
        v1 = ADAM_B2 * v_ + (1.0 - ADAM_B2) * (g_ * g_)
        m_hat = m1 / (1.0 - ADAM_B1 ** ADAM_STEP)
        v_hat = v1 / (1.0 - ADAM_B2 ** ADAM_STEP)
        delta = -ADAM_LR * (m_hat / (jnp.sqrt(v_hat) + ADAM_EPS) + ADAM_WD * w_)
        return (delta, m1, v1), ()

    rows, width = w.shape
    tile = rows
    for cand in (256, 128, 64, 32, 16, 8):
        if rows % cand == 0:
            tile = cand
            break
    o, _ = _rows(fn, [(w, 0, width), (g, 0, width), (m, 0, width), (v, 0, width)], [],
                 [(width, F32)] * 3, [], tile, name)
    return o


def _col_pieces(start, width, shard):
    out = []
    while width:
        chip, off = divmod(start, shard)
        take = min(width, shard - off)
        out.append((chip, off, take))
        start, width = start + take, width - take
    return out


def _gather_cols(parts, ranges, pad=0):
    shard = parts.shape[3]
    halves = []
    for h in range(2):
        cols = [parts[h, chip, :, off:off + w] for s, wd in ranges for chip, off, w in _col_pieces(s, wd, shard)]
        if pad:
            cols.append(jnp.zeros((parts.shape[2], pad), parts.dtype))
        halves.append(jnp.concatenate(cols, axis=1))
    return jnp.concatenate(halves, axis=0)


def _scatter_cols(g, ranges, shard):
    starts, pos = {}, 0
    for s, wd in ranges:
        starts[s] = (pos, wd)
        pos += wd
    order = sorted(starts)
    hr = g.shape[0] // 2
    out = []
    for h in range(2):
        per_chip = []
        for chip in range(4):
            cols = []
            for s in order:
                p0, wd = starts[s]
                lo, hi = max(s, chip * shard), min(s + wd, (chip + 1) * shard)
                if lo < hi:
                    cols.append(g[h * hr:(h + 1) * hr, p0 + lo - s:p0 + hi - s])
            per_chip.append(jnp.concatenate(cols, axis=1))
        out.append(jnp.stack(per_chip))
    return jnp.stack(out)


_IN_RANGES = tuple((o, wd) for _, o, wd in _PACK)


def _double_heads(t):
    length = t.shape[0]
    h = jnp.transpose(t.reshape(length, 2, 64), (1, 0, 2))
    return jnp.concatenate([h, h], axis=-1)


def _fold_heads(d_cur, d_prev):
    length = d_cur.shape[1]
    d = d_cur + jnp.concatenate([d_prev[:, BLK:], jnp.zeros((2, BLK, BLK), F32)], axis=1)
    d = d[..., :64] + d[..., 64:]
    return jnp.transpose(d, (1, 0, 2)).reshape(length, 128)


def _ln_fwd(a, r, g, b, name):
    def fn(a_, r_, g_, b_):
        y = _layer_norm(ALPHA * a_ + r_, g_, b_)
        return (y, y), ()
    o, _ = _rows(fn, [(a, 0, D_MODEL), (r, 0, D_MODEL)], [g, b], [(D_MODEL, F32), (D_MODEL, BF)], [], 256, name)
    return o


def _ln_bwd(a, r, d1, d2, g, b, name):
    def fn(a_, r_, d1_, d2_, g_, b_):
        _, vjp = jax.vjp(_layer_norm, ALPHA * a_ + r_, g_, b_)
        ds, dg, db = vjp(ALPHA * d1_ + d2_)
        return (ds, ds), (dg, db)
    o, acc = _rows(fn, [(a, 0, D_MODEL), (r, 0, D_MODEL), (d1, 0, D_MODEL), (d2, 0, D_MODEL)], [g, b],
                   [(D_MODEL, F32), (D_MODEL, BF)], [(1, D_MODEL), (1, D_MODEL)], 256, name)
    return o[0], o[1], acc[0], acc[1]


def _layer_fwd(l, h, hb, w, p, hosted=None, hosted_attn=None):
    sv = {"h": h, "hb": hb}
    proj = _mm(hb, w["win"], "nn", F32, 2048, 512, 1024, f"proj{l}")
    sv["proj"] = proj
    pre = _conv_fwd(proj, p["conv_w"], p["conv_b"], f"conv{l}")
    sv["pre"] = pre
    dt_t = jnp.transpose(proj[:, _OFF["dt"]:_OFF["dt"] + SSD_HEADS])
    sv["dt_t"] = dt_t
    res = _ssd_fwd(pre, proj, dt_t, p["dt_bias"], p["a_log"], p["d_skip_c"], p["ssd_norm_w"], f"ssd{l}", hosted)
    yn, s_in = res[:2]
    if hosted:
        sv["hosted"] = res[2]
    sv["yn"], sv["s_in"] = yn, s_in
    ya = _mm(yn, w["wso"], "nn", F32, 1024, 1024, 2048, f"ssdout{l}")
    k2 = _double_heads(proj[:, _OFF["k"]:_OFF["k"] + 128])
    v2 = _double_heads(proj[:, _OFF["v"]:_OFF["v"] + 128])
    sv["k2"], sv["v2"] = k2, v2
    att = _attn_fwd(proj, k2, v2, p["att_sinks"], f"attn{l}", hosted_attn)
    if hosted_attn:
        att, sv["hosted_attn"] = att
    sv["att"] = att
    yb = _mm(att, w["wao"], "nn", F32, 1024, 1024, 1024, f"attout{l}")
    sv["ya"], sv["yb"] = ya, yb
    gcb = _OFF["gates"] // 1024

    def gate_fn(ga, gb, ya_, yb_):
        return (_sigmoid(ga) * ya_ + _sigmoid(gb) * yb_,), ()
    (ub,), _ = _rows(gate_fn, [(proj, gcb, 1024), (proj, gcb + 1, 1024), (ya, 0, 1024), (yb, 0, 1024)], [],
                     [(D_MODEL, BF)], [], 256, f"gate{l}")
    sv["ub"] = ub
    mix = _mm(ub, w["wmo"], "nn", F32, 1024, 1024, 1024, f"mixout{l}")
    sv["mix"] = mix
    h1, h1b = _ln_fwd(h, mix, p["ln_mix_g"], p["ln_mix_b"], f"lnmix{l}")
    sv["h1"], sv["h1b"] = h1, h1b
    gu = _mm(h1b, w["wgu"], "nn", F32, 2048, 512, 1024, f"ffnin{l}")
    sv["gu"] = gu

    def act_fn(g_, u_):
        return (_silu(g_) * u_,), ()
    (act,), _ = _rows(act_fn, [(gu, 0, FFN_HIDDEN), (gu, 1, FFN_HIDDEN)], [], [(FFN_HIDDEN, BF)], [], 256, f"swiglu{l}")
    sv["act"] = act
    ffn = _mm(act, w["wd"], "nn", F32, 1024, 1024, FFN_HIDDEN, f"ffnout{l}")
    sv["ffn"] = ffn
    return sv


def _layer_bwd(l, sv, w, p, ds_f, ds_b, hosted=None, late_host=None):
    g = {}
    g["w_ffn_down"] = _mm(sv["act"], ds_b, "tn", F32, 1408, 1024, 1024, f"dwd{l}")
    dact = _mm(ds_b, w["wd"], "nt", BF, 1024, FFN_HIDDEN, 1024, f"dact{l}")

    def act_bwd(g_, u_, d_):
        _, vjp = jax.vjp(lambda a, b: _silu(a) * b, g_, u_)
        dg_, du_ = vjp(d_.astype(F32))
        return (jnp.concatenate([dg_, du_], axis=1),), ()
    (dgu,), _ = _rows(act_bwd, [(sv["gu"], 0, FFN_HIDDEN), (sv["gu"], 1, FFN_HIDDEN), (dact, 0, FFN_HIDDEN)], [],
                      [(2 * FFN_HIDDEN, BF)], [], 128, f"dswiglu{l}")
    dwgu = _mm(sv["h1b"], dgu, "tn", F32, 1024, 2816, 512, f"dwgu{l}")
    g["w_ffn_gate"], g["w_ffn_up"] = dwgu[:, :FFN_HIDDEN], dwgu[:, FFN_HIDDEN:]
    dh1 = _mm(dgu, w["wgu"], "nt", F32, 1024, 1024, 2816, f"dh1{l}")
    ds2_f, ds2_b, g["ln_mix_g"], g["ln_mix_b"] = _ln_bwd(sv["h"], sv["mix"], ds_f, dh1, p["ln_mix_g"], p["ln_mix_b"], f"dlnmix{l}")
    g["w_mix_out"] = _mm(sv["ub"], ds2_b, "tn", F32, 1024, 1024, 512, f"dwmo{l}")
    du = _mm(ds2_b, w["wmo"], "nt", BF, 1024, 1024, 1024, f"du{l}")
    proj = sv["proj"]
    gcb = _OFF["gates"] // 1024

    def gate_bwd(ga, gb, ya_, yb_, du_):
        _, vjp = jax.vjp(lambda a, b, c, d: _sigmoid(a) * c + _sigmoid(b) * d, ga, gb, ya_, yb_)
        dga, dgb, dya, dyb = vjp(du_.astype(F32))
        return (jnp.concatenate([dga, dgb], axis=1), dya, dyb), ()
    (dproj, dya, dyb), _ = _rows(
        gate_bwd, [(proj, gcb, 1024), (proj, gcb + 1, 1024), (sv["ya"], 0, 1024), (sv["yb"], 0, 1024), (du, 0, 1024)], [],
        [(2048, BF, NP), (D_MODEL, BF), (D_MODEL, BF)], [], 256, f"dgate{l}")
    g["w_att_out"] = _mm(sv["att"], dyb, "tn", F32, 1024, 1024, 512, f"dwao{l}")
    datt = _mm(dyb, w["wao"], "nt", BF, 1024, 1024, 1024, f"datt{l}")
    g["w_ssd_out"] = _mm(sv["yn"], dya, "tn", F32, 1024, 1024, 512, f"dwso{l}")
    dyn = _mm(dya, w["wso"], "nt", F32, 1024, 1024, 1024, f"dyn{l}")
    late = late_host(g) if late_host else None
    dproj, dkp, dkc, dvp, dvc, g["att_sinks"] = _attn_bwd(proj, sv["k2"], sv["v2"], p["att_sinks"], datt, dproj, f"dattn{l}")
    res = _ssd_bwd(sv["pre"], proj, sv["dt_t"], p["dt_bias"], p["a_log"], p["d_skip_c"], p["ssd_norm_w"], sv["s_in"], dyn,
                   dproj, f"dssd{l}", hosted)
    (dpx, dpb, dpc, dproj, ddt_t, g["dt_bias"], g["a_log"], ddsk, g["ssd_norm_w"]) = res[:9]
    if hosted:
        g["hosted"] = res[9]
    g["d_skip"] = jnp.sum(ddsk.reshape(SSD_HEADS, 64), axis=1)
    dws, dbs = [], []
    for nm, dpart, part in (("x", dpx, 0), ("b", dpb, 4), ("c", dpc, 5)):
        dproj, dw_, db_ = _conv_bwd(dpart, proj, p["conv_w"], dproj, part, f"dconv{nm}{l}")
        dws.append(dw_)
        dbs.append(db_)
    g["conv_w"], g["conv_b"] = jnp.concatenate(dws, axis=1), jnp.concatenate(dbs, axis=1)
    length = proj.shape[0]
    tail = jnp.concatenate([_fold_heads(dkc, dkp).astype(BF), _fold_heads(dvc, dvp).astype(BF),
                            jnp.transpose(ddt_t).astype(BF), jnp.zeros((length, _PAD), BF)], axis=1)
    dproj = lax.dynamic_update_slice(dproj, tail, (0, _OFF["k"]))
    g["w_in"] = _mm(sv["hb"], dproj, "tn", F32, 1024, 2176, 1024, f"dwin{l}")
    dh = _mm(dproj, w["win"], "nt", F32, 1024, 1024, 2176, f"dh{l}", late)
    if late:
        dh, g["late_hosted"] = dh
    return g, ds2_f, dh


def kernel(x, ln_in_g, ln_in_b, w_in, conv_w, conv_b, dt_bias, a_log, d_skip, ssd_norm_w, att_sinks, w_ssd_out, w_att_out, w_mix_out, ln_mix_g, ln_mix_b, w_ffn_gate, w_ffn_up, w_ffn_down, ln_ffn_g, ln_ffn_b, loss_target, m_ln_in_g, m_ln_in_b, m_w_in, m_conv_w, m_conv_b, m_dt_bias, m_a_log, m_d_skip, m_ssd_norm_w, m_att_sinks, m_w_ssd_out, m_w_att_out, m_w_mix_out, m_ln_mix_g, m_ln_mix_b, m_w_ffn_gate, m_w_ffn_up, m_w_ffn_down, m_ln_ffn_g, m_ln_ffn_b, v_ln_in_g, v_ln_in_b, v_w_in, v_conv_w, v_conv_b, v_dt_bias, v_a_log, v_d_skip, v_ssd_norm_w, v_att_sinks, v_w_ssd_out, v_w_att_out, v_w_mix_out, v_ln_mix_g, v_ln_mix_b, v_w_ffn_gate, v_w_ffn_up, v_w_ffn_down, v_ln_ffn_g, v_ln_ffn_b):
    env = dict(locals())
    wts = {n: env[n] for n in WEIGHTS}
    mom1 = {n: env["m_" + n] for n in WEIGHTS}
    mom2 = {n: env["v_" + n] for n in WEIGHTS}
    ci = lax.axis_index("c")
    chip = 2 * lax.axis_index("x") + lax.axis_index("y")
    xs_ = x[0]
    tgt = loss_target[0]

    COLS = ("w_in", "w_ffn_gate", "w_ffn_up")

    def half(t):
        return lax.dynamic_slice_in_dim(t, ci * (t.shape[0] // 2), t.shape[0] // 2, axis=0)

    def assemble(name, parts):
        if name == "w_in":
            return _gather_cols(parts, _IN_RANGES, _PAD)
        if name in COLS:
            return _gather_cols(parts, ((0, 4 * parts.shape[3]),))
        return jnp.concatenate([parts[h, s] for s in range(4) for h in range(2)], axis=0)

    def split(name, gfull):
        if name == "w_in":
            return _scatter_cols(gfull, _IN_RANGES, IN_DIM // 4)
        if name in COLS:
            return _scatter_cols(gfull, ((0, gfull.shape[1]),), gfull.shape[1] // 4)
        hr = gfull.shape[0] // 8
        return jnp.stack([jnp.stack([gfull[(2 * s + h) * hr:(2 * s + h + 1) * hr] for s in range(4)]) for h in range(2)])

    mine = [[half(wts[n][l].astype(BF)) for n in BIG] for l in range(DEPTH)]
    conv_all, = _with_own(_exchange([conv_w], "xy", True, "gather_conv"), [conv_w], chip)
    conv_full = jnp.transpose(conv_all, (1, 2, 0, 3)).reshape(DEPTH, SSD_CONV, CONV_DIM)

    def layer_weights(l, by_chip):
        by_chip = _with_own(by_chip, mine[l], chip)
        parts = dict(zip(BIG, _with_own(_exchange(by_chip, "c", True, f"gather_cores{l}"), by_chip, ci)))
        fw = {n: assemble(n, parts[n]) for n in BIG}
        return {"win": fw["w_in"], "wso": fw["w_ssd_out"], "wao": fw["w_att_out"], "wmo": fw["w_mix_out"],
                "wgu": jnp.concatenate([fw["w_ffn_gate"], fw["w_ffn_up"]], axis=1), "wd": fw["w_ffn_down"]}

    layer_p = []
    for l in range(DEPTH):
        layer_p.append({
            "conv_w": conv_full[l], "conv_b": conv_b[l][None],
            "dt_bias": dt_bias[l][:, None], "a_log": a_log[l][:, None],
            "d_skip_c": jnp.repeat(d_skip[l], 64)[None], "ssd_norm_w": ssd_norm_w[l][None],
            "att_sinks": att_sinks[l][None], "ln_mix_g": ln_mix_g[l][None], "ln_mix_b": ln_mix_b[l][None],
            "ln_ffn_g": ln_ffn_g[l][None], "ln_ffn_b": ln_ffn_b[l][None],
        })

    def ln_in_fn(x_, g_, b_):
        y = _layer_norm(x_, g_, b_)
        return (y, y), ()
    (h, hb), _ = _rows(ln_in_fn, [(xs_, 0, D_MODEL)], [ln_in_g[None], ln_in_b[None]], [(D_MODEL, F32), (D_MODEL, BF)], [], 256, "ln_in")
    layer_w = [layer_weights(0, _exchange(mine[0], "xy", True, "gather_chips0"))]
    behind_ssd = [i for i, n in enumerate(BIG) if n in COLS]
    behind_attn = [i for i, n in enumerate(BIG) if n not in COLS]
    saved = []
    for l in range(DEPTH):
        nxt = l + 1 < DEPTH
        sv = _layer_fwd(l, h, hb, layer_w[l], layer_p[l], ([mine[l + 1][i] for i in behind_ssd], True) if nxt else None,
                        ([mine[l + 1][i] for i in behind_attn], True) if nxt else None)
        saved.append(sv)
        if nxt:
            got = dict(zip(behind_ssd + behind_attn, sv["hosted"] + sv["hosted_attn"]))
            layer_w.append(layer_weights(l + 1, [got[i] for i in range(len(BIG))]))
            h, hb = _ln_fwd(sv["h1"], sv["ffn"], layer_p[l]["ln_ffn_g"], layer_p[l]["ln_ffn_b"], f"lnffn{l}")

    def loss_fn(a_, r_, t_, g_, b_):
        y, vjp = jax.vjp(_layer_norm, ALPHA * a_ + r_, g_, b_)
        err = y - t_
        ds, dg, db = vjp(err * (1.0 / D_MODEL))
        part = 0.5 * jnp.sum(jnp.mean(err * err, axis=-1, keepdims=True), axis=0, keepdims=True)
        return (ds, ds), (dg, db, jnp.broadcast_to(part, (1, BLK)))
    sv = saved[-1]
    (ds_f, ds_b), (dg_last, db_last, loss_part) = _rows(
        loss_fn, [(sv["h1"], 0, D_MODEL), (sv["ffn"], 0, D_MODEL), (tgt, 0, D_MODEL)],
        [layer_p[-1]["ln_ffn_g"], layer_p[-1]["ln_ffn_b"]], [(D_MODEL, F32), (D_MODEL, BF)],
        [(1, D_MODEL), (1, D_MODEL), (1, BLK)], 256, "loss")
    loss = lax.psum(loss_part[0, 0], ("x", "y", "c"))

    def reduce_cores(l, g, names):
        src = [split(n, g[n]) for n in names]
        recv = _exchange(src, "c", False, f"reduce_cores{l}_{names[0]}")
        return {n: _sum_pieces(r, o, True, ci, f"sum_cores{l}_{n}") for n, r, o in zip(names, recv, src)}

    def bf16_of(sums, names):
        return [sums[n][1] for n in names]

    def finish_reduce(l, sums, from_chips):
        reduced = [_sum_pieces(from_chips[n], sums[n][0], True, chip, f"sum_chips{l}_{n}")[0] for n in BIG]
        halves = _with_own(_exchange(reduced, "c", True, f"swap_halves{l}"), reduced, ci)
        return {n: t.reshape(2 * t.shape[1], t.shape[2]) for n, t in zip(BIG, halves)}

    EARLY = tuple(n for n in BIG if n != "w_in")
    grads = [None] * DEPTH
    chip_sums = [None] * DEPTH
    for l in reversed(range(DEPTH)):
        sv = saved[l]
        if l == DEPTH - 1:
            lg, lb = dg_last, db_last
        else:
            ds_f, ds_b, lg, lb = _ln_bwd(sv["h1"], sv["ffn"], d1, d2, layer_p[l]["ln_ffn_g"], layer_p[l]["ln_ffn_b"], f"dlnffn{l}")
        hosted = (bf16_of(chip_sums[l + 1], BIG), False) if l < DEPTH - 1 else None
        early = {}

        def late_host(g, l=l, early=early):
            early.update(reduce_cores(l, g, EARLY))
            return bf16_of(early, EARLY), False
        g, d1, d2 = _layer_bwd(l, sv, layer_w[l], layer_p[l], ds_f, ds_b, hosted, late_host if l == 0 else None)
        g["ln_ffn_g"], g["ln_ffn_b"] = lg, lb
        grads[l] = g
        chip_sums[l] = {**early, **reduce_cores(l, g, ("w_in",) if l == 0 else BIG)}

    def ln_in_bwd(x_, d1_, d2_, g_, b_):
        _, vjp = jax.vjp(_layer_norm, x_, g_, b_)
        dx, dg, db = vjp(ALPHA * d1_ + d2_)
        return (dx,), (dg, db)
    (grad_x,), (g_ln_in_g, g_ln_in_b) = _rows(
        ln_in_bwd, [(xs_, 0, D_MODEL), (d1, 0, D_MODEL), (d2, 0, D_MODEL)], [ln_in_g[None], ln_in_b[None]],
        [(D_MODEL, F32)], [(1, D_MODEL), (1, D_MODEL)], 256, "dln_in")

    shard_grads = [None] * DEPTH
    for l in range(1, DEPTH):
        shard_grads[l] = finish_reduce(l, chip_sums[l], dict(zip(BIG, grads[l - 1]["hosted"])))
    from_chips = dict(zip(EARLY, grads[0]["late_hosted"]))
    from_chips["w_in"], = _exchange(bf16_of(chip_sums[0], ("w_in",)), "xy", False, "reduce_chips0")
    shard_grads[0] = finish_reduce(0, chip_sums[0], from_chips)
    big_grad = {n: jnp.stack([shard_grads[l][n] for l in range(DEPTH)]).reshape(wts[n].shape) for n in BIG}

    small_g = {"ln_in_g": g_ln_in_g[0], "ln_in_b": g_ln_in_b[0]}
    for n in SMALL[2:]:
        small_g[n] = jnp.stack([grads[l][n].reshape(wts[n].shape[1:] if n != "conv_w" else (SSD_CONV, CONV_DIM)) for l in range(DEPTH)])
    flat = jnp.concatenate([small_g[n].reshape(-1) for n in SMALL])
    n_small = flat.shape[0]
    width = -(-n_small // 1024) * 1024
    flat = jnp.pad(flat, (0, width - n_small)).reshape(8, width // 8)
    gathered, = _with_own(_exchange([flat], "all", True, "gather_small"), [flat], 2 * chip + ci)
    total = _sum_slots(gathered, "sum_small").reshape(-1)
    small_grad, off = {}, 0
    for n in SMALL:
        shp = small_g[n].shape
        cnt = math.prod(shp)
        small_grad[n] = total[off:off + cnt].reshape(shp)
        off += cnt
    small_grad["conv_w"] = lax.dynamic_slice_in_dim(small_grad["conv_w"], chip * (CONV_DIM // 4), CONV_DIM // 4, axis=2)

    out_g, out_d, out_m, out_v = {}, {}, {}, {}
    for n in BIG:
        shp = wts[n].shape
        two_d = lambda t: t.reshape(shp[0] * shp[1], shp[2])
        d_, m_, v_ = _adamw(two_d(wts[n]), two_d(big_grad[n]), two_d(mom1[n]), two_d(mom2[n]), f"adamw_{n}")
        out_g[n], out_d[n], out_m[n], out_v[n] = big_grad[n], d_.reshape(shp), m_.reshape(shp), v_.reshape(shp)

    def flat_small(d):
        f = jnp.concatenate([d[n].reshape(-1) for n in SMALL])
        return jnp.pad(f, (0, swidth - f.shape[0])).reshape(8, swidth // 8)
    n_sw = sum(math.prod(wts[n].shape) for n in SMALL)
    swidth = -(-n_sw // 1024) * 1024
    d_, m_, v_ = _adamw(flat_small(wts), flat_small(small_grad), flat_small(mom1), flat_small(mom2), "adamw_small")
    off = 0
    for n in SMALL:
        shp = wts[n].shape
        cnt = math.prod(shp)
        out_g[n] = small_grad[n]
        out_d[n] = d_.reshape(-1)[off:off + cnt].reshape(shp)
        out_m[n] = m_.reshape(-1)[off:off + cnt].reshape(shp)
        out_v[n] = v_.reshape(-1)[off:off + cnt].reshape(shp)
        off += cnt

    return (loss, grad_x[None], *[out_g[n] for n in WEIGHTS], *[out_d[n] for n in WEIGHTS],
            *[out_m[n] for n in WEIGHTS], *[out_v[n] for n in WEIGHTS])
```

```python
import functools
import itertools
import math

import jax
import jax.numpy as jnp
from jax import lax
from jax.experimental import pallas as pl
from jax.experimental.pallas import tpu as pltpu

F32 = jnp.float32
BF = jnp.bfloat16

D_MODEL = 1024
DEPTH = 2
ATT_HEADS = 16
ATT_HEAD_DIM = 64
BLK = 128
SSD_D_INNER = 2048
SSD_HEADS = 32
SSD_GROUPS = 4
SSD_STATE = 128
SSD_CONV = 4
BC_DIM = 512
CONV_DIM = 3072
FFN_HIDDEN = 2816
IN_DIM = 8480
LN_EPS = 1e-5
RMS_EPS = 1e-5
ALPHA = (2 * DEPTH) ** 0.25
ADAM_LR = 0.001
ADAM_B1 = 0.9
ADAM_B2 = 0.999
ADAM_EPS = 1e-08
ADAM_WD = 0.01
ADAM_STEP = 10

_PACK = (("gates", 6432, 2048), ("z", 1280, 2048), ("q", 0, 1024), ("xs", 3328, 2048), ("B", 5376, 512),
         ("C", 5888, 512), ("k", 1024, 128), ("v", 1152, 128), ("dt", 6400, 32))
NP = 8704
_OFF = {}
_o = 0
for _n, _, _w in _PACK:
    _OFF[_n] = _o
    _o += _w
_PAD = NP - _o

VMEM_LIMIT_BYTES = 56 * 1024 * 1024
BIG = ("w_in", "w_ssd_out", "w_att_out", "w_mix_out", "w_ffn_gate", "w_ffn_up", "w_ffn_down")
WEIGHTS = ("ln_in_g", "ln_in_b", "w_in", "conv_w", "conv_b", "dt_bias", "a_log", "d_skip", "ssd_norm_w", "att_sinks",
           "w_ssd_out", "w_att_out", "w_mix_out", "ln_mix_g", "ln_mix_b", "w_ffn_gate", "w_ffn_up", "w_ffn_down",
           "ln_ffn_g", "ln_ffn_b")
SMALL = tuple(n for n in WEIGHTS if n not in BIG)


def _params(n_grid):
    return pltpu.CompilerParams(dimension_semantics=("arbitrary",) * n_grid, vmem_limit_bytes=VMEM_LIMIT_BYTES)


def _dot(a, b, ca, cb):
    return lax.dot_general(a.astype(BF), b.astype(BF), (((ca,), (cb,)), ((), ())), preferred_element_type=F32)


@jax.custom_vjp
def _nn(a, b):
    return _dot(a, b, 1, 0)


def _nn_f(a, b):
    return _dot(a, b, 1, 0), (a, b)


def _nn_b(res, g):
    a, b = res
    return _dot(g, b, 1, 1).astype(a.dtype), _dot(a, g, 0, 0).astype(b.dtype)


_nn.defvjp(_nn_f, _nn_b)


@jax.custom_vjp
def _nt(a, b):
    return _dot(a, b, 1, 1)


def _nt_f(a, b):
    return _dot(a, b, 1, 1), (a, b)


def _nt_b(res, g):
    a, b = res
    return _dot(g, b, 1, 0).astype(a.dtype), _dot(g, a, 0, 0).astype(b.dtype)


_nt.defvjp(_nt_f, _nt_b)


@jax.custom_vjp
def _tn(a, b):
    return _dot(a, b, 0, 0)


def _tn_f(a, b):
    return _dot(a, b, 0, 0), (a, b)


def _tn_b(res, g):
    a, b = res
    return _dot(b, g, 1, 1).astype(a.dtype), _dot(a, g, 1, 0).astype(b.dtype)


_tn.defvjp(_tn_f, _tn_b)


def _sigmoid(x):
    return 0.5 * jnp.tanh(0.5 * x) + 0.5


def _silu(x):
    return x * _sigmoid(x)


def _layer_norm(s, g, b):
    mu = jnp.mean(s, axis=-1, keepdims=True)
    sc = s - mu
    var = jnp.mean(sc * sc, axis=-1, keepdims=True)
    return sc * lax.rsqrt(var + LN_EPS) * g + b


def _ssd_group(pre_x, pre_b, pre_c, z, dtr, dtb, alog, dsk, nw, state):
    t = pre_x.shape[0]
    xs, bm, cm = _silu(pre_x), _silu(pre_b), _silu(pre_c)
    dt = jax.nn.softplus(dtr + dtb)
    da = dt * (-jnp.exp(alog))
    row = lax.broadcasted_iota(jnp.int32, (t, t), 0)
    col = lax.broadcasted_iota(jnp.int32, (t, t), 1)
    upper = (row <= col).astype(F32)
    cum = jnp.dot(da, upper, precision=lax.Precision.HIGHEST, preferred_element_type=F32)
    tot = jnp.sum(da, axis=1, keepdims=True)
    cb = _nt(cm, bm)
    tril = row >= col
    first = lax.broadcasted_iota(jnp.int32, (t, BLK), 1) < 64
    first_row = lax.broadcasted_iota(jnp.int32, (1, BLK), 1) < 64

    def col_form(r):
        return jnp.broadcast_to(r, (t, t)).T

    ys, new_state = [], []
    for p in range(4):
        h0, h1 = 2 * p, 2 * p + 1
        sl = slice(BLK * p, BLK * (p + 1))
        x_pair = xs[:, sl] * jnp.where(first, col_form(dt[h0:h0 + 1]), col_form(dt[h1:h1 + 1]))
        y_pair = None
        cc = []
        for h, keep in ((h0, first), (h1, jnp.logical_not(first))):
            cr = jnp.broadcast_to(cum[h:h + 1], (t, t))
            cc.append(cr.T)
            decay = jnp.exp(jnp.where(tril, cc[-1] - cr, -1e30))
            y_h = _nn(decay * cb, jnp.where(keep, x_pair, 0.0))
            y_pair = y_h if y_pair is None else y_pair + y_h
        s_pair = state[:, sl]
        y_pair = y_pair + _nn(cm, s_pair) * jnp.where(first, jnp.exp(cc[0]), jnp.exp(cc[1]))
        to_end = jnp.where(first, jnp.exp(tot[h0:h0 + 1] - cc[0]), jnp.exp(tot[h1:h1 + 1] - cc[1]))
        chunk_decay = jnp.where(first_row, jnp.exp(tot[h0:h0 + 1]), jnp.exp(tot[h1:h1 + 1]))
        new_state.append(s_pair * chunk_decay + _tn(bm, x_pair * to_end))
        ys.append(y_pair + dsk[:, sl] * xs[:, sl])
    y = jnp.concatenate(ys, axis=1) * _silu(z)
    y = y * lax.rsqrt(jnp.mean(y * y, axis=-1, keepdims=True) + RMS_EPS) * nw
    return y, jnp.concatenate(new_state, axis=1)


def _attn_block(q, kp, kc, vp, vc, sinks, n, kv):
    t = q.shape[0]
    kb = jnp.concatenate([kp, kc], axis=0)
    vb = jnp.concatenate([vp, vc], axis=0)
    qi = lax.broadcasted_iota(jnp.int32, (t, 2 * t), 0)
    kj = lax.broadcasted_iota(jnp.int32, (t, 2 * t), 1)
    rel = qi + t - kj
    valid = (rel >= 0) & (rel < t) & ((n * t - t + kj) >= 0)
    relf = rel.astype(F32)
    first = lax.broadcasted_iota(jnp.int32, (t, BLK), 1) < 64
    lane16 = lax.broadcasted_iota(jnp.int32, (1, ATT_HEADS), 1)
    outs = []
    for p in range(4):
        qp = q[:, BLK * p:BLK * (p + 1)] * (ATT_HEAD_DIM ** -0.5)
        o_pair = None
        for half, keep in enumerate((first, jnp.logical_not(first))):
            h = kv * 8 + 2 * p + half
            s = _nt(jnp.where(keep, qp, 0.0), kb)
            slope = jnp.exp((-8.0 * math.log(2.0) / ATT_HEADS) * jnp.asarray(h + 1, F32))
            s = jnp.where(valid, s - slope * relf, -1e30)
            sink = jnp.sum(jnp.where(lane16 == h, sinks, 0.0), axis=1, keepdims=True)
            m = lax.stop_gradient(jnp.maximum(jnp.max(s, axis=1, keepdims=True), sink))
            e = jnp.exp(s - m)
            den = jnp.sum(e, axis=1, keepdims=True) + jnp.exp(sink - m)
            o = _nn(e * (1.0 / den), vb)
            o_pair = o if o_pair is None else jnp.where(first, o_pair, o)
        outs.append(o_pair)
    return jnp.concatenate(outs, axis=1)


def _chip_exchange(src, out, send_sems, recv_sems, bcast):
    x, y, c = lax.axis_index("x"), lax.axis_index("y"), lax.axis_index("c")
    me = 2 * x + y

    def copy(a, d, slot):
        j = jnp.bitwise_xor(me, d)
        return pltpu.make_async_remote_copy(
            src_ref=src[a] if bcast else src[a].at[j], dst_ref=out[a].at[slot],
            send_sem=send_sems.at[a * 4 + d], recv_sem=recv_sems.at[a * 4 + d],
            device_id=(j // 2, j % 2, c), device_id_type=pl.DeviceIdType.MESH)

    pairs = [(a, d) for a in range(len(src)) for d in range(1, 4)]

    def start():
        for a, d in pairs:
            copy(a, d, me).start()

    def wait():
        for a, d in pairs:
            copy(a, d, jnp.bitwise_xor(me, d)).wait_recv()
        for a, d in pairs:
            copy(a, d, me).wait_send()

    return start, wait


def _mm(a, b, mode, out_dtype, tm, tn, tk, name, hosted=None):
    if mode == "nn":
        (m, k), (k2, n) = a.shape, b.shape
    elif mode == "nt":
        (m, k), (n, k2) = a.shape, b.shape
    else:
        (k, m), (k2, n) = a.shape, b.shape
    assert k == k2, (a.shape, b.shape, mode)
    tm, tn, tk = min(tm, m), min(tn, n), min(tk, k)
    assert m % tm == 0 and n % tn == 0 and k % tk == 0, (m, n, k, tm, tn, tk)
    nk = k // tk
    grid = (m // tm, n // tn, nk)
    ca, cb = {"nn": (1, 0), "nt": (1, 1), "tn": (0, 0)}[mode]
    h_arrays, h_bcast = hosted if hosted else ((), False)
    nh = len(h_arrays)

    def body(*refs):
        a_ref, b_ref = refs[:2]
        o_ref = refs[2 + nh]
        scratch = refs[3 + 2 * nh:]
        step = [pl.program_id(ax) for ax in range(3)]
        if nh:
            start, wait = _chip_exchange(refs[2:2 + nh], refs[3 + nh:3 + 2 * nh], scratch[-2], scratch[-1], h_bcast)

            @pl.when((step[0] == 0) & (step[1] == 0) & (step[2] == 0))
            def _():
                start()

        part = _dot(a_ref[...], b_ref[...], ca, cb)
        if nk == 1:
            o_ref[...] = part.astype(o_ref.dtype)
        else:
            acc_ref = scratch[0]

            @pl.when(step[2] == 0)
            def _():
                acc_ref[...] = part

            @pl.when(step[2] > 0)
            def _():
                acc_ref[...] += part

            @pl.when(step[2] == nk - 1)
            def _():
                o_ref[...] = acc_ref[...].astype(o_ref.dtype)

        if nh:
            @pl.when((step[0] == grid[0] - 1) & (step[1] == grid[1] - 1) & (step[2] == nk - 1))
            def _():
                wait()

    a_spec = pl.BlockSpec((tk, tm), lambda i, j, kk: (kk, i)) if mode == "tn" else pl.BlockSpec((tm, tk), lambda i, j, kk: (i, kk))
    b_spec = pl.BlockSpec((tn, tk), lambda i, j, kk: (j, kk)) if mode == "nt" else pl.BlockSpec((tk, tn), lambda i, j, kk: (kk, j))
    any_spec = pl.BlockSpec(memory_space=pl.ANY)
    res = pl.pallas_call(
        body, name=name, grid=grid,
        in_specs=[a_spec, b_spec] + [any_spec] * nh,
        out_specs=[pl.BlockSpec((tm, tn), lambda i, j, kk: (i, j))] + [any_spec] * nh,
        out_shape=[jax.ShapeDtypeStruct((m, n), out_dtype)]
        + [jax.ShapeDtypeStruct(((4,) + s.shape) if h_bcast else s.shape, s.dtype) for s in h_arrays],
        scratch_shapes=([] if nk == 1 else [pltpu.VMEM((tm, tn), F32)])
        + ([pltpu.SemaphoreType.DMA((4 * nh,)), pltpu.SemaphoreType.DMA((4 * nh,))] if nh else []),
        compiler_params=_params(3),
    )(a, b, *h_arrays)
    return (res[0], list(res[1:])) if nh else res[0]


def _rows(fn, rows, params, outs, accs, tile, name):
    length = rows[0][0].shape[0]
    tile = min(tile, length)
    assert length % tile == 0
    nr, npar, no = len(rows), len(params), len(outs)

    def body(*refs):
        vals = [r[...] for r in refs[:nr + npar]]
        o, a = fn(*vals)
        for ref, val in zip(refs[nr + npar:nr + npar + no], o):
            ref[...] = val.astype(ref.dtype)
        i = pl.program_id(0)
        for ref, val in zip(refs[nr + npar + no:], a):
            @pl.when(i == 0)
            def _(ref=ref, val=val):
                ref[...] = val

            @pl.when(i > 0)
            def _(ref=ref, val=val):
                ref[...] += val

    in_specs = [pl.BlockSpec((tile, w), functools.partial(lambda i, cb: (i, cb), cb=cb)) for _, cb, w in rows]
    in_specs += [pl.BlockSpec(p.shape, lambda i: (0, 0)) for p in params]
    outs = [o if len(o) == 3 else (o[0], o[1], o[0]) for o in outs]
    out_specs = [pl.BlockSpec((tile, w), lambda i: (i, 0)) for w, _, _ in outs]
    out_specs += [pl.BlockSpec((r, w), lambda i: (0, 0)) for r, w in accs]
    out_shape = [jax.ShapeDtypeStruct((length, full), dt) for _, dt, full in outs]
    out_shape += [jax.ShapeDtypeStruct((r, w), F32) for r, w in accs]
    res = pl.pallas_call(
        body, name=name, grid=(length // tile,), in_specs=in_specs, out_specs=out_specs, out_shape=out_shape,
        compiler_params=_params(1),
    )(*[r[0] for r in rows], *params)
    return res[:no], res[no:]


CONV_SUB = 32


def _conv_fwd(proj, conv_w, conv_b, name):
    length = proj.shape[0]
    tl, cw = min(512, length), 512
    cb0 = _OFF["xs"] // cw

    def body(u_ref, halo_ref, w_ref, b_ref, o_ref, win):
        i = pl.program_id(1)
        win[0:8, :] = jnp.where(i > 0, halo_ref[...], 0.0)
        win[8:8 + tl, :] = u_ref[...]
        taps = [w_ref[kk:kk + 1, :] for kk in range(SSD_CONV)]
        bias = b_ref[...]
        for r in range(0, tl, CONV_SUB):
            acc = bias + taps[0] * win[5 + r:5 + r + CONV_SUB, :]
            for kk in range(1, SSD_CONV):
                acc = acc + taps[kk] * win[5 + kk + r:5 + kk + r + CONV_SUB, :]
            o_ref[r:r + CONV_SUB, :] = acc

    return pl.pallas_call(
        body, name=name, grid=(CONV_DIM // cw, length // tl),
        in_specs=[pl.BlockSpec((tl, cw), lambda j, i: (i, cb0 + j)),
                  pl.BlockSpec((8, cw), lambda j, i: (jnp.maximum(i * (tl // 8) - 1, 0), cb0 + j)),
                  pl.BlockSpec((SSD_CONV, cw), lambda j, i: (0, j)),
                  pl.BlockSpec((1, cw), lambda j, i: (0, j))],
        out_specs=pl.BlockSpec((tl, cw), lambda j, i: (i, j)),
        out_shape=jax.ShapeDtypeStruct((length, CONV_DIM), F32),
        scratch_shapes=[pltpu.VMEM((8 + tl, cw), F32)],
        compiler_params=_params(2),
    )(proj, proj, conv_w, conv_b)


def _conv_bwd(dpre, proj, conv_w, dproj, part, name):
    length = proj.shape[0]
    width = dpre.shape[1]
    tl, cw = min(512, length), 512
    cb0 = _OFF["xs"] // cw + part
    n_t = length // tl

    def body(d_ref, dnext_ref, u_ref, halo_ref, w_ref, _, du_ref, dw_ref, db_ref, dwin, uwin):
        i = pl.program_id(1)
        dwin[0:tl, :] = d_ref[...]
        dwin[tl:tl + 8, :] = jnp.where(i < n_t - 1, dnext_ref[...], 0.0)
        uwin[0:8, :] = jnp.where(i > 0, halo_ref[...], 0.0)
        uwin[8:8 + tl, :] = u_ref[...]
        taps = [w_ref[kk:kk + 1, :] for kk in range(SSD_CONV)]
        sub = CONV_SUB // 2
        acc_w = [jnp.zeros((sub, cw), F32) for _ in range(SSD_CONV)]
        acc_b = jnp.zeros((sub, cw), F32)
        for r in range(0, tl, sub):
            d = dwin[r:r + sub, :]
            du = taps[3] * d
            for kk in range(SSD_CONV - 1):
                du = du + taps[kk] * dwin[3 - kk + r:3 - kk + r + sub, :]
            du_ref[r:r + sub, :] = du.astype(du_ref.dtype)
            for kk in range(SSD_CONV):
                acc_w[kk] = acc_w[kk] + d * uwin[5 + kk + r:5 + kk + r + sub, :]
            acc_b = acc_b + d
        dw = jnp.concatenate([jnp.sum(a, axis=0, keepdims=True) for a in acc_w], axis=0)
        db = jnp.sum(acc_b, axis=0, keepdims=True)

        @pl.when(i == 0)
        def _():
            dw_ref[...] = dw
            db_ref[...] = db

        @pl.when(i > 0)
        def _():
            dw_ref[...] += dw
            db_ref[...] += db

    return pl.pallas_call(
        body, name=name, grid=(width // cw, n_t), input_output_aliases={5: 0},
        in_specs=[pl.BlockSpec((tl, cw), lambda j, i: (i, j)),
                  pl.BlockSpec((8, cw), lambda j, i: (jnp.minimum((i + 1) * (tl // 8), length // 8 - 1), j)),
                  pl.BlockSpec((tl, cw), lambda j, i: (i, cb0 + j)),
                  pl.BlockSpec((8, cw), lambda j, i: (jnp.maximum(i * (tl // 8) - 1, 0), cb0 + j)),
                  pl.BlockSpec((SSD_CONV, cw), lambda j, i: (0, part + j)),
                  pl.BlockSpec(memory_space=pl.ANY)],
        out_specs=[pl.BlockSpec((tl, cw), lambda j, i: (i, cb0 + j)),
                   pl.BlockSpec((SSD_CONV, cw), lambda j, i: (0, j)),
                   pl.BlockSpec((1, cw), lambda j, i: (0, j))],
        out_shape=[jax.ShapeDtypeStruct((length, NP), BF),
                   jax.ShapeDtypeStruct((SSD_CONV, width), F32),
                   jax.ShapeDtypeStruct((1, width), F32)],
        scratch_shapes=[pltpu.VMEM((tl + 8, cw), F32), pltpu.VMEM((8 + tl, cw), F32)],
        compiler_params=_params(2),
    )(dpre, dpre, proj, proj, conv_w, dproj)


def _ssd_in_specs(rev, nc):
    def cidx(c):
        return nc - 1 - c if rev else c
    whole = lambda c: (0, 0)
    return [pl.BlockSpec((BLK, SSD_D_INNER), lambda c: (cidx(c), 0)),
            pl.BlockSpec((BLK, BC_DIM), lambda c: (cidx(c), SSD_D_INNER // BC_DIM)),
            pl.BlockSpec((BLK, BC_DIM), lambda c: (cidx(c), SSD_D_INNER // BC_DIM + 1)),
            pl.BlockSpec((BLK, SSD_D_INNER), lambda c: (cidx(c), _OFF["z"] // SSD_D_INNER)),
            pl.BlockSpec((SSD_HEADS, BLK), lambda c: (0, cidx(c))),
            pl.BlockSpec((SSD_HEADS, 1), whole),
            pl.BlockSpec((SSD_HEADS, 1), whole),
            pl.BlockSpec((1, SSD_D_INNER), whole),
            pl.BlockSpec((1, SSD_D_INNER), whole)]


def _group_args(g, px, pb, pc, z, dtr, dtb, alog, dsk, nw):
    wide, narrow, heads = slice(512 * g, 512 * (g + 1)), slice(BLK * g, BLK * (g + 1)), slice(8 * g, 8 * (g + 1))
    return (px[:, wide], pb[:, narrow], pc[:, narrow], z[:, wide], dtr[heads, :], dtb[heads, :], alog[heads, :],
            dsk[:, wide], nw[:, wide])


def _ssd_fwd(pre, proj, dt_t, dtb, alog, dsk, nw, name, hosted=None):
    length = pre.shape[0]
    nc = length // BLK
    h_arrays, h_bcast = hosted if hosted else ((), False)
    nh = len(h_arrays)

    def body(*refs):
        px, pb, pc, z, dtr, dtb_r, al_r, dsk_r, nw_r = refs[:9]
        y_ref, sin_ref = refs[9 + nh:11 + nh]
        st = refs[11 + 2 * nh]
        c = pl.program_id(0)
        if nh:
            start, wait = _chip_exchange(refs[9:9 + nh], refs[11 + nh:11 + 2 * nh], refs[-2], refs[-1], h_bcast)

        @pl.when(c == 0)
        def _():
            st[...] = jnp.zeros_like(st)
            if nh:
                start()

        for g in range(SSD_GROUPS):
            s_in = st[g]
            sin_ref[g] = s_in
            y, s_out = _ssd_group(*_group_args(g, px, pb, pc, z, dtr, dtb_r, al_r, dsk_r, nw_r), s_in)
            y_ref[:, 512 * g:512 * (g + 1)] = y.astype(y_ref.dtype)
            st[g] = s_out

        if nh:
            @pl.when(c == nc - 1)
            def _():
                wait()

    any_spec = pl.BlockSpec(memory_space=pl.ANY)
    res = pl.pallas_call(
        body, name=name, grid=(nc,), in_specs=_ssd_in_specs(False, nc) + [any_spec] * nh,
        out_specs=[pl.BlockSpec((BLK, SSD_D_INNER), lambda c: (c, 0)),
                   pl.BlockSpec((SSD_GROUPS, None, SSD_STATE, 512), lambda c: (0, c, 0, 0))] + [any_spec] * nh,
        out_shape=[jax.ShapeDtypeStruct((length, SSD_D_INNER), BF),
                   jax.ShapeDtypeStruct((SSD_GROUPS, nc, SSD_STATE, 512), F32)]
        + [jax.ShapeDtypeStruct(((4,) + s.shape) if h_bcast else s.shape, s.dtype) for s in h_arrays],
        scratch_shapes=[pltpu.VMEM((SSD_GROUPS, SSD_STATE, 512), F32)]
        + ([pltpu.SemaphoreType.DMA((4 * nh,)), pltpu.SemaphoreType.DMA((4 * nh,))] if nh else []),
        compiler_params=_params(1),
    )(pre, pre, pre, proj, dt_t, dtb, alog, dsk, nw, *h_arrays)
    return (res[0], res[1], list(res[2:])) if nh else (res[0], res[1])


def _ssd_bwd(pre, proj, dt_t, dtb, alog, dsk, nw, s_in, dy, dproj, name, hosted=None):
    length = pre.shape[0]
    nc = length // BLK
    h_arrays, h_bcast = hosted if hosted else ((), False)
    nh = len(h_arrays)

    def body(*refs):
        px, pb, pc, z, dtr, dtb_r, al_r, dsk_r, nw_r, sin_r, dy_r = refs[:11]
        dpx, dpb, dpc, dz, ddt, ddtb, dal, ddsk, dnw = refs[12 + nh:21 + nh]
        dst = refs[21 + 2 * nh]
        c = pl.program_id(0)
        if nh:
            start, wait = _chip_exchange(refs[12:12 + nh], refs[21 + nh:21 + 2 * nh], refs[-2], refs[-1], h_bcast)

        @pl.when(c == 0)
        def _():
            dst[...] = jnp.zeros_like(dst)
            if nh:
                start()

        for grp in range(SSD_GROUPS):
            wide, narrow, heads = slice(512 * grp, 512 * (grp + 1)), slice(BLK * grp, BLK * (grp + 1)), slice(8 * grp, 8 * (grp + 1))
            _, vjp = jax.vjp(_ssd_group, *_group_args(grp, px, pb, pc, z, dtr, dtb_r, al_r, dsk_r, nw_r), sin_r[grp])
            g = vjp((dy_r[:, wide], dst[grp]))
            dpx[:, wide] = g[0]
            dpb[:, narrow] = g[1]
            dpc[:, narrow] = g[2]
            dz[:, wide] = g[3].astype(dz.dtype)
            ddt[heads, :] = g[4]
            dst[grp] = g[9]
            for ref, val, idx in ((ddtb, g[5], (heads, slice(None))), (dal, g[6], (heads, slice(None))),
                                  (ddsk, g[7], (slice(None), wide)), (dnw, g[8], (slice(None), wide))):
                @pl.when(c == 0)
                def _(ref=ref, val=val, idx=idx):
                    ref[idx] = val

                @pl.when(c > 0)
                def _(ref=ref, val=val, idx=idx):
                    ref[idx] += val

        if nh:
            @pl.when(c == nc - 1)
            def _():
                wait()

    rev = lambda c: nc - 1 - c
    whole = lambda c: (0, 0)
    any_spec = pl.BlockSpec(memory_space=pl.ANY)
    in_specs = _ssd_in_specs(True, nc) + [
        pl.BlockSpec((SSD_GROUPS, None, SSD_STATE, 512), lambda c: (0, rev(c), 0, 0)),
        pl.BlockSpec((BLK, SSD_D_INNER), lambda c: (rev(c), 0)),
        any_spec] + [any_spec] * nh
    res = pl.pallas_call(
        body, name=name, grid=(nc,), in_specs=in_specs, input_output_aliases={11: 3},
        out_specs=[pl.BlockSpec((BLK, SSD_D_INNER), lambda c: (rev(c), 0)),
                   pl.BlockSpec((BLK, BC_DIM), lambda c: (rev(c), 0)),
                   pl.BlockSpec((BLK, BC_DIM), lambda c: (rev(c), 0)),
                   pl.BlockSpec((BLK, SSD_D_INNER), lambda c: (rev(c), _OFF["z"] // SSD_D_INNER)),
                   pl.BlockSpec((SSD_HEADS, BLK), lambda c: (0, rev(c))),
                   pl.BlockSpec((SSD_HEADS, 1), whole),
                   pl.BlockSpec((SSD_HEADS, 1), whole),
                   pl.BlockSpec((1, SSD_D_INNER), whole),
                   pl.BlockSpec((1, SSD_D_INNER), whole)] + [any_spec] * nh,
        out_shape=[jax.ShapeDtypeStruct((length, SSD_D_INNER), F32),
                   jax.ShapeDtypeStruct((length, BC_DIM), F32),
                   jax.ShapeDtypeStruct((length, BC_DIM), F32),
                   jax.ShapeDtypeStruct((length, NP), BF),
                   jax.ShapeDtypeStruct((SSD_HEADS, length), F32),
                   jax.ShapeDtypeStruct((SSD_HEADS, 1), F32),
                   jax.ShapeDtypeStruct((SSD_HEADS, 1), F32),
                   jax.ShapeDtypeStruct((1, SSD_D_INNER), F32),
                   jax.ShapeDtypeStruct((1, SSD_D_INNER), F32)]
        + [jax.ShapeDtypeStruct(((4,) + s.shape) if h_bcast else s.shape, s.dtype) for s in h_arrays],
        scratch_shapes=[pltpu.VMEM((SSD_GROUPS, SSD_STATE, 512), F32)]
        + ([pltpu.SemaphoreType.DMA((4 * nh,)), pltpu.SemaphoreType.DMA((4 * nh,))] if nh else []),
        compiler_params=_params(1),
    )(pre, pre, pre, proj, dt_t, dtb, alog, dsk, nw, s_in, dy, dproj, *h_arrays)
    return tuple(res[:9]) + ((list(res[9:]),) if nh else ())


assert _OFF["gates"] == 0 and _OFF["z"] % SSD_D_INNER == 0 and _OFF["q"] % D_MODEL == 0 and _OFF["k"] + 512 == NP


def _attn_in_specs():
    prev = lambda n: (0, jnp.maximum(n - 1, 0), 0)
    cur = lambda n: (0, n, 0)
    return [pl.BlockSpec((BLK, D_MODEL), lambda n: (n, _OFF["q"] // D_MODEL)),
            pl.BlockSpec((2, BLK, BLK), prev),
            pl.BlockSpec((2, BLK, BLK), cur),
            pl.BlockSpec((2, BLK, BLK), prev),
            pl.BlockSpec((2, BLK, BLK), cur),
            pl.BlockSpec((1, ATT_HEADS), lambda n: (0, 0))]


def _attn_fwd(proj, k2, v2, sinks, name, hosted=None):
    length = proj.shape[0]
    nb = length // BLK
    h_arrays, h_bcast = hosted if hosted else ((), False)
    nh = len(h_arrays)

    def body(*refs):
        q, kp, kc, vp, vc, sk = refs[:6]
        o_ref = refs[6 + nh]
        n = pl.program_id(0)
        if nh:
            start, wait = _chip_exchange(refs[6:6 + nh], refs[7 + nh:7 + 2 * nh], refs[-2], refs[-1], h_bcast)

            @pl.when(n == 0)
            def _():
                start()

        for kv in range(2):
            cols = slice(512 * kv, 512 * (kv + 1))
            o = _attn_block(q[:, cols], kp[kv], kc[kv], vp[kv], vc[kv], sk[...], n, kv)
            o_ref[:, cols] = o.astype(o_ref.dtype)

        if nh:
            @pl.when(n == nb - 1)
            def _():
                wait()

    any_spec = pl.BlockSpec(memory_space=pl.ANY)
    res = pl.pallas_call(
        body, name=name, grid=(nb,), in_specs=_attn_in_specs() + [any_spec] * nh,
        out_specs=[pl.BlockSpec((BLK, D_MODEL), lambda n: (n, 0))] + [any_spec] * nh,
        out_shape=[jax.ShapeDtypeStruct((length, D_MODEL), BF)]
        + [jax.ShapeDtypeStruct(((4,) + s.shape) if h_bcast else s.shape, s.dtype) for s in h_arrays],
        scratch_shapes=[pltpu.SemaphoreType.DMA((4 * nh,)), pltpu.SemaphoreType.DMA((4 * nh,))] if nh else [],
        compiler_params=_params(1),
    )(proj, k2, k2, v2, v2, sinks, *h_arrays)
    return (res[0], list(res[1:])) if nh else res[0]


def _attn_bwd(proj, k2, v2, sinks, datt, dproj, name):
    length = proj.shape[0]

    def body(q, kp, kc, vp, vc, sk, do, _, dq, dkp, dkc, dvp, dvc, dsk):
        n = pl.program_id(0)
        dsinks = None
        for kv in range(2):
            cols = slice(512 * kv, 512 * (kv + 1))
            _, vjp = jax.vjp(lambda *a: _attn_block(*a, n, kv), q[:, cols], kp[kv], kc[kv], vp[kv], vc[kv], sk[...])
            g = vjp(do[:, cols].astype(F32))
            dq[:, cols] = g[0].astype(dq.dtype)
            dkp[kv] = g[1]
            dkc[kv] = g[2]
            dvp[kv] = g[3]
            dvc[kv] = g[4]
            dsinks = g[5] if dsinks is None else dsinks + g[5]

        @pl.when(n == 0)
        def _():
            dsk[...] = dsinks

        @pl.when(n > 0)
        def _():
            dsk[...] += dsinks

    blk3 = pl.BlockSpec((2, BLK, BLK), lambda n: (0, n, 0))
    kv_shape = jax.ShapeDtypeStruct((2, length, BLK), F32)
    return pl.pallas_call(
        body, name=name, grid=(length // BLK,),
        in_specs=_attn_in_specs() + [pl.BlockSpec((BLK, D_MODEL), lambda n: (n, 0)), pl.BlockSpec(memory_space=pl.ANY)],
        out_specs=[pl.BlockSpec((BLK, D_MODEL), lambda n: (n, _OFF["q"] // D_MODEL)), blk3, blk3, blk3, blk3,
                   pl.BlockSpec((1, ATT_HEADS), lambda n: (0, 0))],
        out_shape=[jax.ShapeDtypeStruct((length, NP), BF), kv_shape, kv_shape, kv_shape, kv_shape,
                   jax.ShapeDtypeStruct((1, ATT_HEADS), F32)],
        input_output_aliases={7: 0},
        compiler_params=_params(1),
    )(proj, k2, k2, v2, v2, sinks, datt, dproj)


DMA_CHUNK_BYTES = 1 << 20
N_STAGE = 8
LOOKAHEAD = 3


def _piece_chunks(shape, itemsize):
    if len(shape) < 2 or shape[-2] % 16 != 0:
        return [()]
    rows, cols = shape[-2:]
    step = min(rows, max(16, DMA_CHUNK_BYTES // (cols * itemsize) // 16 * 16))
    out = []
    for lead in itertools.product(*[range(d) for d in shape[:-2]]):
        for r0 in range(0, rows, step):
            out.append(lead + (pl.ds(r0, min(step, rows - r0)),))
    return out


def _exchange(srcs, group, bcast, name):
    size = {"c": 2, "xy": 4, "all": 8}[group]
    npeer = size - 1
    n = len(srcs)
    for s in srcs:
        assert bcast or s.shape[0] == size
    pieces = [s.shape if bcast else s.shape[1:] for s in srcs]
    chunks = [_piece_chunks(p, s.dtype.itemsize) for p, s in zip(pieces, srcs)]

    def body(*refs):
        src, out = refs[:n], refs[n:2 * n]
        load_sems, send_sems, recv_sems = refs[2 * n:]
        x, y, c = lax.axis_index("x"), lax.axis_index("y"), lax.axis_index("c")
        if group == "c":
            me = c
        elif group == "xy":
            me = 2 * x + y
        else:
            me = 4 * x + 2 * y + c

        def device(j):
            if group == "c":
                return (x, y, j)
            if group == "xy":
                return (j // 2, j % 2, c)
            return (j // 4, (j // 2) % 2, j % 2)

        def part(ref, idx):
            return ref.at[idx] if idx else ref

        def piece(a, j):
            return src[a] if bcast else src[a].at[j]

        for a in range(n):
            jobs = [(None, idx) for idx in chunks[a]] if bcast else [(d, idx) for idx in chunks[a] for d in range(1, size)]
            full_rows = chunks[a][0][-1].size if chunks[a][0] else None
            slot_shape = (full_rows, pieces[a][-1]) if chunks[a][0] else pieces[a]

            def stream(stage, a=a, jobs=jobs, full_rows=full_rows):
                def slot(q):
                    idx = jobs[q][1]
                    view = stage.at[q % N_STAGE]
                    if idx and idx[-1].size != full_rows:
                        view = view.at[pl.ds(0, idx[-1].size)]
                    return view

                def load(q):
                    d, idx = jobs[q]
                    j = me if d is None else jnp.bitwise_xor(me, d)
                    return pltpu.make_async_copy(part(piece(a, j), idx), slot(q), load_sems.at[q % N_STAGE])

                def sends(q):
                    d, idx = jobs[q]
                    return [pltpu.make_async_remote_copy(
                        src_ref=slot(q), dst_ref=part(out[a].at[me], idx),
                        send_sem=send_sems.at[(q % N_STAGE) * npeer + dd - 1], recv_sem=recv_sems.at[a * size + dd],
                        device_id=device(jnp.bitwise_xor(me, dd)), device_id_type=pl.DeviceIdType.MESH)
                        for dd in (range(1, size) if d is None else (d,))]

                nq = len(jobs)
                for q in range(nq + LOOKAHEAD):
                    if q < nq:
                        if q >= N_STAGE:
                            for cp in sends(q - N_STAGE):
                                cp.wait_send()
                        load(q).start()
                    if q >= LOOKAHEAD:
                        load(q - LOOKAHEAD).wait()
                        for cp in sends(q - LOOKAHEAD):
                            cp.start()
                for q in range(max(0, nq - N_STAGE), nq):
                    for cp in sends(q):
                        cp.wait_send()

            pl.run_scoped(stream, pltpu.VMEM((N_STAGE,) + tuple(slot_shape), srcs[a].dtype))

        for a in range(n):
            for d in range(1, size):
                j = jnp.bitwise_xor(me, d)
                pltpu.make_async_remote_copy(
                    src_ref=piece(a, j), dst_ref=out[a].at[j], send_sem=send_sems.at[0], recv_sem=recv_sems.at[a * size + d],
                    device_id=device(j), device_id_type=pl.DeviceIdType.MESH).wait_recv()

    any_spec = pl.BlockSpec(memory_space=pl.ANY)
    return pl.pallas_call(
        body, name=name, in_specs=[any_spec] * n, out_specs=[any_spec] * n,
        out_shape=[jax.ShapeDtypeStruct((size,) + tuple(p), s.dtype) for p, s in zip(pieces, srcs)],
        scratch_shapes=[pltpu.SemaphoreType.DMA((N_STAGE,)), pltpu.SemaphoreType.DMA((N_STAGE * npeer,)),
                        pltpu.SemaphoreType.DMA((n * size,))],
        compiler_params=pltpu.CompilerParams(vmem_limit_bytes=VMEM_LIMIT_BYTES),
    )(*srcs)


def _with_own(outs, owns, me):
    return [lax.dynamic_update_index_in_dim(o, w, me, 0) for o, w in zip(outs, owns)]


def _sum_slots(arr, name):
    k = arr.shape[0]
    rest = arr.shape[1:]
    width = rest[-1]
    rows_per = math.prod(rest[:-1])
    a2 = arr.reshape(k * rows_per, width)
    tile = rows_per
    for cand in (256, 128, 64, 32, 16, 8):
        if rows_per % cand == 0:
            tile = cand
            break
    nt = rows_per // tile

    def body(*refs):
        acc = refs[0][...]
        for r in refs[1:k]:
            acc = acc + r[...]
        refs[k][...] = acc

    in_specs = [pl.BlockSpec((tile, width), functools.partial(lambda i, s: (s * nt + i, 0), s=s)) for s in range(k)]
    out = pl.pallas_call(
        body, name=name, grid=(nt,), in_specs=in_specs, out_specs=pl.BlockSpec((tile, width), lambda i: (i, 0)),
        out_shape=jax.ShapeDtypeStruct((rows_per, width), arr.dtype), compiler_params=_params(1),
    )(*([a2] * k))
    return out.reshape(rest)


def _sum_pieces(recv, own, own_slotted, me, name):
    k = recv.shape[0]
    rest = recv.shape[1:]
    width = rest[-1]
    rows_per = math.prod(rest[:-1])
    r2 = recv.reshape(k * rows_per, width)
    o2 = own.reshape(-1, width)
    tile = rows_per
    for cand in (256, 128, 64, 32, 16, 8):
        if rows_per % cand == 0:
            tile = cand
            break
    nt = rows_per // tile

    def body(me_ref, *refs):
        mine = refs[k][...].astype(F32)
        acc = None
        for j in range(k):
            term = jnp.where(me_ref[0] == j, mine, refs[j][...].astype(F32))
            acc = term if acc is None else acc + term
        refs[k + 1][...] = acc
        refs[k + 2][...] = acc.astype(BF)

    in_specs = [pl.BlockSpec((tile, width), functools.partial(lambda i, m, s: (s * nt + i, 0), s=s)) for s in range(k)]
    in_specs.append(pl.BlockSpec((tile, width), (lambda i, m: (m[0] * nt + i, 0)) if own_slotted else (lambda i, m: (i, 0))))
    out_spec = pl.BlockSpec((tile, width), lambda i, m: (i, 0))
    out, out_bf = pl.pallas_call(
        body, name=name,
        grid_spec=pltpu.PrefetchScalarGridSpec(num_scalar_prefetch=1, grid=(nt,), in_specs=in_specs, out_specs=[out_spec, out_spec]),
        out_shape=[jax.ShapeDtypeStruct((rows_per, width), F32), jax.ShapeDtypeStruct((rows_per, width), BF)],
        compiler_params=_params(1),
    )(jnp.reshape(me, (1,)).astype(jnp.int32), *([r2] * k), o2)
    return out.reshape(rest), out_bf.reshape(rest)


def _adamw(w, g, m, v, name):
    def fn(w_, g_, m_, v_):
        m1 = ADAM_B1 * m_ + (1.0 - ADAM_B1) * g_
        v1 = ADAM_B2 * v_ + (1.0 - ADAM_B2) * (g_ * g_)
        m_hat = m1 / (1.0 - ADAM_B1 ** ADAM_STEP)
        v_hat = v1 / (1.0 - ADAM_B2 ** ADAM_STEP)
        delta = -ADAM_LR * (m_hat / (jnp.sqrt(v_hat) + ADAM_EPS) + ADAM_WD * w_)
        return (delta, m1, v1), ()

    rows, width = w.shape
    tile = rows
    for cand in (256, 128, 64, 32, 16, 8):
        if rows % cand == 0:
            tile = cand
            break
    o, _ = _rows(fn, [(w, 0, width), (g, 0, width), (m, 0, width), (v, 0, width)], [],
                 [(width, F32)] * 3, [], tile, name)
    return o


def _col_pieces(start, width, shard):
    out = []
    while width:
        chip, off = divmod(start, shard)
        take = min(width, shard - off)
        out.append((chip, off, take))
        start, width = start + take, width - take
    return out


def _gather_cols(parts, ranges, pad=0):
    shard = parts.shape[3]
    halves = []
    for h in range(2):
        cols = [parts[h, chip, :, off:off + w] for s, wd in ranges for chip, off, w in _col_pieces(s, wd, shard)]
        if pad:
            cols.append(jnp.zeros((parts.shape[2], pad), parts.dtype))
        halves.append(jnp.concatenate(cols, axis=1))
    return jnp.concatenate(halves, axis=0)


def _scatter_cols(g, ranges, shard):
    starts, pos = {}, 0
    for s, wd in ranges:
        starts[s] = (pos, wd)
        pos += wd
    order = sorted(starts)
    hr = g.shape[0] // 2
    out = []
    for h in range(2):
        per_chip = []
        for chip in range(4):
            cols = []
            for s in order:
                p0, wd = starts[s]
                lo, hi = max(s, chip * shard), min(s + wd, (chip + 1) * shard)
                if lo < hi:
                    cols.append(g[h * hr:(h + 1) * hr, p0 + lo - s:p0 + hi - s])
            per_chip.append(jnp.concatenate(cols, axis=1))
        out.append(jnp.stack(per_chip))
    return jnp.stack(out)


_IN_RANGES = tuple((o, wd) for _, o, wd in _PACK)


def _double_heads(t):
    length = t.shape[0]
    h = jnp.transpose(t.reshape(length, 2, 64), (1, 0, 2))
    return jnp.concatenate([h, h], axis=-1)


def _fold_heads(d_cur, d_prev):
    length = d_cur.shape[1]
    d = d_cur + jnp.concatenate([d_prev[:, BLK:], jnp.zeros((2, BLK, BLK), F32)], axis=1)
    d = d[..., :64] + d[..., 64:]
    return jnp.transpose(d, (1, 0, 2)).reshape(length, 128)


def _ln_fwd(a, r, g, b, name):
    def fn(a_, r_, g_, b_):
        y = _layer_norm(ALPHA * a_ + r_, g_, b_)
        return (y, y), ()
    o, _ = _rows(fn, [(a, 0, D_MODEL), (r, 0, D_MODEL)], [g, b], [(D_MODEL, F32), (D_MODEL, BF)], [], 256, name)
    return o


def _ln_bwd(a, r, d1, d2, g, b, name):
    def fn(a_, r_, d1_, d2_, g_, b_):
        _, vjp = jax.vjp(_layer_norm, ALPHA * a_ + r_, g_, b_)
        ds, dg, db = vjp(ALPHA * d1_ + d2_)
        return (ds, ds), (dg, db)
    o, acc = _rows(fn, [(a, 0, D_MODEL), (r, 0, D_MODEL), (d1, 0, D_MODEL), (d2, 0, D_MODEL)], [g, b],
                   [(D_MODEL, F32), (D_MODEL, BF)], [(1, D_MODEL), (1, D_MODEL)], 256, name)
    return o[0], o[1], acc[0], acc[1]


def _layer_fwd(l, h, hb, w, p, hosted=None, hosted_attn=None):
    sv = {"h": h, "hb": hb}
    proj = _mm(hb, w["win"], "nn", F32, 2048, 512, 1024, f"proj{l}")
    sv["proj"] = proj
    pre = _conv_fwd(proj, p["conv_w"], p["conv_b"], f"conv{l}")
    sv["pre"] = pre
    dt_t = jnp.transpose(proj[:, _OFF["dt"]:_OFF["dt"] + SSD_HEADS])
    sv["dt_t"] = dt_t
    res = _ssd_fwd(pre, proj, dt_t, p["dt_bias"], p["a_log"], p["d_skip_c"], p["ssd_norm_w"], f"ssd{l}", hosted)
    yn, s_in = res[:2]
    if hosted:
        sv["hosted"] = res[2]
    sv["yn"], sv["s_in"] = yn, s_in
    ya = _mm(yn, w["wso"], "nn", F32, 1024, 1024, 2048, f"ssdout{l}")
    k2 = _double_heads(proj[:, _OFF["k"]:_OFF["k"] + 128])
    v2 = _double_heads(proj[:, _OFF["v"]:_OFF["v"] + 128])
    sv["k2"], sv["v2"] = k2, v2
    att = _attn_fwd(proj, k2, v2, p["att_sinks"], f"attn{l}", hosted_attn)
    if hosted_attn:
        att, sv["hosted_attn"] = att
    sv["att"] = att
    yb = _mm(att, w["wao"], "nn", F32, 1024, 1024, 1024, f"attout{l}")
    sv["ya"], sv["yb"] = ya, yb
    gcb = _OFF["gates"] // 1024

    def gate_fn(ga, gb, ya_, yb_):
        return (_sigmoid(ga) * ya_ + _sigmoid(gb) * yb_,), ()
    (ub,), _ = _rows(gate_fn, [(proj, gcb, 1024), (proj, gcb + 1, 1024), (ya, 0, 1024), (yb, 0, 1024)], [],
                     [(D_MODEL, BF)], [], 256, f"gate{l}")
    sv["ub"] = ub
    mix = _mm(ub, w["wmo"], "nn", F32, 1024, 1024, 1024, f"mixout{l}")
    sv["mix"] = mix
    h1, h1b = _ln_fwd(h, mix, p["ln_mix_g"], p["ln_mix_b"], f"lnmix{l}")
    sv["h1"], sv["h1b"] = h1, h1b
    gu = _mm(h1b, w["wgu"], "nn", F32, 2048, 512, 1024, f"ffnin{l}")
    sv["gu"] = gu

    def act_fn(g_, u_):
        return (_silu(g_) * u_,), ()
    (act,), _ = _rows(act_fn, [(gu, 0, FFN_HIDDEN), (gu, 1, FFN_HIDDEN)], [], [(FFN_HIDDEN, BF)], [], 256, f"swiglu{l}")
    sv["act"] = act
    ffn = _mm(act, w["wd"], "nn", F32, 1024, 1024, FFN_HIDDEN, f"ffnout{l}")
    sv["ffn"] = ffn
    return sv


def _layer_bwd(l, sv, w, p, ds_f, ds_b, hosted=(), early_hook=None, late_hook=None):
    g = {}
    g["w_ffn_down"] = _mm(sv["act"], ds_b, "tn", F32, 1408, 1024, 1024, f"dwd{l}")
    dact = _mm(ds_b, w["wd"], "nt", BF, 1024, FFN_HIDDEN, 1024, f"dact{l}")

    def act_bwd(g_, u_, d_):
        _, vjp = jax.vjp(lambda a, b: _silu(a) * b, g_, u_)
        dg_, du_ = vjp(d_.astype(F32))
        return (jnp.concatenate([dg_, du_], axis=1),), ()
    (dgu,), _ = _rows(act_bwd, [(sv["gu"], 0, FFN_HIDDEN), (sv["gu"], 1, FFN_HIDDEN), (dact, 0, FFN_HIDDEN)], [],
                      [(2 * FFN_HIDDEN, BF)], [], 128, f"dswiglu{l}")
    dwgu = _mm(sv["h1b"], dgu, "tn", F32, 1024, 2816, 512, f"dwgu{l}")
    g["w_ffn_gate"], g["w_ffn_up"] = dwgu[:, :FFN_HIDDEN], dwgu[:, FFN_HIDDEN:]
    dh1 = _mm(dgu, w["wgu"], "nt", F32, 1024, 1024, 2816, f"dh1{l}")
    ds2_f, ds2_b, g["ln_mix_g"], g["ln_mix_b"] = _ln_bwd(sv["h"], sv["mix"], ds_f, dh1, p["ln_mix_g"], p["ln_mix_b"], f"dlnmix{l}")
    g["w_mix_out"] = _mm(sv["ub"], ds2_b, "tn", F32, 1024, 1024, 512, f"dwmo{l}")
    du = _mm(ds2_b, w["wmo"], "nt", BF, 1024, 1024, 1024, f"du{l}")
    proj = sv["proj"]
    gcb = _OFF["gates"] // 1024

    def gate_bwd(ga, gb, ya_, yb_, du_):
        _, vjp = jax.vjp(lambda a, b, c, d: _sigmoid(a) * c + _sigmoid(b) * d, ga, gb, ya_, yb_)
        dga, dgb, dya, dyb = vjp(du_.astype(F32))
        return (jnp.concatenate([dga, dgb], axis=1), dya, dyb), ()
    (dproj, dya, dyb), _ = _rows(
        gate_bwd, [(proj, gcb, 1024), (proj, gcb + 1, 1024), (sv["ya"], 0, 1024), (sv["yb"], 0, 1024), (du, 0, 1024)], [],
        [(2048, BF, NP), (D_MODEL, BF), (D_MODEL, BF)], [], 256, f"dgate{l}")
    g["w_att_out"] = _mm(sv["att"], dyb, "tn", F32, 1024, 1024, 512, f"dwao{l}")
    datt = _mm(dyb, w["wao"], "nt", BF, 1024, 1024, 1024, f"datt{l}")
    g["w_ssd_out"] = _mm(sv["yn"], dya, "tn", F32, 1024, 1024, 512, f"dwso{l}")
    dyn = _mm(dya, w["wso"], "nt", F32, 1024, 1024, 1024, f"dyn{l}")
    hosted = list(hosted) + (list(early_hook(g)) if early_hook else [])
    dproj, dkp, dkc, dvp, dvc, g["att_sinks"] = _attn_bwd(proj, sv["k2"], sv["v2"], p["att_sinks"], datt, dproj, f"dattn{l}")
    res = _ssd_bwd(sv["pre"], proj, sv["dt_t"], p["dt_bias"], p["a_log"], p["d_skip_c"], p["ssd_norm_w"], sv["s_in"], dyn,
                   dproj, f"dssd{l}", (hosted, False) if hosted else None)
    (dpx, dpb, dpc, dproj, ddt_t, g["dt_bias"], g["a_log"], ddsk, g["ssd_norm_w"]) = res[:9]
    if hosted:
        g["hosted"] = res[9]
    g["d_skip"] = jnp.sum(ddsk.reshape(SSD_HEADS, 64), axis=1)
    dws, dbs = [], []
    for nm, dpart, part in (("x", dpx, 0), ("b", dpb, 4), ("c", dpc, 5)):
        dproj, dw_, db_ = _conv_bwd(dpart, proj, p["conv_w"], dproj, part, f"dconv{nm}{l}")
        dws.append(dw_)
        dbs.append(db_)
    g["conv_w"], g["conv_b"] = jnp.concatenate(dws, axis=1), jnp.concatenate(dbs, axis=1)
    length = proj.shape[0]
    tail = jnp.concatenate([_fold_heads(dkc, dkp).astype(BF), _fold_heads(dvc, dvp).astype(BF),
                            jnp.transpose(ddt_t).astype(BF), jnp.zeros((length, _PAD), BF)], axis=1)
    dproj = lax.dynamic_update_slice(dproj, tail, (0, _OFF["k"]))
    g["w_in"] = _mm(sv["hb"], dproj, "tn", F32, 1024, 2176, 1024, f"dwin{l}")
    late = list(late_hook(g)) if late_hook else []
    dh = _mm(dproj, w["win"], "nt", F32, 1024, 1024, 2176, f"dh{l}", (late, False) if late else None)
    if late:
        dh, g["late_hosted"] = dh
    return g, ds2_f, dh


def kernel(x, ln_in_g, ln_in_b, w_in, conv_w, conv_b, dt_bias, a_log, d_skip, ssd_norm_w, att_sinks, w_ssd_out, w_att_out, w_mix_out, ln_mix_g, ln_mix_b, w_ffn_gate, w_ffn_up, w_ffn_down, ln_ffn_g, ln_ffn_b, loss_target, m_ln_in_g, m_ln_in_b, m_w_in, m_conv_w, m_conv_b, m_dt_bias, m_a_log, m_d_skip, m_ssd_norm_w, m_att_sinks, m_w_ssd_out, m_w_att_out, m_w_mix_out, m_ln_mix_g, m_ln_mix_b, m_w_ffn_gate, m_w_ffn_up, m_w_ffn_down, m_ln_ffn_g, m_ln_ffn_b, v_ln_in_g, v_ln_in_b, v_w_in, v_conv_w, v_conv_b, v_dt_bias, v_a_log, v_d_skip, v_ssd_norm_w, v_att_sinks, v_w_ssd_out, v_w_att_out, v_w_mix_out, v_ln_mix_g, v_ln_mix_b, v_w_ffn_gate, v_w_ffn_up, v_w_ffn_down, v_ln_ffn_g, v_ln_ffn_b):
    env = dict(locals())
    wts = {n: env[n] for n in WEIGHTS}
    mom1 = {n: env["m_" + n] for n in WEIGHTS}
    mom2 = {n: env["v_" + n] for n in WEIGHTS}
    ci = lax.axis_index("c")
    chip = 2 * lax.axis_index("x") + lax.axis_index("y")
    xs_ = x[0]
    tgt = loss_target[0]

    COLS = ("w_in", "w_ffn_gate", "w_ffn_up")

    def half(t):
        return lax.dynamic_slice_in_dim(t, ci * (t.shape[0] // 2), t.shape[0] // 2, axis=0)

    def assemble(name, parts):
        if name == "w_in":
            return _gather_cols(parts, _IN_RANGES, _PAD)
        if name in COLS:
            return _gather_cols(parts, ((0, 4 * parts.shape[3]),))
        return jnp.concatenate([parts[h, s] for s in range(4) for h in range(2)], axis=0)

    def split(name, gfull):
        if name == "w_in":
            return _scatter_cols(gfull, _IN_RANGES, IN_DIM // 4)
        if name in COLS:
            return _scatter_cols(gfull, ((0, gfull.shape[1]),), gfull.shape[1] // 4)
        hr = gfull.shape[0] // 8
        return jnp.stack([jnp.stack([gfull[(2 * s + h) * hr:(2 * s + h + 1) * hr] for s in range(4)]) for h in range(2)])

    mine = [[half(wts[n][l].astype(BF)) for n in BIG] for l in range(DEPTH)]
    conv_all, = _with_own(_exchange([conv_w], "xy", True, "gather_conv"), [conv_w], chip)
    conv_full = jnp.transpose(conv_all, (1, 2, 0, 3)).reshape(DEPTH, SSD_CONV, CONV_DIM)

    def layer_weights(l, by_chip):
        by_chip = _with_own(by_chip, mine[l], chip)
        parts = dict(zip(BIG, _with_own(_exchange(by_chip, "c", True, f"gather_cores{l}"), by_chip, ci)))
        fw = {n: assemble(n, parts[n]) for n in BIG}
        return {"win": fw["w_in"], "wso": fw["w_ssd_out"], "wao": fw["w_att_out"], "wmo": fw["w_mix_out"],
                "wgu": jnp.concatenate([fw["w_ffn_gate"], fw["w_ffn_up"]], axis=1), "wd": fw["w_ffn_down"]}

    layer_p = []
    for l in range(DEPTH):
        layer_p.append({
            "conv_w": conv_full[l], "conv_b": conv_b[l][None],
            "dt_bias": dt_bias[l][:, None], "a_log": a_log[l][:, None],
            "d_skip_c": jnp.repeat(d_skip[l], 64)[None], "ssd_norm_w": ssd_norm_w[l][None],
            "att_sinks": att_sinks[l][None], "ln_mix_g": ln_mix_g[l][None], "ln_mix_b": ln_mix_b[l][None],
            "ln_ffn_g": ln_ffn_g[l][None], "ln_ffn_b": ln_ffn_b[l][None],
        })

    def ln_in_fn(x_, g_, b_):
        y = _layer_norm(x_, g_, b_)
        return (y, y), ()
    (h, hb), _ = _rows(ln_in_fn, [(xs_, 0, D_MODEL)], [ln_in_g[None], ln_in_b[None]], [(D_MODEL, F32), (D_MODEL, BF)], [], 256, "ln_in")
    layer_w = [layer_weights(0, _exchange(mine[0], "xy", True, "gather_chips0"))]
    behind_ssd = [i for i, n in enumerate(BIG) if n in COLS]
    behind_attn = [i for i, n in enumerate(BIG) if n not in COLS]
    saved = []
    for l in range(DEPTH):
        nxt = l + 1 < DEPTH
        sv = _layer_fwd(l, h, hb, layer_w[l], layer_p[l], ([mine[l + 1][i] for i in behind_ssd], True) if nxt else None,
                        ([mine[l + 1][i] for i in behind_attn], True) if nxt else None)
        saved.append(sv)
        if nxt:
            got = dict(zip(behind_ssd + behind_attn, sv["hosted"] + sv["hosted_attn"]))
            layer_w.append(layer_weights(l + 1, [got[i] for i in range(len(BIG))]))
            h, hb = _ln_fwd(sv["h1"], sv["ffn"], layer_p[l]["ln_ffn_g"], layer_p[l]["ln_ffn_b"], f"lnffn{l}")

    def loss_fn(a_, r_, t_, g_, b_):
        y, vjp = jax.vjp(_layer_norm, ALPHA * a_ + r_, g_, b_)
        err = y - t_
        ds, dg, db = vjp(err * (1.0 / D_MODEL))
        part = 0.5 * jnp.sum(jnp.mean(err * err, axis=-1, keepdims=True), axis=0, keepdims=True)
        return (ds, ds), (dg, db, jnp.broadcast_to(part, (1, BLK)))
    sv = saved[-1]
    (ds_f, ds_b), (dg_last, db_last, loss_part) = _rows(
        loss_fn, [(sv["h1"], 0, D_MODEL), (sv["ffn"], 0, D_MODEL), (tgt, 0, D_MODEL)],
        [layer_p[-1]["ln_ffn_g"], layer_p[-1]["ln_ffn_b"]], [(D_MODEL, F32), (D_MODEL, BF)],
        [(1, D_MODEL), (1, D_MODEL), (1, BLK)], 256, "loss")
    loss = lax.psum(loss_part[0, 0], ("x", "y", "c"))

    def reduce_cores(l, g, names):
        src = [split(n, g[n]) for n in names]
        recv = _exchange(src, "c", False, f"reduce_cores{l}_{names[0]}")
        return {n: _sum_pieces(r, o, True, ci, f"sum_cores{l}_{n}") for n, r, o in zip(names, recv, src)}

    def bf16_of(sums, names):
        return [sums[n][1] for n in names]

    def finish_reduce(l, sums, from_chips):
        reduced = [_sum_pieces(from_chips[n], sums[n][0], True, chip, f"sum_chips{l}_{n}")[0] for n in BIG]
        halves = _with_own(_exchange(reduced, "c", True, f"swap_halves{l}"), reduced, ci)
        return {n: t.reshape(2 * t.shape[1], t.shape[2]) for n, t in zip(BIG, halves)}

    EARLY = tuple(n for n in BIG if n != "w_in")
    grads = [None] * DEPTH
    chip_sums = [None] * DEPTH
    for l in reversed(range(DEPTH)):
        sv = saved[l]
        if l == DEPTH - 1:
            lg, lb = dg_last, db_last
        else:
            ds_f, ds_b, lg, lb = _ln_bwd(sv["h1"], sv["ffn"], d1, d2, layer_p[l]["ln_ffn_g"], layer_p[l]["ln_ffn_b"], f"dlnffn{l}")
        sums = {}

        def early_hook(g, l=l, sums=sums):
            sums.update(reduce_cores(l, g, EARLY))
            return bf16_of(sums, EARLY)

        def late_hook(g, l=l, sums=sums):
            sums.update(reduce_cores(l, g, ("w_in",)))
            return bf16_of(sums, ("w_in",))
        if l == 0:
            g, d1, d2 = _layer_bwd(l, sv, layer_w[l], layer_p[l], ds_f, ds_b, bf16_of(chip_sums[l + 1], BIG), early_hook, late_hook)
        else:
            g, d1, d2 = _layer_bwd(l, sv, layer_w[l], layer_p[l], ds_f, ds_b)
            sums.update(reduce_cores(l, g, BIG))
        g["ln_ffn_g"], g["ln_ffn_b"] = lg, lb
        grads[l] = g
        chip_sums[l] = sums

    def ln_in_bwd(x_, d1_, d2_, g_, b_):
        _, vjp = jax.vjp(_layer_norm, x_, g_, b_)
        dx, dg, db = vjp(ALPHA * d1_ + d2_)
        return (dx,), (dg, db)
    (grad_x,), (g_ln_in_g, g_ln_in_b) = _rows(
        ln_in_bwd, [(xs_, 0, D_MODEL), (d1, 0, D_MODEL), (d2, 0, D_MODEL)], [ln_in_g[None], ln_in_b[None]],
        [(D_MODEL, F32)], [(1, D_MODEL), (1, D_MODEL)], 256, "dln_in")

    shard_grads = [None] * DEPTH
    carried = grads[0]["hosted"]
    shard_grads[1] = finish_reduce(1, chip_sums[1], dict(zip(BIG, carried[:len(BIG)])))
    from_chips = dict(zip(EARLY, carried[len(BIG):]))
    from_chips["w_in"], = grads[0]["late_hosted"]
    shard_grads[0] = finish_reduce(0, chip_sums[0], from_chips)
    big_grad = {n: jnp.stack([shard_grads[l][n] for l in range(DEPTH)]).reshape(wts[n].shape) for n in BIG}

    small_g = {"ln_in_g": g_ln_in_g[0], "ln_in_b": g_ln_in_b[0]}
    for n in SMALL[2:]:
        small_g[n] = jnp.stack([grads[l][n].reshape(wts[n].shape[1:] if n != "conv_w" else (SSD_CONV, CONV_DIM)) for l in range(DEPTH)])
    flat = jnp.concatenate([small_g[n].reshape(-1) for n in SMALL])
    n_small = flat.shape[0]
    width = -(-n_small // 1024) * 1024
    flat = jnp.pad(flat, (0, width - n_small)).reshape(8, width // 8)
    gathered, = _with_own(_exchange([flat], "all", True, "gather_small"), [flat], 2 * chip + ci)
    total = _sum_slots(gathered, "sum_small").reshape(-1)
    small_grad, off = {}, 0
    for n in SMALL:
        shp = small_g[n].shape
        cnt = math.prod(shp)
        small_grad[n] = total[off:off + cnt].reshape(shp)
        off += cnt
    small_grad["conv_w"] = lax.dynamic_slice_in_dim(small_grad["conv_w"], chip * (CONV_DIM // 4), CONV_DIM // 4, axis=2)

    out_g, out_d, out_m, out_v = {}, {}, {}, {}
    for n in BIG:
        shp = wts[n].shape
        two_d = lambda t: t.reshape(shp[0] * shp[1], shp[2])
        d_, m_, v_ = _adamw(two_d(wts[n]), two_d(big_grad[n]), two_d(mom1[n]), two_d(mom2[n]), f"adamw_{n}")
        out_g[n], out_d[n], out_m[n], out_v[n] = big_grad[n], d_.reshape(shp), m_.reshape(shp), v_.reshape(shp)

    def flat_small(d):
        f = jnp.concatenate([d[n].reshape(-1) for n in SMALL])
        return jnp.pad(f, (0, swidth - f.shape[0])).reshape(8, swidth // 8)
    n_sw = sum(math.prod(wts[n].shape) for n in SMALL)
    swidth = -(-n_sw // 1024) * 1024
    d_, m_, v_ = _adamw(flat_small(wts), flat_small(small_grad), flat_small(mom1), flat_small(mom2), "adamw_small")
    off = 0
    for n in SMALL:
        shp = wts[n].shape
        cnt = math.prod(shp)
        out_g[n] = small_grad[n]
        out_d[n] = d_.reshape(-1)[off:off + cnt].reshape(shp)
        out_m[n] = m_.reshape(-1)[off:off + cnt].reshape(shp)
        out_v[n] = v_.reshape(-1)[off:off + cnt].reshape(shp)
        off += cnt

    return (loss, grad_x[None], *[out_g[n] for n in WEIGHTS], *[out_d[n] for n in WEIGHTS],
            *[out_m[n] for n in WEIGHTS], *[out_v[n] for n in WEIGHTS])
```

```python
import functools
import itertools
import math

import jax
import jax.numpy as jnp
from jax import lax
from jax.experimental import pallas as pl
from jax.experimental.pallas import tpu as pltpu

F32 = jnp.float32
BF = jnp.bfloat16

D_MODEL = 1024
DEPTH = 2
ATT_HEADS = 16
ATT_HEAD_DIM = 64
BLK = 128
SSD_D_INNER = 2048
SSD_HEADS = 32
SSD_GROUPS = 4
SSD_STATE = 128
SSD_CONV = 4
BC_DIM = 512
CONV_DIM = 3072
FFN_HIDDEN = 2816
IN_DIM = 8480
LN_EPS = 1e-5
RMS_EPS = 1e-5
ALPHA = (2 * DEPTH) ** 0.25
ADAM_LR = 0.001
ADAM_B1 = 0.9
ADAM_B2 = 0.999
ADAM_EPS = 1e-08
ADAM_WD = 0.01
ADAM_STEP = 10

_PACK = (("gates", 6432, 2048), ("z", 1280, 2048), ("q", 0, 1024), ("xs", 3328, 2048), ("B", 5376, 512),
         ("C", 5888, 512), ("k", 1024, 128), ("v", 1152, 128), ("dt", 6400, 32))
NP = 8704
_OFF = {}
_o = 0
for _n, _, _w in _PACK:
    _OFF[_n] = _o
    _o += _w
_PAD = NP - _o

VMEM_LIMIT_BYTES = 56 * 1024 * 1024
BIG = ("w_in", "w_ssd_out", "w_att_out", "w_mix_out", "w_ffn_gate", "w_ffn_up", "w_ffn_down")
WEIGHTS = ("ln_in_g", "ln_in_b", "w_in", "conv_w", "conv_b", "dt_bias", "a_log", "d_skip", "ssd_norm_w", "att_sinks",
           "w_ssd_out", "w_att_out", "w_mix_out", "ln_mix_g", "ln_mix_b", "w_ffn_gate", "w_ffn_up", "w_ffn_down",
           "ln_ffn_g", "ln_ffn_b")
SMALL = tuple(n for n in WEIGHTS if n not in BIG)


def _params(n_grid):
    return pltpu.CompilerParams(dimension_semantics=("arbitrary",) * n_grid, vmem_limit_bytes=VMEM_LIMIT_BYTES)


def _dot(a, b, ca, cb):
    return lax.dot_general(a.astype(BF), b.astype(BF), (((ca,), (cb,)), ((), ())), preferred_element_type=F32)


@jax.custom_vjp
def _nn(a, b):
    return _dot(a, b, 1, 0)


def _nn_f(a, b):
    return _dot(a, b, 1, 0), (a, b)


def _nn_b(res, g):
    a, b = res
    return _dot(g, b, 1, 1).astype(a.dtype), _dot(a, g, 0, 0).astype(b.dtype)


_nn.defvjp(_nn_f, _nn_b)


@jax.custom_vjp
def _nt(a, b):
    return _dot(a, b, 1, 1)


def _nt_f(a, b):
    return _dot(a, b, 1, 1), (a, b)


def _nt_b(res, g):
    a, b = res
    return _dot(g, b, 1, 0).astype(a.dtype), _dot(g, a, 0, 0).astype(b.dtype)


_nt.defvjp(_nt_f, _nt_b)


@jax.custom_vjp
def _tn(a, b):
    return _dot(a, b, 0, 0)


def _tn_f(a, b):
    return _dot(a, b, 0, 0), (a, b)


def _tn_b(res, g):
    a, b = res
    return _dot(b, g, 1, 1).astype(a.dtype), _dot(a, g, 1, 0).astype(b.dtype)


_tn.defvjp(_tn_f, _tn_b)


def _sigmoid(x):
    return 0.5 * jnp.tanh(0.5 * x) + 0.5


def _silu(x):
    return x * _sigmoid(x)


def _layer_norm(s, g, b):
    mu = jnp.mean(s, axis=-1, keepdims=True)
    sc = s - mu
    var = jnp.mean(sc * sc, axis=-1, keepdims=True)
    return sc * lax.rsqrt(var + LN_EPS) * g + b


def _ssd_group(pre_x, pre_b, pre_c, z, dtr, dtb, alog, dsk, nw, state):
    t = pre_x.shape[0]
    xs, bm, cm = _silu(pre_x), _silu(pre_b), _silu(pre_c)
    dt = jax.nn.softplus(dtr + dtb)
    da = dt * (-jnp.exp(alog))
    row = lax.broadcasted_iota(jnp.int32, (t, t), 0)
    col = lax.broadcasted_iota(jnp.int32, (t, t), 1)
    upper = (row <= col).astype(F32)
    cum = jnp.dot(da, upper, precision=lax.Precision.HIGHEST, preferred_element_type=F32)
    tot = jnp.sum(da, axis=1, keepdims=True)
    cb = _nt(cm, bm)
    tril = row >= col
    first = lax.broadcasted_iota(jnp.int32, (t, BLK), 1) < 64
    first_row = lax.broadcasted_iota(jnp.int32, (1, BLK), 1) < 64

    def col_form(r):
        return jnp.broadcast_to(r, (t, t)).T

    ys, new_state = [], []
    for p in range(4):
        h0, h1 = 2 * p, 2 * p + 1
        sl = slice(BLK * p, BLK * (p + 1))
        x_pair = xs[:, sl] * jnp.where(first, col_form(dt[h0:h0 + 1]), col_form(dt[h1:h1 + 1]))
        y_pair = None
        cc = []
        for h, keep in ((h0, first), (h1, jnp.logical_not(first))):
            cr = jnp.broadcast_to(cum[h:h + 1], (t, t))
            cc.append(cr.T)
            decay = jnp.exp(jnp.where(tril, cc[-1] - cr, -1e30))
            y_h = _nn(decay * cb, jnp.where(keep, x_pair, 0.0))
            y_pair = y_h if y_pair is None else y_pair + y_h
        s_pair = state[:, sl]
        y_pair = y_pair + _nn(cm, s_pair) * jnp.where(first, jnp.exp(cc[0]), jnp.exp(cc[1]))
        to_end = jnp.where(first, jnp.exp(tot[h0:h0 + 1] - cc[0]), jnp.exp(tot[h1:h1 + 1] - cc[1]))
        chunk_decay = jnp.where(first_row, jnp.exp(tot[h0:h0 + 1]), jnp.exp(tot[h1:h1 + 1]))
        new_state.append(s_pair * chunk_decay + _tn(bm, x_pair * to_end))
        ys.append(y_pair + dsk[:, sl] * xs[:, sl])
    y = jnp.concatenate(ys, axis=1) * _silu(z)
    y = y * lax.rsqrt(jnp.mean(y * y, axis=-1, keepdims=True) + RMS_EPS) * nw
    return y, jnp.concatenate(new_state, axis=1)


def _attn_block(q, kp, kc, vp, vc, sinks, n, kv):
    t = q.shape[0]
    kb = jnp.concatenate([kp, kc], axis=0)
    vb = jnp.concatenate([vp, vc], axis=0)
    qi = lax.broadcasted_iota(jnp.int32, (t, 2 * t), 0)
    kj = lax.broadcasted_iota(jnp.int32, (t, 2 * t), 1)
    rel = qi + t - kj
    valid = (rel >= 0) & (rel < t) & ((n * t - t + kj) >= 0)
    relf = rel.astype(F32)
    first = lax.broadcasted_iota(jnp.int32, (t, BLK), 1) < 64
    lane16 = lax.broadcasted_iota(jnp.int32, (1, ATT_HEADS), 1)
    outs = []
    for p in range(4):
        qp = q[:, BLK * p:BLK * (p + 1)] * (ATT_HEAD_DIM ** -0.5)
        o_pair = None
        for half, keep in enumerate((first, jnp.logical_not(first))):
            h = kv * 8 + 2 * p + half
            s = _nt(jnp.where(keep, qp, 0.0), kb)
            slope = jnp.exp((-8.0 * math.log(2.0) / ATT_HEADS) * jnp.asarray(h + 1, F32))
            s = jnp.where(valid, s - slope * relf, -1e30)
            sink = jnp.sum(jnp.where(lane16 == h, sinks, 0.0), axis=1, keepdims=True)
            m = lax.stop_gradient(jnp.maximum(jnp.max(s, axis=1, keepdims=True), sink))
            e = jnp.exp(s - m)
            den = jnp.sum(e, axis=1, keepdims=True) + jnp.exp(sink - m)
            o = _nn(e * (1.0 / den), vb)
            o_pair = o if o_pair is None else jnp.where(first, o_pair, o)
        outs.append(o_pair)
    return jnp.concatenate(outs, axis=1)


def _chip_exchange(src, out, send_sems, recv_sems, bcast):
    x, y, c = lax.axis_index("x"), lax.axis_index("y"), lax.axis_index("c")
    me = 2 * x + y

    def copy(a, d, slot):
        j = jnp.bitwise_xor(me, d)
        return pltpu.make_async_remote_copy(
            src_ref=src[a] if bcast else src[a].at[j], dst_ref=out[a].at[slot],
            send_sem=send_sems.at[a * 4 + d], recv_sem=recv_sems.at[a * 4 + d],
            device_id=(j // 2, j % 2, c), device_id_type=pl.DeviceIdType.MESH)

    pairs = [(a, d) for a in range(len(src)) for d in range(1, 4)]

    def start():
        for a, d in pairs:
            copy(a, d, me).start()

    def wait():
        for a, d in pairs:
            copy(a, d, jnp.bitwise_xor(me, d)).wait_recv()
        for a, d in pairs:
            copy(a, d, me).wait_send()

    return start, wait


def _mm(a, b, mode, out_dtype, tm, tn, tk, name, hosted=None):
    if mode == "nn":
        (m, k), (k2, n) = a.shape, b.shape
    elif mode == "nt":
        (m, k), (n, k2) = a.shape, b.shape
    else:
        (k, m), (k2, n) = a.shape, b.shape
    assert k == k2, (a.shape, b.shape, mode)
    tm, tn, tk = min(tm, m), min(tn, n), min(tk, k)
    assert m % tm == 0 and n % tn == 0 and k % tk == 0, (m, n, k, tm, tn, tk)
    nk = k // tk
    grid = (m // tm, n // tn, nk)
    ca, cb = {"nn": (1, 0), "nt": (1, 1), "tn": (0, 0)}[mode]
    h_arrays, h_bcast = hosted if hosted else ((), False)
    nh = len(h_arrays)

    def body(*refs):
        a_ref, b_ref = refs[:2]
        o_ref = refs[2 + nh]
        scratch = refs[3 + 2 * nh:]
        step = [pl.program_id(ax) for ax in range(3)]
        if nh:
            start, wait = _chip_exchange(refs[2:2 + nh], refs[3 + nh:3 + 2 * nh], scratch[-2], scratch[-1], h_bcast)

            @pl.when((step[0] == 0) & (step[1] == 0) & (step[2] == 0))
            def _():
                start()

        part = _dot(a_ref[...], b_ref[...], ca, cb)
        if nk == 1:
            o_ref[...] = part.astype(o_ref.dtype)
        else:
            acc_ref = scratch[0]

            @pl.when(step[2] == 0)
            def _():
                acc_ref[...] = part

            @pl.when(step[2] > 0)
            def _():
                acc_ref[...] += part

            @pl.when(step[2] == nk - 1)
            def _():
                o_ref[...] = acc_ref[...].astype(o_ref.dtype)

        if nh:
            @pl.when((step[0] == grid[0] - 1) & (step[1] == grid[1] - 1) & (step[2] == nk - 1))
            def _():
                wait()

    a_spec = pl.BlockSpec((tk, tm), lambda i, j, kk: (kk, i)) if mode == "tn" else pl.BlockSpec((tm, tk), lambda i, j, kk: (i, kk))
    b_spec = pl.BlockSpec((tn, tk), lambda i, j, kk: (j, kk)) if mode == "nt" else pl.BlockSpec((tk, tn), lambda i, j, kk: (kk, j))
    any_spec = pl.BlockSpec(memory_space=pl.ANY)
    res = pl.pallas_call(
        body, name=name, grid=grid,
        in_specs=[a_spec, b_spec] + [any_spec] * nh,
        out_specs=[pl.BlockSpec((tm, tn), lambda i, j, kk: (i, j))] + [any_spec] * nh,
        out_shape=[jax.ShapeDtypeStruct((m, n), out_dtype)]
        + [jax.ShapeDtypeStruct(((4,) + s.shape) if h_bcast else s.shape, s.dtype) for s in h_arrays],
        scratch_shapes=([] if nk == 1 else [pltpu.VMEM((tm, tn), F32)])
        + ([pltpu.SemaphoreType.DMA((4 * nh,)), pltpu.SemaphoreType.DMA((4 * nh,))] if nh else []),
        compiler_params=_params(3),
    )(a, b, *h_arrays)
    return (res[0], list(res[1:])) if nh else res[0]


def _rows(fn, rows, params, outs, accs, tile, name):
    length = rows[0][0].shape[0]
    tile = min(tile, length)
    assert length % tile == 0
    nr, npar, no = len(rows), len(params), len(outs)

    def body(*refs):
        vals = [r[...] for r in refs[:nr + npar]]
        o, a = fn(*vals)
        for ref, val in zip(refs[nr + npar:nr + npar + no], o):
            ref[...] = val.astype(ref.dtype)
        i = pl.program_id(0)
        for ref, val in zip(refs[nr + npar + no:], a):
            @pl.when(i == 0)
            def _(ref=ref, val=val):
                ref[...] = val

            @pl.when(i > 0)
            def _(ref=ref, val=val):
                ref[...] += val

    in_specs = [pl.BlockSpec((tile, w), functools.partial(lambda i, cb: (i, cb), cb=cb)) for _, cb, w in rows]
    in_specs += [pl.BlockSpec(p.shape, lambda i: (0, 0)) for p in params]
    outs = [o if len(o) == 3 else (o[0], o[1], o[0]) for o in outs]
    out_specs = [pl.BlockSpec((tile, w), lambda i: (i, 0)) for w, _, _ in outs]
    out_specs += [pl.BlockSpec((r, w), lambda i: (0, 0)) for r, w in accs]
    out_shape = [jax.ShapeDtypeStruct((length, full), dt) for _, dt, full in outs]
    out_shape += [jax.ShapeDtypeStruct((r, w), F32) for r, w in accs]
    res = pl.pallas_call(
        body, name=name, grid=(length // tile,), in_specs=in_specs, out_specs=out_specs, out_shape=out_shape,
        compiler_params=_params(1),
    )(*[r[0] for r in rows], *params)
    return res[:no], res[no:]


CONV_SUB = 32


def _conv_fwd(proj, conv_w, conv_b, name, hosted=None):
    length = proj.shape[0]
    tl, cw = min(512, length), 512
    cb0 = _OFF["xs"] // cw
    grid = (CONV_DIM // cw, length // tl)
    h_arrays, h_bcast = hosted if hosted else ((), False)
    nh = len(h_arrays)

    def body(*refs):
        u_ref, halo_ref, w_ref, b_ref = refs[:4]
        o_ref = refs[4 + nh]
        win = refs[5 + 2 * nh]
        j, i = pl.program_id(0), pl.program_id(1)
        if nh:
            start, wait = _chip_exchange(refs[4:4 + nh], refs[5 + nh:5 + 2 * nh], refs[-2], refs[-1], h_bcast)

            @pl.when((j == 0) & (i == 0))
            def _():
                start()

            @pl.when((j == grid[0] - 1) & (i == grid[1] - 1))
            def _():
                wait()

        win[0:8, :] = jnp.where(i > 0, halo_ref[...], 0.0)
        win[8:8 + tl, :] = u_ref[...]
        taps = [w_ref[kk:kk + 1, :] for kk in range(SSD_CONV)]
        bias = b_ref[...]
        for r in range(0, tl, CONV_SUB):
            acc = bias + taps[0] * win[5 + r:5 + r + CONV_SUB, :]
            for kk in range(1, SSD_CONV):
                acc = acc + taps[kk] * win[5 + kk + r:5 + kk + r + CONV_SUB, :]
            o_ref[r:r + CONV_SUB, :] = acc

    any_spec = pl.BlockSpec(memory_space=pl.ANY)
    res = pl.pallas_call(
        body, name=name, grid=grid,
        in_specs=[pl.BlockSpec((tl, cw), lambda j, i: (i, cb0 + j)),
                  pl.BlockSpec((8, cw), lambda j, i: (jnp.maximum(i * (tl // 8) - 1, 0), cb0 + j)),
                  pl.BlockSpec((SSD_CONV, cw), lambda j, i: (0, j)),
                  pl.BlockSpec((1, cw), lambda j, i: (0, j))] + [any_spec] * nh,
        out_specs=[pl.BlockSpec((tl, cw), lambda j, i: (i, j))] + [any_spec] * nh,
        out_shape=[jax.ShapeDtypeStruct((length, CONV_DIM), F32)]
        + [jax.ShapeDtypeStruct(((4,) + s.shape) if h_bcast else s.shape, s.dtype) for s in h_arrays],
        scratch_shapes=[pltpu.VMEM((8 + tl, cw), F32)]
        + ([pltpu.SemaphoreType.DMA((4 * nh,)), pltpu.SemaphoreType.DMA((4 * nh,))] if nh else []),
        compiler_params=_params(2),
    )(proj, proj, conv_w, conv_b, *h_arrays)
    return (res[0], list(res[1:])) if nh else res[0]


def _conv_bwd(dpre, proj, conv_w, dproj, part, name):
    length = proj.shape[0]
    width = dpre.shape[1]
    tl, cw = min(512, length), 512
    cb0 = _OFF["xs"] // cw + part
    n_t = length // tl

    def body(d_ref, dnext_ref, u_ref, halo_ref, w_ref, _, du_ref, dw_ref, db_ref, dwin, uwin):
        i = pl.program_id(1)
        dwin[0:tl, :] = d_ref[...]
        dwin[tl:tl + 8, :] = jnp.where(i < n_t - 1, dnext_ref[...], 0.0)
        uwin[0:8, :] = jnp.where(i > 0, halo_ref[...], 0.0)
        uwin[8:8 + tl, :] = u_ref[...]
        taps = [w_ref[kk:kk + 1, :] for kk in range(SSD_CONV)]
        sub = CONV_SUB // 2
        acc_w = [jnp.zeros((sub, cw), F32) for _ in range(SSD_CONV)]
        acc_b = jnp.zeros((sub, cw), F32)
        for r in range(0, tl, sub):
            d = dwin[r:r + sub, :]
            du = taps[3] * d
            for kk in range(SSD_CONV - 1):
                du = du + taps[kk] * dwin[3 - kk + r:3 - kk + r + sub, :]
            du_ref[r:r + sub, :] = du.astype(du_ref.dtype)
            for kk in range(SSD_CONV):
                acc_w[kk] = acc_w[kk] + d * uwin[5 + kk + r:5 + kk + r + sub, :]
            acc_b = acc_b + d
        dw = jnp.concatenate([jnp.sum(a, axis=0, keepdims=True) for a in acc_w], axis=0)
        db = jnp.sum(acc_b, axis=0, keepdims=True)

        @pl.when(i == 0)
        def _():
            dw_ref[...] = dw
            db_ref[...] = db

        @pl.when(i > 0)
        def _():
            dw_ref[...] += dw
            db_ref[...] += db

    return pl.pallas_call(
        body, name=name, grid=(width // cw, n_t), input_output_aliases={5: 0},
        in_specs=[pl.BlockSpec((tl, cw), lambda j, i: (i, j)),
                  pl.BlockSpec((8, cw), lambda j, i: (jnp.minimum((i + 1) * (tl // 8), length // 8 - 1), j)),
                  pl.BlockSpec((tl, cw), lambda j, i: (i, cb0 + j)),
                  pl.BlockSpec((8, cw), lambda j, i: (jnp.maximum(i * (tl // 8) - 1, 0), cb0 + j)),
                  pl.BlockSpec((SSD_CONV, cw), lambda j, i: (0, part + j)),
                  pl.BlockSpec(memory_space=pl.ANY)],
        out_specs=[pl.BlockSpec((tl, cw), lambda j, i: (i, cb0 + j)),
                   pl.BlockSpec((SSD_CONV, cw), lambda j, i: (0, j)),
                   pl.BlockSpec((1, cw), lambda j, i: (0, j))],
        out_shape=[jax.ShapeDtypeStruct((length, NP), BF),
                   jax.ShapeDtypeStruct((SSD_CONV, width), F32),
                   jax.ShapeDtypeStruct((1, width), F32)],
        scratch_shapes=[pltpu.VMEM((tl + 8, cw), F32), pltpu.VMEM((8 + tl, cw), F32)],
        compiler_params=_params(2),
    )(dpre, dpre, proj, proj, conv_w, dproj)


def _ssd_in_specs(rev, nc):
    def cidx(c):
        return nc - 1 - c if rev else c
    whole = lambda c: (0, 0)
    return [pl.BlockSpec((BLK, SSD_D_INNER), lambda c: (cidx(c), 0)),
            pl.BlockSpec((BLK, BC_DIM), lambda c: (cidx(c), SSD_D_INNER // BC_DIM)),
            pl.BlockSpec((BLK, BC_DIM), lambda c: (cidx(c), SSD_D_INNER // BC_DIM + 1)),
            pl.BlockSpec((BLK, SSD_D_INNER), lambda c: (cidx(c), _OFF["z"] // SSD_D_INNER)),
            pl.BlockSpec((SSD_HEADS, BLK), lambda c: (0, cidx(c))),
            pl.BlockSpec((SSD_HEADS, 1), whole),
            pl.BlockSpec((SSD_HEADS, 1), whole),
            pl.BlockSpec((1, SSD_D_INNER), whole),
            pl.BlockSpec((1, SSD_D_INNER), whole)]


def _group_args(g, px, pb, pc, z, dtr, dtb, alog, dsk, nw):
    wide, narrow, heads = slice(512 * g, 512 * (g + 1)), slice(BLK * g, BLK * (g + 1)), slice(8 * g, 8 * (g + 1))
    return (px[:, wide], pb[:, narrow], pc[:, narrow], z[:, wide], dtr[heads, :], dtb[heads, :], alog[heads, :],
            dsk[:, wide], nw[:, wide])


def _ssd_fwd(pre, proj, dt_t, dtb, alog, dsk, nw, name, hosted=None):
    length = pre.shape[0]
    nc = length // BLK
    h_arrays, h_bcast = hosted if hosted else ((), False)
    nh = len(h_arrays)

    def body(*refs):
        px, pb, pc, z, dtr, dtb_r, al_r, dsk_r, nw_r = refs[:9]
        y_ref, sin_ref = refs[9 + nh:11 + nh]
        st = refs[11 + 2 * nh]
        c = pl.program_id(0)
        if nh:
            start, wait = _chip_exchange(refs[9:9 + nh], refs[11 + nh:11 + 2 * nh], refs[-2], refs[-1], h_bcast)

        @pl.when(c == 0)
        def _():
            st[...] = jnp.zeros_like(st)
            if nh:
                start()

        for g in range(SSD_GROUPS):
            s_in = st[g]
            sin_ref[g] = s_in
            y, s_out = _ssd_group(*_group_args(g, px, pb, pc, z, dtr, dtb_r, al_r, dsk_r, nw_r), s_in)
            y_ref[:, 512 * g:512 * (g + 1)] = y.astype(y_ref.dtype)
            st[g] = s_out

        if nh:
            @pl.when(c == nc - 1)
            def _():
                wait()

    any_spec = pl.BlockSpec(memory_space=pl.ANY)
    res = pl.pallas_call(
        body, name=name, grid=(nc,), in_specs=_ssd_in_specs(False, nc) + [any_spec] * nh,
        out_specs=[pl.BlockSpec((BLK, SSD_D_INNER), lambda c: (c, 0)),
                   pl.BlockSpec((SSD_GROUPS, None, SSD_STATE, 512), lambda c: (0, c, 0, 0))] + [any_spec] * nh,
        out_shape=[jax.ShapeDtypeStruct((length, SSD_D_INNER), BF),
                   jax.ShapeDtypeStruct((SSD_GROUPS, nc, SSD_STATE, 512), F32)]
        + [jax.ShapeDtypeStruct(((4,) + s.shape) if h_bcast else s.shape, s.dtype) for s in h_arrays],
        scratch_shapes=[pltpu.VMEM((SSD_GROUPS, SSD_STATE, 512), F32)]
        + ([pltpu.SemaphoreType.DMA((4 * nh,)), pltpu.SemaphoreType.DMA((4 * nh,))] if nh else []),
        compiler_params=_params(1),
    )(pre, pre, pre, proj, dt_t, dtb, alog, dsk, nw, *h_arrays)
    return (res[0], res[1], list(res[2:])) if nh else (res[0], res[1])


def _ssd_bwd(pre, proj, dt_t, dtb, alog, dsk, nw, s_in, dy, dproj, name, hosted=None):
    length = pre.shape[0]
    nc = length // BLK
    h_arrays, h_bcast = hosted if hosted else ((), False)
    nh = len(h_arrays)

    def body(*refs):
        px, pb, pc, z, dtr, dtb_r, al_r, dsk_r, nw_r, sin_r, dy_r = refs[:11]
        dpx, dpb, dpc, dz, ddt, ddtb, dal, ddsk, dnw = refs[12 + nh:21 + nh]
        dst = refs[21 + 2 * nh]
        c = pl.program_id(0)
        if nh:
            start, wait = _chip_exchange(refs[12:12 + nh], refs[21 + nh:21 + 2 * nh], refs[-2], refs[-1], h_bcast)

        @pl.when(c == 0)
        def _():
            dst[...] = jnp.zeros_like(dst)
            if nh:
                start()

        for grp in range(SSD_GROUPS):
            wide, narrow, heads = slice(512 * grp, 512 * (grp + 1)), slice(BLK * grp, BLK * (grp + 1)), slice(8 * grp, 8 * (grp + 1))
            _, vjp = jax.vjp(_ssd_group, *_group_args(grp, px, pb, pc, z, dtr, dtb_r, al_r, dsk_r, nw_r), sin_r[grp])
            g = vjp((dy_r[:, wide], dst[grp]))
            dpx[:, wide] = g[0]
            dpb[:, narrow] = g[1]
            dpc[:, narrow] = g[2]
            dz[:, wide] = g[3].astype(dz.dtype)
            ddt[heads, :] = g[4]
            dst[grp] = g[9]
            for ref, val, idx in ((ddtb, g[5], (heads, slice(None))), (dal, g[6], (heads, slice(None))),
                                  (ddsk, g[7], (slice(None), wide)), (dnw, g[8], (slice(None), wide))):
                @pl.when(c == 0)
                def _(ref=ref, val=val, idx=idx):
                    ref[idx] = val

                @pl.when(c > 0)
                def _(ref=ref, val=val, idx=idx):
                    ref[idx] += val

        if nh:
            @pl.when(c == nc - 1)
            def _():
                wait()

    rev = lambda c: nc - 1 - c
    whole = lambda c: (0, 0)
    any_spec = pl.BlockSpec(memory_space=pl.ANY)
    in_specs = _ssd_in_specs(True, nc) + [
        pl.BlockSpec((SSD_GROUPS, None, SSD_STATE, 512), lambda c: (0, rev(c), 0, 0)),
        pl.BlockSpec((BLK, SSD_D_INNER), lambda c: (rev(c), 0)),
        any_spec] + [any_spec] * nh
    res = pl.pallas_call(
        body, name=name, grid=(nc,), in_specs=in_specs, input_output_aliases={11: 3},
        out_specs=[pl.BlockSpec((BLK, SSD_D_INNER), lambda c: (rev(c), 0)),
                   pl.BlockSpec((BLK, BC_DIM), lambda c: (rev(c), 0)),
                   pl.BlockSpec((BLK, BC_DIM), lambda c: (rev(c), 0)),
                   pl.BlockSpec((BLK, SSD_D_INNER), lambda c: (rev(c), _OFF["z"] // SSD_D_INNER)),
                   pl.BlockSpec((SSD_HEADS, BLK), lambda c: (0, rev(c))),
                   pl.BlockSpec((SSD_HEADS, 1), whole),
                   pl.BlockSpec((SSD_HEADS, 1), whole),
                   pl.BlockSpec((1, SSD_D_INNER), whole),
                   pl.BlockSpec((1, SSD_D_INNER), whole)] + [any_spec] * nh,
        out_shape=[jax.ShapeDtypeStruct((length, SSD_D_INNER), F32),
                   jax.ShapeDtypeStruct((length, BC_DIM), F32),
                   jax.ShapeDtypeStruct((length, BC_DIM), F32),
                   jax.ShapeDtypeStruct((length, NP), BF),
                   jax.ShapeDtypeStruct((SSD_HEADS, length), F32),
                   jax.ShapeDtypeStruct((SSD_HEADS, 1), F32),
                   jax.ShapeDtypeStruct((SSD_HEADS, 1), F32),
                   jax.ShapeDtypeStruct((1, SSD_D_INNER), F32),
                   jax.ShapeDtypeStruct((1, SSD_D_INNER), F32)]
        + [jax.ShapeDtypeStruct(((4,) + s.shape) if h_bcast else s.shape, s.dtype) for s in h_arrays],
        scratch_shapes=[pltpu.VMEM((SSD_GROUPS, SSD_STATE, 512), F32)]
        + ([pltpu.SemaphoreType.DMA((4 * nh,)), pltpu.SemaphoreType.DMA((4 * nh,))] if nh else []),
        compiler_params=_params(1),
    )(pre, pre, pre, proj, dt_t, dtb, alog, dsk, nw, s_in, dy, dproj, *h_arrays)
    return tuple(res[:9]) + ((list(res[9:]),) if nh else ())


assert _OFF["gates"] == 0 and _OFF["z"] % SSD_D_INNER == 0 and _OFF["q"] % D_MODEL == 0 and _OFF["k"] + 512 == NP


def _attn_in_specs():
    prev = lambda n: (0, jnp.maximum(n - 1, 0), 0)
    cur = lambda n: (0, n, 0)
    return [pl.BlockSpec((BLK, D_MODEL), lambda n: (n, _OFF["q"] // D_MODEL)),
            pl.BlockSpec((2, BLK, BLK), prev),
            pl.BlockSpec((2, BLK, BLK), cur),
            pl.BlockSpec((2, BLK, BLK), prev),
            pl.BlockSpec((2, BLK, BLK), cur),
            pl.BlockSpec((1, ATT_HEADS), lambda n: (0, 0))]


def _attn_fwd(proj, k2, v2, sinks, name, hosted=None):
    length = proj.shape[0]
    nb = length // BLK
    h_arrays, h_bcast = hosted if hosted else ((), False)
    nh = len(h_arrays)

    def body(*refs):
        q, kp, kc, vp, vc, sk = refs[:6]
        o_ref = refs[6 + nh]
        n = pl.program_id(0)
        if nh:
            start, wait = _chip_exchange(refs[6:6 + nh], refs[7 + nh:7 + 2 * nh], refs[-2], refs[-1], h_bcast)

            @pl.when(n == 0)
            def _():
                start()

        for kv in range(2):
            cols = slice(512 * kv, 512 * (kv + 1))
            o = _attn_block(q[:, cols], kp[kv], kc[kv], vp[kv], vc[kv], sk[...], n, kv)
            o_ref[:, cols] = o.astype(o_ref.dtype)

        if nh:
            @pl.when(n == nb - 1)
            def _():
                wait()

    any_spec = pl.BlockSpec(memory_space=pl.ANY)
    res = pl.pallas_call(
        body, name=name, grid=(nb,), in_specs=_attn_in_specs() + [any_spec] * nh,
        out_specs=[pl.BlockSpec((BLK, D_MODEL), lambda n: (n, 0))] + [any_spec] * nh,
        out_shape=[jax.ShapeDtypeStruct((length, D_MODEL), BF)]
        + [jax.ShapeDtypeStruct(((4,) + s.shape) if h_bcast else s.shape, s.dtype) for s in h_arrays],
        scratch_shapes=[pltpu.SemaphoreType.DMA((4 * nh,)), pltpu.SemaphoreType.DMA((4 * nh,))] if nh else [],
        compiler_params=_params(1),
    )(proj, k2, k2, v2, v2, sinks, *h_arrays)
    return (res[0], list(res[1:])) if nh else res[0]


def _attn_bwd(proj, k2, v2, sinks, datt, dproj, name):
    length = proj.shape[0]

    def body(q, kp, kc, vp, vc, sk, do, _, dq, dkp, dkc, dvp, dvc, dsk):
        n = pl.program_id(0)
        dsinks = None
        for kv in range(2):
            cols = slice(512 * kv, 512 * (kv + 1))
            _, vjp = jax.vjp(lambda *a: _attn_block(*a, n, kv), q[:, cols], kp[kv], kc[kv], vp[kv], vc[kv], sk[...])
            g = vjp(do[:, cols].astype(F32))
            dq[:, cols] = g[0].astype(dq.dtype)
            dkp[kv] = g[1]
            dkc[kv] = g[2]
            dvp[kv] = g[3]
            dvc[kv] = g[4]
            dsinks = g[5] if dsinks is None else dsinks + g[5]

        @pl.when(n == 0)
        def _():
            dsk[...] = dsinks

        @pl.when(n > 0)
        def _():
            dsk[...] += dsinks

    blk3 = pl.BlockSpec((2, BLK, BLK), lambda n: (0, n, 0))
    kv_shape = jax.ShapeDtypeStruct((2, length, BLK), F32)
    return pl.pallas_call(
        body, name=name, grid=(length // BLK,),
        in_specs=_attn_in_specs() + [pl.BlockSpec((BLK, D_MODEL), lambda n: (n, 0)), pl.BlockSpec(memory_space=pl.ANY)],
        out_specs=[pl.BlockSpec((BLK, D_MODEL), lambda n: (n, _OFF["q"] // D_MODEL)), blk3, blk3, blk3, blk3,
                   pl.BlockSpec((1, ATT_HEADS), lambda n: (0, 0))],
        out_shape=[jax.ShapeDtypeStruct((length, NP), BF), kv_shape, kv_shape, kv_shape, kv_shape,
                   jax.ShapeDtypeStruct((1, ATT_HEADS), F32)],
        input_output_aliases={7: 0},
        compiler_params=_params(1),
    )(proj, k2, k2, v2, v2, sinks, datt, dproj)


DMA_CHUNK_BYTES = 1 << 20
N_STAGE = 8
LOOKAHEAD = 3


def _piece_chunks(shape, itemsize):
    if len(shape) < 2 or shape[-2] % 16 != 0:
        return [()]
    rows, cols = shape[-2:]
    step = min(rows, max(16, DMA_CHUNK_BYTES // (cols * itemsize) // 16 * 16))
    out = []
    for lead in itertools.product(*[range(d) for d in shape[:-2]]):
        for r0 in range(0, rows, step):
            out.append(lead + (pl.ds(r0, min(step, rows - r0)),))
    return out


def _exchange(srcs, group, bcast, name):
    size = {"c": 2, "xy": 4, "all": 8}[group]
    npeer = size - 1
    n = len(srcs)
    for s in srcs:
        assert bcast or s.shape[0] == size
    pieces = [s.shape if bcast else s.shape[1:] for s in srcs]
    chunks = [_piece_chunks(p, s.dtype.itemsize) for p, s in zip(pieces, srcs)]

    def body(*refs):
        src, out = refs[:n], refs[n:2 * n]
        load_sems, send_sems, recv_sems = refs[2 * n:]
        x, y, c = lax.axis_index("x"), lax.axis_index("y"), lax.axis_index("c")
        if group == "c":
            me = c
        elif group == "xy":
            me = 2 * x + y
        else:
            me = 4 * x + 2 * y + c

        def device(j):
            if group == "c":
                return (x, y, j)
            if group == "xy":
                return (j // 2, j % 2, c)
            return (j // 4, (j // 2) % 2, j % 2)

        def part(ref, idx):
            return ref.at[idx] if idx else ref

        def piece(a, j):
            return src[a] if bcast else src[a].at[j]

        for a in range(n):
            jobs = [(None, idx) for idx in chunks[a]] if bcast else [(d, idx) for idx in chunks[a] for d in range(1, size)]
            full_rows = chunks[a][0][-1].size if chunks[a][0] else None
            slot_shape = (full_rows, pieces[a][-1]) if chunks[a][0] else pieces[a]

            def stream(stage, a=a, jobs=jobs, full_rows=full_rows):
                def slot(q):
                    idx = jobs[q][1]
                    view = stage.at[q % N_STAGE]
                    if idx and idx[-1].size != full_rows:
                        view = view.at[pl.ds(0, idx[-1].size)]
                    return view

                def load(q):
                    d, idx = jobs[q]
                    j = me if d is None else jnp.bitwise_xor(me, d)
                    return pltpu.make_async_copy(part(piece(a, j), idx), slot(q), load_sems.at[q % N_STAGE])

                def sends(q):
                    d, idx = jobs[q]
                    return [pltpu.make_async_remote_copy(
                        src_ref=slot(q), dst_ref=part(out[a].at[me], idx),
                        send_sem=send_sems.at[(q % N_STAGE) * npeer + dd - 1], recv_sem=recv_sems.at[a * size + dd],
                        device_id=device(jnp.bitwise_xor(me, dd)), device_id_type=pl.DeviceIdType.MESH)
                        for dd in (range(1, size) if d is None else (d,))]

                nq = len(jobs)
                for q in range(nq + LOOKAHEAD):
                    if q < nq:
                        if q >= N_STAGE:
                            for cp in sends(q - N_STAGE):
                                cp.wait_send()
                        load(q).start()
                    if q >= LOOKAHEAD:
                        load(q - LOOKAHEAD).wait()
                        for cp in sends(q - LOOKAHEAD):
                            cp.start()
                for q in range(max(0, nq - N_STAGE), nq):
                    for cp in sends(q):
                        cp.wait_send()

            pl.run_scoped(stream, pltpu.VMEM((N_STAGE,) + tuple(slot_shape), srcs[a].dtype))

        for a in range(n):
            for d in range(1, size):
                j = jnp.bitwise_xor(me, d)
                pltpu.make_async_remote_copy(
                    src_ref=piece(a, j), dst_ref=out[a].at[j], send_sem=send_sems.at[0], recv_sem=recv_sems.at[a * size + d],
                    device_id=device(j), device_id_type=pl.DeviceIdType.MESH).wait_recv()

    any_spec = pl.BlockSpec(memory_space=pl.ANY)
    return pl.pallas_call(
        body, name=name, in_specs=[any_spec] * n, out_specs=[any_spec] * n,
        out_shape=[jax.ShapeDtypeStruct((size,) + tuple(p), s.dtype) for p, s in zip(pieces, srcs)],
        scratch_shapes=[pltpu.SemaphoreType.DMA((N_STAGE,)), pltpu.SemaphoreType.DMA((N_STAGE * npeer,)),
                        pltpu.SemaphoreType.DMA((n * size,))],
        compiler_params=pltpu.CompilerParams(vmem_limit_bytes=VMEM_LIMIT_BYTES),
    )(*srcs)


def _with_own(outs, owns, me):
    return [lax.dynamic_update_index_in_dim(o, w, me, 0) for o, w in zip(outs, owns)]


def _sum_slots(arr, name):
    k = arr.shape[0]
    rest = arr.shape[1:]
    width = rest[-1]
    rows_per = math.prod(rest[:-1])
    a2 = arr.reshape(k * rows_per, width)
    tile = rows_per
    for cand in (256, 128, 64, 32, 16, 8):
        if rows_per % cand == 0:
            tile = cand
            break
    nt = rows_per // tile

    def body(*refs):
        acc = refs[0][...]
        for r in refs[1:k]:
            acc = acc + r[...]
        refs[k][...] = acc

    in_specs = [pl.BlockSpec((tile, width), functools.partial(lambda i, s: (s * nt + i, 0), s=s)) for s in range(k)]
    out = pl.pallas_call(
        body, name=name, grid=(nt,), in_specs=in_specs, out_specs=pl.BlockSpec((tile, width), lambda i: (i, 0)),
        out_shape=jax.ShapeDtypeStruct((rows_per, width), arr.dtype), compiler_params=_params(1),
    )(*([a2] * k))
    return out.reshape(rest)


def _sum_pieces(recv, own, own_slotted, me, name):
    k = recv.shape[0]
    rest = recv.shape[1:]
    width = rest[-1]
    rows_per = math.prod(rest[:-1])
    r2 = recv.reshape(k * rows_per, width)
    o2 = own.reshape(-1, width)
    tile = rows_per
    for cand in (256, 128, 64, 32, 16, 8):
        if rows_per % cand == 0:
            tile = cand
            break
    nt = rows_per // tile

    def body(me_ref, *refs):
        mine = refs[k][...].astype(F32)
        acc = None
        for j in range(k):
            term = jnp.where(me_ref[0] == j, mine, refs[j][...].astype(F32))
            acc = term if acc is None else acc + term
        refs[k + 1][...] = acc
        refs[k + 2][...] = acc.astype(BF)

    in_specs = [pl.BlockSpec((tile, width), functools.partial(lambda i, m, s: (s * nt + i, 0), s=s)) for s in range(k)]
    in_specs.append(pl.BlockSpec((tile, width), (lambda i, m: (m[0] * nt + i, 0)) if own_slotted else (lambda i, m: (i, 0))))
    out_spec = pl.BlockSpec((tile, width), lambda i, m: (i, 0))
    out, out_bf = pl.pallas_call(
        body, name=name,
        grid_spec=pltpu.PrefetchScalarGridSpec(num_scalar_prefetch=1, grid=(nt,), in_specs=in_specs, out_specs=[out_spec, out_spec]),
        out_shape=[jax.ShapeDtypeStruct((rows_per, width), F32), jax.ShapeDtypeStruct((rows_per, width), BF)],
        compiler_params=_params(1),
    )(jnp.reshape(me, (1,)).astype(jnp.int32), *([r2] * k), o2)
    return out.reshape(rest), out_bf.reshape(rest)


def _adamw(w, g, m, v, name):
    def fn(w_, g_, m_, v_):
        m1 = ADAM_B1 * m_ + (1.0 - ADAM_B1) * g_
        v1 = ADAM_B2 * v_ + (1.0 - ADAM_B2) * (g_ * g_)
        m_hat = m1 / (1.0 - ADAM_B1 ** ADAM_STEP)
        v_hat = v1 / (1.0 - ADAM_B2 ** ADAM_STEP)
        delta = -ADAM_LR * (m_hat / (jnp.sqrt(v_hat) + ADAM_EPS) + ADAM_WD * w_)
        return (delta, m1, v1), ()

    rows, width = w.shape
    tile = rows
    for cand in (256, 128, 64, 32, 16, 8):
        if rows % cand == 0:
            tile = cand
            break
    o, _ = _rows(fn, [(w, 0, width), (g, 0, width), (m, 0, width), (v, 0, width)], [],
                 [(width, F32)] * 3, [], tile, name)
    return o


def _col_pieces(start, width, shard):
    out = []
    while width:
        chip, off = divmod(start, shard)
        take = min(width, shard - off)
        out.append((chip, off, take))
        start, width = start + take, width - take
    return out


def _gather_cols(parts, ranges, pad=0):
    shard = parts.shape[3]
    halves = []
    for h in range(2):
        cols = [parts[h, chip, :, off:off + w] for s, wd in ranges for chip, off, w in _col_pieces(s, wd, shard)]
        if pad:
            cols.append(jnp.zeros((parts.shape[2], pad), parts.dtype))
        halves.append(jnp.concatenate(cols, axis=1))
    return jnp.concatenate(halves, axis=0)


def _scatter_cols(g, ranges, shard):
    starts, pos = {}, 0
    for s, wd in ranges:
        starts[s] = (pos, wd)
        pos += wd
    order = sorted(starts)
    hr = g.shape[0] // 2
    out = []
    for h in range(2):
        per_chip = []
        for chip in range(4):
            cols = []
            for s in order:
                p0, wd = starts[s]
                lo, hi = max(s, chip * shard), min(s + wd, (chip + 1) * shard)
                if lo < hi:
                    cols.append(g[h * hr:(h + 1) * hr, p0 + lo - s:p0 + hi - s])
            per_chip.append(jnp.concatenate(cols, axis=1))
        out.append(jnp.stack(per_chip))
    return jnp.stack(out)


_IN_RANGES = tuple((o, wd) for _, o, wd in _PACK)


def _double_heads(t):
    length = t.shape[0]
    h = jnp.transpose(t.reshape(length, 2, 64), (1, 0, 2))
    return jnp.concatenate([h, h], axis=-1)


def _fold_heads(d_cur, d_prev):
    length = d_cur.shape[1]
    d = d_cur + jnp.concatenate([d_prev[:, BLK:], jnp.zeros((2, BLK, BLK), F32)], axis=1)
    d = d[..., :64] + d[..., 64:]
    return jnp.transpose(d, (1, 0, 2)).reshape(length, 128)


def _ln_fwd(a, r, g, b, name):
    def fn(a_, r_, g_, b_):
        y = _layer_norm(ALPHA * a_ + r_, g_, b_)
        return (y, y), ()
    o, _ = _rows(fn, [(a, 0, D_MODEL), (r, 0, D_MODEL)], [g, b], [(D_MODEL, F32), (D_MODEL, BF)], [], 256, name)
    return o


def _ln_bwd(a, r, d1, d2, g, b, name):
    def fn(a_, r_, d1_, d2_, g_, b_):
        _, vjp = jax.vjp(_layer_norm, ALPHA * a_ + r_, g_, b_)
        ds, dg, db = vjp(ALPHA * d1_ + d2_)
        return (ds, ds), (dg, db)
    o, acc = _rows(fn, [(a, 0, D_MODEL), (r, 0, D_MODEL), (d1, 0, D_MODEL), (d2, 0, D_MODEL)], [g, b],
                   [(D_MODEL, F32), (D_MODEL, BF)], [(1, D_MODEL), (1, D_MODEL)], 256, name)
    return o[0], o[1], acc[0], acc[1]


def _layer_fwd(l, h, hb, w, p, hosted=None, hosted_attn=None, hosted_conv=None, late_weights=None):
    sv = {"h": h, "hb": hb}
    proj = _mm(hb, w["win"], "nn", F32, 2048, 512, 1024, f"proj{l}")
    sv["proj"] = proj
    pre = _conv_fwd(proj, p["conv_w"], p["conv_b"], f"conv{l}", hosted_conv)
    if hosted_conv:
        pre, got = pre
        w.update(late_weights(got))
    sv["pre"] = pre
    dt_t = jnp.transpose(proj[:, _OFF["dt"]:_OFF["dt"] + SSD_HEADS])
    sv["dt_t"] = dt_t
    res = _ssd_fwd(pre, proj, dt_t, p["dt_bias"], p["a_log"], p["d_skip_c"], p["ssd_norm_w"], f"ssd{l}", hosted)
    yn, s_in = res[:2]
    if hosted:
        sv["hosted"] = res[2]
    sv["yn"], sv["s_in"] = yn, s_in
    ya = _mm(yn, w["wso"], "nn", F32, 1024, 1024, 2048, f"ssdout{l}")
    k2 = _double_heads(proj[:, _OFF["k"]:_OFF["k"] + 128])
    v2 = _double_heads(proj[:, _OFF["v"]:_OFF["v"] + 128])
    sv["k2"], sv["v2"] = k2, v2
    att = _attn_fwd(proj, k2, v2, p["att_sinks"], f"attn{l}", hosted_attn)
    if hosted_attn:
        att, sv["hosted_attn"] = att
    sv["att"] = att
    yb = _mm(att, w["wao"], "nn", F32, 1024, 1024, 1024, f"attout{l}")
    sv["ya"], sv["yb"] = ya, yb
    gcb = _OFF["gates"] // 1024

    def gate_fn(ga, gb, ya_, yb_):
        return (_sigmoid(ga) * ya_ + _sigmoid(gb) * yb_,), ()
    (ub,), _ = _rows(gate_fn, [(proj, gcb, 1024), (proj, gcb + 1, 1024), (ya, 0, 1024), (yb, 0, 1024)], [],
                     [(D_MODEL, BF)], [], 256, f"gate{l}")
    sv["ub"] = ub
    mix = _mm(ub, w["wmo"], "nn", F32, 1024, 1024, 1024, f"mixout{l}")
    sv["mix"] = mix
    h1, h1b = _ln_fwd(h, mix, p["ln_mix_g"], p["ln_mix_b"], f"lnmix{l}")
    sv["h1"], sv["h1b"] = h1, h1b
    gu = _mm(h1b, w["wgu"], "nn", F32, 2048, 512, 1024, f"ffnin{l}")
    sv["gu"] = gu

    def act_fn(g_, u_):
        return (_silu(g_) * u_,), ()
    (act,), _ = _rows(act_fn, [(gu, 0, FFN_HIDDEN), (gu, 1, FFN_HIDDEN)], [], [(FFN_HIDDEN, BF)], [], 256, f"swiglu{l}")
    sv["act"] = act
    ffn = _mm(act, w["wd"], "nn", F32, 1024, 1024, FFN_HIDDEN, f"ffnout{l}")
    sv["ffn"] = ffn
    return sv


def _layer_bwd(l, sv, w, p, ds_f, ds_b, hosted=(), early_hook=None, late_hook=None):
    g = {}
    g["w_ffn_down"] = _mm(sv["act"], ds_b, "tn", F32, 1408, 1024, 1024, f"dwd{l}")
    dact = _mm(ds_b, w["wd"], "nt", BF, 1024, FFN_HIDDEN, 1024, f"dact{l}")

    def act_bwd(g_, u_, d_):
        _, vjp = jax.vjp(lambda a, b: _silu(a) * b, g_, u_)
        dg_, du_ = vjp(d_.astype(F32))
        return (jnp.concatenate([dg_, du_], axis=1),), ()
    (dgu,), _ = _rows(act_bwd, [(sv["gu"], 0, FFN_HIDDEN), (sv["gu"], 1, FFN_HIDDEN), (dact, 0, FFN_HIDDEN)], [],
                      [(2 * FFN_HIDDEN, BF)], [], 128, f"dswiglu{l}")
    dwgu = _mm(sv["h1b"], dgu, "tn", F32, 1024, 2816, 512, f"dwgu{l}")
    g["w_ffn_gate"], g["w_ffn_up"] = dwgu[:, :FFN_HIDDEN], dwgu[:, FFN_HIDDEN:]
    dh1 = _mm(dgu, w["wgu"], "nt", F32, 1024, 1024, 2816, f"dh1{l}")
    ds2_f, ds2_b, g["ln_mix_g"], g["ln_mix_b"] = _ln_bwd(sv["h"], sv["mix"], ds_f, dh1, p["ln_mix_g"], p["ln_mix_b"], f"dlnmix{l}")
    g["w_mix_out"] = _mm(sv["ub"], ds2_b, "tn", F32, 1024, 1024, 512, f"dwmo{l}")
    du = _mm(ds2_b, w["wmo"], "nt", BF, 1024, 1024, 1024, f"du{l}")
    proj = sv["proj"]
    gcb = _OFF["gates"] // 1024

    def gate_bwd(ga, gb, ya_, yb_, du_):
        _, vjp = jax.vjp(lambda a, b, c, d: _sigmoid(a) * c + _sigmoid(b) * d, ga, gb, ya_, yb_)
        dga, dgb, dya, dyb = vjp(du_.astype(F32))
        return (jnp.concatenate([dga, dgb], axis=1), dya, dyb), ()
    (dproj, dya, dyb), _ = _rows(
        gate_bwd, [(proj, gcb, 1024), (proj, gcb + 1, 1024), (sv["ya"], 0, 1024), (sv["yb"], 0, 1024), (du, 0, 1024)], [],
        [(2048, BF, NP), (D_MODEL, BF), (D_MODEL, BF)], [], 256, f"dgate{l}")
    g["w_att_out"] = _mm(sv["att"], dyb, "tn", F32, 1024, 1024, 512, f"dwao{l}")
    datt = _mm(dyb, w["wao"], "nt", BF, 1024, 1024, 1024, f"datt{l}")
    g["w_ssd_out"] = _mm(sv["yn"], dya, "tn", F32, 1024, 1024, 512, f"dwso{l}")
    dyn = _mm(dya, w["wso"], "nt", F32, 1024, 1024, 1024, f"dyn{l}")
    hosted = list(hosted) + (list(early_hook(g)) if early_hook else [])
    dproj, dkp, dkc, dvp, dvc, g["att_sinks"] = _attn_bwd(proj, sv["k2"], sv["v2"], p["att_sinks"], datt, dproj, f"dattn{l}")
    res = _ssd_bwd(sv["pre"], proj, sv["dt_t"], p["dt_bias"], p["a_log"], p["d_skip_c"], p["ssd_norm_w"], sv["s_in"], dyn,
                   dproj, f"dssd{l}", (hosted, False) if hosted else None)
    (dpx, dpb, dpc, dproj, ddt_t, g["dt_bias"], g["a_log"], ddsk, g["ssd_norm_w"]) = res[:9]
    if hosted:
        g["hosted"] = res[9]
    g["d_skip"] = jnp.sum(ddsk.reshape(SSD_HEADS, 64), axis=1)
    dws, dbs = [], []
    for nm, dpart, part in (("x", dpx, 0), ("b", dpb, 4), ("c", dpc, 5)):
        dproj, dw_, db_ = _conv_bwd(dpart, proj, p["conv_w"], dproj, part, f"dconv{nm}{l}")
        dws.append(dw_)
        dbs.append(db_)
    g["conv_w"], g["conv_b"] = jnp.concatenate(dws, axis=1), jnp.concatenate(dbs, axis=1)
    length = proj.shape[0]
    tail = jnp.concatenate([_fold_heads(dkc, dkp).astype(BF), _fold_heads(dvc, dvp).astype(BF),
                            jnp.transpose(ddt_t).astype(BF), jnp.zeros((length, _PAD), BF)], axis=1)
    dproj = lax.dynamic_update_slice(dproj, tail, (0, _OFF["k"]))
    g["w_in"] = _mm(sv["hb"], dproj, "tn", F32, 1024, 2176, 1024, f"dwin{l}")
    late = list(late_hook(g)) if late_hook else []
    dh = _mm(dproj, w["win"], "nt", F32, 1024, 1024, 2176, f"dh{l}", (late, False) if late else None)
    if late:
        dh, g["late_hosted"] = dh
    return g, ds2_f, dh


def kernel(x, ln_in_g, ln_in_b, w_in, conv_w, conv_b, dt_bias, a_log, d_skip, ssd_norm_w, att_sinks, w_ssd_out, w_att_out, w_mix_out, ln_mix_g, ln_mix_b, w_ffn_gate, w_ffn_up, w_ffn_down, ln_ffn_g, ln_ffn_b, loss_target, m_ln_in_g, m_ln_in_b, m_w_in, m_conv_w, m_conv_b, m_dt_bias, m_a_log, m_d_skip, m_ssd_norm_w, m_att_sinks, m_w_ssd_out, m_w_att_out, m_w_mix_out, m_ln_mix_g, m_ln_mix_b, m_w_ffn_gate, m_w_ffn_up, m_w_ffn_down, m_ln_ffn_g, m_ln_ffn_b, v_ln_in_g, v_ln_in_b, v_w_in, v_conv_w, v_conv_b, v_dt_bias, v_a_log, v_d_skip, v_ssd_norm_w, v_att_sinks, v_w_ssd_out, v_w_att_out, v_w_mix_out, v_ln_mix_g, v_ln_mix_b, v_w_ffn_gate, v_w_ffn_up, v_w_ffn_down, v_ln_ffn_g, v_ln_ffn_b):
    env = dict(locals())
    wts = {n: env[n] for n in WEIGHTS}
    mom1 = {n: env["m_" + n] for n in WEIGHTS}
    mom2 = {n: env["v_" + n] for n in WEIGHTS}
    ci = lax.axis_index("c")
    chip = 2 * lax.axis_index("x") + lax.axis_index("y")
    xs_ = x[0]
    tgt = loss_target[0]

    COLS = ("w_in", "w_ffn_gate", "w_ffn_up")

    def half(t):
        return lax.dynamic_slice_in_dim(t, ci * (t.shape[0] // 2), t.shape[0] // 2, axis=0)

    def assemble(name, parts):
        if name == "w_in":
            return _gather_cols(parts, _IN_RANGES, _PAD)
        if name in COLS:
            return _gather_cols(parts, ((0, 4 * parts.shape[3]),))
        return jnp.concatenate([parts[h, s] for s in range(4) for h in range(2)], axis=0)

    def split(name, gfull):
        if name == "w_in":
            return _scatter_cols(gfull, _IN_RANGES, IN_DIM // 4)
        if name in COLS:
            return _scatter_cols(gfull, ((0, gfull.shape[1]),), gfull.shape[1] // 4)
        hr = gfull.shape[0] // 8
        return jnp.stack([jnp.stack([gfull[(2 * s + h) * hr:(2 * s + h + 1) * hr] for s in range(4)]) for h in range(2)])

    mine = [[half(wts[n][l].astype(BF)) for n in BIG] for l in range(DEPTH)]
    conv_all, = _with_own(_exchange([conv_w], "xy", True, "gather_conv"), [conv_w], chip)
    conv_full = jnp.transpose(conv_all, (1, 2, 0, 3)).reshape(DEPTH, SSD_CONV, CONV_DIM)

    KEYS = {"w_in": "win", "w_ssd_out": "wso", "w_att_out": "wao", "w_mix_out": "wmo", "w_ffn_down": "wd"}

    def layer_weights(l, names, by_chip):
        own = [mine[l][BIG.index(n)] for n in names]
        by_chip = _with_own(by_chip, own, chip)
        parts = _with_own(_exchange(by_chip, "c", True, f"gather_cores{l}_{names[0]}"), by_chip, ci)
        fw = {n: assemble(n, part) for n, part in zip(names, parts)}
        w = {KEYS[n]: fw[n] for n in names if n in KEYS}
        if "w_ffn_gate" in fw:
            w["wgu"] = jnp.concatenate([fw["w_ffn_gate"], fw["w_ffn_up"]], axis=1)
        return w

    layer_p = []
    for l in range(DEPTH):
        layer_p.append({
            "conv_w": conv_full[l], "conv_b": conv_b[l][None],
            "dt_bias": dt_bias[l][:, None], "a_log": a_log[l][:, None],
            "d_skip_c": jnp.repeat(d_skip[l], 64)[None], "ssd_norm_w": ssd_norm_w[l][None],
            "att_sinks": att_sinks[l][None], "ln_mix_g": ln_mix_g[l][None], "ln_mix_b": ln_mix_b[l][None],
            "ln_ffn_g": ln_ffn_g[l][None], "ln_ffn_b": ln_ffn_b[l][None],
        })

    def ln_in_fn(x_, g_, b_):
        y = _layer_norm(x_, g_, b_)
        return (y, y), ()
    (h, hb), _ = _rows(ln_in_fn, [(xs_, 0, D_MODEL)], [ln_in_g[None], ln_in_b[None]], [(D_MODEL, F32), (D_MODEL, BF)], [], 256, "ln_in")
    rest = tuple(n for n in BIG if n != "w_in")
    layer_w = [layer_weights(0, ("w_in",), _exchange([mine[0][BIG.index("w_in")]], "xy", True, "gather_chips0"))]
    behind_ssd = [i for i, n in enumerate(BIG) if n in COLS]
    behind_attn = [i for i, n in enumerate(BIG) if n not in COLS]
    saved = []
    for l in range(DEPTH):
        nxt = l + 1 < DEPTH
        sv = _layer_fwd(l, h, hb, layer_w[l], layer_p[l], ([mine[l + 1][i] for i in behind_ssd], True) if nxt else None,
                        ([mine[l + 1][i] for i in behind_attn], True) if nxt else None,
                        ([mine[l][BIG.index(n)] for n in rest], True) if l == 0 else None,
                        functools.partial(layer_weights, l, rest))
        saved.append(sv)
        if nxt:
            got = dict(zip(behind_ssd + behind_attn, sv["hosted"] + sv["hosted_attn"]))
            layer_w.append(layer_weights(l + 1, BIG, [got[i] for i in range(len(BIG))]))
            h, hb = _ln_fwd(sv["h1"], sv["ffn"], layer_p[l]["ln_ffn_g"], layer_p[l]["ln_ffn_b"], f"lnffn{l}")

    def loss_fn(a_, r_, t_, g_, b_):
        y, vjp = jax.vjp(_layer_norm, ALPHA * a_ + r_, g_, b_)
        err = y - t_
        ds, dg, db = vjp(err * (1.0 / D_MODEL))
        part = 0.5 * jnp.sum(jnp.mean(err * err, axis=-1, keepdims=True), axis=0, keepdims=True)
        return (ds, ds), (dg, db, jnp.broadcast_to(part, (1, BLK)))
    sv = saved[-1]
    (ds_f, ds_b), (dg_last, db_last, loss_part) = _rows(
        loss_fn, [(sv["h1"], 0, D_MODEL), (sv["ffn"], 0, D_MODEL), (tgt, 0, D_MODEL)],
        [layer_p[-1]["ln_ffn_g"], layer_p[-1]["ln_ffn_b"]], [(D_MODEL, F32), (D_MODEL, BF)],
        [(1, D_MODEL), (1, D_MODEL), (1, BLK)], 256, "loss")
    loss = lax.psum(loss_part[0, 0], ("x", "y", "c"))

    def reduce_cores(l, g, names):
        src = [split(n, g[n]) for n in names]
        recv = _exchange(src, "c", False, f"reduce_cores{l}_{names[0]}")
        return {n: _sum_pieces(r, o, True, ci, f"sum_cores{l}_{n}") for n, r, o in zip(names, recv, src)}

    def bf16_of(sums, names):
        return [sums[n][1] for n in names]

    def finish_reduce(l, sums, from_chips):
        reduced = [_sum_pieces(from_chips[n], sums[n][0], True, chip, f"sum_chips{l}_{n}")[0] for n in BIG]
        halves = _with_own(_exchange(reduced, "c", True, f"swap_halves{l}"), reduced, ci)
        return {n: t.reshape(2 * t.shape[1], t.shape[2]) for n, t in zip(BIG, halves)}

    EARLY = tuple(n for n in BIG if n != "w_in")
    grads = [None] * DEPTH
    chip_sums = [None] * DEPTH
    for l in reversed(range(DEPTH)):
        sv = saved[l]
        if l == DEPTH - 1:
            lg, lb = dg_last, db_last
        else:
            ds_f, ds_b, lg, lb = _ln_bwd(sv["h1"], sv["ffn"], d1, d2, layer_p[l]["ln_ffn_g"], layer_p[l]["ln_ffn_b"], f"dlnffn{l}")
        sums = {}

        def early_hook(g, l=l, sums=sums):
            sums.update(reduce_cores(l, g, EARLY))
            return bf16_of(sums, EARLY)

        def late_hook(g, l=l, sums=sums):
            sums.update(reduce_cores(l, g, ("w_in",)))
            return bf16_of(sums, ("w_in",))
        if l == 0:
            g, d1, d2 = _layer_bwd(l, sv, layer_w[l], layer_p[l], ds_f, ds_b, bf16_of(chip_sums[l + 1], BIG), early_hook, late_hook)
        else:
            g, d1, d2 = _layer_bwd(l, sv, layer_w[l], layer_p[l], ds_f, ds_b)
            sums.update(reduce_cores(l, g, BIG))
        g["ln_ffn_g"], g["ln_ffn_b"] = lg, lb
        grads[l] = g
        chip_sums[l] = sums

    def ln_in_bwd(x_, d1_, d2_, g_, b_):
        _, vjp = jax.vjp(_layer_norm, x_, g_, b_)
        dx, dg, db = vjp(ALPHA * d1_ + d2_)
        return (dx,), (dg, db)
    (grad_x,), (g_ln_in_g, g_ln_in_b) = _rows(
        ln_in_bwd, [(xs_, 0, D_MODEL), (d1, 0, D_MODEL), (d2, 0, D_MODEL)], [ln_in_g[None], ln_in_b[None]],
        [(D_MODEL, F32)], [(1, D_MODEL), (1, D_MODEL)], 256, "dln_in")

    shard_grads = [None] * DEPTH
    carried = grads[0]["hosted"]
    shard_grads[1] = finish_reduce(1, chip_sums[1], dict(zip(BIG, carried[:len(BIG)])))
    from_chips = dict(zip(EARLY, carried[len(BIG):]))
    from_chips["w_in"], = grads[0]["late_hosted"]
    shard_grads[0] = finish_reduce(0, chip_sums[0], from_chips)
    big_grad = {n: jnp.stack([shard_grads[l][n] for l in range(DEPTH)]).reshape(wts[n].shape) for n in BIG}

    small_g = {"ln_in_g": g_ln_in_g[0], "ln_in_b": g_ln_in_b[0]}
    for n in SMALL[2:]:
        small_g[n] = jnp.stack([grads[l][n].reshape(wts[n].shape[1:] if n != "conv_w" else (SSD_CONV, CONV_DIM)) for l in range(DEPTH)])
    flat = jnp.concatenate([small_g[n].reshape(-1) for n in SMALL])
    n_small = flat.shape[0]
    width = -(-n_small // 1024) * 1024
    flat = jnp.pad(flat, (0, width - n_small)).reshape(8, width // 8)
    gathered, = _with_own(_exchange([flat], "all", True, "gather_small"), [flat], 2 * chip + ci)
    total = _sum_slots(gathered, "sum_small").reshape(-1)
    small_grad, off = {}, 0
    for n in SMALL:
        shp = small_g[n].shape
        cnt = math.prod(shp)
        small_grad[n] = total[off:off + cnt].reshape(shp)
        off += cnt
    small_grad["conv_w"] = lax.dynamic_slice_in_dim(small_grad["conv_w"], chip * (CONV_DIM // 4), CONV_DIM // 4, axis=2)

    out_g, out_d, out_m, out_v = {}, {}, {}, {}
    for n in BIG:
        shp = wts[n].shape
        two_d = lambda t: t.reshape(shp[0] * shp[1], shp[2])
        d_, m_, v_ = _adamw(two_d(wts[n]), two_d(big_grad[n]), two_d(mom1[n]), two_d(mom2[n]), f"adamw_{n}")
        out_g[n], out_d[n], out_m[n], out_v[n] = big_grad[n], d_.reshape(shp), m_.reshape(shp), v_.reshape(shp)

    def flat_small(d):
        f = jnp.concatenate([d[n].reshape(-1) for n in SMALL])
        return jnp.pad(f, (0, swidth - f.shape[0])).reshape(8, swidth // 8)
    n_sw = sum(math.prod(wts[n].shape) for n in SMALL)
    swidth = -(-n_sw // 1024) * 1024
    d_, m_, v_ = _adamw(flat_small(wts), flat_small(small_grad), flat_small(mom1), flat_small(mom2), "adamw_small")
    off = 0
    for n in SMALL:
        shp = wts[n].shape
        cnt = math.prod(shp)
        out_g[n] = small_grad[n]
        out_d[n] = d_.reshape(-1)[off:off + cnt].reshape(shp)
        out_m[n] = m_.reshape(-1)[off:off + cnt].reshape(shp)
        out_v[n] = v_.reshape(-1)[off:off + cnt].reshape(shp)
        off += cnt

    return (loss, grad_x[None], *[out_g[n] for n in WEIGHTS], *[out_d[n] for n in WEIGHTS],
            *[out_m[n] for n in WEIGHTS], *[out_v[n] for n in WEIGHTS])
```

```python
import functools
import itertools
import math

import jax
import jax.numpy as jnp
from jax import lax
from jax.experimental import pallas as pl
from jax.experimental.pallas import tpu as pltpu

F32 = jnp.float32
BF = jnp.bfloat16

D_MODEL = 1024
DEPTH = 2
ATT_HEADS = 16
ATT_HEAD_DIM = 64
BLK = 128
SSD_D_INNER = 2048
SSD_HEADS = 32
SSD_GROUPS = 4
SSD_STATE = 128
SSD_CONV = 4
BC_DIM = 512
CONV_DIM = 3072
FFN_HIDDEN = 2816
IN_DIM = 8480
LN_EPS = 1e-5
RMS_EPS = 1e-5
ALPHA = (2 * DEPTH) ** 0.25
ADAM_LR = 0.001
ADAM_B1 = 0.9
ADAM_B2 = 0.999
ADAM_EPS = 1e-08
ADAM_WD = 0.01
ADAM_STEP = 10

_PACK = (("gates", 6432, 2048), ("z", 1280, 2048), ("q", 0, 1024), ("xs", 3328, 2048), ("B", 5376, 512),
         ("C", 5888, 512), ("k", 1024, 128), ("v", 1152, 128), ("dt", 6400, 32))
NP = 8704
_OFF = {}
_o = 0
for _n, _, _w in _PACK:
    _OFF[_n] = _o
    _o += _w
_PAD = NP - _o

VMEM_LIMIT_BYTES = 56 * 1024 * 1024
BIG = ("w_in", "w_ssd_out", "w_att_out", "w_mix_out", "w_ffn_gate", "w_ffn_up", "w_ffn_down")
WEIGHTS = ("ln_in_g", "ln_in_b", "w_in", "conv_w", "conv_b", "dt_bias", "a_log", "d_skip", "ssd_norm_w", "att_sinks",
           "w_ssd_out", "w_att_out", "w_mix_out", "ln_mix_g", "ln_mix_b", "w_ffn_gate", "w_ffn_up", "w_ffn_down",
           "ln_ffn_g", "ln_ffn_b")
SMALL = tuple(n for n in WEIGHTS if n not in BIG)


def _params(n_grid):
    return pltpu.CompilerParams(dimension_semantics=("arbitrary",) * n_grid, vmem_limit_bytes=VMEM_LIMIT_BYTES)


def _dot(a, b, ca, cb):
    return lax.dot_general(a.astype(BF), b.astype(BF), (((ca,), (cb,)), ((), ())), preferred_element_type=F32)


@jax.custom_vjp
def _nn(a, b):
    return _dot(a, b, 1, 0)


def _nn_f(a, b):
    return _dot(a, b, 1, 0), (a.astype(BF), b.astype(BF))


def _nn_b(res, g):
    a, b = res
    return _dot(g, b, 1, 1), _dot(a, g, 0, 0)


_nn.defvjp(_nn_f, _nn_b)


@jax.custom_vjp
def _nt(a, b):
    return _dot(a, b, 1, 1)


def _nt_f(a, b):
    return _dot(a, b, 1, 1), (a.astype(BF), b.astype(BF))


def _nt_b(res, g):
    a, b = res
    return _dot(g, b, 1, 0), _dot(g, a, 0, 0)


_nt.defvjp(_nt_f, _nt_b)


@jax.custom_vjp
def _tn(a, b):
    return _dot(a, b, 0, 0)


def _tn_f(a, b):
    return _dot(a, b, 0, 0), (a.astype(BF), b.astype(BF))


def _tn_b(res, g):
    a, b = res
    return _dot(b, g, 1, 1), _dot(a, g, 1, 0)


_tn.defvjp(_tn_f, _tn_b)


def _sigmoid(x):
    return 0.5 * jnp.tanh(0.5 * x) + 0.5


def _silu(x):
    return x * _sigmoid(x)


def _layer_norm(s, g, b):
    mu = jnp.mean(s, axis=-1, keepdims=True)
    sc = s - mu
    var = jnp.mean(sc * sc, axis=-1, keepdims=True)
    return sc * lax.rsqrt(var + LN_EPS) * g + b


def _ssd_group(pre_x, pre_b, pre_c, z, dtr, dtb, alog, dsk, nw, state):
    t = pre_x.shape[0]
    xs, bm, cm = _silu(pre_x), _silu(pre_b), _silu(pre_c)
    dt = jax.nn.softplus(dtr + dtb)
    da = dt * (-jnp.exp(alog))
    row = lax.broadcasted_iota(jnp.int32, (t, t), 0)
    col = lax.broadcasted_iota(jnp.int32, (t, t), 1)
    upper = (row <= col).astype(F32)
    cum = jnp.dot(da, upper, precision=lax.Precision.HIGHEST, preferred_element_type=F32)
    tot = jnp.sum(da, axis=1, keepdims=True)
    cb = _nt(cm, bm)
    tril = row >= col
    first = lax.broadcasted_iota(jnp.int32, (t, BLK), 1) < 64
    first_row = lax.broadcasted_iota(jnp.int32, (1, BLK), 1) < 64

    def col_form(r):
        return jnp.broadcast_to(r, (t, t)).T

    ys, new_state = [], []
    for p in range(4):
        h0, h1 = 2 * p, 2 * p + 1
        sl = slice(BLK * p, BLK * (p + 1))
        x_pair = xs[:, sl] * jnp.where(first, col_form(dt[h0:h0 + 1]), col_form(dt[h1:h1 + 1]))
        y_pair = None
        cc = []
        for h, keep in ((h0, first), (h1, jnp.logical_not(first))):
            cr = jnp.broadcast_to(cum[h:h + 1], (t, t))
            cc.append(cr.T)
            decay = jnp.exp(jnp.where(tril, cc[-1] - cr, -1e30))
            y_h = _nn(decay * cb, jnp.where(keep, x_pair, 0.0))
            y_pair = y_h if y_pair is None else y_pair + y_h
        s_pair = state[:, sl]
        y_pair = y_pair + _nn(cm, s_pair) * jnp.where(first, jnp.exp(cc[0]), jnp.exp(cc[1]))
        to_end = jnp.where(first, jnp.exp(tot[h0:h0 + 1] - cc[0]), jnp.exp(tot[h1:h1 + 1] - cc[1]))
        chunk_decay = jnp.where(first_row, jnp.exp(tot[h0:h0 + 1]), jnp.exp(tot[h1:h1 + 1]))
        new_state.append(s_pair * chunk_decay + _tn(bm, x_pair * to_end))
        ys.append(y_pair + dsk[:, sl] * xs[:, sl])
    y = jnp.concatenate(ys, axis=1) * _silu(z)
    y = y * lax.rsqrt(jnp.mean(y * y, axis=-1, keepdims=True) + RMS_EPS) * nw
    return y, jnp.concatenate(new_state, axis=1)


def _attn_block(q, kp, kc, vp, vc, sinks, n, kv):
    t = q.shape[0]
    kb = jnp.concatenate([kp, kc], axis=0)
    vb = jnp.concatenate([vp, vc], axis=0)
    qi = lax.broadcasted_iota(jnp.int32, (t, 2 * t), 0)
    kj = lax.broadcasted_iota(jnp.int32, (t, 2 * t), 1)
    rel = qi + t - kj
    valid = (rel >= 0) & (rel < t) & ((n * t - t + kj) >= 0)
    relf = rel.astype(F32)
    first = lax.broadcasted_iota(jnp.int32, (t, BLK), 1) < 64
    lane16 = lax.broadcasted_iota(jnp.int32, (1, ATT_HEADS), 1)
    outs = []
    for p in range(4):
        qp = q[:, BLK * p:BLK * (p + 1)] * (ATT_HEAD_DIM ** -0.5)
        o_pair = None
        for half, keep in enumerate((first, jnp.logical_not(first))):
            h = kv * 8 + 2 * p + half
            s = _nt(jnp.where(keep, qp, 0.0), kb)
            slope = jnp.exp((-8.0 * math.log(2.0) / ATT_HEADS) * jnp.asarray(h + 1, F32))
            s = jnp.where(valid, s - slope * relf, -1e30)
            sink = jnp.sum(jnp.where(lane16 == h, sinks, 0.0), axis=1, keepdims=True)
            m = lax.stop_gradient(jnp.maximum(jnp.max(s, axis=1, keepdims=True), sink))
            e = jnp.exp(s - m)
            den = jnp.sum(e, axis=1, keepdims=True) + jnp.exp(sink - m)
            o = _nn(e * (1.0 / den), vb)
            o_pair = o if o_pair is None else jnp.where(first, o_pair, o)
        outs.append(o_pair)
    return jnp.concatenate(outs, axis=1)


def _chip_exchange(src, out, send_sems, recv_sems, bcast):
    x, y, c = lax.axis_index("x"), lax.axis_index("y"), lax.axis_index("c")
    me = 2 * x + y

    def copy(a, d, slot):
        j = jnp.bitwise_xor(me, d)
        return pltpu.make_async_remote_copy(
            src_ref=src[a] if bcast else src[a].at[j], dst_ref=out[a].at[slot],
            send_sem=send_sems.at[a * 4 + d], recv_sem=recv_sems.at[a * 4 + d],
            device_id=(j // 2, j % 2, c), device_id_type=pl.DeviceIdType.MESH)

    pairs = [(a, d) for a in range(len(src)) for d in range(1, 4)]

    def start():
        for a, d in pairs:
            copy(a, d, me).start()

    def wait():
        for a, d in pairs:
            copy(a, d, jnp.bitwise_xor(me, d)).wait_recv()
        for a, d in pairs:
            copy(a, d, me).wait_send()

    return start, wait


def _mm(a, b, mode, out_dtype, tm, tn, tk, name, hosted=None):
    if mode == "nn":
        (m, k), (k2, n) = a.shape, b.shape
    elif mode == "nt":
        (m, k), (n, k2) = a.shape, b.shape
    else:
        (k, m), (k2, n) = a.shape, b.shape
    assert k == k2, (a.shape, b.shape, mode)
    tm, tn, tk = min(tm, m), min(tn, n), min(tk, k)
    assert m % tm == 0 and n % tn == 0 and k % tk == 0, (m, n, k, tm, tn, tk)
    nk = k // tk
    grid = (m // tm, n // tn, nk)
    ca, cb = {"nn": (1, 0), "nt": (1, 1), "tn": (0, 0)}[mode]
    h_arrays, h_bcast = hosted if hosted else ((), False)
    nh = len(h_arrays)

    def body(*refs):
        a_ref, b_ref = refs[:2]
        o_ref = refs[2 + nh]
        scratch = refs[3 + 2 * nh:]
        step = [pl.program_id(ax) for ax in range(3)]
        if nh:
            start, wait = _chip_exchange(refs[2:2 + nh], refs[3 + nh:3 + 2 * nh], scratch[-2], scratch[-1], h_bcast)

            @pl.when((step[0] == 0) & (step[1] == 0) & (step[2] == 0))
            def _():
                start()

        part = _dot(a_ref[...], b_ref[...], ca, cb)
        if nk == 1:
            o_ref[...] = part.astype(o_ref.dtype)
        else:
            acc_ref = scratch[0]

            @pl.when(step[2] == 0)
            def _():
                acc_ref[...] = part

            @pl.when(step[2] > 0)
            def _():
                acc_ref[...] += part

            @pl.when(step[2] == nk - 1)
            def _():
                o_ref[...] = acc_ref[...].astype(o_ref.dtype)

        if nh:
            @pl.when((step[0] == grid[0] - 1) & (step[1] == grid[1] - 1) & (step[2] == nk - 1))
            def _():
                wait()

    a_spec = pl.BlockSpec((tk, tm), lambda i, j, kk: (kk, i)) if mode == "tn" else pl.BlockSpec((tm, tk), lambda i, j, kk: (i, kk))
    b_spec = pl.BlockSpec((tn, tk), lambda i, j, kk: (j, kk)) if mode == "nt" else pl.BlockSpec((tk, tn), lambda i, j, kk: (kk, j))
    any_spec = pl.BlockSpec(memory_space=pl.ANY)
    res = pl.pallas_call(
        body, name=name, grid=grid,
        in_specs=[a_spec, b_spec] + [any_spec] * nh,
        out_specs=[pl.BlockSpec((tm, tn), lambda i, j, kk: (i, j))] + [any_spec] * nh,
        out_shape=[jax.ShapeDtypeStruct((m, n), out_dtype)]
        + [jax.ShapeDtypeStruct(((4,) + s.shape) if h_bcast else s.shape, s.dtype) for s in h_arrays],
        scratch_shapes=([] if nk == 1 else [pltpu.VMEM((tm, tn), F32)])
        + ([pltpu.SemaphoreType.DMA((4 * nh,)), pltpu.SemaphoreType.DMA((4 * nh,))] if nh else []),
        compiler_params=_params(3),
    )(a, b, *h_arrays)
    return (res[0], list(res[1:])) if nh else res[0]


def _rows(fn, rows, params, outs, accs, tile, name):
    length = rows[0][0].shape[0]
    tile = min(tile, length)
    assert length % tile == 0
    nr, npar, no = len(rows), len(params), len(outs)

    def body(*refs):
        vals = [r[...] for r in refs[:nr + npar]]
        o, a = fn(*vals)
        for ref, val in zip(refs[nr + npar:nr + npar + no], o):
            ref[...] = val.astype(ref.dtype)
        i = pl.program_id(0)
        for ref, val in zip(refs[nr + npar + no:], a):
            @pl.when(i == 0)
            def _(ref=ref, val=val):
                ref[...] = val

            @pl.when(i > 0)
            def _(ref=ref, val=val):
                ref[...] += val

    in_specs = [pl.BlockSpec((tile, w), functools.partial(lambda i, cb: (i, cb), cb=cb)) for _, cb, w in rows]
    in_specs += [pl.BlockSpec(p.shape, lambda i: (0, 0)) for p in params]
    outs = [o if len(o) == 3 else (o[0], o[1], o[0]) for o in outs]
    out_specs = [pl.BlockSpec((tile, w), lambda i: (i, 0)) for w, _, _ in outs]
    out_specs += [pl.BlockSpec((r, w), lambda i: (0, 0)) for r, w in accs]
    out_shape = [jax.ShapeDtypeStruct((length, full), dt) for _, dt, full in outs]
    out_shape += [jax.ShapeDtypeStruct((r, w), F32) for r, w in accs]
    res = pl.pallas_call(
        body, name=name, grid=(length // tile,), in_specs=in_specs, out_specs=out_specs, out_shape=out_shape,
        compiler_params=_params(1),
    )(*[r[0] for r in rows], *params)
    return res[:no], res[no:]


CONV_SUB = 32


def _conv_fwd(proj, conv_w, conv_b, name, hosted=None):
    length = proj.shape[0]
    tl, cw = min(512, length), 512
    cb0 = _OFF["xs"] // cw
    grid = (CONV_DIM // cw, length // tl)
    h_arrays, h_bcast = hosted if hosted else ((), False)
    nh = len(h_arrays)

    def body(*refs):
        u_ref, halo_ref, w_ref, b_ref = refs[:4]
        o_ref = refs[4 + nh]
        win = refs[5 + 2 * nh]
        j, i = pl.program_id(0), pl.program_id(1)
        if nh:
            start, wait = _chip_exchange(refs[4:4 + nh], refs[5 + nh:5 + 2 * nh], refs[-2], refs[-1], h_bcast)

            @pl.when((j == 0) & (i == 0))
            def _():
                start()

            @pl.when((j == grid[0] - 1) & (i == grid[1] - 1))
            def _():
                wait()

        win[0:8, :] = jnp.where(i > 0, halo_ref[...], 0.0)
        win[8:8 + tl, :] = u_ref[...]
        taps = [w_ref[kk:kk + 1, :] for kk in range(SSD_CONV)]
        bias = b_ref[...]
        for r in range(0, tl, CONV_SUB):
            acc = bias + taps[0] * win[5 + r:5 + r + CONV_SUB, :]
            for kk in range(1, SSD_CONV):
                acc = acc + taps[kk] * win[5 + kk + r:5 + kk + r + CONV_SUB, :]
            o_ref[r:r + CONV_SUB, :] = acc

    any_spec = pl.BlockSpec(memory_space=pl.ANY)
    res = pl.pallas_call(
        body, name=name, grid=grid,
        in_specs=[pl.BlockSpec((tl, cw), lambda j, i: (i, cb0 + j)),
                  pl.BlockSpec((8, cw), lambda j, i: (jnp.maximum(i * (tl // 8) - 1, 0), cb0 + j)),
                  pl.BlockSpec((SSD_CONV, cw), lambda j, i: (0, j)),
                  pl.BlockSpec((1, cw), lambda j, i: (0, j))] + [any_spec] * nh,
        out_specs=[pl.BlockSpec((tl, cw), lambda j, i: (i, j))] + [any_spec] * nh,
        out_shape=[jax.ShapeDtypeStruct((length, CONV_DIM), F32)]
        + [jax.ShapeDtypeStruct(((4,) + s.shape) if h_bcast else s.shape, s.dtype) for s in h_arrays],
        scratch_shapes=[pltpu.VMEM((8 + tl, cw), F32)]
        + ([pltpu.SemaphoreType.DMA((4 * nh,)), pltpu.SemaphoreType.DMA((4 * nh,))] if nh else []),
        compiler_params=_params(2),
    )(proj, proj, conv_w, conv_b, *h_arrays)
    return (res[0], list(res[1:])) if nh else res[0]


def _conv_bwd(dpre, proj, conv_w, dproj, part, name):
    length = proj.shape[0]
    width = dpre.shape[1]
    tl, cw = min(512, length), 512
    cb0 = _OFF["xs"] // cw + part
    n_t = length // tl

    def body(d_ref, dnext_ref, u_ref, halo_ref, w_ref, _, du_ref, dw_ref, db_ref, dwin, uwin):
        i = pl.program_id(1)
        dwin[0:tl, :] = d_ref[...]
        dwin[tl:tl + 8, :] = jnp.where(i < n_t - 1, dnext_ref[...], 0.0)
        uwin[0:8, :] = jnp.where(i > 0, halo_ref[...], 0.0)
        uwin[8:8 + tl, :] = u_ref[...]
        taps = [w_ref[kk:kk + 1, :] for kk in range(SSD_CONV)]
        sub = CONV_SUB // 2
        acc_w = [jnp.zeros((sub, cw), F32) for _ in range(SSD_CONV)]
        acc_b = jnp.zeros((sub, cw), F32)
        for r in range(0, tl, sub):
            d = dwin[r:r + sub, :]
            du = taps[3] * d
            for kk in range(SSD_CONV - 1):
                du = du + taps[kk] * dwin[3 - kk + r:3 - kk + r + sub, :]
            du_ref[r:r + sub, :] = du.astype(du_ref.dtype)
            for kk in range(SSD_CONV):
                acc_w[kk] = acc_w[kk] + d * uwin[5 + kk + r:5 + kk + r + sub, :]
            acc_b = acc_b + d
        dw = jnp.concatenate([jnp.sum(a, axis=0, keepdims=True) for a in acc_w], axis=0)
        db = jnp.sum(acc_b, axis=0, keepdims=True)

        @pl.when(i == 0)
        def _():
            dw_ref[...] = dw
            db_ref[...] = db

        @pl.when(i > 0)
        def _():
            dw_ref[...] += dw
            db_ref[...] += db

    return pl.pallas_call(
        body, name=name, grid=(width // cw, n_t), input_output_aliases={5: 0},
        in_specs=[pl.BlockSpec((tl, cw), lambda j, i: (i, j)),
                  pl.BlockSpec((8, cw), lambda j, i: (jnp.minimum((i + 1) * (tl // 8), length // 8 - 1), j)),
                  pl.BlockSpec((tl, cw), lambda j, i: (i, cb0 + j)),
                  pl.BlockSpec((8, cw), lambda j, i: (jnp.maximum(i * (tl // 8) - 1, 0), cb0 + j)),
                  pl.BlockSpec((SSD_CONV, cw), lambda j, i: (0, part + j)),
                  pl.BlockSpec(memory_space=pl.ANY)],
        out_specs=[pl.BlockSpec((tl, cw), lambda j, i: (i, cb0 + j)),
                   pl.BlockSpec((SSD_CONV, cw), lambda j, i: (0, j)),
                   pl.BlockSpec((1, cw), lambda j, i: (0, j))],
        out_shape=[jax.ShapeDtypeStruct((length, NP), BF),
                   jax.ShapeDtypeStruct((SSD_CONV, width), F32),
                   jax.ShapeDtypeStruct((1, width), F32)],
        scratch_shapes=[pltpu.VMEM((tl + 8, cw), F32), pltpu.VMEM((8 + tl, cw), F32)],
        compiler_params=_params(2),
    )(dpre, dpre, proj, proj, conv_w, dproj)


def _ssd_in_specs(rev, nc):
    def cidx(c):
        return nc - 1 - c if rev else c
    whole = lambda c: (0, 0)
    return [pl.BlockSpec((BLK, SSD_D_INNER), lambda c: (cidx(c), 0)),
            pl.BlockSpec((BLK, BC_DIM), lambda c: (cidx(c), SSD_D_INNER // BC_DIM)),
            pl.BlockSpec((BLK, BC_DIM), lambda c: (cidx(c), SSD_D_INNER // BC_DIM + 1)),
            pl.BlockSpec((BLK, SSD_D_INNER), lambda c: (cidx(c), _OFF["z"] // SSD_D_INNER)),
            pl.BlockSpec((SSD_HEADS, BLK), lambda c: (0, cidx(c))),
            pl.BlockSpec((SSD_HEADS, 1), whole),
            pl.BlockSpec((SSD_HEADS, 1), whole),
            pl.BlockSpec((1, SSD_D_INNER), whole),
            pl.BlockSpec((1, SSD_D_INNER), whole)]


def _group_args(g, px, pb, pc, z, dtr, dtb, alog, dsk, nw):
    wide, narrow, heads = slice(512 * g, 512 * (g + 1)), slice(BLK * g, BLK * (g + 1)), slice(8 * g, 8 * (g + 1))
    return (px[:, wide], pb[:, narrow], pc[:, narrow], z[:, wide], dtr[heads, :], dtb[heads, :], alog[heads, :],
            dsk[:, wide], nw[:, wide])


def _ssd_fwd(pre, proj, dt_t, dtb, alog, dsk, nw, name, hosted=None):
    length = pre.shape[0]
    nc = length // BLK
    h_arrays, h_bcast = hosted if hosted else ((), False)
    nh = len(h_arrays)

    def body(*refs):
        px, pb, pc, z, dtr, dtb_r, al_r, dsk_r, nw_r = refs[:9]
        y_ref, sin_ref = refs[9 + nh:11 + nh]
        st = refs[11 + 2 * nh]
        c = pl.program_id(0)
        if nh:
            start, wait = _chip_exchange(refs[9:9 + nh], refs[11 + nh:11 + 2 * nh], refs[-2], refs[-1], h_bcast)

        @pl.when(c == 0)
        def _():
            st[...] = jnp.zeros_like(st)
            if nh:
                start()

        for g in range(SSD_GROUPS):
            s_in = st[g]
            sin_ref[g] = s_in
            y, s_out = _ssd_group(*_group_args(g, px, pb, pc, z, dtr, dtb_r, al_r, dsk_r, nw_r), s_in)
            y_ref[:, 512 * g:512 * (g + 1)] = y.astype(y_ref.dtype)
            st[g] = s_out

        if nh:
            @pl.when(c == nc - 1)
            def _():
                wait()

    any_spec = pl.BlockSpec(memory_space=pl.ANY)
    res = pl.pallas_call(
        body, name=name, grid=(nc,), in_specs=_ssd_in_specs(False, nc) + [any_spec] * nh,
        out_specs=[pl.BlockSpec((BLK, SSD_D_INNER), lambda c: (c, 0)),
                   pl.BlockSpec((SSD_GROUPS, None, SSD_STATE, 512), lambda c: (0, c, 0, 0))] + [any_spec] * nh,
        out_shape=[jax.ShapeDtypeStruct((length, SSD_D_INNER), BF),
                   jax.ShapeDtypeStruct((SSD_GROUPS, nc, SSD_STATE, 512), F32)]
        + [jax.ShapeDtypeStruct(((4,) + s.shape) if h_bcast else s.shape, s.dtype) for s in h_arrays],
        scratch_shapes=[pltpu.VMEM((SSD_GROUPS, SSD_STATE, 512), F32)]
        + ([pltpu.SemaphoreType.DMA((4 * nh,)), pltpu.SemaphoreType.DMA((4 * nh,))] if nh else []),
        compiler_params=_params(1),
    )(pre, pre, pre, proj, dt_t, dtb, alog, dsk, nw, *h_arrays)
    return (res[0], res[1], list(res[2:])) if nh else (res[0], res[1])


def _ssd_bwd(pre, proj, dt_t, dtb, alog, dsk, nw, s_in, dy, dproj, name, hosted=None):
    length = pre.shape[0]
    nc = length // BLK
    h_arrays, h_bcast = hosted if hosted else ((), False)
    nh = len(h_arrays)

    def body(*refs):
        px, pb, pc, z, dtr, dtb_r, al_r, dsk_r, nw_r, sin_r, dy_r = refs[:11]
        dpx, dpb, dpc, dz, ddt, ddtb, dal, ddsk, dnw = refs[12 + nh:21 + nh]
        dst = refs[21 + 2 * nh]
        c = pl.program_id(0)
        if nh:
            start, wait = _chip_exchange(refs[12:12 + nh], refs[21 + nh:21 + 2 * nh], refs[-2], refs[-1], h_bcast)

        @pl.when(c == 0)
        def _():
            dst[...] = jnp.zeros_like(dst)
            if nh:
                start()

        for grp in range(SSD_GROUPS):
            wide, narrow, heads = slice(512 * grp, 512 * (grp + 1)), slice(BLK * grp, BLK * (grp + 1)), slice(8 * grp, 8 * (grp + 1))
            _, vjp = jax.vjp(_ssd_group, *_group_args(grp, px, pb, pc, z, dtr, dtb_r, al_r, dsk_r, nw_r), sin_r[grp])
            g = vjp((dy_r[:, wide], dst[grp]))
            dpx[:, wide] = g[0]
            dpb[:, narrow] = g[1]
            dpc[:, narrow] = g[2]
            dz[:, wide] = g[3].astype(dz.dtype)
            ddt[heads, :] = g[4]
            dst[grp] = g[9]
            for ref, val, idx in ((ddtb, g[5], (heads, slice(None))), (dal, g[6], (heads, slice(None))),
                                  (ddsk, g[7], (slice(None), wide)), (dnw, g[8], (slice(None), wide))):
                @pl.when(c == 0)
                def _(ref=ref, val=val, idx=idx):
                    ref[idx] = val

                @pl.when(c > 0)
                def _(ref=ref, val=val, idx=idx):
                    ref[idx] += val

        if nh:
            @pl.when(c == nc - 1)
            def _():
                wait()

    rev = lambda c: nc - 1 - c
    whole = lambda c: (0, 0)
    any_spec = pl.BlockSpec(memory_space=pl.ANY)
    in_specs = _ssd_in_specs(True, nc) + [
        pl.BlockSpec((SSD_GROUPS, None, SSD_STATE, 512), lambda c: (0, rev(c), 0, 0)),
        pl.BlockSpec((BLK, SSD_D_INNER), lambda c: (rev(c), 0)),
        any_spec] + [any_spec] * nh
    res = pl.pallas_call(
        body, name=name, grid=(nc,), in_specs=in_specs, input_output_aliases={11: 3},
        out_specs=[pl.BlockSpec((BLK, SSD_D_INNER), lambda c: (rev(c), 0)),
                   pl.BlockSpec((BLK, BC_DIM), lambda c: (rev(c), 0)),
                   pl.BlockSpec((BLK, BC_DIM), lambda c: (rev(c), 0)),
                   pl.BlockSpec((BLK, SSD_D_INNER), lambda c: (rev(c), _OFF["z"] // SSD_D_INNER)),
                   pl.BlockSpec((SSD_HEADS, BLK), lambda c: (0, rev(c))),
                   pl.BlockSpec((SSD_HEADS, 1), whole),
                   pl.BlockSpec((SSD_HEADS, 1), whole),
                   pl.BlockSpec((1, SSD_D_INNER), whole),
                   pl.BlockSpec((1, SSD_D_INNER), whole)] + [any_spec] * nh,
        out_shape=[jax.ShapeDtypeStruct((length, SSD_D_INNER), F32),
                   jax.ShapeDtypeStruct((length, BC_DIM), F32),
                   jax.ShapeDtypeStruct((length, BC_DIM), F32),
                   jax.ShapeDtypeStruct((length, NP), BF),
                   jax.ShapeDtypeStruct((SSD_HEADS, length), F32),
                   jax.ShapeDtypeStruct((SSD_HEADS, 1), F32),
                   jax.ShapeDtypeStruct((SSD_HEADS, 1), F32),
                   jax.ShapeDtypeStruct((1, SSD_D_INNER), F32),
                   jax.ShapeDtypeStruct((1, SSD_D_INNER), F32)]
        + [jax.ShapeDtypeStruct(((4,) + s.shape) if h_bcast else s.shape, s.dtype) for s in h_arrays],
        scratch_shapes=[pltpu.VMEM((SSD_GROUPS, SSD_STATE, 512), F32)]
        + ([pltpu.SemaphoreType.DMA((4 * nh,)), pltpu.SemaphoreType.DMA((4 * nh,))] if nh else []),
        compiler_params=_params(1),
    )(pre, pre, pre, proj, dt_t, dtb, alog, dsk, nw, s_in, dy, dproj, *h_arrays)
    return tuple(res[:9]) + ((list(res[9:]),) if nh else ())


assert _OFF["gates"] == 0 and _OFF["z"] % SSD_D_INNER == 0 and _OFF["q"] % D_MODEL == 0 and _OFF["k"] + 512 == NP


def _attn_in_specs():
    prev = lambda n: (0, jnp.maximum(n - 1, 0), 0)
    cur = lambda n: (0, n, 0)
    return [pl.BlockSpec((BLK, D_MODEL), lambda n: (n, _OFF["q"] // D_MODEL)),
            pl.BlockSpec((2, BLK, BLK), prev),
            pl.BlockSpec((2, BLK, BLK), cur),
            pl.BlockSpec((2, BLK, BLK), prev),
            pl.BlockSpec((2, BLK, BLK), cur),
            pl.BlockSpec((1, ATT_HEADS), lambda n: (0, 0))]


def _attn_fwd(proj, k2, v2, sinks, name, hosted=None):
    length = proj.shape[0]
    nb = length // BLK
    h_arrays, h_bcast = hosted if hosted else ((), False)
    nh = len(h_arrays)

    def body(*refs):
        q, kp, kc, vp, vc, sk = refs[:6]
        o_ref = refs[6 + nh]
        n = pl.program_id(0)
        if nh:
            start, wait = _chip_exchange(refs[6:6 + nh], refs[7 + nh:7 + 2 * nh], refs[-2], refs[-1], h_bcast)

            @pl.when(n == 0)
            def _():
                start()

        for kv in range(2):
            cols = slice(512 * kv, 512 * (kv + 1))
            o = _attn_block(q[:, cols], kp[kv], kc[kv], vp[kv], vc[kv], sk[...], n, kv)
            o_ref[:, cols] = o.astype(o_ref.dtype)

        if nh:
            @pl.when(n == nb - 1)
            def _():
                wait()

    any_spec = pl.BlockSpec(memory_space=pl.ANY)
    res = pl.pallas_call(
        body, name=name, grid=(nb,), in_specs=_attn_in_specs() + [any_spec] * nh,
        out_specs=[pl.BlockSpec((BLK, D_MODEL), lambda n: (n, 0))] + [any_spec] * nh,
        out_shape=[jax.ShapeDtypeStruct((length, D_MODEL), BF)]
        + [jax.ShapeDtypeStruct(((4,) + s.shape) if h_bcast else s.shape, s.dtype) for s in h_arrays],
        scratch_shapes=[pltpu.SemaphoreType.DMA((4 * nh,)), pltpu.SemaphoreType.DMA((4 * nh,))] if nh else [],
        compiler_params=_params(1),
    )(proj, k2, k2, v2, v2, sinks, *h_arrays)
    return (res[0], list(res[1:])) if nh else res[0]


def _attn_bwd(proj, k2, v2, sinks, datt, dproj, name):
    length = proj.shape[0]

    def body(q, kp, kc, vp, vc, sk, do, _, dq, dkp, dkc, dvp, dvc, dsk):
        n = pl.program_id(0)
        dsinks = None
        for kv in range(2):
            cols = slice(512 * kv, 512 * (kv + 1))
            _, vjp = jax.vjp(lambda *a: _attn_block(*a, n, kv), q[:, cols], kp[kv], kc[kv], vp[kv], vc[kv], sk[...])
            g = vjp(do[:, cols].astype(F32))
            dq[:, cols] = g[0].astype(dq.dtype)
            dkp[kv] = g[1]
            dkc[kv] = g[2]
            dvp[kv] = g[3]
            dvc[kv] = g[4]
            dsinks = g[5] if dsinks is None else dsinks + g[5]

        @pl.when(n == 0)
        def _():
            dsk[...] = dsinks

        @pl.when(n > 0)
        def _():
            dsk[...] += dsinks

    blk3 = pl.BlockSpec((2, BLK, BLK), lambda n: (0, n, 0))
    kv_shape = jax.ShapeDtypeStruct((2, length, BLK), F32)
    return pl.pallas_call(
        body, name=name, grid=(length // BLK,),
        in_specs=_attn_in_specs() + [pl.BlockSpec((BLK, D_MODEL), lambda n: (n, 0)), pl.BlockSpec(memory_space=pl.ANY)],
        out_specs=[pl.BlockSpec((BLK, D_MODEL), lambda n: (n, _OFF["q"] // D_MODEL)), blk3, blk3, blk3, blk3,
                   pl.BlockSpec((1, ATT_HEADS), lambda n: (0, 0))],
        out_shape=[jax.ShapeDtypeStruct((length, NP), BF), kv_shape, kv_shape, kv_shape, kv_shape,
                   jax.ShapeDtypeStruct((1, ATT_HEADS), F32)],
        input_output_aliases={7: 0},
        compiler_params=_params(1),
    )(proj, k2, k2, v2, v2, sinks, datt, dproj)


DMA_CHUNK_BYTES = 1 << 20
N_STAGE = 8
LOOKAHEAD = 3


def _piece_chunks(shape, itemsize):
    if len(shape) < 2 or shape[-2] % 16 != 0:
        return [()]
    rows, cols = shape[-2:]
    step = min(rows, max(16, DMA_CHUNK_BYTES // (cols * itemsize) // 16 * 16))
    out = []
    for lead in itertools.product(*[range(d) for d in shape[:-2]]):
        for r0 in range(0, rows, step):
            out.append(lead + (pl.ds(r0, min(step, rows - r0)),))
    return out


def _exchange(srcs, group, bcast, name):
    size = {"c": 2, "xy": 4, "all": 8}[group]
    npeer = size - 1
    n = len(srcs)
    for s in srcs:
        assert bcast or s.shape[0] == size
    pieces = [s.shape if bcast else s.shape[1:] for s in srcs]
    chunks = [_piece_chunks(p, s.dtype.itemsize) for p, s in zip(pieces, srcs)]

    def body(*refs):
        src, out = refs[:n], refs[n:2 * n]
        load_sems, send_sems, recv_sems = refs[2 * n:]
        x, y, c = lax.axis_index("x"), lax.axis_index("y"), lax.axis_index("c")
        if group == "c":
            me = c
        elif group == "xy":
            me = 2 * x + y
        else:
            me = 4 * x + 2 * y + c

        def device(j):
            if group == "c":
                return (x, y, j)
            if group == "xy":
                return (j // 2, j % 2, c)
            return (j // 4, (j // 2) % 2, j % 2)

        def part(ref, idx):
            return ref.at[idx] if idx else ref

        def piece(a, j):
            return src[a] if bcast else src[a].at[j]

        for a in range(n):
            jobs = [(None, idx) for idx in chunks[a]] if bcast else [(d, idx) for idx in chunks[a] for d in range(1, size)]
            full_rows = chunks[a][0][-1].size if chunks[a][0] else None
            slot_shape = (full_rows, pieces[a][-1]) if chunks[a][0] else pieces[a]

            def stream(stage, a=a, jobs=jobs, full_rows=full_rows):
                def slot(q):
                    idx = jobs[q][1]
                    view = stage.at[q % N_STAGE]
                    if idx and idx[-1].size != full_rows:
                        view = view.at[pl.ds(0, idx[-1].size)]
                    return view

                def load(q):
                    d, idx = jobs[q]
                    j = me if d is None else jnp.bitwise_xor(me, d)
                    return pltpu.make_async_copy(part(piece(a, j), idx), slot(q), load_sems.at[q % N_STAGE])

                def sends(q):
                    d, idx = jobs[q]
                    return [pltpu.make_async_remote_copy(
                        src_ref=slot(q), dst_ref=part(out[a].at[me], idx),
                        send_sem=send_sems.at[(q % N_STAGE) * npeer + dd - 1], recv_sem=recv_sems.at[a * size + dd],
                        device_id=device(jnp.bitwise_xor(me, dd)), device_id_type=pl.DeviceIdType.MESH)
                        for dd in (range(1, size) if d is None else (d,))]

                nq = len(jobs)
                for q in range(nq + LOOKAHEAD):
                    if q < nq:
                        if q >= N_STAGE:
                            for cp in sends(q - N_STAGE):
                                cp.wait_send()
                        load(q).start()
                    if q >= LOOKAHEAD:
                        load(q - LOOKAHEAD).wait()
                        for cp in sends(q - LOOKAHEAD):
                            cp.start()
                for q in range(max(0, nq - N_STAGE), nq):
                    for cp in sends(q):
                        cp.wait_send()

            pl.run_scoped(stream, pltpu.VMEM((N_STAGE,) + tuple(slot_shape), srcs[a].dtype))

        for a in range(n):
            for d in range(1, size):
                j = jnp.bitwise_xor(me, d)
                pltpu.make_async_remote_copy(
                    src_ref=piece(a, j), dst_ref=out[a].at[j], send_sem=send_sems.at[0], recv_sem=recv_sems.at[a * size + d],
                    device_id=device(j), device_id_type=pl.DeviceIdType.MESH).wait_recv()

    any_spec = pl.BlockSpec(memory_space=pl.ANY)
    return pl.pallas_call(
        body, name=name, in_specs=[any_spec] * n, out_specs=[any_spec] * n,
        out_shape=[jax.ShapeDtypeStruct((size,) + tuple(p), s.dtype) for p, s in zip(pieces, srcs)],
        scratch_shapes=[pltpu.SemaphoreType.DMA((N_STAGE,)), pltpu.SemaphoreType.DMA((N_STAGE * npeer,)),
                        pltpu.SemaphoreType.DMA((n * size,))],
        compiler_params=pltpu.CompilerParams(vmem_limit_bytes=VMEM_LIMIT_BYTES),
    )(*srcs)


def _with_own(outs, owns, me):
    return [lax.dynamic_update_index_in_dim(o, w, me, 0) for o, w in zip(outs, owns)]


def _sum_slots(arr, name):
    k = arr.shape[0]
    rest = arr.shape[1:]
    width = rest[-1]
    rows_per = math.prod(rest[:-1])
    a2 = arr.reshape(k * rows_per, width)
    tile = rows_per
    for cand in (256, 128, 64, 32, 16, 8):
        if rows_per % cand == 0:
            tile = cand
            break
    nt = rows_per // tile

    def body(*refs):
        acc = refs[0][...]
        for r in refs[1:k]:
            acc = acc + r[...]
        refs[k][...] = acc

    in_specs = [pl.BlockSpec((tile, width), functools.partial(lambda i, s: (s * nt + i, 0), s=s)) for s in range(k)]
    out = pl.pallas_call(
        body, name=name, grid=(nt,), in_specs=in_specs, out_specs=pl.BlockSpec((tile, width), lambda i: (i, 0)),
        out_shape=jax.ShapeDtypeStruct((rows_per, width), arr.dtype), compiler_params=_params(1),
    )(*([a2] * k))
    return out.reshape(rest)


def _sum_pieces(recv, own, own_slotted, me, name):
    k = recv.shape[0]
    rest = recv.shape[1:]
    width = rest[-1]
    rows_per = math.prod(rest[:-1])
    r2 = recv.reshape(k * rows_per, width)
    o2 = own.reshape(-1, width)
    tile = rows_per
    for cand in (256, 128, 64, 32, 16, 8):
        if rows_per % cand == 0:
            tile = cand
            break
    nt = rows_per // tile

    def body(me_ref, *refs):
        mine = refs[k][...].astype(F32)
        acc = None
        for j in range(k):
            term = jnp.where(me_ref[0] == j, mine, refs[j][...].astype(F32))
            acc = term if acc is None else acc + term
        refs[k + 1][...] = acc
        refs[k + 2][...] = acc.astype(BF)

    in_specs = [pl.BlockSpec((tile, width), functools.partial(lambda i, m, s: (s * nt + i, 0), s=s)) for s in range(k)]
    in_specs.append(pl.BlockSpec((tile, width), (lambda i, m: (m[0] * nt + i, 0)) if own_slotted else (lambda i, m: (i, 0))))
    out_spec = pl.BlockSpec((tile, width), lambda i, m: (i, 0))
    out, out_bf = pl.pallas_call(
        body, name=name,
        grid_spec=pltpu.PrefetchScalarGridSpec(num_scalar_prefetch=1, grid=(nt,), in_specs=in_specs, out_specs=[out_spec, out_spec]),
        out_shape=[jax.ShapeDtypeStruct((rows_per, width), F32), jax.ShapeDtypeStruct((rows_per, width), BF)],
        compiler_params=_params(1),
    )(jnp.reshape(me, (1,)).astype(jnp.int32), *([r2] * k), o2)
    return out.reshape(rest), out_bf.reshape(rest)


def _adamw(w, g, m, v, name):
    def fn(w_, g_, m_, v_):
        m1 = ADAM_B1 * m_ + (1.0 - ADAM_B1) * g_
        v1 = ADAM_B2 * v_ + (1.0 - ADAM_B2) * (g_ * g_)
        m_hat = m1 / (1.0 - ADAM_B1 ** ADAM_STEP)
        v_hat = v1 / (1.0 - ADAM_B2 ** ADAM_STEP)
        delta = -ADAM_LR * (m_hat / (jnp.sqrt(v_hat) + ADAM_EPS) + ADAM_WD * w_)
        return (delta, m1, v1), ()

    rows, width = w.shape
    tile = rows
    for cand in (256, 128, 64, 32, 16, 8):
        if rows % cand == 0:
            tile = cand
            break
    o, _ = _rows(fn, [(w, 0, width), (g, 0, width), (m, 0, width), (v, 0, width)], [],
                 [(width, F32)] * 3, [], tile, name)
    return o


def _col_pieces(start, width, shard):
    out = []
    while width:
        chip, off = divmod(start, shard)
        take = min(width, shard - off)
        out.append((chip, off, take))
        start, width = start + take, width - take
    return out


def _gather_cols(parts, ranges, pad=0):
    shard = parts.shape[3]
    halves = []
    for h in range(2):
        cols = [parts[h, chip, :, off:off + w] for s, wd in ranges for chip, off, w in _col_pieces(s, wd, shard)]
        if pad:
            cols.append(jnp.zeros((parts.shape[2], pad), parts.dtype))
        halves.append(jnp.concatenate(cols, axis=1))
    return jnp.concatenate(halves, axis=0)


def _scatter_cols(g, ranges, shard):
    starts, pos = {}, 0
    for s, wd in ranges:
        starts[s] = (pos, wd)
        pos += wd
    order = sorted(starts)
    hr = g.shape[0] // 2
    out = []
    for h in range(2):
        per_chip = []
        for chip in range(4):
            cols = []
            for s in order:
                p0, wd = starts[s]
                lo, hi = max(s, chip * shard), min(s + wd, (chip + 1) * shard)
                if lo < hi:
                    cols.append(g[h * hr:(h + 1) * hr, p0 + lo - s:p0 + hi - s])
            per_chip.append(jnp.concatenate(cols, axis=1))
        out.append(jnp.stack(per_chip))
    return jnp.stack(out)


_IN_RANGES = tuple((o, wd) for _, o, wd in _PACK)


def _double_heads(t):
    length = t.shape[0]
    h = jnp.transpose(t.reshape(length, 2, 64), (1, 0, 2))
    return jnp.concatenate([h, h], axis=-1)


def _fold_heads(d_cur, d_prev):
    length = d_cur.shape[1]
    d = d_cur + jnp.concatenate([d_prev[:, BLK:], jnp.zeros((2, BLK, BLK), F32)], axis=1)
    d = d[..., :64] + d[..., 64:]
    return jnp.transpose(d, (1, 0, 2)).reshape(length, 128)


def _ln_fwd(a, r, g, b, name):
    def fn(a_, r_, g_, b_):
        y = _layer_norm(ALPHA * a_ + r_, g_, b_)
        return (y, y), ()
    o, _ = _rows(fn, [(a, 0, D_MODEL), (r, 0, D_MODEL)], [g, b], [(D_MODEL, F32), (D_MODEL, BF)], [], 256, name)
    return o


def _ln_bwd(a, r, d1, d2, g, b, name):
    def fn(a_, r_, d1_, d2_, g_, b_):
        _, vjp = jax.vjp(_layer_norm, ALPHA * a_ + r_, g_, b_)
        ds, dg, db = vjp(ALPHA * d1_ + d2_)
        return (ds, ds), (dg, db)
    o, acc = _rows(fn, [(a, 0, D_MODEL), (r, 0, D_MODEL), (d1, 0, D_MODEL), (d2, 0, D_MODEL)], [g, b],
                   [(D_MODEL, F32), (D_MODEL, BF)], [(1, D_MODEL), (1, D_MODEL)], 256, name)
    return o[0], o[1], acc[0], acc[1]


def _layer_fwd(l, h, hb, w, p, hosted=None, hosted_attn=None, hosted_conv=None, late_weights=None):
    sv = {"h": h, "hb": hb}
    proj = _mm(hb, w["win"], "nn", F32, 2048, 512, 1024, f"proj{l}")
    sv["proj"] = proj
    pre = _conv_fwd(proj, p["conv_w"], p["conv_b"], f"conv{l}", hosted_conv)
    if hosted_conv:
        pre, got = pre
        w.update(late_weights(got))
    sv["pre"] = pre
    dt_t = jnp.transpose(proj[:, _OFF["dt"]:_OFF["dt"] + SSD_HEADS])
    sv["dt_t"] = dt_t
    res = _ssd_fwd(pre, proj, dt_t, p["dt_bias"], p["a_log"], p["d_skip_c"], p["ssd_norm_w"], f"ssd{l}", hosted)
    yn, s_in = res[:2]
    if hosted:
        sv["hosted"] = res[2]
    sv["yn"], sv["s_in"] = yn, s_in
    ya = _mm(yn, w["wso"], "nn", F32, 1024, 1024, 2048, f"ssdout{l}")
    k2 = _double_heads(proj[:, _OFF["k"]:_OFF["k"] + 128])
    v2 = _double_heads(proj[:, _OFF["v"]:_OFF["v"] + 128])
    sv["k2"], sv["v2"] = k2, v2
    att = _attn_fwd(proj, k2, v2, p["att_sinks"], f"attn{l}", hosted_attn)
    if hosted_attn:
        att, sv["hosted_attn"] = att
    sv["att"] = att
    yb = _mm(att, w["wao"], "nn", F32, 1024, 1024, 1024, f"attout{l}")
    sv["ya"], sv["yb"] = ya, yb
    gcb = _OFF["gates"] // 1024

    def gate_fn(ga, gb, ya_, yb_):
        return (_sigmoid(ga) * ya_ + _sigmoid(gb) * yb_,), ()
    (ub,), _ = _rows(gate_fn, [(proj, gcb, 1024), (proj, gcb + 1, 1024), (ya, 0, 1024), (yb, 0, 1024)], [],
                     [(D_MODEL, BF)], [], 256, f"gate{l}")
    sv["ub"] = ub
    mix = _mm(ub, w["wmo"], "nn", F32, 1024, 1024, 1024, f"mixout{l}")
    sv["mix"] = mix
    h1, h1b = _ln_fwd(h, mix, p["ln_mix_g"], p["ln_mix_b"], f"lnmix{l}")
    sv["h1"], sv["h1b"] = h1, h1b
    gu = _mm(h1b, w["wgu"], "nn", F32, 2048, 512, 1024, f"ffnin{l}")
    sv["gu"] = gu

    def act_fn(g_, u_):
        return (_silu(g_) * u_,), ()
    (act,), _ = _rows(act_fn, [(gu, 0, FFN_HIDDEN), (gu, 1, FFN_HIDDEN)], [], [(FFN_HIDDEN, BF)], [], 256, f"swiglu{l}")
    sv["act"] = act
    ffn = _mm(act, w["wd"], "nn", F32, 1024, 1024, FFN_HIDDEN, f"ffnout{l}")
    sv["ffn"] = ffn
    return sv


def _layer_bwd(l, sv, w, p, ds_f, ds_b, hosted=(), early_hook=None, late_hook=None):
    g = {}
    g["w_ffn_down"] = _mm(sv["act"], ds_b, "tn", F32, 1408, 1024, 2048, f"dwd{l}")
    dact = _mm(ds_b, w["wd"], "nt", BF, 1024, FFN_HIDDEN, 1024, f"dact{l}")

    def act_bwd(g_, u_, d_):
        _, vjp = jax.vjp(lambda a, b: _silu(a) * b, g_, u_)
        dg_, du_ = vjp(d_.astype(F32))
        return (jnp.concatenate([dg_, du_], axis=1),), ()
    (dgu,), _ = _rows(act_bwd, [(sv["gu"], 0, FFN_HIDDEN), (sv["gu"], 1, FFN_HIDDEN), (dact, 0, FFN_HIDDEN)], [],
                      [(2 * FFN_HIDDEN, BF)], [], 128, f"dswiglu{l}")
    dwgu = _mm(sv["h1b"], dgu, "tn", F32, 1024, 2816, 512, f"dwgu{l}")
    g["w_ffn_gate"], g["w_ffn_up"] = dwgu[:, :FFN_HIDDEN], dwgu[:, FFN_HIDDEN:]
    dh1 = _mm(dgu, w["wgu"], "nt", F32, 1024, 1024, 2816, f"dh1{l}")
    ds2_f, ds2_b, g["ln_mix_g"], g["ln_mix_b"] = _ln_bwd(sv["h"], sv["mix"], ds_f, dh1, p["ln_mix_g"], p["ln_mix_b"], f"dlnmix{l}")
    g["w_mix_out"] = _mm(sv["ub"], ds2_b, "tn", F32, 1024, 1024, 2048, f"dwmo{l}")
    du = _mm(ds2_b, w["wmo"], "nt", BF, 1024, 1024, 1024, f"du{l}")
    proj = sv["proj"]
    gcb = _OFF["gates"] // 1024

    def gate_bwd(ga, gb, ya_, yb_, du_):
        _, vjp = jax.vjp(lambda a, b, c, d: _sigmoid(a) * c + _sigmoid(b) * d, ga, gb, ya_, yb_)
        dga, dgb, dya, dyb = vjp(du_.astype(F32))
        return (jnp.concatenate([dga, dgb], axis=1), dya, dyb), ()
    (dproj, dya, dyb), _ = _rows(
        gate_bwd, [(proj, gcb, 1024), (proj, gcb + 1, 1024), (sv["ya"], 0, 1024), (sv["yb"], 0, 1024), (du, 0, 1024)], [],
        [(2048, BF, NP), (D_MODEL, BF), (D_MODEL, BF)], [], 256, f"dgate{l}")
    g["w_att_out"] = _mm(sv["att"], dyb, "tn", F32, 1024, 1024, 2048, f"dwao{l}")
    datt = _mm(dyb, w["wao"], "nt", BF, 1024, 1024, 1024, f"datt{l}")
    g["w_ssd_out"] = _mm(sv["yn"], dya, "tn", F32, 1024, 1024, 2048, f"dwso{l}")
    dyn = _mm(dya, w["wso"], "nt", F32, 1024, 1024, 1024, f"dyn{l}")
    hosted = list(hosted) + (list(early_hook(g)) if early_hook else [])
    dproj, dkp, dkc, dvp, dvc, g["att_sinks"] = _attn_bwd(proj, sv["k2"], sv["v2"], p["att_sinks"], datt, dproj, f"dattn{l}")
    res = _ssd_bwd(sv["pre"], proj, sv["dt_t"], p["dt_bias"], p["a_log"], p["d_skip_c"], p["ssd_norm_w"], sv["s_in"], dyn,
                   dproj, f"dssd{l}", (hosted, False) if hosted else None)
    (dpx, dpb, dpc, dproj, ddt_t, g["dt_bias"], g["a_log"], ddsk, g["ssd_norm_w"]) = res[:9]
    if hosted:
        g["hosted"] = res[9]
    g["d_skip"] = jnp.sum(ddsk.reshape(SSD_HEADS, 64), axis=1)
    dws, dbs = [], []
    for nm, dpart, part in (("x", dpx, 0), ("b", dpb, 4), ("c", dpc, 5)):
        dproj, dw_, db_ = _conv_bwd(dpart, proj, p["conv_w"], dproj, part, f"dconv{nm}{l}")
        dws.append(dw_)
        dbs.append(db_)
    g["conv_w"], g["conv_b"] = jnp.concatenate(dws, axis=1), jnp.concatenate(dbs, axis=1)
    length = proj.shape[0]
    tail = jnp.concatenate([_fold_heads(dkc, dkp).astype(BF), _fold_heads(dvc, dvp).astype(BF),
                            jnp.transpose(ddt_t).astype(BF), jnp.zeros((length, _PAD), BF)], axis=1)
    dproj = lax.dynamic_update_slice(dproj, tail, (0, _OFF["k"]))
    g["w_in"] = _mm(sv["hb"], dproj, "tn", F32, 1024, 2176, 1024, f"dwin{l}")
    late = list(late_hook(g)) if late_hook else []
    dh = _mm(dproj, w["win"], "nt", F32, 1024, 1024, 2176, f"dh{l}", (late, False) if late else None)
    if late:
        dh, g["late_hosted"] = dh
    return g, ds2_f, dh


def kernel(x, ln_in_g, ln_in_b, w_in, conv_w, conv_b, dt_bias, a_log, d_skip, ssd_norm_w, att_sinks, w_ssd_out, w_att_out, w_mix_out, ln_mix_g, ln_mix_b, w_ffn_gate, w_ffn_up, w_ffn_down, ln_ffn_g, ln_ffn_b, loss_target, m_ln_in_g, m_ln_in_b, m_w_in, m_conv_w, m_conv_b, m_dt_bias, m_a_log, m_d_skip, m_ssd_norm_w, m_att_sinks, m_w_ssd_out, m_w_att_out, m_w_mix_out, m_ln_mix_g, m_ln_mix_b, m_w_ffn_gate, m_w_ffn_up, m_w_ffn_down, m_ln_ffn_g, m_ln_ffn_b, v_ln_in_g, v_ln_in_b, v_w_in, v_conv_w, v_conv_b, v_dt_bias, v_a_log, v_d_skip, v_ssd_norm_w, v_att_sinks, v_w_ssd_out, v_w_att_out, v_w_mix_out, v_ln_mix_g, v_ln_mix_b, v_w_ffn_gate, v_w_ffn_up, v_w_ffn_down, v_ln_ffn_g, v_ln_ffn_b):
    env = dict(locals())
    wts = {n: env[n] for n in WEIGHTS}
    mom1 = {n: env["m_" + n] for n in WEIGHTS}
    mom2 = {n: env["v_" + n] for n in WEIGHTS}
    ci = lax.axis_index("c")
    chip = 2 * lax.axis_index("x") + lax.axis_index("y")
    xs_ = x[0]
    tgt = loss_target[0]

    COLS = ("w_in", "w_ffn_gate", "w_ffn_up")

    def half(t):
        return lax.dynamic_slice_in_dim(t, ci * (t.shape[0] // 2), t.shape[0] // 2, axis=0)

    def assemble(name, parts):
        if name == "w_in":
            return _gather_cols(parts, _IN_RANGES, _PAD)
        if name in COLS:
            return _gather_cols(parts, ((0, 4 * parts.shape[3]),))
        return jnp.concatenate([parts[h, s] for s in range(4) for h in range(2)], axis=0)

    def split(name, gfull):
        if name == "w_in":
            return _scatter_cols(gfull, _IN_RANGES, IN_DIM // 4)
        if name in COLS:
            return _scatter_cols(gfull, ((0, gfull.shape[1]),), gfull.shape[1] // 4)
        hr = gfull.shape[0] // 8
        return jnp.stack([jnp.stack([gfull[(2 * s + h) * hr:(2 * s + h + 1) * hr] for s in range(4)]) for h in range(2)])

    mine = [[half(wts[n][l].astype(BF)) for n in BIG] for l in range(DEPTH)]
    conv_all, = _with_own(_exchange([conv_w], "xy", True, "gather_conv"), [conv_w], chip)
    conv_full = jnp.transpose(conv_all, (1, 2, 0, 3)).reshape(DEPTH, SSD_CONV, CONV_DIM)

    KEYS = {"w_in": "win", "w_ssd_out": "wso", "w_att_out": "wao", "w_mix_out": "wmo", "w_ffn_down": "wd"}

    def layer_weights(l, names, by_chip):
        own = [mine[l][BIG.index(n)] for n in names]
        by_chip = _with_own(by_chip, own, chip)
        parts = _with_own(_exchange(by_chip, "c", True, f"gather_cores{l}_{names[0]}"), by_chip, ci)
        fw = {n: assemble(n, part) for n, part in zip(names, parts)}
        w = {KEYS[n]: fw[n] for n in names if n in KEYS}
        if "w_ffn_gate" in fw:
            w["wgu"] = jnp.concatenate([fw["w_ffn_gate"], fw["w_ffn_up"]], axis=1)
        return w

    layer_p = []
    for l in range(DEPTH):
        layer_p.append({
            "conv_w": conv_full[l], "conv_b": conv_b[l][None],
            "dt_bias": dt_bias[l][:, None], "a_log": a_log[l][:, None],
            "d_skip_c": jnp.repeat(d_skip[l], 64)[None], "ssd_norm_w": ssd_norm_w[l][None],
            "att_sinks": att_sinks[l][None], "ln_mix_g": ln_mix_g[l][None], "ln_mix_b": ln_mix_b[l][None],
            "ln_ffn_g": ln_ffn_g[l][None], "ln_ffn_b": ln_ffn_b[l][None],
        })

    def ln_in_fn(x_, g_, b_):
        y = _layer_norm(x_, g_, b_)
        return (y, y), ()
    (h, hb), _ = _rows(ln_in_fn, [(xs_, 0, D_MODEL)], [ln_in_g[None], ln_in_b[None]], [(D_MODEL, F32), (D_MODEL, BF)], [], 256, "ln_in")
    rest = tuple(n for n in BIG if n != "w_in")
    layer_w = [layer_weights(0, ("w_in",), _exchange([mine[0][BIG.index("w_in")]], "xy", True, "gather_chips0"))]
    behind_ssd = [i for i, n in enumerate(BIG) if n in COLS]
    behind_attn = [i for i, n in enumerate(BIG) if n not in COLS]
    saved = []
    for l in range(DEPTH):
        nxt = l + 1 < DEPTH
        sv = _layer_fwd(l, h, hb, layer_w[l], layer_p[l], ([mine[l + 1][i] for i in behind_ssd], True) if nxt else None,
                        ([mine[l + 1][i] for i in behind_attn], True) if nxt else None,
                        ([mine[l][BIG.index(n)] for n in rest], True) if l == 0 else None,
                        functools.partial(layer_weights, l, rest))
        saved.append(sv)
        if nxt:
            got = dict(zip(behind_ssd + behind_attn, sv["hosted"] + sv["hosted_attn"]))
            layer_w.append(layer_weights(l + 1, BIG, [got[i] for i in range(len(BIG))]))
            h, hb = _ln_fwd(sv["h1"], sv["ffn"], layer_p[l]["ln_ffn_g"], layer_p[l]["ln_ffn_b"], f"lnffn{l}")

    def loss_fn(a_, r_, t_, g_, b_):
        y, vjp = jax.vjp(_layer_norm, ALPHA * a_ + r_, g_, b_)
        err = y - t_
        ds, dg, db = vjp(err * (1.0 / D_MODEL))
        part = 0.5 * jnp.sum(jnp.mean(err * err, axis=-1, keepdims=True), axis=0, keepdims=True)
        return (ds, ds), (dg, db, jnp.broadcast_to(part, (1, BLK)))
    sv = saved[-1]
    (ds_f, ds_b), (dg_last, db_last, loss_part) = _rows(
        loss_fn, [(sv["h1"], 0, D_MODEL), (sv["ffn"], 0, D_MODEL), (tgt, 0, D_MODEL)],
        [layer_p[-1]["ln_ffn_g"], layer_p[-1]["ln_ffn_b"]], [(D_MODEL, F32), (D_MODEL, BF)],
        [(1, D_MODEL), (1, D_MODEL), (1, BLK)], 256, "loss")
    loss = lax.psum(loss_part[0, 0], ("x", "y", "c"))

    def reduce_cores(l, g, names):
        src = [split(n, g[n]) for n in names]
        recv = _exchange(src, "c", False, f"reduce_cores{l}_{names[0]}")
        return {n: _sum_pieces(r, o, True, ci, f"sum_cores{l}_{n}") for n, r, o in zip(names, recv, src)}

    def bf16_of(sums, names):
        return [sums[n][1] for n in names]

    def add_chips(l, sums, from_chips):
        return [_sum_pieces(from_chips[n], sums[n][0], True, chip, f"sum_chips{l}_{n}")[0] for n in BIG]

    EARLY = tuple(n for n in BIG if n != "w_in")
    grads = [None] * DEPTH
    chip_sums = [None] * DEPTH
    for l in reversed(range(DEPTH)):
        sv = saved[l]
        if l == DEPTH - 1:
            lg, lb = dg_last, db_last
        else:
            ds_f, ds_b, lg, lb = _ln_bwd(sv["h1"], sv["ffn"], d1, d2, layer_p[l]["ln_ffn_g"], layer_p[l]["ln_ffn_b"], f"dlnffn{l}")
        sums = {}

        def early_hook(g, l=l, sums=sums):
            sums.update(reduce_cores(l, g, EARLY))
            return bf16_of(sums, EARLY)

        def late_hook(g, l=l, sums=sums):
            sums.update(reduce_cores(l, g, ("w_in",)))
            return bf16_of(sums, ("w_in",))
        if l == 0:
            g, d1, d2 = _layer_bwd(l, sv, layer_w[l], layer_p[l], ds_f, ds_b, bf16_of(chip_sums[l + 1], BIG), early_hook, late_hook)
        else:
            g, d1, d2 = _layer_bwd(l, sv, layer_w[l], layer_p[l], ds_f, ds_b)
            sums.update(reduce_cores(l, g, BIG))
        g["ln_ffn_g"], g["ln_ffn_b"] = lg, lb
        grads[l] = g
        chip_sums[l] = sums

    def ln_in_bwd(x_, d1_, d2_, g_, b_):
        _, vjp = jax.vjp(_layer_norm, x_, g_, b_)
        dx, dg, db = vjp(ALPHA * d1_ + d2_)
        return (dx,), (dg, db)
    (grad_x,), (g_ln_in_g, g_ln_in_b) = _rows(
        ln_in_bwd, [(xs_, 0, D_MODEL), (d1, 0, D_MODEL), (d2, 0, D_MODEL)], [ln_in_g[None], ln_in_b[None]],
        [(D_MODEL, F32)], [(1, D_MODEL), (1, D_MODEL)], 256, "dln_in")

    carried = grads[0]["hosted"]
    from_chips = dict(zip(EARLY, carried[len(BIG):]))
    from_chips["w_in"], = grads[0]["late_hosted"]
    reduced = add_chips(0, chip_sums[0], from_chips) + add_chips(1, chip_sums[1], dict(zip(BIG, carried[:len(BIG)])))
    halves = _with_own(_exchange(reduced, "c", True, "swap_halves"), reduced, ci)
    shards = [t.reshape(2 * t.shape[1], t.shape[2]) for t in halves]
    big_grad = {n: jnp.stack([shards[l * len(BIG) + i] for l in range(DEPTH)]).reshape(wts[n].shape) for i, n in enumerate(BIG)}

    small_g = {"ln_in_g": g_ln_in_g[0], "ln_in_b": g_ln_in_b[0]}
    for n in SMALL[2:]:
        small_g[n] = jnp.stack([grads[l][n].reshape(wts[n].shape[1:] if n != "conv_w" else (SSD_CONV, CONV_DIM)) for l in range(DEPTH)])
    flat = jnp.concatenate([small_g[n].reshape(-1) for n in SMALL])
    n_small = flat.shape[0]
    width = -(-n_small // 1024) * 1024
    flat = jnp.pad(flat, (0, width - n_small)).reshape(8, width // 8)
    gathered, = _with_own(_exchange([flat], "all", True, "gather_small"), [flat], 2 * chip + ci)
    total = _sum_slots(gathered, "sum_small").reshape(-1)
    small_grad, off = {}, 0
    for n in SMALL:
        shp = small_g[n].shape
        cnt = math.prod(shp)
        small_grad[n] = total[off:off + cnt].reshape(shp)
        off += cnt
    small_grad["conv_w"] = lax.dynamic_slice_in_dim(small_grad["conv_w"], chip * (CONV_DIM // 4), CONV_DIM // 4, axis=2)

    out_g, out_d, out_m, out_v = {}, {}, {}, {}
    for n in BIG:
        shp = wts[n].shape
        two_d = lambda t: t.reshape(shp[0] * shp[1], shp[2])
        d_, m_, v_ = _adamw(two_d(wts[n]), two_d(big_grad[n]), two_d(mom1[n]), two_d(mom2[n]), f"adamw_{n}")
        out_g[n], out_d[n], out_m[n], out_v[n] = big_grad[n], d_.reshape(shp), m_.reshape(shp), v_.reshape(shp)

    def flat_small(d):
        f = jnp.concatenate([d[n].reshape(-1) for n in SMALL])
        return jnp.pad(f, (0, swidth - f.shape[0])).reshape(8, swidth // 8)
    n_sw = sum(math.prod(wts[n].shape) for n in SMALL)
    swidth = -(-n_sw // 1024) * 1024
    d_, m_, v_ = _adamw(flat_small(wts), flat_small(small_grad), flat_small(mom1), flat_small(mom2), "adamw_small")
    off = 0
    for n in SMALL:
        shp = wts[n].shape
        cnt = math.prod(shp)
        out_g[n] = small_grad[n]
        out_d[n] = d_.reshape(-1)[off:off + cnt].reshape(shp)
        out_m[n] = m_.reshape(-1)[off:off + cnt].reshape(shp)
        out_v[n] = v_.reshape(-1)[off:off + cnt].reshape(shp)
        off += cnt

    return (loss, grad_x[None], *[out_g[n] for n in WEIGHTS], *[out_d[n] for n in WEIGHTS],
            *[out_m[n] for n in WEIGHTS], *[out_v[n] for n in WEIGHTS])
```

```python
import functools
import itertools
import math

import jax
import jax.numpy as jnp
from jax import lax
from jax.experimental import pallas as pl
from jax.experimental.pallas import tpu as pltpu

F32 = jnp.float32
BF = jnp.bfloat16

D_MODEL = 1024
DEPTH = 2
ATT_HEADS = 16
ATT_HEAD_DIM = 64
BLK = 128
SSD_D_INNER = 2048
SSD_HEADS = 32
SSD_GROUPS = 4
SSD_STATE = 128
SSD_CONV = 4
BC_DIM = 512
CONV_DIM = 3072
FFN_HIDDEN = 2816
IN_DIM = 8480
LN_EPS = 1e-5
RMS_EPS = 1e-5
ALPHA = (2 * DEPTH) ** 0.25
ADAM_LR = 0.001
ADAM_B1 = 0.9
ADAM_B2 = 0.999
ADAM_EPS = 1e-08
ADAM_WD = 0.01
ADAM_STEP = 10

_PACK = (("gates", 6432, 2048), ("z", 1280, 2048), ("q", 0, 1024), ("xs", 3328, 2048), ("B", 5376, 512),
         ("C", 5888, 512), ("k", 1024, 128), ("v", 1152, 128), ("dt", 6400, 32))
NP = 8704
_OFF = {}
_o = 0
for _n, _, _w in _PACK:
    _OFF[_n] = _o
    _o += _w
_PAD = NP - _o

VMEM_LIMIT_BYTES = 56 * 1024 * 1024
ROW_TILE = 512
BIG = ("w_in", "w_ssd_out", "w_att_out", "w_mix_out", "w_ffn_gate", "w_ffn_up", "w_ffn_down")
WEIGHTS = ("ln_in_g", "ln_in_b", "w_in", "conv_w", "conv_b", "dt_bias", "a_log", "d_skip", "ssd_norm_w", "att_sinks",
           "w_ssd_out", "w_att_out", "w_mix_out", "ln_mix_g", "ln_mix_b", "w_ffn_gate", "w_ffn_up", "w_ffn_down",
           "ln_ffn_g", "ln_ffn_b")
SMALL = tuple(n for n in WEIGHTS if n not in BIG)


def _params(n_grid):
    return pltpu.CompilerParams(dimension_semantics=("arbitrary",) * n_grid, vmem_limit_bytes=VMEM_LIMIT_BYTES)


def _dot(a, b, ca, cb):
    return lax.dot_general(a.astype(BF), b.astype(BF), (((ca,), (cb,)), ((), ())), preferred_element_type=F32)


@jax.custom_vjp
def _nn(a, b):
    return _dot(a, b, 1, 0)


def _nn_f(a, b):
    return _dot(a, b, 1, 0), (a.astype(BF), b.astype(BF))


def _nn_b(res, g):
    a, b = res
    return _dot(g, b, 1, 1), _dot(a, g, 0, 0)


_nn.defvjp(_nn_f, _nn_b)


@jax.custom_vjp
def _nt(a, b):
    return _dot(a, b, 1, 1)


def _nt_f(a, b):
    return _dot(a, b, 1, 1), (a.astype(BF), b.astype(BF))


def _nt_b(res, g):
    a, b = res
    return _dot(g, b, 1, 0), _dot(g, a, 0, 0)


_nt.defvjp(_nt_f, _nt_b)


@jax.custom_vjp
def _tn(a, b):
    return _dot(a, b, 0, 0)


def _tn_f(a, b):
    return _dot(a, b, 0, 0), (a.astype(BF), b.astype(BF))


def _tn_b(res, g):
    a, b = res
    return _dot(b, g, 1, 1), _dot(a, g, 1, 0)


_tn.defvjp(_tn_f, _tn_b)


def _sigmoid(x):
    return 0.5 * jnp.tanh(0.5 * x) + 0.5


def _silu(x):
    return x * _sigmoid(x)


def _layer_norm(s, g, b):
    mu = jnp.mean(s, axis=-1, keepdims=True)
    sc = s - mu
    var = jnp.mean(sc * sc, axis=-1, keepdims=True)
    return sc * lax.rsqrt(var + LN_EPS) * g + b


def _ssd_group(pre_x, pre_b, pre_c, z, dtr, dtb, alog, dsk, nw, state):
    t = pre_x.shape[0]
    xs, bm, cm = _silu(pre_x), _silu(pre_b), _silu(pre_c)
    dt = jax.nn.softplus(dtr + dtb)
    da = dt * (-jnp.exp(alog))
    row = lax.broadcasted_iota(jnp.int32, (t, t), 0)
    col = lax.broadcasted_iota(jnp.int32, (t, t), 1)
    upper = (row <= col).astype(F32)
    cum = jnp.dot(da, upper, precision=lax.Precision.HIGHEST, preferred_element_type=F32)
    tot = jnp.sum(da, axis=1, keepdims=True)
    cb = _nt(cm, bm)
    tril = row >= col
    first = lax.broadcasted_iota(jnp.int32, (t, BLK), 1) < 64
    first_row = lax.broadcasted_iota(jnp.int32, (1, BLK), 1) < 64

    def col_form(r):
        return jnp.broadcast_to(r, (t, t)).T

    ys, new_state = [], []
    for p in range(4):
        h0, h1 = 2 * p, 2 * p + 1
        sl = slice(BLK * p, BLK * (p + 1))
        x_pair = xs[:, sl] * jnp.where(first, col_form(dt[h0:h0 + 1]), col_form(dt[h1:h1 + 1]))
        y_pair = None
        cc = []
        for h, keep in ((h0, first), (h1, jnp.logical_not(first))):
            cr = jnp.broadcast_to(cum[h:h + 1], (t, t))
            cc.append(cr.T)
            decay = jnp.exp(jnp.where(tril, cc[-1] - cr, -1e30))
            y_h = _nn(decay * cb, jnp.where(keep, x_pair, 0.0))
            y_pair = y_h if y_pair is None else y_pair + y_h
        s_pair = state[:, sl]
        y_pair = y_pair + _nn(cm, s_pair) * jnp.where(first, jnp.exp(cc[0]), jnp.exp(cc[1]))
        to_end = jnp.where(first, jnp.exp(tot[h0:h0 + 1] - cc[0]), jnp.exp(tot[h1:h1 + 1] - cc[1]))
        chunk_decay = jnp.where(first_row, jnp.exp(tot[h0:h0 + 1]), jnp.exp(tot[h1:h1 + 1]))
        new_state.append(s_pair * chunk_decay + _tn(bm, x_pair * to_end))
        ys.append(y_pair + dsk[:, sl] * xs[:, sl])
    y = jnp.concatenate(ys, axis=1) * _silu(z)
    y = y * lax.rsqrt(jnp.mean(y * y, axis=-1, keepdims=True) + RMS_EPS) * nw
    return y, jnp.concatenate(new_state, axis=1)


def _attn_block(q, kp, kc, vp, vc, sinks, n, kv):
    t = q.shape[0]
    kb = jnp.concatenate([kp, kc], axis=0)
    vb = jnp.concatenate([vp, vc], axis=0)
    qi = lax.broadcasted_iota(jnp.int32, (t, 2 * t), 0)
    kj = lax.broadcasted_iota(jnp.int32, (t, 2 * t), 1)
    rel = qi + t - kj
    valid = (rel >= 0) & (rel < t) & ((n * t - t + kj) >= 0)
    relf = rel.astype(F32)
    first = lax.broadcasted_iota(jnp.int32, (t, BLK), 1) < 64
    lane16 = lax.broadcasted_iota(jnp.int32, (1, ATT_HEADS), 1)
    outs = []
    for p in range(4):
        qp = q[:, BLK * p:BLK * (p + 1)] * (ATT_HEAD_DIM ** -0.5)
        o_pair = None
        for half, keep in enumerate((first, jnp.logical_not(first))):
            h = kv * 8 + 2 * p + half
            s = _nt(jnp.where(keep, qp, 0.0), kb)
            slope = jnp.exp((-8.0 * math.log(2.0) / ATT_HEADS) * jnp.asarray(h + 1, F32))
            s = jnp.where(valid, s - slope * relf, -1e30)
            sink = jnp.sum(jnp.where(lane16 == h, sinks, 0.0), axis=1, keepdims=True)
            m = lax.stop_gradient(jnp.maximum(jnp.max(s, axis=1, keepdims=True), sink))
            e = jnp.exp(s - m)
            den = jnp.sum(e, axis=1, keepdims=True) + jnp.exp(sink - m)
            o = _nn(e * (1.0 / den), vb)
            o_pair = o if o_pair is None else jnp.where(first, o_pair, o)
        outs.append(o_pair)
    return jnp.concatenate(outs, axis=1)


def _chip_exchange(src, out, send_sems, recv_sems, bcast):
    x, y, c = lax.axis_index("x"), lax.axis_index("y"), lax.axis_index("c")
    me = 2 * x + y

    def copy(a, d, slot):
        j = jnp.bitwise_xor(me, d)
        return pltpu.make_async_remote_copy(
            src_ref=src[a] if bcast else src[a].at[j], dst_ref=out[a].at[slot],
            send_sem=send_sems.at[a * 4 + d], recv_sem=recv_sems.at[a * 4 + d],
            device_id=(j // 2, j % 2, c), device_id_type=pl.DeviceIdType.MESH)

    pairs = [(a, d) for a in range(len(src)) for d in range(1, 4)]

    def start():
        for a, d in pairs:
            copy(a, d, me).start()

    def wait():
        for a, d in pairs:
            copy(a, d, jnp.bitwise_xor(me, d)).wait_recv()
        for a, d in pairs:
            copy(a, d, me).wait_send()

    return start, wait


def _mm(a, b, mode, out_dtype, tm, tn, tk, name, hosted=None):
    if mode == "nn":
        (m, k), (k2, n) = a.shape, b.shape
    elif mode == "nt":
        (m, k), (n, k2) = a.shape, b.shape
    else:
        (k, m), (k2, n) = a.shape, b.shape
    assert k == k2, (a.shape, b.shape, mode)
    tm, tn, tk = min(tm, m), min(tn, n), min(tk, k)
    assert m % tm == 0 and n % tn == 0 and k % tk == 0, (m, n, k, tm, tn, tk)
    nk = k // tk
    grid = (m // tm, n // tn, nk)
    ca, cb = {"nn": (1, 0), "nt": (1, 1), "tn": (0, 0)}[mode]
    h_arrays, h_bcast = hosted if hosted else ((), False)
    nh = len(h_arrays)

    def body(*refs):
        a_ref, b_ref = refs[:2]
        o_ref = refs[2 + nh]
        scratch = refs[3 + 2 * nh:]
        step = [pl.program_id(ax) for ax in range(3)]
        if nh:
            start, wait = _chip_exchange(refs[2:2 + nh], refs[3 + nh:3 + 2 * nh], scratch[-2], scratch[-1], h_bcast)

            @pl.when((step[0] == 0) & (step[1] == 0) & (step[2] == 0))
            def _():
                start()

        part = _dot(a_ref[...], b_ref[...], ca, cb)
        if nk == 1:
            o_ref[...] = part.astype(o_ref.dtype)
        else:
            acc_ref = scratch[0]

            @pl.when(step[2] == 0)
            def _():
                acc_ref[...] = part

            @pl.when(step[2] > 0)
            def _():
                acc_ref[...] += part

            @pl.when(step[2] == nk - 1)
            def _():
                o_ref[...] = acc_ref[...].astype(o_ref.dtype)

        if nh:
            @pl.when((step[0] == grid[0] - 1) & (step[1] == grid[1] - 1) & (step[2] == nk - 1))
            def _():
                wait()

    a_spec = pl.BlockSpec((tk, tm), lambda i, j, kk: (kk, i)) if mode == "tn" else pl.BlockSpec((tm, tk), lambda i, j, kk: (i, kk))
    b_spec = pl.BlockSpec((tn, tk), lambda i, j, kk: (j, kk)) if mode == "nt" else pl.BlockSpec((tk, tn), lambda i, j, kk: (kk, j))
    any_spec = pl.BlockSpec(memory_space=pl.ANY)
    res = pl.pallas_call(
        body, name=name, grid=grid,
        in_specs=[a_spec, b_spec] + [any_spec] * nh,
        out_specs=[pl.BlockSpec((tm, tn), lambda i, j, kk: (i, j))] + [any_spec] * nh,
        out_shape=[jax.ShapeDtypeStruct((m, n), out_dtype)]
        + [jax.ShapeDtypeStruct(((4,) + s.shape) if h_bcast else s.shape, s.dtype) for s in h_arrays],
        scratch_shapes=([] if nk == 1 else [pltpu.VMEM((tm, tn), F32)])
        + ([pltpu.SemaphoreType.DMA((4 * nh,)), pltpu.SemaphoreType.DMA((4 * nh,))] if nh else []),
        compiler_params=_params(3),
    )(a, b, *h_arrays)
    return (res[0], list(res[1:])) if nh else res[0]


def _rows(fn, rows, params, outs, accs, tile, name):
    length = rows[0][0].shape[0]
    tile = min(tile, length)
    assert length % tile == 0
    nr, npar, no = len(rows), len(params), len(outs)

    def body(*refs):
        vals = [r[...] for r in refs[:nr + npar]]
        o, a = fn(*vals)
        for ref, val in zip(refs[nr + npar:nr + npar + no], o):
            ref[...] = val.astype(ref.dtype)
        i = pl.program_id(0)
        for ref, val in zip(refs[nr + npar + no:], a):
            @pl.when(i == 0)
            def _(ref=ref, val=val):
                ref[...] = val

            @pl.when(i > 0)
            def _(ref=ref, val=val):
                ref[...] += val

    in_specs = [pl.BlockSpec((tile, w), functools.partial(lambda i, cb: (i, cb), cb=cb)) for _, cb, w in rows]
    in_specs += [pl.BlockSpec(p.shape, lambda i: (0, 0)) for p in params]
    outs = [o if len(o) == 3 else (o[0], o[1], o[0]) for o in outs]
    out_specs = [pl.BlockSpec((tile, w), lambda i: (i, 0)) for w, _, _ in outs]
    out_specs += [pl.BlockSpec((r, w), lambda i: (0, 0)) for r, w in accs]
    out_shape = [jax.ShapeDtypeStruct((length, full), dt) for _, dt, full in outs]
    out_shape += [jax.ShapeDtypeStruct((r, w), F32) for r, w in accs]
    res = pl.pallas_call(
        body, name=name, grid=(length // tile,), in_specs=in_specs, out_specs=out_specs, out_shape=out_shape,
        compiler_params=_params(1),
    )(*[r[0] for r in rows], *params)
    return res[:no], res[no:]


CONV_SUB = 32


def _conv_fwd(proj, conv_w, conv_b, name, hosted=None):
    length = proj.shape[0]
    tl, cw = min(512, length), 512
    cb0 = _OFF["xs"] // cw
    grid = (CONV_DIM // cw, length // tl)
    h_arrays, h_bcast = hosted if hosted else ((), False)
    nh = len(h_arrays)

    def body(*refs):
        u_ref, halo_ref, w_ref, b_ref = refs[:4]
        o_ref = refs[4 + nh]
        win = refs[5 + 2 * nh]
        j, i = pl.program_id(0), pl.program_id(1)
        if nh:
            start, wait = _chip_exchange(refs[4:4 + nh], refs[5 + nh:5 + 2 * nh], refs[-2], refs[-1], h_bcast)

            @pl.when((j == 0) & (i == 0))
            def _():
                start()

            @pl.when((j == grid[0] - 1) & (i == grid[1] - 1))
            def _():
                wait()

        win[0:8, :] = jnp.where(i > 0, halo_ref[...], 0.0)
        win[8:8 + tl, :] = u_ref[...]
        taps = [w_ref[kk:kk + 1, :] for kk in range(SSD_CONV)]
        bias = b_ref[...]
        for r in range(0, tl, CONV_SUB):
            acc = bias + taps[0] * win[5 + r:5 + r + CONV_SUB, :]
            for kk in range(1, SSD_CONV):
                acc = acc + taps[kk] * win[5 + kk + r:5 + kk + r + CONV_SUB, :]
            o_ref[r:r + CONV_SUB, :] = acc

    any_spec = pl.BlockSpec(memory_space=pl.ANY)
    res = pl.pallas_call(
        body, name=name, grid=grid,
        in_specs=[pl.BlockSpec((tl, cw), lambda j, i: (i, cb0 + j)),
                  pl.BlockSpec((8, cw), lambda j, i: (jnp.maximum(i * (tl // 8) - 1, 0), cb0 + j)),
                  pl.BlockSpec((SSD_CONV, cw), lambda j, i: (0, j)),
                  pl.BlockSpec((1, cw), lambda j, i: (0, j))] + [any_spec] * nh,
        out_specs=[pl.BlockSpec((tl, cw), lambda j, i: (i, j))] + [any_spec] * nh,
        out_shape=[jax.ShapeDtypeStruct((length, CONV_DIM), F32)]
        + [jax.ShapeDtypeStruct(((4,) + s.shape) if h_bcast else s.shape, s.dtype) for s in h_arrays],
        scratch_shapes=[pltpu.VMEM((8 + tl, cw), F32)]
        + ([pltpu.SemaphoreType.DMA((4 * nh,)), pltpu.SemaphoreType.DMA((4 * nh,))] if nh else []),
        compiler_params=_params(2),
    )(proj, proj, conv_w, conv_b, *h_arrays)
    return (res[0], list(res[1:])) if nh else res[0]


def _conv_bwd(dpre, proj, conv_w, dproj, part, name):
    length = proj.shape[0]
    width = dpre.shape[1]
    tl, cw = min(512, length), 512
    cb0 = _OFF["xs"] // cw + part
    n_t = length // tl

    def body(d_ref, dnext_ref, u_ref, halo_ref, w_ref, _, du_ref, dw_ref, db_ref, dwin, uwin):
        i = pl.program_id(1)
        dwin[0:tl, :] = d_ref[...]
        dwin[tl:tl + 8, :] = jnp.where(i < n_t - 1, dnext_ref[...], 0.0)
        uwin[0:8, :] = jnp.where(i > 0, halo_ref[...], 0.0)
        uwin[8:8 + tl, :] = u_ref[...]
        taps = [w_ref[kk:kk + 1, :] for kk in range(SSD_CONV)]
        sub = CONV_SUB // 2
        acc_w = [jnp.zeros((sub, cw), F32) for _ in range(SSD_CONV)]
        acc_b = jnp.zeros((sub, cw), F32)
        for r in range(0, tl, sub):
            d = dwin[r:r + sub, :]
            du = taps[3] * d
            for kk in range(SSD_CONV - 1):
                du = du + taps[kk] * dwin[3 - kk + r:3 - kk + r + sub, :]
            du_ref[r:r + sub, :] = du.astype(du_ref.dtype)
            for kk in range(SSD_CONV):
                acc_w[kk] = acc_w[kk] + d * uwin[5 + kk + r:5 + kk + r + sub, :]
            acc_b = acc_b + d
        dw = jnp.concatenate([jnp.sum(a, axis=0, keepdims=True) for a in acc_w], axis=0)
        db = jnp.sum(acc_b, axis=0, keepdims=True)

        @pl.when(i == 0)
        def _():
            dw_ref[...] = dw
            db_ref[...] = db

        @pl.when(i > 0)
        def _():
            dw_ref[...] += dw
            db_ref[...] += db

    return pl.pallas_call(
        body, name=name, grid=(width // cw, n_t), input_output_aliases={5: 0},
        in_specs=[pl.BlockSpec((tl, cw), lambda j, i: (i, j)),
                  pl.BlockSpec((8, cw), lambda j, i: (jnp.minimum((i + 1) * (tl // 8), length // 8 - 1), j)),
                  pl.BlockSpec((tl, cw), lambda j, i: (i, cb0 + j)),
                  pl.BlockSpec((8, cw), lambda j, i: (jnp.maximum(i * (tl // 8) - 1, 0), cb0 + j)),
                  pl.BlockSpec((SSD_CONV, cw), lambda j, i: (0, part + j)),
                  pl.BlockSpec(memory_space=pl.ANY)],
        out_specs=[pl.BlockSpec((tl, cw), lambda j, i: (i, cb0 + j)),
                   pl.BlockSpec((SSD_CONV, cw), lambda j, i: (0, j)),
                   pl.BlockSpec((1, cw), lambda j, i: (0, j))],
        out_shape=[jax.ShapeDtypeStruct((length, NP), BF),
                   jax.ShapeDtypeStruct((SSD_CONV, width), F32),
                   jax.ShapeDtypeStruct((1, width), F32)],
        scratch_shapes=[pltpu.VMEM((tl + 8, cw), F32), pltpu.VMEM((8 + tl, cw), F32)],
        compiler_params=_params(2),
    )(dpre, dpre, proj, proj, conv_w, dproj)


def _ssd_in_specs(rev, nc):
    def cidx(c):
        return nc - 1 - c if rev else c
    whole = lambda c: (0, 0)
    return [pl.BlockSpec((BLK, SSD_D_INNER), lambda c: (cidx(c), 0)),
            pl.BlockSpec((BLK, BC_DIM), lambda c: (cidx(c), SSD_D_INNER // BC_DIM)),
            pl.BlockSpec((BLK, BC_DIM), lambda c: (cidx(c), SSD_D_INNER // BC_DIM + 1)),
            pl.BlockSpec((BLK, SSD_D_INNER), lambda c: (cidx(c), _OFF["z"] // SSD_D_INNER)),
            pl.BlockSpec((SSD_HEADS, BLK), lambda c: (0, cidx(c))),
            pl.BlockSpec((SSD_HEADS, 1), whole),
            pl.BlockSpec((SSD_HEADS, 1), whole),
            pl.BlockSpec((1, SSD_D_INNER), whole),
            pl.BlockSpec((1, SSD_D_INNER), whole)]


def _group_args(g, px, pb, pc, z, dtr, dtb, alog, dsk, nw):
    wide, narrow, heads = slice(512 * g, 512 * (g + 1)), slice(BLK * g, BLK * (g + 1)), slice(8 * g, 8 * (g + 1))
    return (px[:, wide], pb[:, narrow], pc[:, narrow], z[:, wide], dtr[heads, :], dtb[heads, :], alog[heads, :],
            dsk[:, wide], nw[:, wide])


def _ssd_fwd(pre, proj, dt_t, dtb, alog, dsk, nw, name, hosted=None):
    length = pre.shape[0]
    nc = length // BLK
    h_arrays, h_bcast = hosted if hosted else ((), False)
    nh = len(h_arrays)

    def body(*refs):
        px, pb, pc, z, dtr, dtb_r, al_r, dsk_r, nw_r = refs[:9]
        y_ref, sin_ref = refs[9 + nh:11 + nh]
        st = refs[11 + 2 * nh]
        c = pl.program_id(0)
        if nh:
            start, wait = _chip_exchange(refs[9:9 + nh], refs[11 + nh:11 + 2 * nh], refs[-2], refs[-1], h_bcast)

        @pl.when(c == 0)
        def _():
            st[...] = jnp.zeros_like(st)
            if nh:
                start()

        for g in range(SSD_GROUPS):
            s_in = st[g]
            sin_ref[g] = s_in
            y, s_out = _ssd_group(*_group_args(g, px, pb, pc, z, dtr, dtb_r, al_r, dsk_r, nw_r), s_in)
            y_ref[:, 512 * g:512 * (g + 1)] = y.astype(y_ref.dtype)
            st[g] = s_out

        if nh:
            @pl.when(c == nc - 1)
            def _():
                wait()

    any_spec = pl.BlockSpec(memory_space=pl.ANY)
    res = pl.pallas_call(
        body, name=name, grid=(nc,), in_specs=_ssd_in_specs(False, nc) + [any_spec] * nh,
        out_specs=[pl.BlockSpec((BLK, SSD_D_INNER), lambda c: (c, 0)),
                   pl.BlockSpec((SSD_GROUPS, None, SSD_STATE, 512), lambda c: (0, c, 0, 0))] + [any_spec] * nh,
        out_shape=[jax.ShapeDtypeStruct((length, SSD_D_INNER), BF),
                   jax.ShapeDtypeStruct((SSD_GROUPS, nc, SSD_STATE, 512), F32)]
        + [jax.ShapeDtypeStruct(((4,) + s.shape) if h_bcast else s.shape, s.dtype) for s in h_arrays],
        scratch_shapes=[pltpu.VMEM((SSD_GROUPS, SSD_STATE, 512), F32)]
        + ([pltpu.SemaphoreType.DMA((4 * nh,)), pltpu.SemaphoreType.DMA((4 * nh,))] if nh else []),
        compiler_params=_params(1),
    )(pre, pre, pre, proj, dt_t, dtb, alog, dsk, nw, *h_arrays)
    return (res[0], res[1], list(res[2:])) if nh else (res[0], res[1])


def _ssd_bwd(pre, proj, dt_t, dtb, alog, dsk, nw, s_in, dy, dproj, name, hosted=None):
    length = pre.shape[0]
    nc = length // BLK
    h_arrays, h_bcast = hosted if hosted else ((), False)
    nh = len(h_arrays)

    def body(*refs):
        px, pb, pc, z, dtr, dtb_r, al_r, dsk_r, nw_r, sin_r, dy_r = refs[:11]
        dpx, dpb, dpc, dz, ddt, ddtb, dal, ddsk, dnw = refs[12 + nh:21 + nh]
        dst = refs[21 + 2 * nh]
        c = pl.program_id(0)
        if nh:
            start, wait = _chip_exchange(refs[12:12 + nh], refs[21 + nh:21 + 2 * nh], refs[-2], refs[-1], h_bcast)

        @pl.when(c == 0)
        def _():
            dst[...] = jnp.zeros_like(dst)
            if nh:
                start()

        for grp in range(SSD_GROUPS):
            wide, narrow, heads = slice(512 * grp, 512 * (grp + 1)), slice(BLK * grp, BLK * (grp + 1)), slice(8 * grp, 8 * (grp + 1))
            _, vjp = jax.vjp(_ssd_group, *_group_args(grp, px, pb, pc, z, dtr, dtb_r, al_r, dsk_r, nw_r), sin_r[grp])
            g = vjp((dy_r[:, wide], dst[grp]))
            dpx[:, wide] = g[0]
            dpb[:, narrow] = g[1]
            dpc[:, narrow] = g[2]
            dz[:, wide] = g[3].astype(dz.dtype)
            ddt[heads, :] = g[4]
            dst[grp] = g[9]
            for ref, val, idx in ((ddtb, g[5], (heads, slice(None))), (dal, g[6], (heads, slice(None))),
                                  (ddsk, g[7], (slice(None), wide)), (dnw, g[8], (slice(None), wide))):
                @pl.when(c == 0)
                def _(ref=ref, val=val, idx=idx):
                    ref[idx] = val

                @pl.when(c > 0)
                def _(ref=ref, val=val, idx=idx):
                    ref[idx] += val

        if nh:
            @pl.when(c == nc - 1)
            def _():
                wait()

    rev = lambda c: nc - 1 - c
    whole = lambda c: (0, 0)
    any_spec = pl.BlockSpec(memory_space=pl.ANY)
    in_specs = _ssd_in_specs(True, nc) + [
        pl.BlockSpec((SSD_GROUPS, None, SSD_STATE, 512), lambda c: (0, rev(c), 0, 0)),
        pl.BlockSpec((BLK, SSD_D_INNER), lambda c: (rev(c), 0)),
        any_spec] + [any_spec] * nh
    res = pl.pallas_call(
        body, name=name, grid=(nc,), in_specs=in_specs, input_output_aliases={11: 3},
        out_specs=[pl.BlockSpec((BLK, SSD_D_INNER), lambda c: (rev(c), 0)),
                   pl.BlockSpec((BLK, BC_DIM), lambda c: (rev(c), 0)),
                   pl.BlockSpec((BLK, BC_DIM), lambda c: (rev(c), 0)),
                   pl.BlockSpec((BLK, SSD_D_INNER), lambda c: (rev(c), _OFF["z"] // SSD_D_INNER)),
                   pl.BlockSpec((SSD_HEADS, BLK), lambda c: (0, rev(c))),
                   pl.BlockSpec((SSD_HEADS, 1), whole),
                   pl.BlockSpec((SSD_HEADS, 1), whole),
                   pl.BlockSpec((1, SSD_D_INNER), whole),
                   pl.BlockSpec((1, SSD_D_INNER), whole)] + [any_spec] * nh,
        out_shape=[jax.ShapeDtypeStruct((length, SSD_D_INNER), F32),
                   jax.ShapeDtypeStruct((length, BC_DIM), F32),
                   jax.ShapeDtypeStruct((length, BC_DIM), F32),
                   jax.ShapeDtypeStruct((length, NP), BF),
                   jax.ShapeDtypeStruct((SSD_HEADS, length), F32),
                   jax.ShapeDtypeStruct((SSD_HEADS, 1), F32),
                   jax.ShapeDtypeStruct((SSD_HEADS, 1), F32),
                   jax.ShapeDtypeStruct((1, SSD_D_INNER), F32),
                   jax.ShapeDtypeStruct((1, SSD_D_INNER), F32)]
        + [jax.ShapeDtypeStruct(((4,) + s.shape) if h_bcast else s.shape, s.dtype) for s in h_arrays],
        scratch_shapes=[pltpu.VMEM((SSD_GROUPS, SSD_STATE, 512), F32)]
        + ([pltpu.SemaphoreType.DMA((4 * nh,)), pltpu.SemaphoreType.DMA((4 * nh,))] if nh else []),
        compiler_params=_params(1),
    )(pre, pre, pre, proj, dt_t, dtb, alog, dsk, nw, s_in, dy, dproj, *h_arrays)
    return tuple(res[:9]) + ((list(res[9:]),) if nh else ())


assert _OFF["gates"] == 0 and _OFF["z"] % SSD_D_INNER == 0 and _OFF["q"] % D_MODEL == 0 and _OFF["k"] + 512 == NP


def _attn_in_specs():
    prev = lambda n: (0, jnp.maximum(n - 1, 0), 0)
    cur = lambda n: (0, n, 0)
    return [pl.BlockSpec((BLK, D_MODEL), lambda n: (n, _OFF["q"] // D_MODEL)),
            pl.BlockSpec((2, BLK, BLK), prev),
            pl.BlockSpec((2, BLK, BLK), cur),
            pl.BlockSpec((2, BLK, BLK), prev),
            pl.BlockSpec((2, BLK, BLK), cur),
            pl.BlockSpec((1, ATT_HEADS), lambda n: (0, 0))]


def _attn_fwd(proj, k2, v2, sinks, name, hosted=None):
    length = proj.shape[0]
    nb = length // BLK
    h_arrays, h_bcast = hosted if hosted else ((), False)
    nh = len(h_arrays)

    def body(*refs):
        q, kp, kc, vp, vc, sk = refs[:6]
        o_ref = refs[6 + nh]
        n = pl.program_id(0)
        if nh:
            start, wait = _chip_exchange(refs[6:6 + nh], refs[7 + nh:7 + 2 * nh], refs[-2], refs[-1], h_bcast)

            @pl.when(n == 0)
            def _():
                start()

        for kv in range(2):
            cols = slice(512 * kv, 512 * (kv + 1))
            o = _attn_block(q[:, cols], kp[kv], kc[kv], vp[kv], vc[kv], sk[...], n, kv)
            o_ref[:, cols] = o.astype(o_ref.dtype)

        if nh:
            @pl.when(n == nb - 1)
            def _():
                wait()

    any_spec = pl.BlockSpec(memory_space=pl.ANY)
    res = pl.pallas_call(
        body, name=name, grid=(nb,), in_specs=_attn_in_specs() + [any_spec] * nh,
        out_specs=[pl.BlockSpec((BLK, D_MODEL), lambda n: (n, 0))] + [any_spec] * nh,
        out_shape=[jax.ShapeDtypeStruct((length, D_MODEL), BF)]
        + [jax.ShapeDtypeStruct(((4,) + s.shape) if h_bcast else s.shape, s.dtype) for s in h_arrays],
        scratch_shapes=[pltpu.SemaphoreType.DMA((4 * nh,)), pltpu.SemaphoreType.DMA((4 * nh,))] if nh else [],
        compiler_params=_params(1),
    )(proj, k2, k2, v2, v2, sinks, *h_arrays)
    return (res[0], list(res[1:])) if nh else res[0]


def _attn_bwd(proj, k2, v2, sinks, datt, dproj, name):
    length = proj.shape[0]

    def body(q, kp, kc, vp, vc, sk, do, _, dq, dkp, dkc, dvp, dvc, dsk):
        n = pl.program_id(0)
        dsinks = None
        for kv in range(2):
            cols = slice(512 * kv, 512 * (kv + 1))
            _, vjp = jax.vjp(lambda *a: _attn_block(*a, n, kv), q[:, cols], kp[kv], kc[kv], vp[kv], vc[kv], sk[...])
            g = vjp(do[:, cols].astype(F32))
            dq[:, cols] = g[0].astype(dq.dtype)
            dkp[kv] = g[1]
            dkc[kv] = g[2]
            dvp[kv] = g[3]
            dvc[kv] = g[4]
            dsinks = g[5] if dsinks is None else dsinks + g[5]

        @pl.when(n == 0)
        def _():
            dsk[...] = dsinks

        @pl.when(n > 0)
        def _():
            dsk[...] += dsinks

    blk3 = pl.BlockSpec((2, BLK, BLK), lambda n: (0, n, 0))
    kv_shape = jax.ShapeDtypeStruct((2, length, BLK), F32)
    return pl.pallas_call(
        body, name=name, grid=(length // BLK,),
        in_specs=_attn_in_specs() + [pl.BlockSpec((BLK, D_MODEL), lambda n: (n, 0)), pl.BlockSpec(memory_space=pl.ANY)],
        out_specs=[pl.BlockSpec((BLK, D_MODEL), lambda n: (n, _OFF["q"] // D_MODEL)), blk3, blk3, blk3, blk3,
                   pl.BlockSpec((1, ATT_HEADS), lambda n: (0, 0))],
        out_shape=[jax.ShapeDtypeStruct((length, NP), BF), kv_shape, kv_shape, kv_shape, kv_shape,
                   jax.ShapeDtypeStruct((1, ATT_HEADS), F32)],
        input_output_aliases={7: 0},
        compiler_params=_params(1),
    )(proj, k2, k2, v2, v2, sinks, datt, dproj)


DMA_CHUNK_BYTES = 1 << 20
N_STAGE = 8
LOOKAHEAD = 3


def _piece_chunks(shape, itemsize):
    if len(shape) < 2 or shape[-2] % 16 != 0:
        return [()]
    rows, cols = shape[-2:]
    step = min(rows, max(16, DMA_CHUNK_BYTES // (cols * itemsize) // 16 * 16))
    out = []
    for lead in itertools.product(*[range(d) for d in shape[:-2]]):
        for r0 in range(0, rows, step):
            out.append(lead + (pl.ds(r0, min(step, rows - r0)),))
    return out


def _exchange(srcs, group, bcast, name):
    size = {"c": 2, "xy": 4, "all": 8}[group]
    npeer = size - 1
    n = len(srcs)
    for s in srcs:
        assert bcast or s.shape[0] == size
    pieces = [s.shape if bcast else s.shape[1:] for s in srcs]
    chunks = [_piece_chunks(p, s.dtype.itemsize) for p, s in zip(pieces, srcs)]

    def body(*refs):
        src, out = refs[:n], refs[n:2 * n]
        load_sems, send_sems, recv_sems = refs[2 * n:]
        x, y, c = lax.axis_index("x"), lax.axis_index("y"), lax.axis_index("c")
        if group == "c":
            me = c
        elif group == "xy":
            me = 2 * x + y
        else:
            me = 4 * x + 2 * y + c

        def device(j):
            if group == "c":
                return (x, y, j)
            if group == "xy":
                return (j // 2, j % 2, c)
            return (j // 4, (j // 2) % 2, j % 2)

        def part(ref, idx):
            return ref.at[idx] if idx else ref

        def piece(a, j):
            return src[a] if bcast else src[a].at[j]

        for a in range(n):
            jobs = [(None, idx) for idx in chunks[a]] if bcast else [(d, idx) for idx in chunks[a] for d in range(1, size)]
            full_rows = chunks[a][0][-1].size if chunks[a][0] else None
            slot_shape = (full_rows, pieces[a][-1]) if chunks[a][0] else pieces[a]

            def stream(stage, a=a, jobs=jobs, full_rows=full_rows):
                def slot(q):
                    idx = jobs[q][1]
                    view = stage.at[q % N_STAGE]
                    if idx and idx[-1].size != full_rows:
                        view = view.at[pl.ds(0, idx[-1].size)]
                    return view

                def load(q):
                    d, idx = jobs[q]
                    j = me if d is None else jnp.bitwise_xor(me, d)
                    return pltpu.make_async_copy(part(piece(a, j), idx), slot(q), load_sems.at[q % N_STAGE])

                def sends(q):
                    d, idx = jobs[q]
                    return [pltpu.make_async_remote_copy(
                        src_ref=slot(q), dst_ref=part(out[a].at[me], idx),
                        send_sem=send_sems.at[(q % N_STAGE) * npeer + dd - 1], recv_sem=recv_sems.at[a * size + dd],
                        device_id=device(jnp.bitwise_xor(me, dd)), device_id_type=pl.DeviceIdType.MESH)
                        for dd in (range(1, size) if d is None else (d,))]

                nq = len(jobs)
                for q in range(nq + LOOKAHEAD):
                    if q < nq:
                        if q >= N_STAGE:
                            for cp in sends(q - N_STAGE):
                                cp.wait_send()
                        load(q).start()
                    if q >= LOOKAHEAD:
                        load(q - LOOKAHEAD).wait()
                        for cp in sends(q - LOOKAHEAD):
                            cp.start()
                for q in range(max(0, nq - N_STAGE), nq):
                    for cp in sends(q):
                        cp.wait_send()

            pl.run_scoped(stream, pltpu.VMEM((N_STAGE,) + tuple(slot_shape), srcs[a].dtype))

        for a in range(n):
            for d in range(1, size):
                j = jnp.bitwise_xor(me, d)
                pltpu.make_async_remote_copy(
                    src_ref=piece(a, j), dst_ref=out[a].at[j], send_sem=send_sems.at[0], recv_sem=recv_sems.at[a * size + d],
                    device_id=device(j), device_id_type=pl.DeviceIdType.MESH).wait_recv()

    any_spec = pl.BlockSpec(memory_space=pl.ANY)
    return pl.pallas_call(
        body, name=name, in_specs=[any_spec] * n, out_specs=[any_spec] * n,
        out_shape=[jax.ShapeDtypeStruct((size,) + tuple(p), s.dtype) for p, s in zip(pieces, srcs)],
        scratch_shapes=[pltpu.SemaphoreType.DMA((N_STAGE,)), pltpu.SemaphoreType.DMA((N_STAGE * npeer,)),
                        pltpu.SemaphoreType.DMA((n * size,))],
        compiler_params=pltpu.CompilerParams(vmem_limit_bytes=VMEM_LIMIT_BYTES),
    )(*srcs)


def _with_own(outs, owns, me):
    return [lax.dynamic_update_index_in_dim(o, w, me, 0) for o, w in zip(outs, owns)]


def _sum_slots(arr, name):
    k = arr.shape[0]
    rest = arr.shape[1:]
    width = rest[-1]
    rows_per = math.prod(rest[:-1])
    a2 = arr.reshape(k * rows_per, width)
    tile = rows_per
    for cand in (256, 128, 64, 32, 16, 8):
        if rows_per % cand == 0:
            tile = cand
            break
    nt = rows_per // tile

    def body(*refs):
        acc = refs[0][...]
        for r in refs[1:k]:
            acc = acc + r[...]
        refs[k][...] = acc

    in_specs = [pl.BlockSpec((tile, width), functools.partial(lambda i, s: (s * nt + i, 0), s=s)) for s in range(k)]
    out = pl.pallas_call(
        body, name=name, grid=(nt,), in_specs=in_specs, out_specs=pl.BlockSpec((tile, width), lambda i: (i, 0)),
        out_shape=jax.ShapeDtypeStruct((rows_per, width), arr.dtype), compiler_params=_params(1),
    )(*([a2] * k))
    return out.reshape(rest)


def _sum_pieces(recv, own, own_slotted, me, name):
    k = recv.shape[0]
    rest = recv.shape[1:]
    width = rest[-1]
    rows_per = math.prod(rest[:-1])
    r2 = recv.reshape(k * rows_per, width)
    o2 = own.reshape(-1, width)
    tile = rows_per
    for cand in (256, 128, 64, 32, 16, 8):
        if rows_per % cand == 0:
            tile = cand
            break
    nt = rows_per // tile

    def body(me_ref, *refs):
        mine = refs[k][...].astype(F32)
        acc = None
        for j in range(k):
            term = jnp.where(me_ref[0] == j, mine, refs[j][...].astype(F32))
            acc = term if acc is None else acc + term
        refs[k + 1][...] = acc
        refs[k + 2][...] = acc.astype(BF)

    in_specs = [pl.BlockSpec((tile, width), functools.partial(lambda i, m, s: (s * nt + i, 0), s=s)) for s in range(k)]
    in_specs.append(pl.BlockSpec((tile, width), (lambda i, m: (m[0] * nt + i, 0)) if own_slotted else (lambda i, m: (i, 0))))
    out_spec = pl.BlockSpec((tile, width), lambda i, m: (i, 0))
    out, out_bf = pl.pallas_call(
        body, name=name,
        grid_spec=pltpu.PrefetchScalarGridSpec(num_scalar_prefetch=1, grid=(nt,), in_specs=in_specs, out_specs=[out_spec, out_spec]),
        out_shape=[jax.ShapeDtypeStruct((rows_per, width), F32), jax.ShapeDtypeStruct((rows_per, width), BF)],
        compiler_params=_params(1),
    )(jnp.reshape(me, (1,)).astype(jnp.int32), *([r2] * k), o2)
    return out.reshape(rest), out_bf.reshape(rest)


def _adamw(w, g, m, v, name):
    def fn(w_, g_, m_, v_):
        m1 = ADAM_B1 * m_ + (1.0 - ADAM_B1) * g_
        v1 = ADAM_B2 * v_ + (1.0 - ADAM_B2) * (g_ * g_)
        m_hat = m1 / (1.0 - ADAM_B1 ** ADAM_STEP)
        v_hat = v1 / (1.0 - ADAM_B2 ** ADAM_STEP)
        delta = -ADAM_LR * (m_hat / (jnp.sqrt(v_hat) + ADAM_EPS) + ADAM_WD * w_)
        return (delta, m1, v1), ()

    rows, width = w.shape
    tile = rows
    for cand in (256, 128, 64, 32, 16, 8):
        if rows % cand == 0:
            tile = cand
            break
    o, _ = _rows(fn, [(w, 0, width), (g, 0, width), (m, 0, width), (v, 0, width)], [],
                 [(width, F32)] * 3, [], tile, name)
    return o


def _col_pieces(start, width, shard):
    out = []
    while width:
        chip, off = divmod(start, shard)
        take = min(width, shard - off)
        out.append((chip, off, take))
        start, width = start + take, width - take
    return out


def _gather_cols(parts, ranges, pad=0):
    shard = parts.shape[3]
    halves = []
    for h in range(2):
        cols = [parts[h, chip, :, off:off + w] for s, wd in ranges for chip, off, w in _col_pieces(s, wd, shard)]
        if pad:
            cols.append(jnp.zeros((parts.shape[2], pad), parts.dtype))
        halves.append(jnp.concatenate(cols, axis=1))
    return jnp.concatenate(halves, axis=0)


def _scatter_cols(g, ranges, shard):
    starts, pos = {}, 0
    for s, wd in ranges:
        starts[s] = (pos, wd)
        pos += wd
    order = sorted(starts)
    hr = g.shape[0] // 2
    out = []
    for h in range(2):
        per_chip = []
        for chip in range(4):
            cols = []
            for s in order:
                p0, wd = starts[s]
                lo, hi = max(s, chip * shard), min(s + wd, (chip + 1) * shard)
                if lo < hi:
                    cols.append(g[h * hr:(h + 1) * hr, p0 + lo - s:p0 + hi - s])
            per_chip.append(jnp.concatenate(cols, axis=1))
        out.append(jnp.stack(per_chip))
    return jnp.stack(out)


_IN_RANGES = tuple((o, wd) for _, o, wd in _PACK)


def _double_heads(t):
    length = t.shape[0]
    h = jnp.transpose(t.reshape(length, 2, 64), (1, 0, 2))
    return jnp.concatenate([h, h], axis=-1)


def _fold_heads(d_cur, d_prev):
    length = d_cur.shape[1]
    d = d_cur + jnp.concatenate([d_prev[:, BLK:], jnp.zeros((2, BLK, BLK), F32)], axis=1)
    d = d[..., :64] + d[..., 64:]
    return jnp.transpose(d, (1, 0, 2)).reshape(length, 128)


def _ln_fwd(a, r, g, b, name):
    def fn(a_, r_, g_, b_):
        y = _layer_norm(ALPHA * a_ + r_, g_, b_)
        return (y, y), ()
    o, _ = _rows(fn, [(a, 0, D_MODEL), (r, 0, D_MODEL)], [g, b], [(D_MODEL, F32), (D_MODEL, BF)], [], ROW_TILE, name)
    return o


def _ln_bwd(a, r, d1, d2, g, b, name):
    def fn(a_, r_, d1_, d2_, g_, b_):
        _, vjp = jax.vjp(_layer_norm, ALPHA * a_ + r_, g_, b_)
        ds, dg, db = vjp(ALPHA * d1_ + d2_)
        return (ds, ds), (dg, db)
    o, acc = _rows(fn, [(a, 0, D_MODEL), (r, 0, D_MODEL), (d1, 0, D_MODEL), (d2, 0, D_MODEL)], [g, b],
                   [(D_MODEL, F32), (D_MODEL, BF)], [(1, D_MODEL), (1, D_MODEL)], ROW_TILE, name)
    return o[0], o[1], acc[0], acc[1]


def _layer_fwd(l, h, hb, w, p, hosted=None, hosted_attn=None, hosted_conv=None, late_weights=None):
    sv = {"h": h, "hb": hb}
    proj = _mm(hb, w["win"], "nn", F32, 4096, 512, 1024, f"proj{l}")
    sv["proj"] = proj
    pre = _conv_fwd(proj, p["conv_w"], p["conv_b"], f"conv{l}", hosted_conv)
    if hosted_conv:
        pre, got = pre
        w.update(late_weights(got))
    sv["pre"] = pre
    dt_t = jnp.transpose(proj[:, _OFF["dt"]:_OFF["dt"] + SSD_HEADS])
    sv["dt_t"] = dt_t
    res = _ssd_fwd(pre, proj, dt_t, p["dt_bias"], p["a_log"], p["d_skip_c"], p["ssd_norm_w"], f"ssd{l}", hosted)
    yn, s_in = res[:2]
    if hosted:
        sv["hosted"] = res[2]
    sv["yn"], sv["s_in"] = yn, s_in
    ya = _mm(yn, w["wso"], "nn", F32, 1024, 1024, 2048, f"ssdout{l}")
    k2 = _double_heads(proj[:, _OFF["k"]:_OFF["k"] + 128])
    v2 = _double_heads(proj[:, _OFF["v"]:_OFF["v"] + 128])
    sv["k2"], sv["v2"] = k2, v2
    att = _attn_fwd(proj, k2, v2, p["att_sinks"], f"attn{l}", hosted_attn)
    if hosted_attn:
        att, sv["hosted_attn"] = att
    sv["att"] = att
    yb = _mm(att, w["wao"], "nn", F32, 1024, 1024, 1024, f"attout{l}")
    sv["ya"], sv["yb"] = ya, yb
    gcb = _OFF["gates"] // 1024

    def gate_fn(ga, gb, ya_, yb_):
        return (_sigmoid(ga) * ya_ + _sigmoid(gb) * yb_,), ()
    (ub,), _ = _rows(gate_fn, [(proj, gcb, 1024), (proj, gcb + 1, 1024), (ya, 0, 1024), (yb, 0, 1024)], [],
                     [(D_MODEL, BF)], [], ROW_TILE, f"gate{l}")
    sv["ub"] = ub
    mix = _mm(ub, w["wmo"], "nn", F32, 1024, 1024, 1024, f"mixout{l}")
    sv["mix"] = mix
    h1, h1b = _ln_fwd(h, mix, p["ln_mix_g"], p["ln_mix_b"], f"lnmix{l}")
    sv["h1"], sv["h1b"] = h1, h1b
    gu = _mm(h1b, w["wgu"], "nn", F32, 4096, 512, 1024, f"ffnin{l}")
    sv["gu"] = gu

    def act_fn(g_, u_):
        return (_silu(g_) * u_,), ()
    (act,), _ = _rows(act_fn, [(gu, 0, FFN_HIDDEN), (gu, 1, FFN_HIDDEN)], [], [(FFN_HIDDEN, BF)], [], 256, f"swiglu{l}")
    sv["act"] = act
    ffn = _mm(act, w["wd"], "nn", F32, 1024, 1024, FFN_HIDDEN, f"ffnout{l}")
    sv["ffn"] = ffn
    return sv


def _layer_bwd(l, sv, w, p, ds_f, ds_b, hosted=(), early_hook=None, late_hook=None):
    g = {}
    g["w_ffn_down"] = _mm(sv["act"], ds_b, "tn", F32, 1408, 1024, 2048, f"dwd{l}")
    dact = _mm(ds_b, w["wd"], "nt", BF, 1024, FFN_HIDDEN, 1024, f"dact{l}")

    def act_bwd(g_, u_, d_):
        _, vjp = jax.vjp(lambda a, b: _silu(a) * b, g_, u_)
        dg_, du_ = vjp(d_.astype(F32))
        return (jnp.concatenate([dg_, du_], axis=1),), ()
    (dgu,), _ = _rows(act_bwd, [(sv["gu"], 0, FFN_HIDDEN), (sv["gu"], 1, FFN_HIDDEN), (dact, 0, FFN_HIDDEN)], [],
                      [(2 * FFN_HIDDEN, BF)], [], 128, f"dswiglu{l}")
    dwgu = _mm(sv["h1b"], dgu, "tn", F32, 1024, 1408, 2048, f"dwgu{l}")
    g["w_ffn_gate"], g["w_ffn_up"] = dwgu[:, :FFN_HIDDEN], dwgu[:, FFN_HIDDEN:]
    dh1 = _mm(dgu, w["wgu"], "nt", F32, 1024, 1024, 2816, f"dh1{l}")
    ds2_f, ds2_b, g["ln_mix_g"], g["ln_mix_b"] = _ln_bwd(sv["h"], sv["mix"], ds_f, dh1, p["ln_mix_g"], p["ln_mix_b"], f"dlnmix{l}")
    g["w_mix_out"] = _mm(sv["ub"], ds2_b, "tn", F32, 1024, 1024, 2048, f"dwmo{l}")
    du = _mm(ds2_b, w["wmo"], "nt", BF, 1024, 1024, 1024, f"du{l}")
    proj = sv["proj"]
    gcb = _OFF["gates"] // 1024

    def gate_bwd(ga, gb, ya_, yb_, du_):
        _, vjp = jax.vjp(lambda a, b, c, d: _sigmoid(a) * c + _sigmoid(b) * d, ga, gb, ya_, yb_)
        dga, dgb, dya, dyb = vjp(du_.astype(F32))
        return (jnp.concatenate([dga, dgb], axis=1), dya, dyb), ()
    (dproj, dya, dyb), _ = _rows(
        gate_bwd, [(proj, gcb, 1024), (proj, gcb + 1, 1024), (sv["ya"], 0, 1024), (sv["yb"], 0, 1024), (du, 0, 1024)], [],
        [(2048, BF, NP), (D_MODEL, BF), (D_MODEL, BF)], [], ROW_TILE, f"dgate{l}")
    g["w_att_out"] = _mm(sv["att"], dyb, "tn", F32, 1024, 1024, 2048, f"dwao{l}")
    datt = _mm(dyb, w["wao"], "nt", BF, 1024, 1024, 1024, f"datt{l}")
    g["w_ssd_out"] = _mm(sv["yn"], dya, "tn", F32, 1024, 1024, 2048, f"dwso{l}")
    dyn = _mm(dya, w["wso"], "nt", F32, 1024, 1024, 1024, f"dyn{l}")
    hosted = list(hosted) + (list(early_hook(g)) if early_hook else [])
    dproj, dkp, dkc, dvp, dvc, g["att_sinks"] = _attn_bwd(proj, sv["k2"], sv["v2"], p["att_sinks"], datt, dproj, f"dattn{l}")
    res = _ssd_bwd(sv["pre"], proj, sv["dt_t"], p["dt_bias"], p["a_log"], p["d_skip_c"], p["ssd_norm_w"], sv["s_in"], dyn,
                   dproj, f"dssd{l}", (hosted, False) if hosted else None)
    (dpx, dpb, dpc, dproj, ddt_t, g["dt_bias"], g["a_log"], ddsk, g["ssd_norm_w"]) = res[:9]
    if hosted:
        g["hosted"] = res[9]
    g["d_skip"] = jnp.sum(ddsk.reshape(SSD_HEADS, 64), axis=1)
    dws, dbs = [], []
    for nm, dpart, part in (("x", dpx, 0), ("b", dpb, 4), ("c", dpc, 5)):
        dproj, dw_, db_ = _conv_bwd(dpart, proj, p["conv_w"], dproj, part, f"dconv{nm}{l}")
        dws.append(dw_)
        dbs.append(db_)
    g["conv_w"], g["conv_b"] = jnp.concatenate(dws, axis=1), jnp.concatenate(dbs, axis=1)
    length = proj.shape[0]
    tail = jnp.concatenate([_fold_heads(dkc, dkp).astype(BF), _fold_heads(dvc, dvp).astype(BF),
                            jnp.transpose(ddt_t).astype(BF), jnp.zeros((length, _PAD), BF)], axis=1)
    dproj = lax.dynamic_update_slice(dproj, tail, (0, _OFF["k"]))
    g["w_in"] = _mm(sv["hb"], dproj, "tn", F32, 1024, 2176, 1024, f"dwin{l}")
    late = list(late_hook(g)) if late_hook else []
    dh = _mm(dproj, w["win"], "nt", F32, 1024, 1024, 2176, f"dh{l}", (late, False) if late else None)
    if late:
        dh, g["late_hosted"] = dh
    return g, ds2_f, dh


def kernel(x, ln_in_g, ln_in_b, w_in, conv_w, conv_b, dt_bias, a_log, d_skip, ssd_norm_w, att_sinks, w_ssd_out, w_att_out, w_mix_out, ln_mix_g, ln_mix_b, w_ffn_gate, w_ffn_up, w_ffn_down, ln_ffn_g, ln_ffn_b, loss_target, m_ln_in_g, m_ln_in_b, m_w_in, m_conv_w, m_conv_b, m_dt_bias, m_a_log, m_d_skip, m_ssd_norm_w, m_att_sinks, m_w_ssd_out, m_w_att_out, m_w_mix_out, m_ln_mix_g, m_ln_mix_b, m_w_ffn_gate, m_w_ffn_up, m_w_ffn_down, m_ln_ffn_g, m_ln_ffn_b, v_ln_in_g, v_ln_in_b, v_w_in, v_conv_w, v_conv_b, v_dt_bias, v_a_log, v_d_skip, v_ssd_norm_w, v_att_sinks, v_w_ssd_out, v_w_att_out, v_w_mix_out, v_ln_mix_g, v_ln_mix_b, v_w_ffn_gate, v_w_ffn_up, v_w_ffn_down, v_ln_ffn_g, v_ln_ffn_b):
    env = dict(locals())
    wts = {n: env[n] for n in WEIGHTS}
    mom1 = {n: env["m_" + n] for n in WEIGHTS}
    mom2 = {n: env["v_" + n] for n in WEIGHTS}
    ci = lax.axis_index("c")
    chip = 2 * lax.axis_index("x") + lax.axis_index("y")
    xs_ = x[0]
    tgt = loss_target[0]

    COLS = ("w_in", "w_ffn_gate", "w_ffn_up")

    def half(t):
        return lax.dynamic_slice_in_dim(t, ci * (t.shape[0] // 2), t.shape[0] // 2, axis=0)

    def assemble(name, parts):
        if name == "w_in":
            return _gather_cols(parts, _IN_RANGES, _PAD)
        if name in COLS:
            return _gather_cols(parts, ((0, 4 * parts.shape[3]),))
        return jnp.concatenate([parts[h, s] for s in range(4) for h in range(2)], axis=0)

    def split(name, gfull):
        if name == "w_in":
            return _scatter_cols(gfull, _IN_RANGES, IN_DIM // 4)
        if name in COLS:
            return _scatter_cols(gfull, ((0, gfull.shape[1]),), gfull.shape[1] // 4)
        hr = gfull.shape[0] // 8
        return jnp.stack([jnp.stack([gfull[(2 * s + h) * hr:(2 * s + h + 1) * hr] for s in range(4)]) for h in range(2)])

    mine = [[half(wts[n][l].astype(BF)) for n in BIG] for l in range(DEPTH)]
    conv_all, = _with_own(_exchange([conv_w], "xy", True, "gather_conv"), [conv_w], chip)
    conv_full = jnp.transpose(conv_all, (1, 2, 0, 3)).reshape(DEPTH, SSD_CONV, CONV_DIM)

    KEYS = {"w_in": "win", "w_ssd_out": "wso", "w_att_out": "wao", "w_mix_out": "wmo", "w_ffn_down": "wd"}

    def layer_weights(l, names, by_chip):
        own = [mine[l][BIG.index(n)] for n in names]
        by_chip = _with_own(by_chip, own, chip)
        parts = _with_own(_exchange(by_chip, "c", True, f"gather_cores{l}_{names[0]}"), by_chip, ci)
        fw = {n: assemble(n, part) for n, part in zip(names, parts)}
        w = {KEYS[n]: fw[n] for n in names if n in KEYS}
        if "w_ffn_gate" in fw:
            w["wgu"] = jnp.concatenate([fw["w_ffn_gate"], fw["w_ffn_up"]], axis=1)
        return w

    layer_p = []
    for l in range(DEPTH):
        layer_p.append({
            "conv_w": conv_full[l], "conv_b": conv_b[l][None],
            "dt_bias": dt_bias[l][:, None], "a_log": a_log[l][:, None],
            "d_skip_c": jnp.repeat(d_skip[l], 64)[None], "ssd_norm_w": ssd_norm_w[l][None],
            "att_sinks": att_sinks[l][None], "ln_mix_g": ln_mix_g[l][None], "ln_mix_b": ln_mix_b[l][None],
            "ln_ffn_g": ln_ffn_g[l][None], "ln_ffn_b": ln_ffn_b[l][None],
        })

    def ln_in_fn(x_, g_, b_):
        y = _layer_norm(x_, g_, b_)
        return (y, y), ()
    (h, hb), _ = _rows(ln_in_fn, [(xs_, 0, D_MODEL)], [ln_in_g[None], ln_in_b[None]], [(D_MODEL, F32), (D_MODEL, BF)], [], ROW_TILE, "ln_in")
    rest = tuple(n for n in BIG if n != "w_in")
    layer_w = [layer_weights(0, ("w_in",), _exchange([mine[0][BIG.index("w_in")]], "xy", True, "gather_chips0"))]
    behind_ssd = [i for i, n in enumerate(BIG) if n in COLS]
    behind_attn = [i for i, n in enumerate(BIG) if n not in COLS]
    saved = []
    for l in range(DEPTH):
        nxt = l + 1 < DEPTH
        sv = _layer_fwd(l, h, hb, layer_w[l], layer_p[l], ([mine[l + 1][i] for i in behind_ssd], True) if nxt else None,
                        ([mine[l + 1][i] for i in behind_attn], True) if nxt else None,
                        ([mine[l][BIG.index(n)] for n in rest], True) if l == 0 else None,
                        functools.partial(layer_weights, l, rest))
        saved.append(sv)
        if nxt:
            got = dict(zip(behind_ssd + behind_attn, sv["hosted"] + sv["hosted_attn"]))
            layer_w.append(layer_weights(l + 1, BIG, [got[i] for i in range(len(BIG))]))
            h, hb = _ln_fwd(sv["h1"], sv["ffn"], layer_p[l]["ln_ffn_g"], layer_p[l]["ln_ffn_b"], f"lnffn{l}")

    def loss_fn(a_, r_, t_, g_, b_):
        y, vjp = jax.vjp(_layer_norm, ALPHA * a_ + r_, g_, b_)
        err = y - t_
        ds, dg, db = vjp(err * (1.0 / D_MODEL))
        part = 0.5 * jnp.sum(jnp.mean(err * err, axis=-1, keepdims=True), axis=0, keepdims=True)
        return (ds, ds), (dg, db, jnp.broadcast_to(part, (1, BLK)))
    sv = saved[-1]
    (ds_f, ds_b), (dg_last, db_last, loss_part) = _rows(
        loss_fn, [(sv["h1"], 0, D_MODEL), (sv["ffn"], 0, D_MODEL), (tgt, 0, D_MODEL)],
        [layer_p[-1]["ln_ffn_g"], layer_p[-1]["ln_ffn_b"]], [(D_MODEL, F32), (D_MODEL, BF)],
        [(1, D_MODEL), (1, D_MODEL), (1, BLK)], ROW_TILE, "loss")
    loss = lax.psum(loss_part[0, 0], ("x", "y", "c"))

    def reduce_cores(l, g, names):
        src = [split(n, g[n]) for n in names]
        recv = _exchange(src, "c", False, f"reduce_cores{l}_{names[0]}")
        return {n: _sum_pieces(r, o, True, ci, f"sum_cores{l}_{n}") for n, r, o in zip(names, recv, src)}

    def bf16_of(sums, names):
        return [sums[n][1] for n in names]

    def add_chips(l, sums, from_chips):
        return [_sum_pieces(from_chips[n], sums[n][0], True, chip, f"sum_chips{l}_{n}")[0] for n in BIG]

    EARLY = tuple(n for n in BIG if n != "w_in")
    grads = [None] * DEPTH
    chip_sums = [None] * DEPTH
    for l in reversed(range(DEPTH)):
        sv = saved[l]
        if l == DEPTH - 1:
            lg, lb = dg_last, db_last
        else:
            ds_f, ds_b, lg, lb = _ln_bwd(sv["h1"], sv["ffn"], d1, d2, layer_p[l]["ln_ffn_g"], layer_p[l]["ln_ffn_b"], f"dlnffn{l}")
        sums = {}

        def early_hook(g, l=l, sums=sums):
            sums.update(reduce_cores(l, g, EARLY))
            return bf16_of(sums, EARLY)

        def late_hook(g, l=l, sums=sums):
            sums.update(reduce_cores(l, g, ("w_in",)))
            return bf16_of(sums, ("w_in",))
        if l == 0:
            g, d1, d2 = _layer_bwd(l, sv, layer_w[l], layer_p[l], ds_f, ds_b, bf16_of(chip_sums[l + 1], BIG), early_hook, late_hook)
        else:
            g, d1, d2 = _layer_bwd(l, sv, layer_w[l], layer_p[l], ds_f, ds_b)
            sums.update(reduce_cores(l, g, BIG))
        g["ln_ffn_g"], g["ln_ffn_b"] = lg, lb
        grads[l] = g
        chip_sums[l] = sums

    def ln_in_bwd(x_, d1_, d2_, g_, b_):
        _, vjp = jax.vjp(_layer_norm, x_, g_, b_)
        dx, dg, db = vjp(ALPHA * d1_ + d2_)
        return (dx,), (dg, db)
    (grad_x,), (g_ln_in_g, g_ln_in_b) = _rows(
        ln_in_bwd, [(xs_, 0, D_MODEL), (d1, 0, D_MODEL), (d2, 0, D_MODEL)], [ln_in_g[None], ln_in_b[None]],
        [(D_MODEL, F32)], [(1, D_MODEL), (1, D_MODEL)], ROW_TILE, "dln_in")

    carried = grads[0]["hosted"]
    from_chips = dict(zip(EARLY, carried[len(BIG):]))
    from_chips["w_in"], = grads[0]["late_hosted"]
    reduced = add_chips(0, chip_sums[0], from_chips) + add_chips(1, chip_sums[1], dict(zip(BIG, carried[:len(BIG)])))
    halves = _with_own(_exchange(reduced, "c", True, "swap_halves"), reduced, ci)
    shards = [t.reshape(2 * t.shape[1], t.shape[2]) for t in halves]
    big_grad = {n: jnp.stack([shards[l * len(BIG) + i] for l in range(DEPTH)]).reshape(wts[n].shape) for i, n in enumerate(BIG)}

    small_g = {"ln_in_g": g_ln_in_g[0], "ln_in_b": g_ln_in_b[0]}
    for n in SMALL[2:]:
        small_g[n] = jnp.stack([grads[l][n].reshape(wts[n].shape[1:] if n != "conv_w" else (SSD_CONV, CONV_DIM)) for l in range(DEPTH)])
    flat = jnp.concatenate([small_g[n].reshape(-1) for n in SMALL])
    n_small = flat.shape[0]
    width = -(-n_small // 1024) * 1024
    flat = jnp.pad(flat, (0, width - n_small)).reshape(8, width // 8)
    gathered, = _with_own(_exchange([flat], "all", True, "gather_small"), [flat], 2 * chip + ci)
    total = _sum_slots(gathered, "sum_small").reshape(-1)
    small_grad, off = {}, 0
    for n in SMALL:
        shp = small_g[n].shape
        cnt = math.prod(shp)
        small_grad[n] = total[off:off + cnt].reshape(shp)
        off += cnt
    small_grad["conv_w"] = lax.dynamic_slice_in_dim(small_grad["conv_w"], chip * (CONV_DIM // 4), CONV_DIM // 4, axis=2)

    out_g, out_d, out_m, out_v = {}, {}, {}, {}
    for n in BIG:
        shp = wts[n].shape
        two_d = lambda t: t.reshape(shp[0] * shp[1], shp[2])
        d_, m_, v_ = _adamw(two_d(wts[n]), two_d(big_grad[n]), two_d(mom1[n]), two_d(mom2[n]), f"adamw_{n}")
        out_g[n], out_d[n], out_m[n], out_v[n] = big_grad[n], d_.reshape(shp), m_.reshape(shp), v_.reshape(shp)

    def flat_small(d):
        f = jnp.concatenate([d[n].reshape(-1) for n in SMALL])
        return jnp.pad(f, (0, swidth - f.shape[0])).reshape(8, swidth // 8)
    n_sw = sum(math.prod(wts[n].shape) for n in SMALL)
    swidth = -(-n_sw // 1024) * 1024
    d_, m_, v_ = _adamw(flat_small(wts), flat_small(small_grad), flat_small(mom1), flat_small(mom2), "adamw_small")
    off = 0
    for n in SMALL:
        shp = wts[n].shape
        cnt = math.prod(shp)
        out_g[n] = small_grad[n]
        out_d[n] = d_.reshape(-1)[off:off + cnt].reshape(shp)
        out_m[n] = m_.reshape(-1)[off:off + cnt].reshape(shp)
        out_v[n] = v_.reshape(-1)[off:off + cnt].reshape(shp)
        off += cnt

    return (loss, grad_x[None], *[out_g[n] for n in WEIGHTS], *[out_d[n] for n in WEIGHTS],
            *[out_m[n] for n in WEIGHTS], *[out_v[n] for n in WEIGHTS])
```

```python
import functools
import itertools
import math

import jax
import jax.numpy as jnp
from jax import lax
from jax.experimental import pallas as pl
from jax.experimental.pallas import tpu as pltpu

F32 = jnp.float32
BF = jnp.bfloat16

D_MODEL = 1024
DEPTH = 2
ATT_HEADS = 16
ATT_HEAD_DIM = 64
BLK = 128
SSD_D_INNER = 2048
SSD_HEADS = 32
SSD_GROUPS = 4
SSD_STATE = 128
SSD_CONV = 4
BC_DIM = 512
CONV_DIM = 3072
FFN_HIDDEN = 2816
IN_DIM = 8480
LN_EPS = 1e-5
RMS_EPS = 1e-5
ALPHA = (2 * DEPTH) ** 0.25
ADAM_LR = 0.001
ADAM_B1 = 0.9
ADAM_B2 = 0.999
ADAM_EPS = 1e-08
ADAM_WD = 0.01
ADAM_STEP = 10

_PACK = (("gates", 6432, 2048), ("z", 1280, 2048), ("q", 0, 1024), ("xs", 3328, 2048), ("B", 5376, 512),
         ("C", 5888, 512), ("k", 1024, 128), ("v", 1152, 128), ("dt", 6400, 32))
NP = 8704
_OFF = {}
_o = 0
for _n, _, _w in _PACK:
    _OFF[_n] = _o
    _o += _w
_PAD = NP - _o

VMEM_LIMIT_BYTES = 56 * 1024 * 1024
ROW_TILE = 512
BIG = ("w_in", "w_ssd_out", "w_att_out", "w_mix_out", "w_ffn_gate", "w_ffn_up", "w_ffn_down")
WEIGHTS = ("ln_in_g", "ln_in_b", "w_in", "conv_w", "conv_b", "dt_bias", "a_log", "d_skip", "ssd_norm_w", "att_sinks",
           "w_ssd_out", "w_att_out", "w_mix_out", "ln_mix_g", "ln_mix_b", "w_ffn_gate", "w_ffn_up", "w_ffn_down",
           "ln_ffn_g", "ln_ffn_b")
SMALL = tuple(n for n in WEIGHTS if n not in BIG)


def _params(n_grid):
    return pltpu.CompilerParams(dimension_semantics=("arbitrary",) * n_grid, vmem_limit_bytes=VMEM_LIMIT_BYTES)


def _dot(a, b, ca, cb):
    return lax.dot_general(a.astype(BF), b.astype(BF), (((ca,), (cb,)), ((), ())), preferred_element_type=F32)


@jax.custom_vjp
def _nn(a, b):
    return _dot(a, b, 1, 0)


def _nn_f(a, b):
    return _dot(a, b, 1, 0), (a.astype(BF), b.astype(BF))


def _nn_b(res, g):
    a, b = res
    return _dot(g, b, 1, 1), _dot(a, g, 0, 0)


_nn.defvjp(_nn_f, _nn_b)


@jax.custom_vjp
def _nt(a, b):
    return _dot(a, b, 1, 1)


def _nt_f(a, b):
    return _dot(a, b, 1, 1), (a.astype(BF), b.astype(BF))


def _nt_b(res, g):
    a, b = res
    return _dot(g, b, 1, 0), _dot(g, a, 0, 0)


_nt.defvjp(_nt_f, _nt_b)


@jax.custom_vjp
def _tn(a, b):
    return _dot(a, b, 0, 0)


def _tn_f(a, b):
    return _dot(a, b, 0, 0), (a.astype(BF), b.astype(BF))


def _tn_b(res, g):
    a, b = res
    return _dot(b, g, 1, 1), _dot(a, g, 1, 0)


_tn.defvjp(_tn_f, _tn_b)


def _sigmoid(x):
    return 0.5 * jnp.tanh(0.5 * x) + 0.5


def _silu(x):
    return x * _sigmoid(x)


def _layer_norm(s, g, b):
    mu = jnp.mean(s, axis=-1, keepdims=True)
    sc = s - mu
    var = jnp.mean(sc * sc, axis=-1, keepdims=True)
    return sc * lax.rsqrt(var + LN_EPS) * g + b


def _ssd_group(pre_x, pre_b, pre_c, z, dtr, dtb, alog, dsk, nw, state):
    t = pre_x.shape[0]
    xs, bm, cm = _silu(pre_x), _silu(pre_b), _silu(pre_c)
    dt = jax.nn.softplus(dtr + dtb)
    da = dt * (-jnp.exp(alog))
    row = lax.broadcasted_iota(jnp.int32, (t, t), 0)
    col = lax.broadcasted_iota(jnp.int32, (t, t), 1)
    upper = (row <= col).astype(F32)
    cum = jnp.dot(da, upper, precision=lax.Precision.HIGHEST, preferred_element_type=F32)
    tot = jnp.sum(da, axis=1, keepdims=True)
    cb = _nt(cm, bm)
    tril = row >= col
    first = lax.broadcasted_iota(jnp.int32, (t, BLK), 1) < 64
    first_row = lax.broadcasted_iota(jnp.int32, (1, BLK), 1) < 64

    def col_form(r):
        return jnp.broadcast_to(r, (t, t)).T

    ys, new_state = [], []
    for p in range(4):
        h0, h1 = 2 * p, 2 * p + 1
        sl = slice(BLK * p, BLK * (p + 1))
        x_pair = xs[:, sl] * jnp.where(first, col_form(dt[h0:h0 + 1]), col_form(dt[h1:h1 + 1]))
        y_pair = None
        cc = []
        for h, keep in ((h0, first), (h1, jnp.logical_not(first))):
            cr = jnp.broadcast_to(cum[h:h + 1], (t, t))
            cc.append(cr.T)
            decay = jnp.exp(jnp.where(tril, cc[-1] - cr, -1e30))
            y_h = _nn(decay * cb, jnp.where(keep, x_pair, 0.0))
            y_pair = y_h if y_pair is None else y_pair + y_h
        s_pair = state[:, sl]
        y_pair = y_pair + _nn(cm, s_pair) * jnp.where(first, jnp.exp(cc[0]), jnp.exp(cc[1]))
        to_end = jnp.where(first, jnp.exp(tot[h0:h0 + 1] - cc[0]), jnp.exp(tot[h1:h1 + 1] - cc[1]))
        chunk_decay = jnp.where(first_row, jnp.exp(tot[h0:h0 + 1]), jnp.exp(tot[h1:h1 + 1]))
        new_state.append(s_pair * chunk_decay + _tn(bm, x_pair * to_end))
        ys.append(y_pair + dsk[:, sl] * xs[:, sl])
    y = jnp.concatenate(ys, axis=1) * _silu(z)
    y = y * lax.rsqrt(jnp.mean(y * y, axis=-1, keepdims=True) + RMS_EPS) * nw
    return y, jnp.concatenate(new_state, axis=1)


def _attn_block(q, kp, kc, vp, vc, sinks, n, kv):
    t = q.shape[0]
    kb = jnp.concatenate([kp, kc], axis=0)
    vb = jnp.concatenate([vp, vc], axis=0)
    qi = lax.broadcasted_iota(jnp.int32, (t, 2 * t), 0)
    kj = lax.broadcasted_iota(jnp.int32, (t, 2 * t), 1)
    rel = qi + t - kj
    valid = (rel >= 0) & (rel < t) & ((n * t - t + kj) >= 0)
    relf = rel.astype(F32)
    first = lax.broadcasted_iota(jnp.int32, (t, BLK), 1) < 64
    lane16 = lax.broadcasted_iota(jnp.int32, (1, ATT_HEADS), 1)
    outs = []
    for p in range(4):
        qp = q[:, BLK * p:BLK * (p + 1)] * (ATT_HEAD_DIM ** -0.5)
        o_pair = None
        for half, keep in enumerate((first, jnp.logical_not(first))):
            h = kv * 8 + 2 * p + half
            s = _nt(jnp.where(keep, qp, 0.0), kb)
            slope = jnp.exp((-8.0 * math.log(2.0) / ATT_HEADS) * jnp.asarray(h + 1, F32))
            s = jnp.where(valid, s - slope * relf, -1e30)
            sink = jnp.sum(jnp.where(lane16 == h, sinks, 0.0), axis=1, keepdims=True)
            m = lax.stop_gradient(jnp.maximum(jnp.max(s, axis=1, keepdims=True), sink))
            e = jnp.exp(s - m)
            den = jnp.sum(e, axis=1, keepdims=True) + jnp.exp(sink - m)
            o = _nn(e * (1.0 / den), vb)
            o_pair = o if o_pair is None else jnp.where(first, o_pair, o)
        outs.append(o_pair)
    return jnp.concatenate(outs, axis=1)


def _chip_exchange(src, out, send_sems, recv_sems, bcast):
    x, y, c = lax.axis_index("x"), lax.axis_index("y"), lax.axis_index("c")
    me = 2 * x + y

    def copy(a, d, slot):
        j = jnp.bitwise_xor(me, d)
        return pltpu.make_async_remote_copy(
            src_ref=src[a] if bcast else src[a].at[j], dst_ref=out[a].at[slot],
            send_sem=send_sems.at[a * 4 + d], recv_sem=recv_sems.at[a * 4 + d],
            device_id=(j // 2, j % 2, c), device_id_type=pl.DeviceIdType.MESH)

    pairs = [(a, d) for a in range(len(src)) for d in range(1, 4)]

    def start():
        for a, d in pairs:
            copy(a, d, me).start()

    def wait():
        for a, d in pairs:
            copy(a, d, jnp.bitwise_xor(me, d)).wait_recv()
        for a, d in pairs:
            copy(a, d, me).wait_send()

    return start, wait


def _mm(a, b, mode, out_dtype, tm, tn, tk, name, hosted=None):
    if mode == "nn":
        (m, k), (k2, n) = a.shape, b.shape
    elif mode == "nt":
        (m, k), (n, k2) = a.shape, b.shape
    else:
        (k, m), (k2, n) = a.shape, b.shape
    assert k == k2, (a.shape, b.shape, mode)
    tm, tn, tk = min(tm, m), min(tn, n), min(tk, k)
    assert m % tm == 0 and n % tn == 0 and k % tk == 0, (m, n, k, tm, tn, tk)
    nk = k // tk
    grid = (m // tm, n // tn, nk)
    ca, cb = {"nn": (1, 0), "nt": (1, 1), "tn": (0, 0)}[mode]
    h_arrays, h_bcast = hosted if hosted else ((), False)
    nh = len(h_arrays)

    def body(*refs):
        a_ref, b_ref = refs[:2]
        o_ref = refs[2 + nh]
        scratch = refs[3 + 2 * nh:]
        step = [pl.program_id(ax) for ax in range(3)]
        if nh:
            start, wait = _chip_exchange(refs[2:2 + nh], refs[3 + nh:3 + 2 * nh], scratch[-2], scratch[-1], h_bcast)

            @pl.when((step[0] == 0) & (step[1] == 0) & (step[2] == 0))
            def _():
                start()

        part = _dot(a_ref[...], b_ref[...], ca, cb)
        if nk == 1:
            o_ref[...] = part.astype(o_ref.dtype)
        else:
            acc_ref = scratch[0]

            @pl.when(step[2] == 0)
            def _():
                acc_ref[...] = part

            @pl.when(step[2] > 0)
            def _():
                acc_ref[...] += part

            @pl.when(step[2] == nk - 1)
            def _():
                o_ref[...] = acc_ref[...].astype(o_ref.dtype)

        if nh:
            @pl.when((step[0] == grid[0] - 1) & (step[1] == grid[1] - 1) & (step[2] == nk - 1))
            def _():
                wait()

    a_spec = pl.BlockSpec((tk, tm), lambda i, j, kk: (kk, i)) if mode == "tn" else pl.BlockSpec((tm, tk), lambda i, j, kk: (i, kk))
    b_spec = pl.BlockSpec((tn, tk), lambda i, j, kk: (j, kk)) if mode == "nt" else pl.BlockSpec((tk, tn), lambda i, j, kk: (kk, j))
    any_spec = pl.BlockSpec(memory_space=pl.ANY)
    res = pl.pallas_call(
        body, name=name, grid=grid,
        in_specs=[a_spec, b_spec] + [any_spec] * nh,
        out_specs=[pl.BlockSpec((tm, tn), lambda i, j, kk: (i, j))] + [any_spec] * nh,
        out_shape=[jax.ShapeDtypeStruct((m, n), out_dtype)]
        + [jax.ShapeDtypeStruct(((4,) + s.shape) if h_bcast else s.shape, s.dtype) for s in h_arrays],
        scratch_shapes=([] if nk == 1 else [pltpu.VMEM((tm, tn), F32)])
        + ([pltpu.SemaphoreType.DMA((4 * nh,)), pltpu.SemaphoreType.DMA((4 * nh,))] if nh else []),
        compiler_params=_params(3),
    )(a, b, *h_arrays)
    return (res[0], list(res[1:])) if nh else res[0]


def _rows(fn, rows, params, outs, accs, tile, name):
    length = rows[0][0].shape[0]
    tile = min(tile, length)
    assert length % tile == 0
    nr, npar, no = len(rows), len(params), len(outs)

    def body(*refs):
        vals = [r[...] for r in refs[:nr + npar]]
        o, a = fn(*vals)
        for ref, val in zip(refs[nr + npar:nr + npar + no], o):
            ref[...] = val.astype(ref.dtype)
        i = pl.program_id(0)
        for ref, val in zip(refs[nr + npar + no:], a):
            @pl.when(i == 0)
            def _(ref=ref, val=val):
                ref[...] = val

            @pl.when(i > 0)
            def _(ref=ref, val=val):
                ref[...] += val

    in_specs = [pl.BlockSpec((tile, w), functools.partial(lambda i, cb: (i, cb), cb=cb)) for _, cb, w in rows]
    in_specs += [pl.BlockSpec(p.shape, lambda i: (0, 0)) for p in params]
    outs = [o if len(o) == 3 else (o[0], o[1], o[0]) for o in outs]
    out_specs = [pl.BlockSpec((tile, w), lambda i: (i, 0)) for w, _, _ in outs]
    out_specs += [pl.BlockSpec((r, w), lambda i: (0, 0)) for r, w in accs]
    out_shape = [jax.ShapeDtypeStruct((length, full), dt) for _, dt, full in outs]
    out_shape += [jax.ShapeDtypeStruct((r, w), F32) for r, w in accs]
    res = pl.pallas_call(
        body, name=name, grid=(length // tile,), in_specs=in_specs, out_specs=out_specs, out_shape=out_shape,
        compiler_params=_params(1),
    )(*[r[0] for r in rows], *params)
    return res[:no], res[no:]


CONV_SUB = 32


def _conv_fwd(proj, conv_w, conv_b, name, hosted=None):
    length = proj.shape[0]
    tl, cw = min(1024, length), 512
    cb0 = _OFF["xs"] // cw
    grid = (CONV_DIM // cw, length // tl)
    h_arrays, h_bcast = hosted if hosted else ((), False)
    nh = len(h_arrays)

    def body(*refs):
        u_ref, halo_ref, w_ref, b_ref = refs[:4]
        o_ref = refs[4 + nh]
        win = refs[5 + 2 * nh]
        j, i = pl.program_id(0), pl.program_id(1)
        if nh:
            start, wait = _chip_exchange(refs[4:4 + nh], refs[5 + nh:5 + 2 * nh], refs[-2], refs[-1], h_bcast)

            @pl.when((j == 0) & (i == 0))
            def _():
                start()

            @pl.when((j == grid[0] - 1) & (i == grid[1] - 1))
            def _():
                wait()

        win[0:8, :] = jnp.where(i > 0, halo_ref[...], 0.0)
        win[8:8 + tl, :] = u_ref[...]
        taps = [w_ref[kk:kk + 1, :] for kk in range(SSD_CONV)]
        bias = b_ref[...]
        for r in range(0, tl, CONV_SUB):
            acc = bias + taps[0] * win[5 + r:5 + r + CONV_SUB, :]
            for kk in range(1, SSD_CONV):
                acc = acc + taps[kk] * win[5 + kk + r:5 + kk + r + CONV_SUB, :]
            o_ref[r:r + CONV_SUB, :] = acc

    any_spec = pl.BlockSpec(memory_space=pl.ANY)
    res = pl.pallas_call(
        body, name=name, grid=grid,
        in_specs=[pl.BlockSpec((tl, cw), lambda j, i: (i, cb0 + j)),
                  pl.BlockSpec((8, cw), lambda j, i: (jnp.maximum(i * (tl // 8) - 1, 0), cb0 + j)),
                  pl.BlockSpec((SSD_CONV, cw), lambda j, i: (0, j)),
                  pl.BlockSpec((1, cw), lambda j, i: (0, j))] + [any_spec] * nh,
        out_specs=[pl.BlockSpec((tl, cw), lambda j, i: (i, j))] + [any_spec] * nh,
        out_shape=[jax.ShapeDtypeStruct((length, CONV_DIM), F32)]
        + [jax.ShapeDtypeStruct(((4,) + s.shape) if h_bcast else s.shape, s.dtype) for s in h_arrays],
        scratch_shapes=[pltpu.VMEM((8 + tl, cw), F32)]
        + ([pltpu.SemaphoreType.DMA((4 * nh,)), pltpu.SemaphoreType.DMA((4 * nh,))] if nh else []),
        compiler_params=_params(2),
    )(proj, proj, conv_w, conv_b, *h_arrays)
    return (res[0], list(res[1:])) if nh else res[0]


def _conv_bwd(dpre, proj, conv_w, dproj, part, name):
    length = proj.shape[0]
    width = dpre.shape[1]
    tl, cw = min(512, length), 512
    cb0 = _OFF["xs"] // cw + part
    n_t = length // tl

    def body(d_ref, dnext_ref, u_ref, halo_ref, w_ref, _, du_ref, dw_ref, db_ref, dwin, uwin):
        i = pl.program_id(1)
        dwin[0:tl, :] = d_ref[...]
        dwin[tl:tl + 8, :] = jnp.where(i < n_t - 1, dnext_ref[...], 0.0)
        uwin[0:8, :] = jnp.where(i > 0, halo_ref[...], 0.0)
        uwin[8:8 + tl, :] = u_ref[...]
        taps = [w_ref[kk:kk + 1, :] for kk in range(SSD_CONV)]
        sub = CONV_SUB // 2
        acc_w = [jnp.zeros((sub, cw), F32) for _ in range(SSD_CONV)]
        acc_b = jnp.zeros((sub, cw), F32)
        for r in range(0, tl, sub):
            d = dwin[r:r + sub, :]
            du = taps[3] * d
            for kk in range(SSD_CONV - 1):
                du = du + taps[kk] * dwin[3 - kk + r:3 - kk + r + sub, :]
            du_ref[r:r + sub, :] = du.astype(du_ref.dtype)
            for kk in range(SSD_CONV):
                acc_w[kk] = acc_w[kk] + d * uwin[5 + kk + r:5 + kk + r + sub, :]
            acc_b = acc_b + d
        dw = jnp.concatenate([jnp.sum(a, axis=0, keepdims=True) for a in acc_w], axis=0)
        db = jnp.sum(acc_b, axis=0, keepdims=True)

        @pl.when(i == 0)
        def _():
            dw_ref[...] = dw
            db_ref[...] = db

        @pl.when(i > 0)
        def _():
            dw_ref[...] += dw
            db_ref[...] += db

    return pl.pallas_call(
        body, name=name, grid=(width // cw, n_t), input_output_aliases={5: 0},
        in_specs=[pl.BlockSpec((tl, cw), lambda j, i: (i, j)),
                  pl.BlockSpec((8, cw), lambda j, i: (jnp.minimum((i + 1) * (tl // 8), length // 8 - 1), j)),
                  pl.BlockSpec((tl, cw), lambda j, i: (i, cb0 + j)),
                  pl.BlockSpec((8, cw), lambda j, i: (jnp.maximum(i * (tl // 8) - 1, 0), cb0 + j)),
                  pl.BlockSpec((SSD_CONV, cw), lambda j, i: (0, part + j)),
                  pl.BlockSpec(memory_space=pl.ANY)],
        out_specs=[pl.BlockSpec((tl, cw), lambda j, i: (i, cb0 + j)),
                   pl.BlockSpec((SSD_CONV, cw), lambda j, i: (0, j)),
                   pl.BlockSpec((1, cw), lambda j, i: (0, j))],
        out_shape=[jax.ShapeDtypeStruct((length, NP), BF),
                   jax.ShapeDtypeStruct((SSD_CONV, width), F32),
                   jax.ShapeDtypeStruct((1, width), F32)],
        scratch_shapes=[pltpu.VMEM((tl + 8, cw), F32), pltpu.VMEM((8 + tl, cw), F32)],
        compiler_params=_params(2),
    )(dpre, dpre, proj, proj, conv_w, dproj)


def _ssd_in_specs(rev, nc):
    def cidx(c):
        return nc - 1 - c if rev else c
    whole = lambda c: (0, 0)
    return [pl.BlockSpec((BLK, SSD_D_INNER), lambda c: (cidx(c), 0)),
            pl.BlockSpec((BLK, BC_DIM), lambda c: (cidx(c), SSD_D_INNER // BC_DIM)),
            pl.BlockSpec((BLK, BC_DIM), lambda c: (cidx(c), SSD_D_INNER // BC_DIM + 1)),
            pl.BlockSpec((BLK, SSD_D_INNER), lambda c: (cidx(c), _OFF["z"] // SSD_D_INNER)),
            pl.BlockSpec((SSD_HEADS, BLK), lambda c: (0, cidx(c))),
            pl.BlockSpec((SSD_HEADS, 1), whole),
            pl.BlockSpec((SSD_HEADS, 1), whole),
            pl.BlockSpec((1, SSD_D_INNER), whole),
            pl.BlockSpec((1, SSD_D_INNER), whole)]


def _group_args(g, px, pb, pc, z, dtr, dtb, alog, dsk, nw):
    wide, narrow, heads = slice(512 * g, 512 * (g + 1)), slice(BLK * g, BLK * (g + 1)), slice(8 * g, 8 * (g + 1))
    return (px[:, wide], pb[:, narrow], pc[:, narrow], z[:, wide], dtr[heads, :], dtb[heads, :], alog[heads, :],
            dsk[:, wide], nw[:, wide])


def _ssd_fwd(pre, proj, dt_t, dtb, alog, dsk, nw, name, hosted=None):
    length = pre.shape[0]
    nc = length // BLK
    h_arrays, h_bcast = hosted if hosted else ((), False)
    nh = len(h_arrays)

    def body(*refs):
        px, pb, pc, z, dtr, dtb_r, al_r, dsk_r, nw_r = refs[:9]
        y_ref, sin_ref = refs[9 + nh:11 + nh]
        st = refs[11 + 2 * nh]
        c = pl.program_id(0)
        if nh:
            start, wait = _chip_exchange(refs[9:9 + nh], refs[11 + nh:11 + 2 * nh], refs[-2], refs[-1], h_bcast)

        @pl.when(c == 0)
        def _():
            st[...] = jnp.zeros_like(st)
            if nh:
                start()

        for g in range(SSD_GROUPS):
            s_in = st[g]
            sin_ref[g] = s_in
            y, s_out = _ssd_group(*_group_args(g, px, pb, pc, z, dtr, dtb_r, al_r, dsk_r, nw_r), s_in)
            y_ref[:, 512 * g:512 * (g + 1)] = y.astype(y_ref.dtype)
            st[g] = s_out

        if nh:
            @pl.when(c == nc - 1)
            def _():
                wait()

    any_spec = pl.BlockSpec(memory_space=pl.ANY)
    res = pl.pallas_call(
        body, name=name, grid=(nc,), in_specs=_ssd_in_specs(False, nc) + [any_spec] * nh,
        out_specs=[pl.BlockSpec((BLK, SSD_D_INNER), lambda c: (c, 0)),
                   pl.BlockSpec((SSD_GROUPS, None, SSD_STATE, 512), lambda c: (0, c, 0, 0))] + [any_spec] * nh,
        out_shape=[jax.ShapeDtypeStruct((length, SSD_D_INNER), BF),
                   jax.ShapeDtypeStruct((SSD_GROUPS, nc, SSD_STATE, 512), F32)]
        + [jax.ShapeDtypeStruct(((4,) + s.shape) if h_bcast else s.shape, s.dtype) for s in h_arrays],
        scratch_shapes=[pltpu.VMEM((SSD_GROUPS, SSD_STATE, 512), F32)]
        + ([pltpu.SemaphoreType.DMA((4 * nh,)), pltpu.SemaphoreType.DMA((4 * nh,))] if nh else []),
        compiler_params=_params(1),
    )(pre, pre, pre, proj, dt_t, dtb, alog, dsk, nw, *h_arrays)
    return (res[0], res[1], list(res[2:])) if nh else (res[0], res[1])


def _ssd_bwd(pre, proj, dt_t, dtb, alog, dsk, nw, s_in, dy, dproj, name, hosted=None):
    length = pre.shape[0]
    nc = length // BLK
    h_arrays, h_bcast = hosted if hosted else ((), False)
    nh = len(h_arrays)

    def body(*refs):
        px, pb, pc, z, dtr, dtb_r, al_r, dsk_r, nw_r, sin_r, dy_r = refs[:11]
        dpx, dpb, dpc, dz, ddt, ddtb, dal, ddsk, dnw = refs[12 + nh:21 + nh]
        dst = refs[21 + 2 * nh]
        c = pl.program_id(0)
        if nh:
            start, wait = _chip_exchange(refs[12:12 + nh], refs[21 + nh:21 + 2 * nh], refs[-2], refs[-1], h_bcast)

        @pl.when(c == 0)
        def _():
            dst[...] = jnp.zeros_like(dst)
            if nh:
                start()

        for grp in range(SSD_GROUPS):
            wide, narrow, heads = slice(512 * grp, 512 * (grp + 1)), slice(BLK * grp, BLK * (grp + 1)), slice(8 * grp, 8 * (grp + 1))
            _, vjp = jax.vjp(_ssd_group, *_group_args(grp, px, pb, pc, z, dtr, dtb_r, al_r, dsk_r, nw_r), sin_r[grp])
            g = vjp((dy_r[:, wide], dst[grp]))
            dpx[:, wide] = g[0]
            dpb[:, narrow] = g[1]
            dpc[:, narrow] = g[2]
            dz[:, wide] = g[3].astype(dz.dtype)
            ddt[heads, :] = g[4]
            dst[grp] = g[9]
            for ref, val, idx in ((ddtb, g[5], (heads, slice(None))), (dal, g[6], (heads, slice(None))),
                                  (ddsk, g[7], (slice(None), wide)), (dnw, g[8], (slice(None), wide))):
                @pl.when(c == 0)
                def _(ref=ref, val=val, idx=idx):
                    ref[idx] = val

                @pl.when(c > 0)
                def _(ref=ref, val=val, idx=idx):
                    ref[idx] += val

        if nh:
            @pl.when(c == nc - 1)
            def _():
                wait()

    rev = lambda c: nc - 1 - c
    whole = lambda c: (0, 0)
    any_spec = pl.BlockSpec(memory_space=pl.ANY)
    in_specs = _ssd_in_specs(True, nc) + [
        pl.BlockSpec((SSD_GROUPS, None, SSD_STATE, 512), lambda c: (0, rev(c), 0, 0)),
        pl.BlockSpec((BLK, SSD_D_INNER), lambda c: (rev(c), 0)),
        any_spec] + [any_spec] * nh
    res = pl.pallas_call(
        body, name=name, grid=(nc,), in_specs=in_specs, input_output_aliases={11: 3},
        out_specs=[pl.BlockSpec((BLK, SSD_D_INNER), lambda c: (rev(c), 0)),
                   pl.BlockSpec((BLK, BC_DIM), lambda c: (rev(c), 0)),
                   pl.BlockSpec((BLK, BC_DIM), lambda c: (rev(c), 0)),
                   pl.BlockSpec((BLK, SSD_D_INNER), lambda c: (rev(c), _OFF["z"] // SSD_D_INNER)),
                   pl.BlockSpec((SSD_HEADS, BLK), lambda c: (0, rev(c))),
                   pl.BlockSpec((SSD_HEADS, 1), whole),
                   pl.BlockSpec((SSD_HEADS, 1), whole),
                   pl.BlockSpec((1, SSD_D_INNER), whole),
                   pl.BlockSpec((1, SSD_D_INNER), whole)] + [any_spec] * nh,
        out_shape=[jax.ShapeDtypeStruct((length, SSD_D_INNER), F32),
                   jax.ShapeDtypeStruct((length, BC_DIM), F32),
                   jax.ShapeDtypeStruct((length, BC_DIM), F32),
                   jax.ShapeDtypeStruct((length, NP), BF),
                   jax.ShapeDtypeStruct((SSD_HEADS, length), F32),
                   jax.ShapeDtypeStruct((SSD_HEADS, 1), F32),
                   jax.ShapeDtypeStruct((SSD_HEADS, 1), F32),
                   jax.ShapeDtypeStruct((1, SSD_D_INNER), F32),
                   jax.ShapeDtypeStruct((1, SSD_D_INNER), F32)]
        + [jax.ShapeDtypeStruct(((4,) + s.shape) if h_bcast else s.shape, s.dtype) for s in h_arrays],
        scratch_shapes=[pltpu.VMEM((SSD_GROUPS, SSD_STATE, 512), F32)]
        + ([pltpu.SemaphoreType.DMA((4 * nh,)), pltpu.SemaphoreType.DMA((4 * nh,))] if nh else []),
        compiler_params=_params(1),
    )(pre, pre, pre, proj, dt_t, dtb, alog, dsk, nw, s_in, dy, dproj, *h_arrays)
    return tuple(res[:9]) + ((list(res[9:]),) if nh else ())


assert _OFF["gates"] == 0 and _OFF["z"] % SSD_D_INNER == 0 and _OFF["q"] % D_MODEL == 0 and _OFF["k"] + 512 == NP


def _attn_in_specs():
    prev = lambda n: (0, jnp.maximum(n - 1, 0), 0)
    cur = lambda n: (0, n, 0)
    return [pl.BlockSpec((BLK, D_MODEL), lambda n: (n, _OFF["q"] // D_MODEL)),
            pl.BlockSpec((2, BLK, BLK), prev),
            pl.BlockSpec((2, BLK, BLK), cur),
            pl.BlockSpec((2, BLK, BLK), prev),
            pl.BlockSpec((2, BLK, BLK), cur),
            pl.BlockSpec((1, ATT_HEADS), lambda n: (0, 0))]


def _attn_fwd(proj, k2, v2, sinks, name, hosted=None):
    length = proj.shape[0]
    nb = length // BLK
    h_arrays, h_bcast = hosted if hosted else ((), False)
    nh = len(h_arrays)

    def body(*refs):
        q, kp, kc, vp, vc, sk = refs[:6]
        o_ref = refs[6 + nh]
        n = pl.program_id(0)
        if nh:
            start, wait = _chip_exchange(refs[6:6 + nh], refs[7 + nh:7 + 2 * nh], refs[-2], refs[-1], h_bcast)

            @pl.when(n == 0)
            def _():
                start()

        for kv in range(2):
            cols = slice(512 * kv, 512 * (kv + 1))
            o = _attn_block(q[:, cols], kp[kv], kc[kv], vp[kv], vc[kv], sk[...], n, kv)
            o_ref[:, cols] = o.astype(o_ref.dtype)

        if nh:
            @pl.when(n == nb - 1)
            def _():
                wait()

    any_spec = pl.BlockSpec(memory_space=pl.ANY)
    res = pl.pallas_call(
        body, name=name, grid=(nb,), in_specs=_attn_in_specs() + [any_spec] * nh,
        out_specs=[pl.BlockSpec((BLK, D_MODEL), lambda n: (n, 0))] + [any_spec] * nh,
        out_shape=[jax.ShapeDtypeStruct((length, D_MODEL), BF)]
        + [jax.ShapeDtypeStruct(((4,) + s.shape) if h_bcast else s.shape, s.dtype) for s in h_arrays],
        scratch_shapes=[pltpu.SemaphoreType.DMA((4 * nh,)), pltpu.SemaphoreType.DMA((4 * nh,))] if nh else [],
        compiler_params=_params(1),
    )(proj, k2, k2, v2, v2, sinks, *h_arrays)
    return (res[0], list(res[1:])) if nh else res[0]


def _attn_bwd(proj, k2, v2, sinks, datt, dproj, name):
    length = proj.shape[0]

    def body(q, kp, kc, vp, vc, sk, do, _, dq, dkp, dkc, dvp, dvc, dsk):
        n = pl.program_id(0)
        dsinks = None
        for kv in range(2):
            cols = slice(512 * kv, 512 * (kv + 1))
            _, vjp = jax.vjp(lambda *a: _attn_block(*a, n, kv), q[:, cols], kp[kv], kc[kv], vp[kv], vc[kv], sk[...])
            g = vjp(do[:, cols].astype(F32))
            dq[:, cols] = g[0].astype(dq.dtype)
            dkp[kv] = g[1]
            dkc[kv] = g[2]
            dvp[kv] = g[3]
            dvc[kv] = g[4]
            dsinks = g[5] if dsinks is None else dsinks + g[5]

        @pl.when(n == 0)
        def _():
            dsk[...] = dsinks

        @pl.when(n > 0)
        def _():
            dsk[...] += dsinks

    blk3 = pl.BlockSpec((2, BLK, BLK), lambda n: (0, n, 0))
    kv_shape = jax.ShapeDtypeStruct((2, length, BLK), F32)
    return pl.pallas_call(
        body, name=name, grid=(length // BLK,),
        in_specs=_attn_in_specs() + [pl.BlockSpec((BLK, D_MODEL), lambda n: (n, 0)), pl.BlockSpec(memory_space=pl.ANY)],
        out_specs=[pl.BlockSpec((BLK, D_MODEL), lambda n: (n, _OFF["q"] // D_MODEL)), blk3, blk3, blk3, blk3,
                   pl.BlockSpec((1, ATT_HEADS), lambda n: (0, 0))],
        out_shape=[jax.ShapeDtypeStruct((length, NP), BF), kv_shape, kv_shape, kv_shape, kv_shape,
                   jax.ShapeDtypeStruct((1, ATT_HEADS), F32)],
        input_output_aliases={7: 0},
        compiler_params=_params(1),
    )(proj, k2, k2, v2, v2, sinks, datt, dproj)


DMA_CHUNK_BYTES = 1 << 20
N_STAGE = 8
LOOKAHEAD = 3


def _piece_chunks(shape, itemsize):
    if len(shape) < 2 or shape[-2] % 16 != 0:
        return [()]
    rows, cols = shape[-2:]
    step = min(rows, max(16, DMA_CHUNK_BYTES // (cols * itemsize) // 16 * 16))
    out = []
    for lead in itertools.product(*[range(d) for d in shape[:-2]]):
        for r0 in range(0, rows, step):
            out.append(lead + (pl.ds(r0, min(step, rows - r0)),))
    return out


def _exchange(srcs, group, bcast, name):
    size = {"c": 2, "xy": 4, "all": 8}[group]
    npeer = size - 1
    n = len(srcs)
    for s in srcs:
        assert bcast or s.shape[0] == size
    pieces = [s.shape if bcast else s.shape[1:] for s in srcs]
    chunks = [_piece_chunks(p, s.dtype.itemsize) for p, s in zip(pieces, srcs)]

    def body(*refs):
        src, out = refs[:n], refs[n:2 * n]
        load_sems, send_sems, recv_sems = refs[2 * n:]
        x, y, c = lax.axis_index("x"), lax.axis_index("y"), lax.axis_index("c")
        if group == "c":
            me = c
        elif group == "xy":
            me = 2 * x + y
        else:
            me = 4 * x + 2 * y + c

        def device(j):
            if group == "c":
                return (x, y, j)
            if group == "xy":
                return (j // 2, j % 2, c)
            return (j // 4, (j // 2) % 2, j % 2)

        def part(ref, idx):
            return ref.at[idx] if idx else ref

        def piece(a, j):
            return src[a] if bcast else src[a].at[j]

        for a in range(n):
            jobs = [(None, idx) for idx in chunks[a]] if bcast else [(d, idx) for idx in chunks[a] for d in range(1, size)]
            full_rows = chunks[a][0][-1].size if chunks[a][0] else None
            slot_shape = (full_rows, pieces[a][-1]) if chunks[a][0] else pieces[a]

            def stream(stage, a=a, jobs=jobs, full_rows=full_rows):
                def slot(q):
                    idx = jobs[q][1]
                    view = stage.at[q % N_STAGE]
                    if idx and idx[-1].size != full_rows:
                        view = view.at[pl.ds(0, idx[-1].size)]
                    return view

                def load(q):
                    d, idx = jobs[q]
                    j = me if d is None else jnp.bitwise_xor(me, d)
                    return pltpu.make_async_copy(part(piece(a, j), idx), slot(q), load_sems.at[q % N_STAGE])

                def sends(q):
                    d, idx = jobs[q]
                    return [pltpu.make_async_remote_copy(
                        src_ref=slot(q), dst_ref=part(out[a].at[me], idx),
                        send_sem=send_sems.at[(q % N_STAGE) * npeer + dd - 1], recv_sem=recv_sems.at[a * size + dd],
                        device_id=device(jnp.bitwise_xor(me, dd)), device_id_type=pl.DeviceIdType.MESH)
                        for dd in (range(1, size) if d is None else (d,))]

                nq = len(jobs)
                for q in range(nq + LOOKAHEAD):
                    if q < nq:
                        if q >= N_STAGE:
                            for cp in sends(q - N_STAGE):
                                cp.wait_send()
                        load(q).start()
                    if q >= LOOKAHEAD:
                        load(q - LOOKAHEAD).wait()
                        for cp in sends(q - LOOKAHEAD):
                            cp.start()
                for q in range(max(0, nq - N_STAGE), nq):
                    for cp in sends(q):
                        cp.wait_send()

            pl.run_scoped(stream, pltpu.VMEM((N_STAGE,) + tuple(slot_shape), srcs[a].dtype))

        for a in range(n):
            for d in range(1, size):
                j = jnp.bitwise_xor(me, d)
                pltpu.make_async_remote_copy(
                    src_ref=piece(a, j), dst_ref=out[a].at[j], send_sem=send_sems.at[0], recv_sem=recv_sems.at[a * size + d],
                    device_id=device(j), device_id_type=pl.DeviceIdType.MESH).wait_recv()

    any_spec = pl.BlockSpec(memory_space=pl.ANY)
    return pl.pallas_call(
        body, name=name, in_specs=[any_spec] * n, out_specs=[any_spec] * n,
        out_shape=[jax.ShapeDtypeStruct((size,) + tuple(p), s.dtype) for p, s in zip(pieces, srcs)],
        scratch_shapes=[pltpu.SemaphoreType.DMA((N_STAGE,)), pltpu.SemaphoreType.DMA((N_STAGE * npeer,)),
                        pltpu.SemaphoreType.DMA((n * size,))],
        compiler_params=pltpu.CompilerParams(vmem_limit_bytes=VMEM_LIMIT_BYTES),
    )(*srcs)


def _with_own(outs, owns, me):
    return [lax.dynamic_update_index_in_dim(o, w, me, 0) for o, w in zip(outs, owns)]


def _sum_slots(arr, name):
    k = arr.shape[0]
    rest = arr.shape[1:]
    width = rest[-1]
    rows_per = math.prod(rest[:-1])
    a2 = arr.reshape(k * rows_per, width)
    tile = rows_per
    for cand in (256, 128, 64, 32, 16, 8):
        if rows_per % cand == 0:
            tile = cand
            break
    nt = rows_per // tile

    def body(*refs):
        acc = refs[0][...]
        for r in refs[1:k]:
            acc = acc + r[...]
        refs[k][...] = acc

    in_specs = [pl.BlockSpec((tile, width), functools.partial(lambda i, s: (s * nt + i, 0), s=s)) for s in range(k)]
    out = pl.pallas_call(
        body, name=name, grid=(nt,), in_specs=in_specs, out_specs=pl.BlockSpec((tile, width), lambda i: (i, 0)),
        out_shape=jax.ShapeDtypeStruct((rows_per, width), arr.dtype), compiler_params=_params(1),
    )(*([a2] * k))
    return out.reshape(rest)


def _sum_pieces(recv, own, own_slotted, me, name):
    k = recv.shape[0]
    rest = recv.shape[1:]
    width = rest[-1]
    rows_per = math.prod(rest[:-1])
    r2 = recv.reshape(k * rows_per, width)
    o2 = own.reshape(-1, width)
    tile = rows_per
    for cand in (256, 128, 64, 32, 16, 8):
        if rows_per % cand == 0:
            tile = cand
            break
    nt = rows_per // tile

    def body(me_ref, *refs):
        mine = refs[k][...].astype(F32)
        acc = None
        for j in range(k):
            term = jnp.where(me_ref[0] == j, mine, refs[j][...].astype(F32))
            acc = term if acc is None else acc + term
        refs[k + 1][...] = acc
        refs[k + 2][...] = acc.astype(BF)

    in_specs = [pl.BlockSpec((tile, width), functools.partial(lambda i, m, s: (s * nt + i, 0), s=s)) for s in range(k)]
    in_specs.append(pl.BlockSpec((tile, width), (lambda i, m: (m[0] * nt + i, 0)) if own_slotted else (lambda i, m: (i, 0))))
    out_spec = pl.BlockSpec((tile, width), lambda i, m: (i, 0))
    out, out_bf = pl.pallas_call(
        body, name=name,
        grid_spec=pltpu.PrefetchScalarGridSpec(num_scalar_prefetch=1, grid=(nt,), in_specs=in_specs, out_specs=[out_spec, out_spec]),
        out_shape=[jax.ShapeDtypeStruct((rows_per, width), F32), jax.ShapeDtypeStruct((rows_per, width), BF)],
        compiler_params=_params(1),
    )(jnp.reshape(me, (1,)).astype(jnp.int32), *([r2] * k), o2)
    return out.reshape(rest), out_bf.reshape(rest)


def _adamw(w, g, m, v, name):
    def fn(w_, g_, m_, v_):
        m1 = ADAM_B1 * m_ + (1.0 - ADAM_B1) * g_
        v1 = ADAM_B2 * v_ + (1.0 - ADAM_B2) * (g_ * g_)
        m_hat = m1 / (1.0 - ADAM_B1 ** ADAM_STEP)
        v_hat = v1 / (1.0 - ADAM_B2 ** ADAM_STEP)
        delta = -ADAM_LR * (m_hat / (jnp.sqrt(v_hat) + ADAM_EPS) + ADAM_WD * w_)
        return (delta, m1, v1), ()

    rows, width = w.shape
    tile = rows
    for cand in (256, 128, 64, 32, 16, 8):
        if rows % cand == 0:
            tile = cand
            break
    o, _ = _rows(fn, [(w, 0, width), (g, 0, width), (m, 0, width), (v, 0, width)], [],
                 [(width, F32)] * 3, [], tile, name)
    return o


def _col_pieces(start, width, shard):
    out = []
    while width:
        chip, off = divmod(start, shard)
        take = min(width, shard - off)
        out.append((chip, off, take))
        start, width = start + take, width - take
    return out


def _gather_cols(parts, ranges, pad=0):
    shard = parts.shape[3]
    halves = []
    for h in range(2):
        cols = [parts[h, chip, :, off:off + w] for s, wd in ranges for chip, off, w in _col_pieces(s, wd, shard)]
        if pad:
            cols.append(jnp.zeros((parts.shape[2], pad), parts.dtype))
        halves.append(jnp.concatenate(cols, axis=1))
    return jnp.concatenate(halves, axis=0)


def _scatter_cols(g, ranges, shard):
    starts, pos = {}, 0
    for s, wd in ranges:
        starts[s] = (pos, wd)
        pos += wd
    order = sorted(starts)
    hr = g.shape[0] // 2
    out = []
    for h in range(2):
        per_chip = []
        for chip in range(4):
            cols = []
            for s in order:
                p0, wd = starts[s]
                lo, hi = max(s, chip * shard), min(s + wd, (chip + 1) * shard)
                if lo < hi:
                    cols.append(g[h * hr:(h + 1) * hr, p0 + lo - s:p0 + hi - s])
            per_chip.append(jnp.concatenate(cols, axis=1))
        out.append(jnp.stack(per_chip))
    return jnp.stack(out)


_IN_RANGES = tuple((o, wd) for _, o, wd in _PACK)


def _double_heads(t):
    length = t.shape[0]
    h = jnp.transpose(t.reshape(length, 2, 64), (1, 0, 2))
    return jnp.concatenate([h, h], axis=-1)


def _fold_heads(d_cur, d_prev):
    length = d_cur.shape[1]
    d = d_cur + jnp.concatenate([d_prev[:, BLK:], jnp.zeros((2, BLK, BLK), F32)], axis=1)
    d = d[..., :64] + d[..., 64:]
    return jnp.transpose(d, (1, 0, 2)).reshape(length, 128)


def _ln_fwd(a, r, g, b, name):
    def fn(a_, r_, g_, b_):
        y = _layer_norm(ALPHA * a_ + r_, g_, b_)
        return (y, y), ()
    o, _ = _rows(fn, [(a, 0, D_MODEL), (r, 0, D_MODEL)], [g, b], [(D_MODEL, F32), (D_MODEL, BF)], [], ROW_TILE, name)
    return o


def _ln_bwd(a, r, d1, d2, g, b, name):
    def fn(a_, r_, d1_, d2_, g_, b_):
        _, vjp = jax.vjp(_layer_norm, ALPHA * a_ + r_, g_, b_)
        ds, dg, db = vjp(ALPHA * d1_ + d2_)
        return (ds, ds), (dg, db)
    o, acc = _rows(fn, [(a, 0, D_MODEL), (r, 0, D_MODEL), (d1, 0, D_MODEL), (d2, 0, D_MODEL)], [g, b],
                   [(D_MODEL, F32), (D_MODEL, BF)], [(1, D_MODEL), (1, D_MODEL)], ROW_TILE, name)
    return o[0], o[1], acc[0], acc[1]


def _layer_fwd(l, h, hb, w, p, hosted=None, hosted_attn=None, hosted_conv=None, late_weights=None):
    sv = {"h": h, "hb": hb}
    proj = _mm(hb, w["win"], "nn", F32, 4096, 512, 1024, f"proj{l}")
    sv["proj"] = proj
    pre = _conv_fwd(proj, p["conv_w"], p["conv_b"], f"conv{l}", hosted_conv)
    if hosted_conv:
        pre, got = pre
        w.update(late_weights(got))
    sv["pre"] = pre
    dt_t = jnp.transpose(proj[:, _OFF["dt"]:_OFF["dt"] + SSD_HEADS])
    sv["dt_t"] = dt_t
    res = _ssd_fwd(pre, proj, dt_t, p["dt_bias"], p["a_log"], p["d_skip_c"], p["ssd_norm_w"], f"ssd{l}", hosted)
    yn, s_in = res[:2]
    if hosted:
        sv["hosted"] = res[2]
    sv["yn"], sv["s_in"] = yn, s_in
    ya = _mm(yn, w["wso"], "nn", F32, 1024, 1024, 2048, f"ssdout{l}")
    k2 = _double_heads(proj[:, _OFF["k"]:_OFF["k"] + 128])
    v2 = _double_heads(proj[:, _OFF["v"]:_OFF["v"] + 128])
    sv["k2"], sv["v2"] = k2, v2
    att = _attn_fwd(proj, k2, v2, p["att_sinks"], f"attn{l}", hosted_attn)
    if hosted_attn:
        att, sv["hosted_attn"] = att
    sv["att"] = att
    yb = _mm(att, w["wao"], "nn", F32, 1024, 1024, 1024, f"attout{l}")
    sv["ya"], sv["yb"] = ya, yb
    gcb = _OFF["gates"] // 1024

    def gate_fn(ga, gb, ya_, yb_):
        return (_sigmoid(ga) * ya_ + _sigmoid(gb) * yb_,), ()
    (ub,), _ = _rows(gate_fn, [(proj, gcb, 1024), (proj, gcb + 1, 1024), (ya, 0, 1024), (yb, 0, 1024)], [],
                     [(D_MODEL, BF)], [], ROW_TILE, f"gate{l}")
    sv["ub"] = ub
    mix = _mm(ub, w["wmo"], "nn", F32, 1024, 1024, 1024, f"mixout{l}")
    sv["mix"] = mix
    h1, h1b = _ln_fwd(h, mix, p["ln_mix_g"], p["ln_mix_b"], f"lnmix{l}")
    sv["h1"], sv["h1b"] = h1, h1b
    gu = _mm(h1b, w["wgu"], "nn", F32, 4096, 512, 1024, f"ffnin{l}")
    sv["gu"] = gu

    def act_fn(g_, u_):
        return (_silu(g_) * u_,), ()
    (act,), _ = _rows(act_fn, [(gu, 0, FFN_HIDDEN), (gu, 1, FFN_HIDDEN)], [], [(FFN_HIDDEN, BF)], [], ROW_TILE, f"swiglu{l}")
    sv["act"] = act
    ffn = _mm(act, w["wd"], "nn", F32, 1024, 1024, FFN_HIDDEN, f"ffnout{l}")
    sv["ffn"] = ffn
    return sv


def _layer_bwd(l, sv, w, p, ds_f, ds_b, hosted=(), early_hook=None, late_hook=None):
    g = {}
    g["w_ffn_down"] = _mm(sv["act"], ds_b, "tn", F32, 1408, 1024, 2048, f"dwd{l}")
    dact = _mm(ds_b, w["wd"], "nt", BF, 1024, FFN_HIDDEN, 1024, f"dact{l}")

    def act_bwd(g_, u_, d_):
        _, vjp = jax.vjp(lambda a, b: _silu(a) * b, g_, u_)
        dg_, du_ = vjp(d_.astype(F32))
        return (jnp.concatenate([dg_, du_], axis=1),), ()
    (dgu,), _ = _rows(act_bwd, [(sv["gu"], 0, FFN_HIDDEN), (sv["gu"], 1, FFN_HIDDEN), (dact, 0, FFN_HIDDEN)], [],
                      [(2 * FFN_HIDDEN, BF)], [], ROW_TILE // 2, f"dswiglu{l}")
    dwgu = _mm(sv["h1b"], dgu, "tn", F32, 1024, 1408, 2048, f"dwgu{l}")
    g["w_ffn_gate"], g["w_ffn_up"] = dwgu[:, :FFN_HIDDEN], dwgu[:, FFN_HIDDEN:]
    dh1 = _mm(dgu, w["wgu"], "nt", F32, 1024, 1024, 2816, f"dh1{l}")
    ds2_f, ds2_b, g["ln_mix_g"], g["ln_mix_b"] = _ln_bwd(sv["h"], sv["mix"], ds_f, dh1, p["ln_mix_g"], p["ln_mix_b"], f"dlnmix{l}")
    g["w_mix_out"] = _mm(sv["ub"], ds2_b, "tn", F32, 1024, 1024, 2048, f"dwmo{l}")
    du = _mm(ds2_b, w["wmo"], "nt", BF, 1024, 1024, 1024, f"du{l}")
    proj = sv["proj"]
    gcb = _OFF["gates"] // 1024

    def gate_bwd(ga, gb, ya_, yb_, du_):
        _, vjp = jax.vjp(lambda a, b, c, d: _sigmoid(a) * c + _sigmoid(b) * d, ga, gb, ya_, yb_)
        dga, dgb, dya, dyb = vjp(du_.astype(F32))
        return (jnp.concatenate([dga, dgb], axis=1), dya, dyb), ()
    (dproj, dya, dyb), _ = _rows(
        gate_bwd, [(proj, gcb, 1024), (proj, gcb + 1, 1024), (sv["ya"], 0, 1024), (sv["yb"], 0, 1024), (du, 0, 1024)], [],
        [(2048, BF, NP), (D_MODEL, BF), (D_MODEL, BF)], [], ROW_TILE, f"dgate{l}")
    g["w_att_out"] = _mm(sv["att"], dyb, "tn", F32, 1024, 1024, 2048, f"dwao{l}")
    datt = _mm(dyb, w["wao"], "nt", BF, 1024, 1024, 1024, f"datt{l}")
    g["w_ssd_out"] = _mm(sv["yn"], dya, "tn", F32, 1024, 1024, 2048, f"dwso{l}")
    dyn = _mm(dya, w["wso"], "nt", F32, 1024, 1024, 1024, f"dyn{l}")
    hosted = list(hosted) + (list(early_hook(g)) if early_hook else [])
    dproj, dkp, dkc, dvp, dvc, g["att_sinks"] = _attn_bwd(proj, sv["k2"], sv["v2"], p["att_sinks"], datt, dproj, f"dattn{l}")
    res = _ssd_bwd(sv["pre"], proj, sv["dt_t"], p["dt_bias"], p["a_log"], p["d_skip_c"], p["ssd_norm_w"], sv["s_in"], dyn,
                   dproj, f"dssd{l}", (hosted, False) if hosted else None)
    (dpx, dpb, dpc, dproj, ddt_t, g["dt_bias"], g["a_log"], ddsk, g["ssd_norm_w"]) = res[:9]
    if hosted:
        g["hosted"] = res[9]
    g["d_skip"] = jnp.sum(ddsk.reshape(SSD_HEADS, 64), axis=1)
    dws, dbs = [], []
    for nm, dpart, part in (("x", dpx, 0), ("b", dpb, 4), ("c", dpc, 5)):
        dproj, dw_, db_ = _conv_bwd(dpart, proj, p["conv_w"], dproj, part, f"dconv{nm}{l}")
        dws.append(dw_)
        dbs.append(db_)
    g["conv_w"], g["conv_b"] = jnp.concatenate(dws, axis=1), jnp.concatenate(dbs, axis=1)
    length = proj.shape[0]
    tail = jnp.concatenate([_fold_heads(dkc, dkp).astype(BF), _fold_heads(dvc, dvp).astype(BF),
                            jnp.transpose(ddt_t).astype(BF), jnp.zeros((length, _PAD), BF)], axis=1)
    dproj = lax.dynamic_update_slice(dproj, tail, (0, _OFF["k"]))
    g["w_in"] = _mm(sv["hb"], dproj, "tn", F32, 1024, 2176, 1024, f"dwin{l}")
    late = list(late_hook(g)) if late_hook else []
    dh = _mm(dproj, w["win"], "nt", F32, 1024, 1024, 2176, f"dh{l}", (late, False) if late else None)
    if late:
        dh, g["late_hosted"] = dh
    return g, ds2_f, dh


def kernel(x, ln_in_g, ln_in_b, w_in, conv_w, conv_b, dt_bias, a_log, d_skip, ssd_norm_w, att_sinks, w_ssd_out, w_att_out, w_mix_out, ln_mix_g, ln_mix_b, w_ffn_gate, w_ffn_up, w_ffn_down, ln_ffn_g, ln_ffn_b, loss_target, m_ln_in_g, m_ln_in_b, m_w_in, m_conv_w, m_conv_b, m_dt_bias, m_a_log, m_d_skip, m_ssd_norm_w, m_att_sinks, m_w_ssd_out, m_w_att_out, m_w_mix_out, m_ln_mix_g, m_ln_mix_b, m_w_ffn_gate, m_w_ffn_up, m_w_ffn_down, m_ln_ffn_g, m_ln_ffn_b, v_ln_in_g, v_ln_in_b, v_w_in, v_conv_w, v_conv_b, v_dt_bias, v_a_log, v_d_skip, v_ssd_norm_w, v_att_sinks, v_w_ssd_out, v_w_att_out, v_w_mix_out, v_ln_mix_g, v_ln_mix_b, v_w_ffn_gate, v_w_ffn_up, v_w_ffn_down, v_ln_ffn_g, v_ln_ffn_b):
    env = dict(locals())
    wts = {n: env[n] for n in WEIGHTS}
    mom1 = {n: env["m_" + n] for n in WEIGHTS}
    mom2 = {n: env["v_" + n] for n in WEIGHTS}
    ci = lax.axis_index("c")
    chip = 2 * lax.axis_index("x") + lax.axis_index("y")
    xs_ = x[0]
    tgt = loss_target[0]

    COLS = ("w_in", "w_ffn_gate", "w_ffn_up")

    def half(t):
        return lax.dynamic_slice_in_dim(t, ci * (t.shape[0] // 2), t.shape[0] // 2, axis=0)

    def assemble(name, parts):
        if name == "w_in":
            return _gather_cols(parts, _IN_RANGES, _PAD)
        if name in COLS:
            return _gather_cols(parts, ((0, 4 * parts.shape[3]),))
        return jnp.concatenate([parts[h, s] for s in range(4) for h in range(2)], axis=0)

    def split(name, gfull):
        if name == "w_in":
            return _scatter_cols(gfull, _IN_RANGES, IN_DIM // 4)
        if name in COLS:
            return _scatter_cols(gfull, ((0, gfull.shape[1]),), gfull.shape[1] // 4)
        hr = gfull.shape[0] // 8
        return jnp.stack([jnp.stack([gfull[(2 * s + h) * hr:(2 * s + h + 1) * hr] for s in range(4)]) for h in range(2)])

    mine = [[half(wts[n][l].astype(BF)) for n in BIG] for l in range(DEPTH)]
    conv_all, = _with_own(_exchange([conv_w], "xy", True, "gather_conv"), [conv_w], chip)
    conv_full = jnp.transpose(conv_all, (1, 2, 0, 3)).reshape(DEPTH, SSD_CONV, CONV_DIM)

    KEYS = {"w_in": "win", "w_ssd_out": "wso", "w_att_out": "wao", "w_mix_out": "wmo", "w_ffn_down": "wd"}

    def layer_weights(l, names, by_chip):
        own = [mine[l][BIG.index(n)] for n in names]
        by_chip = _with_own(by_chip, own, chip)
        parts = _with_own(_exchange(by_chip, "c", True, f"gather_cores{l}_{names[0]}"), by_chip, ci)
        fw = {n: assemble(n, part) for n, part in zip(names, parts)}
        w = {KEYS[n]: fw[n] for n in names if n in KEYS}
        if "w_ffn_gate" in fw:
            w["wgu"] = jnp.concatenate([fw["w_ffn_gate"], fw["w_ffn_up"]], axis=1)
        return w

    layer_p = []
    for l in range(DEPTH):
        layer_p.append({
            "conv_w": conv_full[l], "conv_b": conv_b[l][None],
            "dt_bias": dt_bias[l][:, None], "a_log": a_log[l][:, None],
            "d_skip_c": jnp.repeat(d_skip[l], 64)[None], "ssd_norm_w": ssd_norm_w[l][None],
            "att_sinks": att_sinks[l][None], "ln_mix_g": ln_mix_g[l][None], "ln_mix_b": ln_mix_b[l][None],
            "ln_ffn_g": ln_ffn_g[l][None], "ln_ffn_b": ln_ffn_b[l][None],
        })

    def ln_in_fn(x_, g_, b_):
        y = _layer_norm(x_, g_, b_)
        return (y, y), ()
    (h, hb), _ = _rows(ln_in_fn, [(xs_, 0, D_MODEL)], [ln_in_g[None], ln_in_b[None]], [(D_MODEL, F32), (D_MODEL, BF)], [], ROW_TILE, "ln_in")
    rest = tuple(n for n in BIG if n != "w_in")
    layer_w = [layer_weights(0, ("w_in",), _exchange([mine[0][BIG.index("w_in")]], "xy", True, "gather_chips0"))]
    behind_ssd = [i for i, n in enumerate(BIG) if n in COLS]
    behind_attn = [i for i, n in enumerate(BIG) if n not in COLS]
    saved = []
    for l in range(DEPTH):
        nxt = l + 1 < DEPTH
        sv = _layer_fwd(l, h, hb, layer_w[l], layer_p[l], ([mine[l + 1][i] for i in behind_ssd], True) if nxt else None,
                        ([mine[l + 1][i] for i in behind_attn], True) if nxt else None,
                        ([mine[l][BIG.index(n)] for n in rest], True) if l == 0 else None,
                        functools.partial(layer_weights, l, rest))
        saved.append(sv)
        if nxt:
            got = dict(zip(behind_ssd + behind_attn, sv["hosted"] + sv["hosted_attn"]))
            layer_w.append(layer_weights(l + 1, BIG, [got[i] for i in range(len(BIG))]))
            h, hb = _ln_fwd(sv["h1"], sv["ffn"], layer_p[l]["ln_ffn_g"], layer_p[l]["ln_ffn_b"], f"lnffn{l}")

    def loss_fn(a_, r_, t_, g_, b_):
        y, vjp = jax.vjp(_layer_norm, ALPHA * a_ + r_, g_, b_)
        err = y - t_
        ds, dg, db = vjp(err * (1.0 / D_MODEL))
        part = 0.5 * jnp.sum(jnp.mean(err * err, axis=-1, keepdims=True), axis=0, keepdims=True)
        return (ds, ds), (dg, db, jnp.broadcast_to(part, (1, BLK)))
    sv = saved[-1]
    (ds_f, ds_b), (dg_last, db_last, loss_part) = _rows(
        loss_fn, [(sv["h1"], 0, D_MODEL), (sv["ffn"], 0, D_MODEL), (tgt, 0, D_MODEL)],
        [layer_p[-1]["ln_ffn_g"], layer_p[-1]["ln_ffn_b"]], [(D_MODEL, F32), (D_MODEL, BF)],
        [(1, D_MODEL), (1, D_MODEL), (1, BLK)], ROW_TILE, "loss")
    loss = lax.psum(loss_part[0, 0], ("x", "y", "c"))

    def reduce_cores(l, g, names):
        src = [split(n, g[n]) for n in names]
        recv = _exchange([s.astype(BF) for s in src], "c", False, f"reduce_cores{l}_{names[0]}")
        return {n: _sum_pieces(r, o, True, ci, f"sum_cores{l}_{n}") for n, r, o in zip(names, recv, src)}

    def bf16_of(sums, names):
        return [sums[n][1] for n in names]

    def add_chips(l, sums, from_chips):
        return [_sum_pieces(from_chips[n], sums[n][0], True, chip, f"sum_chips{l}_{n}")[0] for n in BIG]

    EARLY = tuple(n for n in BIG if n != "w_in")
    grads = [None] * DEPTH
    chip_sums = [None] * DEPTH
    for l in reversed(range(DEPTH)):
        sv = saved[l]
        if l == DEPTH - 1:
            lg, lb = dg_last, db_last
        else:
            ds_f, ds_b, lg, lb = _ln_bwd(sv["h1"], sv["ffn"], d1, d2, layer_p[l]["ln_ffn_g"], layer_p[l]["ln_ffn_b"], f"dlnffn{l}")
        sums = {}

        def early_hook(g, l=l, sums=sums):
            sums.update(reduce_cores(l, g, EARLY))
            return bf16_of(sums, EARLY)

        def late_hook(g, l=l, sums=sums):
            sums.update(reduce_cores(l, g, ("w_in",)))
            return bf16_of(sums, ("w_in",))
        if l == 0:
            g, d1, d2 = _layer_bwd(l, sv, layer_w[l], layer_p[l], ds_f, ds_b, bf16_of(chip_sums[l + 1], BIG), early_hook, late_hook)
        else:
            g, d1, d2 = _layer_bwd(l, sv, layer_w[l], layer_p[l], ds_f, ds_b)
            sums.update(reduce_cores(l, g, BIG))
        g["ln_ffn_g"], g["ln_ffn_b"] = lg, lb
        grads[l] = g
        chip_sums[l] = sums

    def ln_in_bwd(x_, d1_, d2_, g_, b_):
        _, vjp = jax.vjp(_layer_norm, x_, g_, b_)
        dx, dg, db = vjp(ALPHA * d1_ + d2_)
        return (dx,), (dg, db)
    (grad_x,), (g_ln_in_g, g_ln_in_b) = _rows(
        ln_in_bwd, [(xs_, 0, D_MODEL), (d1, 0, D_MODEL), (d2, 0, D_MODEL)], [ln_in_g[None], ln_in_b[None]],
        [(D_MODEL, F32)], [(1, D_MODEL), (1, D_MODEL)], ROW_TILE, "dln_in")

    carried = grads[0]["hosted"]
    from_chips = dict(zip(EARLY, carried[len(BIG):]))
    from_chips["w_in"], = grads[0]["late_hosted"]
    reduced = add_chips(0, chip_sums[0], from_chips) + add_chips(1, chip_sums[1], dict(zip(BIG, carried[:len(BIG)])))
    halves = _with_own(_exchange(reduced, "c", True, "swap_halves"), reduced, ci)
    shards = [t.reshape(2 * t.shape[1], t.shape[2]) for t in halves]
    big_grad = {n: jnp.stack([shards[l * len(BIG) + i] for l in range(DEPTH)]).reshape(wts[n].shape) for i, n in enumerate(BIG)}

    small_g = {"ln_in_g": g_ln_in_g[0], "ln_in_b": g_ln_in_b[0]}
    for n in SMALL[2:]:
        small_g[n] = jnp.stack([grads[l][n].reshape(wts[n].shape[1:] if n != "conv_w" else (SSD_CONV, CONV_DIM)) for l in range(DEPTH)])
    flat = jnp.concatenate([small_g[n].reshape(-1) for n in SMALL])
    n_small = flat.shape[0]
    width = -(-n_small // 1024) * 1024
    flat = jnp.pad(flat, (0, width - n_small)).reshape(8, width // 8)
    gathered, = _with_own(_exchange([flat], "all", True, "gather_small"), [flat], 2 * chip + ci)
    total = _sum_slots(gathered, "sum_small").reshape(-1)
    small_grad, off = {}, 0
    for n in SMALL:
        shp = small_g[n].shape
        cnt = math.prod(shp)
        small_grad[n] = total[off:off + cnt].reshape(shp)
        off += cnt
    small_grad["conv_w"] = lax.dynamic_slice_in_dim(small_grad["conv_w"], chip * (CONV_DIM // 4), CONV_DIM // 4, axis=2)

    out_g, out_d, out_m, out_v = {}, {}, {}, {}
    for n in BIG:
        shp = wts[n].shape
        two_d = lambda t: t.reshape(shp[0] * shp[1], shp[2])
        d_, m_, v_ = _adamw(two_d(wts[n]), two_d(big_grad[n]), two_d(mom1[n]), two_d(mom2[n]), f"adamw_{n}")
        out_g[n], out_d[n], out_m[n], out_v[n] = big_grad[n], d_.reshape(shp), m_.reshape(shp), v_.reshape(shp)

    def flat_small(d):
        f = jnp.concatenate([d[n].reshape(-1) for n in SMALL])
        return jnp.pad(f, (0, swidth - f.shape[0])).reshape(8, swidth // 8)
    n_sw = sum(math.prod(wts[n].shape) for n in SMALL)
    swidth = -(-n_sw // 1024) * 1024
    d_, m_, v_ = _adamw(flat_small(wts), flat_small(small_grad), flat_small(mom1), flat_small(mom2), "adamw_small")
    off = 0
    for n in SMALL:
        shp = wts[n].shape
        cnt = math.prod(shp)
        out_g[n] = small_grad[n]
        out_d[n] = d_.reshape(-1)[off:off + cnt].reshape(shp)
        out_m[n] = m_.reshape(-1)[off:off + cnt].reshape(shp)
        out_v[n] = v_.reshape(-1)[off:off + cnt].reshape(shp)
        off += cnt

    return (loss, grad_x[None], *[out_g[n] for n in WEIGHTS], *[out_d[n] for n in WEIGHTS],
            *[out_m[n] for n in WEIGHTS], *[out_v[n] for n in WEIGHTS])
```

```python
import functools
import itertools
import math

import jax
import jax.numpy as jnp
from jax import lax
from jax.experimental import pallas as pl
from jax.experimental.pallas import tpu as pltpu

F32 = jnp.float32
BF = jnp.bfloat16

D_MODEL = 1024
DEPTH = 2
ATT_HEADS = 16
ATT_HEAD_DIM = 64
BLK = 128
SSD_D_INNER = 2048
SSD_HEADS = 32
SSD_GROUPS = 4
SSD_STATE = 128
SSD_CONV = 4
BC_DIM = 512
CONV_DIM = 3072
FFN_HIDDEN = 2816
IN_DIM = 8480
LN_EPS = 1e-5
RMS_EPS = 1e-5
ALPHA = (2 * DEPTH) ** 0.25
ADAM_LR = 0.001
ADAM_B1 = 0.9
ADAM_B2 = 0.999
ADAM_EPS = 1e-08
ADAM_WD = 0.01
ADAM_STEP = 10

_PACK = (("gates", 6432, 2048), ("z", 1280, 2048), ("q", 0, 1024), ("xs", 3328, 2048), ("B", 5376, 512),
         ("C", 5888, 512), ("k", 1024, 128), ("v", 1152, 128), ("dt", 6400, 32))
NP = 8704
_OFF = {}
_o = 0
for _n, _, _w in _PACK:
    _OFF[_n] = _o
    _o += _w
_PAD = NP - _o

VMEM_LIMIT_BYTES = 56 * 1024 * 1024
ROW_TILE = 512
BIG = ("w_in", "w_ssd_out", "w_att_out", "w_mix_out", "w_ffn_gate", "w_ffn_up", "w_ffn_down")
WEIGHTS = ("ln_in_g", "ln_in_b", "w_in", "conv_w", "conv_b", "dt_bias", "a_log", "d_skip", "ssd_norm_w", "att_sinks",
           "w_ssd_out", "w_att_out", "w_mix_out", "ln_mix_g", "ln_mix_b", "w_ffn_gate", "w_ffn_up", "w_ffn_down",
           "ln_ffn_g", "ln_ffn_b")
SMALL = tuple(n for n in WEIGHTS if n not in BIG)


def _params(n_grid):
    return pltpu.CompilerParams(dimension_semantics=("arbitrary",) * n_grid, vmem_limit_bytes=VMEM_LIMIT_BYTES)


def _dot(a, b, ca, cb):
    return lax.dot_general(a.astype(BF), b.astype(BF), (((ca,), (cb,)), ((), ())), preferred_element_type=F32)


@jax.custom_vjp
def _nn(a, b):
    return _dot(a, b, 1, 0)


def _nn_f(a, b):
    return _dot(a, b, 1, 0), (a.astype(BF), b.astype(BF))


def _nn_b(res, g):
    a, b = res
    return _dot(g, b, 1, 1), _dot(a, g, 0, 0)


_nn.defvjp(_nn_f, _nn_b)


@jax.custom_vjp
def _nt(a, b):
    return _dot(a, b, 1, 1)


def _nt_f(a, b):
    return _dot(a, b, 1, 1), (a.astype(BF), b.astype(BF))


def _nt_b(res, g):
    a, b = res
    return _dot(g, b, 1, 0), _dot(g, a, 0, 0)


_nt.defvjp(_nt_f, _nt_b)


@jax.custom_vjp
def _tn(a, b):
    return _dot(a, b, 0, 0)


def _tn_f(a, b):
    return _dot(a, b, 0, 0), (a.astype(BF), b.astype(BF))


def _tn_b(res, g):
    a, b = res
    return _dot(b, g, 1, 1), _dot(a, g, 1, 0)


_tn.defvjp(_tn_f, _tn_b)


def _sigmoid(x):
    return 0.5 * jnp.tanh(0.5 * x) + 0.5


def _silu(x):
    return x * _sigmoid(x)


def _layer_norm(s, g, b):
    mu = jnp.mean(s, axis=-1, keepdims=True)
    sc = s - mu
    var = jnp.mean(sc * sc, axis=-1, keepdims=True)
    return sc * lax.rsqrt(var + LN_EPS) * g + b


def _ssd_group(pre_x, pre_b, pre_c, z, dtr, dtb, alog, dsk, nw, state):
    t = pre_x.shape[0]
    xs, bm, cm = _silu(pre_x), _silu(pre_b), _silu(pre_c)
    dt = jax.nn.softplus(dtr + dtb)
    da = dt * (-jnp.exp(alog))
    row = lax.broadcasted_iota(jnp.int32, (t, t), 0)
    col = lax.broadcasted_iota(jnp.int32, (t, t), 1)
    upper = (row <= col).astype(F32)
    cum = jnp.dot(da, upper, precision=lax.Precision.HIGHEST, preferred_element_type=F32)
    tot = jnp.sum(da, axis=1, keepdims=True)
    cb = _nt(cm, bm)
    tril = row >= col
    first = lax.broadcasted_iota(jnp.int32, (t, BLK), 1) < 64
    first_row = lax.broadcasted_iota(jnp.int32, (1, BLK), 1) < 64

    def col_form(r):
        return jnp.broadcast_to(r, (t, t)).T

    ys, new_state = [], []
    for p in range(4):
        h0, h1 = 2 * p, 2 * p + 1
        sl = slice(BLK * p, BLK * (p + 1))
        x_pair = xs[:, sl] * jnp.where(first, col_form(dt[h0:h0 + 1]), col_form(dt[h1:h1 + 1]))
        y_pair = None
        cc = []
        for h, keep in ((h0, first), (h1, jnp.logical_not(first))):
            cr = jnp.broadcast_to(cum[h:h + 1], (t, t))
            cc.append(cr.T)
            decay = jnp.exp(jnp.where(tril, cc[-1] - cr, -1e30))
            y_h = _nn(decay * cb, jnp.where(keep, x_pair, 0.0))
            y_pair = y_h if y_pair is None else y_pair + y_h
        s_pair = state[:, sl]
        y_pair = y_pair + _nn(cm, s_pair) * jnp.where(first, jnp.exp(cc[0]), jnp.exp(cc[1]))
        to_end = jnp.where(first, jnp.exp(tot[h0:h0 + 1] - cc[0]), jnp.exp(tot[h1:h1 + 1] - cc[1]))
        chunk_decay = jnp.where(first_row, jnp.exp(tot[h0:h0 + 1]), jnp.exp(tot[h1:h1 + 1]))
        new_state.append(s_pair * chunk_decay + _tn(bm, x_pair * to_end))
        ys.append(y_pair + dsk[:, sl] * xs[:, sl])
    y = jnp.concatenate(ys, axis=1) * _silu(z)
    y = y * lax.rsqrt(jnp.mean(y * y, axis=-1, keepdims=True) + RMS_EPS) * nw
    return y, jnp.concatenate(new_state, axis=1)


def _attn_block(q, kp, kc, vp, vc, sinks, n, kv):
    t = q.shape[0]
    kb = jnp.concatenate([kp, kc], axis=0)
    vb = jnp.concatenate([vp, vc], axis=0)
    qi = lax.broadcasted_iota(jnp.int32, (t, 2 * t), 0)
    kj = lax.broadcasted_iota(jnp.int32, (t, 2 * t), 1)
    rel = qi + t - kj
    valid = (rel >= 0) & (rel < t) & ((n * t - t + kj) >= 0)
    relf = rel.astype(F32)
    first = lax.broadcasted_iota(jnp.int32, (t, BLK), 1) < 64
    lane16 = lax.broadcasted_iota(jnp.int32, (1, ATT_HEADS), 1)
    outs = []
    for p in range(4):
        qp = q[:, BLK * p:BLK * (p + 1)] * (ATT_HEAD_DIM ** -0.5)
        o_pair = None
        for half, keep in enumerate((first, jnp.logical_not(first))):
            h = kv * 8 + 2 * p + half
            s = _nt(jnp.where(keep, qp, 0.0), kb)
            slope = jnp.exp((-8.0 * math.log(2.0) / ATT_HEADS) * jnp.asarray(h + 1, F32))
            s = jnp.where(valid, s - slope * relf, -1e30)
            sink = jnp.sum(jnp.where(lane16 == h, sinks, 0.0), axis=1, keepdims=True)
            m = lax.stop_gradient(jnp.maximum(jnp.max(s, axis=1, keepdims=True), sink))
            e = jnp.exp(s - m)
            den = jnp.sum(e, axis=1, keepdims=True) + jnp.exp(sink - m)
            o = _nn(e * (1.0 / den), vb)
            o_pair = o if o_pair is None else jnp.where(first, o_pair, o)
        outs.append(o_pair)
    return jnp.concatenate(outs, axis=1)


def _chip_exchange(src, out, send_sems, recv_sems, bcast):
    x, y, c = lax.axis_index("x"), lax.axis_index("y"), lax.axis_index("c")
    me = 2 * x + y

    def copy(a, d, slot):
        j = jnp.bitwise_xor(me, d)
        return pltpu.make_async_remote_copy(
            src_ref=src[a] if bcast else src[a].at[j], dst_ref=out[a].at[slot],
            send_sem=send_sems.at[a * 4 + d], recv_sem=recv_sems.at[a * 4 + d],
            device_id=(j // 2, j % 2, c), device_id_type=pl.DeviceIdType.MESH)

    pairs = [(a, d) for a in range(len(src)) for d in range(1, 4)]

    def start():
        for a, d in pairs:
            copy(a, d, me).start()

    def wait():
        for a, d in pairs:
            copy(a, d, jnp.bitwise_xor(me, d)).wait_recv()
        for a, d in pairs:
            copy(a, d, me).wait_send()

    return start, wait


def _mm(a, b, mode, out_dtype, tm, tn, tk, name, hosted=None):
    if mode == "nn":
        (m, k), (k2, n) = a.shape, b.shape
    elif mode == "nt":
        (m, k), (n, k2) = a.shape, b.shape
    else:
        (k, m), (k2, n) = a.shape, b.shape
    assert k == k2, (a.shape, b.shape, mode)
    tm, tn, tk = min(tm, m), min(tn, n), min(tk, k)
    assert m % tm == 0 and n % tn == 0 and k % tk == 0, (m, n, k, tm, tn, tk)
    nk = k // tk
    grid = (m // tm, n // tn, nk)
    ca, cb = {"nn": (1, 0), "nt": (1, 1), "tn": (0, 0)}[mode]
    h_arrays, h_bcast = hosted if hosted else ((), False)
    nh = len(h_arrays)

    def body(*refs):
        a_ref, b_ref = refs[:2]
        o_ref = refs[2 + nh]
        scratch = refs[3 + 2 * nh:]
        step = [pl.program_id(ax) for ax in range(3)]
        if nh:
            start, wait = _chip_exchange(refs[2:2 + nh], refs[3 + nh:3 + 2 * nh], scratch[-2], scratch[-1], h_bcast)

            @pl.when((step[0] == 0) & (step[1] == 0) & (step[2] == 0))
            def _():
                start()

        part = _dot(a_ref[...], b_ref[...], ca, cb)
        if nk == 1:
            o_ref[...] = part.astype(o_ref.dtype)
        else:
            acc_ref = scratch[0]

            @pl.when(step[2] == 0)
            def _():
                acc_ref[...] = part

            @pl.when(step[2] > 0)
            def _():
                acc_ref[...] += part

            @pl.when(step[2] == nk - 1)
            def _():
                o_ref[...] = acc_ref[...].astype(o_ref.dtype)

        if nh:
            @pl.when((step[0] == grid[0] - 1) & (step[1] == grid[1] - 1) & (step[2] == nk - 1))
            def _():
                wait()

    a_spec = pl.BlockSpec((tk, tm), lambda i, j, kk: (kk, i)) if mode == "tn" else pl.BlockSpec((tm, tk), lambda i, j, kk: (i, kk))
    b_spec = pl.BlockSpec((tn, tk), lambda i, j, kk: (j, kk)) if mode == "nt" else pl.BlockSpec((tk, tn), lambda i, j, kk: (kk, j))
    any_spec = pl.BlockSpec(memory_space=pl.ANY)
    res = pl.pallas_call(
        body, name=name, grid=grid,
        in_specs=[a_spec, b_spec] + [any_spec] * nh,
        out_specs=[pl.BlockSpec((tm, tn), lambda i, j, kk: (i, j))] + [any_spec] * nh,
        out_shape=[jax.ShapeDtypeStruct((m, n), out_dtype)]
        + [jax.ShapeDtypeStruct(((4,) + s.shape) if h_bcast else s.shape, s.dtype) for s in h_arrays],
        scratch_shapes=([] if nk == 1 else [pltpu.VMEM((tm, tn), F32)])
        + ([pltpu.SemaphoreType.DMA((4 * nh,)), pltpu.SemaphoreType.DMA((4 * nh,))] if nh else []),
        compiler_params=_params(3),
    )(a, b, *h_arrays)
    return (res[0], list(res[1:])) if nh else res[0]


def _rows(fn, rows, params, outs, accs, tile, name):
    length = rows[0][0].shape[0]
    tile = min(tile, length)
    assert length % tile == 0
    nr, npar, no = len(rows), len(params), len(outs)

    def body(*refs):
        vals = [r[...] for r in refs[:nr + npar]]
        o, a = fn(*vals)
        for ref, val in zip(refs[nr + npar:nr + npar + no], o):
            ref[...] = val.astype(ref.dtype)
        i = pl.program_id(0)
        for ref, val in zip(refs[nr + npar + no:], a):
            @pl.when(i == 0)
            def _(ref=ref, val=val):
                ref[...] = val

            @pl.when(i > 0)
            def _(ref=ref, val=val):
                ref[...] += val

    in_specs = [pl.BlockSpec((tile, w), functools.partial(lambda i, cb: (i, cb), cb=cb)) for _, cb, w in rows]
    in_specs += [pl.BlockSpec(p.shape, lambda i: (0, 0)) for p in params]
    outs = [o if len(o) == 3 else (o[0], o[1], o[0]) for o in outs]
    out_specs = [pl.BlockSpec((tile, w), lambda i: (i, 0)) for w, _, _ in outs]
    out_specs += [pl.BlockSpec((r, w), lambda i: (0, 0)) for r, w in accs]
    out_shape = [jax.ShapeDtypeStruct((length, full), dt) for _, dt, full in outs]
    out_shape += [jax.ShapeDtypeStruct((r, w), F32) for r, w in accs]
    res = pl.pallas_call(
        body, name=name, grid=(length // tile,), in_specs=in_specs, out_specs=out_specs, out_shape=out_shape,
        compiler_params=_params(1),
    )(*[r[0] for r in rows], *params)
    return res[:no], res[no:]


CONV_SUB = 32


def _conv_fwd(proj, conv_w, conv_b, name, hosted=None):
    length = proj.shape[0]
    tl, cw = min(1024, length), 512
    cb0 = _OFF["xs"] // cw
    grid = (CONV_DIM // cw, length // tl)
    h_arrays, h_bcast = hosted if hosted else ((), False)
    nh = len(h_arrays)

    def body(*refs):
        u_ref, halo_ref, w_ref, b_ref = refs[:4]
        o_ref = refs[4 + nh]
        win = refs[5 + 2 * nh]
        j, i = pl.program_id(0), pl.program_id(1)
        if nh:
            start, wait = _chip_exchange(refs[4:4 + nh], refs[5 + nh:5 + 2 * nh], refs[-2], refs[-1], h_bcast)

            @pl.when((j == 0) & (i == 0))
            def _():
                start()

            @pl.when((j == grid[0] - 1) & (i == grid[1] - 1))
            def _():
                wait()

        win[0:8, :] = jnp.where(i > 0, halo_ref[...], 0.0)
        win[8:8 + tl, :] = u_ref[...]
        taps = [w_ref[kk:kk + 1, :] for kk in range(SSD_CONV)]
        bias = b_ref[...]
        for r in range(0, tl, CONV_SUB):
            acc = bias + taps[0] * win[5 + r:5 + r + CONV_SUB, :]
            for kk in range(1, SSD_CONV):
                acc = acc + taps[kk] * win[5 + kk + r:5 + kk + r + CONV_SUB, :]
            o_ref[r:r + CONV_SUB, :] = acc

    any_spec = pl.BlockSpec(memory_space=pl.ANY)
    res = pl.pallas_call(
        body, name=name, grid=grid,
        in_specs=[pl.BlockSpec((tl, cw), lambda j, i: (i, cb0 + j)),
                  pl.BlockSpec((8, cw), lambda j, i: (jnp.maximum(i * (tl // 8) - 1, 0), cb0 + j)),
                  pl.BlockSpec((SSD_CONV, cw), lambda j, i: (0, j)),
                  pl.BlockSpec((1, cw), lambda j, i: (0, j))] + [any_spec] * nh,
        out_specs=[pl.BlockSpec((tl, cw), lambda j, i: (i, j))] + [any_spec] * nh,
        out_shape=[jax.ShapeDtypeStruct((length, CONV_DIM), F32)]
        + [jax.ShapeDtypeStruct(((4,) + s.shape) if h_bcast else s.shape, s.dtype) for s in h_arrays],
        scratch_shapes=[pltpu.VMEM((8 + tl, cw), F32)]
        + ([pltpu.SemaphoreType.DMA((4 * nh,)), pltpu.SemaphoreType.DMA((4 * nh,))] if nh else []),
        compiler_params=_params(2),
    )(proj, proj, conv_w, conv_b, *h_arrays)
    return (res[0], list(res[1:])) if nh else res[0]


def _conv_bwd(dpre, proj, conv_w, dproj, part, name):
    length = proj.shape[0]
    width = dpre.shape[1]
    tl, cw = min(512, length), 512
    cb0 = _OFF["xs"] // cw + part
    n_t = length // tl

    def body(d_ref, dnext_ref, u_ref, halo_ref, w_ref, _, du_ref, dw_ref, db_ref, dwin, uwin):
        i = pl.program_id(1)
        dwin[0:tl, :] = d_ref[...]
        dwin[tl:tl + 8, :] = jnp.where(i < n_t - 1, dnext_ref[...], 0.0)
        uwin[0:8, :] = jnp.where(i > 0, halo_ref[...], 0.0)
        uwin[8:8 + tl, :] = u_ref[...]
        taps = [w_ref[kk:kk + 1, :] for kk in range(SSD_CONV)]
        sub = CONV_SUB // 2
        acc_w = [jnp.zeros((sub, cw), F32) for _ in range(SSD_CONV)]
        acc_b = jnp.zeros((sub, cw), F32)
        for r in range(0, tl, sub):
            d = dwin[r:r + sub, :]
            du = taps[3] * d
            for kk in range(SSD_CONV - 1):
                du = du + taps[kk] * dwin[3 - kk + r:3 - kk + r + sub, :]
            du_ref[r:r + sub, :] = du.astype(du_ref.dtype)
            for kk in range(SSD_CONV):
                acc_w[kk] = acc_w[kk] + d * uwin[5 + kk + r:5 + kk + r + sub, :]
            acc_b = acc_b + d
        dw = jnp.concatenate([jnp.sum(a, axis=0, keepdims=True) for a in acc_w], axis=0)
        db = jnp.sum(acc_b, axis=0, keepdims=True)

        @pl.when(i == 0)
        def _():
            dw_ref[...] = dw
            db_ref[...] = db

        @pl.when(i > 0)
        def _():
            dw_ref[...] += dw
            db_ref[...] += db

    return pl.pallas_call(
        body, name=name, grid=(width // cw, n_t), input_output_aliases={5: 0},
        in_specs=[pl.BlockSpec((tl, cw), lambda j, i: (i, j)),
                  pl.BlockSpec((8, cw), lambda j, i: (jnp.minimum((i + 1) * (tl // 8), length // 8 - 1), j)),
                  pl.BlockSpec((tl, cw), lambda j, i: (i, cb0 + j)),
                  pl.BlockSpec((8, cw), lambda j, i: (jnp.maximum(i * (tl // 8) - 1, 0), cb0 + j)),
                  pl.BlockSpec((SSD_CONV, cw), lambda j, i: (0, part + j)),
                  pl.BlockSpec(memory_space=pl.ANY)],
        out_specs=[pl.BlockSpec((tl, cw), lambda j, i: (i, cb0 + j)),
                   pl.BlockSpec((SSD_CONV, cw), lambda j, i: (0, j)),
                   pl.BlockSpec((1, cw), lambda j, i: (0, j))],
        out_shape=[jax.ShapeDtypeStruct((length, NP), BF),
                   jax.ShapeDtypeStruct((SSD_CONV, width), F32),
                   jax.ShapeDtypeStruct((1, width), F32)],
        scratch_shapes=[pltpu.VMEM((tl + 8, cw), F32), pltpu.VMEM((8 + tl, cw), F32)],
        compiler_params=_params(2),
    )(dpre, dpre, proj, proj, conv_w, dproj)


def _ssd_in_specs(rev, nc):
    def cidx(c):
        return nc - 1 - c if rev else c
    whole = lambda c: (0, 0)
    return [pl.BlockSpec((BLK, SSD_D_INNER), lambda c: (cidx(c), 0)),
            pl.BlockSpec((BLK, BC_DIM), lambda c: (cidx(c), SSD_D_INNER // BC_DIM)),
            pl.BlockSpec((BLK, BC_DIM), lambda c: (cidx(c), SSD_D_INNER // BC_DIM + 1)),
            pl.BlockSpec((BLK, SSD_D_INNER), lambda c: (cidx(c), _OFF["z"] // SSD_D_INNER)),
            pl.BlockSpec((SSD_HEADS, BLK), lambda c: (0, cidx(c))),
            pl.BlockSpec((SSD_HEADS, 1), whole),
            pl.BlockSpec((SSD_HEADS, 1), whole),
            pl.BlockSpec((1, SSD_D_INNER), whole),
            pl.BlockSpec((1, SSD_D_INNER), whole)]


def _group_args(g, px, pb, pc, z, dtr, dtb, alog, dsk, nw):
    wide, narrow, heads = slice(512 * g, 512 * (g + 1)), slice(BLK * g, BLK * (g + 1)), slice(8 * g, 8 * (g + 1))
    return (px[:, wide], pb[:, narrow], pc[:, narrow], z[:, wide], dtr[heads, :], dtb[heads, :], alog[heads, :],
            dsk[:, wide], nw[:, wide])


def _ssd_fwd(pre, proj, dt_t, dtb, alog, dsk, nw, name, hosted=None):
    length = pre.shape[0]
    nc = length // BLK
    h_arrays, h_bcast = hosted if hosted else ((), False)
    nh = len(h_arrays)

    def body(*refs):
        px, pb, pc, z, dtr, dtb_r, al_r, dsk_r, nw_r = refs[:9]
        y_ref, sin_ref = refs[9 + nh:11 + nh]
        st = refs[11 + 2 * nh]
        c = pl.program_id(0)
        if nh:
            start, wait = _chip_exchange(refs[9:9 + nh], refs[11 + nh:11 + 2 * nh], refs[-2], refs[-1], h_bcast)

        @pl.when(c == 0)
        def _():
            st[...] = jnp.zeros_like(st)
            if nh:
                start()

        for g in range(SSD_GROUPS):
            s_in = st[g]
            sin_ref[g] = s_in
            y, s_out = _ssd_group(*_group_args(g, px, pb, pc, z, dtr, dtb_r, al_r, dsk_r, nw_r), s_in)
            y_ref[:, 512 * g:512 * (g + 1)] = y.astype(y_ref.dtype)
            st[g] = s_out

        if nh:
            @pl.when(c == nc - 1)
            def _():
                wait()

    any_spec = pl.BlockSpec(memory_space=pl.ANY)
    res = pl.pallas_call(
        body, name=name, grid=(nc,), in_specs=_ssd_in_specs(False, nc) + [any_spec] * nh,
        out_specs=[pl.BlockSpec((BLK, SSD_D_INNER), lambda c: (c, 0)),
                   pl.BlockSpec((SSD_GROUPS, None, SSD_STATE, 512), lambda c: (0, c, 0, 0))] + [any_spec] * nh,
        out_shape=[jax.ShapeDtypeStruct((length, SSD_D_INNER), BF),
                   jax.ShapeDtypeStruct((SSD_GROUPS, nc, SSD_STATE, 512), F32)]
        + [jax.ShapeDtypeStruct(((4,) + s.shape) if h_bcast else s.shape, s.dtype) for s in h_arrays],
        scratch_shapes=[pltpu.VMEM((SSD_GROUPS, SSD_STATE, 512), F32)]
        + ([pltpu.SemaphoreType.DMA((4 * nh,)), pltpu.SemaphoreType.DMA((4 * nh,))] if nh else []),
        compiler_params=_params(1),
    )(pre, pre, pre, proj, dt_t, dtb, alog, dsk, nw, *h_arrays)
    return (res[0], res[1], list(res[2:])) if nh else (res[0], res[1])


def _ssd_bwd(pre, proj, dt_t, dtb, alog, dsk, nw, s_in, dy, dproj, name, hosted=None):
    length = pre.shape[0]
    nc = length // BLK
    h_arrays, h_bcast = hosted if hosted else ((), False)
    nh = len(h_arrays)

    def body(*refs):
        px, pb, pc, z, dtr, dtb_r, al_r, dsk_r, nw_r, sin_r, dy_r = refs[:11]
        dpx, dpb, dpc, dz, ddt, ddtb, dal, ddsk, dnw = refs[12 + nh:21 + nh]
        dst = refs[21 + 2 * nh]
        c = pl.program_id(0)
        if nh:
            start, wait = _chip_exchange(refs[12:12 + nh], refs[21 + nh:21 + 2 * nh], refs[-2], refs[-1], h_bcast)

        @pl.when(c == 0)
        def _():
            dst[...] = jnp.zeros_like(dst)
            if nh:
                start()

        for grp in range(SSD_GROUPS):
            wide, narrow, heads = slice(512 * grp, 512 * (grp + 1)), slice(BLK * grp, BLK * (grp + 1)), slice(8 * grp, 8 * (grp + 1))
            _, vjp = jax.vjp(_ssd_group, *_group_args(grp, px, pb, pc, z, dtr, dtb_r, al_r, dsk_r, nw_r), sin_r[grp])
            g = vjp((dy_r[:, wide], dst[grp]))
            dpx[:, wide] = g[0]
            dpb[:, narrow] = g[1]
            dpc[:, narrow] = g[2]
            dz[:, wide] = g[3].astype(dz.dtype)
            ddt[heads, :] = g[4]
            dst[grp] = g[9]
            for ref, val, idx in ((ddtb, g[5], (heads, slice(None))), (dal, g[6], (heads, slice(None))),
                                  (ddsk, g[7], (slice(None), wide)), (dnw, g[8], (slice(None), wide))):
                @pl.when(c == 0)
                def _(ref=ref, val=val, idx=idx):
                    ref[idx] = val

                @pl.when(c > 0)
                def _(ref=ref, val=val, idx=idx):
                    ref[idx] += val

        if nh:
            @pl.when(c == nc - 1)
            def _():
                wait()

    rev = lambda c: nc - 1 - c
    whole = lambda c: (0, 0)
    any_spec = pl.BlockSpec(memory_space=pl.ANY)
    in_specs = _ssd_in_specs(True, nc) + [
        pl.BlockSpec((SSD_GROUPS, None, SSD_STATE, 512), lambda c: (0, rev(c), 0, 0)),
        pl.BlockSpec((BLK, SSD_D_INNER), lambda c: (rev(c), 0)),
        any_spec] + [any_spec] * nh
    res = pl.pallas_call(
        body, name=name, grid=(nc,), in_specs=in_specs, input_output_aliases={11: 3},
        out_specs=[pl.BlockSpec((BLK, SSD_D_INNER), lambda c: (rev(c), 0)),
                   pl.BlockSpec((BLK, BC_DIM), lambda c: (rev(c), 0)),
                   pl.BlockSpec((BLK, BC_DIM), lambda c: (rev(c), 0)),
                   pl.BlockSpec((BLK, SSD_D_INNER), lambda c: (rev(c), _OFF["z"] // SSD_D_INNER)),
                   pl.BlockSpec((SSD_HEADS, BLK), lambda c: (0, rev(c))),
                   pl.BlockSpec((SSD_HEADS, 1), whole),
                   pl.BlockSpec((SSD_HEADS, 1), whole),
                   pl.BlockSpec((1, SSD_D_INNER), whole),
                   pl.BlockSpec((1, SSD_D_INNER), whole)] + [any_spec] * nh,
        out_shape=[jax.ShapeDtypeStruct((length, SSD_D_INNER), F32),
                   jax.ShapeDtypeStruct((length, BC_DIM), F32),
                   jax.ShapeDtypeStruct((length, BC_DIM), F32),
                   jax.ShapeDtypeStruct((length, NP), BF),
                   jax.ShapeDtypeStruct((SSD_HEADS, length), F32),
                   jax.ShapeDtypeStruct((SSD_HEADS, 1), F32),
                   jax.ShapeDtypeStruct((SSD_HEADS, 1), F32),
                   jax.ShapeDtypeStruct((1, SSD_D_INNER), F32),
                   jax.ShapeDtypeStruct((1, SSD_D_INNER), F32)]
        + [jax.ShapeDtypeStruct(((4,) + s.shape) if h_bcast else s.shape, s.dtype) for s in h_arrays],
        scratch_shapes=[pltpu.VMEM((SSD_GROUPS, SSD_STATE, 512), F32)]
        + ([pltpu.SemaphoreType.DMA((4 * nh,)), pltpu.SemaphoreType.DMA((4 * nh,))] if nh else []),
        compiler_params=_params(1),
    )(pre, pre, pre, proj, dt_t, dtb, alog, dsk, nw, s_in, dy, dproj, *h_arrays)
    return tuple(res[:9]) + ((list(res[9:]),) if nh else ())


assert _OFF["gates"] == 0 and _OFF["z"] % SSD_D_INNER == 0 and _OFF["q"] % D_MODEL == 0 and _OFF["k"] + 512 == NP


def _attn_in_specs():
    prev = lambda n: (0, jnp.maximum(n - 1, 0), 0)
    cur = lambda n: (0, n, 0)
    return [pl.BlockSpec((BLK, D_MODEL), lambda n: (n, _OFF["q"] // D_MODEL)),
            pl.BlockSpec((2, BLK, BLK), prev),
            pl.BlockSpec((2, BLK, BLK), cur),
            pl.BlockSpec((2, BLK, BLK), prev),
            pl.BlockSpec((2, BLK, BLK), cur),
            pl.BlockSpec((1, ATT_HEADS), lambda n: (0, 0))]


def _attn_fwd(proj, k2, v2, sinks, name, hosted=None):
    length = proj.shape[0]
    nb = length // BLK
    h_arrays, h_bcast = hosted if hosted else ((), False)
    nh = len(h_arrays)

    def body(*refs):
        q, kp, kc, vp, vc, sk = refs[:6]
        o_ref = refs[6 + nh]
        n = pl.program_id(0)
        if nh:
            start, wait = _chip_exchange(refs[6:6 + nh], refs[7 + nh:7 + 2 * nh], refs[-2], refs[-1], h_bcast)

            @pl.when(n == 0)
            def _():
                start()

        for kv in range(2):
            cols = slice(512 * kv, 512 * (kv + 1))
            o = _attn_block(q[:, cols], kp[kv], kc[kv], vp[kv], vc[kv], sk[...], n, kv)
            o_ref[:, cols] = o.astype(o_ref.dtype)

        if nh:
            @pl.when(n == nb - 1)
            def _():
                wait()

    any_spec = pl.BlockSpec(memory_space=pl.ANY)
    res = pl.pallas_call(
        body, name=name, grid=(nb,), in_specs=_attn_in_specs() + [any_spec] * nh,
        out_specs=[pl.BlockSpec((BLK, D_MODEL), lambda n: (n, 0))] + [any_spec] * nh,
        out_shape=[jax.ShapeDtypeStruct((length, D_MODEL), BF)]
        + [jax.ShapeDtypeStruct(((4,) + s.shape) if h_bcast else s.shape, s.dtype) for s in h_arrays],
        scratch_shapes=[pltpu.SemaphoreType.DMA((4 * nh,)), pltpu.SemaphoreType.DMA((4 * nh,))] if nh else [],
        compiler_params=_params(1),
    )(proj, k2, k2, v2, v2, sinks, *h_arrays)
    return (res[0], list(res[1:])) if nh else res[0]


def _attn_bwd(proj, k2, v2, sinks, datt, dproj, name):
    length = proj.shape[0]

    def body(q, kp, kc, vp, vc, sk, do, _, dq, dkp, dkc, dvp, dvc, dsk):
        n = pl.program_id(0)
        dsinks = None
        for kv in range(2):
            cols = slice(512 * kv, 512 * (kv + 1))
            _, vjp = jax.vjp(lambda *a: _attn_block(*a, n, kv), q[:, cols], kp[kv], kc[kv], vp[kv], vc[kv], sk[...])
            g = vjp(do[:, cols].astype(F32))
            dq[:, cols] = g[0].astype(dq.dtype)
            dkp[kv] = g[1]
            dkc[kv] = g[2]
            dvp[kv] = g[3]
            dvc[kv] = g[4]
            dsinks = g[5] if dsinks is None else dsinks + g[5]

        @pl.when(n == 0)
        def _():
            dsk[...] = dsinks

        @pl.when(n > 0)
        def _():
            dsk[...] += dsinks

    blk3 = pl.BlockSpec((2, BLK, BLK), lambda n: (0, n, 0))
    kv_shape = jax.ShapeDtypeStruct((2, length, BLK), F32)
    return pl.pallas_call(
        body, name=name, grid=(length // BLK,),
        in_specs=_attn_in_specs() + [pl.BlockSpec((BLK, D_MODEL), lambda n: (n, 0)), pl.BlockSpec(memory_space=pl.ANY)],
        out_specs=[pl.BlockSpec((BLK, D_MODEL), lambda n: (n, _OFF["q"] // D_MODEL)), blk3, blk3, blk3, blk3,
                   pl.BlockSpec((1, ATT_HEADS), lambda n: (0, 0))],
        out_shape=[jax.ShapeDtypeStruct((length, NP), BF), kv_shape, kv_shape, kv_shape, kv_shape,
                   jax.ShapeDtypeStruct((1, ATT_HEADS), F32)],
        input_output_aliases={7: 0},
        compiler_params=_params(1),
    )(proj, k2, k2, v2, v2, sinks, datt, dproj)


DMA_CHUNK_BYTES = 1 << 20
N_STAGE = 8
LOOKAHEAD = 3


def _piece_chunks(shape, itemsize):
    if len(shape) < 2 or shape[-2] % 16 != 0:
        return [()]
    rows, cols = shape[-2:]
    step = min(rows, max(16, DMA_CHUNK_BYTES // (cols * itemsize) // 16 * 16))
    out = []
    for lead in itertools.product(*[range(d) for d in shape[:-2]]):
        for r0 in range(0, rows, step):
            out.append(lead + (pl.ds(r0, min(step, rows - r0)),))
    return out


def _exchange(srcs, group, bcast, name):
    size = {"c": 2, "xy": 4, "all": 8}[group]
    npeer = size - 1
    n = len(srcs)
    for s in srcs:
        assert bcast or s.shape[0] == size
    pieces = [s.shape if bcast else s.shape[1:] for s in srcs]
    chunks = [_piece_chunks(p, s.dtype.itemsize) for p, s in zip(pieces, srcs)]

    def body(*refs):
        src, out = refs[:n], refs[n:2 * n]
        load_sems, send_sems, recv_sems = refs[2 * n:]
        x, y, c = lax.axis_index("x"), lax.axis_index("y"), lax.axis_index("c")
        if group == "c":
            me = c
        elif group == "xy":
            me = 2 * x + y
        else:
            me = 4 * x + 2 * y + c

        def device(j):
            if group == "c":
                return (x, y, j)
            if group == "xy":
                return (j // 2, j % 2, c)
            return (j // 4, (j // 2) % 2, j % 2)

        def part(ref, idx):
            return ref.at[idx] if idx else ref

        def piece(a, j):
            return src[a] if bcast else src[a].at[j]

        for a in range(n):
            jobs = [(None, idx) for idx in chunks[a]] if bcast else [(d, idx) for idx in chunks[a] for d in range(1, size)]
            full_rows = chunks[a][0][-1].size if chunks[a][0] else None
            slot_shape = (full_rows, pieces[a][-1]) if chunks[a][0] else pieces[a]

            def stream(stage, a=a, jobs=jobs, full_rows=full_rows):
                def slot(q):
                    idx = jobs[q][1]
                    view = stage.at[q % N_STAGE]
                    if idx and idx[-1].size != full_rows:
                        view = view.at[pl.ds(0, idx[-1].size)]
                    return view

                def load(q):
                    d, idx = jobs[q]
                    j = me if d is None else jnp.bitwise_xor(me, d)
                    return pltpu.make_async_copy(part(piece(a, j), idx), slot(q), load_sems.at[q % N_STAGE])

                def sends(q):
                    d, idx = jobs[q]
                    return [pltpu.make_async_remote_copy(
                        src_ref=slot(q), dst_ref=part(out[a].at[me], idx),
                        send_sem=send_sems.at[(q % N_STAGE) * npeer + dd - 1], recv_sem=recv_sems.at[a * size + dd],
                        device_id=device(jnp.bitwise_xor(me, dd)), device_id_type=pl.DeviceIdType.MESH)
                        for dd in (range(1, size) if d is None else (d,))]

                nq = len(jobs)
                for q in range(nq + LOOKAHEAD):
                    if q < nq:
                        if q >= N_STAGE:
                            for cp in sends(q - N_STAGE):
                                cp.wait_send()
                        load(q).start()
                    if q >= LOOKAHEAD:
                        load(q - LOOKAHEAD).wait()
                        for cp in sends(q - LOOKAHEAD):
                            cp.start()
                for q in range(max(0, nq - N_STAGE), nq):
                    for cp in sends(q):
                        cp.wait_send()

            pl.run_scoped(stream, pltpu.VMEM((N_STAGE,) + tuple(slot_shape), srcs[a].dtype))

        for a in range(n):
            for d in range(1, size):
                j = jnp.bitwise_xor(me, d)
                pltpu.make_async_remote_copy(
                    src_ref=piece(a, j), dst_ref=out[a].at[j], send_sem=send_sems.at[0], recv_sem=recv_sems.at[a * size + d],
                    device_id=device(j), device_id_type=pl.DeviceIdType.MESH).wait_recv()

    any_spec = pl.BlockSpec(memory_space=pl.ANY)
    return pl.pallas_call(
        body, name=name, in_specs=[any_spec] * n, out_specs=[any_spec] * n,
        out_shape=[jax.ShapeDtypeStruct((size,) + tuple(p), s.dtype) for p, s in zip(pieces, srcs)],
        scratch_shapes=[pltpu.SemaphoreType.DMA((N_STAGE,)), pltpu.SemaphoreType.DMA((N_STAGE * npeer,)),
                        pltpu.SemaphoreType.DMA((n * size,))],
        compiler_params=pltpu.CompilerParams(vmem_limit_bytes=VMEM_LIMIT_BYTES),
    )(*srcs)


def _with_own(outs, owns, me):
    return [lax.dynamic_update_index_in_dim(o, w, me, 0) for o, w in zip(outs, owns)]


def _sum_slots(arr, name):
    k = arr.shape[0]
    rest = arr.shape[1:]
    width = rest[-1]
    rows_per = math.prod(rest[:-1])
    a2 = arr.reshape(k * rows_per, width)
    tile = rows_per
    for cand in (256, 128, 64, 32, 16, 8):
        if rows_per % cand == 0:
            tile = cand
            break
    nt = rows_per // tile

    def body(*refs):
        acc = refs[0][...]
        for r in refs[1:k]:
            acc = acc + r[...]
        refs[k][...] = acc

    in_specs = [pl.BlockSpec((tile, width), functools.partial(lambda i, s: (s * nt + i, 0), s=s)) for s in range(k)]
    out = pl.pallas_call(
        body, name=name, grid=(nt,), in_specs=in_specs, out_specs=pl.BlockSpec((tile, width), lambda i: (i, 0)),
        out_shape=jax.ShapeDtypeStruct((rows_per, width), arr.dtype), compiler_params=_params(1),
    )(*([a2] * k))
    return out.reshape(rest)


def _sum_pieces(recv, own, own_slotted, me, name):
    k = recv.shape[0]
    rest = recv.shape[1:]
    width = rest[-1]
    rows_per = math.prod(rest[:-1])
    r2 = recv.reshape(k * rows_per, width)
    o2 = own.reshape(-1, width)
    tile = rows_per
    for cand in (256, 128, 64, 32, 16, 8):
        if rows_per % cand == 0:
            tile = cand
            break
    nt = rows_per // tile

    def body(me_ref, *refs):
        mine = refs[k][...].astype(F32)
        acc = None
        for j in range(k):
            term = jnp.where(me_ref[0] == j, mine, refs[j][...].astype(F32))
            acc = term if acc is None else acc + term
        refs[k + 1][...] = acc
        refs[k + 2][...] = acc.astype(BF)

    in_specs = [pl.BlockSpec((tile, width), functools.partial(lambda i, m, s: (s * nt + i, 0), s=s)) for s in range(k)]
    in_specs.append(pl.BlockSpec((tile, width), (lambda i, m: (m[0] * nt + i, 0)) if own_slotted else (lambda i, m: (i, 0))))
    out_spec = pl.BlockSpec((tile, width), lambda i, m: (i, 0))
    out, out_bf = pl.pallas_call(
        body, name=name,
        grid_spec=pltpu.PrefetchScalarGridSpec(num_scalar_prefetch=1, grid=(nt,), in_specs=in_specs, out_specs=[out_spec, out_spec]),
        out_shape=[jax.ShapeDtypeStruct((rows_per, width), F32), jax.ShapeDtypeStruct((rows_per, width), BF)],
        compiler_params=_params(1),
    )(jnp.reshape(me, (1,)).astype(jnp.int32), *([r2] * k), o2)
    return out.reshape(rest), out_bf.reshape(rest)


def _adamw(w, g, m, v, name):
    def fn(w_, g_, m_, v_):
        m1 = ADAM_B1 * m_ + (1.0 - ADAM_B1) * g_
        v1 = ADAM_B2 * v_ + (1.0 - ADAM_B2) * (g_ * g_)
        m_hat = m1 / (1.0 - ADAM_B1 ** ADAM_STEP)
        v_hat = v1 / (1.0 - ADAM_B2 ** ADAM_STEP)
        delta = -ADAM_LR * (m_hat / (jnp.sqrt(v_hat) + ADAM_EPS) + ADAM_WD * w_)
        return (delta, m1, v1), ()

    rows, width = w.shape
    tile = rows
    for cand in (256, 128, 64, 32, 16, 8):
        if rows % cand == 0:
            tile = cand
            break
    o, _ = _rows(fn, [(w, 0, width), (g, 0, width), (m, 0, width), (v, 0, width)], [],
                 [(width, F32)] * 3, [], tile, name)
    return o


def _col_pieces(start, width, shard):
    out = []
    while width:
        chip, off = divmod(start, shard)
        take = min(width, shard - off)
        out.append((chip, off, take))
        start, width = start + take, width - take
    return out


def _gather_cols(parts, ranges, pad=0):
    shard = parts.shape[3]
    halves = []
    for h in range(2):
        cols = [parts[h, chip, :, off:off + w] for s, wd in ranges for chip, off, w in _col_pieces(s, wd, shard)]
        if pad:
            cols.append(jnp.zeros((parts.shape[2], pad), parts.dtype))
        halves.append(jnp.concatenate(cols, axis=1))
    return jnp.concatenate(halves, axis=0)


def _scatter_cols(g, ranges, shard):
    starts, pos = {}, 0
    for s, wd in ranges:
        starts[s] = (pos, wd)
        pos += wd
    order = sorted(starts)
    hr = g.shape[0] // 2
    out = []
    for h in range(2):
        per_chip = []
        for chip in range(4):
            cols = []
            for s in order:
                p0, wd = starts[s]
                lo, hi = max(s, chip * shard), min(s + wd, (chip + 1) * shard)
                if lo < hi:
                    cols.append(g[h * hr:(h + 1) * hr, p0 + lo - s:p0 + hi - s])
            per_chip.append(jnp.concatenate(cols, axis=1))
        out.append(jnp.stack(per_chip))
    return jnp.stack(out)


_IN_RANGES = tuple((o, wd) for _, o, wd in _PACK)


def _double_heads(t):
    length = t.shape[0]
    h = jnp.transpose(t.reshape(length, 2, 64), (1, 0, 2))
    return jnp.concatenate([h, h], axis=-1)


def _fold_heads(d_cur, d_prev):
    length = d_cur.shape[1]
    d = d_cur + jnp.concatenate([d_prev[:, BLK:], jnp.zeros((2, BLK, BLK), F32)], axis=1)
    d = d[..., :64] + d[..., 64:]
    return jnp.transpose(d, (1, 0, 2)).reshape(length, 128)


def _ln_fwd(a, r, g, b, name):
    def fn(a_, r_, g_, b_):
        y = _layer_norm(ALPHA * a_ + r_, g_, b_)
        return (y, y), ()
    o, _ = _rows(fn, [(a, 0, D_MODEL), (r, 0, D_MODEL)], [g, b], [(D_MODEL, F32), (D_MODEL, BF)], [], ROW_TILE, name)
    return o


def _ln_bwd(a, r, d1, d2, g, b, name):
    def fn(a_, r_, d1_, d2_, g_, b_):
        _, vjp = jax.vjp(_layer_norm, ALPHA * a_ + r_, g_, b_)
        ds, dg, db = vjp(ALPHA * d1_ + d2_)
        return (ds, ds), (dg, db)
    o, acc = _rows(fn, [(a, 0, D_MODEL), (r, 0, D_MODEL), (d1, 0, D_MODEL), (d2, 0, D_MODEL)], [g, b],
                   [(D_MODEL, F32), (D_MODEL, BF)], [(1, D_MODEL), (1, D_MODEL)], ROW_TILE, name)
    return o[0], o[1], acc[0], acc[1]


def _layer_fwd(l, h, hb, w, p, hosted=None, hosted_attn=None, hosted_conv=None, late_weights=None):
    sv = {"h": h, "hb": hb}
    proj = _mm(hb, w["win"], "nn", F32, 4096, 512, 1024, f"proj{l}")
    sv["proj"] = proj
    pre = _conv_fwd(proj, p["conv_w"], p["conv_b"], f"conv{l}", hosted_conv)
    if hosted_conv:
        pre, got = pre
        w.update(late_weights(got))
    sv["pre"] = pre
    dt_t = jnp.transpose(proj[:, _OFF["dt"]:_OFF["dt"] + SSD_HEADS])
    sv["dt_t"] = dt_t
    res = _ssd_fwd(pre, proj, dt_t, p["dt_bias"], p["a_log"], p["d_skip_c"], p["ssd_norm_w"], f"ssd{l}", hosted)
    yn, s_in = res[:2]
    if hosted:
        sv["hosted"] = res[2]
    sv["yn"], sv["s_in"] = yn, s_in
    ya = _mm(yn, w["wso"], "nn", BF, 1024, 1024, 2048, f"ssdout{l}")
    k2 = _double_heads(proj[:, _OFF["k"]:_OFF["k"] + 128])
    v2 = _double_heads(proj[:, _OFF["v"]:_OFF["v"] + 128])
    sv["k2"], sv["v2"] = k2, v2
    att = _attn_fwd(proj, k2, v2, p["att_sinks"], f"attn{l}", hosted_attn)
    if hosted_attn:
        att, sv["hosted_attn"] = att
    sv["att"] = att
    yb = _mm(att, w["wao"], "nn", BF, 1024, 1024, 1024, f"attout{l}")
    sv["ya"], sv["yb"] = ya, yb
    gcb = _OFF["gates"] // 1024

    def gate_fn(ga, gb, ya_, yb_):
        return (_sigmoid(ga) * ya_.astype(F32) + _sigmoid(gb) * yb_.astype(F32),), ()
    (ub,), _ = _rows(gate_fn, [(proj, gcb, 1024), (proj, gcb + 1, 1024), (ya, 0, 1024), (yb, 0, 1024)], [],
                     [(D_MODEL, BF)], [], ROW_TILE, f"gate{l}")
    sv["ub"] = ub
    mix = _mm(ub, w["wmo"], "nn", F32, 1024, 1024, 1024, f"mixout{l}")
    sv["mix"] = mix
    h1, h1b = _ln_fwd(h, mix, p["ln_mix_g"], p["ln_mix_b"], f"lnmix{l}")
    sv["h1"], sv["h1b"] = h1, h1b
    gu = _mm(h1b, w["wgu"], "nn", BF, 4096, 512, 1024, f"ffnin{l}")
    sv["gu"] = gu

    def act_fn(g_, u_):
        return (_silu(g_.astype(F32)) * u_.astype(F32),), ()
    (act,), _ = _rows(act_fn, [(gu, 0, FFN_HIDDEN), (gu, 1, FFN_HIDDEN)], [], [(FFN_HIDDEN, BF)], [], ROW_TILE, f"swiglu{l}")
    sv["act"] = act
    ffn = _mm(act, w["wd"], "nn", F32, 1024, 1024, FFN_HIDDEN, f"ffnout{l}")
    sv["ffn"] = ffn
    return sv


def _layer_bwd(l, sv, w, p, ds_f, ds_b, hosted=(), early_hook=None, late_hook=None):
    g = {}
    g["w_ffn_down"] = _mm(sv["act"], ds_b, "tn", F32, 1408, 1024, 2048, f"dwd{l}")
    dact = _mm(ds_b, w["wd"], "nt", BF, 1024, FFN_HIDDEN, 1024, f"dact{l}")

    def act_bwd(g_, u_, d_):
        _, vjp = jax.vjp(lambda a, b: _silu(a) * b, g_.astype(F32), u_.astype(F32))
        dg_, du_ = vjp(d_.astype(F32))
        return (jnp.concatenate([dg_, du_], axis=1),), ()
    (dgu,), _ = _rows(act_bwd, [(sv["gu"], 0, FFN_HIDDEN), (sv["gu"], 1, FFN_HIDDEN), (dact, 0, FFN_HIDDEN)], [],
                      [(2 * FFN_HIDDEN, BF)], [], ROW_TILE // 2, f"dswiglu{l}")
    dwgu = _mm(sv["h1b"], dgu, "tn", F32, 1024, 1408, 2048, f"dwgu{l}")
    g["w_ffn_gate"], g["w_ffn_up"] = dwgu[:, :FFN_HIDDEN], dwgu[:, FFN_HIDDEN:]
    dh1 = _mm(dgu, w["wgu"], "nt", F32, 1024, 1024, 2816, f"dh1{l}")
    ds2_f, ds2_b, g["ln_mix_g"], g["ln_mix_b"] = _ln_bwd(sv["h"], sv["mix"], ds_f, dh1, p["ln_mix_g"], p["ln_mix_b"], f"dlnmix{l}")
    g["w_mix_out"] = _mm(sv["ub"], ds2_b, "tn", F32, 1024, 1024, 2048, f"dwmo{l}")
    du = _mm(ds2_b, w["wmo"], "nt", BF, 1024, 1024, 1024, f"du{l}")
    proj = sv["proj"]
    gcb = _OFF["gates"] // 1024

    def gate_bwd(ga, gb, ya_, yb_, du_):
        _, vjp = jax.vjp(lambda a, b, c, d: _sigmoid(a) * c + _sigmoid(b) * d, ga, gb, ya_.astype(F32), yb_.astype(F32))
        dga, dgb, dya, dyb = vjp(du_.astype(F32))
        return (jnp.concatenate([dga, dgb], axis=1), dya, dyb), ()
    (dproj, dya, dyb), _ = _rows(
        gate_bwd, [(proj, gcb, 1024), (proj, gcb + 1, 1024), (sv["ya"], 0, 1024), (sv["yb"], 0, 1024), (du, 0, 1024)], [],
        [(2048, BF, NP), (D_MODEL, BF), (D_MODEL, BF)], [], ROW_TILE, f"dgate{l}")
    g["w_att_out"] = _mm(sv["att"], dyb, "tn", F32, 1024, 1024, 2048, f"dwao{l}")
    datt = _mm(dyb, w["wao"], "nt", BF, 1024, 1024, 1024, f"datt{l}")
    g["w_ssd_out"] = _mm(sv["yn"], dya, "tn", F32, 1024, 1024, 2048, f"dwso{l}")
    dyn = _mm(dya, w["wso"], "nt", F32, 1024, 1024, 1024, f"dyn{l}")
    hosted = list(hosted) + (list(early_hook(g)) if early_hook else [])
    dproj, dkp, dkc, dvp, dvc, g["att_sinks"] = _attn_bwd(proj, sv["k2"], sv["v2"], p["att_sinks"], datt, dproj, f"dattn{l}")
    res = _ssd_bwd(sv["pre"], proj, sv["dt_t"], p["dt_bias"], p["a_log"], p["d_skip_c"], p["ssd_norm_w"], sv["s_in"], dyn,
                   dproj, f"dssd{l}", (hosted, False) if hosted else None)
    (dpx, dpb, dpc, dproj, ddt_t, g["dt_bias"], g["a_log"], ddsk, g["ssd_norm_w"]) = res[:9]
    if hosted:
        g["hosted"] = res[9]
    g["d_skip"] = jnp.sum(ddsk.reshape(SSD_HEADS, 64), axis=1)
    dws, dbs = [], []
    for nm, dpart, part in (("x", dpx, 0), ("b", dpb, 4), ("c", dpc, 5)):
        dproj, dw_, db_ = _conv_bwd(dpart, proj, p["conv_w"], dproj, part, f"dconv{nm}{l}")
        dws.append(dw_)
        dbs.append(db_)
    g["conv_w"], g["conv_b"] = jnp.concatenate(dws, axis=1), jnp.concatenate(dbs, axis=1)
    length = proj.shape[0]
    tail = jnp.concatenate([_fold_heads(dkc, dkp).astype(BF), _fold_heads(dvc, dvp).astype(BF),
                            jnp.transpose(ddt_t).astype(BF), jnp.zeros((length, _PAD), BF)], axis=1)
    dproj = lax.dynamic_update_slice(dproj, tail, (0, _OFF["k"]))
    g["w_in"] = _mm(sv["hb"], dproj, "tn", F32, 1024, 2176, 1024, f"dwin{l}")
    late = list(late_hook(g)) if late_hook else []
    dh = _mm(dproj, w["win"], "nt", F32, 1024, 1024, 2176, f"dh{l}", (late, False) if late else None)
    if late:
        dh, g["late_hosted"] = dh
    return g, ds2_f, dh


def kernel(x, ln_in_g, ln_in_b, w_in, conv_w, conv_b, dt_bias, a_log, d_skip, ssd_norm_w, att_sinks, w_ssd_out, w_att_out, w_mix_out, ln_mix_g, ln_mix_b, w_ffn_gate, w_ffn_up, w_ffn_down, ln_ffn_g, ln_ffn_b, loss_target, m_ln_in_g, m_ln_in_b, m_w_in, m_conv_w, m_conv_b, m_dt_bias, m_a_log, m_d_skip, m_ssd_norm_w, m_att_sinks, m_w_ssd_out, m_w_att_out, m_w_mix_out, m_ln_mix_g, m_ln_mix_b, m_w_ffn_gate, m_w_ffn_up, m_w_ffn_down, m_ln_ffn_g, m_ln_ffn_b, v_ln_in_g, v_ln_in_b, v_w_in, v_conv_w, v_conv_b, v_dt_bias, v_a_log, v_d_skip, v_ssd_norm_w, v_att_sinks, v_w_ssd_out, v_w_att_out, v_w_mix_out, v_ln_mix_g, v_ln_mix_b, v_w_ffn_gate, v_w_ffn_up, v_w_ffn_down, v_ln_ffn_g, v_ln_ffn_b):
    env = dict(locals())
    wts = {n: env[n] for n in WEIGHTS}
    mom1 = {n: env["m_" + n] for n in WEIGHTS}
    mom2 = {n: env["v_" + n] for n in WEIGHTS}
    ci = lax.axis_index("c")
    chip = 2 * lax.axis_index("x") + lax.axis_index("y")
    xs_ = x[0]
    tgt = loss_target[0]

    COLS = ("w_in", "w_ffn_gate", "w_ffn_up")

    def half(t):
        return lax.dynamic_slice_in_dim(t, ci * (t.shape[0] // 2), t.shape[0] // 2, axis=0)

    def assemble(name, parts):
        if name == "w_in":
            return _gather_cols(parts, _IN_RANGES, _PAD)
        if name in COLS:
            return _gather_cols(parts, ((0, 4 * parts.shape[3]),))
        return jnp.concatenate([parts[h, s] for s in range(4) for h in range(2)], axis=0)

    def split(name, gfull):
        if name == "w_in":
            return _scatter_cols(gfull, _IN_RANGES, IN_DIM // 4)
        if name in COLS:
            return _scatter_cols(gfull, ((0, gfull.shape[1]),), gfull.shape[1] // 4)
        hr = gfull.shape[0] // 8
        return jnp.stack([jnp.stack([gfull[(2 * s + h) * hr:(2 * s + h + 1) * hr] for s in range(4)]) for h in range(2)])

    mine = [[half(wts[n][l].astype(BF)) for n in BIG] for l in range(DEPTH)]
    conv_all, = _with_own(_exchange([conv_w], "xy", True, "gather_conv"), [conv_w], chip)
    conv_full = jnp.transpose(conv_all, (1, 2, 0, 3)).reshape(DEPTH, SSD_CONV, CONV_DIM)

    KEYS = {"w_in": "win", "w_ssd_out": "wso", "w_att_out": "wao", "w_mix_out": "wmo", "w_ffn_down": "wd"}

    def layer_weights(l, names, by_chip):
        own = [mine[l][BIG.index(n)] for n in names]
        by_chip = _with_own(by_chip, own, chip)
        parts = _with_own(_exchange(by_chip, "c", True, f"gather_cores{l}_{names[0]}"), by_chip, ci)
        fw = {n: assemble(n, part) for n, part in zip(names, parts)}
        w = {KEYS[n]: fw[n] for n in names if n in KEYS}
        if "w_ffn_gate" in fw:
            w["wgu"] = jnp.concatenate([fw["w_ffn_gate"], fw["w_ffn_up"]], axis=1)
        return w

    layer_p = []
    for l in range(DEPTH):
        layer_p.append({
            "conv_w": conv_full[l], "conv_b": conv_b[l][None],
            "dt_bias": dt_bias[l][:, None], "a_log": a_log[l][:, None],
            "d_skip_c": jnp.repeat(d_skip[l], 64)[None], "ssd_norm_w": ssd_norm_w[l][None],
            "att_sinks": att_sinks[l][None], "ln_mix_g": ln_mix_g[l][None], "ln_mix_b": ln_mix_b[l][None],
            "ln_ffn_g": ln_ffn_g[l][None], "ln_ffn_b": ln_ffn_b[l][None],
        })

    def ln_in_fn(x_, g_, b_):
        y = _layer_norm(x_, g_, b_)
        return (y, y), ()
    (h, hb), _ = _rows(ln_in_fn, [(xs_, 0, D_MODEL)], [ln_in_g[None], ln_in_b[None]], [(D_MODEL, F32), (D_MODEL, BF)], [], ROW_TILE, "ln_in")
    rest = tuple(n for n in BIG if n != "w_in")
    layer_w = [layer_weights(0, ("w_in",), _exchange([mine[0][BIG.index("w_in")]], "xy", True, "gather_chips0"))]
    behind_ssd = [i for i, n in enumerate(BIG) if n in COLS]
    behind_attn = [i for i, n in enumerate(BIG) if n not in COLS]
    saved = []
    for l in range(DEPTH):
        nxt = l + 1 < DEPTH
        sv = _layer_fwd(l, h, hb, layer_w[l], layer_p[l], ([mine[l + 1][i] for i in behind_ssd], True) if nxt else None,
                        ([mine[l + 1][i] for i in behind_attn], True) if nxt else None,
                        ([mine[l][BIG.index(n)] for n in rest], True) if l == 0 else None,
                        functools.partial(layer_weights, l, rest))
        saved.append(sv)
        if nxt:
            got = dict(zip(behind_ssd + behind_attn, sv["hosted"] + sv["hosted_attn"]))
            layer_w.append(layer_weights(l + 1, BIG, [got[i] for i in range(len(BIG))]))
            h, hb = _ln_fwd(sv["h1"], sv["ffn"], layer_p[l]["ln_ffn_g"], layer_p[l]["ln_ffn_b"], f"lnffn{l}")

    def loss_fn(a_, r_, t_, g_, b_):
        y, vjp = jax.vjp(_layer_norm, ALPHA * a_ + r_, g_, b_)
        err = y - t_
        ds, dg, db = vjp(err * (1.0 / D_MODEL))
        part = 0.5 * jnp.sum(jnp.mean(err * err, axis=-1, keepdims=True), axis=0, keepdims=True)
        return (ds, ds), (dg, db, jnp.broadcast_to(part, (1, BLK)))
    sv = saved[-1]
    (ds_f, ds_b), (dg_last, db_last, loss_part) = _rows(
        loss_fn, [(sv["h1"], 0, D_MODEL), (sv["ffn"], 0, D_MODEL), (tgt, 0, D_MODEL)],
        [layer_p[-1]["ln_ffn_g"], layer_p[-1]["ln_ffn_b"]], [(D_MODEL, F32), (D_MODEL, BF)],
        [(1, D_MODEL), (1, D_MODEL), (1, BLK)], ROW_TILE, "loss")
    loss = lax.psum(loss_part[0, 0], ("x", "y", "c"))

    def reduce_cores(l, g, names):
        src = [split(n, g[n]) for n in names]
        recv = _exchange([s.astype(BF) for s in src], "c", False, f"reduce_cores{l}_{names[0]}")
        return {n: _sum_pieces(r, o, True, ci, f"sum_cores{l}_{n}") for n, r, o in zip(names, recv, src)}

    def bf16_of(sums, names):
        return [sums[n][1] for n in names]

    def add_chips(l, sums, from_chips):
        return [_sum_pieces(from_chips[n], sums[n][0], True, chip, f"sum_chips{l}_{n}")[0] for n in BIG]

    EARLY = tuple(n for n in BIG if n != "w_in")
    grads = [None] * DEPTH
    chip_sums = [None] * DEPTH
    for l in reversed(range(DEPTH)):
        sv = saved[l]
        if l == DEPTH - 1:
            lg, lb = dg_last, db_last
        else:
            ds_f, ds_b, lg, lb = _ln_bwd(sv["h1"], sv["ffn"], d1, d2, layer_p[l]["ln_ffn_g"], layer_p[l]["ln_ffn_b"], f"dlnffn{l}")
        sums = {}

        def early_hook(g, l=l, sums=sums):
            sums.update(reduce_cores(l, g, EARLY))
            return bf16_of(sums, EARLY)

        def late_hook(g, l=l, sums=sums):
            sums.update(reduce_cores(l, g, ("w_in",)))
            return bf16_of(sums, ("w_in",))
        if l == 0:
            g, d1, d2 = _layer_bwd(l, sv, layer_w[l], layer_p[l], ds_f, ds_b, bf16_of(chip_sums[l + 1], BIG), early_hook, late_hook)
        else:
            g, d1, d2 = _layer_bwd(l, sv, layer_w[l], layer_p[l], ds_f, ds_b)
            sums.update(reduce_cores(l, g, BIG))
        g["ln_ffn_g"], g["ln_ffn_b"] = lg, lb
        grads[l] = g
        chip_sums[l] = sums

    def ln_in_bwd(x_, d1_, d2_, g_, b_):
        _, vjp = jax.vjp(_layer_norm, x_, g_, b_)
        dx, dg, db = vjp(ALPHA * d1_ + d2_)
        return (dx,), (dg, db)
    (grad_x,), (g_ln_in_g, g_ln_in_b) = _rows(
        ln_in_bwd, [(xs_, 0, D_MODEL), (d1, 0, D_MODEL), (d2, 0, D_MODEL)], [ln_in_g[None], ln_in_b[None]],
        [(D_MODEL, F32)], [(1, D_MODEL), (1, D_MODEL)], ROW_TILE, "dln_in")

    carried = grads[0]["hosted"]
    from_chips = dict(zip(EARLY, carried[len(BIG):]))
    from_chips["w_in"], = grads[0]["late_hosted"]
    reduced = add_chips(0, chip_sums[0], from_chips) + add_chips(1, chip_sums[1], dict(zip(BIG, carried[:len(BIG)])))
    halves = _with_own(_exchange(reduced, "c", True, "swap_halves"), reduced, ci)
    shards = [t.reshape(2 * t.shape[1], t.shape[2]) for t in halves]
    big_grad = {n: jnp.stack([shards[l * len(BIG) + i] for l in range(DEPTH)]).reshape(wts[n].shape) for i, n in enumerate(BIG)}

    small_g = {"ln_in_g": g_ln_in_g[0], "ln_in_b": g_ln_in_b[0]}
    for n in SMALL[2:]:
        small_g[n] = jnp.stack([grads[l][n].reshape(wts[n].shape[1:] if n != "conv_w" else (SSD_CONV, CONV_DIM)) for l in range(DEPTH)])
    flat = jnp.concatenate([small_g[n].reshape(-1) for n in SMALL])
    n_small = flat.shape[0]
    width = -(-n_small // 1024) * 1024
    flat = jnp.pad(flat, (0, width - n_small)).reshape(8, width // 8)
    gathered, = _with_own(_exchange([flat], "all", True, "gather_small"), [flat], 2 * chip + ci)
    total = _sum_slots(gathered, "sum_small").reshape(-1)
    small_grad, off = {}, 0
    for n in SMALL:
        shp = small_g[n].shape
        cnt = math.prod(shp)
        small_grad[n] = total[off:off + cnt].reshape(shp)
        off += cnt
    small_grad["conv_w"] = lax.dynamic_slice_in_dim(small_grad["conv_w"], chip * (CONV_DIM // 4), CONV_DIM // 4, axis=2)

    out_g, out_d, out_m, out_v = {}, {}, {}, {}
    for n in BIG:
        shp = wts[n].shape
        two_d = lambda t: t.reshape(shp[0] * shp[1], shp[2])
        d_, m_, v_ = _adamw(two_d(wts[n]), two_d(big_grad[n]), two_d(mom1[n]), two_d(mom2[n]), f"adamw_{n}")
        out_g[n], out_d[n], out_m[n], out_v[n] = big_grad[n], d_.reshape(shp), m_.reshape(shp), v_.reshape(shp)

    def flat_small(d):
        f = jnp.concatenate([d[n].reshape(-1) for n in SMALL])
        return jnp.pad(f, (0, swidth - f.shape[0])).reshape(8, swidth // 8)
    n_sw = sum(math.prod(wts[n].shape) for n in SMALL)
    swidth = -(-n_sw // 1024) * 1024
    d_, m_, v_ = _adamw(flat_small(wts), flat_small(small_grad), flat_small(mom1), flat_small(mom2), "adamw_small")
    off = 0
    for n in SMALL:
        shp = wts[n].shape
        cnt = math.prod(shp)
        out_g[n] = small_grad[n]
        out_d[n] = d_.reshape(-1)[off:off + cnt].reshape(shp)
        out_m[n] = m_.reshape(-1)[off:off + cnt].reshape(shp)
        out_v[n] = v_.reshape(-1)[off:off + cnt].reshape(shp)
        off += cnt

    return (loss, grad_x[None], *[out_g[n] for n in WEIGHTS], *[out_d[n] for n in WEIGHTS],
            *[out_m[n] for n in WEIGHTS], *[out_v[n] for n in WEIGHTS])
```

```python
import functools
import itertools
import math

import jax
import jax.numpy as jnp
from jax import lax
from jax.experimental import pallas as pl
from jax.experimental.pallas import tpu as pltpu

F32 = jnp.float32
BF = jnp.bfloat16

D_MODEL = 1024
DEPTH = 2
ATT_HEADS = 16
ATT_HEAD_DIM = 64
BLK = 128
SSD_D_INNER = 2048
SSD_HEADS = 32
SSD_GROUPS = 4
SSD_STATE = 128
SSD_CONV = 4
BC_DIM = 512
CONV_DIM = 3072
FFN_HIDDEN = 2816
IN_DIM = 8480
LN_EPS = 1e-5
RMS_EPS = 1e-5
ALPHA = (2 * DEPTH) ** 0.25
ADAM_LR = 0.001
ADAM_B1 = 0.9
ADAM_B2 = 0.999
ADAM_EPS = 1e-08
ADAM_WD = 0.01
ADAM_STEP = 10

_PACK = (("gates", 6432, 2048), ("z", 1280, 2048), ("q", 0, 1024), ("xs", 3328, 2048), ("B", 5376, 512),
         ("C", 5888, 512), ("k", 1024, 128), ("v", 1152, 128), ("dt", 6400, 32))
NP = 8704
_OFF = {}
_o = 0
for _n, _, _w in _PACK:
    _OFF[_n] = _o
    _o += _w
_PAD = NP - _o

VMEM_LIMIT_BYTES = 56 * 1024 * 1024
ROW_TILE = 512
BIG = ("w_in", "w_ssd_out", "w_att_out", "w_mix_out", "w_ffn_gate", "w_ffn_up", "w_ffn_down")
WEIGHTS = ("ln_in_g", "ln_in_b", "w_in", "conv_w", "conv_b", "dt_bias", "a_log", "d_skip", "ssd_norm_w", "att_sinks",
           "w_ssd_out", "w_att_out", "w_mix_out", "ln_mix_g", "ln_mix_b", "w_ffn_gate", "w_ffn_up", "w_ffn_down",
           "ln_ffn_g", "ln_ffn_b")
SMALL = tuple(n for n in WEIGHTS if n not in BIG)


def _params(n_grid):
    return pltpu.CompilerParams(dimension_semantics=("arbitrary",) * n_grid, vmem_limit_bytes=VMEM_LIMIT_BYTES)


def _dot(a, b, ca, cb):
    return lax.dot_general(a.astype(BF), b.astype(BF), (((ca,), (cb,)), ((), ())), preferred_element_type=F32)


@jax.custom_vjp
def _nn(a, b):
    return _dot(a, b, 1, 0)


def _nn_f(a, b):
    return _dot(a, b, 1, 0), (a.astype(BF), b.astype(BF))


def _nn_b(res, g):
    a, b = res
    return _dot(g, b, 1, 1), _dot(a, g, 0, 0)


_nn.defvjp(_nn_f, _nn_b)


@jax.custom_vjp
def _nt(a, b):
    return _dot(a, b, 1, 1)


def _nt_f(a, b):
    return _dot(a, b, 1, 1), (a.astype(BF), b.astype(BF))


def _nt_b(res, g):
    a, b = res
    return _dot(g, b, 1, 0), _dot(g, a, 0, 0)


_nt.defvjp(_nt_f, _nt_b)


@jax.custom_vjp
def _tn(a, b):
    return _dot(a, b, 0, 0)


def _tn_f(a, b):
    return _dot(a, b, 0, 0), (a.astype(BF), b.astype(BF))


def _tn_b(res, g):
    a, b = res
    return _dot(b, g, 1, 1), _dot(a, g, 1, 0)


_tn.defvjp(_tn_f, _tn_b)


def _sigmoid(x):
    return 0.5 * jnp.tanh(0.5 * x) + 0.5


def _silu(x):
    return x * _sigmoid(x)


def _layer_norm(s, g, b):
    mu = jnp.mean(s, axis=-1, keepdims=True)
    sc = s - mu
    var = jnp.mean(sc * sc, axis=-1, keepdims=True)
    return sc * lax.rsqrt(var + LN_EPS) * g + b


def _ssd_group(pre_x, pre_b, pre_c, z, dtr, dtb, alog, dsk, nw, state):
    t = pre_x.shape[0]
    xs, bm, cm = _silu(pre_x), _silu(pre_b), _silu(pre_c)
    dt = jax.nn.softplus(dtr + dtb)
    da = dt * (-jnp.exp(alog))
    row = lax.broadcasted_iota(jnp.int32, (t, t), 0)
    col = lax.broadcasted_iota(jnp.int32, (t, t), 1)
    upper = (row <= col).astype(F32)
    cum = jnp.dot(da, upper, precision=lax.Precision.HIGHEST, preferred_element_type=F32)
    tot = jnp.sum(da, axis=1, keepdims=True)
    cb = _nt(cm, bm)
    tril = row >= col
    first = lax.broadcasted_iota(jnp.int32, (t, BLK), 1) < 64
    first_row = lax.broadcasted_iota(jnp.int32, (1, BLK), 1) < 64

    def col_form(r):
        return jnp.broadcast_to(r, (t, t)).T

    ys, new_state = [], []
    for p in range(4):
        h0, h1 = 2 * p, 2 * p + 1
        sl = slice(BLK * p, BLK * (p + 1))
        x_pair = xs[:, sl] * jnp.where(first, col_form(dt[h0:h0 + 1]), col_form(dt[h1:h1 + 1]))
        y_pair = None
        cc = []
        for h, keep in ((h0, first), (h1, jnp.logical_not(first))):
            cr = jnp.broadcast_to(cum[h:h + 1], (t, t))
            cc.append(cr.T)
            decay = jnp.exp(jnp.where(tril, cc[-1] - cr, -1e30))
            y_h = _nn(decay * cb, jnp.where(keep, x_pair, 0.0))
            y_pair = y_h if y_pair is None else y_pair + y_h
        s_pair = state[:, sl]
        y_pair = y_pair + _nn(cm, s_pair) * jnp.where(first, jnp.exp(cc[0]), jnp.exp(cc[1]))
        to_end = jnp.where(first, jnp.exp(tot[h0:h0 + 1] - cc[0]), jnp.exp(tot[h1:h1 + 1] - cc[1]))
        chunk_decay = jnp.where(first_row, jnp.exp(tot[h0:h0 + 1]), jnp.exp(tot[h1:h1 + 1]))
        new_state.append(s_pair * chunk_decay + _tn(bm, x_pair * to_end))
        ys.append(y_pair + dsk[:, sl] * xs[:, sl])
    y = jnp.concatenate(ys, axis=1) * _silu(z)
    y = y * lax.rsqrt(jnp.mean(y * y, axis=-1, keepdims=True) + RMS_EPS) * nw
    return y, jnp.concatenate(new_state, axis=1)


def _attn_block(q, kp, kc, vp, vc, sinks, n, kv):
    t = q.shape[0]
    kb = jnp.concatenate([kp, kc], axis=0)
    vb = jnp.concatenate([vp, vc], axis=0)
    qi = lax.broadcasted_iota(jnp.int32, (t, 2 * t), 0)
    kj = lax.broadcasted_iota(jnp.int32, (t, 2 * t), 1)
    rel = qi + t - kj
    valid = (rel >= 0) & (rel < t) & ((n * t - t + kj) >= 0)
    relf = rel.astype(F32)
    first = lax.broadcasted_iota(jnp.int32, (t, BLK), 1) < 64
    lane16 = lax.broadcasted_iota(jnp.int32, (1, ATT_HEADS), 1)
    outs = []
    for p in range(4):
        qp = q[:, BLK * p:BLK * (p + 1)] * (ATT_HEAD_DIM ** -0.5)
        o_pair = None
        for half, keep in enumerate((first, jnp.logical_not(first))):
            h = kv * 8 + 2 * p + half
            s = _nt(jnp.where(keep, qp, 0.0), kb)
            slope = jnp.exp((-8.0 * math.log(2.0) / ATT_HEADS) * jnp.asarray(h + 1, F32))
            s = jnp.where(valid, s - slope * relf, -1e30)
            sink = jnp.sum(jnp.where(lane16 == h, sinks, 0.0), axis=1, keepdims=True)
            m = lax.stop_gradient(jnp.maximum(jnp.max(s, axis=1, keepdims=True), sink))
            e = jnp.exp(s - m)
            den = jnp.sum(e, axis=1, keepdims=True) + jnp.exp(sink - m)
            o = _nn(e * (1.0 / den), vb)
            o_pair = o if o_pair is None else jnp.where(first, o_pair, o)
        outs.append(o_pair)
    return jnp.concatenate(outs, axis=1)


def _chip_exchange(src, out, send_sems, recv_sems, bcast):
    x, y, c = lax.axis_index("x"), lax.axis_index("y"), lax.axis_index("c")
    me = 2 * x + y

    def copy(a, d, slot):
        j = jnp.bitwise_xor(me, d)
        return pltpu.make_async_remote_copy(
            src_ref=src[a] if bcast else src[a].at[j], dst_ref=out[a].at[slot],
            send_sem=send_sems.at[a * 4 + d], recv_sem=recv_sems.at[a * 4 + d],
            device_id=(j // 2, j % 2, c), device_id_type=pl.DeviceIdType.MESH)

    pairs = [(a, d) for a in range(len(src)) for d in range(1, 4)]

    def start():
        for a, d in pairs:
            copy(a, d, me).start()

    def wait():
        for a, d in pairs:
            copy(a, d, jnp.bitwise_xor(me, d)).wait_recv()
        for a, d in pairs:
            copy(a, d, me).wait_send()

    return start, wait


def _mm(a, b, mode, out_dtype, tm, tn, tk, name, hosted=None):
    if mode == "nn":
        (m, k), (k2, n) = a.shape, b.shape
    elif mode == "nt":
        (m, k), (n, k2) = a.shape, b.shape
    else:
        (k, m), (k2, n) = a.shape, b.shape
    assert k == k2, (a.shape, b.shape, mode)
    tm, tn, tk = min(tm, m), min(tn, n), min(tk, k)
    assert m % tm == 0 and n % tn == 0 and k % tk == 0, (m, n, k, tm, tn, tk)
    nk = k // tk
    grid = (m // tm, n // tn, nk)
    ca, cb = {"nn": (1, 0), "nt": (1, 1), "tn": (0, 0)}[mode]
    h_arrays, h_bcast = hosted if hosted else ((), False)
    nh = len(h_arrays)

    def body(*refs):
        a_ref, b_ref = refs[:2]
        o_ref = refs[2 + nh]
        scratch = refs[3 + 2 * nh:]
        step = [pl.program_id(ax) for ax in range(3)]
        if nh:
            start, wait = _chip_exchange(refs[2:2 + nh], refs[3 + nh:3 + 2 * nh], scratch[-2], scratch[-1], h_bcast)

            @pl.when((step[0] == 0) & (step[1] == 0) & (step[2] == 0))
            def _():
                start()

        part = _dot(a_ref[...], b_ref[...], ca, cb)
        if nk == 1:
            o_ref[...] = part.astype(o_ref.dtype)
        else:
            acc_ref = scratch[0]

            @pl.when(step[2] == 0)
            def _():
                acc_ref[...] = part

            @pl.when(step[2] > 0)
            def _():
                acc_ref[...] += part

            @pl.when(step[2] == nk - 1)
            def _():
                o_ref[...] = acc_ref[...].astype(o_ref.dtype)

        if nh:
            @pl.when((step[0] == grid[0] - 1) & (step[1] == grid[1] - 1) & (step[2] == nk - 1))
            def _():
                wait()

    a_spec = pl.BlockSpec((tk, tm), lambda i, j, kk: (kk, i)) if mode == "tn" else pl.BlockSpec((tm, tk), lambda i, j, kk: (i, kk))
    b_spec = pl.BlockSpec((tn, tk), lambda i, j, kk: (j, kk)) if mode == "nt" else pl.BlockSpec((tk, tn), lambda i, j, kk: (kk, j))
    any_spec = pl.BlockSpec(memory_space=pl.ANY)
    res = pl.pallas_call(
        body, name=name, grid=grid,
        in_specs=[a_spec, b_spec] + [any_spec] * nh,
        out_specs=[pl.BlockSpec((tm, tn), lambda i, j, kk: (i, j))] + [any_spec] * nh,
        out_shape=[jax.ShapeDtypeStruct((m, n), out_dtype)]
        + [jax.ShapeDtypeStruct(((4,) + s.shape) if h_bcast else s.shape, s.dtype) for s in h_arrays],
        scratch_shapes=([] if nk == 1 else [pltpu.VMEM((tm, tn), F32)])
        + ([pltpu.SemaphoreType.DMA((4 * nh,)), pltpu.SemaphoreType.DMA((4 * nh,))] if nh else []),
        compiler_params=_params(3),
    )(a, b, *h_arrays)
    return (res[0], list(res[1:])) if nh else res[0]


def _rows(fn, rows, params, outs, accs, tile, name):
    length = rows[0][0].shape[0]
    tile = min(tile, length)
    assert length % tile == 0
    nr, npar, no = len(rows), len(params), len(outs)

    def body(*refs):
        vals = [r[...] for r in refs[:nr + npar]]
        o, a = fn(*vals)
        for ref, val in zip(refs[nr + npar:nr + npar + no], o):
            ref[...] = val.astype(ref.dtype)
        i = pl.program_id(0)
        for ref, val in zip(refs[nr + npar + no:], a):
            @pl.when(i == 0)
            def _(ref=ref, val=val):
                ref[...] = val

            @pl.when(i > 0)
            def _(ref=ref, val=val):
                ref[...] += val

    in_specs = [pl.BlockSpec((tile, w), functools.partial(lambda i, cb: (i, cb), cb=cb)) for _, cb, w in rows]
    in_specs += [pl.BlockSpec(p.shape, lambda i: (0, 0)) for p in params]
    outs = [o if len(o) == 3 else (o[0], o[1], o[0]) for o in outs]
    out_specs = [pl.BlockSpec((tile, w), lambda i: (i, 0)) for w, _, _ in outs]
    out_specs += [pl.BlockSpec((r, w), lambda i: (0, 0)) for r, w in accs]
    out_shape = [jax.ShapeDtypeStruct((length, full), dt) for _, dt, full in outs]
    out_shape += [jax.ShapeDtypeStruct((r, w), F32) for r, w in accs]
    res = pl.pallas_call(
        body, name=name, grid=(length // tile,), in_specs=in_specs, out_specs=out_specs, out_shape=out_shape,
        compiler_params=_params(1),
    )(*[r[0] for r in rows], *params)
    return res[:no], res[no:]


CONV_SUB = 32


def _conv_fwd(proj, conv_w, conv_b, name, hosted=None):
    length = proj.shape[0]
    tl, cw = min(1024, length), 512
    cb0 = _OFF["xs"] // cw
    grid = (CONV_DIM // cw, length // tl)
    h_arrays, h_bcast = hosted if hosted else ((), False)
    nh = len(h_arrays)

    def body(*refs):
        u_ref, halo_ref, w_ref, b_ref = refs[:4]
        o_ref = refs[4 + nh]
        win = refs[5 + 2 * nh]
        j, i = pl.program_id(0), pl.program_id(1)
        if nh:
            start, wait = _chip_exchange(refs[4:4 + nh], refs[5 + nh:5 + 2 * nh], refs[-2], refs[-1], h_bcast)

            @pl.when((j == 0) & (i == 0))
            def _():
                start()

            @pl.when((j == grid[0] - 1) & (i == grid[1] - 1))
            def _():
                wait()

        win[0:8, :] = jnp.where(i > 0, halo_ref[...], 0.0)
        win[8:8 + tl, :] = u_ref[...]
        taps = [w_ref[kk:kk + 1, :] for kk in range(SSD_CONV)]
        bias = b_ref[...]
        for r in range(0, tl, CONV_SUB):
            acc = bias + taps[0] * win[5 + r:5 + r + CONV_SUB, :]
            for kk in range(1, SSD_CONV):
                acc = acc + taps[kk] * win[5 + kk + r:5 + kk + r + CONV_SUB, :]
            o_ref[r:r + CONV_SUB, :] = acc

    any_spec = pl.BlockSpec(memory_space=pl.ANY)
    res = pl.pallas_call(
        body, name=name, grid=grid,
        in_specs=[pl.BlockSpec((tl, cw), lambda j, i: (i, cb0 + j)),
                  pl.BlockSpec((8, cw), lambda j, i: (jnp.maximum(i * (tl // 8) - 1, 0), cb0 + j)),
                  pl.BlockSpec((SSD_CONV, cw), lambda j, i: (0, j)),
                  pl.BlockSpec((1, cw), lambda j, i: (0, j))] + [any_spec] * nh,
        out_specs=[pl.BlockSpec((tl, cw), lambda j, i: (i, j))] + [any_spec] * nh,
        out_shape=[jax.ShapeDtypeStruct((length, CONV_DIM), F32)]
        + [jax.ShapeDtypeStruct(((4,) + s.shape) if h_bcast else s.shape, s.dtype) for s in h_arrays],
        scratch_shapes=[pltpu.VMEM((8 + tl, cw), F32)]
        + ([pltpu.SemaphoreType.DMA((4 * nh,)), pltpu.SemaphoreType.DMA((4 * nh,))] if nh else []),
        compiler_params=_params(2),
    )(proj, proj, conv_w, conv_b, *h_arrays)
    return (res[0], list(res[1:])) if nh else res[0]


def _conv_bwd(dpre, proj, conv_w, dproj, part, name):
    length = proj.shape[0]
    width = dpre.shape[1]
    tl, cw = min(512, length), 512
    cb0 = _OFF["xs"] // cw + part
    n_t = length // tl

    def body(d_ref, dnext_ref, u_ref, halo_ref, w_ref, _, du_ref, dw_ref, db_ref, dwin, uwin):
        i = pl.program_id(1)
        dwin[0:tl, :] = d_ref[...].astype(F32)
        dwin[tl:tl + 8, :] = jnp.where(i < n_t - 1, dnext_ref[...].astype(F32)[0:8], 0.0)
        uwin[0:8, :] = jnp.where(i > 0, halo_ref[...], 0.0)
        uwin[8:8 + tl, :] = u_ref[...]
        taps = [w_ref[kk:kk + 1, :] for kk in range(SSD_CONV)]
        sub = CONV_SUB // 2
        acc_w = [jnp.zeros((sub, cw), F32) for _ in range(SSD_CONV)]
        acc_b = jnp.zeros((sub, cw), F32)
        for r in range(0, tl, sub):
            d = dwin[r:r + sub, :]
            du = taps[3] * d
            for kk in range(SSD_CONV - 1):
                du = du + taps[kk] * dwin[3 - kk + r:3 - kk + r + sub, :]
            du_ref[r:r + sub, :] = du.astype(du_ref.dtype)
            for kk in range(SSD_CONV):
                acc_w[kk] = acc_w[kk] + d * uwin[5 + kk + r:5 + kk + r + sub, :]
            acc_b = acc_b + d
        dw = jnp.concatenate([jnp.sum(a, axis=0, keepdims=True) for a in acc_w], axis=0)
        db = jnp.sum(acc_b, axis=0, keepdims=True)

        @pl.when(i == 0)
        def _():
            dw_ref[...] = dw
            db_ref[...] = db

        @pl.when(i > 0)
        def _():
            dw_ref[...] += dw
            db_ref[...] += db

    return pl.pallas_call(
        body, name=name, grid=(width // cw, n_t), input_output_aliases={5: 0},
        in_specs=[pl.BlockSpec((tl, cw), lambda j, i: (i, j)),
                  pl.BlockSpec((16, cw), lambda j, i: (jnp.minimum((i + 1) * (tl // 16), length // 16 - 1), j)),
                  pl.BlockSpec((tl, cw), lambda j, i: (i, cb0 + j)),
                  pl.BlockSpec((8, cw), lambda j, i: (jnp.maximum(i * (tl // 8) - 1, 0), cb0 + j)),
                  pl.BlockSpec((SSD_CONV, cw), lambda j, i: (0, part + j)),
                  pl.BlockSpec(memory_space=pl.ANY)],
        out_specs=[pl.BlockSpec((tl, cw), lambda j, i: (i, cb0 + j)),
                   pl.BlockSpec((SSD_CONV, cw), lambda j, i: (0, j)),
                   pl.BlockSpec((1, cw), lambda j, i: (0, j))],
        out_shape=[jax.ShapeDtypeStruct((length, NP), BF),
                   jax.ShapeDtypeStruct((SSD_CONV, width), F32),
                   jax.ShapeDtypeStruct((1, width), F32)],
        scratch_shapes=[pltpu.VMEM((tl + 8, cw), F32), pltpu.VMEM((8 + tl, cw), F32)],
        compiler_params=_params(2),
    )(dpre, dpre, proj, proj, conv_w, dproj)


def _ssd_in_specs(rev, nc):
    def cidx(c):
        return nc - 1 - c if rev else c
    whole = lambda c: (0, 0)
    return [pl.BlockSpec((BLK, SSD_D_INNER), lambda c: (cidx(c), 0)),
            pl.BlockSpec((BLK, BC_DIM), lambda c: (cidx(c), SSD_D_INNER // BC_DIM)),
            pl.BlockSpec((BLK, BC_DIM), lambda c: (cidx(c), SSD_D_INNER // BC_DIM + 1)),
            pl.BlockSpec((BLK, SSD_D_INNER), lambda c: (cidx(c), _OFF["z"] // SSD_D_INNER)),
            pl.BlockSpec((SSD_HEADS, BLK), lambda c: (0, cidx(c))),
            pl.BlockSpec((SSD_HEADS, 1), whole),
            pl.BlockSpec((SSD_HEADS, 1), whole),
            pl.BlockSpec((1, SSD_D_INNER), whole),
            pl.BlockSpec((1, SSD_D_INNER), whole)]


def _group_args(g, px, pb, pc, z, dtr, dtb, alog, dsk, nw):
    wide, narrow, heads = slice(512 * g, 512 * (g + 1)), slice(BLK * g, BLK * (g + 1)), slice(8 * g, 8 * (g + 1))
    return (px[:, wide], pb[:, narrow], pc[:, narrow], z[:, wide], dtr[heads, :], dtb[heads, :], alog[heads, :],
            dsk[:, wide], nw[:, wide])


def _ssd_fwd(pre, proj, dt_t, dtb, alog, dsk, nw, name, hosted=None):
    length = pre.shape[0]
    nc = length // BLK
    h_arrays, h_bcast = hosted if hosted else ((), False)
    nh = len(h_arrays)

    def body(*refs):
        px, pb, pc, z, dtr, dtb_r, al_r, dsk_r, nw_r = refs[:9]
        y_ref, sin_ref = refs[9 + nh:11 + nh]
        st = refs[11 + 2 * nh]
        c = pl.program_id(0)
        if nh:
            start, wait = _chip_exchange(refs[9:9 + nh], refs[11 + nh:11 + 2 * nh], refs[-2], refs[-1], h_bcast)

        @pl.when(c == 0)
        def _():
            st[...] = jnp.zeros_like(st)
            if nh:
                start()

        for g in range(SSD_GROUPS):
            s_in = st[g]
            sin_ref[g] = s_in
            y, s_out = _ssd_group(*_group_args(g, px, pb, pc, z, dtr, dtb_r, al_r, dsk_r, nw_r), s_in)
            y_ref[:, 512 * g:512 * (g + 1)] = y.astype(y_ref.dtype)
            st[g] = s_out

        if nh:
            @pl.when(c == nc - 1)
            def _():
                wait()

    any_spec = pl.BlockSpec(memory_space=pl.ANY)
    res = pl.pallas_call(
        body, name=name, grid=(nc,), in_specs=_ssd_in_specs(False, nc) + [any_spec] * nh,
        out_specs=[pl.BlockSpec((BLK, SSD_D_INNER), lambda c: (c, 0)),
                   pl.BlockSpec((SSD_GROUPS, None, SSD_STATE, 512), lambda c: (0, c, 0, 0))] + [any_spec] * nh,
        out_shape=[jax.ShapeDtypeStruct((length, SSD_D_INNER), BF),
                   jax.ShapeDtypeStruct((SSD_GROUPS, nc, SSD_STATE, 512), F32)]
        + [jax.ShapeDtypeStruct(((4,) + s.shape) if h_bcast else s.shape, s.dtype) for s in h_arrays],
        scratch_shapes=[pltpu.VMEM((SSD_GROUPS, SSD_STATE, 512), F32)]
        + ([pltpu.SemaphoreType.DMA((4 * nh,)), pltpu.SemaphoreType.DMA((4 * nh,))] if nh else []),
        compiler_params=_params(1),
    )(pre, pre, pre, proj, dt_t, dtb, alog, dsk, nw, *h_arrays)
    return (res[0], res[1], list(res[2:])) if nh else (res[0], res[1])


def _ssd_bwd(pre, proj, dt_t, dtb, alog, dsk, nw, s_in, dy, dproj, name, hosted=None):
    length = pre.shape[0]
    nc = length // BLK
    h_arrays, h_bcast = hosted if hosted else ((), False)
    nh = len(h_arrays)

    def body(*refs):
        px, pb, pc, z, dtr, dtb_r, al_r, dsk_r, nw_r, sin_r, dy_r = refs[:11]
        dpx, dpb, dpc, dz, ddt, ddtb, dal, ddsk, dnw = refs[12 + nh:21 + nh]
        dst = refs[21 + 2 * nh]
        c = pl.program_id(0)
        if nh:
            start, wait = _chip_exchange(refs[12:12 + nh], refs[21 + nh:21 + 2 * nh], refs[-2], refs[-1], h_bcast)

        @pl.when(c == 0)
        def _():
            dst[...] = jnp.zeros_like(dst)
            if nh:
                start()

        for grp in range(SSD_GROUPS):
            wide, narrow, heads = slice(512 * grp, 512 * (grp + 1)), slice(BLK * grp, BLK * (grp + 1)), slice(8 * grp, 8 * (grp + 1))
            _, vjp = jax.vjp(_ssd_group, *_group_args(grp, px, pb, pc, z, dtr, dtb_r, al_r, dsk_r, nw_r), sin_r[grp])
            g = vjp((dy_r[:, wide], dst[grp]))
            dpx[:, wide] = g[0].astype(dpx.dtype)
            dpb[:, narrow] = g[1].astype(dpb.dtype)
            dpc[:, narrow] = g[2].astype(dpc.dtype)
            dz[:, wide] = g[3].astype(dz.dtype)
            ddt[heads, :] = g[4]
            dst[grp] = g[9]
            for ref, val, idx in ((ddtb, g[5], (heads, slice(None))), (dal, g[6], (heads, slice(None))),
                                  (ddsk, g[7], (slice(None), wide)), (dnw, g[8], (slice(None), wide))):
                @pl.when(c == 0)
                def _(ref=ref, val=val, idx=idx):
                    ref[idx] = val

                @pl.when(c > 0)
                def _(ref=ref, val=val, idx=idx):
                    ref[idx] += val

        if nh:
            @pl.when(c == nc - 1)
            def _():
                wait()

    rev = lambda c: nc - 1 - c
    whole = lambda c: (0, 0)
    any_spec = pl.BlockSpec(memory_space=pl.ANY)
    in_specs = _ssd_in_specs(True, nc) + [
        pl.BlockSpec((SSD_GROUPS, None, SSD_STATE, 512), lambda c: (0, rev(c), 0, 0)),
        pl.BlockSpec((BLK, SSD_D_INNER), lambda c: (rev(c), 0)),
        any_spec] + [any_spec] * nh
    res = pl.pallas_call(
        body, name=name, grid=(nc,), in_specs=in_specs, input_output_aliases={11: 3},
        out_specs=[pl.BlockSpec((BLK, SSD_D_INNER), lambda c: (rev(c), 0)),
                   pl.BlockSpec((BLK, BC_DIM), lambda c: (rev(c), 0)),
                   pl.BlockSpec((BLK, BC_DIM), lambda c: (rev(c), 0)),
                   pl.BlockSpec((BLK, SSD_D_INNER), lambda c: (rev(c), _OFF["z"] // SSD_D_INNER)),
                   pl.BlockSpec((SSD_HEADS, BLK), lambda c: (0, rev(c))),
                   pl.BlockSpec((SSD_HEADS, 1), whole),
                   pl.BlockSpec((SSD_HEADS, 1), whole),
                   pl.BlockSpec((1, SSD_D_INNER), whole),
                   pl.BlockSpec((1, SSD_D_INNER), whole)] + [any_spec] * nh,
        out_shape=[jax.ShapeDtypeStruct((length, SSD_D_INNER), BF),
                   jax.ShapeDtypeStruct((length, BC_DIM), BF),
                   jax.ShapeDtypeStruct((length, BC_DIM), BF),
                   jax.ShapeDtypeStruct((length, NP), BF),
                   jax.ShapeDtypeStruct((SSD_HEADS, length), F32),
                   jax.ShapeDtypeStruct((SSD_HEADS, 1), F32),
                   jax.ShapeDtypeStruct((SSD_HEADS, 1), F32),
                   jax.ShapeDtypeStruct((1, SSD_D_INNER), F32),
                   jax.ShapeDtypeStruct((1, SSD_D_INNER), F32)]
        + [jax.ShapeDtypeStruct(((4,) + s.shape) if h_bcast else s.shape, s.dtype) for s in h_arrays],
        scratch_shapes=[pltpu.VMEM((SSD_GROUPS, SSD_STATE, 512), F32)]
        + ([pltpu.SemaphoreType.DMA((4 * nh,)), pltpu.SemaphoreType.DMA((4 * nh,))] if nh else []),
        compiler_params=_params(1),
    )(pre, pre, pre, proj, dt_t, dtb, alog, dsk, nw, s_in, dy, dproj, *h_arrays)
    return tuple(res[:9]) + ((list(res[9:]),) if nh else ())


assert _OFF["gates"] == 0 and _OFF["z"] % SSD_D_INNER == 0 and _OFF["q"] % D_MODEL == 0 and _OFF["k"] + 512 == NP


def _attn_in_specs():
    prev = lambda n: (0, jnp.maximum(n - 1, 0), 0)
    cur = lambda n: (0, n, 0)
    return [pl.BlockSpec((BLK, D_MODEL), lambda n: (n, _OFF["q"] // D_MODEL)),
            pl.BlockSpec((2, BLK, BLK), prev),
            pl.BlockSpec((2, BLK, BLK), cur),
            pl.BlockSpec((2, BLK, BLK), prev),
            pl.BlockSpec((2, BLK, BLK), cur),
            pl.BlockSpec((1, ATT_HEADS), lambda n: (0, 0))]


def _attn_fwd(proj, k2, v2, sinks, name, hosted=None):
    length = proj.shape[0]
    nb = length // BLK
    h_arrays, h_bcast = hosted if hosted else ((), False)
    nh = len(h_arrays)

    def body(*refs):
        q, kp, kc, vp, vc, sk = refs[:6]
        o_ref = refs[6 + nh]
        n = pl.program_id(0)
        if nh:
            start, wait = _chip_exchange(refs[6:6 + nh], refs[7 + nh:7 + 2 * nh], refs[-2], refs[-1], h_bcast)

            @pl.when(n == 0)
            def _():
                start()

        for kv in range(2):
            cols = slice(512 * kv, 512 * (kv + 1))
            o = _attn_block(q[:, cols], kp[kv], kc[kv], vp[kv], vc[kv], sk[...], n, kv)
            o_ref[:, cols] = o.astype(o_ref.dtype)

        if nh:
            @pl.when(n == nb - 1)
            def _():
                wait()

    any_spec = pl.BlockSpec(memory_space=pl.ANY)
    res = pl.pallas_call(
        body, name=name, grid=(nb,), in_specs=_attn_in_specs() + [any_spec] * nh,
        out_specs=[pl.BlockSpec((BLK, D_MODEL), lambda n: (n, 0))] + [any_spec] * nh,
        out_shape=[jax.ShapeDtypeStruct((length, D_MODEL), BF)]
        + [jax.ShapeDtypeStruct(((4,) + s.shape) if h_bcast else s.shape, s.dtype) for s in h_arrays],
        scratch_shapes=[pltpu.SemaphoreType.DMA((4 * nh,)), pltpu.SemaphoreType.DMA((4 * nh,))] if nh else [],
        compiler_params=_params(1),
    )(proj, k2, k2, v2, v2, sinks, *h_arrays)
    return (res[0], list(res[1:])) if nh else res[0]


def _attn_bwd(proj, k2, v2, sinks, datt, dproj, name):
    length = proj.shape[0]

    def body(q, kp, kc, vp, vc, sk, do, _, dq, dkp, dkc, dvp, dvc, dsk):
        n = pl.program_id(0)
        dsinks = None
        for kv in range(2):
            cols = slice(512 * kv, 512 * (kv + 1))
            _, vjp = jax.vjp(lambda *a: _attn_block(*a, n, kv), q[:, cols], kp[kv], kc[kv], vp[kv], vc[kv], sk[...])
            g = vjp(do[:, cols].astype(F32))
            dq[:, cols] = g[0].astype(dq.dtype)
            dkp[kv] = g[1]
            dkc[kv] = g[2]
            dvp[kv] = g[3]
            dvc[kv] = g[4]
            dsinks = g[5] if dsinks is None else dsinks + g[5]

        @pl.when(n == 0)
        def _():
            dsk[...] = dsinks

        @pl.when(n > 0)
        def _():
            dsk[...] += dsinks

    blk3 = pl.BlockSpec((2, BLK, BLK), lambda n: (0, n, 0))
    kv_shape = jax.ShapeDtypeStruct((2, length, BLK), F32)
    return pl.pallas_call(
        body, name=name, grid=(length // BLK,),
        in_specs=_attn_in_specs() + [pl.BlockSpec((BLK, D_MODEL), lambda n: (n, 0)), pl.BlockSpec(memory_space=pl.ANY)],
        out_specs=[pl.BlockSpec((BLK, D_MODEL), lambda n: (n, _OFF["q"] // D_MODEL)), blk3, blk3, blk3, blk3,
                   pl.BlockSpec((1, ATT_HEADS), lambda n: (0, 0))],
        out_shape=[jax.ShapeDtypeStruct((length, NP), BF), kv_shape, kv_shape, kv_shape, kv_shape,
                   jax.ShapeDtypeStruct((1, ATT_HEADS), F32)],
        input_output_aliases={7: 0},
        compiler_params=_params(1),
    )(proj, k2, k2, v2, v2, sinks, datt, dproj)


DMA_CHUNK_BYTES = 1 << 20
N_STAGE = 8
LOOKAHEAD = 3


def _piece_chunks(shape, itemsize):
    if len(shape) < 2 or shape[-2] % 16 != 0:
        return [()]
    rows, cols = shape[-2:]
    step = min(rows, max(16, DMA_CHUNK_BYTES // (cols * itemsize) // 16 * 16))
    out = []
    for lead in itertools.product(*[range(d) for d in shape[:-2]]):
        for r0 in range(0, rows, step):
            out.append(lead + (pl.ds(r0, min(step, rows - r0)),))
    return out


def _exchange(srcs, group, bcast, name):
    size = {"c": 2, "xy": 4, "all": 8}[group]
    npeer = size - 1
    n = len(srcs)
    for s in srcs:
        assert bcast or s.shape[0] == size
    pieces = [s.shape if bcast else s.shape[1:] for s in srcs]
    chunks = [_piece_chunks(p, s.dtype.itemsize) for p, s in zip(pieces, srcs)]

    def body(*refs):
        src, out = refs[:n], refs[n:2 * n]
        load_sems, send_sems, recv_sems = refs[2 * n:]
        x, y, c = lax.axis_index("x"), lax.axis_index("y"), lax.axis_index("c")
        if group == "c":
            me = c
        elif group == "xy":
            me = 2 * x + y
        else:
            me = 4 * x + 2 * y + c

        def device(j):
            if group == "c":
                return (x, y, j)
            if group == "xy":
                return (j // 2, j % 2, c)
            return (j // 4, (j // 2) % 2, j % 2)

        def part(ref, idx):
            return ref.at[idx] if idx else ref

        def piece(a, j):
            return src[a] if bcast else src[a].at[j]

        for a in range(n):
            jobs = [(None, idx) for idx in chunks[a]] if bcast else [(d, idx) for idx in chunks[a] for d in range(1, size)]
            full_rows = chunks[a][0][-1].size if chunks[a][0] else None
            slot_shape = (full_rows, pieces[a][-1]) if chunks[a][0] else pieces[a]

            def stream(stage, a=a, jobs=jobs, full_rows=full_rows):
                def slot(q):
                    idx = jobs[q][1]
                    view = stage.at[q % N_STAGE]
                    if idx and idx[-1].size != full_rows:
                        view = view.at[pl.ds(0, idx[-1].size)]
                    return view

                def load(q):
                    d, idx = jobs[q]
                    j = me if d is None else jnp.bitwise_xor(me, d)
                    return pltpu.make_async_copy(part(piece(a, j), idx), slot(q), load_sems.at[q % N_STAGE])

                def sends(q):
                    d, idx = jobs[q]
                    return [pltpu.make_async_remote_copy(
                        src_ref=slot(q), dst_ref=part(out[a].at[me], idx),
                        send_sem=send_sems.at[(q % N_STAGE) * npeer + dd - 1], recv_sem=recv_sems.at[a * size + dd],
                        device_id=device(jnp.bitwise_xor(me, dd)), device_id_type=pl.DeviceIdType.MESH)
                        for dd in (range(1, size) if d is None else (d,))]

                nq = len(jobs)
                for q in range(nq + LOOKAHEAD):
                    if q < nq:
                        if q >= N_STAGE:
                            for cp in sends(q - N_STAGE):
                                cp.wait_send()
                        load(q).start()
                    if q >= LOOKAHEAD:
                        load(q - LOOKAHEAD).wait()
                        for cp in sends(q - LOOKAHEAD):
                            cp.start()
                for q in range(max(0, nq - N_STAGE), nq):
                    for cp in sends(q):
                        cp.wait_send()

            pl.run_scoped(stream, pltpu.VMEM((N_STAGE,) + tuple(slot_shape), srcs[a].dtype))

        for a in range(n):
            for d in range(1, size):
                j = jnp.bitwise_xor(me, d)
                pltpu.make_async_remote_copy(
                    src_ref=piece(a, j), dst_ref=out[a].at[j], send_sem=send_sems.at[0], recv_sem=recv_sems.at[a * size + d],
                    device_id=device(j), device_id_type=pl.DeviceIdType.MESH).wait_recv()

    any_spec = pl.BlockSpec(memory_space=pl.ANY)
    return pl.pallas_call(
        body, name=name, in_specs=[any_spec] * n, out_specs=[any_spec] * n,
        out_shape=[jax.ShapeDtypeStruct((size,) + tuple(p), s.dtype) for p, s in zip(pieces, srcs)],
        scratch_shapes=[pltpu.SemaphoreType.DMA((N_STAGE,)), pltpu.SemaphoreType.DMA((N_STAGE * npeer,)),
                        pltpu.SemaphoreType.DMA((n * size,))],
        compiler_params=pltpu.CompilerParams(vmem_limit_bytes=VMEM_LIMIT_BYTES),
    )(*srcs)


def _with_own(outs, owns, me):
    return [lax.dynamic_update_index_in_dim(o, w, me, 0) for o, w in zip(outs, owns)]


def _sum_slots(arr, name):
    k = arr.shape[0]
    rest = arr.shape[1:]
    width = rest[-1]
    rows_per = math.prod(rest[:-1])
    a2 = arr.reshape(k * rows_per, width)
    tile = rows_per
    for cand in (256, 128, 64, 32, 16, 8):
        if rows_per % cand == 0:
            tile = cand
            break
    nt = rows_per // tile

    def body(*refs):
        acc = refs[0][...]
        for r in refs[1:k]:
            acc = acc + r[...]
        refs[k][...] = acc

    in_specs = [pl.BlockSpec((tile, width), functools.partial(lambda i, s: (s * nt + i, 0), s=s)) for s in range(k)]
    out = pl.pallas_call(
        body, name=name, grid=(nt,), in_specs=in_specs, out_specs=pl.BlockSpec((tile, width), lambda i: (i, 0)),
        out_shape=jax.ShapeDtypeStruct((rows_per, width), arr.dtype), compiler_params=_params(1),
    )(*([a2] * k))
    return out.reshape(rest)


def _sum_pieces(recv, own, own_slotted, me, name):
    k = recv.shape[0]
    rest = recv.shape[1:]
    width = rest[-1]
    rows_per = math.prod(rest[:-1])
    r2 = recv.reshape(k * rows_per, width)
    o2 = own.reshape(-1, width)
    tile = rows_per
    for cand in (256, 128, 64, 32, 16, 8):
        if rows_per % cand == 0:
            tile = cand
            break
    nt = rows_per // tile

    def body(me_ref, *refs):
        mine = refs[k][...].astype(F32)
        acc = None
        for j in range(k):
            term = jnp.where(me_ref[0] == j, mine, refs[j][...].astype(F32))
            acc = term if acc is None else acc + term
        refs[k + 1][...] = acc
        refs[k + 2][...] = acc.astype(BF)

    in_specs = [pl.BlockSpec((tile, width), functools.partial(lambda i, m, s: (s * nt + i, 0), s=s)) for s in range(k)]
    in_specs.append(pl.BlockSpec((tile, width), (lambda i, m: (m[0] * nt + i, 0)) if own_slotted else (lambda i, m: (i, 0))))
    out_spec = pl.BlockSpec((tile, width), lambda i, m: (i, 0))
    out, out_bf = pl.pallas_call(
        body, name=name,
        grid_spec=pltpu.PrefetchScalarGridSpec(num_scalar_prefetch=1, grid=(nt,), in_specs=in_specs, out_specs=[out_spec, out_spec]),
        out_shape=[jax.ShapeDtypeStruct((rows_per, width), F32), jax.ShapeDtypeStruct((rows_per, width), BF)],
        compiler_params=_params(1),
    )(jnp.reshape(me, (1,)).astype(jnp.int32), *([r2] * k), o2)
    return out.reshape(rest), out_bf.reshape(rest)


def _adamw(w, g, m, v, name):
    def fn(w_, g_, m_, v_):
        m1 = ADAM_B1 * m_ + (1.0 - ADAM_B1) * g_
        v1 = ADAM_B2 * v_ + (1.0 - ADAM_B2) * (g_ * g_)
        m_hat = m1 / (1.0 - ADAM_B1 ** ADAM_STEP)
        v_hat = v1 / (1.0 - ADAM_B2 ** ADAM_STEP)
        delta = -ADAM_LR * (m_hat / (jnp.sqrt(v_hat) + ADAM_EPS) + ADAM_WD * w_)
        return (delta, m1, v1), ()

    rows, width = w.shape
    tile = rows
    for cand in (256, 128, 64, 32, 16, 8):
        if rows % cand == 0:
            tile = cand
            break
    o, _ = _rows(fn, [(w, 0, width), (g, 0, width), (m, 0, width), (v, 0, width)], [],
                 [(width, F32)] * 3, [], tile, name)
    return o


def _col_pieces(start, width, shard):
    out = []
    while width:
        chip, off = divmod(start, shard)
        take = min(width, shard - off)
        out.append((chip, off, take))
        start, width = start + take, width - take
    return out


def _gather_cols(parts, ranges, pad=0):
    shard = parts.shape[3]
    halves = []
    for h in range(2):
        cols = [parts[h, chip, :, off:off + w] for s, wd in ranges for chip, off, w in _col_pieces(s, wd, shard)]
        if pad:
            cols.append(jnp.zeros((parts.shape[2], pad), parts.dtype))
        halves.append(jnp.concatenate(cols, axis=1))
    return jnp.concatenate(halves, axis=0)


def _scatter_cols(g, ranges, shard):
    starts, pos = {}, 0
    for s, wd in ranges:
        starts[s] = (pos, wd)
        pos += wd
    order = sorted(starts)
    hr = g.shape[0] // 2
    out = []
    for h in range(2):
        per_chip = []
        for chip in range(4):
            cols = []
            for s in order:
                p0, wd = starts[s]
                lo, hi = max(s, chip * shard), min(s + wd, (chip + 1) * shard)
                if lo < hi:
                    cols.append(g[h * hr:(h + 1) * hr, p0 + lo - s:p0 + hi - s])
            per_chip.append(jnp.concatenate(cols, axis=1))
        out.append(jnp.stack(per_chip))
    return jnp.stack(out)


_IN_RANGES = tuple((o, wd) for _, o, wd in _PACK)


def _double_heads(t):
    length = t.shape[0]
    h = jnp.transpose(t.reshape(length, 2, 64), (1, 0, 2))
    return jnp.concatenate([h, h], axis=-1)


def _fold_heads(d_cur, d_prev):
    length = d_cur.shape[1]
    d = d_cur + jnp.concatenate([d_prev[:, BLK:], jnp.zeros((2, BLK, BLK), F32)], axis=1)
    d = d[..., :64] + d[..., 64:]
    return jnp.transpose(d, (1, 0, 2)).reshape(length, 128)


def _ln_fwd(a, r, g, b, name):
    def fn(a_, r_, g_, b_):
        y = _layer_norm(ALPHA * a_ + r_, g_, b_)
        return (y, y), ()
    o, _ = _rows(fn, [(a, 0, D_MODEL), (r, 0, D_MODEL)], [g, b], [(D_MODEL, F32), (D_MODEL, BF)], [], ROW_TILE, name)
    return o


def _ln_bwd(a, r, d1, d2, g, b, name):
    def fn(a_, r_, d1_, d2_, g_, b_):
        _, vjp = jax.vjp(_layer_norm, ALPHA * a_ + r_, g_, b_)
        ds, dg, db = vjp(ALPHA * d1_ + d2_)
        return (ds, ds), (dg, db)
    o, acc = _rows(fn, [(a, 0, D_MODEL), (r, 0, D_MODEL), (d1, 0, D_MODEL), (d2, 0, D_MODEL)], [g, b],
                   [(D_MODEL, F32), (D_MODEL, BF)], [(1, D_MODEL), (1, D_MODEL)], ROW_TILE, name)
    return o[0], o[1], acc[0], acc[1]


def _layer_fwd(l, h, hb, w, p, hosted=None, hosted_attn=None, hosted_conv=None, late_weights=None):
    sv = {"h": h, "hb": hb}
    proj = _mm(hb, w["win"], "nn", F32, 4096, 512, 1024, f"proj{l}")
    sv["proj"] = proj
    pre = _conv_fwd(proj, p["conv_w"], p["conv_b"], f"conv{l}", hosted_conv)
    if hosted_conv:
        pre, got = pre
        w.update(late_weights(got))
    sv["pre"] = pre
    dt_t = jnp.transpose(proj[:, _OFF["dt"]:_OFF["dt"] + SSD_HEADS])
    sv["dt_t"] = dt_t
    res = _ssd_fwd(pre, proj, dt_t, p["dt_bias"], p["a_log"], p["d_skip_c"], p["ssd_norm_w"], f"ssd{l}", hosted)
    yn, s_in = res[:2]
    if hosted:
        sv["hosted"] = res[2]
    sv["yn"], sv["s_in"] = yn, s_in
    ya = _mm(yn, w["wso"], "nn", BF, 1024, 1024, 2048, f"ssdout{l}")
    k2 = _double_heads(proj[:, _OFF["k"]:_OFF["k"] + 128])
    v2 = _double_heads(proj[:, _OFF["v"]:_OFF["v"] + 128])
    sv["k2"], sv["v2"] = k2, v2
    att = _attn_fwd(proj, k2, v2, p["att_sinks"], f"attn{l}", hosted_attn)
    if hosted_attn:
        att, sv["hosted_attn"] = att
    sv["att"] = att
    yb = _mm(att, w["wao"], "nn", BF, 1024, 1024, 1024, f"attout{l}")
    sv["ya"], sv["yb"] = ya, yb
    gcb = _OFF["gates"] // 1024

    def gate_fn(ga, gb, ya_, yb_):
        return (_sigmoid(ga) * ya_.astype(F32) + _sigmoid(gb) * yb_.astype(F32),), ()
    (ub,), _ = _rows(gate_fn, [(proj, gcb, 1024), (proj, gcb + 1, 1024), (ya, 0, 1024), (yb, 0, 1024)], [],
                     [(D_MODEL, BF)], [], ROW_TILE, f"gate{l}")
    sv["ub"] = ub
    mix = _mm(ub, w["wmo"], "nn", F32, 1024, 1024, 1024, f"mixout{l}")
    sv["mix"] = mix
    h1, h1b = _ln_fwd(h, mix, p["ln_mix_g"], p["ln_mix_b"], f"lnmix{l}")
    sv["h1"], sv["h1b"] = h1, h1b
    gu = _mm(h1b, w["wgu"], "nn", BF, 4096, 512, 1024, f"ffnin{l}")
    sv["gu"] = gu

    def act_fn(g_, u_):
        return (_silu(g_.astype(F32)) * u_.astype(F32),), ()
    (act,), _ = _rows(act_fn, [(gu, 0, FFN_HIDDEN), (gu, 1, FFN_HIDDEN)], [], [(FFN_HIDDEN, BF)], [], ROW_TILE, f"swiglu{l}")
    sv["act"] = act
    ffn = _mm(act, w["wd"], "nn", F32, 1024, 1024, FFN_HIDDEN, f"ffnout{l}")
    sv["ffn"] = ffn
    return sv


def _layer_bwd(l, sv, w, p, ds_f, ds_b, hosted=(), early_hook=None, late_hook=None):
    g = {}
    g["w_ffn_down"] = _mm(sv["act"], ds_b, "tn", F32, 1408, 1024, 2048, f"dwd{l}")
    dact = _mm(ds_b, w["wd"], "nt", BF, 1024, FFN_HIDDEN, 1024, f"dact{l}")

    def act_bwd(g_, u_, d_):
        _, vjp = jax.vjp(lambda a, b: _silu(a) * b, g_.astype(F32), u_.astype(F32))
        dg_, du_ = vjp(d_.astype(F32))
        return (jnp.concatenate([dg_, du_], axis=1),), ()
    (dgu,), _ = _rows(act_bwd, [(sv["gu"], 0, FFN_HIDDEN), (sv["gu"], 1, FFN_HIDDEN), (dact, 0, FFN_HIDDEN)], [],
                      [(2 * FFN_HIDDEN, BF)], [], ROW_TILE // 2, f"dswiglu{l}")
    dwgu = _mm(sv["h1b"], dgu, "tn", F32, 1024, 1408, 2048, f"dwgu{l}")
    g["w_ffn_gate"], g["w_ffn_up"] = dwgu[:, :FFN_HIDDEN], dwgu[:, FFN_HIDDEN:]
    dh1 = _mm(dgu, w["wgu"], "nt", F32, 1024, 1024, 2816, f"dh1{l}")
    ds2_f, ds2_b, g["ln_mix_g"], g["ln_mix_b"] = _ln_bwd(sv["h"], sv["mix"], ds_f, dh1, p["ln_mix_g"], p["ln_mix_b"], f"dlnmix{l}")
    g["w_mix_out"] = _mm(sv["ub"], ds2_b, "tn", F32, 1024, 1024, 2048, f"dwmo{l}")
    du = _mm(ds2_b, w["wmo"], "nt", BF, 1024, 1024, 1024, f"du{l}")
    proj = sv["proj"]
    gcb = _OFF["gates"] // 1024

    def gate_bwd(ga, gb, ya_, yb_, du_):
        _, vjp = jax.vjp(lambda a, b, c, d: _sigmoid(a) * c + _sigmoid(b) * d, ga, gb, ya_.astype(F32), yb_.astype(F32))
        dga, dgb, dya, dyb = vjp(du_.astype(F32))
        return (jnp.concatenate([dga, dgb], axis=1), dya, dyb), ()
    (dproj, dya, dyb), _ = _rows(
        gate_bwd, [(proj, gcb, 1024), (proj, gcb + 1, 1024), (sv["ya"], 0, 1024), (sv["yb"], 0, 1024), (du, 0, 1024)], [],
        [(2048, BF, NP), (D_MODEL, BF), (D_MODEL, BF)], [], ROW_TILE, f"dgate{l}")
    g["w_att_out"] = _mm(sv["att"], dyb, "tn", F32, 1024, 1024, 2048, f"dwao{l}")
    datt = _mm(dyb, w["wao"], "nt", BF, 1024, 1024, 1024, f"datt{l}")
    g["w_ssd_out"] = _mm(sv["yn"], dya, "tn", F32, 1024, 1024, 2048, f"dwso{l}")
    dyn = _mm(dya, w["wso"], "nt", F32, 1024, 1024, 1024, f"dyn{l}")
    hosted = list(hosted) + (list(early_hook(g)) if early_hook else [])
    dproj, dkp, dkc, dvp, dvc, g["att_sinks"] = _attn_bwd(proj, sv["k2"], sv["v2"], p["att_sinks"], datt, dproj, f"dattn{l}")
    res = _ssd_bwd(sv["pre"], proj, sv["dt_t"], p["dt_bias"], p["a_log"], p["d_skip_c"], p["ssd_norm_w"], sv["s_in"], dyn,
                   dproj, f"dssd{l}", (hosted, False) if hosted else None)
    (dpx, dpb, dpc, dproj, ddt_t, g["dt_bias"], g["a_log"], ddsk, g["ssd_norm_w"]) = res[:9]
    if hosted:
        g["hosted"] = res[9]
    g["d_skip"] = jnp.sum(ddsk.reshape(SSD_HEADS, 64), axis=1)
    dws, dbs = [], []
    for nm, dpart, part in (("x", dpx, 0), ("b", dpb, 4), ("c", dpc, 5)):
        dproj, dw_, db_ = _conv_bwd(dpart, proj, p["conv_w"], dproj, part, f"dconv{nm}{l}")
        dws.append(dw_)
        dbs.append(db_)
    g["conv_w"], g["conv_b"] = jnp.concatenate(dws, axis=1), jnp.concatenate(dbs, axis=1)
    length = proj.shape[0]
    tail = jnp.concatenate([_fold_heads(dkc, dkp).astype(BF), _fold_heads(dvc, dvp).astype(BF),
                            jnp.transpose(ddt_t).astype(BF), jnp.zeros((length, _PAD), BF)], axis=1)
    dproj = lax.dynamic_update_slice(dproj, tail, (0, _OFF["k"]))
    g["w_in"] = _mm(sv["hb"], dproj, "tn", F32, 1024, 2176, 1024, f"dwin{l}")
    late = list(late_hook(g)) if late_hook else []
    dh = _mm(dproj, w["win"], "nt", F32, 1024, 1024, 2176, f"dh{l}", (late, False) if late else None)
    if late:
        dh, g["late_hosted"] = dh
    return g, ds2_f, dh


def kernel(x, ln_in_g, ln_in_b, w_in, conv_w, conv_b, dt_bias, a_log, d_skip, ssd_norm_w, att_sinks, w_ssd_out, w_att_out, w_mix_out, ln_mix_g, ln_mix_b, w_ffn_gate, w_ffn_up, w_ffn_down, ln_ffn_g, ln_ffn_b, loss_target, m_ln_in_g, m_ln_in_b, m_w_in, m_conv_w, m_conv_b, m_dt_bias, m_a_log, m_d_skip, m_ssd_norm_w, m_att_sinks, m_w_ssd_out, m_w_att_out, m_w_mix_out, m_ln_mix_g, m_ln_mix_b, m_w_ffn_gate, m_w_ffn_up, m_w_ffn_down, m_ln_ffn_g, m_ln_ffn_b, v_ln_in_g, v_ln_in_b, v_w_in, v_conv_w, v_conv_b, v_dt_bias, v_a_log, v_d_skip, v_ssd_norm_w, v_att_sinks, v_w_ssd_out, v_w_att_out, v_w_mix_out, v_ln_mix_g, v_ln_mix_b, v_w_ffn_gate, v_w_ffn_up, v_w_ffn_down, v_ln_ffn_g, v_ln_ffn_b):
    env = dict(locals())
    wts = {n: env[n] for n in WEIGHTS}
    mom1 = {n: env["m_" + n] for n in WEIGHTS}
    mom2 = {n: env["v_" + n] for n in WEIGHTS}
    ci = lax.axis_index("c")
    chip = 2 * lax.axis_index("x") + lax.axis_index("y")
    xs_ = x[0]
    tgt = loss_target[0]

    COLS = ("w_in", "w_ffn_gate", "w_ffn_up")

    def half(t):
        return lax.dynamic_slice_in_dim(t, ci * (t.shape[0] // 2), t.shape[0] // 2, axis=0)

    def assemble(name, parts):
        if name == "w_in":
            return _gather_cols(parts, _IN_RANGES, _PAD)
        if name in COLS:
            return _gather_cols(parts, ((0, 4 * parts.shape[3]),))
        return jnp.concatenate([parts[h, s] for s in range(4) for h in range(2)], axis=0)

    def split(name, gfull):
        if name == "w_in":
            return _scatter_cols(gfull, _IN_RANGES, IN_DIM // 4)
        if name in COLS:
            return _scatter_cols(gfull, ((0, gfull.shape[1]),), gfull.shape[1] // 4)
        hr = gfull.shape[0] // 8
        return jnp.stack([jnp.stack([gfull[(2 * s + h) * hr:(2 * s + h + 1) * hr] for s in range(4)]) for h in range(2)])

    mine = [[half(wts[n][l].astype(BF)) for n in BIG] for l in range(DEPTH)]
    conv_all, = _with_own(_exchange([conv_w], "xy", True, "gather_conv"), [conv_w], chip)
    conv_full = jnp.transpose(conv_all, (1, 2, 0, 3)).reshape(DEPTH, SSD_CONV, CONV_DIM)

    KEYS = {"w_in": "win", "w_ssd_out": "wso", "w_att_out": "wao", "w_mix_out": "wmo", "w_ffn_down": "wd"}

    def layer_weights(l, names, by_chip):
        own = [mine[l][BIG.index(n)] for n in names]
        by_chip = _with_own(by_chip, own, chip)
        parts = _with_own(_exchange(by_chip, "c", True, f"gather_cores{l}_{names[0]}"), by_chip, ci)
        fw = {n: assemble(n, part) for n, part in zip(names, parts)}
        w = {KEYS[n]: fw[n] for n in names if n in KEYS}
        if "w_ffn_gate" in fw:
            w["wgu"] = jnp.concatenate([fw["w_ffn_gate"], fw["w_ffn_up"]], axis=1)
        return w

    layer_p = []
    for l in range(DEPTH):
        layer_p.append({
            "conv_w": conv_full[l], "conv_b": conv_b[l][None],
            "dt_bias": dt_bias[l][:, None], "a_log": a_log[l][:, None],
            "d_skip_c": jnp.repeat(d_skip[l], 64)[None], "ssd_norm_w": ssd_norm_w[l][None],
            "att_sinks": att_sinks[l][None], "ln_mix_g": ln_mix_g[l][None], "ln_mix_b": ln_mix_b[l][None],
            "ln_ffn_g": ln_ffn_g[l][None], "ln_ffn_b": ln_ffn_b[l][None],
        })

    def ln_in_fn(x_, g_, b_):
        y = _layer_norm(x_, g_, b_)
        return (y, y), ()
    (h, hb), _ = _rows(ln_in_fn, [(xs_, 0, D_MODEL)], [ln_in_g[None], ln_in_b[None]], [(D_MODEL, F32), (D_MODEL, BF)], [], ROW_TILE, "ln_in")
    rest = tuple(n for n in BIG if n != "w_in")
    layer_w = [layer_weights(0, ("w_in",), _exchange([mine[0][BIG.index("w_in")]], "xy", True, "gather_chips0"))]
    behind_ssd = [i for i, n in enumerate(BIG) if n in COLS]
    behind_attn = [i for i, n in enumerate(BIG) if n not in COLS]
    saved = []
    for l in range(DEPTH):
        nxt = l + 1 < DEPTH
        sv = _layer_fwd(l, h, hb, layer_w[l], layer_p[l], ([mine[l + 1][i] for i in behind_ssd], True) if nxt else None,
                        ([mine[l + 1][i] for i in behind_attn], True) if nxt else None,
                        ([mine[l][BIG.index(n)] for n in rest], True) if l == 0 else None,
                        functools.partial(layer_weights, l, rest))
        saved.append(sv)
        if nxt:
            got = dict(zip(behind_ssd + behind_attn, sv["hosted"] + sv["hosted_attn"]))
            layer_w.append(layer_weights(l + 1, BIG, [got[i] for i in range(len(BIG))]))
            h, hb = _ln_fwd(sv["h1"], sv["ffn"], layer_p[l]["ln_ffn_g"], layer_p[l]["ln_ffn_b"], f"lnffn{l}")

    def loss_fn(a_, r_, t_, g_, b_):
        y, vjp = jax.vjp(_layer_norm, ALPHA * a_ + r_, g_, b_)
        err = y - t_
        ds, dg, db = vjp(err * (1.0 / D_MODEL))
        part = 0.5 * jnp.sum(jnp.mean(err * err, axis=-1, keepdims=True), axis=0, keepdims=True)
        return (ds, ds), (dg, db, jnp.broadcast_to(part, (1, BLK)))
    sv = saved[-1]
    (ds_f, ds_b), (dg_last, db_last, loss_part) = _rows(
        loss_fn, [(sv["h1"], 0, D_MODEL), (sv["ffn"], 0, D_MODEL), (tgt, 0, D_MODEL)],
        [layer_p[-1]["ln_ffn_g"], layer_p[-1]["ln_ffn_b"]], [(D_MODEL, F32), (D_MODEL, BF)],
        [(1, D_MODEL), (1, D_MODEL), (1, BLK)], ROW_TILE, "loss")
    loss = lax.psum(loss_part[0, 0], ("x", "y", "c"))

    def reduce_cores(l, g, names):
        src = [split(n, g[n]) for n in names]
        recv = _exchange([s.astype(BF) for s in src], "c", False, f"reduce_cores{l}_{names[0]}")
        return {n: _sum_pieces(r, o, True, ci, f"sum_cores{l}_{n}") for n, r, o in zip(names, recv, src)}

    def bf16_of(sums, names):
        return [sums[n][1] for n in names]

    def add_chips(l, sums, from_chips):
        return [_sum_pieces(from_chips[n], sums[n][0], True, chip, f"sum_chips{l}_{n}")[0] for n in BIG]

    EARLY = tuple(n for n in BIG if n != "w_in")
    grads = [None] * DEPTH
    chip_sums = [None] * DEPTH
    for l in reversed(range(DEPTH)):
        sv = saved[l]
        if l == DEPTH - 1:
            lg, lb = dg_last, db_last
        else:
            ds_f, ds_b, lg, lb = _ln_bwd(sv["h1"], sv["ffn"], d1, d2, layer_p[l]["ln_ffn_g"], layer_p[l]["ln_ffn_b"], f"dlnffn{l}")
        sums = {}

        def early_hook(g, l=l, sums=sums):
            sums.update(reduce_cores(l, g, EARLY))
            return bf16_of(sums, EARLY)

        def late_hook(g, l=l, sums=sums):
            sums.update(reduce_cores(l, g, ("w_in",)))
            return bf16_of(sums, ("w_in",))
        if l == 0:
            g, d1, d2 = _layer_bwd(l, sv, layer_w[l], layer_p[l], ds_f, ds_b, bf16_of(chip_sums[l + 1], BIG), early_hook, late_hook)
        else:
            g, d1, d2 = _layer_bwd(l, sv, layer_w[l], layer_p[l], ds_f, ds_b)
            sums.update(reduce_cores(l, g, BIG))
        g["ln_ffn_g"], g["ln_ffn_b"] = lg, lb
        grads[l] = g
        chip_sums[l] = sums

    def ln_in_bwd(x_, d1_, d2_, g_, b_):
        _, vjp = jax.vjp(_layer_norm, x_, g_, b_)
        dx, dg, db = vjp(ALPHA * d1_ + d2_)
        return (dx,), (dg, db)
    (grad_x,), (g_ln_in_g, g_ln_in_b) = _rows(
        ln_in_bwd, [(xs_, 0, D_MODEL), (d1, 0, D_MODEL), (d2, 0, D_MODEL)], [ln_in_g[None], ln_in_b[None]],
        [(D_MODEL, F32)], [(1, D_MODEL), (1, D_MODEL)], ROW_TILE, "dln_in")

    carried = grads[0]["hosted"]
    from_chips = dict(zip(EARLY, carried[len(BIG):]))
    from_chips["w_in"], = grads[0]["late_hosted"]
    reduced = add_chips(0, chip_sums[0], from_chips) + add_chips(1, chip_sums[1], dict(zip(BIG, carried[:len(BIG)])))
    halves = _with_own(_exchange(reduced, "c", True, "swap_halves"), reduced, ci)
    shards = [t.reshape(2 * t.shape[1], t.shape[2]) for t in halves]
    big_grad = {n: jnp.stack([shards[l * len(BIG) + i] for l in range(DEPTH)]).reshape(wts[n].shape) for i, n in enumerate(BIG)}

    small_g = {"ln_in_g": g_ln_in_g[0], "ln_in_b": g_ln_in_b[0]}
    for n in SMALL[2:]:
        small_g[n] = jnp.stack([grads[l][n].reshape(wts[n].shape[1:] if n != "conv_w" else (SSD_CONV, CONV_DIM)) for l in range(DEPTH)])
    flat = jnp.concatenate([small_g[n].reshape(-1) for n in SMALL])
    n_small = flat.shape[0]
    width = -(-n_small // 1024) * 1024
    flat = jnp.pad(flat, (0, width - n_small)).reshape(8, width // 8)
    gathered, = _with_own(_exchange([flat], "all", True, "gather_small"), [flat], 2 * chip + ci)
    total = _sum_slots(gathered, "sum_small").reshape(-1)
    small_grad, off = {}, 0
    for n in SMALL:
        shp = small_g[n].shape
        cnt = math.prod(shp)
        small_grad[n] = total[off:off + cnt].reshape(shp)
        off += cnt
    small_grad["conv_w"] = lax.dynamic_slice_in_dim(small_grad["conv_w"], chip * (CONV_DIM // 4), CONV_DIM // 4, axis=2)

    out_g, out_d, out_m, out_v = {}, {}, {}, {}
    for n in BIG:
        shp = wts[n].shape
        two_d = lambda t: t.reshape(shp[0] * shp[1], shp[2])
        d_, m_, v_ = _adamw(two_d(wts[n]), two_d(big_grad[n]), two_d(mom1[n]), two_d(mom2[n]), f"adamw_{n}")
        out_g[n], out_d[n], out_m[n], out_v[n] = big_grad[n], d_.reshape(shp), m_.reshape(shp), v_.reshape(shp)

    def flat_small(d):
        f = jnp.concatenate([d[n].reshape(-1) for n in SMALL])
        return jnp.pad(f, (0, swidth - f.shape[0])).reshape(8, swidth // 8)
    n_sw = sum(math.prod(wts[n].shape) for n in SMALL)
    swidth = -(-n_sw // 1024) * 1024
    d_, m_, v_ = _adamw(flat_small(wts), flat_small(small_grad), flat_small(mom1), flat_small(mom2), "adamw_small")
    off = 0
    for n in SMALL:
        shp = wts[n].shape
        cnt = math.prod(shp)
        out_g[n] = small_grad[n]
        out_d[n] = d_.reshape(-1)[off:off + cnt].reshape(shp)
        out_m[n] = m_.reshape(-1)[off:off + cnt].reshape(shp)
        out_v[n] = v_.reshape(-1)[off:off + cnt].reshape(shp)
        off += cnt

    return (loss, grad_x[None], *[out_g[n] for n in WEIGHTS], *[out_d[n] for n in WEIGHTS],
            *[out_m[n] for n in WEIGHTS], *[out_v[n] for n in WEIGHTS])
```

```python
import functools
import itertools
import math

import jax
import jax.numpy as jnp
from jax import lax
from jax.experimental import pallas as pl
from jax.experimental.pallas import tpu as pltpu

F32 = jnp.float32
BF = jnp.bfloat16

D_MODEL = 1024
DEPTH = 2
ATT_HEADS = 16
ATT_HEAD_DIM = 64
BLK = 128
SSD_D_INNER = 2048
SSD_HEADS = 32
SSD_GROUPS = 4
SSD_STATE = 128
SSD_CONV = 4
BC_DIM = 512
CONV_DIM = 3072
FFN_HIDDEN = 2816
IN_DIM = 8480
LN_EPS = 1e-5
RMS_EPS = 1e-5
ALPHA = (2 * DEPTH) ** 0.25
ADAM_LR = 0.001
ADAM_B1 = 0.9
ADAM_B2 = 0.999
ADAM_EPS = 1e-08
ADAM_WD = 0.01
ADAM_STEP = 10

_PACK = (("gates", 6432, 2048), ("z", 1280, 2048), ("q", 0, 1024), ("xs", 3328, 2048), ("B", 5376, 512),
         ("C", 5888, 512), ("k", 1024, 128), ("v", 1152, 128), ("dt", 6400, 32))
NP = 8704
_OFF = {}
_o = 0
for _n, _, _w in _PACK:
    _OFF[_n] = _o
    _o += _w
_PAD = NP - _o

VMEM_LIMIT_BYTES = 56 * 1024 * 1024
ROW_TILE = 512
BIG = ("w_in", "w_ssd_out", "w_att_out", "w_mix_out", "w_ffn_gate", "w_ffn_up", "w_ffn_down")
WEIGHTS = ("ln_in_g", "ln_in_b", "w_in", "conv_w", "conv_b", "dt_bias", "a_log", "d_skip", "ssd_norm_w", "att_sinks",
           "w_ssd_out", "w_att_out", "w_mix_out", "ln_mix_g", "ln_mix_b", "w_ffn_gate", "w_ffn_up", "w_ffn_down",
           "ln_ffn_g", "ln_ffn_b")
SMALL = tuple(n for n in WEIGHTS if n not in BIG)


def _params(n_grid):
    return pltpu.CompilerParams(dimension_semantics=("arbitrary",) * n_grid, vmem_limit_bytes=VMEM_LIMIT_BYTES)


def _dot(a, b, ca, cb):
    return lax.dot_general(a.astype(BF), b.astype(BF), (((ca,), (cb,)), ((), ())), preferred_element_type=F32)


@jax.custom_vjp
def _nn(a, b):
    return _dot(a, b, 1, 0)


def _nn_f(a, b):
    return _dot(a, b, 1, 0), (a.astype(BF), b.astype(BF))


def _nn_b(res, g):
    a, b = res
    return _dot(g, b, 1, 1), _dot(a, g, 0, 0)


_nn.defvjp(_nn_f, _nn_b)


@jax.custom_vjp
def _nt(a, b):
    return _dot(a, b, 1, 1)


def _nt_f(a, b):
    return _dot(a, b, 1, 1), (a.astype(BF), b.astype(BF))


def _nt_b(res, g):
    a, b = res
    return _dot(g, b, 1, 0), _dot(g, a, 0, 0)


_nt.defvjp(_nt_f, _nt_b)


@jax.custom_vjp
def _tn(a, b):
    return _dot(a, b, 0, 0)


def _tn_f(a, b):
    return _dot(a, b, 0, 0), (a.astype(BF), b.astype(BF))


def _tn_b(res, g):
    a, b = res
    return _dot(b, g, 1, 1), _dot(a, g, 1, 0)


_tn.defvjp(_tn_f, _tn_b)


def _sigmoid(x):
    return 0.5 * jnp.tanh(0.5 * x) + 0.5


def _silu(x):
    return x * _sigmoid(x)


def _layer_norm(s, g, b):
    mu = jnp.mean(s, axis=-1, keepdims=True)
    sc = s - mu
    var = jnp.mean(sc * sc, axis=-1, keepdims=True)
    return sc * lax.rsqrt(var + LN_EPS) * g + b


def _ssd_group(pre_x, pre_b, pre_c, z, dtr, dtb, alog, dsk, nw, state):
    t = pre_x.shape[0]
    xs, bm, cm = _silu(pre_x), _silu(pre_b), _silu(pre_c)
    dt = jax.nn.softplus(dtr + dtb)
    da = dt * (-jnp.exp(alog))
    row = lax.broadcasted_iota(jnp.int32, (t, t), 0)
    col = lax.broadcasted_iota(jnp.int32, (t, t), 1)
    upper = (row <= col).astype(F32)
    cum = jnp.dot(da, upper, precision=lax.Precision.HIGHEST, preferred_element_type=F32)
    tot = jnp.sum(da, axis=1, keepdims=True)
    cb = _nt(cm, bm)
    tril = row >= col
    first = lax.broadcasted_iota(jnp.int32, (t, BLK), 1) < 64
    first_row = lax.broadcasted_iota(jnp.int32, (1, BLK), 1) < 64

    def col_form(r):
        return jnp.broadcast_to(r, (t, t)).T

    ys, new_state = [], []
    for p in range(4):
        h0, h1 = 2 * p, 2 * p + 1
        sl = slice(BLK * p, BLK * (p + 1))
        x_pair = xs[:, sl] * jnp.where(first, col_form(dt[h0:h0 + 1]), col_form(dt[h1:h1 + 1]))
        y_pair = None
        cc = []
        for h, keep in ((h0, first), (h1, jnp.logical_not(first))):
            cr = jnp.broadcast_to(cum[h:h + 1], (t, t))
            cc.append(cr.T)
            decay = jnp.exp(jnp.where(tril, cc[-1] - cr, -1e30))
            y_h = _nn(decay * cb, jnp.where(keep, x_pair, 0.0))
            y_pair = y_h if y_pair is None else y_pair + y_h
        s_pair = state[:, sl]
        y_pair = y_pair + _nn(cm, s_pair) * jnp.where(first, jnp.exp(cc[0]), jnp.exp(cc[1]))
        to_end = jnp.where(first, jnp.exp(tot[h0:h0 + 1] - cc[0]), jnp.exp(tot[h1:h1 + 1] - cc[1]))
        chunk_decay = jnp.where(first_row, jnp.exp(tot[h0:h0 + 1]), jnp.exp(tot[h1:h1 + 1]))
        new_state.append(s_pair * chunk_decay + _tn(bm, x_pair * to_end))
        ys.append(y_pair + dsk[:, sl] * xs[:, sl])
    y = jnp.concatenate(ys, axis=1) * _silu(z)
    y = y * lax.rsqrt(jnp.mean(y * y, axis=-1, keepdims=True) + RMS_EPS) * nw
    return y, jnp.concatenate(new_state, axis=1)


def _attn_block(q, kp, kc, vp, vc, sinks, n, kv):
    t = q.shape[0]
    kb = jnp.concatenate([kp, kc], axis=0)
    vb = jnp.concatenate([vp, vc], axis=0)
    qi = lax.broadcasted_iota(jnp.int32, (t, 2 * t), 0)
    kj = lax.broadcasted_iota(jnp.int32, (t, 2 * t), 1)
    rel = qi + t - kj
    valid = (rel >= 0) & (rel < t) & ((n * t - t + kj) >= 0)
    relf = rel.astype(F32)
    first = lax.broadcasted_iota(jnp.int32, (t, BLK), 1) < 64
    lane16 = lax.broadcasted_iota(jnp.int32, (1, ATT_HEADS), 1)
    outs = []
    for p in range(4):
        qp = q[:, BLK * p:BLK * (p + 1)] * (ATT_HEAD_DIM ** -0.5)
        o_pair = None
        for half, keep in enumerate((first, jnp.logical_not(first))):
            h = kv * 8 + 2 * p + half
            s = _nt(jnp.where(keep, qp, 0.0), kb)
            slope = jnp.exp((-8.0 * math.log(2.0) / ATT_HEADS) * jnp.asarray(h + 1, F32))
            s = jnp.where(valid, s - slope * relf, -1e30)
            sink = jnp.sum(jnp.where(lane16 == h, sinks, 0.0), axis=1, keepdims=True)
            m = lax.stop_gradient(jnp.maximum(jnp.max(s, axis=1, keepdims=True), sink))
            e = jnp.exp(s - m)
            den = jnp.sum(e, axis=1, keepdims=True) + jnp.exp(sink - m)
            o = _nn(e * (1.0 / den), vb)
            o_pair = o if o_pair is None else jnp.where(first, o_pair, o)
        outs.append(o_pair)
    return jnp.concatenate(outs, axis=1)


def _chip_exchange(src, out, send_sems, recv_sems, bcast):
    x, y, c = lax.axis_index("x"), lax.axis_index("y"), lax.axis_index("c")
    me = 2 * x + y

    def copy(a, d, slot):
        j = jnp.bitwise_xor(me, d)
        return pltpu.make_async_remote_copy(
            src_ref=src[a] if bcast else src[a].at[j], dst_ref=out[a].at[slot],
            send_sem=send_sems.at[a * 4 + d], recv_sem=recv_sems.at[a * 4 + d],
            device_id=(j // 2, j % 2, c), device_id_type=pl.DeviceIdType.MESH)

    pairs = [(a, d) for a in range(len(src)) for d in range(1, 4)]

    def start():
        for a, d in pairs:
            copy(a, d, me).start()

    def wait():
        for a, d in pairs:
            copy(a, d, jnp.bitwise_xor(me, d)).wait_recv()
        for a, d in pairs:
            copy(a, d, me).wait_send()

    return start, wait


def _mm(a, b, mode, out_dtype, tm, tn, tk, name, hosted=None):
    if mode == "nn":
        (m, k), (k2, n) = a.shape, b.shape
    elif mode == "nt":
        (m, k), (n, k2) = a.shape, b.shape
    else:
        (k, m), (k2, n) = a.shape, b.shape
    assert k == k2, (a.shape, b.shape, mode)
    tm, tn, tk = min(tm, m), min(tn, n), min(tk, k)
    assert m % tm == 0 and n % tn == 0 and k % tk == 0, (m, n, k, tm, tn, tk)
    nk = k // tk
    grid = (m // tm, n // tn, nk)
    ca, cb = {"nn": (1, 0), "nt": (1, 1), "tn": (0, 0)}[mode]
    h_arrays, h_bcast = hosted if hosted else ((), False)
    nh = len(h_arrays)

    def body(*refs):
        a_ref, b_ref = refs[:2]
        o_ref = refs[2 + nh]
        scratch = refs[3 + 2 * nh:]
        step = [pl.program_id(ax) for ax in range(3)]
        if nh:
            start, wait = _chip_exchange(refs[2:2 + nh], refs[3 + nh:3 + 2 * nh], scratch[-2], scratch[-1], h_bcast)

            @pl.when((step[0] == 0) & (step[1] == 0) & (step[2] == 0))
            def _():
                start()

        part = _dot(a_ref[...], b_ref[...], ca, cb)
        if nk == 1:
            o_ref[...] = part.astype(o_ref.dtype)
        else:
            acc_ref = scratch[0]

            @pl.when(step[2] == 0)
            def _():
                acc_ref[...] = part

            @pl.when(step[2] > 0)
            def _():
                acc_ref[...] += part

            @pl.when(step[2] == nk - 1)
            def _():
                o_ref[...] = acc_ref[...].astype(o_ref.dtype)

        if nh:
            @pl.when((step[0] == grid[0] - 1) & (step[1] == grid[1] - 1) & (step[2] == nk - 1))
            def _():
                wait()

    a_spec = pl.BlockSpec((tk, tm), lambda i, j, kk: (kk, i)) if mode == "tn" else pl.BlockSpec((tm, tk), lambda i, j, kk: (i, kk))
    b_spec = pl.BlockSpec((tn, tk), lambda i, j, kk: (j, kk)) if mode == "nt" else pl.BlockSpec((tk, tn), lambda i, j, kk: (kk, j))
    any_spec = pl.BlockSpec(memory_space=pl.ANY)
    res = pl.pallas_call(
        body, name=name, grid=grid,
        in_specs=[a_spec, b_spec] + [any_spec] * nh,
        out_specs=[pl.BlockSpec((tm, tn), lambda i, j, kk: (i, j))] + [any_spec] * nh,
        out_shape=[jax.ShapeDtypeStruct((m, n), out_dtype)]
        + [jax.ShapeDtypeStruct(((4,) + s.shape) if h_bcast else s.shape, s.dtype) for s in h_arrays],
        scratch_shapes=([] if nk == 1 else [pltpu.VMEM((tm, tn), F32)])
        + ([pltpu.SemaphoreType.DMA((4 * nh,)), pltpu.SemaphoreType.DMA((4 * nh,))] if nh else []),
        compiler_params=_params(3),
    )(a, b, *h_arrays)
    return (res[0], list(res[1:])) if nh else res[0]


def _rows(fn, rows, params, outs, accs, tile, name):
    length = rows[0][0].shape[0]
    tile = min(tile, length)
    assert length % tile == 0
    nr, npar, no = len(rows), len(params), len(outs)

    def body(*refs):
        vals = [r[...] for r in refs[:nr + npar]]
        o, a = fn(*vals)
        for ref, val in zip(refs[nr + npar:nr + npar + no], o):
            ref[...] = val.astype(ref.dtype)
        i = pl.program_id(0)
        for ref, val in zip(refs[nr + npar + no:], a):
            @pl.when(i == 0)
            def _(ref=ref, val=val):
                ref[...] = val

            @pl.when(i > 0)
            def _(ref=ref, val=val):
                ref[...] += val

    in_specs = [pl.BlockSpec((tile, w), functools.partial(lambda i, cb: (i, cb), cb=cb)) for _, cb, w in rows]
    in_specs += [pl.BlockSpec(p.shape, lambda i: (0, 0)) for p in params]
    outs = [o if len(o) == 3 else (o[0], o[1], o[0]) for o in outs]
    out_specs = [pl.BlockSpec((tile, w), lambda i: (i, 0)) for w, _, _ in outs]
    out_specs += [pl.BlockSpec((r, w), lambda i: (0, 0)) for r, w in accs]
    out_shape = [jax.ShapeDtypeStruct((length, full), dt) for _, dt, full in outs]
    out_shape += [jax.ShapeDtypeStruct((r, w), F32) for r, w in accs]
    res = pl.pallas_call(
        body, name=name, grid=(length // tile,), in_specs=in_specs, out_specs=out_specs, out_shape=out_shape,
        compiler_params=_params(1),
    )(*[r[0] for r in rows], *params)
    return res[:no], res[no:]


CONV_SUB = 32


def _conv_fwd(proj, conv_w, conv_b, name, hosted=None):
    length = proj.shape[0]
    tl, cw = min(1024, length), 512
    cb0 = _OFF["xs"] // cw
    grid = (CONV_DIM // cw, length // tl)
    h_arrays, h_bcast = hosted if hosted else ((), False)
    nh = len(h_arrays)

    def body(*refs):
        u_ref, halo_ref, w_ref, b_ref = refs[:4]
        o_ref = refs[4 + nh]
        win = refs[5 + 2 * nh]
        j, i = pl.program_id(0), pl.program_id(1)
        if nh:
            start, wait = _chip_exchange(refs[4:4 + nh], refs[5 + nh:5 + 2 * nh], refs[-2], refs[-1], h_bcast)

            @pl.when((j == 0) & (i == 0))
            def _():
                start()

            @pl.when((j == grid[0] - 1) & (i == grid[1] - 1))
            def _():
                wait()

        win[0:8, :] = jnp.where(i > 0, halo_ref[...], 0.0)
        win[8:8 + tl, :] = u_ref[...]
        taps = [w_ref[kk:kk + 1, :] for kk in range(SSD_CONV)]
        bias = b_ref[...]
        for r in range(0, tl, CONV_SUB):
            acc = bias + taps[0] * win[5 + r:5 + r + CONV_SUB, :]
            for kk in range(1, SSD_CONV):
                acc = acc + taps[kk] * win[5 + kk + r:5 + kk + r + CONV_SUB, :]
            o_ref[r:r + CONV_SUB, :] = acc

    any_spec = pl.BlockSpec(memory_space=pl.ANY)
    res = pl.pallas_call(
        body, name=name, grid=grid,
        in_specs=[pl.BlockSpec((tl, cw), lambda j, i: (i, cb0 + j)),
                  pl.BlockSpec((8, cw), lambda j, i: (jnp.maximum(i * (tl // 8) - 1, 0), cb0 + j)),
                  pl.BlockSpec((SSD_CONV, cw), lambda j, i: (0, j)),
                  pl.BlockSpec((1, cw), lambda j, i: (0, j))] + [any_spec] * nh,
        out_specs=[pl.BlockSpec((tl, cw), lambda j, i: (i, j))] + [any_spec] * nh,
        out_shape=[jax.ShapeDtypeStruct((length, CONV_DIM), F32)]
        + [jax.ShapeDtypeStruct(((4,) + s.shape) if h_bcast else s.shape, s.dtype) for s in h_arrays],
        scratch_shapes=[pltpu.VMEM((8 + tl, cw), F32)]
        + ([pltpu.SemaphoreType.DMA((4 * nh,)), pltpu.SemaphoreType.DMA((4 * nh,))] if nh else []),
        compiler_params=_params(2),
    )(proj, proj, conv_w, conv_b, *h_arrays)
    return (res[0], list(res[1:])) if nh else res[0]


def _conv_bwd(dpre, proj, conv_w, dproj, part, name):
    length = proj.shape[0]
    width = dpre.shape[1]
    tl, cw = min(1024, length), 512
    cb0 = _OFF["xs"] // cw + part
    n_t = length // tl

    def body(d_ref, dnext_ref, u_ref, halo_ref, w_ref, _, du_ref, dw_ref, db_ref, dwin, uwin):
        i = pl.program_id(1)
        dwin[0:tl, :] = d_ref[...].astype(F32)
        dwin[tl:tl + 8, :] = jnp.where(i < n_t - 1, dnext_ref[...].astype(F32)[0:8], 0.0)
        uwin[0:8, :] = jnp.where(i > 0, halo_ref[...], 0.0)
        uwin[8:8 + tl, :] = u_ref[...]
        taps = [w_ref[kk:kk + 1, :] for kk in range(SSD_CONV)]
        sub = CONV_SUB // 2
        acc_w = [jnp.zeros((sub, cw), F32) for _ in range(SSD_CONV)]
        acc_b = jnp.zeros((sub, cw), F32)
        for r in range(0, tl, sub):
            d = dwin[r:r + sub, :]
            du = taps[3] * d
            for kk in range(SSD_CONV - 1):
                du = du + taps[kk] * dwin[3 - kk + r:3 - kk + r + sub, :]
            du_ref[r:r + sub, :] = du.astype(du_ref.dtype)
            for kk in range(SSD_CONV):
                acc_w[kk] = acc_w[kk] + d * uwin[5 + kk + r:5 + kk + r + sub, :]
            acc_b = acc_b + d
        dw = jnp.concatenate([jnp.sum(a, axis=0, keepdims=True) for a in acc_w], axis=0)
        db = jnp.sum(acc_b, axis=0, keepdims=True)

        @pl.when(i == 0)
        def _():
            dw_ref[...] = dw
            db_ref[...] = db

        @pl.when(i > 0)
        def _():
            dw_ref[...] += dw
            db_ref[...] += db

    return pl.pallas_call(
        body, name=name, grid=(width // cw, n_t), input_output_aliases={5: 0},
        in_specs=[pl.BlockSpec((tl, cw), lambda j, i: (i, j)),
                  pl.BlockSpec((16, cw), lambda j, i: (jnp.minimum((i + 1) * (tl // 16), length // 16 - 1), j)),
                  pl.BlockSpec((tl, cw), lambda j, i: (i, cb0 + j)),
                  pl.BlockSpec((8, cw), lambda j, i: (jnp.maximum(i * (tl // 8) - 1, 0), cb0 + j)),
                  pl.BlockSpec((SSD_CONV, cw), lambda j, i: (0, part + j)),
                  pl.BlockSpec(memory_space=pl.ANY)],
        out_specs=[pl.BlockSpec((tl, cw), lambda j, i: (i, cb0 + j)),
                   pl.BlockSpec((SSD_CONV, cw), lambda j, i: (0, j)),
                   pl.BlockSpec((1, cw), lambda j, i: (0, j))],
        out_shape=[jax.ShapeDtypeStruct((length, NP), BF),
                   jax.ShapeDtypeStruct((SSD_CONV, width), F32),
                   jax.ShapeDtypeStruct((1, width), F32)],
        scratch_shapes=[pltpu.VMEM((tl + 8, cw), F32), pltpu.VMEM((8 + tl, cw), F32)],
        compiler_params=_params(2),
    )(dpre, dpre, proj, proj, conv_w, dproj)


def _ssd_in_specs(rev, nc):
    def cidx(c):
        return nc - 1 - c if rev else c
    whole = lambda c: (0, 0)
    return [pl.BlockSpec((BLK, SSD_D_INNER), lambda c: (cidx(c), 0)),
            pl.BlockSpec((BLK, BC_DIM), lambda c: (cidx(c), SSD_D_INNER // BC_DIM)),
            pl.BlockSpec((BLK, BC_DIM), lambda c: (cidx(c), SSD_D_INNER // BC_DIM + 1)),
            pl.BlockSpec((BLK, SSD_D_INNER), lambda c: (cidx(c), _OFF["z"] // SSD_D_INNER)),
            pl.BlockSpec((SSD_HEADS, BLK), lambda c: (0, cidx(c))),
            pl.BlockSpec((SSD_HEADS, 1), whole),
            pl.BlockSpec((SSD_HEADS, 1), whole),
            pl.BlockSpec((1, SSD_D_INNER), whole),
            pl.BlockSpec((1, SSD_D_INNER), whole)]


def _group_args(g, px, pb, pc, z, dtr, dtb, alog, dsk, nw):
    wide, narrow, heads = slice(512 * g, 512 * (g + 1)), slice(BLK * g, BLK * (g + 1)), slice(8 * g, 8 * (g + 1))
    return (px[:, wide], pb[:, narrow], pc[:, narrow], z[:, wide], dtr[heads, :], dtb[heads, :], alog[heads, :],
            dsk[:, wide], nw[:, wide])


def _ssd_fwd(pre, proj, dt_t, dtb, alog, dsk, nw, name, hosted=None):
    length = pre.shape[0]
    nc = length // BLK
    h_arrays, h_bcast = hosted if hosted else ((), False)
    nh = len(h_arrays)

    def body(*refs):
        px, pb, pc, z, dtr, dtb_r, al_r, dsk_r, nw_r = refs[:9]
        y_ref, sin_ref = refs[9 + nh:11 + nh]
        st = refs[11 + 2 * nh]
        c = pl.program_id(0)
        if nh:
            start, wait = _chip_exchange(refs[9:9 + nh], refs[11 + nh:11 + 2 * nh], refs[-2], refs[-1], h_bcast)

        @pl.when(c == 0)
        def _():
            st[...] = jnp.zeros_like(st)
            if nh:
                start()

        for g in range(SSD_GROUPS):
            s_in = st[g]
            sin_ref[g] = s_in
            y, s_out = _ssd_group(*_group_args(g, px, pb, pc, z, dtr, dtb_r, al_r, dsk_r, nw_r), s_in)
            y_ref[:, 512 * g:512 * (g + 1)] = y.astype(y_ref.dtype)
            st[g] = s_out

        if nh:
            @pl.when(c == nc - 1)
            def _():
                wait()

    any_spec = pl.BlockSpec(memory_space=pl.ANY)
    res = pl.pallas_call(
        body, name=name, grid=(nc,), in_specs=_ssd_in_specs(False, nc) + [any_spec] * nh,
        out_specs=[pl.BlockSpec((BLK, SSD_D_INNER), lambda c: (c, 0)),
                   pl.BlockSpec((SSD_GROUPS, None, SSD_STATE, 512), lambda c: (0, c, 0, 0))] + [any_spec] * nh,
        out_shape=[jax.ShapeDtypeStruct((length, SSD_D_INNER), BF),
                   jax.ShapeDtypeStruct((SSD_GROUPS, nc, SSD_STATE, 512), F32)]
        + [jax.ShapeDtypeStruct(((4,) + s.shape) if h_bcast else s.shape, s.dtype) for s in h_arrays],
        scratch_shapes=[pltpu.VMEM((SSD_GROUPS, SSD_STATE, 512), F32)]
        + ([pltpu.SemaphoreType.DMA((4 * nh,)), pltpu.SemaphoreType.DMA((4 * nh,))] if nh else []),
        compiler_params=_params(1),
    )(pre, pre, pre, proj, dt_t, dtb, alog, dsk, nw, *h_arrays)
    return (res[0], res[1], list(res[2:])) if nh else (res[0], res[1])


def _ssd_bwd(pre, proj, dt_t, dtb, alog, dsk, nw, s_in, dy, dproj, name, hosted=None):
    length = pre.shape[0]
    nc = length // BLK
    h_arrays, h_bcast = hosted if hosted else ((), False)
    nh = len(h_arrays)

    def body(*refs):
        px, pb, pc, z, dtr, dtb_r, al_r, dsk_r, nw_r, sin_r, dy_r = refs[:11]
        dpre, dz, ddt, ddtb, dal, ddsk, dnw = refs[12 + nh:19 + nh]
        dst = refs[19 + 2 * nh]
        c = pl.program_id(0)
        if nh:
            start, wait = _chip_exchange(refs[12:12 + nh], refs[19 + nh:19 + 2 * nh], refs[-2], refs[-1], h_bcast)

        @pl.when(c == 0)
        def _():
            dst[...] = jnp.zeros_like(dst)
            if nh:
                start()

        for grp in range(SSD_GROUPS):
            wide, narrow, heads = slice(512 * grp, 512 * (grp + 1)), slice(BLK * grp, BLK * (grp + 1)), slice(8 * grp, 8 * (grp + 1))
            _, vjp = jax.vjp(_ssd_group, *_group_args(grp, px, pb, pc, z, dtr, dtb_r, al_r, dsk_r, nw_r), sin_r[grp])
            g = vjp((dy_r[:, wide], dst[grp]))
            dpre[:, wide] = g[0].astype(dpre.dtype)
            dpre[:, slice(SSD_D_INNER + BLK * grp, SSD_D_INNER + BLK * (grp + 1))] = g[1].astype(dpre.dtype)
            dpre[:, slice(SSD_D_INNER + BC_DIM + BLK * grp, SSD_D_INNER + BC_DIM + BLK * (grp + 1))] = g[2].astype(dpre.dtype)
            dz[:, wide] = g[3].astype(dz.dtype)
            ddt[heads, :] = g[4]
            dst[grp] = g[9]
            for ref, val, idx in ((ddtb, g[5], (heads, slice(None))), (dal, g[6], (heads, slice(None))),
                                  (ddsk, g[7], (slice(None), wide)), (dnw, g[8], (slice(None), wide))):
                @pl.when(c == 0)
                def _(ref=ref, val=val, idx=idx):
                    ref[idx] = val

                @pl.when(c > 0)
                def _(ref=ref, val=val, idx=idx):
                    ref[idx] += val

        if nh:
            @pl.when(c == nc - 1)
            def _():
                wait()

    rev = lambda c: nc - 1 - c
    whole = lambda c: (0, 0)
    any_spec = pl.BlockSpec(memory_space=pl.ANY)
    in_specs = _ssd_in_specs(True, nc) + [
        pl.BlockSpec((SSD_GROUPS, None, SSD_STATE, 512), lambda c: (0, rev(c), 0, 0)),
        pl.BlockSpec((BLK, SSD_D_INNER), lambda c: (rev(c), 0)),
        any_spec] + [any_spec] * nh
    res = pl.pallas_call(
        body, name=name, grid=(nc,), in_specs=in_specs, input_output_aliases={11: 1},
        out_specs=[pl.BlockSpec((BLK, CONV_DIM), lambda c: (rev(c), 0)),
                   pl.BlockSpec((BLK, SSD_D_INNER), lambda c: (rev(c), _OFF["z"] // SSD_D_INNER)),
                   pl.BlockSpec((SSD_HEADS, BLK), lambda c: (0, rev(c))),
                   pl.BlockSpec((SSD_HEADS, 1), whole),
                   pl.BlockSpec((SSD_HEADS, 1), whole),
                   pl.BlockSpec((1, SSD_D_INNER), whole),
                   pl.BlockSpec((1, SSD_D_INNER), whole)] + [any_spec] * nh,
        out_shape=[jax.ShapeDtypeStruct((length, CONV_DIM), BF),
                   jax.ShapeDtypeStruct((length, NP), BF),
                   jax.ShapeDtypeStruct((SSD_HEADS, length), F32),
                   jax.ShapeDtypeStruct((SSD_HEADS, 1), F32),
                   jax.ShapeDtypeStruct((SSD_HEADS, 1), F32),
                   jax.ShapeDtypeStruct((1, SSD_D_INNER), F32),
                   jax.ShapeDtypeStruct((1, SSD_D_INNER), F32)]
        + [jax.ShapeDtypeStruct(((4,) + s.shape) if h_bcast else s.shape, s.dtype) for s in h_arrays],
        scratch_shapes=[pltpu.VMEM((SSD_GROUPS, SSD_STATE, 512), F32)]
        + ([pltpu.SemaphoreType.DMA((4 * nh,)), pltpu.SemaphoreType.DMA((4 * nh,))] if nh else []),
        compiler_params=_params(1),
    )(pre, pre, pre, proj, dt_t, dtb, alog, dsk, nw, s_in, dy, dproj, *h_arrays)
    return tuple(res[:7]) + ((list(res[7:]),) if nh else ())


assert _OFF["gates"] == 0 and _OFF["z"] % SSD_D_INNER == 0 and _OFF["q"] % D_MODEL == 0 and _OFF["k"] + 512 == NP


def _attn_in_specs():
    prev = lambda n: (0, jnp.maximum(n - 1, 0), 0)
    cur = lambda n: (0, n, 0)
    return [pl.BlockSpec((BLK, D_MODEL), lambda n: (n, _OFF["q"] // D_MODEL)),
            pl.BlockSpec((2, BLK, BLK), prev),
            pl.BlockSpec((2, BLK, BLK), cur),
            pl.BlockSpec((2, BLK, BLK), prev),
            pl.BlockSpec((2, BLK, BLK), cur),
            pl.BlockSpec((1, ATT_HEADS), lambda n: (0, 0))]


def _attn_fwd(proj, k2, v2, sinks, name, hosted=None):
    length = proj.shape[0]
    nb = length // BLK
    h_arrays, h_bcast = hosted if hosted else ((), False)
    nh = len(h_arrays)

    def body(*refs):
        q, kp, kc, vp, vc, sk = refs[:6]
        o_ref = refs[6 + nh]
        n = pl.program_id(0)
        if nh:
            start, wait = _chip_exchange(refs[6:6 + nh], refs[7 + nh:7 + 2 * nh], refs[-2], refs[-1], h_bcast)

            @pl.when(n == 0)
            def _():
                start()

        for kv in range(2):
            cols = slice(512 * kv, 512 * (kv + 1))
            o = _attn_block(q[:, cols], kp[kv], kc[kv], vp[kv], vc[kv], sk[...], n, kv)
            o_ref[:, cols] = o.astype(o_ref.dtype)

        if nh:
            @pl.when(n == nb - 1)
            def _():
                wait()

    any_spec = pl.BlockSpec(memory_space=pl.ANY)
    res = pl.pallas_call(
        body, name=name, grid=(nb,), in_specs=_attn_in_specs() + [any_spec] * nh,
        out_specs=[pl.BlockSpec((BLK, D_MODEL), lambda n: (n, 0))] + [any_spec] * nh,
        out_shape=[jax.ShapeDtypeStruct((length, D_MODEL), BF)]
        + [jax.ShapeDtypeStruct(((4,) + s.shape) if h_bcast else s.shape, s.dtype) for s in h_arrays],
        scratch_shapes=[pltpu.SemaphoreType.DMA((4 * nh,)), pltpu.SemaphoreType.DMA((4 * nh,))] if nh else [],
        compiler_params=_params(1),
    )(proj, k2, k2, v2, v2, sinks, *h_arrays)
    return (res[0], list(res[1:])) if nh else res[0]


def _attn_bwd(proj, k2, v2, sinks, datt, dproj, name):
    length = proj.shape[0]

    def body(q, kp, kc, vp, vc, sk, do, _, dq, dkp, dkc, dvp, dvc, dsk):
        n = pl.program_id(0)
        dsinks = None
        for kv in range(2):
            cols = slice(512 * kv, 512 * (kv + 1))
            _, vjp = jax.vjp(lambda *a: _attn_block(*a, n, kv), q[:, cols], kp[kv], kc[kv], vp[kv], vc[kv], sk[...])
            g = vjp(do[:, cols].astype(F32))
            dq[:, cols] = g[0].astype(dq.dtype)
            dkp[kv] = g[1]
            dkc[kv] = g[2]
            dvp[kv] = g[3]
            dvc[kv] = g[4]
            dsinks = g[5] if dsinks is None else dsinks + g[5]

        @pl.when(n == 0)
        def _():
            dsk[...] = dsinks

        @pl.when(n > 0)
        def _():
            dsk[...] += dsinks

    blk3 = pl.BlockSpec((2, BLK, BLK), lambda n: (0, n, 0))
    kv_shape = jax.ShapeDtypeStruct((2, length, BLK), F32)
    return pl.pallas_call(
        body, name=name, grid=(length // BLK,),
        in_specs=_attn_in_specs() + [pl.BlockSpec((BLK, D_MODEL), lambda n: (n, 0)), pl.BlockSpec(memory_space=pl.ANY)],
        out_specs=[pl.BlockSpec((BLK, D_MODEL), lambda n: (n, _OFF["q"] // D_MODEL)), blk3, blk3, blk3, blk3,
                   pl.BlockSpec((1, ATT_HEADS), lambda n: (0, 0))],
        out_shape=[jax.ShapeDtypeStruct((length, NP), BF), kv_shape, kv_shape, kv_shape, kv_shape,
                   jax.ShapeDtypeStruct((1, ATT_HEADS), F32)],
        input_output_aliases={7: 0},
        compiler_params=_params(1),
    )(proj, k2, k2, v2, v2, sinks, datt, dproj)


DMA_CHUNK_BYTES = 1 << 20
N_STAGE = 8
LOOKAHEAD = 3


def _piece_chunks(shape, itemsize):
    if len(shape) < 2 or shape[-2] % 16 != 0:
        return [()]
    rows, cols = shape[-2:]
    step = min(rows, max(16, DMA_CHUNK_BYTES // (cols * itemsize) // 16 * 16))
    out = []
    for lead in itertools.product(*[range(d) for d in shape[:-2]]):
        for r0 in range(0, rows, step):
            out.append(lead + (pl.ds(r0, min(step, rows - r0)),))
    return out


def _exchange(srcs, group, bcast, name):
    size = {"c": 2, "xy": 4, "all": 8}[group]
    npeer = size - 1
    n = len(srcs)
    for s in srcs:
        assert bcast or s.shape[0] == size
    pieces = [s.shape if bcast else s.shape[1:] for s in srcs]
    chunks = [_piece_chunks(p, s.dtype.itemsize) for p, s in zip(pieces, srcs)]

    def body(*refs):
        src, out = refs[:n], refs[n:2 * n]
        load_sems, send_sems, recv_sems = refs[2 * n:]
        x, y, c = lax.axis_index("x"), lax.axis_index("y"), lax.axis_index("c")
        if group == "c":
            me = c
        elif group == "xy":
            me = 2 * x + y
        else:
            me = 4 * x + 2 * y + c

        def device(j):
            if group == "c":
                return (x, y, j)
            if group == "xy":
                return (j // 2, j % 2, c)
            return (j // 4, (j // 2) % 2, j % 2)

        def part(ref, idx):
            return ref.at[idx] if idx else ref

        def piece(a, j):
            return src[a] if bcast else src[a].at[j]

        for a in range(n):
            jobs = [(None, idx) for idx in chunks[a]] if bcast else [(d, idx) for idx in chunks[a] for d in range(1, size)]
            full_rows = chunks[a][0][-1].size if chunks[a][0] else None
            slot_shape = (full_rows, pieces[a][-1]) if chunks[a][0] else pieces[a]

            def stream(stage, a=a, jobs=jobs, full_rows=full_rows):
                def slot(q):
                    idx = jobs[q][1]
                    view = stage.at[q % N_STAGE]
                    if idx and idx[-1].size != full_rows:
                        view = view.at[pl.ds(0, idx[-1].size)]
                    return view

                def load(q):
                    d, idx = jobs[q]
                    j = me if d is None else jnp.bitwise_xor(me, d)
                    return pltpu.make_async_copy(part(piece(a, j), idx), slot(q), load_sems.at[q % N_STAGE])

                def sends(q):
                    d, idx = jobs[q]
                    return [pltpu.make_async_remote_copy(
                        src_ref=slot(q), dst_ref=part(out[a].at[me], idx),
                        send_sem=send_sems.at[(q % N_STAGE) * npeer + dd - 1], recv_sem=recv_sems.at[a * size + dd],
                        device_id=device(jnp.bitwise_xor(me, dd)), device_id_type=pl.DeviceIdType.MESH)
                        for dd in (range(1, size) if d is None else (d,))]

                nq = len(jobs)
                for q in range(nq + LOOKAHEAD):
                    if q < nq:
                        if q >= N_STAGE:
                            for cp in sends(q - N_STAGE):
                                cp.wait_send()
                        load(q).start()
                    if q >= LOOKAHEAD:
                        load(q - LOOKAHEAD).wait()
                        for cp in sends(q - LOOKAHEAD):
                            cp.start()
                for q in range(max(0, nq - N_STAGE), nq):
                    for cp in sends(q):
                        cp.wait_send()

            pl.run_scoped(stream, pltpu.VMEM((N_STAGE,) + tuple(slot_shape), srcs[a].dtype))

        for a in range(n):
            for d in range(1, size):
                j = jnp.bitwise_xor(me, d)
                pltpu.make_async_remote_copy(
                    src_ref=piece(a, j), dst_ref=out[a].at[j], send_sem=send_sems.at[0], recv_sem=recv_sems.at[a * size + d],
                    device_id=device(j), device_id_type=pl.DeviceIdType.MESH).wait_recv()

    any_spec = pl.BlockSpec(memory_space=pl.ANY)
    return pl.pallas_call(
        body, name=name, in_specs=[any_spec] * n, out_specs=[any_spec] * n,
        out_shape=[jax.ShapeDtypeStruct((size,) + tuple(p), s.dtype) for p, s in zip(pieces, srcs)],
        scratch_shapes=[pltpu.SemaphoreType.DMA((N_STAGE,)), pltpu.SemaphoreType.DMA((N_STAGE * npeer,)),
                        pltpu.SemaphoreType.DMA((n * size,))],
        compiler_params=pltpu.CompilerParams(vmem_limit_bytes=VMEM_LIMIT_BYTES),
    )(*srcs)


def _with_own(outs, owns, me):
    return [lax.dynamic_update_index_in_dim(o, w, me, 0) for o, w in zip(outs, owns)]


def _sum_slots(arr, name):
    k = arr.shape[0]
    rest = arr.shape[1:]
    width = rest[-1]
    rows_per = math.prod(rest[:-1])
    a2 = arr.reshape(k * rows_per, width)
    tile = rows_per
    for cand in (256, 128, 64, 32, 16, 8):
        if rows_per % cand == 0:
            tile = cand
            break
    nt = rows_per // tile

    def body(*refs):
        acc = refs[0][...]
        for r in refs[1:k]:
            acc = acc + r[...]
        refs[k][...] = acc

    in_specs = [pl.BlockSpec((tile, width), functools.partial(lambda i, s: (s * nt + i, 0), s=s)) for s in range(k)]
    out = pl.pallas_call(
        body, name=name, grid=(nt,), in_specs=in_specs, out_specs=pl.BlockSpec((tile, width), lambda i: (i, 0)),
        out_shape=jax.ShapeDtypeStruct((rows_per, width), arr.dtype), compiler_params=_params(1),
    )(*([a2] * k))
    return out.reshape(rest)


def _sum_pieces(recv, own, own_slotted, me, name):
    k = recv.shape[0]
    rest = recv.shape[1:]
    width = rest[-1]
    rows_per = math.prod(rest[:-1])
    r2 = recv.reshape(k * rows_per, width)
    o2 = own.reshape(-1, width)
    tile = rows_per
    for cand in (256, 128, 64, 32, 16, 8):
        if rows_per % cand == 0:
            tile = cand
            break
    nt = rows_per // tile

    def body(me_ref, *refs):
        mine = refs[k][...].astype(F32)
        acc = None
        for j in range(k):
            term = jnp.where(me_ref[0] == j, mine, refs[j][...].astype(F32))
            acc = term if acc is None else acc + term
        refs[k + 1][...] = acc
        refs[k + 2][...] = acc.astype(BF)

    in_specs = [pl.BlockSpec((tile, width), functools.partial(lambda i, m, s: (s * nt + i, 0), s=s)) for s in range(k)]
    in_specs.append(pl.BlockSpec((tile, width), (lambda i, m: (m[0] * nt + i, 0)) if own_slotted else (lambda i, m: (i, 0))))
    out_spec = pl.BlockSpec((tile, width), lambda i, m: (i, 0))
    out, out_bf = pl.pallas_call(
        body, name=name,
        grid_spec=pltpu.PrefetchScalarGridSpec(num_scalar_prefetch=1, grid=(nt,), in_specs=in_specs, out_specs=[out_spec, out_spec]),
        out_shape=[jax.ShapeDtypeStruct((rows_per, width), F32), jax.ShapeDtypeStruct((rows_per, width), BF)],
        compiler_params=_params(1),
    )(jnp.reshape(me, (1,)).astype(jnp.int32), *([r2] * k), o2)
    return out.reshape(rest), out_bf.reshape(rest)


def _adamw(w, g, m, v, name):
    def fn(w_, g_, m_, v_):
        m1 = ADAM_B1 * m_ + (1.0 - ADAM_B1) * g_
        v1 = ADAM_B2 * v_ + (1.0 - ADAM_B2) * (g_ * g_)
        m_hat = m1 / (1.0 - ADAM_B1 ** ADAM_STEP)
        v_hat = v1 / (1.0 - ADAM_B2 ** ADAM_STEP)
        delta = -ADAM_LR * (m_hat / (jnp.sqrt(v_hat) + ADAM_EPS) + ADAM_WD * w_)
        return (delta, m1, v1), ()

    rows, width = w.shape
    tile = rows
    for cand in (256, 128, 64, 32, 16, 8):
        if rows % cand == 0:
            tile = cand
            break
    o, _ = _rows(fn, [(w, 0, width), (g, 0, width), (m, 0, width), (v, 0, width)], [],
                 [(width, F32)] * 3, [], tile, name)
    return o


def _col_pieces(start, width, shard):
    out = []
    while width:
        chip, off = divmod(start, shard)
        take = min(width, shard - off)
        out.append((chip, off, take))
        start, width = start + take, width - take
    return out


def _gather_cols(parts, ranges, pad=0):
    shard = parts.shape[3]
    halves = []
    for h in range(2):
        cols = [parts[h, chip, :, off:off + w] for s, wd in ranges for chip, off, w in _col_pieces(s, wd, shard)]
        if pad:
            cols.append(jnp.zeros((parts.shape[2], pad), parts.dtype))
        halves.append(jnp.concatenate(cols, axis=1))
    return jnp.concatenate(halves, axis=0)


def _scatter_cols(g, ranges, shard):
    starts, pos = {}, 0
    for s, wd in ranges:
        starts[s] = (pos, wd)
        pos += wd
    order = sorted(starts)
    hr = g.shape[0] // 2
    out = []
    for h in range(2):
        per_chip = []
        for chip in range(4):
            cols = []
            for s in order:
                p0, wd = starts[s]
                lo, hi = max(s, chip * shard), min(s + wd, (chip + 1) * shard)
                if lo < hi:
                    cols.append(g[h * hr:(h + 1) * hr, p0 + lo - s:p0 + hi - s])
            per_chip.append(jnp.concatenate(cols, axis=1))
        out.append(jnp.stack(per_chip))
    return jnp.stack(out)


_IN_RANGES = tuple((o, wd) for _, o, wd in _PACK)


def _double_heads(t):
    length = t.shape[0]
    h = jnp.transpose(t.reshape(length, 2, 64), (1, 0, 2))
    return jnp.concatenate([h, h], axis=-1)


def _fold_heads(d_cur, d_prev):
    length = d_cur.shape[1]
    d = d_cur + jnp.concatenate([d_prev[:, BLK:], jnp.zeros((2, BLK, BLK), F32)], axis=1)
    d = d[..., :64] + d[..., 64:]
    return jnp.transpose(d, (1, 0, 2)).reshape(length, 128)


def _ln_fwd(a, r, g, b, name):
    def fn(a_, r_, g_, b_):
        y = _layer_norm(ALPHA * a_ + r_, g_, b_)
        return (y, y), ()
    o, _ = _rows(fn, [(a, 0, D_MODEL), (r, 0, D_MODEL)], [g, b], [(D_MODEL, F32), (D_MODEL, BF)], [], ROW_TILE, name)
    return o


def _ln_bwd(a, r, d1, d2, g, b, name):
    def fn(a_, r_, d1_, d2_, g_, b_):
        _, vjp = jax.vjp(_layer_norm, ALPHA * a_ + r_, g_, b_)
        ds, dg, db = vjp(ALPHA * d1_ + d2_)
        return (ds, ds), (dg, db)
    o, acc = _rows(fn, [(a, 0, D_MODEL), (r, 0, D_MODEL), (d1, 0, D_MODEL), (d2, 0, D_MODEL)], [g, b],
                   [(D_MODEL, F32), (D_MODEL, BF)], [(1, D_MODEL), (1, D_MODEL)], ROW_TILE, name)
    return o[0], o[1], acc[0], acc[1]


def _layer_fwd(l, h, hb, w, p, hosted=None, hosted_attn=None, hosted_conv=None, late_weights=None):
    sv = {"h": h, "hb": hb}
    proj = _mm(hb, w["win"], "nn", F32, 4096, 512, 1024, f"proj{l}")
    sv["proj"] = proj
    pre = _conv_fwd(proj, p["conv_w"], p["conv_b"], f"conv{l}", hosted_conv)
    if hosted_conv:
        pre, got = pre
        w.update(late_weights(got))
    sv["pre"] = pre
    dt_t = jnp.transpose(proj[:, _OFF["dt"]:_OFF["dt"] + SSD_HEADS])
    sv["dt_t"] = dt_t
    res = _ssd_fwd(pre, proj, dt_t, p["dt_bias"], p["a_log"], p["d_skip_c"], p["ssd_norm_w"], f"ssd{l}", hosted)
    yn, s_in = res[:2]
    if hosted:
        sv["hosted"] = res[2]
    sv["yn"], sv["s_in"] = yn, s_in
    ya = _mm(yn, w["wso"], "nn", BF, 1024, 1024, 2048, f"ssdout{l}")
    k2 = _double_heads(proj[:, _OFF["k"]:_OFF["k"] + 128])
    v2 = _double_heads(proj[:, _OFF["v"]:_OFF["v"] + 128])
    sv["k2"], sv["v2"] = k2, v2
    att = _attn_fwd(proj, k2, v2, p["att_sinks"], f"attn{l}", hosted_attn)
    if hosted_attn:
        att, sv["hosted_attn"] = att
    sv["att"] = att
    yb = _mm(att, w["wao"], "nn", BF, 1024, 1024, 1024, f"attout{l}")
    sv["ya"], sv["yb"] = ya, yb
    gcb = _OFF["gates"] // 1024

    def gate_fn(ga, gb, ya_, yb_):
        return (_sigmoid(ga) * ya_.astype(F32) + _sigmoid(gb) * yb_.astype(F32),), ()
    (ub,), _ = _rows(gate_fn, [(proj, gcb, 1024), (proj, gcb + 1, 1024), (ya, 0, 1024), (yb, 0, 1024)], [],
                     [(D_MODEL, BF)], [], ROW_TILE, f"gate{l}")
    sv["ub"] = ub
    mix = _mm(ub, w["wmo"], "nn", F32, 1024, 1024, 1024, f"mixout{l}")
    sv["mix"] = mix
    h1, h1b = _ln_fwd(h, mix, p["ln_mix_g"], p["ln_mix_b"], f"lnmix{l}")
    sv["h1"], sv["h1b"] = h1, h1b
    gu = _mm(h1b, w["wgu"], "nn", BF, 4096, 512, 1024, f"ffnin{l}")
    sv["gu"] = gu

    def act_fn(g_, u_):
        return (_silu(g_.astype(F32)) * u_.astype(F32),), ()
    (act,), _ = _rows(act_fn, [(gu, 0, FFN_HIDDEN), (gu, 1, FFN_HIDDEN)], [], [(FFN_HIDDEN, BF)], [], ROW_TILE, f"swiglu{l}")
    sv["act"] = act
    ffn = _mm(act, w["wd"], "nn", F32, 1024, 1024, FFN_HIDDEN, f"ffnout{l}")
    sv["ffn"] = ffn
    return sv


def _layer_bwd(l, sv, w, p, ds_f, ds_b, hosted=(), early_hook=None, late_hook=None):
    g = {}
    g["w_ffn_down"] = _mm(sv["act"], ds_b, "tn", F32, 1408, 1024, 2048, f"dwd{l}")
    dact = _mm(ds_b, w["wd"], "nt", BF, 1024, FFN_HIDDEN, 1024, f"dact{l}")

    def act_bwd(g_, u_, d_):
        _, vjp = jax.vjp(lambda a, b: _silu(a) * b, g_.astype(F32), u_.astype(F32))
        dg_, du_ = vjp(d_.astype(F32))
        return (jnp.concatenate([dg_, du_], axis=1),), ()
    (dgu,), _ = _rows(act_bwd, [(sv["gu"], 0, FFN_HIDDEN), (sv["gu"], 1, FFN_HIDDEN), (dact, 0, FFN_HIDDEN)], [],
                      [(2 * FFN_HIDDEN, BF)], [], ROW_TILE // 2, f"dswiglu{l}")
    dwgu = _mm(sv["h1b"], dgu, "tn", F32, 1024, 1408, 2048, f"dwgu{l}")
    g["w_ffn_gate"], g["w_ffn_up"] = dwgu[:, :FFN_HIDDEN], dwgu[:, FFN_HIDDEN:]
    dh1 = _mm(dgu, w["wgu"], "nt", F32, 1024, 1024, 2816, f"dh1{l}")
    ds2_f, ds2_b, g["ln_mix_g"], g["ln_mix_b"] = _ln_bwd(sv["h"], sv["mix"], ds_f, dh1, p["ln_mix_g"], p["ln_mix_b"], f"dlnmix{l}")
    g["w_mix_out"] = _mm(sv["ub"], ds2_b, "tn", F32, 1024, 1024, 2048, f"dwmo{l}")
    du = _mm(ds2_b, w["wmo"], "nt", BF, 1024, 1024, 1024, f"du{l}")
    proj = sv["proj"]
    gcb = _OFF["gates"] // 1024

    def gate_bwd(ga, gb, ya_, yb_, du_):
        _, vjp = jax.vjp(lambda a, b, c, d: _sigmoid(a) * c + _sigmoid(b) * d, ga, gb, ya_.astype(F32), yb_.astype(F32))
        dga, dgb, dya, dyb = vjp(du_.astype(F32))
        return (jnp.concatenate([dga, dgb], axis=1), dya, dyb), ()
    (dproj, dya, dyb), _ = _rows(
        gate_bwd, [(proj, gcb, 1024), (proj, gcb + 1, 1024), (sv["ya"], 0, 1024), (sv["yb"], 0, 1024), (du, 0, 1024)], [],
        [(2048, BF, NP), (D_MODEL, BF), (D_MODEL, BF)], [], ROW_TILE, f"dgate{l}")
    g["w_att_out"] = _mm(sv["att"], dyb, "tn", F32, 1024, 1024, 2048, f"dwao{l}")
    datt = _mm(dyb, w["wao"], "nt", BF, 1024, 1024, 1024, f"datt{l}")
    g["w_ssd_out"] = _mm(sv["yn"], dya, "tn", F32, 1024, 1024, 2048, f"dwso{l}")
    dyn = _mm(dya, w["wso"], "nt", F32, 1024, 1024, 1024, f"dyn{l}")
    hosted = list(hosted) + (list(early_hook(g)) if early_hook else [])
    dproj, dkp, dkc, dvp, dvc, g["att_sinks"] = _attn_bwd(proj, sv["k2"], sv["v2"], p["att_sinks"], datt, dproj, f"dattn{l}")
    res = _ssd_bwd(sv["pre"], proj, sv["dt_t"], p["dt_bias"], p["a_log"], p["d_skip_c"], p["ssd_norm_w"], sv["s_in"], dyn,
                   dproj, f"dssd{l}", (hosted, False) if hosted else None)
    (dpre, dproj, ddt_t, g["dt_bias"], g["a_log"], ddsk, g["ssd_norm_w"]) = res[:7]
    if hosted:
        g["hosted"] = res[7]
    g["d_skip"] = jnp.sum(ddsk.reshape(SSD_HEADS, 64), axis=1)
    dproj, g["conv_w"], g["conv_b"] = _conv_bwd(dpre, proj, p["conv_w"], dproj, 0, f"dconv{l}")
    length = proj.shape[0]
    tail = jnp.concatenate([_fold_heads(dkc, dkp).astype(BF), _fold_heads(dvc, dvp).astype(BF),
                            jnp.transpose(ddt_t).astype(BF), jnp.zeros((length, _PAD), BF)], axis=1)
    dproj = lax.dynamic_update_slice(dproj, tail, (0, _OFF["k"]))
    g["w_in"] = _mm(sv["hb"], dproj, "tn", F32, 1024, 2176, 1024, f"dwin{l}")
    late = list(late_hook(g)) if late_hook else []
    dh = _mm(dproj, w["win"], "nt", F32, 1024, 1024, 2176, f"dh{l}", (late, False) if late else None)
    if late:
        dh, g["late_hosted"] = dh
    return g, ds2_f, dh


def kernel(x, ln_in_g, ln_in_b, w_in, conv_w, conv_b, dt_bias, a_log, d_skip, ssd_norm_w, att_sinks, w_ssd_out, w_att_out, w_mix_out, ln_mix_g, ln_mix_b, w_ffn_gate, w_ffn_up, w_ffn_down, ln_ffn_g, ln_ffn_b, loss_target, m_ln_in_g, m_ln_in_b, m_w_in, m_conv_w, m_conv_b, m_dt_bias, m_a_log, m_d_skip, m_ssd_norm_w, m_att_sinks, m_w_ssd_out, m_w_att_out, m_w_mix_out, m_ln_mix_g, m_ln_mix_b, m_w_ffn_gate, m_w_ffn_up, m_w_ffn_down, m_ln_ffn_g, m_ln_ffn_b, v_ln_in_g, v_ln_in_b, v_w_in, v_conv_w, v_conv_b, v_dt_bias, v_a_log, v_d_skip, v_ssd_norm_w, v_att_sinks, v_w_ssd_out, v_w_att_out, v_w_mix_out, v_ln_mix_g, v_ln_mix_b, v_w_ffn_gate, v_w_ffn_up, v_w_ffn_down, v_ln_ffn_g, v_ln_ffn_b):
    env = dict(locals())
    wts = {n: env[n] for n in WEIGHTS}
    mom1 = {n: env["m_" + n] for n in WEIGHTS}
    mom2 = {n: env["v_" + n] for n in WEIGHTS}
    ci = lax.axis_index("c")
    chip = 2 * lax.axis_index("x") + lax.axis_index("y")
    xs_ = x[0]
    tgt = loss_target[0]

    COLS = ("w_in", "w_ffn_gate", "w_ffn_up")

    def half(t):
        return lax.dynamic_slice_in_dim(t, ci * (t.shape[0] // 2), t.shape[0] // 2, axis=0)

    def assemble(name, parts):
        if name == "w_in":
            return _gather_cols(parts, _IN_RANGES, _PAD)
        if name in COLS:
            return _gather_cols(parts, ((0, 4 * parts.shape[3]),))
        return jnp.concatenate([parts[h, s] for s in range(4) for h in range(2)], axis=0)

    def split(name, gfull):
        if name == "w_in":
            return _scatter_cols(gfull, _IN_RANGES, IN_DIM // 4)
        if name in COLS:
            return _scatter_cols(gfull, ((0, gfull.shape[1]),), gfull.shape[1] // 4)
        hr = gfull.shape[0] // 8
        return jnp.stack([jnp.stack([gfull[(2 * s + h) * hr:(2 * s + h + 1) * hr] for s in range(4)]) for h in range(2)])

    mine = [[half(wts[n][l].astype(BF)) for n in BIG] for l in range(DEPTH)]
    conv_all, = _with_own(_exchange([conv_w], "xy", True, "gather_conv"), [conv_w], chip)
    conv_full = jnp.transpose(conv_all, (1, 2, 0, 3)).reshape(DEPTH, SSD_CONV, CONV_DIM)

    KEYS = {"w_in": "win", "w_ssd_out": "wso", "w_att_out": "wao", "w_mix_out": "wmo", "w_ffn_down": "wd"}

    def layer_weights(l, names, by_chip):
        own = [mine[l][BIG.index(n)] for n in names]
        by_chip = _with_own(by_chip, own, chip)
        parts = _with_own(_exchange(by_chip, "c", True, f"gather_cores{l}_{names[0]}"), by_chip, ci)
        fw = {n: assemble(n, part) for n, part in zip(names, parts)}
        w = {KEYS[n]: fw[n] for n in names if n in KEYS}
        if "w_ffn_gate" in fw:
            w["wgu"] = jnp.concatenate([fw["w_ffn_gate"], fw["w_ffn_up"]], axis=1)
        return w

    layer_p = []
    for l in range(DEPTH):
        layer_p.append({
            "conv_w": conv_full[l], "conv_b": conv_b[l][None],
            "dt_bias": dt_bias[l][:, None], "a_log": a_log[l][:, None],
            "d_skip_c": jnp.repeat(d_skip[l], 64)[None], "ssd_norm_w": ssd_norm_w[l][None],
            "att_sinks": att_sinks[l][None], "ln_mix_g": ln_mix_g[l][None], "ln_mix_b": ln_mix_b[l][None],
            "ln_ffn_g": ln_ffn_g[l][None], "ln_ffn_b": ln_ffn_b[l][None],
        })

    def ln_in_fn(x_, g_, b_):
        y = _layer_norm(x_, g_, b_)
        return (y, y), ()
    (h, hb), _ = _rows(ln_in_fn, [(xs_, 0, D_MODEL)], [ln_in_g[None], ln_in_b[None]], [(D_MODEL, F32), (D_MODEL, BF)], [], ROW_TILE, "ln_in")
    rest = tuple(n for n in BIG if n != "w_in")
    layer_w = [layer_weights(0, ("w_in",), _exchange([mine[0][BIG.index("w_in")]], "xy", True, "gather_chips0"))]
    behind_ssd = [i for i, n in enumerate(BIG) if n in COLS]
    behind_attn = [i for i, n in enumerate(BIG) if n not in COLS]
    saved = []
    for l in range(DEPTH):
        nxt = l + 1 < DEPTH
        sv = _layer_fwd(l, h, hb, layer_w[l], layer_p[l], ([mine[l + 1][i] for i in behind_ssd], True) if nxt else None,
                        ([mine[l + 1][i] for i in behind_attn], True) if nxt else None,
                        ([mine[l][BIG.index(n)] for n in rest], True) if l == 0 else None,
                        functools.partial(layer_weights, l, rest))
        saved.append(sv)
        if nxt:
            got = dict(zip(behind_ssd + behind_attn, sv["hosted"] + sv["hosted_attn"]))
            layer_w.append(layer_weights(l + 1, BIG, [got[i] for i in range(len(BIG))]))
            h, hb = _ln_fwd(sv["h1"], sv["ffn"], layer_p[l]["ln_ffn_g"], layer_p[l]["ln_ffn_b"], f"lnffn{l}")

    def loss_fn(a_, r_, t_, g_, b_):
        y, vjp = jax.vjp(_layer_norm, ALPHA * a_ + r_, g_, b_)
        err = y - t_
        ds, dg, db = vjp(err * (1.0 / D_MODEL))
        part = 0.5 * jnp.sum(jnp.mean(err * err, axis=-1, keepdims=True), axis=0, keepdims=True)
        return (ds, ds), (dg, db, jnp.broadcast_to(part, (1, BLK)))
    sv = saved[-1]
    (ds_f, ds_b), (dg_last, db_last, loss_part) = _rows(
        loss_fn, [(sv["h1"], 0, D_MODEL), (sv["ffn"], 0, D_MODEL), (tgt, 0, D_MODEL)],
        [layer_p[-1]["ln_ffn_g"], layer_p[-1]["ln_ffn_b"]], [(D_MODEL, F32), (D_MODEL, BF)],
        [(1, D_MODEL), (1, D_MODEL), (1, BLK)], ROW_TILE, "loss")
    loss = lax.psum(loss_part[0, 0], ("x", "y", "c"))

    def reduce_cores(l, g, names):
        src = [split(n, g[n]) for n in names]
        recv = _exchange([s.astype(BF) for s in src], "c", False, f"reduce_cores{l}_{names[0]}")
        return {n: _sum_pieces(r, o, True, ci, f"sum_cores{l}_{n}") for n, r, o in zip(names, recv, src)}

    def bf16_of(sums, names):
        return [sums[n][1] for n in names]

    def add_chips(l, sums, from_chips):
        return [_sum_pieces(from_chips[n], sums[n][0], True, chip, f"sum_chips{l}_{n}")[0] for n in BIG]

    EARLY = tuple(n for n in BIG if n != "w_in")
    grads = [None] * DEPTH
    chip_sums = [None] * DEPTH
    for l in reversed(range(DEPTH)):
        sv = saved[l]
        if l == DEPTH - 1:
            lg, lb = dg_last, db_last
        else:
            ds_f, ds_b, lg, lb = _ln_bwd(sv["h1"], sv["ffn"], d1, d2, layer_p[l]["ln_ffn_g"], layer_p[l]["ln_ffn_b"], f"dlnffn{l}")
        sums = {}

        def early_hook(g, l=l, sums=sums):
            sums.update(reduce_cores(l, g, EARLY))
            return bf16_of(sums, EARLY)

        def late_hook(g, l=l, sums=sums):
            sums.update(reduce_cores(l, g, ("w_in",)))
            return bf16_of(sums, ("w_in",))
        if l == 0:
            g, d1, d2 = _layer_bwd(l, sv, layer_w[l], layer_p[l], ds_f, ds_b, bf16_of(chip_sums[l + 1], BIG), early_hook, late_hook)
        else:
            g, d1, d2 = _layer_bwd(l, sv, layer_w[l], layer_p[l], ds_f, ds_b)
            sums.update(reduce_cores(l, g, BIG))
        g["ln_ffn_g"], g["ln_ffn_b"] = lg, lb
        grads[l] = g
        chip_sums[l] = sums

    def ln_in_bwd(x_, d1_, d2_, g_, b_):
        _, vjp = jax.vjp(_layer_norm, x_, g_, b_)
        dx, dg, db = vjp(ALPHA * d1_ + d2_)
        return (dx,), (dg, db)
    (grad_x,), (g_ln_in_g, g_ln_in_b) = _rows(
        ln_in_bwd, [(xs_, 0, D_MODEL), (d1, 0, D_MODEL), (d2, 0, D_MODEL)], [ln_in_g[None], ln_in_b[None]],
        [(D_MODEL, F32)], [(1, D_MODEL), (1, D_MODEL)], ROW_TILE, "dln_in")

    carried = grads[0]["hosted"]
    from_chips = dict(zip(EARLY, carried[len(BIG):]))
    from_chips["w_in"], = grads[0]["late_hosted"]
    reduced = add_chips(0, chip_sums[0], from_chips) + add_chips(1, chip_sums[1], dict(zip(BIG, carried[:len(BIG)])))
    halves = _with_own(_exchange(reduced, "c", True, "swap_halves"), reduced, ci)
    shards = [t.reshape(2 * t.shape[1], t.shape[2]) for t in halves]
    big_grad = {n: jnp.stack([shards[l * len(BIG) + i] for l in range(DEPTH)]).reshape(wts[n].shape) for i, n in enumerate(BIG)}

    small_g = {"ln_in_g": g_ln_in_g[0], "ln_in_b": g_ln_in_b[0]}
    for n in SMALL[2:]:
        small_g[n] = jnp.stack([grads[l][n].reshape(wts[n].shape[1:] if n != "conv_w" else (SSD_CONV, CONV_DIM)) for l in range(DEPTH)])
    flat = jnp.concatenate([small_g[n].reshape(-1) for n in SMALL])
    n_small = flat.shape[0]
    width = -(-n_small // 1024) * 1024
    flat = jnp.pad(flat, (0, width - n_small)).reshape(8, width // 8)
    gathered, = _with_own(_exchange([flat], "all", True, "gather_small"), [flat], 2 * chip + ci)
    total = _sum_slots(gathered, "sum_small").reshape(-1)
    small_grad, off = {}, 0
    for n in SMALL:
        shp = small_g[n].shape
        cnt = math.prod(shp)
        small_grad[n] = total[off:off + cnt].reshape(shp)
        off += cnt
    small_grad["conv_w"] = lax.dynamic_slice_in_dim(small_grad["conv_w"], chip * (CONV_DIM // 4), CONV_DIM // 4, axis=2)

    out_g, out_d, out_m, out_v = {}, {}, {}, {}
    for n in BIG:
        shp = wts[n].shape
        two_d = lambda t: t.reshape(shp[0] * shp[1], shp[2])
        d_, m_, v_ = _adamw(two_d(wts[n]), two_d(big_grad[n]), two_d(mom1[n]), two_d(mom2[n]), f"adamw_{n}")
        out_g[n], out_d[n], out_m[n], out_v[n] = big_grad[n], d_.reshape(shp), m_.reshape(shp), v_.reshape(shp)

    def flat_small(d):
        f = jnp.concatenate([d[n].reshape(-1) for n in SMALL])
        return jnp.pad(f, (0, swidth - f.shape[0])).reshape(8, swidth // 8)
    n_sw = sum(math.prod(wts[n].shape) for n in SMALL)
    swidth = -(-n_sw // 1024) * 1024
    d_, m_, v_ = _adamw(flat_small(wts), flat_small(small_grad), flat_small(mom1), flat_small(mom2), "adamw_small")
    off = 0
    for n in SMALL:
        shp = wts[n].shape
        cnt = math.prod(shp)
        out_g[n] = small_grad[n]
        out_d[n] = d_.reshape(-1)[off:off + cnt].reshape(shp)
        out_m[n] = m_.reshape(-1)[off:off + cnt].reshape(shp)
        out_v[n] = v_.reshape(-1)[off:off + cnt].reshape(shp)
        off += cnt

    return (loss, grad_x[None], *[out_g[n] for n in WEIGHTS], *[out_d[n] for n in WEIGHTS],
            *[out_m[n] for n in WEIGHTS], *[out_v[n] for n in WEIGHTS])
```

```python
import functools
import itertools
import math

import jax
import jax.numpy as jnp
from jax import lax
from jax.experimental import pallas as pl
from jax.experimental.pallas import tpu as pltpu

F32 = jnp.float32
BF = jnp.bfloat16

D_MODEL = 1024
DEPTH = 2
ATT_HEADS = 16
ATT_HEAD_DIM = 64
BLK = 128
SSD_D_INNER = 2048
SSD_HEADS = 32
SSD_GROUPS = 4
SSD_STATE = 128
SSD_CONV = 4
BC_DIM = 512
CONV_DIM = 3072
FFN_HIDDEN = 2816
IN_DIM = 8480
LN_EPS = 1e-5
RMS_EPS = 1e-5
ALPHA = (2 * DEPTH) ** 0.25
ADAM_LR = 0.001
ADAM_B1 = 0.9
ADAM_B2 = 0.999
ADAM_EPS = 1e-08
ADAM_WD = 0.01
ADAM_STEP = 10

_PACK = (("gates", 6432, 2048), ("z", 1280, 2048), ("q", 0, 1024), ("xs", 3328, 2048), ("B", 5376, 512),
         ("C", 5888, 512), ("k", 1024, 128), ("v", 1152, 128), ("dt", 6400, 32))
NP = 8704
_OFF = {}
_o = 0
for _n, _, _w in _PACK:
    _OFF[_n] = _o
    _o += _w
_PAD = NP - _o

VMEM_LIMIT_BYTES = 56 * 1024 * 1024
ROW_TILE = 512
BIG = ("w_in", "w_ssd_out", "w_att_out", "w_mix_out", "w_ffn_gate", "w_ffn_up", "w_ffn_down")
WEIGHTS = ("ln_in_g", "ln_in_b", "w_in", "conv_w", "conv_b", "dt_bias", "a_log", "d_skip", "ssd_norm_w", "att_sinks",
           "w_ssd_out", "w_att_out", "w_mix_out", "ln_mix_g", "ln_mix_b", "w_ffn_gate", "w_ffn_up", "w_ffn_down",
           "ln_ffn_g", "ln_ffn_b")
SMALL = tuple(n for n in WEIGHTS if n not in BIG)


def _params(n_grid):
    return pltpu.CompilerParams(dimension_semantics=("arbitrary",) * n_grid, vmem_limit_bytes=VMEM_LIMIT_BYTES)


def _dot(a, b, ca, cb):
    return lax.dot_general(a.astype(BF), b.astype(BF), (((ca,), (cb,)), ((), ())), preferred_element_type=F32)


@jax.custom_vjp
def _nn(a, b):
    return _dot(a, b, 1, 0)


def _nn_f(a, b):
    return _dot(a, b, 1, 0), (a.astype(BF), b.astype(BF))


def _nn_b(res, g):
    a, b = res
    return _dot(g, b, 1, 1), _dot(a, g, 0, 0)


_nn.defvjp(_nn_f, _nn_b)


@jax.custom_vjp
def _nt(a, b):
    return _dot(a, b, 1, 1)


def _nt_f(a, b):
    return _dot(a, b, 1, 1), (a.astype(BF), b.astype(BF))


def _nt_b(res, g):
    a, b = res
    return _dot(g, b, 1, 0), _dot(g, a, 0, 0)


_nt.defvjp(_nt_f, _nt_b)


@jax.custom_vjp
def _tn(a, b):
    return _dot(a, b, 0, 0)


def _tn_f(a, b):
    return _dot(a, b, 0, 0), (a.astype(BF), b.astype(BF))


def _tn_b(res, g):
    a, b = res
    return _dot(b, g, 1, 1), _dot(a, g, 1, 0)


_tn.defvjp(_tn_f, _tn_b)


def _sigmoid(x):
    return 0.5 * jnp.tanh(0.5 * x) + 0.5


def _silu(x):
    return x * _sigmoid(x)


def _layer_norm(s, g, b):
    mu = jnp.mean(s, axis=-1, keepdims=True)
    sc = s - mu
    var = jnp.mean(sc * sc, axis=-1, keepdims=True)
    return sc * lax.rsqrt(var + LN_EPS) * g + b


def _ssd_group(pre_x, pre_b, pre_c, z, dtr, dtb, alog, dsk, nw, state):
    t = pre_x.shape[0]
    xs, bm, cm = _silu(pre_x), _silu(pre_b), _silu(pre_c)
    dt = jax.nn.softplus(dtr + dtb)
    da = dt * (-jnp.exp(alog))
    row = lax.broadcasted_iota(jnp.int32, (t, t), 0)
    col = lax.broadcasted_iota(jnp.int32, (t, t), 1)
    upper = (row <= col).astype(F32)
    cum = jnp.dot(da, upper, precision=lax.Precision.HIGHEST, preferred_element_type=F32)
    tot = jnp.sum(da, axis=1, keepdims=True)
    cb = _nt(cm, bm)
    tril = row >= col
    first = lax.broadcasted_iota(jnp.int32, (t, BLK), 1) < 64
    first_row = lax.broadcasted_iota(jnp.int32, (1, BLK), 1) < 64

    def col_form(r):
        return jnp.broadcast_to(r, (t, t)).T

    ys, new_state = [], []
    for p in range(4):
        h0, h1 = 2 * p, 2 * p + 1
        sl = slice(BLK * p, BLK * (p + 1))
        x_pair = xs[:, sl] * jnp.where(first, col_form(dt[h0:h0 + 1]), col_form(dt[h1:h1 + 1]))
        y_pair = None
        cc = []
        for h, keep in ((h0, first), (h1, jnp.logical_not(first))):
            cr = jnp.broadcast_to(cum[h:h + 1], (t, t))
            cc.append(cr.T)
            decay = jnp.exp(jnp.where(tril, cc[-1] - cr, -1e30))
            y_h = _nn(decay * cb, jnp.where(keep, x_pair, 0.0))
            y_pair = y_h if y_pair is None else y_pair + y_h
        s_pair = state[:, sl]
        y_pair = y_pair + _nn(cm, s_pair) * jnp.where(first, jnp.exp(cc[0]), jnp.exp(cc[1]))
        to_end = jnp.where(first, jnp.exp(tot[h0:h0 + 1] - cc[0]), jnp.exp(tot[h1:h1 + 1] - cc[1]))
        chunk_decay = jnp.where(first_row, jnp.exp(tot[h0:h0 + 1]), jnp.exp(tot[h1:h1 + 1]))
        new_state.append(s_pair * chunk_decay + _tn(bm, x_pair * to_end))
        ys.append(y_pair + dsk[:, sl] * xs[:, sl])
    y = jnp.concatenate(ys, axis=1) * _silu(z)
    y = y * lax.rsqrt(jnp.mean(y * y, axis=-1, keepdims=True) + RMS_EPS) * nw
    return y, jnp.concatenate(new_state, axis=1)


def _attn_block(q, kp, kc, vp, vc, sinks, n, kv):
    t = q.shape[0]
    kb = jnp.concatenate([kp, kc], axis=0)
    vb = jnp.concatenate([vp, vc], axis=0)
    qi = lax.broadcasted_iota(jnp.int32, (t, 2 * t), 0)
    kj = lax.broadcasted_iota(jnp.int32, (t, 2 * t), 1)
    rel = qi + t - kj
    valid = (rel >= 0) & (rel < t) & ((n * t - t + kj) >= 0)
    relf = rel.astype(F32)
    first = lax.broadcasted_iota(jnp.int32, (t, BLK), 1) < 64
    lane16 = lax.broadcasted_iota(jnp.int32, (1, ATT_HEADS), 1)
    outs = []
    for p in range(4):
        qp = q[:, BLK * p:BLK * (p + 1)] * (ATT_HEAD_DIM ** -0.5)
        o_pair = None
        for half, keep in enumerate((first, jnp.logical_not(first))):
            h = kv * 8 + 2 * p + half
            s = _nt(jnp.where(keep, qp, 0.0), kb)
            slope = jnp.exp((-8.0 * math.log(2.0) / ATT_HEADS) * jnp.asarray(h + 1, F32))
            s = jnp.where(valid, s - slope * relf, -1e30)
            sink = jnp.sum(jnp.where(lane16 == h, sinks, 0.0), axis=1, keepdims=True)
            m = lax.stop_gradient(jnp.maximum(jnp.max(s, axis=1, keepdims=True), sink))
            e = jnp.exp(s - m)
            den = jnp.sum(e, axis=1, keepdims=True) + jnp.exp(sink - m)
            o = _nn(e * (1.0 / den), vb)
            o_pair = o if o_pair is None else jnp.where(first, o_pair, o)
        outs.append(o_pair)
    return jnp.concatenate(outs, axis=1)


def _chip_exchange(src, out, send_sems, recv_sems, bcast):
    x, y, c = lax.axis_index("x"), lax.axis_index("y"), lax.axis_index("c")
    me = 2 * x + y

    def copy(a, d, slot):
        j = jnp.bitwise_xor(me, d)
        return pltpu.make_async_remote_copy(
            src_ref=src[a] if bcast else src[a].at[j], dst_ref=out[a].at[slot],
            send_sem=send_sems.at[a * 4 + d], recv_sem=recv_sems.at[a * 4 + d],
            device_id=(j // 2, j % 2, c), device_id_type=pl.DeviceIdType.MESH)

    pairs = [(a, d) for a in range(len(src)) for d in range(1, 4)]

    def start():
        for a, d in pairs:
            copy(a, d, me).start()

    def wait():
        for a, d in pairs:
            copy(a, d, jnp.bitwise_xor(me, d)).wait_recv()
        for a, d in pairs:
            copy(a, d, me).wait_send()

    return start, wait


def _mm(a, b, mode, out_dtype, tm, tn, tk, name, hosted=None):
    if mode == "nn":
        (m, k), (k2, n) = a.shape, b.shape
    elif mode == "nt":
        (m, k), (n, k2) = a.shape, b.shape
    else:
        (k, m), (k2, n) = a.shape, b.shape
    assert k == k2, (a.shape, b.shape, mode)
    tm, tn, tk = min(tm, m), min(tn, n), min(tk, k)
    assert m % tm == 0 and n % tn == 0 and k % tk == 0, (m, n, k, tm, tn, tk)
    nk = k // tk
    grid = (m // tm, n // tn, nk)
    ca, cb = {"nn": (1, 0), "nt": (1, 1), "tn": (0, 0)}[mode]
    h_arrays, h_bcast = hosted if hosted else ((), False)
    nh = len(h_arrays)

    def body(*refs):
        a_ref, b_ref = refs[:2]
        o_ref = refs[2 + nh]
        scratch = refs[3 + 2 * nh:]
        step = [pl.program_id(ax) for ax in range(3)]
        if nh:
            start, wait = _chip_exchange(refs[2:2 + nh], refs[3 + nh:3 + 2 * nh], scratch[-2], scratch[-1], h_bcast)

            @pl.when((step[0] == 0) & (step[1] == 0) & (step[2] == 0))
            def _():
                start()

        part = _dot(a_ref[...], b_ref[...], ca, cb)
        if nk == 1:
            o_ref[...] = part.astype(o_ref.dtype)
        else:
            acc_ref = scratch[0]

            @pl.when(step[2] == 0)
            def _():
                acc_ref[...] = part

            @pl.when(step[2] > 0)
            def _():
                acc_ref[...] += part

            @pl.when(step[2] == nk - 1)
            def _():
                o_ref[...] = acc_ref[...].astype(o_ref.dtype)

        if nh:
            @pl.when((step[0] == grid[0] - 1) & (step[1] == grid[1] - 1) & (step[2] == nk - 1))
            def _():
                wait()

    a_spec = pl.BlockSpec((tk, tm), lambda i, j, kk: (kk, i)) if mode == "tn" else pl.BlockSpec((tm, tk), lambda i, j, kk: (i, kk))
    b_spec = pl.BlockSpec((tn, tk), lambda i, j, kk: (j, kk)) if mode == "nt" else pl.BlockSpec((tk, tn), lambda i, j, kk: (kk, j))
    any_spec = pl.BlockSpec(memory_space=pl.ANY)
    res = pl.pallas_call(
        body, name=name, grid=grid,
        in_specs=[a_spec, b_spec] + [any_spec] * nh,
        out_specs=[pl.BlockSpec((tm, tn), lambda i, j, kk: (i, j))] + [any_spec] * nh,
        out_shape=[jax.ShapeDtypeStruct((m, n), out_dtype)]
        + [jax.ShapeDtypeStruct(((4,) + s.shape) if h_bcast else s.shape, s.dtype) for s in h_arrays],
        scratch_shapes=([] if nk == 1 else [pltpu.VMEM((tm, tn), F32)])
        + ([pltpu.SemaphoreType.DMA((4 * nh,)), pltpu.SemaphoreType.DMA((4 * nh,))] if nh else []),
        compiler_params=_params(3),
    )(a, b, *h_arrays)
    return (res[0], list(res[1:])) if nh else res[0]


def _rows(fn, rows, params, outs, accs, tile, name):
    length = rows[0][0].shape[0]
    tile = min(tile, length)
    assert length % tile == 0
    nr, npar, no = len(rows), len(params), len(outs)

    def body(*refs):
        vals = [r[...] for r in refs[:nr + npar]]
        o, a = fn(*vals)
        for ref, val in zip(refs[nr + npar:nr + npar + no], o):
            ref[...] = val.astype(ref.dtype)
        i = pl.program_id(0)
        for ref, val in zip(refs[nr + npar + no:], a):
            @pl.when(i == 0)
            def _(ref=ref, val=val):
                ref[...] = val

            @pl.when(i > 0)
            def _(ref=ref, val=val):
                ref[...] += val

    in_specs = [pl.BlockSpec((tile, w), functools.partial(lambda i, cb: (i, cb), cb=cb)) for _, cb, w in rows]
    in_specs += [pl.BlockSpec(p.shape, lambda i: (0, 0)) for p in params]
    outs = [o if len(o) == 3 else (o[0], o[1], o[0]) for o in outs]
    out_specs = [pl.BlockSpec((tile, w), lambda i: (i, 0)) for w, _, _ in outs]
    out_specs += [pl.BlockSpec((r, w), lambda i: (0, 0)) for r, w in accs]
    out_shape = [jax.ShapeDtypeStruct((length, full), dt) for _, dt, full in outs]
    out_shape += [jax.ShapeDtypeStruct((r, w), F32) for r, w in accs]
    res = pl.pallas_call(
        body, name=name, grid=(length // tile,), in_specs=in_specs, out_specs=out_specs, out_shape=out_shape,
        compiler_params=_params(1),
    )(*[r[0] for r in rows], *params)
    return res[:no], res[no:]


CONV_SUB = 32


def _conv_fwd(proj, conv_w, conv_b, name, hosted=None):
    length = proj.shape[0]
    tl, cw = min(1024, length), 512
    cb0 = _OFF["xs"] // cw
    grid = (CONV_DIM // cw, length // tl)
    h_arrays, h_bcast = hosted if hosted else ((), False)
    nh = len(h_arrays)

    def body(*refs):
        u_ref, halo_ref, w_ref, b_ref = refs[:4]
        o_ref = refs[4 + nh]
        win = refs[5 + 2 * nh]
        j, i = pl.program_id(0), pl.program_id(1)
        if nh:
            start, wait = _chip_exchange(refs[4:4 + nh], refs[5 + nh:5 + 2 * nh], refs[-2], refs[-1], h_bcast)

            @pl.when((j == 0) & (i == 0))
            def _():
                start()

            @pl.when((j == grid[0] - 1) & (i == grid[1] - 1))
            def _():
                wait()

        win[0:8, :] = jnp.where(i > 0, halo_ref[...], 0.0)
        win[8:8 + tl, :] = u_ref[...]
        taps = [w_ref[kk:kk + 1, :] for kk in range(SSD_CONV)]
        bias = b_ref[...]
        for r in range(0, tl, CONV_SUB):
            acc = bias + taps[0] * win[5 + r:5 + r + CONV_SUB, :]
            for kk in range(1, SSD_CONV):
                acc = acc + taps[kk] * win[5 + kk + r:5 + kk + r + CONV_SUB, :]
            o_ref[r:r + CONV_SUB, :] = acc

    any_spec = pl.BlockSpec(memory_space=pl.ANY)
    res = pl.pallas_call(
        body, name=name, grid=grid,
        in_specs=[pl.BlockSpec((tl, cw), lambda j, i: (i, cb0 + j)),
                  pl.BlockSpec((8, cw), lambda j, i: (jnp.maximum(i * (tl // 8) - 1, 0), cb0 + j)),
                  pl.BlockSpec((SSD_CONV, cw), lambda j, i: (0, j)),
                  pl.BlockSpec((1, cw), lambda j, i: (0, j))] + [any_spec] * nh,
        out_specs=[pl.BlockSpec((tl, cw), lambda j, i: (i, j))] + [any_spec] * nh,
        out_shape=[jax.ShapeDtypeStruct((length, CONV_DIM), F32)]
        + [jax.ShapeDtypeStruct(((4,) + s.shape) if h_bcast else s.shape, s.dtype) for s in h_arrays],
        scratch_shapes=[pltpu.VMEM((8 + tl, cw), F32)]
        + ([pltpu.SemaphoreType.DMA((4 * nh,)), pltpu.SemaphoreType.DMA((4 * nh,))] if nh else []),
        compiler_params=_params(2),
    )(proj, proj, conv_w, conv_b, *h_arrays)
    return (res[0], list(res[1:])) if nh else res[0]


def _conv_bwd(dpre, proj, conv_w, dproj, part, name):
    length = proj.shape[0]
    width = dpre.shape[1]
    tl, cw = min(1024, length), 512
    cb0 = _OFF["xs"] // cw + part
    n_t = length // tl

    def body(d_ref, dnext_ref, u_ref, halo_ref, w_ref, _, du_ref, dw_ref, db_ref, dwin, uwin):
        i = pl.program_id(1)
        dwin[0:tl, :] = d_ref[...].astype(F32)
        dwin[tl:tl + 8, :] = jnp.where(i < n_t - 1, dnext_ref[...].astype(F32)[0:8], 0.0)
        uwin[0:8, :] = jnp.where(i > 0, halo_ref[...], 0.0)
        uwin[8:8 + tl, :] = u_ref[...]
        taps = [w_ref[kk:kk + 1, :] for kk in range(SSD_CONV)]
        sub = CONV_SUB // 2
        acc_w = [jnp.zeros((sub, cw), F32) for _ in range(SSD_CONV)]
        acc_b = jnp.zeros((sub, cw), F32)
        for r in range(0, tl, sub):
            d = dwin[r:r + sub, :]
            du = taps[3] * d
            for kk in range(SSD_CONV - 1):
                du = du + taps[kk] * dwin[3 - kk + r:3 - kk + r + sub, :]
            du_ref[r:r + sub, :] = du.astype(du_ref.dtype)
            for kk in range(SSD_CONV):
                acc_w[kk] = acc_w[kk] + d * uwin[5 + kk + r:5 + kk + r + sub, :]
            acc_b = acc_b + d
        dw = jnp.concatenate([jnp.sum(a, axis=0, keepdims=True) for a in acc_w], axis=0)
        db = jnp.sum(acc_b, axis=0, keepdims=True)

        @pl.when(i == 0)
        def _():
            dw_ref[...] = dw
            db_ref[...] = db

        @pl.when(i > 0)
        def _():
            dw_ref[...] += dw
            db_ref[...] += db

    return pl.pallas_call(
        body, name=name, grid=(width // cw, n_t), input_output_aliases={5: 0},
        in_specs=[pl.BlockSpec((tl, cw), lambda j, i: (i, j)),
                  pl.BlockSpec((16, cw), lambda j, i: (jnp.minimum((i + 1) * (tl // 16), length // 16 - 1), j)),
                  pl.BlockSpec((tl, cw), lambda j, i: (i, cb0 + j)),
                  pl.BlockSpec((8, cw), lambda j, i: (jnp.maximum(i * (tl // 8) - 1, 0), cb0 + j)),
                  pl.BlockSpec((SSD_CONV, cw), lambda j, i: (0, part + j)),
                  pl.BlockSpec(memory_space=pl.ANY)],
        out_specs=[pl.BlockSpec((tl, cw), lambda j, i: (i, cb0 + j)),
                   pl.BlockSpec((SSD_CONV, cw), lambda j, i: (0, j)),
                   pl.BlockSpec((1, cw), lambda j, i: (0, j))],
        out_shape=[jax.ShapeDtypeStruct((length, NP), BF),
                   jax.ShapeDtypeStruct((SSD_CONV, width), F32),
                   jax.ShapeDtypeStruct((1, width), F32)],
        scratch_shapes=[pltpu.VMEM((tl + 8, cw), F32), pltpu.VMEM((8 + tl, cw), F32)],
        compiler_params=_params(2),
    )(dpre, dpre, proj, proj, conv_w, dproj)


def _ssd_in_specs(rev, nc):
    def cidx(c):
        return nc - 1 - c if rev else c
    whole = lambda c: (0, 0)
    return [pl.BlockSpec((BLK, SSD_D_INNER), lambda c: (cidx(c), 0)),
            pl.BlockSpec((BLK, BC_DIM), lambda c: (cidx(c), SSD_D_INNER // BC_DIM)),
            pl.BlockSpec((BLK, BC_DIM), lambda c: (cidx(c), SSD_D_INNER // BC_DIM + 1)),
            pl.BlockSpec((BLK, SSD_D_INNER), lambda c: (cidx(c), _OFF["z"] // SSD_D_INNER)),
            pl.BlockSpec((SSD_HEADS, BLK), lambda c: (0, cidx(c))),
            pl.BlockSpec((SSD_HEADS, 1), whole),
            pl.BlockSpec((SSD_HEADS, 1), whole),
            pl.BlockSpec((1, SSD_D_INNER), whole),
            pl.BlockSpec((1, SSD_D_INNER), whole)]


def _group_args(g, px, pb, pc, z, dtr, dtb, alog, dsk, nw):
    wide, narrow, heads = slice(512 * g, 512 * (g + 1)), slice(BLK * g, BLK * (g + 1)), slice(8 * g, 8 * (g + 1))
    return (px[:, wide], pb[:, narrow], pc[:, narrow], z[:, wide], dtr[heads, :], dtb[heads, :], alog[heads, :],
            dsk[:, wide], nw[:, wide])


def _ssd_fwd(pre, proj, dt_t, dtb, alog, dsk, nw, name, hosted=None):
    length = pre.shape[0]
    nc = length // BLK
    h_arrays, h_bcast = hosted if hosted else ((), False)
    nh = len(h_arrays)

    def body(*refs):
        px, pb, pc, z, dtr, dtb_r, al_r, dsk_r, nw_r = refs[:9]
        y_ref, sin_ref = refs[9 + nh:11 + nh]
        st = refs[11 + 2 * nh]
        c = pl.program_id(0)
        if nh:
            start, wait = _chip_exchange(refs[9:9 + nh], refs[11 + nh:11 + 2 * nh], refs[-2], refs[-1], h_bcast)

        @pl.when(c == 0)
        def _():
            st[...] = jnp.zeros_like(st)
            if nh:
                start()

        for g in range(SSD_GROUPS):
            s_in = st[g]
            sin_ref[g] = s_in
            y, s_out = _ssd_group(*_group_args(g, px, pb, pc, z, dtr, dtb_r, al_r, dsk_r, nw_r), s_in)
            y_ref[:, 512 * g:512 * (g + 1)] = y.astype(y_ref.dtype)
            st[g] = s_out

        if nh:
            @pl.when(c == nc - 1)
            def _():
                wait()

    any_spec = pl.BlockSpec(memory_space=pl.ANY)
    res = pl.pallas_call(
        body, name=name, grid=(nc,), in_specs=_ssd_in_specs(False, nc) + [any_spec] * nh,
        out_specs=[pl.BlockSpec((BLK, SSD_D_INNER), lambda c: (c, 0)),
                   pl.BlockSpec((SSD_GROUPS, None, SSD_STATE, 512), lambda c: (0, c, 0, 0))] + [any_spec] * nh,
        out_shape=[jax.ShapeDtypeStruct((length, SSD_D_INNER), BF),
                   jax.ShapeDtypeStruct((SSD_GROUPS, nc, SSD_STATE, 512), F32)]
        + [jax.ShapeDtypeStruct(((4,) + s.shape) if h_bcast else s.shape, s.dtype) for s in h_arrays],
        scratch_shapes=[pltpu.VMEM((SSD_GROUPS, SSD_STATE, 512), F32)]
        + ([pltpu.SemaphoreType.DMA((4 * nh,)), pltpu.SemaphoreType.DMA((4 * nh,))] if nh else []),
        compiler_params=_params(1),
    )(pre, pre, pre, proj, dt_t, dtb, alog, dsk, nw, *h_arrays)
    return (res[0], res[1], list(res[2:])) if nh else (res[0], res[1])


def _ssd_bwd(pre, proj, dt_t, dtb, alog, dsk, nw, s_in, dy, dproj, name, hosted=None):
    length = pre.shape[0]
    nc = length // BLK
    h_arrays, h_bcast = hosted if hosted else ((), False)
    nh = len(h_arrays)

    def body(*refs):
        px, pb, pc, z, dtr, dtb_r, al_r, dsk_r, nw_r, sin_r, dy_r = refs[:11]
        dpre, dz, ddt, ddtb, dal, ddsk, dnw = refs[12 + nh:19 + nh]
        dst = refs[19 + 2 * nh]
        c = pl.program_id(0)
        if nh:
            start, wait = _chip_exchange(refs[12:12 + nh], refs[19 + nh:19 + 2 * nh], refs[-2], refs[-1], h_bcast)

        @pl.when(c == 0)
        def _():
            dst[...] = jnp.zeros_like(dst)
            if nh:
                start()

        for grp in range(SSD_GROUPS):
            wide, heads = slice(512 * grp, 512 * (grp + 1)), slice(8 * grp, 8 * (grp + 1))
            _, vjp = jax.vjp(_ssd_group, *_group_args(grp, px, pb, pc, z, dtr, dtb_r, al_r, dsk_r, nw_r), sin_r[grp])
            g = vjp((dy_r[:, wide], dst[grp]))
            dpre[:, wide] = g[0].astype(dpre.dtype)
            dpre[:, slice(SSD_D_INNER + BLK * grp, SSD_D_INNER + BLK * (grp + 1))] = g[1].astype(dpre.dtype)
            dpre[:, slice(SSD_D_INNER + BC_DIM + BLK * grp, SSD_D_INNER + BC_DIM + BLK * (grp + 1))] = g[2].astype(dpre.dtype)
            dz[:, wide] = g[3].astype(dz.dtype)
            ddt[heads, :] = g[4]
            dst[grp] = g[9]
            for ref, val, idx in ((ddtb, g[5], (heads, slice(None))), (dal, g[6], (heads, slice(None))),
                                  (ddsk, g[7], (slice(None), wide)), (dnw, g[8], (slice(None), wide))):
                @pl.when(c == 0)
                def _(ref=ref, val=val, idx=idx):
                    ref[idx] = val

                @pl.when(c > 0)
                def _(ref=ref, val=val, idx=idx):
                    ref[idx] += val

        if nh:
            @pl.when(c == nc - 1)
            def _():
                wait()

    rev = lambda c: nc - 1 - c
    whole = lambda c: (0, 0)
    any_spec = pl.BlockSpec(memory_space=pl.ANY)
    in_specs = _ssd_in_specs(True, nc) + [
        pl.BlockSpec((SSD_GROUPS, None, SSD_STATE, 512), lambda c: (0, rev(c), 0, 0)),
        pl.BlockSpec((BLK, SSD_D_INNER), lambda c: (rev(c), 0)),
        any_spec] + [any_spec] * nh
    res = pl.pallas_call(
        body, name=name, grid=(nc,), in_specs=in_specs, input_output_aliases={11: 1},
        out_specs=[pl.BlockSpec((BLK, CONV_DIM), lambda c: (rev(c), 0)),
                   pl.BlockSpec((BLK, SSD_D_INNER), lambda c: (rev(c), _OFF["z"] // SSD_D_INNER)),
                   pl.BlockSpec((SSD_HEADS, BLK), lambda c: (0, rev(c))),
                   pl.BlockSpec((SSD_HEADS, 1), whole),
                   pl.BlockSpec((SSD_HEADS, 1), whole),
                   pl.BlockSpec((1, SSD_D_INNER), whole),
                   pl.BlockSpec((1, SSD_D_INNER), whole)] + [any_spec] * nh,
        out_shape=[jax.ShapeDtypeStruct((length, CONV_DIM), BF),
                   jax.ShapeDtypeStruct((length, NP), BF),
                   jax.ShapeDtypeStruct((SSD_HEADS, length), F32),
                   jax.ShapeDtypeStruct((SSD_HEADS, 1), F32),
                   jax.ShapeDtypeStruct((SSD_HEADS, 1), F32),
                   jax.ShapeDtypeStruct((1, SSD_D_INNER), F32),
                   jax.ShapeDtypeStruct((1, SSD_D_INNER), F32)]
        + [jax.ShapeDtypeStruct(((4,) + s.shape) if h_bcast else s.shape, s.dtype) for s in h_arrays],
        scratch_shapes=[pltpu.VMEM((SSD_GROUPS, SSD_STATE, 512), F32)]
        + ([pltpu.SemaphoreType.DMA((4 * nh,)), pltpu.SemaphoreType.DMA((4 * nh,))] if nh else []),
        compiler_params=_params(1),
    )(pre, pre, pre, proj, dt_t, dtb, alog, dsk, nw, s_in, dy, dproj, *h_arrays)
    return tuple(res[:7]) + ((list(res[7:]),) if nh else ())


assert _OFF["gates"] == 0 and _OFF["z"] % SSD_D_INNER == 0 and _OFF["q"] % D_MODEL == 0 and _OFF["k"] + 512 == NP


def _attn_in_specs():
    prev = lambda n: (0, jnp.maximum(n - 1, 0), 0)
    cur = lambda n: (0, n, 0)
    return [pl.BlockSpec((BLK, D_MODEL), lambda n: (n, _OFF["q"] // D_MODEL)),
            pl.BlockSpec((2, BLK, BLK), prev),
            pl.BlockSpec((2, BLK, BLK), cur),
            pl.BlockSpec((2, BLK, BLK), prev),
            pl.BlockSpec((2, BLK, BLK), cur),
            pl.BlockSpec((1, ATT_HEADS), lambda n: (0, 0))]


def _attn_fwd(proj, k2, v2, sinks, name, hosted=None):
    length = proj.shape[0]
    nb = length // BLK
    h_arrays, h_bcast = hosted if hosted else ((), False)
    nh = len(h_arrays)

    def body(*refs):
        q, kp, kc, vp, vc, sk = refs[:6]
        o_ref = refs[6 + nh]
        n = pl.program_id(0)
        if nh:
            start, wait = _chip_exchange(refs[6:6 + nh], refs[7 + nh:7 + 2 * nh], refs[-2], refs[-1], h_bcast)

            @pl.when(n == 0)
            def _():
                start()

        for kv in range(2):
            cols = slice(512 * kv, 512 * (kv + 1))
            o = _attn_block(q[:, cols], kp[kv], kc[kv], vp[kv], vc[kv], sk[...], n, kv)
            o_ref[:, cols] = o.astype(o_ref.dtype)

        if nh:
            @pl.when(n == nb - 1)
            def _():
                wait()

    any_spec = pl.BlockSpec(memory_space=pl.ANY)
    res = pl.pallas_call(
        body, name=name, grid=(nb,), in_specs=_attn_in_specs() + [any_spec] * nh,
        out_specs=[pl.BlockSpec((BLK, D_MODEL), lambda n: (n, 0))] + [any_spec] * nh,
        out_shape=[jax.ShapeDtypeStruct((length, D_MODEL), BF)]
        + [jax.ShapeDtypeStruct(((4,) + s.shape) if h_bcast else s.shape, s.dtype) for s in h_arrays],
        scratch_shapes=[pltpu.SemaphoreType.DMA((4 * nh,)), pltpu.SemaphoreType.DMA((4 * nh,))] if nh else [],
        compiler_params=_params(1),
    )(proj, k2, k2, v2, v2, sinks, *h_arrays)
    return (res[0], list(res[1:])) if nh else res[0]


def _attn_bwd(proj, k2, v2, sinks, datt, dproj, name):
    length = proj.shape[0]

    def body(q, kp, kc, vp, vc, sk, do, _, dq, dkp, dkc, dvp, dvc, dsk):
        n = pl.program_id(0)
        dsinks = None
        for kv in range(2):
            cols = slice(512 * kv, 512 * (kv + 1))
            _, vjp = jax.vjp(lambda *a: _attn_block(*a, n, kv), q[:, cols], kp[kv], kc[kv], vp[kv], vc[kv], sk[...])
            g = vjp(do[:, cols].astype(F32))
            dq[:, cols] = g[0].astype(dq.dtype)
            dkp[kv] = g[1]
            dkc[kv] = g[2]
            dvp[kv] = g[3]
            dvc[kv] = g[4]
            dsinks = g[5] if dsinks is None else dsinks + g[5]

        @pl.when(n == 0)
        def _():
            dsk[...] = dsinks

        @pl.when(n > 0)
        def _():
            dsk[...] += dsinks

    blk3 = pl.BlockSpec((2, BLK, BLK), lambda n: (0, n, 0))
    kv_shape = jax.ShapeDtypeStruct((2, length, BLK), F32)
    return pl.pallas_call(
        body, name=name, grid=(length // BLK,),
        in_specs=_attn_in_specs() + [pl.BlockSpec((BLK, D_MODEL), lambda n: (n, 0)), pl.BlockSpec(memory_space=pl.ANY)],
        out_specs=[pl.BlockSpec((BLK, D_MODEL), lambda n: (n, _OFF["q"] // D_MODEL)), blk3, blk3, blk3, blk3,
                   pl.BlockSpec((1, ATT_HEADS), lambda n: (0, 0))],
        out_shape=[jax.ShapeDtypeStruct((length, NP), BF), kv_shape, kv_shape, kv_shape, kv_shape,
                   jax.ShapeDtypeStruct((1, ATT_HEADS), F32)],
        input_output_aliases={7: 0},
        compiler_params=_params(1),
    )(proj, k2, k2, v2, v2, sinks, datt, dproj)


DMA_CHUNK_BYTES = 1 << 20
N_STAGE = 12
LOOKAHEAD = 5


def _piece_chunks(shape, itemsize):
    if len(shape) < 2 or shape[-2] % 16 != 0:
        return [()]
    rows, cols = shape[-2:]
    step = min(rows, max(16, DMA_CHUNK_BYTES // (cols * itemsize) // 16 * 16))
    out = []
    for lead in itertools.product(*[range(d) for d in shape[:-2]]):
        for r0 in range(0, rows, step):
            out.append(lead + (pl.ds(r0, min(step, rows - r0)),))
    return out


def _exchange(srcs, group, bcast, name):
    size = {"c": 2, "xy": 4, "all": 8}[group]
    npeer = size - 1
    n = len(srcs)
    for s in srcs:
        assert bcast or s.shape[0] == size
    pieces = [s.shape if bcast else s.shape[1:] for s in srcs]
    chunks = [_piece_chunks(p, s.dtype.itemsize) for p, s in zip(pieces, srcs)]

    def body(*refs):
        src, out = refs[:n], refs[n:2 * n]
        load_sems, send_sems, recv_sems = refs[2 * n:]
        x, y, c = lax.axis_index("x"), lax.axis_index("y"), lax.axis_index("c")
        if group == "c":
            me = c
        elif group == "xy":
            me = 2 * x + y
        else:
            me = 4 * x + 2 * y + c

        def device(j):
            if group == "c":
                return (x, y, j)
            if group == "xy":
                return (j // 2, j % 2, c)
            return (j // 4, (j // 2) % 2, j % 2)

        def part(ref, idx):
            return ref.at[idx] if idx else ref

        def piece(a, j):
            return src[a] if bcast else src[a].at[j]

        for a in range(n):
            jobs = [(None, idx) for idx in chunks[a]] if bcast else [(d, idx) for idx in chunks[a] for d in range(1, size)]
            full_rows = chunks[a][0][-1].size if chunks[a][0] else None
            slot_shape = (full_rows, pieces[a][-1]) if chunks[a][0] else pieces[a]

            def stream(stage, a=a, jobs=jobs, full_rows=full_rows):
                def slot(q):
                    idx = jobs[q][1]
                    view = stage.at[q % N_STAGE]
                    if idx and idx[-1].size != full_rows:
                        view = view.at[pl.ds(0, idx[-1].size)]
                    return view

                def load(q):
                    d, idx = jobs[q]
                    j = me if d is None else jnp.bitwise_xor(me, d)
                    return pltpu.make_async_copy(part(piece(a, j), idx), slot(q), load_sems.at[q % N_STAGE])

                def sends(q):
                    d, idx = jobs[q]
                    return [pltpu.make_async_remote_copy(
                        src_ref=slot(q), dst_ref=part(out[a].at[me], idx),
                        send_sem=send_sems.at[(q % N_STAGE) * npeer + dd - 1], recv_sem=recv_sems.at[a * size + dd],
                        device_id=device(jnp.bitwise_xor(me, dd)), device_id_type=pl.DeviceIdType.MESH)
                        for dd in (range(1, size) if d is None else (d,))]

                nq = len(jobs)
                for q in range(nq + LOOKAHEAD):
                    if q < nq:
                        if q >= N_STAGE:
                            for cp in sends(q - N_STAGE):
                                cp.wait_send()
                        load(q).start()
                    if q >= LOOKAHEAD:
                        load(q - LOOKAHEAD).wait()
                        for cp in sends(q - LOOKAHEAD):
                            cp.start()
                for q in range(max(0, nq - N_STAGE), nq):
                    for cp in sends(q):
                        cp.wait_send()

            pl.run_scoped(stream, pltpu.VMEM((N_STAGE,) + tuple(slot_shape), srcs[a].dtype))

        for a in range(n):
            for d in range(1, size):
                j = jnp.bitwise_xor(me, d)
                pltpu.make_async_remote_copy(
                    src_ref=piece(a, j), dst_ref=out[a].at[j], send_sem=send_sems.at[0], recv_sem=recv_sems.at[a * size + d],
                    device_id=device(j), device_id_type=pl.DeviceIdType.MESH).wait_recv()

    any_spec = pl.BlockSpec(memory_space=pl.ANY)
    return pl.pallas_call(
        body, name=name, in_specs=[any_spec] * n, out_specs=[any_spec] * n,
        out_shape=[jax.ShapeDtypeStruct((size,) + tuple(p), s.dtype) for p, s in zip(pieces, srcs)],
        scratch_shapes=[pltpu.SemaphoreType.DMA((N_STAGE,)), pltpu.SemaphoreType.DMA((N_STAGE * npeer,)),
                        pltpu.SemaphoreType.DMA((n * size,))],
        compiler_params=pltpu.CompilerParams(vmem_limit_bytes=VMEM_LIMIT_BYTES),
    )(*srcs)


def _with_own(outs, owns, me):
    return [lax.dynamic_update_index_in_dim(o, w, me, 0) for o, w in zip(outs, owns)]


def _sum_slots(arr, name):
    k = arr.shape[0]
    rest = arr.shape[1:]
    width = rest[-1]
    rows_per = math.prod(rest[:-1])
    a2 = arr.reshape(k * rows_per, width)
    tile = rows_per
    for cand in (256, 128, 64, 32, 16, 8):
        if rows_per % cand == 0:
            tile = cand
            break
    nt = rows_per // tile

    def body(*refs):
        acc = refs[0][...]
        for r in refs[1:k]:
            acc = acc + r[...]
        refs[k][...] = acc

    in_specs = [pl.BlockSpec((tile, width), functools.partial(lambda i, s: (s * nt + i, 0), s=s)) for s in range(k)]
    out = pl.pallas_call(
        body, name=name, grid=(nt,), in_specs=in_specs, out_specs=pl.BlockSpec((tile, width), lambda i: (i, 0)),
        out_shape=jax.ShapeDtypeStruct((rows_per, width), arr.dtype), compiler_params=_params(1),
    )(*([a2] * k))
    return out.reshape(rest)


def _sum_pieces(recv, own, own_slotted, me, name):
    k = recv.shape[0]
    rest = recv.shape[1:]
    width = rest[-1]
    rows_per = math.prod(rest[:-1])
    r2 = recv.reshape(k * rows_per, width)
    o2 = own.reshape(-1, width)
    tile = rows_per
    for cand in (256, 128, 64, 32, 16, 8):
        if rows_per % cand == 0:
            tile = cand
            break
    nt = rows_per // tile

    def body(me_ref, *refs):
        mine = refs[k][...].astype(F32)
        acc = None
        for j in range(k):
            term = jnp.where(me_ref[0] == j, mine, refs[j][...].astype(F32))
            acc = term if acc is None else acc + term
        refs[k + 1][...] = acc
        refs[k + 2][...] = acc.astype(BF)

    in_specs = [pl.BlockSpec((tile, width), functools.partial(lambda i, m, s: (s * nt + i, 0), s=s)) for s in range(k)]
    in_specs.append(pl.BlockSpec((tile, width), (lambda i, m: (m[0] * nt + i, 0)) if own_slotted else (lambda i, m: (i, 0))))
    out_spec = pl.BlockSpec((tile, width), lambda i, m: (i, 0))
    out, out_bf = pl.pallas_call(
        body, name=name,
        grid_spec=pltpu.PrefetchScalarGridSpec(num_scalar_prefetch=1, grid=(nt,), in_specs=in_specs, out_specs=[out_spec, out_spec]),
        out_shape=[jax.ShapeDtypeStruct((rows_per, width), F32), jax.ShapeDtypeStruct((rows_per, width), BF)],
        compiler_params=_params(1),
    )(jnp.reshape(me, (1,)).astype(jnp.int32), *([r2] * k), o2)
    return out.reshape(rest), out_bf.reshape(rest)


def _adamw(w, g, m, v, name):
    def fn(w_, g_, m_, v_):
        m1 = ADAM_B1 * m_ + (1.0 - ADAM_B1) * g_
        v1 = ADAM_B2 * v_ + (1.0 - ADAM_B2) * (g_ * g_)
        m_hat = m1 / (1.0 - ADAM_B1 ** ADAM_STEP)
        v_hat = v1 / (1.0 - ADAM_B2 ** ADAM_STEP)
        delta = -ADAM_LR * (m_hat / (jnp.sqrt(v_hat) + ADAM_EPS) + ADAM_WD * w_)
        return (delta, m1, v1), ()

    rows, width = w.shape
    tile = rows
    for cand in (256, 128, 64, 32, 16, 8):
        if rows % cand == 0:
            tile = cand
            break
    o, _ = _rows(fn, [(w, 0, width), (g, 0, width), (m, 0, width), (v, 0, width)], [],
                 [(width, F32)] * 3, [], tile, name)
    return o


def _col_pieces(start, width, shard):
    out = []
    while width:
        chip, off = divmod(start, shard)
        take = min(width, shard - off)
        out.append((chip, off, take))
        start, width = start + take, width - take
    return out


def _gather_cols(parts, ranges, pad=0):
    shard = parts.shape[3]
    halves = []
    for h in range(2):
        cols = [parts[h, chip, :, off:off + w] for s, wd in ranges for chip, off, w in _col_pieces(s, wd, shard)]
        if pad:
            cols.append(jnp.zeros((parts.shape[2], pad), parts.dtype))
        halves.append(jnp.concatenate(cols, axis=1))
    return jnp.concatenate(halves, axis=0)


def _scatter_cols(g, ranges, shard):
    starts, pos = {}, 0
    for s, wd in ranges:
        starts[s] = (pos, wd)
        pos += wd
    order = sorted(starts)
    hr = g.shape[0] // 2
    out = []
    for h in range(2):
        per_chip = []
        for chip in range(4):
            cols = []
            for s in order:
                p0, wd = starts[s]
                lo, hi = max(s, chip * shard), min(s + wd, (chip + 1) * shard)
                if lo < hi:
                    cols.append(g[h * hr:(h + 1) * hr, p0 + lo - s:p0 + hi - s])
            per_chip.append(jnp.concatenate(cols, axis=1))
        out.append(jnp.stack(per_chip))
    return jnp.stack(out)


_IN_RANGES = tuple((o, wd) for _, o, wd in _PACK)


def _double_heads(t):
    length = t.shape[0]
    h = jnp.transpose(t.reshape(length, 2, 64), (1, 0, 2))
    return jnp.concatenate([h, h], axis=-1)


def _fold_heads(d_cur, d_prev):
    length = d_cur.shape[1]
    d = d_cur + jnp.concatenate([d_prev[:, BLK:], jnp.zeros((2, BLK, BLK), F32)], axis=1)
    d = d[..., :64] + d[..., 64:]
    return jnp.transpose(d, (1, 0, 2)).reshape(length, 128)


def _ln_fwd(a, r, g, b, name):
    def fn(a_, r_, g_, b_):
        y = _layer_norm(ALPHA * a_ + r_, g_, b_)
        return (y, y), ()
    o, _ = _rows(fn, [(a, 0, D_MODEL), (r, 0, D_MODEL)], [g, b], [(D_MODEL, F32), (D_MODEL, BF)], [], ROW_TILE, name)
    return o


def _ln_bwd(a, r, d1, d2, g, b, name):
    def fn(a_, r_, d1_, d2_, g_, b_):
        _, vjp = jax.vjp(_layer_norm, ALPHA * a_ + r_, g_, b_)
        ds, dg, db = vjp(ALPHA * d1_ + d2_)
        return (ds, ds), (dg, db)
    o, acc = _rows(fn, [(a, 0, D_MODEL), (r, 0, D_MODEL), (d1, 0, D_MODEL), (d2, 0, D_MODEL)], [g, b],
                   [(D_MODEL, F32), (D_MODEL, BF)], [(1, D_MODEL), (1, D_MODEL)], ROW_TILE, name)
    return o[0], o[1], acc[0], acc[1]


def _layer_fwd(l, h, hb, w, p, hosted=None, hosted_attn=None, hosted_conv=None, late_weights=None):
    sv = {"h": h, "hb": hb}
    proj = _mm(hb, w["win"], "nn", F32, 4096, 512, 1024, f"proj{l}")
    sv["proj"] = proj
    pre = _conv_fwd(proj, p["conv_w"], p["conv_b"], f"conv{l}", hosted_conv)
    if hosted_conv:
        pre, got = pre
        w.update(late_weights(got))
    sv["pre"] = pre
    dt_t = jnp.transpose(proj[:, _OFF["dt"]:_OFF["dt"] + SSD_HEADS])
    sv["dt_t"] = dt_t
    res = _ssd_fwd(pre, proj, dt_t, p["dt_bias"], p["a_log"], p["d_skip_c"], p["ssd_norm_w"], f"ssd{l}", hosted)
    yn, s_in = res[:2]
    if hosted:
        sv["hosted"] = res[2]
    sv["yn"], sv["s_in"] = yn, s_in
    ya = _mm(yn, w["wso"], "nn", BF, 1024, 1024, 2048, f"ssdout{l}")
    k2 = _double_heads(proj[:, _OFF["k"]:_OFF["k"] + 128])
    v2 = _double_heads(proj[:, _OFF["v"]:_OFF["v"] + 128])
    sv["k2"], sv["v2"] = k2, v2
    att = _attn_fwd(proj, k2, v2, p["att_sinks"], f"attn{l}", hosted_attn)
    if hosted_attn:
        att, sv["hosted_attn"] = att
    sv["att"] = att
    yb = _mm(att, w["wao"], "nn", BF, 1024, 1024, 1024, f"attout{l}")
    sv["ya"], sv["yb"] = ya, yb
    gcb = _OFF["gates"] // 1024

    def gate_fn(ga, gb, ya_, yb_):
        return (_sigmoid(ga) * ya_.astype(F32) + _sigmoid(gb) * yb_.astype(F32),), ()
    (ub,), _ = _rows(gate_fn, [(proj, gcb, 1024), (proj, gcb + 1, 1024), (ya, 0, 1024), (yb, 0, 1024)], [],
                     [(D_MODEL, BF)], [], ROW_TILE, f"gate{l}")
    sv["ub"] = ub
    mix = _mm(ub, w["wmo"], "nn", F32, 1024, 1024, 1024, f"mixout{l}")
    sv["mix"] = mix
    h1, h1b = _ln_fwd(h, mix, p["ln_mix_g"], p["ln_mix_b"], f"lnmix{l}")
    sv["h1"], sv["h1b"] = h1, h1b
    gu = _mm(h1b, w["wgu"], "nn", BF, 4096, 512, 1024, f"ffnin{l}")
    sv["gu"] = gu

    def act_fn(g_, u_):
        return (_silu(g_.astype(F32)) * u_.astype(F32),), ()
    (act,), _ = _rows(act_fn, [(gu, 0, FFN_HIDDEN), (gu, 1, FFN_HIDDEN)], [], [(FFN_HIDDEN, BF)], [], ROW_TILE, f"swiglu{l}")
    sv["act"] = act
    ffn = _mm(act, w["wd"], "nn", F32, 1024, 1024, FFN_HIDDEN, f"ffnout{l}")
    sv["ffn"] = ffn
    return sv


def _layer_bwd(l, sv, w, p, ds_f, ds_b, hosted=(), early_hook=None, late_hook=None):
    g = {}
    g["w_ffn_down"] = _mm(sv["act"], ds_b, "tn", F32, 1408, 1024, 2048, f"dwd{l}")
    dact = _mm(ds_b, w["wd"], "nt", BF, 1024, FFN_HIDDEN, 1024, f"dact{l}")

    def act_bwd(g_, u_, d_):
        _, vjp = jax.vjp(lambda a, b: _silu(a) * b, g_.astype(F32), u_.astype(F32))
        dg_, du_ = vjp(d_.astype(F32))
        return (jnp.concatenate([dg_, du_], axis=1),), ()
    (dgu,), _ = _rows(act_bwd, [(sv["gu"], 0, FFN_HIDDEN), (sv["gu"], 1, FFN_HIDDEN), (dact, 0, FFN_HIDDEN)], [],
                      [(2 * FFN_HIDDEN, BF)], [], ROW_TILE // 2, f"dswiglu{l}")
    dwgu = _mm(sv["h1b"], dgu, "tn", F32, 1024, 1408, 2048, f"dwgu{l}")
    g["w_ffn_gate"], g["w_ffn_up"] = dwgu[:, :FFN_HIDDEN], dwgu[:, FFN_HIDDEN:]
    dh1 = _mm(dgu, w["wgu"], "nt", F32, 1024, 1024, 2816, f"dh1{l}")
    ds2_f, ds2_b, g["ln_mix_g"], g["ln_mix_b"] = _ln_bwd(sv["h"], sv["mix"], ds_f, dh1, p["ln_mix_g"], p["ln_mix_b"], f"dlnmix{l}")
    g["w_mix_out"] = _mm(sv["ub"], ds2_b, "tn", F32, 1024, 1024, 2048, f"dwmo{l}")
    du = _mm(ds2_b, w["wmo"], "nt", BF, 1024, 1024, 1024, f"du{l}")
    proj = sv["proj"]
    gcb = _OFF["gates"] // 1024

    def gate_bwd(ga, gb, ya_, yb_, du_):
        _, vjp = jax.vjp(lambda a, b, c, d: _sigmoid(a) * c + _sigmoid(b) * d, ga, gb, ya_.astype(F32), yb_.astype(F32))
        dga, dgb, dya, dyb = vjp(du_.astype(F32))
        return (jnp.concatenate([dga, dgb], axis=1), dya, dyb), ()
    (dproj, dya, dyb), _ = _rows(
        gate_bwd, [(proj, gcb, 1024), (proj, gcb + 1, 1024), (sv["ya"], 0, 1024), (sv["yb"], 0, 1024), (du, 0, 1024)], [],
        [(2048, BF, NP), (D_MODEL, BF), (D_MODEL, BF)], [], ROW_TILE, f"dgate{l}")
    g["w_att_out"] = _mm(sv["att"], dyb, "tn", F32, 1024, 1024, 2048, f"dwao{l}")
    datt = _mm(dyb, w["wao"], "nt", BF, 1024, 1024, 1024, f"datt{l}")
    g["w_ssd_out"] = _mm(sv["yn"], dya, "tn", F32, 1024, 1024, 2048, f"dwso{l}")
    dyn = _mm(dya, w["wso"], "nt", F32, 1024, 1024, 1024, f"dyn{l}")
    hosted = list(hosted) + (list(early_hook(g)) if early_hook else [])
    dproj, dkp, dkc, dvp, dvc, g["att_sinks"] = _attn_bwd(proj, sv["k2"], sv["v2"], p["att_sinks"], datt, dproj, f"dattn{l}")
    res = _ssd_bwd(sv["pre"], proj, sv["dt_t"], p["dt_bias"], p["a_log"], p["d_skip_c"], p["ssd_norm_w"], sv["s_in"], dyn,
                   dproj, f"dssd{l}", (hosted, False) if hosted else None)
    (dpre, dproj, ddt_t, g["dt_bias"], g["a_log"], ddsk, g["ssd_norm_w"]) = res[:7]
    if hosted:
        g["hosted"] = res[7]
    g["d_skip"] = jnp.sum(ddsk.reshape(SSD_HEADS, 64), axis=1)
    dproj, g["conv_w"], g["conv_b"] = _conv_bwd(dpre, proj, p["conv_w"], dproj, 0, f"dconv{l}")
    length = proj.shape[0]
    tail = jnp.concatenate([_fold_heads(dkc, dkp).astype(BF), _fold_heads(dvc, dvp).astype(BF),
                            jnp.transpose(ddt_t).astype(BF), jnp.zeros((length, _PAD), BF)], axis=1)
    dproj = lax.dynamic_update_slice(dproj, tail, (0, _OFF["k"]))
    g["w_in"] = _mm(sv["hb"], dproj, "tn", F32, 1024, 2176, 1024, f"dwin{l}")
    late = list(late_hook(g)) if late_hook else []
    dh = _mm(dproj, w["win"], "nt", F32, 1024, 1024, 2176, f"dh{l}", (late, False) if late else None)
    if late:
        dh, g["late_hosted"] = dh
    return g, ds2_f, dh


def kernel(x, ln_in_g, ln_in_b, w_in, conv_w, conv_b, dt_bias, a_log, d_skip, ssd_norm_w, att_sinks, w_ssd_out, w_att_out, w_mix_out, ln_mix_g, ln_mix_b, w_ffn_gate, w_ffn_up, w_ffn_down, ln_ffn_g, ln_ffn_b, loss_target, m_ln_in_g, m_ln_in_b, m_w_in, m_conv_w, m_conv_b, m_dt_bias, m_a_log, m_d_skip, m_ssd_norm_w, m_att_sinks, m_w_ssd_out, m_w_att_out, m_w_mix_out, m_ln_mix_g, m_ln_mix_b, m_w_ffn_gate, m_w_ffn_up, m_w_ffn_down, m_ln_ffn_g, m_ln_ffn_b, v_ln_in_g, v_ln_in_b, v_w_in, v_conv_w, v_conv_b, v_dt_bias, v_a_log, v_d_skip, v_ssd_norm_w, v_att_sinks, v_w_ssd_out, v_w_att_out, v_w_mix_out, v_ln_mix_g, v_ln_mix_b, v_w_ffn_gate, v_w_ffn_up, v_w_ffn_down, v_ln_ffn_g, v_ln_ffn_b):
    env = dict(locals())
    wts = {n: env[n] for n in WEIGHTS}
    mom1 = {n: env["m_" + n] for n in WEIGHTS}
    mom2 = {n: env["v_" + n] for n in WEIGHTS}
    ci = lax.axis_index("c")
    chip = 2 * lax.axis_index("x") + lax.axis_index("y")
    xs_ = x[0]
    tgt = loss_target[0]

    COLS = ("w_in", "w_ffn_gate", "w_ffn_up")

    def half(t):
        return lax.dynamic_slice_in_dim(t, ci * (t.shape[0] // 2), t.shape[0] // 2, axis=0)

    def assemble(name, parts):
        if name == "w_in":
            return _gather_cols(parts, _IN_RANGES, _PAD)
        if name in COLS:
            return _gather_cols(parts, ((0, 4 * parts.shape[3]),))
        return jnp.concatenate([parts[h, s] for s in range(4) for h in range(2)], axis=0)

    def split(name, gfull):
        if name == "w_in":
            return _scatter_cols(gfull, _IN_RANGES, IN_DIM // 4)
        if name in COLS:
            return _scatter_cols(gfull, ((0, gfull.shape[1]),), gfull.shape[1] // 4)
        hr = gfull.shape[0] // 8
        return jnp.stack([jnp.stack([gfull[(2 * s + h) * hr:(2 * s + h + 1) * hr] for s in range(4)]) for h in range(2)])

    mine = [[half(wts[n][l].astype(BF)) for n in BIG] for l in range(DEPTH)]
    conv_all, = _with_own(_exchange([conv_w], "xy", True, "gather_conv"), [conv_w], chip)
    conv_full = jnp.transpose(conv_all, (1, 2, 0, 3)).reshape(DEPTH, SSD_CONV, CONV_DIM)

    KEYS = {"w_in": "win", "w_ssd_out": "wso", "w_att_out": "wao", "w_mix_out": "wmo", "w_ffn_down": "wd"}

    def layer_weights(l, names, by_chip):
        own = [mine[l][BIG.index(n)] for n in names]
        by_chip = _with_own(by_chip, own, chip)
        parts = _with_own(_exchange(by_chip, "c", True, f"gather_cores{l}_{names[0]}"), by_chip, ci)
        fw = {n: assemble(n, part) for n, part in zip(names, parts)}
        w = {KEYS[n]: fw[n] for n in names if n in KEYS}
        if "w_ffn_gate" in fw:
            w["wgu"] = jnp.concatenate([fw["w_ffn_gate"], fw["w_ffn_up"]], axis=1)
        return w

    layer_p = []
    for l in range(DEPTH):
        layer_p.append({
            "conv_w": conv_full[l], "conv_b": conv_b[l][None],
            "dt_bias": dt_bias[l][:, None], "a_log": a_log[l][:, None],
            "d_skip_c": jnp.repeat(d_skip[l], 64)[None], "ssd_norm_w": ssd_norm_w[l][None],
            "att_sinks": att_sinks[l][None], "ln_mix_g": ln_mix_g[l][None], "ln_mix_b": ln_mix_b[l][None],
            "ln_ffn_g": ln_ffn_g[l][None], "ln_ffn_b": ln_ffn_b[l][None],
        })

    def ln_in_fn(x_, g_, b_):
        y = _layer_norm(x_, g_, b_)
        return (y, y), ()
    (h, hb), _ = _rows(ln_in_fn, [(xs_, 0, D_MODEL)], [ln_in_g[None], ln_in_b[None]], [(D_MODEL, F32), (D_MODEL, BF)], [], ROW_TILE, "ln_in")
    rest = tuple(n for n in BIG if n != "w_in")
    layer_w = [layer_weights(0, ("w_in",), _exchange([mine[0][BIG.index("w_in")]], "xy", True, "gather_chips0"))]
    behind_ssd = [i for i, n in enumerate(BIG) if n in COLS]
    behind_attn = [i for i, n in enumerate(BIG) if n not in COLS]
    saved = []
    for l in range(DEPTH):
        nxt = l + 1 < DEPTH
        sv = _layer_fwd(l, h, hb, layer_w[l], layer_p[l], ([mine[l + 1][i] for i in behind_ssd], True) if nxt else None,
                        ([mine[l + 1][i] for i in behind_attn], True) if nxt else None,
                        ([mine[l][BIG.index(n)] for n in rest], True) if l == 0 else None,
                        functools.partial(layer_weights, l, rest))
        saved.append(sv)
        if nxt:
            got = dict(zip(behind_ssd + behind_attn, sv["hosted"] + sv["hosted_attn"]))
            layer_w.append(layer_weights(l + 1, BIG, [got[i] for i in range(len(BIG))]))
            h, hb = _ln_fwd(sv["h1"], sv["ffn"], layer_p[l]["ln_ffn_g"], layer_p[l]["ln_ffn_b"], f"lnffn{l}")

    def loss_fn(a_, r_, t_, g_, b_):
        y, vjp = jax.vjp(_layer_norm, ALPHA * a_ + r_, g_, b_)
        err = y - t_
        ds, dg, db = vjp(err * (1.0 / D_MODEL))
        part = 0.5 * jnp.sum(jnp.mean(err * err, axis=-1, keepdims=True), axis=0, keepdims=True)
        return (ds, ds), (dg, db, jnp.broadcast_to(part, (1, BLK)))
    sv = saved[-1]
    (ds_f, ds_b), (dg_last, db_last, loss_part) = _rows(
        loss_fn, [(sv["h1"], 0, D_MODEL), (sv["ffn"], 0, D_MODEL), (tgt, 0, D_MODEL)],
        [layer_p[-1]["ln_ffn_g"], layer_p[-1]["ln_ffn_b"]], [(D_MODEL, F32), (D_MODEL, BF)],
        [(1, D_MODEL), (1, D_MODEL), (1, BLK)], ROW_TILE, "loss")
    loss = lax.psum(loss_part[0, 0], ("x", "y", "c"))

    def reduce_cores(l, g, names):
        src = [split(n, g[n]) for n in names]
        recv = _exchange([s.astype(BF) for s in src], "c", False, f"reduce_cores{l}_{names[0]}")
        return {n: _sum_pieces(r, o, True, ci, f"sum_cores{l}_{n}") for n, r, o in zip(names, recv, src)}

    def bf16_of(sums, names):
        return [sums[n][1] for n in names]

    def add_chips(l, sums, from_chips):
        return [_sum_pieces(from_chips[n], sums[n][0], True, chip, f"sum_chips{l}_{n}")[0] for n in BIG]

    EARLY = tuple(n for n in BIG if n != "w_in")
    grads = [None] * DEPTH
    chip_sums = [None] * DEPTH
    for l in reversed(range(DEPTH)):
        sv = saved[l]
        if l == DEPTH - 1:
            lg, lb = dg_last, db_last
        else:
            ds_f, ds_b, lg, lb = _ln_bwd(sv["h1"], sv["ffn"], d1, d2, layer_p[l]["ln_ffn_g"], layer_p[l]["ln_ffn_b"], f"dlnffn{l}")
        sums = {}

        def early_hook(g, l=l, sums=sums):
            sums.update(reduce_cores(l, g, EARLY))
            return bf16_of(sums, EARLY)

        def late_hook(g, l=l, sums=sums):
            sums.update(reduce_cores(l, g, ("w_in",)))
            return bf16_of(sums, ("w_in",))
        if l == 0:
            g, d1, d2 = _layer_bwd(l, sv, layer_w[l], layer_p[l], ds_f, ds_b, bf16_of(chip_sums[l + 1], BIG), early_hook, late_hook)
        else:
            g, d1, d2 = _layer_bwd(l, sv, layer_w[l], layer_p[l], ds_f, ds_b)
            sums.update(reduce_cores(l, g, BIG))
        g["ln_ffn_g"], g["ln_ffn_b"] = lg, lb
        grads[l] = g
        chip_sums[l] = sums

    def ln_in_bwd(x_, d1_, d2_, g_, b_):
        _, vjp = jax.vjp(_layer_norm, x_, g_, b_)
        dx, dg, db = vjp(ALPHA * d1_ + d2_)
        return (dx,), (dg, db)
    (grad_x,), (g_ln_in_g, g_ln_in_b) = _rows(
        ln_in_bwd, [(xs_, 0, D_MODEL), (d1, 0, D_MODEL), (d2, 0, D_MODEL)], [ln_in_g[None], ln_in_b[None]],
        [(D_MODEL, F32)], [(1, D_MODEL), (1, D_MODEL)], ROW_TILE, "dln_in")

    carried = grads[0]["hosted"]
    from_chips = dict(zip(EARLY, carried[len(BIG):]))
    from_chips["w_in"], = grads[0]["late_hosted"]
    reduced = add_chips(0, chip_sums[0], from_chips) + add_chips(1, chip_sums[1], dict(zip(BIG, carried[:len(BIG)])))
    halves = _with_own(_exchange(reduced, "c", True, "swap_halves"), reduced, ci)
    shards = [t.reshape(2 * t.shape[1], t.shape[2]) for t in halves]
    big_grad = {n: jnp.stack([shards[l * len(BIG) + i] for l in range(DEPTH)]).reshape(wts[n].shape) for i, n in enumerate(BIG)}

    small_g = {"ln_in_g": g_ln_in_g[0], "ln_in_b": g_ln_in_b[0]}
    for n in SMALL[2:]:
        small_g[n] = jnp.stack([grads[l][n].reshape(wts[n].shape[1:] if n != "conv_w" else (SSD_CONV, CONV_DIM)) for l in range(DEPTH)])
    flat = jnp.concatenate([small_g[n].reshape(-1) for n in SMALL])
    n_small = flat.shape[0]
    width = -(-n_small // 1024) * 1024
    flat = jnp.pad(flat, (0, width - n_small)).reshape(8, width // 8)
    gathered, = _with_own(_exchange([flat], "all", True, "gather_small"), [flat], 2 * chip + ci)
    total = _sum_slots(gathered, "sum_small").reshape(-1)
    small_grad, off = {}, 0
    for n in SMALL:
        shp = small_g[n].shape
        cnt = math.prod(shp)
        small_grad[n] = total[off:off + cnt].reshape(shp)
        off += cnt
    small_grad["conv_w"] = lax.dynamic_slice_in_dim(small_grad["conv_w"], chip * (CONV_DIM // 4), CONV_DIM // 4, axis=2)

    out_g, out_d, out_m, out_v = {}, {}, {}, {}
    for n in BIG:
        shp = wts[n].shape
        two_d = lambda t: t.reshape(shp[0] * shp[1], shp[2])
        d_, m_, v_ = _adamw(two_d(wts[n]), two_d(big_grad[n]), two_d(mom1[n]), two_d(mom2[n]), f"adamw_{n}")
        out_g[n], out_d[n], out_m[n], out_v[n] = big_grad[n], d_.reshape(shp), m_.reshape(shp), v_.reshape(shp)

    def flat_small(d):
        f = jnp.concatenate([d[n].reshape(-1) for n in SMALL])
        return jnp.pad(f, (0, swidth - f.shape[0])).reshape(8, swidth // 8)
    n_sw = sum(math.prod(wts[n].shape) for n in SMALL)
    swidth = -(-n_sw // 1024) * 1024
    d_, m_, v_ = _adamw(flat_small(wts), flat_small(small_grad), flat_small(mom1), flat_small(mom2), "adamw_small")
    off = 0
    for n in SMALL:
        shp = wts[n].shape
        cnt = math.prod(shp)
        out_g[n] = small_grad[n]
        out_d[n] = d_.reshape(-1)[off:off + cnt].reshape(shp)
        out_m[n] = m_.reshape(-1)[off:off + cnt].reshape(shp)
        out_v[n] = v_.reshape(-1)[off:off + cnt].reshape(shp)
        off += cnt

    return (loss, grad_x[None], *[out_g[n] for n in WEIGHTS], *[out_d[n] for n in WEIGHTS],
            *[out_m[n] for n in WEIGHTS], *[out_v[n] for n in WEIGHTS])
```

```python
import functools
import itertools
import math

import jax
import jax.numpy as jnp
from jax import lax
from jax.experimental import pallas as pl
from jax.experimental.pallas import tpu as pltpu

F32 = jnp.float32
BF = jnp.bfloat16

D_MODEL = 1024
DEPTH = 2
ATT_HEADS = 16
ATT_HEAD_DIM = 64
BLK = 128
SSD_D_INNER = 2048
SSD_HEADS = 32
SSD_GROUPS = 4
SSD_STATE = 128
SSD_CONV = 4
BC_DIM = 512
CONV_DIM = 3072
FFN_HIDDEN = 2816
IN_DIM = 8480
LN_EPS = 1e-5
RMS_EPS = 1e-5
ALPHA = (2 * DEPTH) ** 0.25
ADAM_LR = 0.001
ADAM_B1 = 0.9
ADAM_B2 = 0.999
ADAM_EPS = 1e-08
ADAM_WD = 0.01
ADAM_STEP = 10

_PACK = (("gates", 6432, 2048), ("z", 1280, 2048), ("q", 0, 1024), ("xs", 3328, 2048), ("B", 5376, 512),
         ("C", 5888, 512), ("k", 1024, 128), ("v", 1152, 128), ("dt", 6400, 32))
NP = 8704
_OFF = {}
_o = 0
for _n, _, _w in _PACK:
    _OFF[_n] = _o
    _o += _w
_PAD = NP - _o

VMEM_LIMIT_BYTES = 56 * 1024 * 1024
ROW_TILE = 512
BIG = ("w_in", "w_ssd_out", "w_att_out", "w_mix_out", "w_ffn_gate", "w_ffn_up", "w_ffn_down")
WEIGHTS = ("ln_in_g", "ln_in_b", "w_in", "conv_w", "conv_b", "dt_bias", "a_log", "d_skip", "ssd_norm_w", "att_sinks",
           "w_ssd_out", "w_att_out", "w_mix_out", "ln_mix_g", "ln_mix_b", "w_ffn_gate", "w_ffn_up", "w_ffn_down",
           "ln_ffn_g", "ln_ffn_b")
SMALL = tuple(n for n in WEIGHTS if n not in BIG)


def _params(n_grid):
    return pltpu.CompilerParams(dimension_semantics=("arbitrary",) * n_grid, vmem_limit_bytes=VMEM_LIMIT_BYTES)


def _dot(a, b, ca, cb):
    return lax.dot_general(a.astype(BF), b.astype(BF), (((ca,), (cb,)), ((), ())), preferred_element_type=F32)


@jax.custom_vjp
def _nn(a, b):
    return _dot(a, b, 1, 0)


def _nn_f(a, b):
    return _dot(a, b, 1, 0), (a.astype(BF), b.astype(BF))


def _nn_b(res, g):
    a, b = res
    return _dot(g, b, 1, 1), _dot(a, g, 0, 0)


_nn.defvjp(_nn_f, _nn_b)


@jax.custom_vjp
def _nt(a, b):
    return _dot(a, b, 1, 1)


def _nt_f(a, b):
    return _dot(a, b, 1, 1), (a.astype(BF), b.astype(BF))


def _nt_b(res, g):
    a, b = res
    return _dot(g, b, 1, 0), _dot(g, a, 0, 0)


_nt.defvjp(_nt_f, _nt_b)


@jax.custom_vjp
def _tn(a, b):
    return _dot(a, b, 0, 0)


def _tn_f(a, b):
    return _dot(a, b, 0, 0), (a.astype(BF), b.astype(BF))


def _tn_b(res, g):
    a, b = res
    return _dot(b, g, 1, 1), _dot(a, g, 1, 0)


_tn.defvjp(_tn_f, _tn_b)


def _sigmoid(x):
    return 0.5 * jnp.tanh(0.5 * x) + 0.5


def _silu(x):
    return x * _sigmoid(x)


def _layer_norm(s, g, b):
    mu = jnp.mean(s, axis=-1, keepdims=True)
    sc = s - mu
    var = jnp.mean(sc * sc, axis=-1, keepdims=True)
    return sc * lax.rsqrt(var + LN_EPS) * g + b


def _ssd_group(pre_x, pre_b, pre_c, z, dtr, dtb, alog, dsk, nw, state):
    t = pre_x.shape[0]
    xs, bm, cm = _silu(pre_x.astype(F32)), _silu(pre_b.astype(F32)), _silu(pre_c.astype(F32))
    dt = jax.nn.softplus(dtr + dtb)
    da = dt * (-jnp.exp(alog))
    row = lax.broadcasted_iota(jnp.int32, (t, t), 0)
    col = lax.broadcasted_iota(jnp.int32, (t, t), 1)
    upper = (row <= col).astype(F32)
    cum = jnp.dot(da, upper, precision=lax.Precision.HIGHEST, preferred_element_type=F32)
    tot = jnp.sum(da, axis=1, keepdims=True)
    cb = _nt(cm, bm)
    tril = row >= col
    first = lax.broadcasted_iota(jnp.int32, (t, BLK), 1) < 64
    first_row = lax.broadcasted_iota(jnp.int32, (1, BLK), 1) < 64

    def col_form(r):
        return jnp.broadcast_to(r, (t, t)).T

    ys, new_state = [], []
    for p in range(4):
        h0, h1 = 2 * p, 2 * p + 1
        sl = slice(BLK * p, BLK * (p + 1))
        x_pair = xs[:, sl] * jnp.where(first, col_form(dt[h0:h0 + 1]), col_form(dt[h1:h1 + 1]))
        y_pair = None
        cc = []
        for h, keep in ((h0, first), (h1, jnp.logical_not(first))):
            cr = jnp.broadcast_to(cum[h:h + 1], (t, t))
            cc.append(cr.T)
            decay = jnp.exp(jnp.where(tril, cc[-1] - cr, -1e30))
            y_h = _nn(decay * cb, jnp.where(keep, x_pair, 0.0))
            y_pair = y_h if y_pair is None else y_pair + y_h
        s_pair = state[:, sl]
        y_pair = y_pair + _nn(cm, s_pair) * jnp.where(first, jnp.exp(cc[0]), jnp.exp(cc[1]))
        to_end = jnp.where(first, jnp.exp(tot[h0:h0 + 1] - cc[0]), jnp.exp(tot[h1:h1 + 1] - cc[1]))
        chunk_decay = jnp.where(first_row, jnp.exp(tot[h0:h0 + 1]), jnp.exp(tot[h1:h1 + 1]))
        new_state.append(s_pair * chunk_decay + _tn(bm, x_pair * to_end))
        ys.append(y_pair + dsk[:, sl] * xs[:, sl])
    y = jnp.concatenate(ys, axis=1) * _silu(z)
    y = y * lax.rsqrt(jnp.mean(y * y, axis=-1, keepdims=True) + RMS_EPS) * nw
    return y, jnp.concatenate(new_state, axis=1)


def _attn_block(q, kp, kc, vp, vc, sinks, n, kv):
    t = q.shape[0]
    kb = jnp.concatenate([kp, kc], axis=0)
    vb = jnp.concatenate([vp, vc], axis=0)
    qi = lax.broadcasted_iota(jnp.int32, (t, 2 * t), 0)
    kj = lax.broadcasted_iota(jnp.int32, (t, 2 * t), 1)
    rel = qi + t - kj
    valid = (rel >= 0) & (rel < t) & ((n * t - t + kj) >= 0)
    relf = rel.astype(F32)
    first = lax.broadcasted_iota(jnp.int32, (t, BLK), 1) < 64
    lane16 = lax.broadcasted_iota(jnp.int32, (1, ATT_HEADS), 1)
    outs = []
    for p in range(4):
        qp = q[:, BLK * p:BLK * (p + 1)] * (ATT_HEAD_DIM ** -0.5)
        o_pair = None
        for half, keep in enumerate((first, jnp.logical_not(first))):
            h = kv * 8 + 2 * p + half
            s = _nt(jnp.where(keep, qp, 0.0), kb)
            slope = jnp.exp((-8.0 * math.log(2.0) / ATT_HEADS) * jnp.asarray(h + 1, F32))
            s = jnp.where(valid, s - slope * relf, -1e30)
            sink = jnp.sum(jnp.where(lane16 == h, sinks, 0.0), axis=1, keepdims=True)
            m = lax.stop_gradient(jnp.maximum(jnp.max(s, axis=1, keepdims=True), sink))
            e = jnp.exp(s - m)
            den = jnp.sum(e, axis=1, keepdims=True) + jnp.exp(sink - m)
            o = _nn(e * (1.0 / den), vb)
            o_pair = o if o_pair is None else jnp.where(first, o_pair, o)
        outs.append(o_pair)
    return jnp.concatenate(outs, axis=1)


def _chip_exchange(src, out, send_sems, recv_sems, bcast):
    x, y, c = lax.axis_index("x"), lax.axis_index("y"), lax.axis_index("c")
    me = 2 * x + y

    def copy(a, d, slot):
        j = jnp.bitwise_xor(me, d)
        return pltpu.make_async_remote_copy(
            src_ref=src[a] if bcast else src[a].at[j], dst_ref=out[a].at[slot],
            send_sem=send_sems.at[a * 4 + d], recv_sem=recv_sems.at[a * 4 + d],
            device_id=(j // 2, j % 2, c), device_id_type=pl.DeviceIdType.MESH)

    pairs = [(a, d) for a in range(len(src)) for d in range(1, 4)]

    def start():
        for a, d in pairs:
            copy(a, d, me).start()

    def wait():
        for a, d in pairs:
            copy(a, d, jnp.bitwise_xor(me, d)).wait_recv()
        for a, d in pairs:
            copy(a, d, me).wait_send()

    return start, wait


def _mm(a, b, mode, out_dtype, tm, tn, tk, name, hosted=None):
    if mode == "nn":
        (m, k), (k2, n) = a.shape, b.shape
    elif mode == "nt":
        (m, k), (n, k2) = a.shape, b.shape
    else:
        (k, m), (k2, n) = a.shape, b.shape
    assert k == k2, (a.shape, b.shape, mode)
    tm, tn, tk = min(tm, m), min(tn, n), min(tk, k)
    assert m % tm == 0 and n % tn == 0 and k % tk == 0, (m, n, k, tm, tn, tk)
    nk = k // tk
    grid = (m // tm, n // tn, nk)
    ca, cb = {"nn": (1, 0), "nt": (1, 1), "tn": (0, 0)}[mode]
    h_arrays, h_bcast = hosted if hosted else ((), False)
    nh = len(h_arrays)

    def body(*refs):
        a_ref, b_ref = refs[:2]
        o_ref = refs[2 + nh]
        scratch = refs[3 + 2 * nh:]
        step = [pl.program_id(ax) for ax in range(3)]
        if nh:
            start, wait = _chip_exchange(refs[2:2 + nh], refs[3 + nh:3 + 2 * nh], scratch[-2], scratch[-1], h_bcast)

            @pl.when((step[0] == 0) & (step[1] == 0) & (step[2] == 0))
            def _():
                start()

        part = _dot(a_ref[...], b_ref[...], ca, cb)
        if nk == 1:
            o_ref[...] = part.astype(o_ref.dtype)
        else:
            acc_ref = scratch[0]

            @pl.when(step[2] == 0)
            def _():
                acc_ref[...] = part

            @pl.when(step[2] > 0)
            def _():
                acc_ref[...] += part

            @pl.when(step[2] == nk - 1)
            def _():
                o_ref[...] = acc_ref[...].astype(o_ref.dtype)

        if nh:
            @pl.when((step[0] == grid[0] - 1) & (step[1] == grid[1] - 1) & (step[2] == nk - 1))
            def _():
                wait()

    a_spec = pl.BlockSpec((tk, tm), lambda i, j, kk: (kk, i)) if mode == "tn" else pl.BlockSpec((tm, tk), lambda i, j, kk: (i, kk))
    b_spec = pl.BlockSpec((tn, tk), lambda i, j, kk: (j, kk)) if mode == "nt" else pl.BlockSpec((tk, tn), lambda i, j, kk: (kk, j))
    any_spec = pl.BlockSpec(memory_space=pl.ANY)
    res = pl.pallas_call(
        body, name=name, grid=grid,
        in_specs=[a_spec, b_spec] + [any_spec] * nh,
        out_specs=[pl.BlockSpec((tm, tn), lambda i, j, kk: (i, j))] + [any_spec] * nh,
        out_shape=[jax.ShapeDtypeStruct((m, n), out_dtype)]
        + [jax.ShapeDtypeStruct(((4,) + s.shape) if h_bcast else s.shape, s.dtype) for s in h_arrays],
        scratch_shapes=([] if nk == 1 else [pltpu.VMEM((tm, tn), F32)])
        + ([pltpu.SemaphoreType.DMA((4 * nh,)), pltpu.SemaphoreType.DMA((4 * nh,))] if nh else []),
        compiler_params=_params(3),
    )(a, b, *h_arrays)
    return (res[0], list(res[1:])) if nh else res[0]


def _rows(fn, rows, params, outs, accs, tile, name):
    length = rows[0][0].shape[0]
    tile = min(tile, length)
    assert length % tile == 0
    nr, npar, no = len(rows), len(params), len(outs)

    def body(*refs):
        vals = [r[...] for r in refs[:nr + npar]]
        o, a = fn(*vals)
        for ref, val in zip(refs[nr + npar:nr + npar + no], o):
            ref[...] = val.astype(ref.dtype)
        i = pl.program_id(0)
        for ref, val in zip(refs[nr + npar + no:], a):
            @pl.when(i == 0)
            def _(ref=ref, val=val):
                ref[...] = val

            @pl.when(i > 0)
            def _(ref=ref, val=val):
                ref[...] += val

    in_specs = [pl.BlockSpec((tile, w), functools.partial(lambda i, cb: (i, cb), cb=cb)) for _, cb, w in rows]
    in_specs += [pl.BlockSpec(p.shape, lambda i: (0, 0)) for p in params]
    outs = [o if len(o) == 3 else (o[0], o[1], o[0]) for o in outs]
    out_specs = [pl.BlockSpec((tile, w), lambda i: (i, 0)) for w, _, _ in outs]
    out_specs += [pl.BlockSpec((r, w), lambda i: (0, 0)) for r, w in accs]
    out_shape = [jax.ShapeDtypeStruct((length, full), dt) for _, dt, full in outs]
    out_shape += [jax.ShapeDtypeStruct((r, w), F32) for r, w in accs]
    res = pl.pallas_call(
        body, name=name, grid=(length // tile,), in_specs=in_specs, out_specs=out_specs, out_shape=out_shape,
        compiler_params=_params(1),
    )(*[r[0] for r in rows], *params)
    return res[:no], res[no:]


CONV_SUB = 32


def _conv_fwd(proj, conv_w, conv_b, name, hosted=None):
    length = proj.shape[0]
    tl, cw = min(1024, length), 512
    cb0 = _OFF["xs"] // cw
    grid = (CONV_DIM // cw, length // tl)
    h_arrays, h_bcast = hosted if hosted else ((), False)
    nh = len(h_arrays)

    def body(*refs):
        u_ref, halo_ref, w_ref, b_ref = refs[:4]
        o_ref = refs[4 + nh]
        win = refs[5 + 2 * nh]
        j, i = pl.program_id(0), pl.program_id(1)
        if nh:
            start, wait = _chip_exchange(refs[4:4 + nh], refs[5 + nh:5 + 2 * nh], refs[-2], refs[-1], h_bcast)

            @pl.when((j == 0) & (i == 0))
            def _():
                start()

            @pl.when((j == grid[0] - 1) & (i == grid[1] - 1))
            def _():
                wait()

        win[0:8, :] = jnp.where(i > 0, halo_ref[...], 0.0)
        win[8:8 + tl, :] = u_ref[...]
        taps = [w_ref[kk:kk + 1, :] for kk in range(SSD_CONV)]
        bias = b_ref[...]
        for r in range(0, tl, CONV_SUB):
            acc = bias + taps[0] * win[5 + r:5 + r + CONV_SUB, :]
            for kk in range(1, SSD_CONV):
                acc = acc + taps[kk] * win[5 + kk + r:5 + kk + r + CONV_SUB, :]
            o_ref[r:r + CONV_SUB, :] = acc.astype(o_ref.dtype)

    any_spec = pl.BlockSpec(memory_space=pl.ANY)
    res = pl.pallas_call(
        body, name=name, grid=grid,
        in_specs=[pl.BlockSpec((tl, cw), lambda j, i: (i, cb0 + j)),
                  pl.BlockSpec((8, cw), lambda j, i: (jnp.maximum(i * (tl // 8) - 1, 0), cb0 + j)),
                  pl.BlockSpec((SSD_CONV, cw), lambda j, i: (0, j)),
                  pl.BlockSpec((1, cw), lambda j, i: (0, j))] + [any_spec] * nh,
        out_specs=[pl.BlockSpec((tl, cw), lambda j, i: (i, j))] + [any_spec] * nh,
        out_shape=[jax.ShapeDtypeStruct((length, CONV_DIM), BF)]
        + [jax.ShapeDtypeStruct(((4,) + s.shape) if h_bcast else s.shape, s.dtype) for s in h_arrays],
        scratch_shapes=[pltpu.VMEM((8 + tl, cw), F32)]
        + ([pltpu.SemaphoreType.DMA((4 * nh,)), pltpu.SemaphoreType.DMA((4 * nh,))] if nh else []),
        compiler_params=_params(2),
    )(proj, proj, conv_w, conv_b, *h_arrays)
    return (res[0], list(res[1:])) if nh else res[0]


def _conv_bwd(dpre, proj, conv_w, dproj, part, name):
    length = proj.shape[0]
    width = dpre.shape[1]
    tl, cw = min(1024, length), 512
    cb0 = _OFF["xs"] // cw + part
    n_t = length // tl

    def body(d_ref, dnext_ref, u_ref, halo_ref, w_ref, _, du_ref, dw_ref, db_ref, dwin, uwin):
        i = pl.program_id(1)
        dwin[0:tl, :] = d_ref[...].astype(F32)
        dwin[tl:tl + 8, :] = jnp.where(i < n_t - 1, dnext_ref[...].astype(F32)[0:8], 0.0)
        uwin[0:8, :] = jnp.where(i > 0, halo_ref[...], 0.0)
        uwin[8:8 + tl, :] = u_ref[...]
        taps = [w_ref[kk:kk + 1, :] for kk in range(SSD_CONV)]
        sub = CONV_SUB // 2
        acc_w = [jnp.zeros((sub, cw), F32) for _ in range(SSD_CONV)]
        acc_b = jnp.zeros((sub, cw), F32)
        for r in range(0, tl, sub):
            d = dwin[r:r + sub, :]
            du = taps[3] * d
            for kk in range(SSD_CONV - 1):
                du = du + taps[kk] * dwin[3 - kk + r:3 - kk + r + sub, :]
            du_ref[r:r + sub, :] = du.astype(du_ref.dtype)
            for kk in range(SSD_CONV):
                acc_w[kk] = acc_w[kk] + d * uwin[5 + kk + r:5 + kk + r + sub, :]
            acc_b = acc_b + d
        dw = jnp.concatenate([jnp.sum(a, axis=0, keepdims=True) for a in acc_w], axis=0)
        db = jnp.sum(acc_b, axis=0, keepdims=True)

        @pl.when(i == 0)
        def _():
            dw_ref[...] = dw
            db_ref[...] = db

        @pl.when(i > 0)
        def _():
            dw_ref[...] += dw
            db_ref[...] += db

    return pl.pallas_call(
        body, name=name, grid=(width // cw, n_t), input_output_aliases={5: 0},
        in_specs=[pl.BlockSpec((tl, cw), lambda j, i: (i, j)),
                  pl.BlockSpec((16, cw), lambda j, i: (jnp.minimum((i + 1) * (tl // 16), length // 16 - 1), j)),
                  pl.BlockSpec((tl, cw), lambda j, i: (i, cb0 + j)),
                  pl.BlockSpec((8, cw), lambda j, i: (jnp.maximum(i * (tl // 8) - 1, 0), cb0 + j)),
                  pl.BlockSpec((SSD_CONV, cw), lambda j, i: (0, part + j)),
                  pl.BlockSpec(memory_space=pl.ANY)],
        out_specs=[pl.BlockSpec((tl, cw), lambda j, i: (i, cb0 + j)),
                   pl.BlockSpec((SSD_CONV, cw), lambda j, i: (0, j)),
                   pl.BlockSpec((1, cw), lambda j, i: (0, j))],
        out_shape=[jax.ShapeDtypeStruct((length, NP), BF),
                   jax.ShapeDtypeStruct((SSD_CONV, width), F32),
                   jax.ShapeDtypeStruct((1, width), F32)],
        scratch_shapes=[pltpu.VMEM((tl + 8, cw), F32), pltpu.VMEM((8 + tl, cw), F32)],
        compiler_params=_params(2),
    )(dpre, dpre, proj, proj, conv_w, dproj)


def _ssd_in_specs(rev, nc):
    def cidx(c):
        return nc - 1 - c if rev else c
    whole = lambda c: (0, 0)
    return [pl.BlockSpec((BLK, SSD_D_INNER), lambda c: (cidx(c), 0)),
            pl.BlockSpec((BLK, BC_DIM), lambda c: (cidx(c), SSD_D_INNER // BC_DIM)),
            pl.BlockSpec((BLK, BC_DIM), lambda c: (cidx(c), SSD_D_INNER // BC_DIM + 1)),
            pl.BlockSpec((BLK, SSD_D_INNER), lambda c: (cidx(c), _OFF["z"] // SSD_D_INNER)),
            pl.BlockSpec((SSD_HEADS, BLK), lambda c: (0, cidx(c))),
            pl.BlockSpec((SSD_HEADS, 1), whole),
            pl.BlockSpec((SSD_HEADS, 1), whole),
            pl.BlockSpec((1, SSD_D_INNER), whole),
            pl.BlockSpec((1, SSD_D_INNER), whole)]


def _group_args(g, px, pb, pc, z, dtr, dtb, alog, dsk, nw):
    wide, narrow, heads = slice(512 * g, 512 * (g + 1)), slice(BLK * g, BLK * (g + 1)), slice(8 * g, 8 * (g + 1))
    return (px[:, wide], pb[:, narrow], pc[:, narrow], z[:, wide], dtr[heads, :], dtb[heads, :], alog[heads, :],
            dsk[:, wide], nw[:, wide])


def _ssd_fwd(pre, proj, dt_t, dtb, alog, dsk, nw, name, hosted=None):
    length = pre.shape[0]
    nc = length // BLK
    h_arrays, h_bcast = hosted if hosted else ((), False)
    nh = len(h_arrays)

    def body(*refs):
        px, pb, pc, z, dtr, dtb_r, al_r, dsk_r, nw_r = refs[:9]
        y_ref, sin_ref = refs[9 + nh:11 + nh]
        st = refs[11 + 2 * nh]
        c = pl.program_id(0)
        if nh:
            start, wait = _chip_exchange(refs[9:9 + nh], refs[11 + nh:11 + 2 * nh], refs[-2], refs[-1], h_bcast)

        @pl.when(c == 0)
        def _():
            st[...] = jnp.zeros_like(st)
            if nh:
                start()

        for g in range(SSD_GROUPS):
            s_in = st[g]
            sin_ref[g] = s_in
            y, s_out = _ssd_group(*_group_args(g, px, pb, pc, z, dtr, dtb_r, al_r, dsk_r, nw_r), s_in)
            y_ref[:, 512 * g:512 * (g + 1)] = y.astype(y_ref.dtype)
            st[g] = s_out

        if nh:
            @pl.when(c == nc - 1)
            def _():
                wait()

    any_spec = pl.BlockSpec(memory_space=pl.ANY)
    res = pl.pallas_call(
        body, name=name, grid=(nc,), in_specs=_ssd_in_specs(False, nc) + [any_spec] * nh,
        out_specs=[pl.BlockSpec((BLK, SSD_D_INNER), lambda c: (c, 0)),
                   pl.BlockSpec((SSD_GROUPS, None, SSD_STATE, 512), lambda c: (0, c, 0, 0))] + [any_spec] * nh,
        out_shape=[jax.ShapeDtypeStruct((length, SSD_D_INNER), BF),
                   jax.ShapeDtypeStruct((SSD_GROUPS, nc, SSD_STATE, 512), F32)]
        + [jax.ShapeDtypeStruct(((4,) + s.shape) if h_bcast else s.shape, s.dtype) for s in h_arrays],
        scratch_shapes=[pltpu.VMEM((SSD_GROUPS, SSD_STATE, 512), F32)]
        + ([pltpu.SemaphoreType.DMA((4 * nh,)), pltpu.SemaphoreType.DMA((4 * nh,))] if nh else []),
        compiler_params=_params(1),
    )(pre, pre, pre, proj, dt_t, dtb, alog, dsk, nw, *h_arrays)
    return (res[0], res[1], list(res[2:])) if nh else (res[0], res[1])


def _ssd_bwd(pre, proj, dt_t, dtb, alog, dsk, nw, s_in, dy, dproj, name, hosted=None):
    length = pre.shape[0]
    nc = length // BLK
    h_arrays, h_bcast = hosted if hosted else ((), False)
    nh = len(h_arrays)

    def body(*refs):
        px, pb, pc, z, dtr, dtb_r, al_r, dsk_r, nw_r, sin_r, dy_r = refs[:11]
        dpre, dz, ddt, ddtb, dal, ddsk, dnw = refs[12 + nh:19 + nh]
        dst = refs[19 + 2 * nh]
        c = pl.program_id(0)
        if nh:
            start, wait = _chip_exchange(refs[12:12 + nh], refs[19 + nh:19 + 2 * nh], refs[-2], refs[-1], h_bcast)

        @pl.when(c == 0)
        def _():
            dst[...] = jnp.zeros_like(dst)
            if nh:
                start()

        for grp in range(SSD_GROUPS):
            wide, heads = slice(512 * grp, 512 * (grp + 1)), slice(8 * grp, 8 * (grp + 1))
            _, vjp = jax.vjp(_ssd_group, *_group_args(grp, px, pb, pc, z, dtr, dtb_r, al_r, dsk_r, nw_r), sin_r[grp])
            g = vjp((dy_r[:, wide], dst[grp]))
            dpre[:, wide] = g[0].astype(dpre.dtype)
            dpre[:, slice(SSD_D_INNER + BLK * grp, SSD_D_INNER + BLK * (grp + 1))] = g[1].astype(dpre.dtype)
            dpre[:, slice(SSD_D_INNER + BC_DIM + BLK * grp, SSD_D_INNER + BC_DIM + BLK * (grp + 1))] = g[2].astype(dpre.dtype)
            dz[:, wide] = g[3].astype(dz.dtype)
            ddt[heads, :] = g[4]
            dst[grp] = g[9]
            for ref, val, idx in ((ddtb, g[5], (heads, slice(None))), (dal, g[6], (heads, slice(None))),
                                  (ddsk, g[7], (slice(None), wide)), (dnw, g[8], (slice(None), wide))):
                @pl.when(c == 0)
                def _(ref=ref, val=val, idx=idx):
                    ref[idx] = val

                @pl.when(c > 0)
                def _(ref=ref, val=val, idx=idx):
                    ref[idx] += val

        if nh:
            @pl.when(c == nc - 1)
            def _():
                wait()

    rev = lambda c: nc - 1 - c
    whole = lambda c: (0, 0)
    any_spec = pl.BlockSpec(memory_space=pl.ANY)
    in_specs = _ssd_in_specs(True, nc) + [
        pl.BlockSpec((SSD_GROUPS, None, SSD_STATE, 512), lambda c: (0, rev(c), 0, 0)),
        pl.BlockSpec((BLK, SSD_D_INNER), lambda c: (rev(c), 0)),
        any_spec] + [any_spec] * nh
    res = pl.pallas_call(
        body, name=name, grid=(nc,), in_specs=in_specs, input_output_aliases={11: 1},
        out_specs=[pl.BlockSpec((BLK, CONV_DIM), lambda c: (rev(c), 0)),
                   pl.BlockSpec((BLK, SSD_D_INNER), lambda c: (rev(c), _OFF["z"] // SSD_D_INNER)),
                   pl.BlockSpec((SSD_HEADS, BLK), lambda c: (0, rev(c))),
                   pl.BlockSpec((SSD_HEADS, 1), whole),
                   pl.BlockSpec((SSD_HEADS, 1), whole),
                   pl.BlockSpec((1, SSD_D_INNER), whole),
                   pl.BlockSpec((1, SSD_D_INNER), whole)] + [any_spec] * nh,
        out_shape=[jax.ShapeDtypeStruct((length, CONV_DIM), BF),
                   jax.ShapeDtypeStruct((length, NP), BF),
                   jax.ShapeDtypeStruct((SSD_HEADS, length), F32),
                   jax.ShapeDtypeStruct((SSD_HEADS, 1), F32),
                   jax.ShapeDtypeStruct((SSD_HEADS, 1), F32),
                   jax.ShapeDtypeStruct((1, SSD_D_INNER), F32),
                   jax.ShapeDtypeStruct((1, SSD_D_INNER), F32)]
        + [jax.ShapeDtypeStruct(((4,) + s.shape) if h_bcast else s.shape, s.dtype) for s in h_arrays],
        scratch_shapes=[pltpu.VMEM((SSD_GROUPS, SSD_STATE, 512), F32)]
        + ([pltpu.SemaphoreType.DMA((4 * nh,)), pltpu.SemaphoreType.DMA((4 * nh,))] if nh else []),
        compiler_params=_params(1),
    )(pre, pre, pre, proj, dt_t, dtb, alog, dsk, nw, s_in, dy, dproj, *h_arrays)
    return tuple(res[:7]) + ((list(res[7:]),) if nh else ())


assert _OFF["gates"] == 0 and _OFF["z"] % SSD_D_INNER == 0 and _OFF["q"] % D_MODEL == 0 and _OFF["k"] + 512 == NP


def _attn_in_specs():
    prev = lambda n: (0, jnp.maximum(n - 1, 0), 0)
    cur = lambda n: (0, n, 0)
    return [pl.BlockSpec((BLK, D_MODEL), lambda n: (n, _OFF["q"] // D_MODEL)),
            pl.BlockSpec((2, BLK, BLK), prev),
            pl.BlockSpec((2, BLK, BLK), cur),
            pl.BlockSpec((2, BLK, BLK), prev),
            pl.BlockSpec((2, BLK, BLK), cur),
            pl.BlockSpec((1, ATT_HEADS), lambda n: (0, 0))]


def _attn_fwd(proj, k2, v2, sinks, name, hosted=None):
    length = proj.shape[0]
    nb = length // BLK
    h_arrays, h_bcast = hosted if hosted else ((), False)
    nh = len(h_arrays)

    def body(*refs):
        q, kp, kc, vp, vc, sk = refs[:6]
        o_ref = refs[6 + nh]
        n = pl.program_id(0)
        if nh:
            start, wait = _chip_exchange(refs[6:6 + nh], refs[7 + nh:7 + 2 * nh], refs[-2], refs[-1], h_bcast)

            @pl.when(n == 0)
            def _():
                start()

        for kv in range(2):
            cols = slice(512 * kv, 512 * (kv + 1))
            o = _attn_block(q[:, cols], kp[kv], kc[kv], vp[kv], vc[kv], sk[...], n, kv)
            o_ref[:, cols] = o.astype(o_ref.dtype)

        if nh:
            @pl.when(n == nb - 1)
            def _():
                wait()

    any_spec = pl.BlockSpec(memory_space=pl.ANY)
    res = pl.pallas_call(
        body, name=name, grid=(nb,), in_specs=_attn_in_specs() + [any_spec] * nh,
        out_specs=[pl.BlockSpec((BLK, D_MODEL), lambda n: (n, 0))] + [any_spec] * nh,
        out_shape=[jax.ShapeDtypeStruct((length, D_MODEL), BF)]
        + [jax.ShapeDtypeStruct(((4,) + s.shape) if h_bcast else s.shape, s.dtype) for s in h_arrays],
        scratch_shapes=[pltpu.SemaphoreType.DMA((4 * nh,)), pltpu.SemaphoreType.DMA((4 * nh,))] if nh else [],
        compiler_params=_params(1),
    )(proj, k2, k2, v2, v2, sinks, *h_arrays)
    return (res[0], list(res[1:])) if nh else res[0]


def _attn_bwd(proj, k2, v2, sinks, datt, dproj, name):
    length = proj.shape[0]

    def body(q, kp, kc, vp, vc, sk, do, _, dq, dkp, dkc, dvp, dvc, dsk):
        n = pl.program_id(0)
        dsinks = None
        for kv in range(2):
            cols = slice(512 * kv, 512 * (kv + 1))
            _, vjp = jax.vjp(lambda *a: _attn_block(*a, n, kv), q[:, cols], kp[kv], kc[kv], vp[kv], vc[kv], sk[...])
            g = vjp(do[:, cols].astype(F32))
            dq[:, cols] = g[0].astype(dq.dtype)
            dkp[kv] = g[1]
            dkc[kv] = g[2]
            dvp[kv] = g[3]
            dvc[kv] = g[4]
            dsinks = g[5] if dsinks is None else dsinks + g[5]

        @pl.when(n == 0)
        def _():
            dsk[...] = dsinks

        @pl.when(n > 0)
        def _():
            dsk[...] += dsinks

    blk3 = pl.BlockSpec((2, BLK, BLK), lambda n: (0, n, 0))
    kv_shape = jax.ShapeDtypeStruct((2, length, BLK), F32)
    return pl.pallas_call(
        body, name=name, grid=(length // BLK,),
        in_specs=_attn_in_specs() + [pl.BlockSpec((BLK, D_MODEL), lambda n: (n, 0)), pl.BlockSpec(memory_space=pl.ANY)],
        out_specs=[pl.BlockSpec((BLK, D_MODEL), lambda n: (n, _OFF["q"] // D_MODEL)), blk3, blk3, blk3, blk3,
                   pl.BlockSpec((1, ATT_HEADS), lambda n: (0, 0))],
        out_shape=[jax.ShapeDtypeStruct((length, NP), BF), kv_shape, kv_shape, kv_shape, kv_shape,
                   jax.ShapeDtypeStruct((1, ATT_HEADS), F32)],
        input_output_aliases={7: 0},
        compiler_params=_params(1),
    )(proj, k2, k2, v2, v2, sinks, datt, dproj)


DMA_CHUNK_BYTES = 1 << 20
N_STAGE = 12
LOOKAHEAD = 5


def _piece_chunks(shape, itemsize):
    if len(shape) < 2 or shape[-2] % 16 != 0:
        return [()]
    rows, cols = shape[-2:]
    step = min(rows, max(16, DMA_CHUNK_BYTES // (cols * itemsize) // 16 * 16))
    out = []
    for lead in itertools.product(*[range(d) for d in shape[:-2]]):
        for r0 in range(0, rows, step):
            out.append(lead + (pl.ds(r0, min(step, rows - r0)),))
    return out


def _exchange(srcs, group, bcast, name):
    size = {"c": 2, "xy": 4, "all": 8}[group]
    npeer = size - 1
    n = len(srcs)
    for s in srcs:
        assert bcast or s.shape[0] == size
    pieces = [s.shape if bcast else s.shape[1:] for s in srcs]
    chunks = [_piece_chunks(p, s.dtype.itemsize) for p, s in zip(pieces, srcs)]

    def body(*refs):
        src, out = refs[:n], refs[n:2 * n]
        load_sems, send_sems, recv_sems = refs[2 * n:]
        x, y, c = lax.axis_index("x"), lax.axis_index("y"), lax.axis_index("c")
        if group == "c":
            me = c
        elif group == "xy":
            me = 2 * x + y
        else:
            me = 4 * x + 2 * y + c

        def device(j):
            if group == "c":
                return (x, y, j)
            if group == "xy":
                return (j // 2, j % 2, c)
            return (j // 4, (j // 2) % 2, j % 2)

        def part(ref, idx):
            return ref.at[idx] if idx else ref

        def piece(a, j):
            return src[a] if bcast else src[a].at[j]

        for a in range(n):
            jobs = [(None, idx) for idx in chunks[a]] if bcast else [(d, idx) for idx in chunks[a] for d in range(1, size)]
            full_rows = chunks[a][0][-1].size if chunks[a][0] else None
            slot_shape = (full_rows, pieces[a][-1]) if chunks[a][0] else pieces[a]

            def stream(stage, a=a, jobs=jobs, full_rows=full_rows):
                def slot(q):
                    idx = jobs[q][1]
                    view = stage.at[q % N_STAGE]
                    if idx and idx[-1].size != full_rows:
                        view = view.at[pl.ds(0, idx[-1].size)]
                    return view

                def load(q):
                    d, idx = jobs[q]
                    j = me if d is None else jnp.bitwise_xor(me, d)
                    return pltpu.make_async_copy(part(piece(a, j), idx), slot(q), load_sems.at[q % N_STAGE])

                def sends(q):
                    d, idx = jobs[q]
                    return [pltpu.make_async_remote_copy(
                        src_ref=slot(q), dst_ref=part(out[a].at[me], idx),
                        send_sem=send_sems.at[(q % N_STAGE) * npeer + dd - 1], recv_sem=recv_sems.at[a * size + dd],
                        device_id=device(jnp.bitwise_xor(me, dd)), device_id_type=pl.DeviceIdType.MESH)
                        for dd in (range(1, size) if d is None else (d,))]

                nq = len(jobs)
                for q in range(nq + LOOKAHEAD):
                    if q < nq:
                        if q >= N_STAGE:
                            for cp in sends(q - N_STAGE):
                                cp.wait_send()
                        load(q).start()
                    if q >= LOOKAHEAD:
                        load(q - LOOKAHEAD).wait()
                        for cp in sends(q - LOOKAHEAD):
                            cp.start()
                for q in range(max(0, nq - N_STAGE), nq):
                    for cp in sends(q):
                        cp.wait_send()

            pl.run_scoped(stream, pltpu.VMEM((N_STAGE,) + tuple(slot_shape), srcs[a].dtype))

        for a in range(n):
            for d in range(1, size):
                j = jnp.bitwise_xor(me, d)
                pltpu.make_async_remote_copy(
                    src_ref=piece(a, j), dst_ref=out[a].at[j], send_sem=send_sems.at[0], recv_sem=recv_sems.at[a * size + d],
                    device_id=device(j), device_id_type=pl.DeviceIdType.MESH).wait_recv()

    any_spec = pl.BlockSpec(memory_space=pl.ANY)
    return pl.pallas_call(
        body, name=name, in_specs=[any_spec] * n, out_specs=[any_spec] * n,
        out_shape=[jax.ShapeDtypeStruct((size,) + tuple(p), s.dtype) for p, s in zip(pieces, srcs)],
        scratch_shapes=[pltpu.SemaphoreType.DMA((N_STAGE,)), pltpu.SemaphoreType.DMA((N_STAGE * npeer,)),
                        pltpu.SemaphoreType.DMA((n * size,))],
        compiler_params=pltpu.CompilerParams(vmem_limit_bytes=VMEM_LIMIT_BYTES),
    )(*srcs)


def _with_own(outs, owns, me):
    return [lax.dynamic_update_index_in_dim(o, w, me, 0) for o, w in zip(outs, owns)]


def _sum_slots(arr, name):
    k = arr.shape[0]
    rest = arr.shape[1:]
    width = rest[-1]
    rows_per = math.prod(rest[:-1])
    a2 = arr.reshape(k * rows_per, width)
    tile = rows_per
    for cand in (256, 128, 64, 32, 16, 8):
        if rows_per % cand == 0:
            tile = cand
            break
    nt = rows_per // tile

    def body(*refs):
        acc = refs[0][...]
        for r in refs[1:k]:
            acc = acc + r[...]
        refs[k][...] = acc

    in_specs = [pl.BlockSpec((tile, width), functools.partial(lambda i, s: (s * nt + i, 0), s=s)) for s in range(k)]
    out = pl.pallas_call(
        body, name=name, grid=(nt,), in_specs=in_specs, out_specs=pl.BlockSpec((tile, width), lambda i: (i, 0)),
        out_shape=jax.ShapeDtypeStruct((rows_per, width), arr.dtype), compiler_params=_params(1),
    )(*([a2] * k))
    return out.reshape(rest)


def _sum_pieces(recv, own, own_slotted, me, name):
    k = recv.shape[0]
    rest = recv.shape[1:]
    width = rest[-1]
    rows_per = math.prod(rest[:-1])
    r2 = recv.reshape(k * rows_per, width)
    o2 = own.reshape(-1, width)
    tile = rows_per
    for cand in (256, 128, 64, 32, 16, 8):
        if rows_per % cand == 0:
            tile = cand
            break
    nt = rows_per // tile

    def body(me_ref, *refs):
        mine = refs[k][...].astype(F32)
        acc = None
        for j in range(k):
            term = jnp.where(me_ref[0] == j, mine, refs[j][...].astype(F32))
            acc = term if acc is None else acc + term
        refs[k + 1][...] = acc
        refs[k + 2][...] = acc.astype(BF)

    in_specs = [pl.BlockSpec((tile, width), functools.partial(lambda i, m, s: (s * nt + i, 0), s=s)) for s in range(k)]
    in_specs.append(pl.BlockSpec((tile, width), (lambda i, m: (m[0] * nt + i, 0)) if own_slotted else (lambda i, m: (i, 0))))
    out_spec = pl.BlockSpec((tile, width), lambda i, m: (i, 0))
    out, out_bf = pl.pallas_call(
        body, name=name,
        grid_spec=pltpu.PrefetchScalarGridSpec(num_scalar_prefetch=1, grid=(nt,), in_specs=in_specs, out_specs=[out_spec, out_spec]),
        out_shape=[jax.ShapeDtypeStruct((rows_per, width), F32), jax.ShapeDtypeStruct((rows_per, width), BF)],
        compiler_params=_params(1),
    )(jnp.reshape(me, (1,)).astype(jnp.int32), *([r2] * k), o2)
    return out.reshape(rest), out_bf.reshape(rest)


def _adamw(w, g, m, v, name):
    def fn(w_, g_, m_, v_):
        m1 = ADAM_B1 * m_ + (1.0 - ADAM_B1) * g_
        v1 = ADAM_B2 * v_ + (1.0 - ADAM_B2) * (g_ * g_)
        m_hat = m1 / (1.0 - ADAM_B1 ** ADAM_STEP)
        v_hat = v1 / (1.0 - ADAM_B2 ** ADAM_STEP)
        delta = -ADAM_LR * (m_hat / (jnp.sqrt(v_hat) + ADAM_EPS) + ADAM_WD * w_)
        return (delta, m1, v1), ()

    rows, width = w.shape
    tile = rows
    for cand in (256, 128, 64, 32, 16, 8):
        if rows % cand == 0:
            tile = cand
            break
    o, _ = _rows(fn, [(w, 0, width), (g, 0, width), (m, 0, width), (v, 0, width)], [],
                 [(width, F32)] * 3, [], tile, name)
    return o


def _col_pieces(start, width, shard):
    out = []
    while width:
        chip, off = divmod(start, shard)
        take = min(width, shard - off)
        out.append((chip, off, take))
        start, width = start + take, width - take
    return out


def _gather_cols(parts, ranges, pad=0):
    shard = parts.shape[3]
    halves = []
    for h in range(2):
        cols = [parts[h, chip, :, off:off + w] for s, wd in ranges for chip, off, w in _col_pieces(s, wd, shard)]
        if pad:
            cols.append(jnp.zeros((parts.shape[2], pad), parts.dtype))
        halves.append(jnp.concatenate(cols, axis=1))
    return jnp.concatenate(halves, axis=0)


def _scatter_cols(g, ranges, shard):
    starts, pos = {}, 0
    for s, wd in ranges:
        starts[s] = (pos, wd)
        pos += wd
    order = sorted(starts)
    hr = g.shape[0] // 2
    out = []
    for h in range(2):
        per_chip = []
        for chip in range(4):
            cols = []
            for s in order:
                p0, wd = starts[s]
                lo, hi = max(s, chip * shard), min(s + wd, (chip + 1) * shard)
                if lo < hi:
                    cols.append(g[h * hr:(h + 1) * hr, p0 + lo - s:p0 + hi - s])
            per_chip.append(jnp.concatenate(cols, axis=1))
        out.append(jnp.stack(per_chip))
    return jnp.stack(out)


_IN_RANGES = tuple((o, wd) for _, o, wd in _PACK)


def _double_heads(t):
    length = t.shape[0]
    h = jnp.transpose(t.reshape(length, 2, 64), (1, 0, 2))
    return jnp.concatenate([h, h], axis=-1)


def _fold_heads(d_cur, d_prev):
    length = d_cur.shape[1]
    d = d_cur + jnp.concatenate([d_prev[:, BLK:], jnp.zeros((2, BLK, BLK), F32)], axis=1)
    d = d[..., :64] + d[..., 64:]
    return jnp.transpose(d, (1, 0, 2)).reshape(length, 128)


def _ln_fwd(a, r, g, b, name):
    def fn(a_, r_, g_, b_):
        y = _layer_norm(ALPHA * a_ + r_, g_, b_)
        return (y, y), ()
    o, _ = _rows(fn, [(a, 0, D_MODEL), (r, 0, D_MODEL)], [g, b], [(D_MODEL, F32), (D_MODEL, BF)], [], ROW_TILE, name)
    return o


def _ln_bwd(a, r, d1, d2, g, b, name):
    def fn(a_, r_, d1_, d2_, g_, b_):
        _, vjp = jax.vjp(_layer_norm, ALPHA * a_ + r_, g_, b_)
        ds, dg, db = vjp(ALPHA * d1_ + d2_)
        return (ds, ds), (dg, db)
    o, acc = _rows(fn, [(a, 0, D_MODEL), (r, 0, D_MODEL), (d1, 0, D_MODEL), (d2, 0, D_MODEL)], [g, b],
                   [(D_MODEL, F32), (D_MODEL, BF)], [(1, D_MODEL), (1, D_MODEL)], ROW_TILE, name)
    return o[0], o[1], acc[0], acc[1]


def _layer_fwd(l, h, hb, w, p, hosted=None, hosted_attn=None, hosted_conv=None, late_weights=None):
    sv = {"h": h, "hb": hb}
    proj = _mm(hb, w["win"], "nn", F32, 4096, 512, 1024, f"proj{l}")
    sv["proj"] = proj
    pre = _conv_fwd(proj, p["conv_w"], p["conv_b"], f"conv{l}", hosted_conv)
    if hosted_conv:
        pre, got = pre
        w.update(late_weights(got))
    sv["pre"] = pre
    dt_t = jnp.transpose(proj[:, _OFF["dt"]:_OFF["dt"] + SSD_HEADS])
    sv["dt_t"] = dt_t
    res = _ssd_fwd(pre, proj, dt_t, p["dt_bias"], p["a_log"], p["d_skip_c"], p["ssd_norm_w"], f"ssd{l}", hosted)
    yn, s_in = res[:2]
    if hosted:
        sv["hosted"] = res[2]
    sv["yn"], sv["s_in"] = yn, s_in
    ya = _mm(yn, w["wso"], "nn", BF, 1024, 1024, 2048, f"ssdout{l}")
    k2 = _double_heads(proj[:, _OFF["k"]:_OFF["k"] + 128])
    v2 = _double_heads(proj[:, _OFF["v"]:_OFF["v"] + 128])
    sv["k2"], sv["v2"] = k2, v2
    att = _attn_fwd(proj, k2, v2, p["att_sinks"], f"attn{l}", hosted_attn)
    if hosted_attn:
        att, sv["hosted_attn"] = att
    sv["att"] = att
    yb = _mm(att, w["wao"], "nn", BF, 1024, 1024, 1024, f"attout{l}")
    sv["ya"], sv["yb"] = ya, yb
    gcb = _OFF["gates"] // 1024

    def gate_fn(ga, gb, ya_, yb_):
        return (_sigmoid(ga) * ya_.astype(F32) + _sigmoid(gb) * yb_.astype(F32),), ()
    (ub,), _ = _rows(gate_fn, [(proj, gcb, 1024), (proj, gcb + 1, 1024), (ya, 0, 1024), (yb, 0, 1024)], [],
                     [(D_MODEL, BF)], [], ROW_TILE, f"gate{l}")
    sv["ub"] = ub
    mix = _mm(ub, w["wmo"], "nn", F32, 1024, 1024, 1024, f"mixout{l}")
    sv["mix"] = mix
    h1, h1b = _ln_fwd(h, mix, p["ln_mix_g"], p["ln_mix_b"], f"lnmix{l}")
    sv["h1"], sv["h1b"] = h1, h1b
    gu = _mm(h1b, w["wgu"], "nn", BF, 4096, 512, 1024, f"ffnin{l}")
    sv["gu"] = gu

    def act_fn(g_, u_):
        return (_silu(g_.astype(F32)) * u_.astype(F32),), ()
    (act,), _ = _rows(act_fn, [(gu, 0, FFN_HIDDEN), (gu, 1, FFN_HIDDEN)], [], [(FFN_HIDDEN, BF)], [], ROW_TILE, f"swiglu{l}")
    sv["act"] = act
    ffn = _mm(act, w["wd"], "nn", F32, 1024, 1024, FFN_HIDDEN, f"ffnout{l}")
    sv["ffn"] = ffn
    return sv


def _layer_bwd(l, sv, w, p, ds_f, ds_b, hosted=(), early_hook=None, late_hook=None):
    g = {}
    g["w_ffn_down"] = _mm(sv["act"], ds_b, "tn", F32, 1408, 1024, 2048, f"dwd{l}")
    dact = _mm(ds_b, w["wd"], "nt", BF, 1024, FFN_HIDDEN, 1024, f"dact{l}")

    def act_bwd(g_, u_, d_):
        _, vjp = jax.vjp(lambda a, b: _silu(a) * b, g_.astype(F32), u_.astype(F32))
        dg_, du_ = vjp(d_.astype(F32))
        return (jnp.concatenate([dg_, du_], axis=1),), ()
    (dgu,), _ = _rows(act_bwd, [(sv["gu"], 0, FFN_HIDDEN), (sv["gu"], 1, FFN_HIDDEN), (dact, 0, FFN_HIDDEN)], [],
                      [(2 * FFN_HIDDEN, BF)], [], ROW_TILE // 2, f"dswiglu{l}")
    dwgu = _mm(sv["h1b"], dgu, "tn", F32, 1024, 1408, 2048, f"dwgu{l}")
    g["w_ffn_gate"], g["w_ffn_up"] = dwgu[:, :FFN_HIDDEN], dwgu[:, FFN_HIDDEN:]
    dh1 = _mm(dgu, w["wgu"], "nt", F32, 1024, 1024, 2816, f"dh1{l}")
    ds2_f, ds2_b, g["ln_mix_g"], g["ln_mix_b"] = _ln_bwd(sv["h"], sv["mix"], ds_f, dh1, p["ln_mix_g"], p["ln_mix_b"], f"dlnmix{l}")
    g["w_mix_out"] = _mm(sv["ub"], ds2_b, "tn", F32, 1024, 1024, 2048, f"dwmo{l}")
    du = _mm(ds2_b, w["wmo"], "nt", BF, 1024, 1024, 1024, f"du{l}")
    proj = sv["proj"]
    gcb = _OFF["gates"] // 1024

    def gate_bwd(ga, gb, ya_, yb_, du_):
        _, vjp = jax.vjp(lambda a, b, c, d: _sigmoid(a) * c + _sigmoid(b) * d, ga, gb, ya_.astype(F32), yb_.astype(F32))
        dga, dgb, dya, dyb = vjp(du_.astype(F32))
        return (jnp.concatenate([dga, dgb], axis=1), dya, dyb), ()
    (dproj, dya, dyb), _ = _rows(
        gate_bwd, [(proj, gcb, 1024), (proj, gcb + 1, 1024), (sv["ya"], 0, 1024), (sv["yb"], 0, 1024), (du, 0, 1024)], [],
        [(2048, BF, NP), (D_MODEL, BF), (D_MODEL, BF)], [], ROW_TILE, f"dgate{l}")
    g["w_att_out"] = _mm(sv["att"], dyb, "tn", F32, 1024, 1024, 2048, f"dwao{l}")
    datt = _mm(dyb, w["wao"], "nt", BF, 1024, 1024, 1024, f"datt{l}")
    g["w_ssd_out"] = _mm(sv["yn"], dya, "tn", F32, 1024, 1024, 2048, f"dwso{l}")
    dyn = _mm(dya, w["wso"], "nt", F32, 1024, 1024, 1024, f"dyn{l}")
    hosted = list(hosted) + (list(early_hook(g)) if early_hook else [])
    dproj, dkp, dkc, dvp, dvc, g["att_sinks"] = _attn_bwd(proj, sv["k2"], sv["v2"], p["att_sinks"], datt, dproj, f"dattn{l}")
    res = _ssd_bwd(sv["pre"], proj, sv["dt_t"], p["dt_bias"], p["a_log"], p["d_skip_c"], p["ssd_norm_w"], sv["s_in"], dyn,
                   dproj, f"dssd{l}", (hosted, False) if hosted else None)
    (dpre, dproj, ddt_t, g["dt_bias"], g["a_log"], ddsk, g["ssd_norm_w"]) = res[:7]
    if hosted:
        g["hosted"] = res[7]
    g["d_skip"] = jnp.sum(ddsk.reshape(SSD_HEADS, 64), axis=1)
    dproj, g["conv_w"], g["conv_b"] = _conv_bwd(dpre, proj, p["conv_w"], dproj, 0, f"dconv{l}")
    length = proj.shape[0]
    tail = jnp.concatenate([_fold_heads(dkc, dkp).astype(BF), _fold_heads(dvc, dvp).astype(BF),
                            jnp.transpose(ddt_t).astype(BF), jnp.zeros((length, _PAD), BF)], axis=1)
    dproj = lax.dynamic_update_slice(dproj, tail, (0, _OFF["k"]))
    g["w_in"] = _mm(sv["hb"], dproj, "tn", F32, 1024, 2176, 1024, f"dwin{l}")
    late = list(late_hook(g)) if late_hook else []
    dh = _mm(dproj, w["win"], "nt", F32, 1024, 1024, 2176, f"dh{l}", (late, False) if late else None)
    if late:
        dh, g["late_hosted"] = dh
    return g, ds2_f, dh


def kernel(x, ln_in_g, ln_in_b, w_in, conv_w, conv_b, dt_bias, a_log, d_skip, ssd_norm_w, att_sinks, w_ssd_out, w_att_out, w_mix_out, ln_mix_g, ln_mix_b, w_ffn_gate, w_ffn_up, w_ffn_down, ln_ffn_g, ln_ffn_b, loss_target, m_ln_in_g, m_ln_in_b, m_w_in, m_conv_w, m_conv_b, m_dt_bias, m_a_log, m_d_skip, m_ssd_norm_w, m_att_sinks, m_w_ssd_out, m_w_att_out, m_w_mix_out, m_ln_mix_g, m_ln_mix_b, m_w_ffn_gate, m_w_ffn_up, m_w_ffn_down, m_ln_ffn_g, m_ln_ffn_b, v_ln_in_g, v_ln_in_b, v_w_in, v_conv_w, v_conv_b, v_dt_bias, v_a_log, v_d_skip, v_ssd_norm_w, v_att_sinks, v_w_ssd_out, v_w_att_out, v_w_mix_out, v_ln_mix_g, v_ln_mix_b, v_w_ffn_gate, v_w_ffn_up, v_w_ffn_down, v_ln_ffn_g, v_ln_ffn_b):
    env = dict(locals())
    wts = {n: env[n] for n in WEIGHTS}
    mom1 = {n: env["m_" + n] for n in WEIGHTS}
    mom2 = {n: env["v_" + n] for n in WEIGHTS}
    ci = lax.axis_index("c")
    chip = 2 * lax.axis_index("x") + lax.axis_index("y")
    xs_ = x[0]
    tgt = loss_target[0]

    COLS = ("w_in", "w_ffn_gate", "w_ffn_up")

    def half(t):
        return lax.dynamic_slice_in_dim(t, ci * (t.shape[0] // 2), t.shape[0] // 2, axis=0)

    def assemble(name, parts):
        if name == "w_in":
            return _gather_cols(parts, _IN_RANGES, _PAD)
        if name in COLS:
            return _gather_cols(parts, ((0, 4 * parts.shape[3]),))
        return jnp.concatenate([parts[h, s] for s in range(4) for h in range(2)], axis=0)

    def split(name, gfull):
        if name == "w_in":
            return _scatter_cols(gfull, _IN_RANGES, IN_DIM // 4)
        if name in COLS:
            return _scatter_cols(gfull, ((0, gfull.shape[1]),), gfull.shape[1] // 4)
        hr = gfull.shape[0] // 8
        return jnp.stack([jnp.stack([gfull[(2 * s + h) * hr:(2 * s + h + 1) * hr] for s in range(4)]) for h in range(2)])

    mine = [[half(wts[n][l].astype(BF)) for n in BIG] for l in range(DEPTH)]
    conv_all, = _with_own(_exchange([conv_w], "xy", True, "gather_conv"), [conv_w], chip)
    conv_full = jnp.transpose(conv_all, (1, 2, 0, 3)).reshape(DEPTH, SSD_CONV, CONV_DIM)

    KEYS = {"w_in": "win", "w_ssd_out": "wso", "w_att_out": "wao", "w_mix_out": "wmo", "w_ffn_down": "wd"}

    def layer_weights(l, names, by_chip):
        own = [mine[l][BIG.index(n)] for n in names]
        by_chip = _with_own(by_chip, own, chip)
        parts = _with_own(_exchange(by_chip, "c", True, f"gather_cores{l}_{names[0]}"), by_chip, ci)
        fw = {n: assemble(n, part) for n, part in zip(names, parts)}
        w = {KEYS[n]: fw[n] for n in names if n in KEYS}
        if "w_ffn_gate" in fw:
            w["wgu"] = jnp.concatenate([fw["w_ffn_gate"], fw["w_ffn_up"]], axis=1)
        return w

    layer_p = []
    for l in range(DEPTH):
        layer_p.append({
            "conv_w": conv_full[l], "conv_b": conv_b[l][None],
            "dt_bias": dt_bias[l][:, None], "a_log": a_log[l][:, None],
            "d_skip_c": jnp.repeat(d_skip[l], 64)[None], "ssd_norm_w": ssd_norm_w[l][None],
            "att_sinks": att_sinks[l][None], "ln_mix_g": ln_mix_g[l][None], "ln_mix_b": ln_mix_b[l][None],
            "ln_ffn_g": ln_ffn_g[l][None], "ln_ffn_b": ln_ffn_b[l][None],
        })

    def ln_in_fn(x_, g_, b_):
        y = _layer_norm(x_, g_, b_)
        return (y, y), ()
    (h, hb), _ = _rows(ln_in_fn, [(xs_, 0, D_MODEL)], [ln_in_g[None], ln_in_b[None]], [(D_MODEL, F32), (D_MODEL, BF)], [], ROW_TILE, "ln_in")
    rest = tuple(n for n in BIG if n != "w_in")
    layer_w = [layer_weights(0, ("w_in",), _exchange([mine[0][BIG.index("w_in")]], "xy", True, "gather_chips0"))]
    behind_ssd = [i for i, n in enumerate(BIG) if n in COLS]
    behind_attn = [i for i, n in enumerate(BIG) if n not in COLS]
    saved = []
    for l in range(DEPTH):
        nxt = l + 1 < DEPTH
        sv = _layer_fwd(l, h, hb, layer_w[l], layer_p[l], ([mine[l + 1][i] for i in behind_ssd], True) if nxt else None,
                        ([mine[l + 1][i] for i in behind_attn], True) if nxt else None,
                        ([mine[l][BIG.index(n)] for n in rest], True) if l == 0 else None,
                        functools.partial(layer_weights, l, rest))
        saved.append(sv)
        if nxt:
            got = dict(zip(behind_ssd + behind_attn, sv["hosted"] + sv["hosted_attn"]))
            layer_w.append(layer_weights(l + 1, BIG, [got[i] for i in range(len(BIG))]))
            h, hb = _ln_fwd(sv["h1"], sv["ffn"], layer_p[l]["ln_ffn_g"], layer_p[l]["ln_ffn_b"], f"lnffn{l}")

    def loss_fn(a_, r_, t_, g_, b_):
        y, vjp = jax.vjp(_layer_norm, ALPHA * a_ + r_, g_, b_)
        err = y - t_
        ds, dg, db = vjp(err * (1.0 / D_MODEL))
        part = 0.5 * jnp.sum(jnp.mean(err * err, axis=-1, keepdims=True), axis=0, keepdims=True)
        return (ds, ds), (dg, db, jnp.broadcast_to(part, (1, BLK)))
    sv = saved[-1]
    (ds_f, ds_b), (dg_last, db_last, loss_part) = _rows(
        loss_fn, [(sv["h1"], 0, D_MODEL), (sv["ffn"], 0, D_MODEL), (tgt, 0, D_MODEL)],
        [layer_p[-1]["ln_ffn_g"], layer_p[-1]["ln_ffn_b"]], [(D_MODEL, F32), (D_MODEL, BF)],
        [(1, D_MODEL), (1, D_MODEL), (1, BLK)], ROW_TILE, "loss")
    loss = lax.psum(loss_part[0, 0], ("x", "y", "c"))

    def reduce_cores(l, g, names):
        src = [split(n, g[n]) for n in names]
        recv = _exchange([s.astype(BF) for s in src], "c", False, f"reduce_cores{l}_{names[0]}")
        return {n: _sum_pieces(r, o, True, ci, f"sum_cores{l}_{n}") for n, r, o in zip(names, recv, src)}

    def bf16_of(sums, names):
        return [sums[n][1] for n in names]

    def add_chips(l, sums, from_chips):
        return [_sum_pieces(from_chips[n], sums[n][0], True, chip, f"sum_chips{l}_{n}")[0] for n in BIG]

    EARLY = tuple(n for n in BIG if n != "w_in")
    grads = [None] * DEPTH
    chip_sums = [None] * DEPTH
    for l in reversed(range(DEPTH)):
        sv = saved[l]
        if l == DEPTH - 1:
            lg, lb = dg_last, db_last
        else:
            ds_f, ds_b, lg, lb = _ln_bwd(sv["h1"], sv["ffn"], d1, d2, layer_p[l]["ln_ffn_g"], layer_p[l]["ln_ffn_b"], f"dlnffn{l}")
        sums = {}

        def early_hook(g, l=l, sums=sums):
            sums.update(reduce_cores(l, g, EARLY))
            return bf16_of(sums, EARLY)

        def late_hook(g, l=l, sums=sums):
            sums.update(reduce_cores(l, g, ("w_in",)))
            return bf16_of(sums, ("w_in",))
        if l == 0:
            g, d1, d2 = _layer_bwd(l, sv, layer_w[l], layer_p[l], ds_f, ds_b, bf16_of(chip_sums[l + 1], BIG), early_hook, late_hook)
        else:
            g, d1, d2 = _layer_bwd(l, sv, layer_w[l], layer_p[l], ds_f, ds_b)
            sums.update(reduce_cores(l, g, BIG))
        g["ln_ffn_g"], g["ln_ffn_b"] = lg, lb
        grads[l] = g
        chip_sums[l] = sums

    def ln_in_bwd(x_, d1_, d2_, g_, b_):
        _, vjp = jax.vjp(_layer_norm, x_, g_, b_)
        dx, dg, db = vjp(ALPHA * d1_ + d2_)
        return (dx,), (dg, db)
    (grad_x,), (g_ln_in_g, g_ln_in_b) = _rows(
        ln_in_bwd, [(xs_, 0, D_MODEL), (d1, 0, D_MODEL), (d2, 0, D_MODEL)], [ln_in_g[None], ln_in_b[None]],
        [(D_MODEL, F32)], [(1, D_MODEL), (1, D_MODEL)], ROW_TILE, "dln_in")

    carried = grads[0]["hosted"]
    from_chips = dict(zip(EARLY, carried[len(BIG):]))
    from_chips["w_in"], = grads[0]["late_hosted"]
    reduced = add_chips(0, chip_sums[0], from_chips) + add_chips(1, chip_sums[1], dict(zip(BIG, carried[:len(BIG)])))
    halves = _with_own(_exchange(reduced, "c", True, "swap_halves"), reduced, ci)
    shards = [t.reshape(2 * t.shape[1], t.shape[2]) for t in halves]
    big_grad = {n: jnp.stack([shards[l * len(BIG) + i] for l in range(DEPTH)]).reshape(wts[n].shape) for i, n in enumerate(BIG)}

    small_g = {"ln_in_g": g_ln_in_g[0], "ln_in_b": g_ln_in_b[0]}
    for n in SMALL[2:]:
        small_g[n] = jnp.stack([grads[l][n].reshape(wts[n].shape[1:] if n != "conv_w" else (SSD_CONV, CONV_DIM)) for l in range(DEPTH)])
    flat = jnp.concatenate([small_g[n].reshape(-1) for n in SMALL])
    n_small = flat.shape[0]
    width = -(-n_small // 1024) * 1024
    flat = jnp.pad(flat, (0, width - n_small)).reshape(8, width // 8)
    gathered, = _with_own(_exchange([flat], "all", True, "gather_small"), [flat], 2 * chip + ci)
    total = _sum_slots(gathered, "sum_small").reshape(-1)
    small_grad, off = {}, 0
    for n in SMALL:
        shp = small_g[n].shape
        cnt = math.prod(shp)
        small_grad[n] = total[off:off + cnt].reshape(shp)
        off += cnt
    small_grad["conv_w"] = lax.dynamic_slice_in_dim(small_grad["conv_w"], chip * (CONV_DIM // 4), CONV_DIM // 4, axis=2)

    out_g, out_d, out_m, out_v = {}, {}, {}, {}
    for n in BIG:
        shp = wts[n].shape
        two_d = lambda t: t.reshape(shp[0] * shp[1], shp[2])
        d_, m_, v_ = _adamw(two_d(wts[n]), two_d(big_grad[n]), two_d(mom1[n]), two_d(mom2[n]), f"adamw_{n}")
        out_g[n], out_d[n], out_m[n], out_v[n] = big_grad[n], d_.reshape(shp), m_.reshape(shp), v_.reshape(shp)

    def flat_small(d):
        f = jnp.concatenate([d[n].reshape(-1) for n in SMALL])
        return jnp.pad(f, (0, swidth - f.shape[0])).reshape(8, swidth // 8)
    n_sw = sum(math.prod(wts[n].shape) for n in SMALL)
    swidth = -(-n_sw // 1024) * 1024
    d_, m_, v_ = _adamw(flat_small(wts), flat_small(small_grad), flat_small(mom1), flat_small(mom2), "adamw_small")
    off = 0
    for n in SMALL:
        shp = wts[n].shape
        cnt = math.prod(shp)
        out_g[n] = small_grad[n]
        out_d[n] = d_.reshape(-1)[off:off + cnt].reshape(shp)
        out_m[n] = m_.reshape(-1)[off:off + cnt].reshape(shp)
        out_v[n] = v_.reshape(-1)[off:off + cnt].reshape(shp)
        off += cnt

    return (loss, grad_x[None], *[out_g[n] for n in WEIGHTS], *[out_d[n] for n in WEIGHTS],
            *[out_m[n] for n in WEIGHTS], *[out_v[n] for n in WEIGHTS])
```

```python
import functools
import itertools
import math

import jax
import jax.numpy as jnp
from jax import lax
from jax.experimental import pallas as pl
from jax.experimental.pallas import tpu as pltpu

F32 = jnp.float32
BF = jnp.bfloat16

D_MODEL = 1024
DEPTH = 2
ATT_HEADS = 16
ATT_HEAD_DIM = 64
BLK = 128
SSD_D_INNER = 2048
SSD_HEADS = 32
SSD_GROUPS = 4
SSD_STATE = 128
SSD_CONV = 4
BC_DIM = 512
CONV_DIM = 3072
FFN_HIDDEN = 2816
IN_DIM = 8480
LN_EPS = 1e-5
RMS_EPS = 1e-5
ALPHA = (2 * DEPTH) ** 0.25
ADAM_LR = 0.001
ADAM_B1 = 0.9
ADAM_B2 = 0.999
ADAM_EPS = 1e-08
ADAM_WD = 0.01
ADAM_STEP = 10

_PACK = (("gates", 6432, 2048), ("z", 1280, 2048), ("q", 0, 1024), ("xs", 3328, 2048), ("B", 5376, 512),
         ("C", 5888, 512), ("k", 1024, 128), ("v", 1152, 128), ("dt", 6400, 32))
NP = 8704
_OFF = {}
_o = 0
for _n, _, _w in _PACK:
    _OFF[_n] = _o
    _o += _w
_PAD = NP - _o

VMEM_LIMIT_BYTES = 56 * 1024 * 1024
ROW_TILE = 512
BIG = ("w_in", "w_ssd_out", "w_att_out", "w_mix_out", "w_ffn_gate", "w_ffn_up", "w_ffn_down")
WEIGHTS = ("ln_in_g", "ln_in_b", "w_in", "conv_w", "conv_b", "dt_bias", "a_log", "d_skip", "ssd_norm_w", "att_sinks",
           "w_ssd_out", "w_att_out", "w_mix_out", "ln_mix_g", "ln_mix_b", "w_ffn_gate", "w_ffn_up", "w_ffn_down",
           "ln_ffn_g", "ln_ffn_b")
SMALL = tuple(n for n in WEIGHTS if n not in BIG)


def _params(n_grid):
    return pltpu.CompilerParams(dimension_semantics=("arbitrary",) * n_grid, vmem_limit_bytes=VMEM_LIMIT_BYTES)


def _dot(a, b, ca, cb):
    return lax.dot_general(a.astype(BF), b.astype(BF), (((ca,), (cb,)), ((), ())), preferred_element_type=F32)


@jax.custom_vjp
def _nn(a, b):
    return _dot(a, b, 1, 0)


def _nn_f(a, b):
    return _dot(a, b, 1, 0), (a.astype(BF), b.astype(BF))


def _nn_b(res, g):
    a, b = res
    return _dot(g, b, 1, 1), _dot(a, g, 0, 0)


_nn.defvjp(_nn_f, _nn_b)


@jax.custom_vjp
def _nt(a, b):
    return _dot(a, b, 1, 1)


def _nt_f(a, b):
    return _dot(a, b, 1, 1), (a.astype(BF), b.astype(BF))


def _nt_b(res, g):
    a, b = res
    return _dot(g, b, 1, 0), _dot(g, a, 0, 0)


_nt.defvjp(_nt_f, _nt_b)


@jax.custom_vjp
def _tn(a, b):
    return _dot(a, b, 0, 0)


def _tn_f(a, b):
    return _dot(a, b, 0, 0), (a.astype(BF), b.astype(BF))


def _tn_b(res, g):
    a, b = res
    return _dot(b, g, 1, 1), _dot(a, g, 1, 0)


_tn.defvjp(_tn_f, _tn_b)


def _sigmoid(x):
    return 0.5 * jnp.tanh(0.5 * x) + 0.5


def _silu(x):
    return x * _sigmoid(x)


def _layer_norm(s, g, b):
    mu = jnp.mean(s, axis=-1, keepdims=True)
    sc = s - mu
    var = jnp.mean(sc * sc, axis=-1, keepdims=True)
    return sc * lax.rsqrt(var + LN_EPS) * g + b


def _ssd_group(pre_x, pre_b, pre_c, z, dtr, dtb, alog, dsk, nw, state):
    t = pre_x.shape[0]
    xs, bm, cm = _silu(pre_x), _silu(pre_b), _silu(pre_c)
    dt = jax.nn.softplus(dtr + dtb)
    da = dt * (-jnp.exp(alog))
    row = lax.broadcasted_iota(jnp.int32, (t, t), 0)
    col = lax.broadcasted_iota(jnp.int32, (t, t), 1)
    upper = (row <= col).astype(F32)
    cum = jnp.dot(da, upper, precision=lax.Precision.HIGHEST, preferred_element_type=F32)
    tot = jnp.sum(da, axis=1, keepdims=True)
    cb = _nt(cm, bm)
    tril = row >= col
    first = lax.broadcasted_iota(jnp.int32, (t, BLK), 1) < 64
    first_row = lax.broadcasted_iota(jnp.int32, (1, BLK), 1) < 64

    def col_form(r):
        return jnp.broadcast_to(r, (t, t)).T

    ys, new_state = [], []
    for p in range(4):
        h0, h1 = 2 * p, 2 * p + 1
        sl = slice(BLK * p, BLK * (p + 1))
        x_pair = xs[:, sl] * jnp.where(first, col_form(dt[h0:h0 + 1]), col_form(dt[h1:h1 + 1]))
        y_pair = None
        cc = []
        for h, keep in ((h0, first), (h1, jnp.logical_not(first))):
            cr = jnp.broadcast_to(cum[h:h + 1], (t, t))
            cc.append(cr.T)
            decay = jnp.exp(jnp.where(tril, cc[-1] - cr, -1e30))
            y_h = _nn(decay * cb, jnp.where(keep, x_pair, 0.0))
            y_pair = y_h if y_pair is None else y_pair + y_h
        s_pair = state[:, sl]
        y_pair = y_pair + _nn(cm, s_pair) * jnp.where(first, jnp.exp(cc[0]), jnp.exp(cc[1]))
        to_end = jnp.where(first, jnp.exp(tot[h0:h0 + 1] - cc[0]), jnp.exp(tot[h1:h1 + 1] - cc[1]))
        chunk_decay = jnp.where(first_row, jnp.exp(tot[h0:h0 + 1]), jnp.exp(tot[h1:h1 + 1]))
        new_state.append(s_pair * chunk_decay + _tn(bm, x_pair * to_end))
        ys.append(y_pair + dsk[:, sl] * xs[:, sl])
    y = jnp.concatenate(ys, axis=1) * _silu(z)
    y = y * lax.rsqrt(jnp.mean(y * y, axis=-1, keepdims=True) + RMS_EPS) * nw
    return y, jnp.concatenate(new_state, axis=1)


def _attn_block(q, kp, kc, vp, vc, sinks, n, kv):
    t = q.shape[0]
    kb = jnp.concatenate([kp, kc], axis=0)
    vb = jnp.concatenate([vp, vc], axis=0)
    qi = lax.broadcasted_iota(jnp.int32, (t, 2 * t), 0)
    kj = lax.broadcasted_iota(jnp.int32, (t, 2 * t), 1)
    rel = qi + t - kj
    valid = (rel >= 0) & (rel < t) & ((n * t - t + kj) >= 0)
    relf = rel.astype(F32)
    first = lax.broadcasted_iota(jnp.int32, (t, BLK), 1) < 64
    lane16 = lax.broadcasted_iota(jnp.int32, (1, ATT_HEADS), 1)
    outs = []
    for p in range(4):
        qp = q[:, BLK * p:BLK * (p + 1)] * (ATT_HEAD_DIM ** -0.5)
        o_pair = None
        for half, keep in enumerate((first, jnp.logical_not(first))):
            h = kv * 8 + 2 * p + half
            s = _nt(jnp.where(keep, qp, 0.0), kb)
            slope = jnp.exp((-8.0 * math.log(2.0) / ATT_HEADS) * jnp.asarray(h + 1, F32))
            s = jnp.where(valid, s - slope * relf, -1e30)
            sink = jnp.sum(jnp.where(lane16 == h, sinks, 0.0), axis=1, keepdims=True)
            m = lax.stop_gradient(jnp.maximum(jnp.max(s, axis=1, keepdims=True), sink))
            e = jnp.exp(s - m)
            den = jnp.sum(e, axis=1, keepdims=True) + jnp.exp(sink - m)
            o = _nn(e * (1.0 / den), vb)
            o_pair = o if o_pair is None else jnp.where(first, o_pair, o)
        outs.append(o_pair)
    return jnp.concatenate(outs, axis=1)


def _chip_exchange(src, out, send_sems, recv_sems, bcast):
    x, y, c = lax.axis_index("x"), lax.axis_index("y"), lax.axis_index("c")
    me = 2 * x + y

    def copy(a, d, slot):
        j = jnp.bitwise_xor(me, d)
        return pltpu.make_async_remote_copy(
            src_ref=src[a] if bcast else src[a].at[j], dst_ref=out[a].at[slot],
            send_sem=send_sems.at[a * 4 + d], recv_sem=recv_sems.at[a * 4 + d],
            device_id=(j // 2, j % 2, c), device_id_type=pl.DeviceIdType.MESH)

    pairs = [(a, d) for a in range(len(src)) for d in range(1, 4)]

    def start():
        for a, d in pairs:
            copy(a, d, me).start()

    def wait():
        for a, d in pairs:
            copy(a, d, jnp.bitwise_xor(me, d)).wait_recv()
        for a, d in pairs:
            copy(a, d, me).wait_send()

    return start, wait


def _mm(a, b, mode, out_dtype, tm, tn, tk, name, hosted=None):
    if mode == "nn":
        (m, k), (k2, n) = a.shape, b.shape
    elif mode == "nt":
        (m, k), (n, k2) = a.shape, b.shape
    else:
        (k, m), (k2, n) = a.shape, b.shape
    assert k == k2, (a.shape, b.shape, mode)
    tm, tn, tk = min(tm, m), min(tn, n), min(tk, k)
    assert m % tm == 0 and n % tn == 0 and k % tk == 0, (m, n, k, tm, tn, tk)
    nk = k // tk
    grid = (m // tm, n // tn, nk)
    ca, cb = {"nn": (1, 0), "nt": (1, 1), "tn": (0, 0)}[mode]
    h_arrays, h_bcast = hosted if hosted else ((), False)
    nh = len(h_arrays)

    def body(*refs):
        a_ref, b_ref = refs[:2]
        o_ref = refs[2 + nh]
        scratch = refs[3 + 2 * nh:]
        step = [pl.program_id(ax) for ax in range(3)]
        if nh:
            start, wait = _chip_exchange(refs[2:2 + nh], refs[3 + nh:3 + 2 * nh], scratch[-2], scratch[-1], h_bcast)

            @pl.when((step[0] == 0) & (step[1] == 0) & (step[2] == 0))
            def _():
                start()

        part = _dot(a_ref[...], b_ref[...], ca, cb)
        if nk == 1:
            o_ref[...] = part.astype(o_ref.dtype)
        else:
            acc_ref = scratch[0]

            @pl.when(step[2] == 0)
            def _():
                acc_ref[...] = part

            @pl.when(step[2] > 0)
            def _():
                acc_ref[...] += part

            @pl.when(step[2] == nk - 1)
            def _():
                o_ref[...] = acc_ref[...].astype(o_ref.dtype)

        if nh:
            @pl.when((step[0] == grid[0] - 1) & (step[1] == grid[1] - 1) & (step[2] == nk - 1))
            def _():
                wait()

    a_spec = pl.BlockSpec((tk, tm), lambda i, j, kk: (kk, i)) if mode == "tn" else pl.BlockSpec((tm, tk), lambda i, j, kk: (i, kk))
    b_spec = pl.BlockSpec((tn, tk), lambda i, j, kk: (j, kk)) if mode == "nt" else pl.BlockSpec((tk, tn), lambda i, j, kk: (kk, j))
    any_spec = pl.BlockSpec(memory_space=pl.ANY)
    res = pl.pallas_call(
        body, name=name, grid=grid,
        in_specs=[a_spec, b_spec] + [any_spec] * nh,
        out_specs=[pl.BlockSpec((tm, tn), lambda i, j, kk: (i, j))] + [any_spec] * nh,
        out_shape=[jax.ShapeDtypeStruct((m, n), out_dtype)]
        + [jax.ShapeDtypeStruct(((4,) + s.shape) if h_bcast else s.shape, s.dtype) for s in h_arrays],
        scratch_shapes=([] if nk == 1 else [pltpu.VMEM((tm, tn), F32)])
        + ([pltpu.SemaphoreType.DMA((4 * nh,)), pltpu.SemaphoreType.DMA((4 * nh,))] if nh else []),
        compiler_params=_params(3),
    )(a, b, *h_arrays)
    return (res[0], list(res[1:])) if nh else res[0]


def _rows(fn, rows, params, outs, accs, tile, name):
    length = rows[0][0].shape[0]
    tile = min(tile, length)
    assert length % tile == 0
    nr, npar, no = len(rows), len(params), len(outs)

    def body(*refs):
        vals = [r[...] for r in refs[:nr + npar]]
        o, a = fn(*vals)
        for ref, val in zip(refs[nr + npar:nr + npar + no], o):
            ref[...] = val.astype(ref.dtype)
        i = pl.program_id(0)
        for ref, val in zip(refs[nr + npar + no:], a):
            @pl.when(i == 0)
            def _(ref=ref, val=val):
                ref[...] = val

            @pl.when(i > 0)
            def _(ref=ref, val=val):
                ref[...] += val

    in_specs = [pl.BlockSpec((tile, w), functools.partial(lambda i, cb: (i, cb), cb=cb)) for _, cb, w in rows]
    in_specs += [pl.BlockSpec(p.shape, lambda i: (0, 0)) for p in params]
    outs = [o if len(o) == 3 else (o[0], o[1], o[0]) for o in outs]
    out_specs = [pl.BlockSpec((tile, w), lambda i: (i, 0)) for w, _, _ in outs]
    out_specs += [pl.BlockSpec((r, w), lambda i: (0, 0)) for r, w in accs]
    out_shape = [jax.ShapeDtypeStruct((length, full), dt) for _, dt, full in outs]
    out_shape += [jax.ShapeDtypeStruct((r, w), F32) for r, w in accs]
    res = pl.pallas_call(
        body, name=name, grid=(length // tile,), in_specs=in_specs, out_specs=out_specs, out_shape=out_shape,
        compiler_params=_params(1),
    )(*[r[0] for r in rows], *params)
    return res[:no], res[no:]


CONV_SUB = 32


def _conv_fwd(proj, conv_w, conv_b, name, hosted=None):
    length = proj.shape[0]
    tl, cw = min(1024, length), 512
    cb0 = _OFF["xs"] // cw
    grid = (CONV_DIM // cw, length // tl)
    h_arrays, h_bcast = hosted if hosted else ((), False)
    nh = len(h_arrays)

    def body(*refs):
        u_ref, halo_ref, w_ref, b_ref = refs[:4]
        o_ref = refs[4 + nh]
        win = refs[5 + 2 * nh]
        j, i = pl.program_id(0), pl.program_id(1)
        if nh:
            start, wait = _chip_exchange(refs[4:4 + nh], refs[5 + nh:5 + 2 * nh], refs[-2], refs[-1], h_bcast)

            @pl.when((j == 0) & (i == 0))
            def _():
                start()

            @pl.when((j == grid[0] - 1) & (i == grid[1] - 1))
            def _():
                wait()

        win[0:8, :] = jnp.where(i > 0, halo_ref[...], 0.0)
        win[8:8 + tl, :] = u_ref[...]
        taps = [w_ref[kk:kk + 1, :] for kk in range(SSD_CONV)]
        bias = b_ref[...]
        for r in range(0, tl, CONV_SUB):
            acc = bias + taps[0] * win[5 + r:5 + r + CONV_SUB, :]
            for kk in range(1, SSD_CONV):
                acc = acc + taps[kk] * win[5 + kk + r:5 + kk + r + CONV_SUB, :]
            o_ref[r:r + CONV_SUB, :] = acc

    any_spec = pl.BlockSpec(memory_space=pl.ANY)
    res = pl.pallas_call(
        body, name=name, grid=grid,
        in_specs=[pl.BlockSpec((tl, cw), lambda j, i: (i, cb0 + j)),
                  pl.BlockSpec((8, cw), lambda j, i: (jnp.maximum(i * (tl // 8) - 1, 0), cb0 + j)),
                  pl.BlockSpec((SSD_CONV, cw), lambda j, i: (0, j)),
                  pl.BlockSpec((1, cw), lambda j, i: (0, j))] + [any_spec] * nh,
        out_specs=[pl.BlockSpec((tl, cw), lambda j, i: (i, j))] + [any_spec] * nh,
        out_shape=[jax.ShapeDtypeStruct((length, CONV_DIM), F32)]
        + [jax.ShapeDtypeStruct(((4,) + s.shape) if h_bcast else s.shape, s.dtype) for s in h_arrays],
        scratch_shapes=[pltpu.VMEM((8 + tl, cw), F32)]
        + ([pltpu.SemaphoreType.DMA((4 * nh,)), pltpu.SemaphoreType.DMA((4 * nh,))] if nh else []),
        compiler_params=_params(2),
    )(proj, proj, conv_w, conv_b, *h_arrays)
    return (res[0], list(res[1:])) if nh else res[0]


def _conv_bwd(dpre, proj, conv_w, dproj, part, name):
    length = proj.shape[0]
    width = dpre.shape[1]
    tl, cw = min(1024, length), 512
    cb0 = _OFF["xs"] // cw + part
    n_t = length // tl

    def body(d_ref, dnext_ref, u_ref, halo_ref, w_ref, _, du_ref, dw_ref, db_ref, dwin, uwin):
        i = pl.program_id(1)
        dwin[0:tl, :] = d_ref[...].astype(F32)
        dwin[tl:tl + 8, :] = jnp.where(i < n_t - 1, dnext_ref[...].astype(F32)[0:8], 0.0)
        uwin[0:8, :] = jnp.where(i > 0, halo_ref[...], 0.0)
        uwin[8:8 + tl, :] = u_ref[...]
        taps = [w_ref[kk:kk + 1, :] for kk in range(SSD_CONV)]
        sub = CONV_SUB // 2
        acc_w = [jnp.zeros((sub, cw), F32) for _ in range(SSD_CONV)]
        acc_b = jnp.zeros((sub, cw), F32)
        for r in range(0, tl, sub):
            d = dwin[r:r + sub, :]
            du = taps[3] * d
            for kk in range(SSD_CONV - 1):
                du = du + taps[kk] * dwin[3 - kk + r:3 - kk + r + sub, :]
            du_ref[r:r + sub, :] = du.astype(du_ref.dtype)
            for kk in range(SSD_CONV):
                acc_w[kk] = acc_w[kk] + d * uwin[5 + kk + r:5 + kk + r + sub, :]
            acc_b = acc_b + d
        dw = jnp.concatenate([jnp.sum(a, axis=0, keepdims=True) for a in acc_w], axis=0)
        db = jnp.sum(acc_b, axis=0, keepdims=True)

        @pl.when(i == 0)
        def _():
            dw_ref[...] = dw
            db_ref[...] = db

        @pl.when(i > 0)
        def _():
            dw_ref[...] += dw
            db_ref[...] += db

    return pl.pallas_call(
        body, name=name, grid=(width // cw, n_t), input_output_aliases={5: 0},
        in_specs=[pl.BlockSpec((tl, cw), lambda j, i: (i, j)),
                  pl.BlockSpec((16, cw), lambda j, i: (jnp.minimum((i + 1) * (tl // 16), length // 16 - 1), j)),
                  pl.BlockSpec((tl, cw), lambda j, i: (i, cb0 + j)),
                  pl.BlockSpec((8, cw), lambda j, i: (jnp.maximum(i * (tl // 8) - 1, 0), cb0 + j)),
                  pl.BlockSpec((SSD_CONV, cw), lambda j, i: (0, part + j)),
                  pl.BlockSpec(memory_space=pl.ANY)],
        out_specs=[pl.BlockSpec((tl, cw), lambda j, i: (i, cb0 + j)),
                   pl.BlockSpec((SSD_CONV, cw), lambda j, i: (0, j)),
                   pl.BlockSpec((1, cw), lambda j, i: (0, j))],
        out_shape=[jax.ShapeDtypeStruct((length, NP), BF),
                   jax.ShapeDtypeStruct((SSD_CONV, width), F32),
                   jax.ShapeDtypeStruct((1, width), F32)],
        scratch_shapes=[pltpu.VMEM((tl + 8, cw), F32), pltpu.VMEM((8 + tl, cw), F32)],
        compiler_params=_params(2),
    )(dpre, dpre, proj, proj, conv_w, dproj)


def _ssd_in_specs(rev, nc):
    def cidx(c):
        return nc - 1 - c if rev else c
    whole = lambda c: (0, 0)
    return [pl.BlockSpec((BLK, SSD_D_INNER), lambda c: (cidx(c), 0)),
            pl.BlockSpec((BLK, BC_DIM), lambda c: (cidx(c), SSD_D_INNER // BC_DIM)),
            pl.BlockSpec((BLK, BC_DIM), lambda c: (cidx(c), SSD_D_INNER // BC_DIM + 1)),
            pl.BlockSpec((BLK, SSD_D_INNER), lambda c: (cidx(c), _OFF["z"] // SSD_D_INNER)),
            pl.BlockSpec((SSD_HEADS, BLK), lambda c: (0, cidx(c))),
            pl.BlockSpec((SSD_HEADS, 1), whole),
            pl.BlockSpec((SSD_HEADS, 1), whole),
            pl.BlockSpec((1, SSD_D_INNER), whole),
            pl.BlockSpec((1, SSD_D_INNER), whole)]


def _group_args(g, px, pb, pc, z, dtr, dtb, alog, dsk, nw):
    wide, narrow, heads = slice(512 * g, 512 * (g + 1)), slice(BLK * g, BLK * (g + 1)), slice(8 * g, 8 * (g + 1))
    return (px[:, wide], pb[:, narrow], pc[:, narrow], z[:, wide], dtr[heads, :], dtb[heads, :], alog[heads, :],
            dsk[:, wide], nw[:, wide])


def _ssd_fwd(pre, proj, dt_t, dtb, alog, dsk, nw, name, hosted=None):
    length = pre.shape[0]
    nc = length // BLK
    h_arrays, h_bcast = hosted if hosted else ((), False)
    nh = len(h_arrays)

    def body(*refs):
        px, pb, pc, z, dtr, dtb_r, al_r, dsk_r, nw_r = refs[:9]
        y_ref, sin_ref = refs[9 + nh:11 + nh]
        st = refs[11 + 2 * nh]
        c = pl.program_id(0)
        if nh:
            start, wait = _chip_exchange(refs[9:9 + nh], refs[11 + nh:11 + 2 * nh], refs[-2], refs[-1], h_bcast)

        @pl.when(c == 0)
        def _():
            st[...] = jnp.zeros_like(st)
            if nh:
                start()

        for g in range(SSD_GROUPS):
            s_in = st[g]
            sin_ref[g] = s_in
            y, s_out = _ssd_group(*_group_args(g, px, pb, pc, z, dtr, dtb_r, al_r, dsk_r, nw_r), s_in)
            y_ref[:, 512 * g:512 * (g + 1)] = y.astype(y_ref.dtype)
            st[g] = s_out

        if nh:
            @pl.when(c == nc - 1)
            def _():
                wait()

    any_spec = pl.BlockSpec(memory_space=pl.ANY)
    res = pl.pallas_call(
        body, name=name, grid=(nc,), in_specs=_ssd_in_specs(False, nc) + [any_spec] * nh,
        out_specs=[pl.BlockSpec((BLK, SSD_D_INNER), lambda c: (c, 0)),
                   pl.BlockSpec((SSD_GROUPS, None, SSD_STATE, 512), lambda c: (0, c, 0, 0))] + [any_spec] * nh,
        out_shape=[jax.ShapeDtypeStruct((length, SSD_D_INNER), BF),
                   jax.ShapeDtypeStruct((SSD_GROUPS, nc, SSD_STATE, 512), F32)]
        + [jax.ShapeDtypeStruct(((4,) + s.shape) if h_bcast else s.shape, s.dtype) for s in h_arrays],
        scratch_shapes=[pltpu.VMEM((SSD_GROUPS, SSD_STATE, 512), F32)]
        + ([pltpu.SemaphoreType.DMA((4 * nh,)), pltpu.SemaphoreType.DMA((4 * nh,))] if nh else []),
        compiler_params=_params(1),
    )(pre, pre, pre, proj, dt_t, dtb, alog, dsk, nw, *h_arrays)
    return (res[0], res[1], list(res[2:])) if nh else (res[0], res[1])


def _ssd_bwd(pre, proj, dt_t, dtb, alog, dsk, nw, s_in, dy, dproj, name, hosted=None):
    length = pre.shape[0]
    nc = length // BLK
    h_arrays, h_bcast = hosted if hosted else ((), False)
    nh = len(h_arrays)

    def body(*refs):
        px, pb, pc, z, dtr, dtb_r, al_r, dsk_r, nw_r, sin_r, dy_r = refs[:11]
        dpre, dz, ddt, ddtb, dal, ddsk, dnw = refs[12 + nh:19 + nh]
        dst = refs[19 + 2 * nh]
        c = pl.program_id(0)
        if nh:
            start, wait = _chip_exchange(refs[12:12 + nh], refs[19 + nh:19 + 2 * nh], refs[-2], refs[-1], h_bcast)

        @pl.when(c == 0)
        def _():
            dst[...] = jnp.zeros_like(dst)
            if nh:
                start()

        for grp in range(SSD_GROUPS):
            wide, heads = slice(512 * grp, 512 * (grp + 1)), slice(8 * grp, 8 * (grp + 1))
            _, vjp = jax.vjp(_ssd_group, *_group_args(grp, px, pb, pc, z, dtr, dtb_r, al_r, dsk_r, nw_r), sin_r[grp])
            g = vjp((dy_r[:, wide], dst[grp]))
            dpre[:, wide] = g[0].astype(dpre.dtype)
            dpre[:, slice(SSD_D_INNER + BLK * grp, SSD_D_INNER + BLK * (grp + 1))] = g[1].astype(dpre.dtype)
            dpre[:, slice(SSD_D_INNER + BC_DIM + BLK * grp, SSD_D_INNER + BC_DIM + BLK * (grp + 1))] = g[2].astype(dpre.dtype)
            dz[:, wide] = g[3].astype(dz.dtype)
            ddt[heads, :] = g[4]
            dst[grp] = g[9]
            for ref, val, idx in ((ddtb, g[5], (heads, slice(None))), (dal, g[6], (heads, slice(None))),
                                  (ddsk, g[7], (slice(None), wide)), (dnw, g[8], (slice(None), wide))):
                @pl.when(c == 0)
                def _(ref=ref, val=val, idx=idx):
                    ref[idx] = val

                @pl.when(c > 0)
                def _(ref=ref, val=val, idx=idx):
                    ref[idx] += val

        if nh:
            @pl.when(c == nc - 1)
            def _():
                wait()

    rev = lambda c: nc - 1 - c
    whole = lambda c: (0, 0)
    any_spec = pl.BlockSpec(memory_space=pl.ANY)
    in_specs = _ssd_in_specs(True, nc) + [
        pl.BlockSpec((SSD_GROUPS, None, SSD_STATE, 512), lambda c: (0, rev(c), 0, 0)),
        pl.BlockSpec((BLK, SSD_D_INNER), lambda c: (rev(c), 0)),
        any_spec] + [any_spec] * nh
    res = pl.pallas_call(
        body, name=name, grid=(nc,), in_specs=in_specs, input_output_aliases={11: 1},
        out_specs=[pl.BlockSpec((BLK, CONV_DIM), lambda c: (rev(c), 0)),
                   pl.BlockSpec((BLK, SSD_D_INNER), lambda c: (rev(c), _OFF["z"] // SSD_D_INNER)),
                   pl.BlockSpec((SSD_HEADS, BLK), lambda c: (0, rev(c))),
                   pl.BlockSpec((SSD_HEADS, 1), whole),
                   pl.BlockSpec((SSD_HEADS, 1), whole),
                   pl.BlockSpec((1, SSD_D_INNER), whole),
                   pl.BlockSpec((1, SSD_D_INNER), whole)] + [any_spec] * nh,
        out_shape=[jax.ShapeDtypeStruct((length, CONV_DIM), BF),
                   jax.ShapeDtypeStruct((length, NP), BF),
                   jax.ShapeDtypeStruct((SSD_HEADS, length), F32),
                   jax.ShapeDtypeStruct((SSD_HEADS, 1), F32),
                   jax.ShapeDtypeStruct((SSD_HEADS, 1), F32),
                   jax.ShapeDtypeStruct((1, SSD_D_INNER), F32),
                   jax.ShapeDtypeStruct((1, SSD_D_INNER), F32)]
        + [jax.ShapeDtypeStruct(((4,) + s.shape) if h_bcast else s.shape, s.dtype) for s in h_arrays],
        scratch_shapes=[pltpu.VMEM((SSD_GROUPS, SSD_STATE, 512), F32)]
        + ([pltpu.SemaphoreType.DMA((4 * nh,)), pltpu.SemaphoreType.DMA((4 * nh,))] if nh else []),
        compiler_params=_params(1),
    )(pre, pre, pre, proj, dt_t, dtb, alog, dsk, nw, s_in, dy, dproj, *h_arrays)
    return tuple(res[:7]) + ((list(res[7:]),) if nh else ())


assert _OFF["gates"] == 0 and _OFF["z"] % SSD_D_INNER == 0 and _OFF["q"] % D_MODEL == 0 and _OFF["k"] + 512 == NP


def _attn_in_specs():
    prev = lambda n: (0, jnp.maximum(n - 1, 0), 0)
    cur = lambda n: (0, n, 0)
    return [pl.BlockSpec((BLK, D_MODEL), lambda n: (n, _OFF["q"] // D_MODEL)),
            pl.BlockSpec((2, BLK, BLK), prev),
            pl.BlockSpec((2, BLK, BLK), cur),
            pl.BlockSpec((2, BLK, BLK), prev),
            pl.BlockSpec((2, BLK, BLK), cur),
            pl.BlockSpec((1, ATT_HEADS), lambda n: (0, 0))]


def _attn_fwd(proj, k2, v2, sinks, name, hosted=None):
    length = proj.shape[0]
    nb = length // BLK
    h_arrays, h_bcast = hosted if hosted else ((), False)
    nh = len(h_arrays)

    def body(*refs):
        q, kp, kc, vp, vc, sk = refs[:6]
        o_ref = refs[6 + nh]
        n = pl.program_id(0)
        if nh:
            start, wait = _chip_exchange(refs[6:6 + nh], refs[7 + nh:7 + 2 * nh], refs[-2], refs[-1], h_bcast)

            @pl.when(n == 0)
            def _():
                start()

        for kv in range(2):
            cols = slice(512 * kv, 512 * (kv + 1))
            o = _attn_block(q[:, cols], kp[kv], kc[kv], vp[kv], vc[kv], sk[...], n, kv)
            o_ref[:, cols] = o.astype(o_ref.dtype)

        if nh:
            @pl.when(n == nb - 1)
            def _():
                wait()

    any_spec = pl.BlockSpec(memory_space=pl.ANY)
    res = pl.pallas_call(
        body, name=name, grid=(nb,), in_specs=_attn_in_specs() + [any_spec] * nh,
        out_specs=[pl.BlockSpec((BLK, D_MODEL), lambda n: (n, 0))] + [any_spec] * nh,
        out_shape=[jax.ShapeDtypeStruct((length, D_MODEL), BF)]
        + [jax.ShapeDtypeStruct(((4,) + s.shape) if h_bcast else s.shape, s.dtype) for s in h_arrays],
        scratch_shapes=[pltpu.SemaphoreType.DMA((4 * nh,)), pltpu.SemaphoreType.DMA((4 * nh,))] if nh else [],
        compiler_params=_params(1),
    )(proj, k2, k2, v2, v2, sinks, *h_arrays)
    return (res[0], list(res[1:])) if nh else res[0]


def _attn_bwd(proj, k2, v2, sinks, datt, dproj, name):
    length = proj.shape[0]

    def body(q, kp, kc, vp, vc, sk, do, _, dq, dkp, dkc, dvp, dvc, dsk):
        n = pl.program_id(0)
        dsinks = None
        for kv in range(2):
            cols = slice(512 * kv, 512 * (kv + 1))
            _, vjp = jax.vjp(lambda *a: _attn_block(*a, n, kv), q[:, cols], kp[kv], kc[kv], vp[kv], vc[kv], sk[...])
            g = vjp(do[:, cols].astype(F32))
            dq[:, cols] = g[0].astype(dq.dtype)
            dkp[kv] = g[1]
            dkc[kv] = g[2]
            dvp[kv] = g[3]
            dvc[kv] = g[4]
            dsinks = g[5] if dsinks is None else dsinks + g[5]

        @pl.when(n == 0)
        def _():
            dsk[...] = dsinks

        @pl.when(n > 0)
        def _():
            dsk[...] += dsinks

    blk3 = pl.BlockSpec((2, BLK, BLK), lambda n: (0, n, 0))
    kv_shape = jax.ShapeDtypeStruct((2, length, BLK), F32)
    return pl.pallas_call(
        body, name=name, grid=(length // BLK,),
        in_specs=_attn_in_specs() + [pl.BlockSpec((BLK, D_MODEL), lambda n: (n, 0)), pl.BlockSpec(memory_space=pl.ANY)],
        out_specs=[pl.BlockSpec((BLK, D_MODEL), lambda n: (n, _OFF["q"] // D_MODEL)), blk3, blk3, blk3, blk3,
                   pl.BlockSpec((1, ATT_HEADS), lambda n: (0, 0))],
        out_shape=[jax.ShapeDtypeStruct((length, NP), BF), kv_shape, kv_shape, kv_shape, kv_shape,
                   jax.ShapeDtypeStruct((1, ATT_HEADS), F32)],
        input_output_aliases={7: 0},
        compiler_params=_params(1),
    )(proj, k2, k2, v2, v2, sinks, datt, dproj)


DMA_CHUNK_BYTES = 1 << 20
N_STAGE = 12
LOOKAHEAD = 5


def _piece_chunks(shape, itemsize):
    if len(shape) < 2 or shape[-2] % 16 != 0:
        return [()]
    rows, cols = shape[-2:]
    step = min(rows, max(16, DMA_CHUNK_BYTES // (cols * itemsize) // 16 * 16))
    out = []
    for lead in itertools.product(*[range(d) for d in shape[:-2]]):
        for r0 in range(0, rows, step):
            out.append(lead + (pl.ds(r0, min(step, rows - r0)),))
    return out


def _exchange(srcs, group, bcast, name):
    size = {"c": 2, "xy": 4, "all": 8}[group]
    npeer = size - 1
    n = len(srcs)
    for s in srcs:
        assert bcast or s.shape[0] == size
    pieces = [s.shape if bcast else s.shape[1:] for s in srcs]
    chunks = [_piece_chunks(p, s.dtype.itemsize) for p, s in zip(pieces, srcs)]

    def body(*refs):
        src, out = refs[:n], refs[n:2 * n]
        load_sems, send_sems, recv_sems = refs[2 * n:]
        x, y, c = lax.axis_index("x"), lax.axis_index("y"), lax.axis_index("c")
        if group == "c":
            me = c
        elif group == "xy":
            me = 2 * x + y
        else:
            me = 4 * x + 2 * y + c

        def device(j):
            if group == "c":
                return (x, y, j)
            if group == "xy":
                return (j // 2, j % 2, c)
            return (j // 4, (j // 2) % 2, j % 2)

        def part(ref, idx):
            return ref.at[idx] if idx else ref

        def piece(a, j):
            return src[a] if bcast else src[a].at[j]

        for a in range(n):
            jobs = [(None, idx) for idx in chunks[a]] if bcast else [(d, idx) for idx in chunks[a] for d in range(1, size)]
            full_rows = chunks[a][0][-1].size if chunks[a][0] else None
            slot_shape = (full_rows, pieces[a][-1]) if chunks[a][0] else pieces[a]

            def stream(stage, a=a, jobs=jobs, full_rows=full_rows):
                def slot(q):
                    idx = jobs[q][1]
                    view = stage.at[q % N_STAGE]
                    if idx and idx[-1].size != full_rows:
                        view = view.at[pl.ds(0, idx[-1].size)]
                    return view

                def load(q):
                    d, idx = jobs[q]
                    j = me if d is None else jnp.bitwise_xor(me, d)
                    return pltpu.make_async_copy(part(piece(a, j), idx), slot(q), load_sems.at[q % N_STAGE])

                def sends(q):
                    d, idx = jobs[q]
                    return [pltpu.make_async_remote_copy(
                        src_ref=slot(q), dst_ref=part(out[a].at[me], idx),
                        send_sem=send_sems.at[(q % N_STAGE) * npeer + dd - 1], recv_sem=recv_sems.at[a * size + dd],
                        device_id=device(jnp.bitwise_xor(me, dd)), device_id_type=pl.DeviceIdType.MESH)
                        for dd in (range(1, size) if d is None else (d,))]

                nq = len(jobs)
                for q in range(nq + LOOKAHEAD):
                    if q < nq:
                        if q >= N_STAGE:
                            for cp in sends(q - N_STAGE):
                                cp.wait_send()
                        load(q).start()
                    if q >= LOOKAHEAD:
                        load(q - LOOKAHEAD).wait()
                        for cp in sends(q - LOOKAHEAD):
                            cp.start()
                for q in range(max(0, nq - N_STAGE), nq):
                    for cp in sends(q):
                        cp.wait_send()

            pl.run_scoped(stream, pltpu.VMEM((N_STAGE,) + tuple(slot_shape), srcs[a].dtype))

        for a in range(n):
            for d in range(1, size):
                j = jnp.bitwise_xor(me, d)
                pltpu.make_async_remote_copy(
                    src_ref=piece(a, j), dst_ref=out[a].at[j], send_sem=send_sems.at[0], recv_sem=recv_sems.at[a * size + d],
                    device_id=device(j), device_id_type=pl.DeviceIdType.MESH).wait_recv()

    any_spec = pl.BlockSpec(memory_space=pl.ANY)
    return pl.pallas_call(
        body, name=name, in_specs=[any_spec] * n, out_specs=[any_spec] * n,
        out_shape=[jax.ShapeDtypeStruct((size,) + tuple(p), s.dtype) for p, s in zip(pieces, srcs)],
        scratch_shapes=[pltpu.SemaphoreType.DMA((N_STAGE,)), pltpu.SemaphoreType.DMA((N_STAGE * npeer,)),
                        pltpu.SemaphoreType.DMA((n * size,))],
        compiler_params=pltpu.CompilerParams(vmem_limit_bytes=VMEM_LIMIT_BYTES),
    )(*srcs)


def _with_own(outs, owns, me):
    return [lax.dynamic_update_index_in_dim(o, w, me, 0) for o, w in zip(outs, owns)]


def _sum_slots(arr, name):
    k = arr.shape[0]
    rest = arr.shape[1:]
    width = rest[-1]
    rows_per = math.prod(rest[:-1])
    a2 = arr.reshape(k * rows_per, width)
    tile = rows_per
    for cand in (256, 128, 64, 32, 16, 8):
        if rows_per % cand == 0:
            tile = cand
            break
    nt = rows_per // tile

    def body(*refs):
        acc = refs[0][...]
        for r in refs[1:k]:
            acc = acc + r[...]
        refs[k][...] = acc

    in_specs = [pl.BlockSpec((tile, width), functools.partial(lambda i, s: (s * nt + i, 0), s=s)) for s in range(k)]
    out = pl.pallas_call(
        body, name=name, grid=(nt,), in_specs=in_specs, out_specs=pl.BlockSpec((tile, width), lambda i: (i, 0)),
        out_shape=jax.ShapeDtypeStruct((rows_per, width), arr.dtype), compiler_params=_params(1),
    )(*([a2] * k))
    return out.reshape(rest)


def _sum_pieces(recv, own, own_slotted, me, name):
    k = recv.shape[0]
    rest = recv.shape[1:]
    width = rest[-1]
    rows_per = math.prod(rest[:-1])
    r2 = recv.reshape(k * rows_per, width)
    o2 = own.reshape(-1, width)
    tile = rows_per
    for cand in (256, 128, 64, 32, 16, 8):
        if rows_per % cand == 0:
            tile = cand
            break
    nt = rows_per // tile

    def body(me_ref, *refs):
        mine = refs[k][...].astype(F32)
        acc = None
        for j in range(k):
            term = jnp.where(me_ref[0] == j, mine, refs[j][...].astype(F32))
            acc = term if acc is None else acc + term
        refs[k + 1][...] = acc
        refs[k + 2][...] = acc.astype(BF)

    in_specs = [pl.BlockSpec((tile, width), functools.partial(lambda i, m, s: (s * nt + i, 0), s=s)) for s in range(k)]
    in_specs.append(pl.BlockSpec((tile, width), (lambda i, m: (m[0] * nt + i, 0)) if own_slotted else (lambda i, m: (i, 0))))
    out_spec = pl.BlockSpec((tile, width), lambda i, m: (i, 0))
    out, out_bf = pl.pallas_call(
        body, name=name,
        grid_spec=pltpu.PrefetchScalarGridSpec(num_scalar_prefetch=1, grid=(nt,), in_specs=in_specs, out_specs=[out_spec, out_spec]),
        out_shape=[jax.ShapeDtypeStruct((rows_per, width), F32), jax.ShapeDtypeStruct((rows_per, width), BF)],
        compiler_params=_params(1),
    )(jnp.reshape(me, (1,)).astype(jnp.int32), *([r2] * k), o2)
    return out.reshape(rest), out_bf.reshape(rest)


def _adamw(w, g, m, v, name):
    def fn(w_, g_, m_, v_):
        m1 = ADAM_B1 * m_ + (1.0 - ADAM_B1) * g_
        v1 = ADAM_B2 * v_ + (1.0 - ADAM_B2) * (g_ * g_)
        m_hat = m1 / (1.0 - ADAM_B1 ** ADAM_STEP)
        v_hat = v1 / (1.0 - ADAM_B2 ** ADAM_STEP)
        delta = -ADAM_LR * (m_hat / (jnp.sqrt(v_hat) + ADAM_EPS) + ADAM_WD * w_)
        return (delta, m1, v1), ()

    rows, width = w.shape
    tile = rows
    for cand in (256, 128, 64, 32, 16, 8):
        if rows % cand == 0:
            tile = cand
            break
    o, _ = _rows(fn, [(w, 0, width), (g, 0, width), (m, 0, width), (v, 0, width)], [],
                 [(width, F32)] * 3, [], tile, name)
    return o


def _col_pieces(start, width, shard):
    out = []
    while width:
        chip, off = divmod(start, shard)
        take = min(width, shard - off)
        out.append((chip, off, take))
        start, width = start + take, width - take
    return out


def _gather_cols(parts, ranges, pad=0):
    shard = parts.shape[3]
    halves = []
    for h in range(2):
        cols = [parts[h, chip, :, off:off + w] for s, wd in ranges for chip, off, w in _col_pieces(s, wd, shard)]
        if pad:
            cols.append(jnp.zeros((parts.shape[2], pad), parts.dtype))
        halves.append(jnp.concatenate(cols, axis=1))
    return jnp.concatenate(halves, axis=0)


def _scatter_cols(g, ranges, shard):
    starts, pos = {}, 0
    for s, wd in ranges:
        starts[s] = (pos, wd)
        pos += wd
    order = sorted(starts)
    hr = g.shape[0] // 2
    out = []
    for h in range(2):
        per_chip = []
        for chip in range(4):
            cols = []
            for s in order:
                p0, wd = starts[s]
                lo, hi = max(s, chip * shard), min(s + wd, (chip + 1) * shard)
                if lo < hi:
                    cols.append(g[h * hr:(h + 1) * hr, p0 + lo - s:p0 + hi - s])
            per_chip.append(jnp.concatenate(cols, axis=1))
        out.append(jnp.stack(per_chip))
    return jnp.stack(out)


_IN_RANGES = tuple((o, wd) for _, o, wd in _PACK)


def _double_heads(t):
    length = t.shape[0]
    h = jnp.transpose(t.reshape(length, 2, 64), (1, 0, 2))
    return jnp.concatenate([h, h], axis=-1)


def _fold_heads(d_cur, d_prev):
    length = d_cur.shape[1]
    d = d_cur + jnp.concatenate([d_prev[:, BLK:], jnp.zeros((2, BLK, BLK), F32)], axis=1)
    d = d[..., :64] + d[..., 64:]
    return jnp.transpose(d, (1, 0, 2)).reshape(length, 128)


def _ln_fwd(a, r, g, b, name):
    def fn(a_, r_, g_, b_):
        y = _layer_norm(ALPHA * a_ + r_, g_, b_)
        return (y, y), ()
    o, _ = _rows(fn, [(a, 0, D_MODEL), (r, 0, D_MODEL)], [g, b], [(D_MODEL, F32), (D_MODEL, BF)], [], ROW_TILE, name)
    return o


def _ln_bwd(a, r, d1, d2, g, b, name):
    def fn(a_, r_, d1_, d2_, g_, b_):
        _, vjp = jax.vjp(_layer_norm, ALPHA * a_ + r_, g_, b_)
        ds, dg, db = vjp(ALPHA * d1_ + d2_)
        return (ds, ds), (dg, db)
    o, acc = _rows(fn, [(a, 0, D_MODEL), (r, 0, D_MODEL), (d1, 0, D_MODEL), (d2, 0, D_MODEL)], [g, b],
                   [(D_MODEL, F32), (D_MODEL, BF)], [(1, D_MODEL), (1, D_MODEL)], ROW_TILE, name)
    return o[0], o[1], acc[0], acc[1]


def _layer_fwd(l, h, hb, w, p, hosted=None, hosted_attn=None, hosted_conv=None, late_weights=None):
    sv = {"h": h, "hb": hb}
    proj = _mm(hb, w["win"], "nn", F32, 4096, 512, 1024, f"proj{l}")
    sv["proj"] = proj
    pre = _conv_fwd(proj, p["conv_w"], p["conv_b"], f"conv{l}", hosted_conv)
    if hosted_conv:
        pre, got = pre
        w.update(late_weights(got))
    sv["pre"] = pre
    dt_t = jnp.transpose(proj[:, _OFF["dt"]:_OFF["dt"] + SSD_HEADS])
    sv["dt_t"] = dt_t
    res = _ssd_fwd(pre, proj, dt_t, p["dt_bias"], p["a_log"], p["d_skip_c"], p["ssd_norm_w"], f"ssd{l}", hosted)
    yn, s_in = res[:2]
    if hosted:
        sv["hosted"] = res[2]
    sv["yn"], sv["s_in"] = yn, s_in
    ya = _mm(yn, w["wso"], "nn", BF, 2048, 1024, 2048, f"ssdout{l}")
    k2 = _double_heads(proj[:, _OFF["k"]:_OFF["k"] + 128])
    v2 = _double_heads(proj[:, _OFF["v"]:_OFF["v"] + 128])
    sv["k2"], sv["v2"] = k2, v2
    att = _attn_fwd(proj, k2, v2, p["att_sinks"], f"attn{l}", hosted_attn)
    if hosted_attn:
        att, sv["hosted_attn"] = att
    sv["att"] = att
    yb = _mm(att, w["wao"], "nn", BF, 2048, 1024, 1024, f"attout{l}")
    sv["ya"], sv["yb"] = ya, yb
    gcb = _OFF["gates"] // 1024

    def gate_fn(ga, gb, ya_, yb_):
        return (_sigmoid(ga) * ya_.astype(F32) + _sigmoid(gb) * yb_.astype(F32),), ()
    (ub,), _ = _rows(gate_fn, [(proj, gcb, 1024), (proj, gcb + 1, 1024), (ya, 0, 1024), (yb, 0, 1024)], [],
                     [(D_MODEL, BF)], [], ROW_TILE, f"gate{l}")
    sv["ub"] = ub
    mix = _mm(ub, w["wmo"], "nn", F32, 2048, 1024, 1024, f"mixout{l}")
    sv["mix"] = mix
    h1, h1b = _ln_fwd(h, mix, p["ln_mix_g"], p["ln_mix_b"], f"lnmix{l}")
    sv["h1"], sv["h1b"] = h1, h1b
    gu = _mm(h1b, w["wgu"], "nn", BF, 4096, 512, 1024, f"ffnin{l}")
    sv["gu"] = gu

    def act_fn(g_, u_):
        return (_silu(g_.astype(F32)) * u_.astype(F32),), ()
    (act,), _ = _rows(act_fn, [(gu, 0, FFN_HIDDEN), (gu, 1, FFN_HIDDEN)], [], [(FFN_HIDDEN, BF)], [], ROW_TILE, f"swiglu{l}")
    sv["act"] = act
    ffn = _mm(act, w["wd"], "nn", F32, 1024, 1024, FFN_HIDDEN, f"ffnout{l}")
    sv["ffn"] = ffn
    return sv


def _layer_bwd(l, sv, w, p, ds_f, ds_b, hosted=(), early_hook=None, late_hook=None):
    g = {}
    g["w_ffn_down"] = _mm(sv["act"], ds_b, "tn", F32, 1408, 1024, 2048, f"dwd{l}")
    dact = _mm(ds_b, w["wd"], "nt", BF, 1024, FFN_HIDDEN, 1024, f"dact{l}")

    def act_bwd(g_, u_, d_):
        _, vjp = jax.vjp(lambda a, b: _silu(a) * b, g_.astype(F32), u_.astype(F32))
        dg_, du_ = vjp(d_.astype(F32))
        return (jnp.concatenate([dg_, du_], axis=1),), ()
    (dgu,), _ = _rows(act_bwd, [(sv["gu"], 0, FFN_HIDDEN), (sv["gu"], 1, FFN_HIDDEN), (dact, 0, FFN_HIDDEN)], [],
                      [(2 * FFN_HIDDEN, BF)], [], ROW_TILE // 2, f"dswiglu{l}")
    dwgu = _mm(sv["h1b"], dgu, "tn", F32, 1024, 1408, 2048, f"dwgu{l}")
    g["w_ffn_gate"], g["w_ffn_up"] = dwgu[:, :FFN_HIDDEN], dwgu[:, FFN_HIDDEN:]
    dh1 = _mm(dgu, w["wgu"], "nt", F32, 1024, 1024, 2816, f"dh1{l}")
    ds2_f, ds2_b, g["ln_mix_g"], g["ln_mix_b"] = _ln_bwd(sv["h"], sv["mix"], ds_f, dh1, p["ln_mix_g"], p["ln_mix_b"], f"dlnmix{l}")
    g["w_mix_out"] = _mm(sv["ub"], ds2_b, "tn", F32, 1024, 1024, 2048, f"dwmo{l}")
    du = _mm(ds2_b, w["wmo"], "nt", BF, 1024, 1024, 1024, f"du{l}")
    proj = sv["proj"]
    gcb = _OFF["gates"] // 1024

    def gate_bwd(ga, gb, ya_, yb_, du_):
        _, vjp = jax.vjp(lambda a, b, c, d: _sigmoid(a) * c + _sigmoid(b) * d, ga, gb, ya_.astype(F32), yb_.astype(F32))
        dga, dgb, dya, dyb = vjp(du_.astype(F32))
        return (jnp.concatenate([dga, dgb], axis=1), dya, dyb), ()
    (dproj, dya, dyb), _ = _rows(
        gate_bwd, [(proj, gcb, 1024), (proj, gcb + 1, 1024), (sv["ya"], 0, 1024), (sv["yb"], 0, 1024), (du, 0, 1024)], [],
        [(2048, BF, NP), (D_MODEL, BF), (D_MODEL, BF)], [], ROW_TILE, f"dgate{l}")
    g["w_att_out"] = _mm(sv["att"], dyb, "tn", F32, 1024, 1024, 2048, f"dwao{l}")
    datt = _mm(dyb, w["wao"], "nt", BF, 1024, 1024, 1024, f"datt{l}")
    g["w_ssd_out"] = _mm(sv["yn"], dya, "tn", F32, 1024, 1024, 2048, f"dwso{l}")
    dyn = _mm(dya, w["wso"], "nt", F32, 1024, 1024, 1024, f"dyn{l}")
    hosted = list(hosted) + (list(early_hook(g)) if early_hook else [])
    dproj, dkp, dkc, dvp, dvc, g["att_sinks"] = _attn_bwd(proj, sv["k2"], sv["v2"], p["att_sinks"], datt, dproj, f"dattn{l}")
    res = _ssd_bwd(sv["pre"], proj, sv["dt_t"], p["dt_bias"], p["a_log"], p["d_skip_c"], p["ssd_norm_w"], sv["s_in"], dyn,
                   dproj, f"dssd{l}", (hosted, False) if hosted else None)
    (dpre, dproj, ddt_t, g["dt_bias"], g["a_log"], ddsk, g["ssd_norm_w"]) = res[:7]
    if hosted:
        g["hosted"] = res[7]
    g["d_skip"] = jnp.sum(ddsk.reshape(SSD_HEADS, 64), axis=1)
    dproj, g["conv_w"], g["conv_b"] = _conv_bwd(dpre, proj, p["conv_w"], dproj, 0, f"dconv{l}")
    length = proj.shape[0]
    tail = jnp.concatenate([_fold_heads(dkc, dkp).astype(BF), _fold_heads(dvc, dvp).astype(BF),
                            jnp.transpose(ddt_t).astype(BF), jnp.zeros((length, _PAD), BF)], axis=1)
    dproj = lax.dynamic_update_slice(dproj, tail, (0, _OFF["k"]))
    g["w_in"] = _mm(sv["hb"], dproj, "tn", F32, 1024, 2176, 1024, f"dwin{l}")
    late = list(late_hook(g)) if late_hook else []
    dh = _mm(dproj, w["win"], "nt", F32, 1024, 1024, 4352, f"dh{l}", (late, False) if late else None)
    if late:
        dh, g["late_hosted"] = dh
    return g, ds2_f, dh


def kernel(x, ln_in_g, ln_in_b, w_in, conv_w, conv_b, dt_bias, a_log, d_skip, ssd_norm_w, att_sinks, w_ssd_out, w_att_out, w_mix_out, ln_mix_g, ln_mix_b, w_ffn_gate, w_ffn_up, w_ffn_down, ln_ffn_g, ln_ffn_b, loss_target, m_ln_in_g, m_ln_in_b, m_w_in, m_conv_w, m_conv_b, m_dt_bias, m_a_log, m_d_skip, m_ssd_norm_w, m_att_sinks, m_w_ssd_out, m_w_att_out, m_w_mix_out, m_ln_mix_g, m_ln_mix_b, m_w_ffn_gate, m_w_ffn_up, m_w_ffn_down, m_ln_ffn_g, m_ln_ffn_b, v_ln_in_g, v_ln_in_b, v_w_in, v_conv_w, v_conv_b, v_dt_bias, v_a_log, v_d_skip, v_ssd_norm_w, v_att_sinks, v_w_ssd_out, v_w_att_out, v_w_mix_out, v_ln_mix_g, v_ln_mix_b, v_w_ffn_gate, v_w_ffn_up, v_w_ffn_down, v_ln_ffn_g, v_ln_ffn_b):
    env = dict(locals())
    wts = {n: env[n] for n in WEIGHTS}
    mom1 = {n: env["m_" + n] for n in WEIGHTS}
    mom2 = {n: env["v_" + n] for n in WEIGHTS}
    ci = lax.axis_index("c")
    chip = 2 * lax.axis_index("x") + lax.axis_index("y")
    xs_ = x[0]
    tgt = loss_target[0]

    COLS = ("w_in", "w_ffn_gate", "w_ffn_up")

    def half(t):
        return lax.dynamic_slice_in_dim(t, ci * (t.shape[0] // 2), t.shape[0] // 2, axis=0)

    def assemble(name, parts):
        if name == "w_in":
            return _gather_cols(parts, _IN_RANGES, _PAD)
        if name in COLS:
            return _gather_cols(parts, ((0, 4 * parts.shape[3]),))
        return jnp.concatenate([parts[h, s] for s in range(4) for h in range(2)], axis=0)

    def split(name, gfull):
        if name == "w_in":
            return _scatter_cols(gfull, _IN_RANGES, IN_DIM // 4)
        if name in COLS:
            return _scatter_cols(gfull, ((0, gfull.shape[1]),), gfull.shape[1] // 4)
        hr = gfull.shape[0] // 8
        return jnp.stack([jnp.stack([gfull[(2 * s + h) * hr:(2 * s + h + 1) * hr] for s in range(4)]) for h in range(2)])

    mine = [[half(wts[n][l].astype(BF)) for n in BIG] for l in range(DEPTH)]
    conv_all, = _with_own(_exchange([conv_w], "xy", True, "gather_conv"), [conv_w], chip)
    conv_full = jnp.transpose(conv_all, (1, 2, 0, 3)).reshape(DEPTH, SSD_CONV, CONV_DIM)

    KEYS = {"w_in": "win", "w_ssd_out": "wso", "w_att_out": "wao", "w_mix_out": "wmo", "w_ffn_down": "wd"}

    def layer_weights(l, names, by_chip):
        own = [mine[l][BIG.index(n)] for n in names]
        by_chip = _with_own(by_chip, own, chip)
        parts = _with_own(_exchange(by_chip, "c", True, f"gather_cores{l}_{names[0]}"), by_chip, ci)
        fw = {n: assemble(n, part) for n, part in zip(names, parts)}
        w = {KEYS[n]: fw[n] for n in names if n in KEYS}
        if "w_ffn_gate" in fw:
            w["wgu"] = jnp.concatenate([fw["w_ffn_gate"], fw["w_ffn_up"]], axis=1)
        return w

    layer_p = []
    for l in range(DEPTH):
        layer_p.append({
            "conv_w": conv_full[l], "conv_b": conv_b[l][None],
            "dt_bias": dt_bias[l][:, None], "a_log": a_log[l][:, None],
            "d_skip_c": jnp.repeat(d_skip[l], 64)[None], "ssd_norm_w": ssd_norm_w[l][None],
            "att_sinks": att_sinks[l][None], "ln_mix_g": ln_mix_g[l][None], "ln_mix_b": ln_mix_b[l][None],
            "ln_ffn_g": ln_ffn_g[l][None], "ln_ffn_b": ln_ffn_b[l][None],
        })

    def ln_in_fn(x_, g_, b_):
        y = _layer_norm(x_, g_, b_)
        return (y, y), ()
    (h, hb), _ = _rows(ln_in_fn, [(xs_, 0, D_MODEL)], [ln_in_g[None], ln_in_b[None]], [(D_MODEL, F32), (D_MODEL, BF)], [], ROW_TILE, "ln_in")
    rest = tuple(n for n in BIG if n != "w_in")
    layer_w = [layer_weights(0, ("w_in",), _exchange([mine[0][BIG.index("w_in")]], "xy", True, "gather_chips0"))]
    behind_ssd = [i for i, n in enumerate(BIG) if n in COLS]
    behind_attn = [i for i, n in enumerate(BIG) if n not in COLS]
    saved = []
    for l in range(DEPTH):
        nxt = l + 1 < DEPTH
        sv = _layer_fwd(l, h, hb, layer_w[l], layer_p[l], ([mine[l + 1][i] for i in behind_ssd], True) if nxt else None,
                        ([mine[l + 1][i] for i in behind_attn], True) if nxt else None,
                        ([mine[l][BIG.index(n)] for n in rest], True) if l == 0 else None,
                        functools.partial(layer_weights, l, rest))
        saved.append(sv)
        if nxt:
            got = dict(zip(behind_ssd + behind_attn, sv["hosted"] + sv["hosted_attn"]))
            layer_w.append(layer_weights(l + 1, BIG, [got[i] for i in range(len(BIG))]))
            h, hb = _ln_fwd(sv["h1"], sv["ffn"], layer_p[l]["ln_ffn_g"], layer_p[l]["ln_ffn_b"], f"lnffn{l}")

    def loss_fn(a_, r_, t_, g_, b_):
        y, vjp = jax.vjp(_layer_norm, ALPHA * a_ + r_, g_, b_)
        err = y - t_
        ds, dg, db = vjp(err * (1.0 / D_MODEL))
        part = 0.5 * jnp.sum(jnp.mean(err * err, axis=-1, keepdims=True), axis=0, keepdims=True)
        return (ds, ds), (dg, db, jnp.broadcast_to(part, (1, BLK)))
    sv = saved[-1]
    (ds_f, ds_b), (dg_last, db_last, loss_part) = _rows(
        loss_fn, [(sv["h1"], 0, D_MODEL), (sv["ffn"], 0, D_MODEL), (tgt, 0, D_MODEL)],
        [layer_p[-1]["ln_ffn_g"], layer_p[-1]["ln_ffn_b"]], [(D_MODEL, F32), (D_MODEL, BF)],
        [(1, D_MODEL), (1, D_MODEL), (1, BLK)], ROW_TILE, "loss")
    loss = lax.psum(loss_part[0, 0], ("x", "y", "c"))

    def reduce_cores(l, g, names):
        src = [split(n, g[n]) for n in names]
        recv = _exchange([s.astype(BF) for s in src], "c", False, f"reduce_cores{l}_{names[0]}")
        return {n: _sum_pieces(r, o, True, ci, f"sum_cores{l}_{n}") for n, r, o in zip(names, recv, src)}

    def bf16_of(sums, names):
        return [sums[n][1] for n in names]

    def add_chips(l, sums, from_chips):
        return [_sum_pieces(from_chips[n], sums[n][0], True, chip, f"sum_chips{l}_{n}")[0] for n in BIG]

    EARLY = tuple(n for n in BIG if n != "w_in")
    grads = [None] * DEPTH
    chip_sums = [None] * DEPTH
    for l in reversed(range(DEPTH)):
        sv = saved[l]
        if l == DEPTH - 1:
            lg, lb = dg_last, db_last
        else:
            ds_f, ds_b, lg, lb = _ln_bwd(sv["h1"], sv["ffn"], d1, d2, layer_p[l]["ln_ffn_g"], layer_p[l]["ln_ffn_b"], f"dlnffn{l}")
        sums = {}

        def early_hook(g, l=l, sums=sums):
            sums.update(reduce_cores(l, g, EARLY))
            return bf16_of(sums, EARLY)

        def late_hook(g, l=l, sums=sums):
            sums.update(reduce_cores(l, g, ("w_in",)))
            return bf16_of(sums, ("w_in",))
        if l == 0:
            g, d1, d2 = _layer_bwd(l, sv, layer_w[l], layer_p[l], ds_f, ds_b, bf16_of(chip_sums[l + 1], BIG), early_hook, late_hook)
        else:
            g, d1, d2 = _layer_bwd(l, sv, layer_w[l], layer_p[l], ds_f, ds_b)
            sums.update(reduce_cores(l, g, BIG))
        g["ln_ffn_g"], g["ln_ffn_b"] = lg, lb
        grads[l] = g
        chip_sums[l] = sums

    def ln_in_bwd(x_, d1_, d2_, g_, b_):
        _, vjp = jax.vjp(_layer_norm, x_, g_, b_)
        dx, dg, db = vjp(ALPHA * d1_ + d2_)
        return (dx,), (dg, db)
    (grad_x,), (g_ln_in_g, g_ln_in_b) = _rows(
        ln_in_bwd, [(xs_, 0, D_MODEL), (d1, 0, D_MODEL), (d2, 0, D_MODEL)], [ln_in_g[None], ln_in_b[None]],
        [(D_MODEL, F32)], [(1, D_MODEL), (1, D_MODEL)], ROW_TILE, "dln_in")

    carried = grads[0]["hosted"]
    from_chips = dict(zip(EARLY, carried[len(BIG):]))
    from_chips["w_in"], = grads[0]["late_hosted"]
    reduced = add_chips(0, chip_sums[0], from_chips) + add_chips(1, chip_sums[1], dict(zip(BIG, carried[:len(BIG)])))
    halves = _with_own(_exchange(reduced, "c", True, "swap_halves"), reduced, ci)
    shards = [t.reshape(2 * t.shape[1], t.shape[2]) for t in halves]
    big_grad = {n: jnp.stack([shards[l * len(BIG) + i] for l in range(DEPTH)]).reshape(wts[n].shape) for i, n in enumerate(BIG)}

    small_g = {"ln_in_g": g_ln_in_g[0], "ln_in_b": g_ln_in_b[0]}
    for n in SMALL[2:]:
        small_g[n] = jnp.stack([grads[l][n].reshape(wts[n].shape[1:] if n != "conv_w" else (SSD_CONV, CONV_DIM)) for l in range(DEPTH)])
    flat = jnp.concatenate([small_g[n].reshape(-1) for n in SMALL])
    n_small = flat.shape[0]
    width = -(-n_small // 1024) * 1024
    flat = jnp.pad(flat, (0, width - n_small)).reshape(8, width // 8)
    gathered, = _with_own(_exchange([flat], "all", True, "gather_small"), [flat], 2 * chip + ci)
    total = _sum_slots(gathered, "sum_small").reshape(-1)
    small_grad, off = {}, 0
    for n in SMALL:
        shp = small_g[n].shape
        cnt = math.prod(shp)
        small_grad[n] = total[off:off + cnt].reshape(shp)
        off += cnt
    small_grad["conv_w"] = lax.dynamic_slice_in_dim(small_grad["conv_w"], chip * (CONV_DIM // 4), CONV_DIM // 4, axis=2)

    out_g, out_d, out_m, out_v = {}, {}, {}, {}
    for n in BIG:
        shp = wts[n].shape
        two_d = lambda t: t.reshape(shp[0] * shp[1], shp[2])
        d_, m_, v_ = _adamw(two_d(wts[n]), two_d(big_grad[n]), two_d(mom1[n]), two_d(mom2[n]), f"adamw_{n}")
        out_g[n], out_d[n], out_m[n], out_v[n] = big_grad[n], d_.reshape(shp), m_.reshape(shp), v_.reshape(shp)

    def flat_small(d):
        f = jnp.concatenate([d[n].reshape(-1) for n in SMALL])
        return jnp.pad(f, (0, swidth - f.shape[0])).reshape(8, swidth // 8)
    n_sw = sum(math.prod(wts[n].shape) for n in SMALL)
    swidth = -(-n_sw // 1024) * 1024
    d_, m_, v_ = _adamw(flat_small(wts), flat_small(small_grad), flat_small(mom1), flat_small(mom2), "adamw_small")
    off = 0
    for n in SMALL:
        shp = wts[n].shape
        cnt = math.prod(shp)
        out_g[n] = small_grad[n]
        out_d[n] = d_.reshape(-1)[off:off + cnt].reshape(shp)
        out_m[n] = m_.reshape(-1)[off:off + cnt].reshape(shp)
        out_v[n] = v_.reshape(-1)[off:off + cnt].reshape(shp)
        off += cnt

    return (loss, grad_x[None], *[out_g[n] for n in WEIGHTS], *[out_d[n] for n in WEIGHTS],
            *[out_m[n] for n in WEIGHTS], *[out_v[n] for n in WEIGHTS])
```

```python
import functools
import itertools
import math

import jax
import jax.numpy as jnp
from jax import lax
from jax.experimental import pallas as pl
from jax.experimental.pallas import tpu as pltpu

F32 = jnp.float32
BF = jnp.bfloat16

D_MODEL = 1024
DEPTH = 2
ATT_HEADS = 16
ATT_HEAD_DIM = 64
BLK = 128
SSD_D_INNER = 2048
SSD_HEADS = 32
SSD_GROUPS = 4
SSD_STATE = 128
SSD_CONV = 4
BC_DIM = 512
CONV_DIM = 3072
FFN_HIDDEN = 2816
IN_DIM = 8480
LN_EPS = 1e-5
RMS_EPS = 1e-5
ALPHA = (2 * DEPTH) ** 0.25
ADAM_LR = 0.001
ADAM_B1 = 0.9
ADAM_B2 = 0.999
ADAM_EPS = 1e-08
ADAM_WD = 0.01
ADAM_STEP = 10

_PACK = (("gates", 6432, 2048), ("z", 1280, 2048), ("q", 0, 1024), ("xs", 3328, 2048), ("B", 5376, 512),
         ("C", 5888, 512), ("k", 1024, 128), ("v", 1152, 128), ("dt", 6400, 32))
NP = 8704
_OFF = {}
_o = 0
for _n, _, _w in _PACK:
    _OFF[_n] = _o
    _o += _w
_PAD = NP - _o

VMEM_LIMIT_BYTES = 56 * 1024 * 1024
ROW_TILE = 512
BIG = ("w_in", "w_ssd_out", "w_att_out", "w_mix_out", "w_ffn_gate", "w_ffn_up", "w_ffn_down")
WEIGHTS = ("ln_in_g", "ln_in_b", "w_in", "conv_w", "conv_b", "dt_bias", "a_log", "d_skip", "ssd_norm_w", "att_sinks",
           "w_ssd_out", "w_att_out", "w_mix_out", "ln_mix_g", "ln_mix_b", "w_ffn_gate", "w_ffn_up", "w_ffn_down",
           "ln_ffn_g", "ln_ffn_b")
SMALL = tuple(n for n in WEIGHTS if n not in BIG)


def _params(n_grid):
    return pltpu.CompilerParams(dimension_semantics=("arbitrary",) * n_grid, vmem_limit_bytes=VMEM_LIMIT_BYTES)


def _dot(a, b, ca, cb):
    return lax.dot_general(a.astype(BF), b.astype(BF), (((ca,), (cb,)), ((), ())), preferred_element_type=F32)


@jax.custom_vjp
def _nn(a, b):
    return _dot(a, b, 1, 0)


def _nn_f(a, b):
    return _dot(a, b, 1, 0), (a.astype(BF), b.astype(BF))


def _nn_b(res, g):
    a, b = res
    return _dot(g, b, 1, 1), _dot(a, g, 0, 0)


_nn.defvjp(_nn_f, _nn_b)


@jax.custom_vjp
def _nt(a, b):
    return _dot(a, b, 1, 1)


def _nt_f(a, b):
    return _dot(a, b, 1, 1), (a.astype(BF), b.astype(BF))


def _nt_b(res, g):
    a, b = res
    return _dot(g, b, 1, 0), _dot(g, a, 0, 0)


_nt.defvjp(_nt_f, _nt_b)


@jax.custom_vjp
def _tn(a, b):
    return _dot(a, b, 0, 0)


def _tn_f(a, b):
    return _dot(a, b, 0, 0), (a.astype(BF), b.astype(BF))


def _tn_b(res, g):
    a, b = res
    return _dot(b, g, 1, 1), _dot(a, g, 1, 0)


_tn.defvjp(_tn_f, _tn_b)


def _sigmoid(x):
    return 0.5 * jnp.tanh(0.5 * x) + 0.5


def _silu(x):
    return x * _sigmoid(x)


def _layer_norm(s, g, b):
    mu = jnp.mean(s, axis=-1, keepdims=True)
    sc = s - mu
    var = jnp.mean(sc * sc, axis=-1, keepdims=True)
    return sc * lax.rsqrt(var + LN_EPS) * g + b


def _ssd_group(pre_x, pre_b, pre_c, z, dtr, dtb, alog, dsk, nw, state):
    t = pre_x.shape[0]
    xs, bm, cm = _silu(pre_x), _silu(pre_b), _silu(pre_c)
    dt = jax.nn.softplus(dtr + dtb)
    da = dt * (-jnp.exp(alog))
    row = lax.broadcasted_iota(jnp.int32, (t, t), 0)
    col = lax.broadcasted_iota(jnp.int32, (t, t), 1)
    upper = (row <= col).astype(F32)
    cum = jnp.dot(da, upper, precision=lax.Precision.HIGHEST, preferred_element_type=F32)
    tot = jnp.sum(da, axis=1, keepdims=True)
    cb = _nt(cm, bm)
    tril = row >= col
    first = lax.broadcasted_iota(jnp.int32, (t, BLK), 1) < 64
    first_row = lax.broadcasted_iota(jnp.int32, (1, BLK), 1) < 64

    def col_form(r):
        return jnp.broadcast_to(r, (t, t)).T

    ys, new_state = [], []
    for p in range(4):
        h0, h1 = 2 * p, 2 * p + 1
        sl = slice(BLK * p, BLK * (p + 1))
        x_pair = xs[:, sl] * jnp.where(first, col_form(dt[h0:h0 + 1]), col_form(dt[h1:h1 + 1]))
        y_pair = None
        cc = []
        for h, keep in ((h0, first), (h1, jnp.logical_not(first))):
            cr = jnp.broadcast_to(cum[h:h + 1], (t, t))
            cc.append(cr.T)
            decay = jnp.exp(jnp.where(tril, cc[-1] - cr, -1e30))
            y_h = _nn(decay * cb, jnp.where(keep, x_pair, 0.0))
            y_pair = y_h if y_pair is None else y_pair + y_h
        s_pair = state[:, sl]
        y_pair = y_pair + _nn(cm, s_pair) * jnp.where(first, jnp.exp(cc[0]), jnp.exp(cc[1]))
        to_end = jnp.where(first, jnp.exp(tot[h0:h0 + 1] - cc[0]), jnp.exp(tot[h1:h1 + 1] - cc[1]))
        chunk_decay = jnp.where(first_row, jnp.exp(tot[h0:h0 + 1]), jnp.exp(tot[h1:h1 + 1]))
        new_state.append(s_pair * chunk_decay + _tn(bm, x_pair * to_end))
        ys.append(y_pair + dsk[:, sl] * xs[:, sl])
    y = jnp.concatenate(ys, axis=1) * _silu(z)
    y = y * lax.rsqrt(jnp.mean(y * y, axis=-1, keepdims=True) + RMS_EPS) * nw
    return y, jnp.concatenate(new_state, axis=1)


def _attn_block(q, kp, kc, vp, vc, sinks, n, kv):
    t = q.shape[0]
    kb = jnp.concatenate([kp, kc], axis=0)
    vb = jnp.concatenate([vp, vc], axis=0)
    qi = lax.broadcasted_iota(jnp.int32, (t, 2 * t), 0)
    kj = lax.broadcasted_iota(jnp.int32, (t, 2 * t), 1)
    rel = qi + t - kj
    valid = (rel >= 0) & (rel < t) & ((n * t - t + kj) >= 0)
    relf = rel.astype(F32)
    first = lax.broadcasted_iota(jnp.int32, (t, BLK), 1) < 64
    lane16 = lax.broadcasted_iota(jnp.int32, (1, ATT_HEADS), 1)
    outs = []
    for p in range(4):
        qp = q[:, BLK * p:BLK * (p + 1)] * (ATT_HEAD_DIM ** -0.5)
        o_pair = None
        for half, keep in enumerate((first, jnp.logical_not(first))):
            h = kv * 8 + 2 * p + half
            s = _nt(jnp.where(keep, qp, 0.0), kb)
            slope = jnp.exp((-8.0 * math.log(2.0) / ATT_HEADS) * jnp.asarray(h + 1, F32))
            s = jnp.where(valid, s - slope * relf, -1e30)
            sink = jnp.sum(jnp.where(lane16 == h, sinks, 0.0), axis=1, keepdims=True)
            m = lax.stop_gradient(jnp.maximum(jnp.max(s, axis=1, keepdims=True), sink))
            e = jnp.exp(s - m)
            den = jnp.sum(e, axis=1, keepdims=True) + jnp.exp(sink - m)
            o = _nn(e * (1.0 / den), vb)
            o_pair = o if o_pair is None else jnp.where(first, o_pair, o)
        outs.append(o_pair)
    return jnp.concatenate(outs, axis=1)


def _chip_exchange(src, out, send_sems, recv_sems, bcast):
    x, y, c = lax.axis_index("x"), lax.axis_index("y"), lax.axis_index("c")
    me = 2 * x + y

    def copy(a, d, slot):
        j = jnp.bitwise_xor(me, d)
        return pltpu.make_async_remote_copy(
            src_ref=src[a] if bcast else src[a].at[j], dst_ref=out[a].at[slot],
            send_sem=send_sems.at[a * 4 + d], recv_sem=recv_sems.at[a * 4 + d],
            device_id=(j // 2, j % 2, c), device_id_type=pl.DeviceIdType.MESH)

    pairs = [(a, d) for a in range(len(src)) for d in range(1, 4)]

    def start():
        for a, d in pairs:
            copy(a, d, me).start()

    def wait():
        for a, d in pairs:
            copy(a, d, jnp.bitwise_xor(me, d)).wait_recv()
        for a, d in pairs:
            copy(a, d, me).wait_send()

    return start, wait


def _mm(a, b, mode, out_dtype, tm, tn, tk, name, hosted=None):
    if mode == "nn":
        (m, k), (k2, n) = a.shape, b.shape
    elif mode == "nt":
        (m, k), (n, k2) = a.shape, b.shape
    else:
        (k, m), (k2, n) = a.shape, b.shape
    assert k == k2, (a.shape, b.shape, mode)
    tm, tn, tk = min(tm, m), min(tn, n), min(tk, k)
    assert m % tm == 0 and n % tn == 0 and k % tk == 0, (m, n, k, tm, tn, tk)
    nk = k // tk
    grid = (m // tm, n // tn, nk)
    ca, cb = {"nn": (1, 0), "nt": (1, 1), "tn": (0, 0)}[mode]
    h_arrays, h_bcast = hosted if hosted else ((), False)
    nh = len(h_arrays)

    def body(*refs):
        a_ref, b_ref = refs[:2]
        o_ref = refs[2 + nh]
        scratch = refs[3 + 2 * nh:]
        step = [pl.program_id(ax) for ax in range(3)]
        if nh:
            start, wait = _chip_exchange(refs[2:2 + nh], refs[3 + nh:3 + 2 * nh], scratch[-2], scratch[-1], h_bcast)

            @pl.when((step[0] == 0) & (step[1] == 0) & (step[2] == 0))
            def _():
                start()

        part = _dot(a_ref[...], b_ref[...], ca, cb)
        if nk == 1:
            o_ref[...] = part.astype(o_ref.dtype)
        else:
            acc_ref = scratch[0]

            @pl.when(step[2] == 0)
            def _():
                acc_ref[...] = part

            @pl.when(step[2] > 0)
            def _():
                acc_ref[...] += part

            @pl.when(step[2] == nk - 1)
            def _():
                o_ref[...] = acc_ref[...].astype(o_ref.dtype)

        if nh:
            @pl.when((step[0] == grid[0] - 1) & (step[1] == grid[1] - 1) & (step[2] == nk - 1))
            def _():
                wait()

    a_spec = pl.BlockSpec((tk, tm), lambda i, j, kk: (kk, i)) if mode == "tn" else pl.BlockSpec((tm, tk), lambda i, j, kk: (i, kk))
    b_spec = pl.BlockSpec((tn, tk), lambda i, j, kk: (j, kk)) if mode == "nt" else pl.BlockSpec((tk, tn), lambda i, j, kk: (kk, j))
    any_spec = pl.BlockSpec(memory_space=pl.ANY)
    res = pl.pallas_call(
        body, name=name, grid=grid,
        in_specs=[a_spec, b_spec] + [any_spec] * nh,
        out_specs=[pl.BlockSpec((tm, tn), lambda i, j, kk: (i, j))] + [any_spec] * nh,
        out_shape=[jax.ShapeDtypeStruct((m, n), out_dtype)]
        + [jax.ShapeDtypeStruct(((4,) + s.shape) if h_bcast else s.shape, s.dtype) for s in h_arrays],
        scratch_shapes=([] if nk == 1 else [pltpu.VMEM((tm, tn), F32)])
        + ([pltpu.SemaphoreType.DMA((4 * nh,)), pltpu.SemaphoreType.DMA((4 * nh,))] if nh else []),
        compiler_params=_params(3),
    )(a, b, *h_arrays)
    return (res[0], list(res[1:])) if nh else res[0]


def _rows(fn, rows, params, outs, accs, tile, name):
    length = rows[0][0].shape[0]
    tile = min(tile, length)
    assert length % tile == 0
    nr, npar, no = len(rows), len(params), len(outs)

    def body(*refs):
        vals = [r[...] for r in refs[:nr + npar]]
        o, a = fn(*vals)
        for ref, val in zip(refs[nr + npar:nr + npar + no], o):
            ref[...] = val.astype(ref.dtype)
        i = pl.program_id(0)
        for ref, val in zip(refs[nr + npar + no:], a):
            @pl.when(i == 0)
            def _(ref=ref, val=val):
                ref[...] = val

            @pl.when(i > 0)
            def _(ref=ref, val=val):
                ref[...] += val

    in_specs = [pl.BlockSpec((tile, w), functools.partial(lambda i, cb: (i, cb), cb=cb)) for _, cb, w in rows]
    in_specs += [pl.BlockSpec(p.shape, lambda i: (0, 0)) for p in params]
    outs = [o if len(o) == 3 else (o[0], o[1], o[0]) for o in outs]
    out_specs = [pl.BlockSpec((tile, w), lambda i: (i, 0)) for w, _, _ in outs]
    out_specs += [pl.BlockSpec((r, w), lambda i: (0, 0)) for r, w in accs]
    out_shape = [jax.ShapeDtypeStruct((length, full), dt) for _, dt, full in outs]
    out_shape += [jax.ShapeDtypeStruct((r, w), F32) for r, w in accs]
    res = pl.pallas_call(
        body, name=name, grid=(length // tile,), in_specs=in_specs, out_specs=out_specs, out_shape=out_shape,
        compiler_params=_params(1),
    )(*[r[0] for r in rows], *params)
    return res[:no], res[no:]


CONV_SUB = 32


def _conv_fwd(proj, conv_w, conv_b, name, hosted=None):
    length = proj.shape[0]
    tl, cw = min(1024, length), 512
    cb0 = _OFF["xs"] // cw
    grid = (CONV_DIM // cw, length // tl)
    h_arrays, h_bcast = hosted if hosted else ((), False)
    nh = len(h_arrays)

    def body(*refs):
        u_ref, halo_ref, w_ref, b_ref = refs[:4]
        o_ref = refs[4 + nh]
        win = refs[5 + 2 * nh]
        j, i = pl.program_id(0), pl.program_id(1)
        if nh:
            start, wait = _chip_exchange(refs[4:4 + nh], refs[5 + nh:5 + 2 * nh], refs[-2], refs[-1], h_bcast)

            @pl.when((j == 0) & (i == 0))
            def _():
                start()

            @pl.when((j == grid[0] - 1) & (i == grid[1] - 1))
            def _():
                wait()

        win[0:8, :] = jnp.where(i > 0, halo_ref[...], 0.0)
        win[8:8 + tl, :] = u_ref[...]
        taps = [w_ref[kk:kk + 1, :] for kk in range(SSD_CONV)]
        bias = b_ref[...]
        for r in range(0, tl, CONV_SUB):
            acc = bias + taps[0] * win[5 + r:5 + r + CONV_SUB, :]
            for kk in range(1, SSD_CONV):
                acc = acc + taps[kk] * win[5 + kk + r:5 + kk + r + CONV_SUB, :]
            o_ref[r:r + CONV_SUB, :] = acc

    any_spec = pl.BlockSpec(memory_space=pl.ANY)
    res = pl.pallas_call(
        body, name=name, grid=grid,
        in_specs=[pl.BlockSpec((tl, cw), lambda j, i: (i, cb0 + j)),
                  pl.BlockSpec((8, cw), lambda j, i: (jnp.maximum(i * (tl // 8) - 1, 0), cb0 + j)),
                  pl.BlockSpec((SSD_CONV, cw), lambda j, i: (0, j)),
                  pl.BlockSpec((1, cw), lambda j, i: (0, j))] + [any_spec] * nh,
        out_specs=[pl.BlockSpec((tl, cw), lambda j, i: (i, j))] + [any_spec] * nh,
        out_shape=[jax.ShapeDtypeStruct((length, CONV_DIM), F32)]
        + [jax.ShapeDtypeStruct(((4,) + s.shape) if h_bcast else s.shape, s.dtype) for s in h_arrays],
        scratch_shapes=[pltpu.VMEM((8 + tl, cw), F32)]
        + ([pltpu.SemaphoreType.DMA((4 * nh,)), pltpu.SemaphoreType.DMA((4 * nh,))] if nh else []),
        compiler_params=_params(2),
    )(proj, proj, conv_w, conv_b, *h_arrays)
    return (res[0], list(res[1:])) if nh else res[0]


def _conv_bwd(dpre, proj, conv_w, dproj, part, name):
    length = proj.shape[0]
    width = dpre.shape[1]
    tl, cw = min(1024, length), 512
    cb0 = _OFF["xs"] // cw + part
    n_t = length // tl

    def body(d_ref, dnext_ref, u_ref, halo_ref, w_ref, _, du_ref, dw_ref, db_ref, dwin, uwin):
        i = pl.program_id(1)
        dwin[0:tl, :] = d_ref[...].astype(F32)
        dwin[tl:tl + 8, :] = jnp.where(i < n_t - 1, dnext_ref[...].astype(F32)[0:8], 0.0)
        uwin[0:8, :] = jnp.where(i > 0, halo_ref[...], 0.0)
        uwin[8:8 + tl, :] = u_ref[...]
        taps = [w_ref[kk:kk + 1, :] for kk in range(SSD_CONV)]
        sub = CONV_SUB // 2
        acc_w = [jnp.zeros((sub, cw), F32) for _ in range(SSD_CONV)]
        acc_b = jnp.zeros((sub, cw), F32)
        for r in range(0, tl, sub):
            d = dwin[r:r + sub, :]
            du = taps[3] * d
            for kk in range(SSD_CONV - 1):
                du = du + taps[kk] * dwin[3 - kk + r:3 - kk + r + sub, :]
            du_ref[r:r + sub, :] = du.astype(du_ref.dtype)
            for kk in range(SSD_CONV):
                acc_w[kk] = acc_w[kk] + d * uwin[5 + kk + r:5 + kk + r + sub, :]
            acc_b = acc_b + d
        dw = jnp.concatenate([jnp.sum(a, axis=0, keepdims=True) for a in acc_w], axis=0)
        db = jnp.sum(acc_b, axis=0, keepdims=True)

        @pl.when(i == 0)
        def _():
            dw_ref[...] = dw
            db_ref[...] = db

        @pl.when(i > 0)
        def _():
            dw_ref[...] += dw
            db_ref[...] += db

    return pl.pallas_call(
        body, name=name, grid=(width // cw, n_t), input_output_aliases={5: 0},
        in_specs=[pl.BlockSpec((tl, cw), lambda j, i: (i, j)),
                  pl.BlockSpec((16, cw), lambda j, i: (jnp.minimum((i + 1) * (tl // 16), length // 16 - 1), j)),
                  pl.BlockSpec((tl, cw), lambda j, i: (i, cb0 + j)),
                  pl.BlockSpec((8, cw), lambda j, i: (jnp.maximum(i * (tl // 8) - 1, 0), cb0 + j)),
                  pl.BlockSpec((SSD_CONV, cw), lambda j, i: (0, part + j)),
                  pl.BlockSpec(memory_space=pl.ANY)],
        out_specs=[pl.BlockSpec((tl, cw), lambda j, i: (i, cb0 + j)),
                   pl.BlockSpec((SSD_CONV, cw), lambda j, i: (0, j)),
                   pl.BlockSpec((1, cw), lambda j, i: (0, j))],
        out_shape=[jax.ShapeDtypeStruct((length, NP), BF),
                   jax.ShapeDtypeStruct((SSD_CONV, width), F32),
                   jax.ShapeDtypeStruct((1, width), F32)],
        scratch_shapes=[pltpu.VMEM((tl + 8, cw), F32), pltpu.VMEM((8 + tl, cw), F32)],
        compiler_params=_params(2),
    )(dpre, dpre, proj, proj, conv_w, dproj)


def _ssd_in_specs(rev, nc):
    def cidx(c):
        return nc - 1 - c if rev else c
    whole = lambda c: (0, 0)
    return [pl.BlockSpec((BLK, SSD_D_INNER), lambda c: (cidx(c), 0)),
            pl.BlockSpec((BLK, BC_DIM), lambda c: (cidx(c), SSD_D_INNER // BC_DIM)),
            pl.BlockSpec((BLK, BC_DIM), lambda c: (cidx(c), SSD_D_INNER // BC_DIM + 1)),
            pl.BlockSpec((BLK, SSD_D_INNER), lambda c: (cidx(c), _OFF["z"] // SSD_D_INNER)),
            pl.BlockSpec((SSD_HEADS, BLK), lambda c: (0, cidx(c))),
            pl.BlockSpec((SSD_HEADS, 1), whole),
            pl.BlockSpec((SSD_HEADS, 1), whole),
            pl.BlockSpec((1, SSD_D_INNER), whole),
            pl.BlockSpec((1, SSD_D_INNER), whole)]


def _group_args(g, px, pb, pc, z, dtr, dtb, alog, dsk, nw):
    wide, narrow, heads = slice(512 * g, 512 * (g + 1)), slice(BLK * g, BLK * (g + 1)), slice(8 * g, 8 * (g + 1))
    return (px[:, wide], pb[:, narrow], pc[:, narrow], z[:, wide], dtr[heads, :], dtb[heads, :], alog[heads, :],
            dsk[:, wide], nw[:, wide])


def _ssd_fwd(pre, proj, dt_t, dtb, alog, dsk, nw, name, hosted=None):
    length = pre.shape[0]
    nc = length // BLK
    h_arrays, h_bcast = hosted if hosted else ((), False)
    nh = len(h_arrays)

    def body(*refs):
        px, pb, pc, z, dtr, dtb_r, al_r, dsk_r, nw_r = refs[:9]
        y_ref, sin_ref = refs[9 + nh:11 + nh]
        st = refs[11 + 2 * nh]
        c = pl.program_id(0)
        if nh:
            start, wait = _chip_exchange(refs[9:9 + nh], refs[11 + nh:11 + 2 * nh], refs[-2], refs[-1], h_bcast)

        @pl.when(c == 0)
        def _():
            st[...] = jnp.zeros_like(st)
            if nh:
                start()

        for g in range(SSD_GROUPS):
            s_in = st[g]
            sin_ref[g] = s_in
            y, s_out = _ssd_group(*_group_args(g, px, pb, pc, z, dtr, dtb_r, al_r, dsk_r, nw_r), s_in)
            y_ref[:, 512 * g:512 * (g + 1)] = y.astype(y_ref.dtype)
            st[g] = s_out

        if nh:
            @pl.when(c == nc - 1)
            def _():
                wait()

    any_spec = pl.BlockSpec(memory_space=pl.ANY)
    res = pl.pallas_call(
        body, name=name, grid=(nc,), in_specs=_ssd_in_specs(False, nc) + [any_spec] * nh,
        out_specs=[pl.BlockSpec((BLK, SSD_D_INNER), lambda c: (c, 0)),
                   pl.BlockSpec((SSD_GROUPS, None, SSD_STATE, 512), lambda c: (0, c, 0, 0))] + [any_spec] * nh,
        out_shape=[jax.ShapeDtypeStruct((length, SSD_D_INNER), BF),
                   jax.ShapeDtypeStruct((SSD_GROUPS, nc, SSD_STATE, 512), F32)]
        + [jax.ShapeDtypeStruct(((4,) + s.shape) if h_bcast else s.shape, s.dtype) for s in h_arrays],
        scratch_shapes=[pltpu.VMEM((SSD_GROUPS, SSD_STATE, 512), F32)]
        + ([pltpu.SemaphoreType.DMA((4 * nh,)), pltpu.SemaphoreType.DMA((4 * nh,))] if nh else []),
        compiler_params=_params(1),
    )(pre, pre, pre, proj, dt_t, dtb, alog, dsk, nw, *h_arrays)
    return (res[0], res[1], list(res[2:])) if nh else (res[0], res[1])


def _ssd_bwd(pre, proj, dt_t, dtb, alog, dsk, nw, s_in, dy, dproj, name, hosted=None):
    length = pre.shape[0]
    nc = length // BLK
    h_arrays, h_bcast = hosted if hosted else ((), False)
    nh = len(h_arrays)

    def body(*refs):
        px, pb, pc, z, dtr, dtb_r, al_r, dsk_r, nw_r, sin_r, dy_r = refs[:11]
        dpre, dz, ddt, ddtb, dal, ddsk, dnw = refs[12 + nh:19 + nh]
        dst = refs[19 + 2 * nh]
        c = pl.program_id(0)
        if nh:
            start, wait = _chip_exchange(refs[12:12 + nh], refs[19 + nh:19 + 2 * nh], refs[-2], refs[-1], h_bcast)

        @pl.when(c == 0)
        def _():
            dst[...] = jnp.zeros_like(dst)
            if nh:
                start()

        for grp in range(SSD_GROUPS):
            wide, heads = slice(512 * grp, 512 * (grp + 1)), slice(8 * grp, 8 * (grp + 1))
            _, vjp = jax.vjp(_ssd_group, *_group_args(grp, px, pb, pc, z, dtr, dtb_r, al_r, dsk_r, nw_r), sin_r[grp])
            g = vjp((dy_r[:, wide], dst[grp]))
            dpre[:, wide] = g[0].astype(dpre.dtype)
            dpre[:, slice(SSD_D_INNER + BLK * grp, SSD_D_INNER + BLK * (grp + 1))] = g[1].astype(dpre.dtype)
            dpre[:, slice(SSD_D_INNER + BC_DIM + BLK * grp, SSD_D_INNER + BC_DIM + BLK * (grp + 1))] = g[2].astype(dpre.dtype)
            dz[:, wide] = g[3].astype(dz.dtype)
            ddt[heads, :] = g[4]
            dst[grp] = g[9]
            for ref, val, idx in ((ddtb, g[5], (heads, slice(None))), (dal, g[6], (heads, slice(None))),
                                  (ddsk, g[7], (slice(None), wide)), (dnw, g[8], (slice(None), wide))):
                @pl.when(c == 0)
                def _(ref=ref, val=val, idx=idx):
                    ref[idx] = val

                @pl.when(c > 0)
                def _(ref=ref, val=val, idx=idx):
                    ref[idx] += val

        if nh:
            @pl.when(c == nc - 1)
            def _():
                wait()

    rev = lambda c: nc - 1 - c
    whole = lambda c: (0, 0)
    any_spec = pl.BlockSpec(memory_space=pl.ANY)
    in_specs = _ssd_in_specs(True, nc) + [
        pl.BlockSpec((SSD_GROUPS, None, SSD_STATE, 512), lambda c: (0, rev(c), 0, 0)),
        pl.BlockSpec((BLK, SSD_D_INNER), lambda c: (rev(c), 0)),
        any_spec] + [any_spec] * nh
    res = pl.pallas_call(
        body, name=name, grid=(nc,), in_specs=in_specs, input_output_aliases={11: 1},
        out_specs=[pl.BlockSpec((BLK, CONV_DIM), lambda c: (rev(c), 0)),
                   pl.BlockSpec((BLK, SSD_D_INNER), lambda c: (rev(c), _OFF["z"] // SSD_D_INNER)),
                   pl.BlockSpec((SSD_HEADS, BLK), lambda c: (0, rev(c))),
                   pl.BlockSpec((SSD_HEADS, 1), whole),
                   pl.BlockSpec((SSD_HEADS, 1), whole),
                   pl.BlockSpec((1, SSD_D_INNER), whole),
                   pl.BlockSpec((1, SSD_D_INNER), whole)] + [any_spec] * nh,
        out_shape=[jax.ShapeDtypeStruct((length, CONV_DIM), BF),
                   jax.ShapeDtypeStruct((length, NP), BF),
                   jax.ShapeDtypeStruct((SSD_HEADS, length), F32),
                   jax.ShapeDtypeStruct((SSD_HEADS, 1), F32),
                   jax.ShapeDtypeStruct((SSD_HEADS, 1), F32),
                   jax.ShapeDtypeStruct((1, SSD_D_INNER), F32),
                   jax.ShapeDtypeStruct((1, SSD_D_INNER), F32)]
        + [jax.ShapeDtypeStruct(((4,) + s.shape) if h_bcast else s.shape, s.dtype) for s in h_arrays],
        scratch_shapes=[pltpu.VMEM((SSD_GROUPS, SSD_STATE, 512), F32)]
        + ([pltpu.SemaphoreType.DMA((4 * nh,)), pltpu.SemaphoreType.DMA((4 * nh,))] if nh else []),
        compiler_params=_params(1),
    )(pre, pre, pre, proj, dt_t, dtb, alog, dsk, nw, s_in, dy, dproj, *h_arrays)
    return tuple(res[:7]) + ((list(res[7:]),) if nh else ())


assert _OFF["gates"] == 0 and _OFF["z"] % SSD_D_INNER == 0 and _OFF["q"] % D_MODEL == 0 and _OFF["k"] + 512 == NP


def _attn_in_specs():
    prev = lambda n: (0, jnp.maximum(n - 1, 0), 0)
    cur = lambda n: (0, n, 0)
    return [pl.BlockSpec((BLK, D_MODEL), lambda n: (n, _OFF["q"] // D_MODEL)),
            pl.BlockSpec((2, BLK, BLK), prev),
            pl.BlockSpec((2, BLK, BLK), cur),
            pl.BlockSpec((2, BLK, BLK), prev),
            pl.BlockSpec((2, BLK, BLK), cur),
            pl.BlockSpec((1, ATT_HEADS), lambda n: (0, 0))]


def _attn_fwd(proj, k2, v2, sinks, name, hosted=None):
    length = proj.shape[0]
    nb = length // BLK
    h_arrays, h_bcast = hosted if hosted else ((), False)
    nh = len(h_arrays)

    def body(*refs):
        q, kp, kc, vp, vc, sk = refs[:6]
        o_ref = refs[6 + nh]
        n = pl.program_id(0)
        if nh:
            start, wait = _chip_exchange(refs[6:6 + nh], refs[7 + nh:7 + 2 * nh], refs[-2], refs[-1], h_bcast)

            @pl.when(n == 0)
            def _():
                start()

        for kv in range(2):
            cols = slice(512 * kv, 512 * (kv + 1))
            o = _attn_block(q[:, cols], kp[kv], kc[kv], vp[kv], vc[kv], sk[...], n, kv)
            o_ref[:, cols] = o.astype(o_ref.dtype)

        if nh:
            @pl.when(n == nb - 1)
            def _():
                wait()

    any_spec = pl.BlockSpec(memory_space=pl.ANY)
    res = pl.pallas_call(
        body, name=name, grid=(nb,), in_specs=_attn_in_specs() + [any_spec] * nh,
        out_specs=[pl.BlockSpec((BLK, D_MODEL), lambda n: (n, 0))] + [any_spec] * nh,
        out_shape=[jax.ShapeDtypeStruct((length, D_MODEL), BF)]
        + [jax.ShapeDtypeStruct(((4,) + s.shape) if h_bcast else s.shape, s.dtype) for s in h_arrays],
        scratch_shapes=[pltpu.SemaphoreType.DMA((4 * nh,)), pltpu.SemaphoreType.DMA((4 * nh,))] if nh else [],
        compiler_params=_params(1),
    )(proj, k2, k2, v2, v2, sinks, *h_arrays)
    return (res[0], list(res[1:])) if nh else res[0]


def _attn_bwd(proj, k2, v2, sinks, datt, dproj, name):
    length = proj.shape[0]

    def body(q, kp, kc, vp, vc, sk, do, _, dq, dkp, dkc, dvp, dvc, dsk):
        n = pl.program_id(0)
        dsinks = None
        for kv in range(2):
            cols = slice(512 * kv, 512 * (kv + 1))
            _, vjp = jax.vjp(lambda *a: _attn_block(*a, n, kv), q[:, cols], kp[kv], kc[kv], vp[kv], vc[kv], sk[...])
            g = vjp(do[:, cols].astype(F32))
            dq[:, cols] = g[0].astype(dq.dtype)
            dkp[kv] = g[1]
            dkc[kv] = g[2]
            dvp[kv] = g[3]
            dvc[kv] = g[4]
            dsinks = g[5] if dsinks is None else dsinks + g[5]

        @pl.when(n == 0)
        def _():
            dsk[...] = dsinks

        @pl.when(n > 0)
        def _():
            dsk[...] += dsinks

    blk3 = pl.BlockSpec((2, BLK, BLK), lambda n: (0, n, 0))
    kv_shape = jax.ShapeDtypeStruct((2, length, BLK), F32)
    return pl.pallas_call(
        body, name=name, grid=(length // BLK,),
        in_specs=_attn_in_specs() + [pl.BlockSpec((BLK, D_MODEL), lambda n: (n, 0)), pl.BlockSpec(memory_space=pl.ANY)],
        out_specs=[pl.BlockSpec((BLK, D_MODEL), lambda n: (n, _OFF["q"] // D_MODEL)), blk3, blk3, blk3, blk3,
                   pl.BlockSpec((1, ATT_HEADS), lambda n: (0, 0))],
        out_shape=[jax.ShapeDtypeStruct((length, NP), BF), kv_shape, kv_shape, kv_shape, kv_shape,
                   jax.ShapeDtypeStruct((1, ATT_HEADS), F32)],
        input_output_aliases={7: 0},
        compiler_params=_params(1),
    )(proj, k2, k2, v2, v2, sinks, datt, dproj)


DMA_CHUNK_BYTES = 1 << 20
N_STAGE = 12
LOOKAHEAD = 5


def _piece_chunks(shape, itemsize):
    if len(shape) < 2 or shape[-2] % 16 != 0:
        return [()]
    rows, cols = shape[-2:]
    step = min(rows, max(16, DMA_CHUNK_BYTES // (cols * itemsize) // 16 * 16))
    out = []
    for lead in itertools.product(*[range(d) for d in shape[:-2]]):
        for r0 in range(0, rows, step):
            out.append(lead + (pl.ds(r0, min(step, rows - r0)),))
    return out


def _exchange(srcs, group, bcast, name):
    size = {"c": 2, "xy": 4, "all": 8}[group]
    npeer = size - 1
    n = len(srcs)
    for s in srcs:
        assert bcast or s.shape[0] == size
    pieces = [s.shape if bcast else s.shape[1:] for s in srcs]
    chunks = [_piece_chunks(p, s.dtype.itemsize) for p, s in zip(pieces, srcs)]

    def body(*refs):
        src, out = refs[:n], refs[n:2 * n]
        load_sems, send_sems, recv_sems = refs[2 * n:]
        x, y, c = lax.axis_index("x"), lax.axis_index("y"), lax.axis_index("c")
        if group == "c":
            me = c
        elif group == "xy":
            me = 2 * x + y
        else:
            me = 4 * x + 2 * y + c

        def device(j):
            if group == "c":
                return (x, y, j)
            if group == "xy":
                return (j // 2, j % 2, c)
            return (j // 4, (j // 2) % 2, j % 2)

        def part(ref, idx):
            return ref.at[idx] if idx else ref

        def piece(a, j):
            return src[a] if bcast else src[a].at[j]

        for a in range(n):
            jobs = [(None, idx) for idx in chunks[a]] if bcast else [(d, idx) for idx in chunks[a] for d in range(1, size)]
            full_rows = chunks[a][0][-1].size if chunks[a][0] else None
            slot_shape = (full_rows, pieces[a][-1]) if chunks[a][0] else pieces[a]

            def stream(stage, a=a, jobs=jobs, full_rows=full_rows):
                def slot(q):
                    idx = jobs[q][1]
                    view = stage.at[q % N_STAGE]
                    if idx and idx[-1].size != full_rows:
                        view = view.at[pl.ds(0, idx[-1].size)]
                    return view

                def load(q):
                    d, idx = jobs[q]
                    j = me if d is None else jnp.bitwise_xor(me, d)
                    return pltpu.make_async_copy(part(piece(a, j), idx), slot(q), load_sems.at[q % N_STAGE])

                def sends(q):
                    d, idx = jobs[q]
                    return [pltpu.make_async_remote_copy(
                        src_ref=slot(q), dst_ref=part(out[a].at[me], idx),
                        send_sem=send_sems.at[(q % N_STAGE) * npeer + dd - 1], recv_sem=recv_sems.at[a * size + dd],
                        device_id=device(jnp.bitwise_xor(me, dd)), device_id_type=pl.DeviceIdType.MESH)
                        for dd in (range(1, size) if d is None else (d,))]

                nq = len(jobs)
                for q in range(nq + LOOKAHEAD):
                    if q < nq:
                        if q >= N_STAGE:
                            for cp in sends(q - N_STAGE):
                                cp.wait_send()
                        load(q).start()
                    if q >= LOOKAHEAD:
                        load(q - LOOKAHEAD).wait()
                        for cp in sends(q - LOOKAHEAD):
                            cp.start()
                for q in range(max(0, nq - N_STAGE), nq):
                    for cp in sends(q):
                        cp.wait_send()

            pl.run_scoped(stream, pltpu.VMEM((N_STAGE,) + tuple(slot_shape), srcs[a].dtype))

        for a in range(n):
            for d in range(1, size):
                j = jnp.bitwise_xor(me, d)
                pltpu.make_async_remote_copy(
                    src_ref=piece(a, j), dst_ref=out[a].at[j], send_sem=send_sems.at[0], recv_sem=recv_sems.at[a * size + d],
                    device_id=device(j), device_id_type=pl.DeviceIdType.MESH).wait_recv()

    any_spec = pl.BlockSpec(memory_space=pl.ANY)
    return pl.pallas_call(
        body, name=name, in_specs=[any_spec] * n, out_specs=[any_spec] * n,
        out_shape=[jax.ShapeDtypeStruct((size,) + tuple(p), s.dtype) for p, s in zip(pieces, srcs)],
        scratch_shapes=[pltpu.SemaphoreType.DMA((N_STAGE,)), pltpu.SemaphoreType.DMA((N_STAGE * npeer,)),
                        pltpu.SemaphoreType.DMA((n * size,))],
        compiler_params=pltpu.CompilerParams(vmem_limit_bytes=VMEM_LIMIT_BYTES),
    )(*srcs)


def _with_own(outs, owns, me):
    return [lax.dynamic_update_index_in_dim(o, w, me, 0) for o, w in zip(outs, owns)]


def _sum_slots(arr, name):
    k = arr.shape[0]
    rest = arr.shape[1:]
    width = rest[-1]
    rows_per = math.prod(rest[:-1])
    a2 = arr.reshape(k * rows_per, width)
    tile = rows_per
    for cand in (256, 128, 64, 32, 16, 8):
        if rows_per % cand == 0:
            tile = cand
            break
    nt = rows_per // tile

    def body(*refs):
        acc = refs[0][...]
        for r in refs[1:k]:
            acc = acc + r[...]
        refs[k][...] = acc

    in_specs = [pl.BlockSpec((tile, width), functools.partial(lambda i, s: (s * nt + i, 0), s=s)) for s in range(k)]
    out = pl.pallas_call(
        body, name=name, grid=(nt,), in_specs=in_specs, out_specs=pl.BlockSpec((tile, width), lambda i: (i, 0)),
        out_shape=jax.ShapeDtypeStruct((rows_per, width), arr.dtype), compiler_params=_params(1),
    )(*([a2] * k))
    return out.reshape(rest)


def _sum_pieces(recv, own, own_slotted, me, name):
    k = recv.shape[0]
    rest = recv.shape[1:]
    width = rest[-1]
    rows_per = math.prod(rest[:-1])
    r2 = recv.reshape(k * rows_per, width)
    o2 = own.reshape(-1, width)
    tile = rows_per
    for cand in (256, 128, 64, 32, 16, 8):
        if rows_per % cand == 0:
            tile = cand
            break
    nt = rows_per // tile

    def body(me_ref, *refs):
        mine = refs[k][...].astype(F32)
        acc = None
        for j in range(k):
            term = jnp.where(me_ref[0] == j, mine, refs[j][...].astype(F32))
            acc = term if acc is None else acc + term
        refs[k + 1][...] = acc
        refs[k + 2][...] = acc.astype(BF)

    in_specs = [pl.BlockSpec((tile, width), functools.partial(lambda i, m, s: (s * nt + i, 0), s=s)) for s in range(k)]
    in_specs.append(pl.BlockSpec((tile, width), (lambda i, m: (m[0] * nt + i, 0)) if own_slotted else (lambda i, m: (i, 0))))
    out_spec = pl.BlockSpec((tile, width), lambda i, m: (i, 0))
    out, out_bf = pl.pallas_call(
        body, name=name,
        grid_spec=pltpu.PrefetchScalarGridSpec(num_scalar_prefetch=1, grid=(nt,), in_specs=in_specs, out_specs=[out_spec, out_spec]),
        out_shape=[jax.ShapeDtypeStruct((rows_per, width), F32), jax.ShapeDtypeStruct((rows_per, width), BF)],
        compiler_params=_params(1),
    )(jnp.reshape(me, (1,)).astype(jnp.int32), *([r2] * k), o2)
    return out.reshape(rest), out_bf.reshape(rest)


def _adamw(w, g, m, v, name):
    def fn(w_, g_, m_, v_):
        m1 = ADAM_B1 * m_ + (1.0 - ADAM_B1) * g_
        v1 = ADAM_B2 * v_ + (1.0 - ADAM_B2) * (g_ * g_)
        m_hat = m1 / (1.0 - ADAM_B1 ** ADAM_STEP)
        v_hat = v1 / (1.0 - ADAM_B2 ** ADAM_STEP)
        delta = -ADAM_LR * (m_hat / (jnp.sqrt(v_hat) + ADAM_EPS) + ADAM_WD * w_)
        return (delta, m1, v1), ()

    rows, width = w.shape
    tile = rows
    for cand in (256, 128, 64, 32, 16, 8):
        if rows % cand == 0:
            tile = cand
            break
    o, _ = _rows(fn, [(w, 0, width), (g, 0, width), (m, 0, width), (v, 0, width)], [],
                 [(width, F32)] * 3, [], tile, name)
    return o


def _col_pieces(start, width, shard):
    out = []
    while width:
        chip, off = divmod(start, shard)
        take = min(width, shard - off)
        out.append((chip, off, take))
        start, width = start + take, width - take
    return out


def _gather_cols(parts, ranges, pad=0):
    shard = parts.shape[3]
    halves = []
    for h in range(2):
        cols = [parts[h, chip, :, off:off + w] for s, wd in ranges for chip, off, w in _col_pieces(s, wd, shard)]
        if pad:
            cols.append(jnp.zeros((parts.shape[2], pad), parts.dtype))
        halves.append(jnp.concatenate(cols, axis=1))
    return jnp.concatenate(halves, axis=0)


def _scatter_cols(g, ranges, shard):
    starts, pos = {}, 0
    for s, wd in ranges:
        starts[s] = (pos, wd)
        pos += wd
    order = sorted(starts)
    hr = g.shape[0] // 2
    out = []
    for h in range(2):
        per_chip = []
        for chip in range(4):
            cols = []
            for s in order:
                p0, wd = starts[s]
                lo, hi = max(s, chip * shard), min(s + wd, (chip + 1) * shard)
                if lo < hi:
                    cols.append(g[h * hr:(h + 1) * hr, p0 + lo - s:p0 + hi - s])
            per_chip.append(jnp.concatenate(cols, axis=1))
        out.append(jnp.stack(per_chip))
    return jnp.stack(out)


_IN_RANGES = tuple((o, wd) for _, o, wd in _PACK)


def _double_heads(t):
    length = t.shape[0]
    h = jnp.transpose(t.reshape(length, 2, 64), (1, 0, 2))
    return jnp.concatenate([h, h], axis=-1)


def _fold_heads(d_cur, d_prev):
    length = d_cur.shape[1]
    d = d_cur + jnp.concatenate([d_prev[:, BLK:], jnp.zeros((2, BLK, BLK), F32)], axis=1)
    d = d[..., :64] + d[..., 64:]
    return jnp.transpose(d, (1, 0, 2)).reshape(length, 128)


def _ln_fwd(a, r, g, b, name):
    def fn(a_, r_, g_, b_):
        y = _layer_norm(ALPHA * a_ + r_, g_, b_)
        return (y, y), ()
    o, _ = _rows(fn, [(a, 0, D_MODEL), (r, 0, D_MODEL)], [g, b], [(D_MODEL, F32), (D_MODEL, BF)], [], 2 * ROW_TILE, name)
    return o


def _ln_bwd(a, r, d1, d2, g, b, name):
    def fn(a_, r_, d1_, d2_, g_, b_):
        _, vjp = jax.vjp(_layer_norm, ALPHA * a_ + r_, g_, b_)
        ds, dg, db = vjp(ALPHA * d1_ + d2_)
        return (ds, ds), (dg, db)
    o, acc = _rows(fn, [(a, 0, D_MODEL), (r, 0, D_MODEL), (d1, 0, D_MODEL), (d2, 0, D_MODEL)], [g, b],
                   [(D_MODEL, F32), (D_MODEL, BF)], [(1, D_MODEL), (1, D_MODEL)], ROW_TILE, name)
    return o[0], o[1], acc[0], acc[1]


def _layer_fwd(l, h, hb, w, p, hosted=None, hosted_attn=None, hosted_conv=None, late_weights=None):
    sv = {"h": h, "hb": hb}
    proj = _mm(hb, w["win"], "nn", F32, 4096, 512, 1024, f"proj{l}")
    sv["proj"] = proj
    pre = _conv_fwd(proj, p["conv_w"], p["conv_b"], f"conv{l}", hosted_conv)
    if hosted_conv:
        pre, got = pre
        w.update(late_weights(got))
    sv["pre"] = pre
    dt_t = jnp.transpose(proj[:, _OFF["dt"]:_OFF["dt"] + SSD_HEADS])
    sv["dt_t"] = dt_t
    res = _ssd_fwd(pre, proj, dt_t, p["dt_bias"], p["a_log"], p["d_skip_c"], p["ssd_norm_w"], f"ssd{l}", hosted)
    yn, s_in = res[:2]
    if hosted:
        sv["hosted"] = res[2]
    sv["yn"], sv["s_in"] = yn, s_in
    ya = _mm(yn, w["wso"], "nn", BF, 2048, 1024, 2048, f"ssdout{l}")
    k2 = _double_heads(proj[:, _OFF["k"]:_OFF["k"] + 128])
    v2 = _double_heads(proj[:, _OFF["v"]:_OFF["v"] + 128])
    sv["k2"], sv["v2"] = k2, v2
    att = _attn_fwd(proj, k2, v2, p["att_sinks"], f"attn{l}", hosted_attn)
    if hosted_attn:
        att, sv["hosted_attn"] = att
    sv["att"] = att
    yb = _mm(att, w["wao"], "nn", BF, 2048, 1024, 1024, f"attout{l}")
    sv["ya"], sv["yb"] = ya, yb
    gcb = _OFF["gates"] // 1024

    def gate_fn(ga, gb, ya_, yb_):
        return (_sigmoid(ga) * ya_.astype(F32) + _sigmoid(gb) * yb_.astype(F32),), ()
    (ub,), _ = _rows(gate_fn, [(proj, gcb, 1024), (proj, gcb + 1, 1024), (ya, 0, 1024), (yb, 0, 1024)], [],
                     [(D_MODEL, BF)], [], 2 * ROW_TILE, f"gate{l}")
    sv["ub"] = ub
    mix = _mm(ub, w["wmo"], "nn", F32, 2048, 1024, 1024, f"mixout{l}")
    sv["mix"] = mix
    h1, h1b = _ln_fwd(h, mix, p["ln_mix_g"], p["ln_mix_b"], f"lnmix{l}")
    sv["h1"], sv["h1b"] = h1, h1b
    gu = _mm(h1b, w["wgu"], "nn", BF, 2048, 2816, 1024, f"ffnin{l}")
    sv["gu"] = gu

    def act_fn(g_, u_):
        return (_silu(g_.astype(F32)) * u_.astype(F32),), ()
    (act,), _ = _rows(act_fn, [(gu, 0, FFN_HIDDEN), (gu, 1, FFN_HIDDEN)], [], [(FFN_HIDDEN, BF)], [], ROW_TILE, f"swiglu{l}")
    sv["act"] = act
    ffn = _mm(act, w["wd"], "nn", F32, 1024, 1024, FFN_HIDDEN, f"ffnout{l}")
    sv["ffn"] = ffn
    return sv


def _layer_bwd(l, sv, w, p, ds_f, ds_b, hosted=(), early_hook=None, late_hook=None):
    g = {}
    g["w_ffn_down"] = _mm(sv["act"], ds_b, "tn", F32, 1408, 1024, 2048, f"dwd{l}")
    dact = _mm(ds_b, w["wd"], "nt", BF, 1024, FFN_HIDDEN, 1024, f"dact{l}")

    def act_bwd(g_, u_, d_):
        _, vjp = jax.vjp(lambda a, b: _silu(a) * b, g_.astype(F32), u_.astype(F32))
        dg_, du_ = vjp(d_.astype(F32))
        return (jnp.concatenate([dg_, du_], axis=1),), ()
    (dgu,), _ = _rows(act_bwd, [(sv["gu"], 0, FFN_HIDDEN), (sv["gu"], 1, FFN_HIDDEN), (dact, 0, FFN_HIDDEN)], [],
                      [(2 * FFN_HIDDEN, BF)], [], ROW_TILE // 2, f"dswiglu{l}")
    dwgu = _mm(sv["h1b"], dgu, "tn", F32, 1024, 1408, 2048, f"dwgu{l}")
    g["w_ffn_gate"], g["w_ffn_up"] = dwgu[:, :FFN_HIDDEN], dwgu[:, FFN_HIDDEN:]
    dh1 = _mm(dgu, w["wgu"], "nt", F32, 1024, 1024, 2816, f"dh1{l}")
    ds2_f, ds2_b, g["ln_mix_g"], g["ln_mix_b"] = _ln_bwd(sv["h"], sv["mix"], ds_f, dh1, p["ln_mix_g"], p["ln_mix_b"], f"dlnmix{l}")
    g["w_mix_out"] = _mm(sv["ub"], ds2_b, "tn", F32, 1024, 1024, 2048, f"dwmo{l}")
    du = _mm(ds2_b, w["wmo"], "nt", BF, 1024, 1024, 1024, f"du{l}")
    proj = sv["proj"]
    gcb = _OFF["gates"] // 1024

    def gate_bwd(ga, gb, ya_, yb_, du_):
        _, vjp = jax.vjp(lambda a, b, c, d: _sigmoid(a) * c + _sigmoid(b) * d, ga, gb, ya_.astype(F32), yb_.astype(F32))
        dga, dgb, dya, dyb = vjp(du_.astype(F32))
        return (jnp.concatenate([dga, dgb], axis=1), dya, dyb), ()
    (dproj, dya, dyb), _ = _rows(
        gate_bwd, [(proj, gcb, 1024), (proj, gcb + 1, 1024), (sv["ya"], 0, 1024), (sv["yb"], 0, 1024), (du, 0, 1024)], [],
        [(2048, BF, NP), (D_MODEL, BF), (D_MODEL, BF)], [], ROW_TILE, f"dgate{l}")
    g["w_att_out"] = _mm(sv["att"], dyb, "tn", F32, 1024, 1024, 2048, f"dwao{l}")
    datt = _mm(dyb, w["wao"], "nt", BF, 1024, 1024, 1024, f"datt{l}")
    g["w_ssd_out"] = _mm(sv["yn"], dya, "tn", F32, 1024, 1024, 2048, f"dwso{l}")
    dyn = _mm(dya, w["wso"], "nt", F32, 1024, 1024, 1024, f"dyn{l}")
    hosted = list(hosted) + (list(early_hook(g)) if early_hook else [])
    dproj, dkp, dkc, dvp, dvc, g["att_sinks"] = _attn_bwd(proj, sv["k2"], sv["v2"], p["att_sinks"], datt, dproj, f"dattn{l}")
    res = _ssd_bwd(sv["pre"], proj, sv["dt_t"], p["dt_bias"], p["a_log"], p["d_skip_c"], p["ssd_norm_w"], sv["s_in"], dyn,
                   dproj, f"dssd{l}", (hosted, False) if hosted else None)
    (dpre, dproj, ddt_t, g["dt_bias"], g["a_log"], ddsk, g["ssd_norm_w"]) = res[:7]
    if hosted:
        g["hosted"] = res[7]
    g["d_skip"] = jnp.sum(ddsk.reshape(SSD_HEADS, 64), axis=1)
    dproj, g["conv_w"], g["conv_b"] = _conv_bwd(dpre, proj, p["conv_w"], dproj, 0, f"dconv{l}")
    length = proj.shape[0]
    tail = jnp.concatenate([_fold_heads(dkc, dkp).astype(BF), _fold_heads(dvc, dvp).astype(BF),
                            jnp.transpose(ddt_t).astype(BF), jnp.zeros((length, _PAD), BF)], axis=1)
    dproj = lax.dynamic_update_slice(dproj, tail, (0, _OFF["k"]))
    g["w_in"] = _mm(sv["hb"], dproj, "tn", F32, 1024, 2176, 1024, f"dwin{l}")
    late = list(late_hook(g)) if late_hook else []
    dh = _mm(dproj, w["win"], "nt", F32, 1024, 1024, 4352, f"dh{l}", (late, False) if late else None)
    if late:
        dh, g["late_hosted"] = dh
    return g, ds2_f, dh


def kernel(x, ln_in_g, ln_in_b, w_in, conv_w, conv_b, dt_bias, a_log, d_skip, ssd_norm_w, att_sinks, w_ssd_out, w_att_out, w_mix_out, ln_mix_g, ln_mix_b, w_ffn_gate, w_ffn_up, w_ffn_down, ln_ffn_g, ln_ffn_b, loss_target, m_ln_in_g, m_ln_in_b, m_w_in, m_conv_w, m_conv_b, m_dt_bias, m_a_log, m_d_skip, m_ssd_norm_w, m_att_sinks, m_w_ssd_out, m_w_att_out, m_w_mix_out, m_ln_mix_g, m_ln_mix_b, m_w_ffn_gate, m_w_ffn_up, m_w_ffn_down, m_ln_ffn_g, m_ln_ffn_b, v_ln_in_g, v_ln_in_b, v_w_in, v_conv_w, v_conv_b, v_dt_bias, v_a_log, v_d_skip, v_ssd_norm_w, v_att_sinks, v_w_ssd_out, v_w_att_out, v_w_mix_out, v_ln_mix_g, v_ln_mix_b, v_w_ffn_gate, v_w_ffn_up, v_w_ffn_down, v_ln_ffn_g, v_ln_ffn_b):
    env = dict(locals())
    wts = {n: env[n] for n in WEIGHTS}
    mom1 = {n: env["m_" + n] for n in WEIGHTS}
    mom2 = {n: env["v_" + n] for n in WEIGHTS}
    ci = lax.axis_index("c")
    chip = 2 * lax.axis_index("x") + lax.axis_index("y")
    xs_ = x[0]
    tgt = loss_target[0]

    COLS = ("w_in", "w_ffn_gate", "w_ffn_up")

    def half(t):
        return lax.dynamic_slice_in_dim(t, ci * (t.shape[0] // 2), t.shape[0] // 2, axis=0)

    def assemble(name, parts):
        if name == "w_in":
            return _gather_cols(parts, _IN_RANGES, _PAD)
        if name in COLS:
            return _gather_cols(parts, ((0, 4 * parts.shape[3]),))
        return jnp.concatenate([parts[h, s] for s in range(4) for h in range(2)], axis=0)

    def split(name, gfull):
        if name == "w_in":
            return _scatter_cols(gfull, _IN_RANGES, IN_DIM // 4)
        if name in COLS:
            return _scatter_cols(gfull, ((0, gfull.shape[1]),), gfull.shape[1] // 4)
        hr = gfull.shape[0] // 8
        return jnp.stack([jnp.stack([gfull[(2 * s + h) * hr:(2 * s + h + 1) * hr] for s in range(4)]) for h in range(2)])

    mine = [[half(wts[n][l].astype(BF)) for n in BIG] for l in range(DEPTH)]
    conv_all, = _with_own(_exchange([conv_w], "xy", True, "gather_conv"), [conv_w], chip)
    conv_full = jnp.transpose(conv_all, (1, 2, 0, 3)).reshape(DEPTH, SSD_CONV, CONV_DIM)

    KEYS = {"w_in": "win", "w_ssd_out": "wso", "w_att_out": "wao", "w_mix_out": "wmo", "w_ffn_down": "wd"}

    def layer_weights(l, names, by_chip):
        own = [mine[l][BIG.index(n)] for n in names]
        by_chip = _with_own(by_chip, own, chip)
        parts = _with_own(_exchange(by_chip, "c", True, f"gather_cores{l}_{names[0]}"), by_chip, ci)
        fw = {n: assemble(n, part) for n, part in zip(names, parts)}
        w = {KEYS[n]: fw[n] for n in names if n in KEYS}
        if "w_ffn_gate" in fw:
            w["wgu"] = jnp.concatenate([fw["w_ffn_gate"], fw["w_ffn_up"]], axis=1)
        return w

    layer_p = []
    for l in range(DEPTH):
        layer_p.append({
            "conv_w": conv_full[l], "conv_b": conv_b[l][None],
            "dt_bias": dt_bias[l][:, None], "a_log": a_log[l][:, None],
            "d_skip_c": jnp.repeat(d_skip[l], 64)[None], "ssd_norm_w": ssd_norm_w[l][None],
            "att_sinks": att_sinks[l][None], "ln_mix_g": ln_mix_g[l][None], "ln_mix_b": ln_mix_b[l][None],
            "ln_ffn_g": ln_ffn_g[l][None], "ln_ffn_b": ln_ffn_b[l][None],
        })

    def ln_in_fn(x_, g_, b_):
        y = _layer_norm(x_, g_, b_)
        return (y, y), ()
    (h, hb), _ = _rows(ln_in_fn, [(xs_, 0, D_MODEL)], [ln_in_g[None], ln_in_b[None]], [(D_MODEL, F32), (D_MODEL, BF)], [], 2 * ROW_TILE, "ln_in")
    rest = tuple(n for n in BIG if n != "w_in")
    layer_w = [layer_weights(0, ("w_in",), _exchange([mine[0][BIG.index("w_in")]], "xy", True, "gather_chips0"))]
    behind_ssd = [i for i, n in enumerate(BIG) if n in COLS]
    behind_attn = [i for i, n in enumerate(BIG) if n not in COLS]
    saved = []
    for l in range(DEPTH):
        nxt = l + 1 < DEPTH
        sv = _layer_fwd(l, h, hb, layer_w[l], layer_p[l], ([mine[l + 1][i] for i in behind_ssd], True) if nxt else None,
                        ([mine[l + 1][i] for i in behind_attn], True) if nxt else None,
                        ([mine[l][BIG.index(n)] for n in rest], True) if l == 0 else None,
                        functools.partial(layer_weights, l, rest))
        saved.append(sv)
        if nxt:
            got = dict(zip(behind_ssd + behind_attn, sv["hosted"] + sv["hosted_attn"]))
            layer_w.append(layer_weights(l + 1, BIG, [got[i] for i in range(len(BIG))]))
            h, hb = _ln_fwd(sv["h1"], sv["ffn"], layer_p[l]["ln_ffn_g"], layer_p[l]["ln_ffn_b"], f"lnffn{l}")

    def loss_fn(a_, r_, t_, g_, b_):
        y, vjp = jax.vjp(_layer_norm, ALPHA * a_ + r_, g_, b_)
        err = y - t_
        ds, dg, db = vjp(err * (1.0 / D_MODEL))
        part = 0.5 * jnp.sum(jnp.mean(err * err, axis=-1, keepdims=True), axis=0, keepdims=True)
        return (ds, ds), (dg, db, jnp.broadcast_to(part, (1, BLK)))
    sv = saved[-1]
    (ds_f, ds_b), (dg_last, db_last, loss_part) = _rows(
        loss_fn, [(sv["h1"], 0, D_MODEL), (sv["ffn"], 0, D_MODEL), (tgt, 0, D_MODEL)],
        [layer_p[-1]["ln_ffn_g"], layer_p[-1]["ln_ffn_b"]], [(D_MODEL, F32), (D_MODEL, BF)],
        [(1, D_MODEL), (1, D_MODEL), (1, BLK)], ROW_TILE, "loss")
    loss = lax.psum(loss_part[0, 0], ("x", "y", "c"))

    def reduce_cores(l, g, names):
        src = [split(n, g[n]) for n in names]
        recv = _exchange([s.astype(BF) for s in src], "c", False, f"reduce_cores{l}_{names[0]}")
        return {n: _sum_pieces(r, o, True, ci, f"sum_cores{l}_{n}") for n, r, o in zip(names, recv, src)}

    def bf16_of(sums, names):
        return [sums[n][1] for n in names]

    def add_chips(l, sums, from_chips):
        return [_sum_pieces(from_chips[n], sums[n][0], True, chip, f"sum_chips{l}_{n}")[0] for n in BIG]

    EARLY = tuple(n for n in BIG if n != "w_in")
    grads = [None] * DEPTH
    chip_sums = [None] * DEPTH
    for l in reversed(range(DEPTH)):
        sv = saved[l]
        if l == DEPTH - 1:
            lg, lb = dg_last, db_last
        else:
            ds_f, ds_b, lg, lb = _ln_bwd(sv["h1"], sv["ffn"], d1, d2, layer_p[l]["ln_ffn_g"], layer_p[l]["ln_ffn_b"], f"dlnffn{l}")
        sums = {}

        def early_hook(g, l=l, sums=sums):
            sums.update(reduce_cores(l, g, EARLY))
            return bf16_of(sums, EARLY)

        def late_hook(g, l=l, sums=sums):
            sums.update(reduce_cores(l, g, ("w_in",)))
            return bf16_of(sums, ("w_in",))
        if l == 0:
            g, d1, d2 = _layer_bwd(l, sv, layer_w[l], layer_p[l], ds_f, ds_b, bf16_of(chip_sums[l + 1], BIG), early_hook, late_hook)
        else:
            g, d1, d2 = _layer_bwd(l, sv, layer_w[l], layer_p[l], ds_f, ds_b)
            sums.update(reduce_cores(l, g, BIG))
        g["ln_ffn_g"], g["ln_ffn_b"] = lg, lb
        grads[l] = g
        chip_sums[l] = sums

    def ln_in_bwd(x_, d1_, d2_, g_, b_):
        _, vjp = jax.vjp(_layer_norm, x_, g_, b_)
        dx, dg, db = vjp(ALPHA * d1_ + d2_)
        return (dx,), (dg, db)
    (grad_x,), (g_ln_in_g, g_ln_in_b) = _rows(
        ln_in_bwd, [(xs_, 0, D_MODEL), (d1, 0, D_MODEL), (d2, 0, D_MODEL)], [ln_in_g[None], ln_in_b[None]],
        [(D_MODEL, F32)], [(1, D_MODEL), (1, D_MODEL)], ROW_TILE, "dln_in")

    carried = grads[0]["hosted"]
    from_chips = dict(zip(EARLY, carried[len(BIG):]))
    from_chips["w_in"], = grads[0]["late_hosted"]
    reduced = add_chips(0, chip_sums[0], from_chips) + add_chips(1, chip_sums[1], dict(zip(BIG, carried[:len(BIG)])))
    halves = _with_own(_exchange(reduced, "c", True, "swap_halves"), reduced, ci)
    shards = [t.reshape(2 * t.shape[1], t.shape[2]) for t in halves]
    big_grad = {n: jnp.stack([shards[l * len(BIG) + i] for l in range(DEPTH)]).reshape(wts[n].shape) for i, n in enumerate(BIG)}

    small_g = {"ln_in_g": g_ln_in_g[0], "ln_in_b": g_ln_in_b[0]}
    for n in SMALL[2:]:
        small_g[n] = jnp.stack([grads[l][n].reshape(wts[n].shape[1:] if n != "conv_w" else (SSD_CONV, CONV_DIM)) for l in range(DEPTH)])
    flat = jnp.concatenate([small_g[n].reshape(-1) for n in SMALL])
    n_small = flat.shape[0]
    width = -(-n_small // 1024) * 1024
    flat = jnp.pad(flat, (0, width - n_small)).reshape(8, width // 8)
    gathered, = _with_own(_exchange([flat], "all", True, "gather_small"), [flat], 2 * chip + ci)
    total = _sum_slots(gathered, "sum_small").reshape(-1)
    small_grad, off = {}, 0
    for n in SMALL:
        shp = small_g[n].shape
        cnt = math.prod(shp)
        small_grad[n] = total[off:off + cnt].reshape(shp)
        off += cnt
    small_grad["conv_w"] = lax.dynamic_slice_in_dim(small_grad["conv_w"], chip * (CONV_DIM // 4), CONV_DIM // 4, axis=2)

    out_g, out_d, out_m, out_v = {}, {}, {}, {}
    for n in BIG:
        shp = wts[n].shape
        two_d = lambda t: t.reshape(shp[0] * shp[1], shp[2])
        d_, m_, v_ = _adamw(two_d(wts[n]), two_d(big_grad[n]), two_d(mom1[n]), two_d(mom2[n]), f"adamw_{n}")
        out_g[n], out_d[n], out_m[n], out_v[n] = big_grad[n], d_.reshape(shp), m_.reshape(shp), v_.reshape(shp)

    def flat_small(d):
        f = jnp.concatenate([d[n].reshape(-1) for n in SMALL])
        return jnp.pad(f, (0, swidth - f.shape[0])).reshape(8, swidth // 8)
    n_sw = sum(math.prod(wts[n].shape) for n in SMALL)
    swidth = -(-n_sw // 1024) * 1024
    d_, m_, v_ = _adamw(flat_small(wts), flat_small(small_grad), flat_small(mom1), flat_small(mom2), "adamw_small")
    off = 0
    for n in SMALL:
        shp = wts[n].shape
        cnt = math.prod(shp)
        out_g[n] = small_grad[n]
        out_d[n] = d_.reshape(-1)[off:off + cnt].reshape(shp)
        out_m[n] = m_.reshape(-1)[off:off + cnt].reshape(shp)
        out_v[n] = v_.reshape(-1)[off:off + cnt].reshape(shp)
        off += cnt

    return (loss, grad_x[None], *[out_g[n] for n in WEIGHTS], *[out_d[n] for n in WEIGHTS],
            *[out_m[n] for n in WEIGHTS], *[out_v[n] for n in WEIGHTS])
```

```python
import functools
import itertools
import math

import jax
import jax.numpy as jnp
from jax import lax
from jax.experimental import pallas as pl
from jax.experimental.pallas import tpu as pltpu

F32 = jnp.float32
BF = jnp.bfloat16

D_MODEL = 1024
DEPTH = 2
ATT_HEADS = 16
ATT_HEAD_DIM = 64
BLK = 128
SSD_D_INNER = 2048
SSD_HEADS = 32
SSD_GROUPS = 4
SSD_STATE = 128
SSD_CONV = 4
BC_DIM = 512
CONV_DIM = 3072
FFN_HIDDEN = 2816
IN_DIM = 8480
LN_EPS = 1e-5
RMS_EPS = 1e-5
ALPHA = (2 * DEPTH) ** 0.25
ADAM_LR = 0.001
ADAM_B1 = 0.9
ADAM_B2 = 0.999
ADAM_EPS = 1e-08
ADAM_WD = 0.01
ADAM_STEP = 10

_PACK = (("gates", 6432, 2048), ("z", 1280, 2048), ("q", 0, 1024), ("xs", 3328, 2048), ("B", 5376, 512),
         ("C", 5888, 512), ("k", 1024, 128), ("v", 1152, 128), ("dt", 6400, 32))
NP = 8704
_OFF = {}
_o = 0
for _n, _, _w in _PACK:
    _OFF[_n] = _o
    _o += _w
_PAD = NP - _o

VMEM_LIMIT_BYTES = 56 * 1024 * 1024
ROW_TILE = 512
BIG = ("w_in", "w_ssd_out", "w_att_out", "w_mix_out", "w_ffn_gate", "w_ffn_up", "w_ffn_down")
WEIGHTS = ("ln_in_g", "ln_in_b", "w_in", "conv_w", "conv_b", "dt_bias", "a_log", "d_skip", "ssd_norm_w", "att_sinks",
           "w_ssd_out", "w_att_out", "w_mix_out", "ln_mix_g", "ln_mix_b", "w_ffn_gate", "w_ffn_up", "w_ffn_down",
           "ln_ffn_g", "ln_ffn_b")
SMALL = tuple(n for n in WEIGHTS if n not in BIG)


def _params(n_grid):
    return pltpu.CompilerParams(dimension_semantics=("arbitrary",) * n_grid, vmem_limit_bytes=VMEM_LIMIT_BYTES)


def _dot(a, b, ca, cb):
    return lax.dot_general(a.astype(BF), b.astype(BF), (((ca,), (cb,)), ((), ())), preferred_element_type=F32)


@jax.custom_vjp
def _nn(a, b):
    return _dot(a, b, 1, 0)


def _nn_f(a, b):
    return _dot(a, b, 1, 0), (a.astype(BF), b.astype(BF))


def _nn_b(res, g):
    a, b = res
    return _dot(g, b, 1, 1), _dot(a, g, 0, 0)


_nn.defvjp(_nn_f, _nn_b)


@jax.custom_vjp
def _nt(a, b):
    return _dot(a, b, 1, 1)


def _nt_f(a, b):
    return _dot(a, b, 1, 1), (a.astype(BF), b.astype(BF))


def _nt_b(res, g):
    a, b = res
    return _dot(g, b, 1, 0), _dot(g, a, 0, 0)


_nt.defvjp(_nt_f, _nt_b)


@jax.custom_vjp
def _tn(a, b):
    return _dot(a, b, 0, 0)


def _tn_f(a, b):
    return _dot(a, b, 0, 0), (a.astype(BF), b.astype(BF))


def _tn_b(res, g):
    a, b = res
    return _dot(b, g, 1, 1), _dot(a, g, 1, 0)


_tn.defvjp(_tn_f, _tn_b)


def _sigmoid(x):
    return 0.5 * jnp.tanh(0.5 * x) + 0.5


def _silu(x):
    return x * _sigmoid(x)


def _layer_norm(s, g, b):
    mu = jnp.mean(s, axis=-1, keepdims=True)
    sc = s - mu
    var = jnp.mean(sc * sc, axis=-1, keepdims=True)
    return sc * lax.rsqrt(var + LN_EPS) * g + b


def _ssd_group(pre_x, pre_b, pre_c, z, dtr, dtb, alog, dsk, nw, state):
    t = pre_x.shape[0]
    xs, bm, cm = _silu(pre_x), _silu(pre_b), _silu(pre_c)
    dt = jax.nn.softplus(dtr + dtb)
    da = dt * (-jnp.exp(alog))
    row = lax.broadcasted_iota(jnp.int32, (t, t), 0)
    col = lax.broadcasted_iota(jnp.int32, (t, t), 1)
    upper = (row <= col).astype(F32)
    cum = jnp.dot(da, upper, precision=lax.Precision.HIGHEST, preferred_element_type=F32)
    tot = jnp.sum(da, axis=1, keepdims=True)
    cb = _nt(cm, bm)
    tril = row >= col
    first = lax.broadcasted_iota(jnp.int32, (t, BLK), 1) < 64
    first_row = lax.broadcasted_iota(jnp.int32, (1, BLK), 1) < 64

    def col_form(r):
        return jnp.broadcast_to(r, (t, t)).T

    ys, new_state = [], []
    for p in range(4):
        h0, h1 = 2 * p, 2 * p + 1
        sl = slice(BLK * p, BLK * (p + 1))
        x_pair = xs[:, sl] * jnp.where(first, col_form(dt[h0:h0 + 1]), col_form(dt[h1:h1 + 1]))
        y_pair = None
        cc = []
        for h, keep in ((h0, first), (h1, jnp.logical_not(first))):
            cr = jnp.broadcast_to(cum[h:h + 1], (t, t))
            cc.append(cr.T)
            decay = jnp.exp(jnp.where(tril, cc[-1] - cr, -1e30))
            y_h = _nn(decay * cb, jnp.where(keep, x_pair, 0.0))
            y_pair = y_h if y_pair is None else y_pair + y_h
        s_pair = state[:, sl]
        y_pair = y_pair + _nn(cm, s_pair) * jnp.where(first, jnp.exp(cc[0]), jnp.exp(cc[1]))
        to_end = jnp.where(first, jnp.exp(tot[h0:h0 + 1] - cc[0]), jnp.exp(tot[h1:h1 + 1] - cc[1]))
        chunk_decay = jnp.where(first_row, jnp.exp(tot[h0:h0 + 1]), jnp.exp(tot[h1:h1 + 1]))
        new_state.append(s_pair * chunk_decay + _tn(bm, x_pair * to_end))
        ys.append(y_pair + dsk[:, sl] * xs[:, sl])
    y = jnp.concatenate(ys, axis=1) * _silu(z)
    y = y * lax.rsqrt(jnp.mean(y * y, axis=-1, keepdims=True) + RMS_EPS) * nw
    return y, jnp.concatenate(new_state, axis=1)


def _attn_block(q, kp, kc, vp, vc, sinks, n, kv):
    t = q.shape[0]
    kb = jnp.concatenate([kp, kc], axis=0)
    vb = jnp.concatenate([vp, vc], axis=0)
    qi = lax.broadcasted_iota(jnp.int32, (t, 2 * t), 0)
    kj = lax.broadcasted_iota(jnp.int32, (t, 2 * t), 1)
    rel = qi + t - kj
    valid = (rel >= 0) & (rel < t) & ((n * t - t + kj) >= 0)
    relf = rel.astype(F32)
    first = lax.broadcasted_iota(jnp.int32, (t, BLK), 1) < 64
    lane16 = lax.broadcasted_iota(jnp.int32, (1, ATT_HEADS), 1)
    outs = []
    for p in range(4):
        qp = q[:, BLK * p:BLK * (p + 1)] * (ATT_HEAD_DIM ** -0.5)
        o_pair = None
        for half, keep in enumerate((first, jnp.logical_not(first))):
            h = kv * 8 + 2 * p + half
            s = _nt(jnp.where(keep, qp, 0.0), kb)
            slope = jnp.exp((-8.0 * math.log(2.0) / ATT_HEADS) * jnp.asarray(h + 1, F32))
            s = jnp.where(valid, s - slope * relf, -1e30)
            sink = jnp.sum(jnp.where(lane16 == h, sinks, 0.0), axis=1, keepdims=True)
            m = lax.stop_gradient(jnp.maximum(jnp.max(s, axis=1, keepdims=True), sink))
            e = jnp.exp(s - m)
            den = jnp.sum(e, axis=1, keepdims=True) + jnp.exp(sink - m)
            o = _nn(e * (1.0 / den), vb)
            o_pair = o if o_pair is None else jnp.where(first, o_pair, o)
        outs.append(o_pair)
    return jnp.concatenate(outs, axis=1)


def _chip_exchange(src, out, send_sems, recv_sems, bcast):
    x, y, c = lax.axis_index("x"), lax.axis_index("y"), lax.axis_index("c")
    me = 2 * x + y

    def copy(a, d, slot):
        j = jnp.bitwise_xor(me, d)
        return pltpu.make_async_remote_copy(
            src_ref=src[a] if bcast else src[a].at[j], dst_ref=out[a].at[slot],
            send_sem=send_sems.at[a * 4 + d], recv_sem=recv_sems.at[a * 4 + d],
            device_id=(j // 2, j % 2, c), device_id_type=pl.DeviceIdType.MESH)

    pairs = [(a, d) for a in range(len(src)) for d in range(1, 4)]

    def start():
        for a, d in pairs:
            copy(a, d, me).start()

    def wait():
        for a, d in pairs:
            copy(a, d, jnp.bitwise_xor(me, d)).wait_recv()
        for a, d in pairs:
            copy(a, d, me).wait_send()

    return start, wait


def _mm(a, b, mode, out_dtype, tm, tn, tk, name, hosted=None):
    if mode == "nn":
        (m, k), (k2, n) = a.shape, b.shape
    elif mode == "nt":
        (m, k), (n, k2) = a.shape, b.shape
    else:
        (k, m), (k2, n) = a.shape, b.shape
    assert k == k2, (a.shape, b.shape, mode)
    tm, tn, tk = min(tm, m), min(tn, n), min(tk, k)
    assert m % tm == 0 and n % tn == 0 and k % tk == 0, (m, n, k, tm, tn, tk)
    nk = k // tk
    grid = (m // tm, n // tn, nk)
    ca, cb = {"nn": (1, 0), "nt": (1, 1), "tn": (0, 0)}[mode]
    h_arrays, h_bcast = hosted if hosted else ((), False)
    nh = len(h_arrays)

    def body(*refs):
        a_ref, b_ref = refs[:2]
        o_ref = refs[2 + nh]
        scratch = refs[3 + 2 * nh:]
        step = [pl.program_id(ax) for ax in range(3)]
        if nh:
            start, wait = _chip_exchange(refs[2:2 + nh], refs[3 + nh:3 + 2 * nh], scratch[-2], scratch[-1], h_bcast)

            @pl.when((step[0] == 0) & (step[1] == 0) & (step[2] == 0))
            def _():
                start()

        part = _dot(a_ref[...], b_ref[...], ca, cb)
        if nk == 1:
            o_ref[...] = part.astype(o_ref.dtype)
        else:
            acc_ref = scratch[0]

            @pl.when(step[2] == 0)
            def _():
                acc_ref[...] = part

            @pl.when(step[2] > 0)
            def _():
                acc_ref[...] += part

            @pl.when(step[2] == nk - 1)
            def _():
                o_ref[...] = acc_ref[...].astype(o_ref.dtype)

        if nh:
            @pl.when((step[0] == grid[0] - 1) & (step[1] == grid[1] - 1) & (step[2] == nk - 1))
            def _():
                wait()

    a_spec = pl.BlockSpec((tk, tm), lambda i, j, kk: (kk, i)) if mode == "tn" else pl.BlockSpec((tm, tk), lambda i, j, kk: (i, kk))
    b_spec = pl.BlockSpec((tn, tk), lambda i, j, kk: (j, kk)) if mode == "nt" else pl.BlockSpec((tk, tn), lambda i, j, kk: (kk, j))
    any_spec = pl.BlockSpec(memory_space=pl.ANY)
    res = pl.pallas_call(
        body, name=name, grid=grid,
        in_specs=[a_spec, b_spec] + [any_spec] * nh,
        out_specs=[pl.BlockSpec((tm, tn), lambda i, j, kk: (i, j))] + [any_spec] * nh,
        out_shape=[jax.ShapeDtypeStruct((m, n), out_dtype)]
        + [jax.ShapeDtypeStruct(((4,) + s.shape) if h_bcast else s.shape, s.dtype) for s in h_arrays],
        scratch_shapes=([] if nk == 1 else [pltpu.VMEM((tm, tn), F32)])
        + ([pltpu.SemaphoreType.DMA((4 * nh,)), pltpu.SemaphoreType.DMA((4 * nh,))] if nh else []),
        compiler_params=_params(3),
    )(a, b, *h_arrays)
    return (res[0], list(res[1:])) if nh else res[0]


def _rows(fn, rows, params, outs, accs, tile, name):
    length = rows[0][0].shape[0]
    tile = min(tile, length)
    assert length % tile == 0
    nr, npar, no = len(rows), len(params), len(outs)

    def body(*refs):
        vals = [r[...] for r in refs[:nr + npar]]
        o, a = fn(*vals)
        for ref, val in zip(refs[nr + npar:nr + npar + no], o):
            ref[...] = val.astype(ref.dtype)
        i = pl.program_id(0)
        for ref, val in zip(refs[nr + npar + no:], a):
            @pl.when(i == 0)
            def _(ref=ref, val=val):
                ref[...] = val

            @pl.when(i > 0)
            def _(ref=ref, val=val):
                ref[...] += val

    in_specs = [pl.BlockSpec((tile, w), functools.partial(lambda i, cb: (i, cb), cb=cb)) for _, cb, w in rows]
    in_specs += [pl.BlockSpec(p.shape, lambda i: (0, 0)) for p in params]
    outs = [o if len(o) == 3 else (o[0], o[1], o[0]) for o in outs]
    out_specs = [pl.BlockSpec((tile, w), lambda i: (i, 0)) for w, _, _ in outs]
    out_specs += [pl.BlockSpec((r, w), lambda i: (0, 0)) for r, w in accs]
    out_shape = [jax.ShapeDtypeStruct((length, full), dt) for _, dt, full in outs]
    out_shape += [jax.ShapeDtypeStruct((r, w), F32) for r, w in accs]
    res = pl.pallas_call(
        body, name=name, grid=(length // tile,), in_specs=in_specs, out_specs=out_specs, out_shape=out_shape,
        compiler_params=_params(1),
    )(*[r[0] for r in rows], *params)
    return res[:no], res[no:]


CONV_SUB = 32


def _conv_fwd(proj, conv_w, conv_b, name, hosted=None):
    length = proj.shape[0]
    tl, cw = min(1024, length), 512
    cb0 = _OFF["xs"] // cw
    grid = (CONV_DIM // cw, length // tl)
    h_arrays, h_bcast = hosted if hosted else ((), False)
    nh = len(h_arrays)

    def body(*refs):
        u_ref, halo_ref, w_ref, b_ref = refs[:4]
        o_ref = refs[4 + nh]
        win = refs[5 + 2 * nh]
        j, i = pl.program_id(0), pl.program_id(1)
        if nh:
            start, wait = _chip_exchange(refs[4:4 + nh], refs[5 + nh:5 + 2 * nh], refs[-2], refs[-1], h_bcast)

            @pl.when((j == 0) & (i == 0))
            def _():
                start()

            @pl.when((j == grid[0] - 1) & (i == grid[1] - 1))
            def _():
                wait()

        win[0:8, :] = jnp.where(i > 0, halo_ref[...], 0.0)
        win[8:8 + tl, :] = u_ref[...]
        taps = [w_ref[kk:kk + 1, :] for kk in range(SSD_CONV)]
        bias = b_ref[...]
        for r in range(0, tl, CONV_SUB):
            acc = bias + taps[0] * win[5 + r:5 + r + CONV_SUB, :]
            for kk in range(1, SSD_CONV):
                acc = acc + taps[kk] * win[5 + kk + r:5 + kk + r + CONV_SUB, :]
            o_ref[r:r + CONV_SUB, :] = acc

    any_spec = pl.BlockSpec(memory_space=pl.ANY)
    res = pl.pallas_call(
        body, name=name, grid=grid,
        in_specs=[pl.BlockSpec((tl, cw), lambda j, i: (i, cb0 + j)),
                  pl.BlockSpec((8, cw), lambda j, i: (jnp.maximum(i * (tl // 8) - 1, 0), cb0 + j)),
                  pl.BlockSpec((SSD_CONV, cw), lambda j, i: (0, j)),
                  pl.BlockSpec((1, cw), lambda j, i: (0, j))] + [any_spec] * nh,
        out_specs=[pl.BlockSpec((tl, cw), lambda j, i: (i, j))] + [any_spec] * nh,
        out_shape=[jax.ShapeDtypeStruct((length, CONV_DIM), F32)]
        + [jax.ShapeDtypeStruct(((4,) + s.shape) if h_bcast else s.shape, s.dtype) for s in h_arrays],
        scratch_shapes=[pltpu.VMEM((8 + tl, cw), F32)]
        + ([pltpu.SemaphoreType.DMA((4 * nh,)), pltpu.SemaphoreType.DMA((4 * nh,))] if nh else []),
        compiler_params=_params(2),
    )(proj, proj, conv_w, conv_b, *h_arrays)
    return (res[0], list(res[1:])) if nh else res[0]


def _conv_bwd(dpre, proj, conv_w, dproj, part, name):
    length = proj.shape[0]
    width = dpre.shape[1]
    tl, cw = min(1024, length), 512
    cb0 = _OFF["xs"] // cw + part
    n_t = length // tl

    def body(d_ref, dnext_ref, u_ref, halo_ref, w_ref, _, du_ref, dw_ref, db_ref, dwin, uwin):
        i = pl.program_id(1)
        dwin[0:tl, :] = d_ref[...].astype(F32)
        dwin[tl:tl + 8, :] = jnp.where(i < n_t - 1, dnext_ref[...].astype(F32)[0:8], 0.0)
        uwin[0:8, :] = jnp.where(i > 0, halo_ref[...], 0.0)
        uwin[8:8 + tl, :] = u_ref[...]
        taps = [w_ref[kk:kk + 1, :] for kk in range(SSD_CONV)]
        sub = CONV_SUB // 2
        acc_w = [jnp.zeros((sub, cw), F32) for _ in range(SSD_CONV)]
        acc_b = jnp.zeros((sub, cw), F32)
        for r in range(0, tl, sub):
            d = dwin[r:r + sub, :]
            du = taps[3] * d
            for kk in range(SSD_CONV - 1):
                du = du + taps[kk] * dwin[3 - kk + r:3 - kk + r + sub, :]
            du_ref[r:r + sub, :] = du.astype(du_ref.dtype)
            for kk in range(SSD_CONV):
                acc_w[kk] = acc_w[kk] + d * uwin[5 + kk + r:5 + kk + r + sub, :]
            acc_b = acc_b + d
        dw = jnp.concatenate([jnp.sum(a, axis=0, keepdims=True) for a in acc_w], axis=0)
        db = jnp.sum(acc_b, axis=0, keepdims=True)

        @pl.when(i == 0)
        def _():
            dw_ref[...] = dw
            db_ref[...] = db

        @pl.when(i > 0)
        def _():
            dw_ref[...] += dw
            db_ref[...] += db

    return pl.pallas_call(
        body, name=name, grid=(width // cw, n_t), input_output_aliases={5: 0},
        in_specs=[pl.BlockSpec((tl, cw), lambda j, i: (i, j)),
                  pl.BlockSpec((16, cw), lambda j, i: (jnp.minimum((i + 1) * (tl // 16), length // 16 - 1), j)),
                  pl.BlockSpec((tl, cw), lambda j, i: (i, cb0 + j)),
                  pl.BlockSpec((8, cw), lambda j, i: (jnp.maximum(i * (tl // 8) - 1, 0), cb0 + j)),
                  pl.BlockSpec((SSD_CONV, cw), lambda j, i: (0, part + j)),
                  pl.BlockSpec(memory_space=pl.ANY)],
        out_specs=[pl.BlockSpec((tl, cw), lambda j, i: (i, cb0 + j)),
                   pl.BlockSpec((SSD_CONV, cw), lambda j, i: (0, j)),
                   pl.BlockSpec((1, cw), lambda j, i: (0, j))],
        out_shape=[jax.ShapeDtypeStruct((length, NP), BF),
                   jax.ShapeDtypeStruct((SSD_CONV, width), F32),
                   jax.ShapeDtypeStruct((1, width), F32)],
        scratch_shapes=[pltpu.VMEM((tl + 8, cw), F32), pltpu.VMEM((8 + tl, cw), F32)],
        compiler_params=_params(2),
    )(dpre, dpre, proj, proj, conv_w, dproj)


def _ssd_in_specs(rev, nc):
    def cidx(c):
        return nc - 1 - c if rev else c
    whole = lambda c: (0, 0)
    return [pl.BlockSpec((BLK, SSD_D_INNER), lambda c: (cidx(c), 0)),
            pl.BlockSpec((BLK, BC_DIM), lambda c: (cidx(c), SSD_D_INNER // BC_DIM)),
            pl.BlockSpec((BLK, BC_DIM), lambda c: (cidx(c), SSD_D_INNER // BC_DIM + 1)),
            pl.BlockSpec((BLK, SSD_D_INNER), lambda c: (cidx(c), _OFF["z"] // SSD_D_INNER)),
            pl.BlockSpec((SSD_HEADS, BLK), lambda c: (0, cidx(c))),
            pl.BlockSpec((SSD_HEADS, 1), whole),
            pl.BlockSpec((SSD_HEADS, 1), whole),
            pl.BlockSpec((1, SSD_D_INNER), whole),
            pl.BlockSpec((1, SSD_D_INNER), whole)]


def _group_args(g, px, pb, pc, z, dtr, dtb, alog, dsk, nw):
    wide, narrow, heads = slice(512 * g, 512 * (g + 1)), slice(BLK * g, BLK * (g + 1)), slice(8 * g, 8 * (g + 1))
    return (px[:, wide], pb[:, narrow], pc[:, narrow], z[:, wide], dtr[heads, :], dtb[heads, :], alog[heads, :],
            dsk[:, wide], nw[:, wide])


def _ssd_fwd(pre, proj, dt_t, dtb, alog, dsk, nw, name, hosted=None):
    length = pre.shape[0]
    nc = length // BLK
    h_arrays, h_bcast = hosted if hosted else ((), False)
    nh = len(h_arrays)

    def body(*refs):
        px, pb, pc, z, dtr, dtb_r, al_r, dsk_r, nw_r = refs[:9]
        y_ref, sin_ref = refs[9 + nh:11 + nh]
        st = refs[11 + 2 * nh]
        c = pl.program_id(0)
        if nh:
            start, wait = _chip_exchange(refs[9:9 + nh], refs[11 + nh:11 + 2 * nh], refs[-2], refs[-1], h_bcast)

        @pl.when(c == 0)
        def _():
            st[...] = jnp.zeros_like(st)
            if nh:
                start()

        for g in range(SSD_GROUPS):
            s_in = st[g]
            sin_ref[g] = s_in
            y, s_out = _ssd_group(*_group_args(g, px, pb, pc, z, dtr, dtb_r, al_r, dsk_r, nw_r), s_in)
            y_ref[:, 512 * g:512 * (g + 1)] = y.astype(y_ref.dtype)
            st[g] = s_out

        if nh:
            @pl.when(c == nc - 1)
            def _():
                wait()

    any_spec = pl.BlockSpec(memory_space=pl.ANY)
    res = pl.pallas_call(
        body, name=name, grid=(nc,), in_specs=_ssd_in_specs(False, nc) + [any_spec] * nh,
        out_specs=[pl.BlockSpec((BLK, SSD_D_INNER), lambda c: (c, 0)),
                   pl.BlockSpec((SSD_GROUPS, None, SSD_STATE, 512), lambda c: (0, c, 0, 0))] + [any_spec] * nh,
        out_shape=[jax.ShapeDtypeStruct((length, SSD_D_INNER), BF),
                   jax.ShapeDtypeStruct((SSD_GROUPS, nc, SSD_STATE, 512), F32)]
        + [jax.ShapeDtypeStruct(((4,) + s.shape) if h_bcast else s.shape, s.dtype) for s in h_arrays],
        scratch_shapes=[pltpu.VMEM((SSD_GROUPS, SSD_STATE, 512), F32)]
        + ([pltpu.SemaphoreType.DMA((4 * nh,)), pltpu.SemaphoreType.DMA((4 * nh,))] if nh else []),
        compiler_params=_params(1),
    )(pre, pre, pre, proj, dt_t, dtb, alog, dsk, nw, *h_arrays)
    return (res[0], res[1], list(res[2:])) if nh else (res[0], res[1])


def _ssd_bwd(pre, proj, dt_t, dtb, alog, dsk, nw, s_in, dy, dproj, name, hosted=None):
    length = pre.shape[0]
    nc = length // BLK
    h_arrays, h_bcast = hosted if hosted else ((), False)
    nh = len(h_arrays)

    def body(*refs):
        px, pb, pc, z, dtr, dtb_r, al_r, dsk_r, nw_r, sin_r, dy_r = refs[:11]
        dpre, dz, ddt, ddtb, dal, ddsk, dnw = refs[12 + nh:19 + nh]
        dst = refs[19 + 2 * nh]
        c = pl.program_id(0)
        if nh:
            start, wait = _chip_exchange(refs[12:12 + nh], refs[19 + nh:19 + 2 * nh], refs[-2], refs[-1], h_bcast)

        @pl.when(c == 0)
        def _():
            dst[...] = jnp.zeros_like(dst)
            if nh:
                start()

        for grp in range(SSD_GROUPS):
            wide, heads = slice(512 * grp, 512 * (grp + 1)), slice(8 * grp, 8 * (grp + 1))
            _, vjp = jax.vjp(_ssd_group, *_group_args(grp, px, pb, pc, z, dtr, dtb_r, al_r, dsk_r, nw_r), sin_r[grp])
            g = vjp((dy_r[:, wide], dst[grp]))
            dpre[:, wide] = g[0].astype(dpre.dtype)
            dpre[:, slice(SSD_D_INNER + BLK * grp, SSD_D_INNER + BLK * (grp + 1))] = g[1].astype(dpre.dtype)
            dpre[:, slice(SSD_D_INNER + BC_DIM + BLK * grp, SSD_D_INNER + BC_DIM + BLK * (grp + 1))] = g[2].astype(dpre.dtype)
            dz[:, wide] = g[3].astype(dz.dtype)
            ddt[heads, :] = g[4]
            dst[grp] = g[9]
            for ref, val, idx in ((ddtb, g[5], (heads, slice(None))), (dal, g[6], (heads, slice(None))),
                                  (ddsk, g[7], (slice(None), wide)), (dnw, g[8], (slice(None), wide))):
                @pl.when(c == 0)
                def _(ref=ref, val=val, idx=idx):
                    ref[idx] = val

                @pl.when(c > 0)
                def _(ref=ref, val=val, idx=idx):
                    ref[idx] += val

        if nh:
            @pl.when(c == nc - 1)
            def _():
                wait()

    rev = lambda c: nc - 1 - c
    whole = lambda c: (0, 0)
    any_spec = pl.BlockSpec(memory_space=pl.ANY)
    in_specs = _ssd_in_specs(True, nc) + [
        pl.BlockSpec((SSD_GROUPS, None, SSD_STATE, 512), lambda c: (0, rev(c), 0, 0)),
        pl.BlockSpec((BLK, SSD_D_INNER), lambda c: (rev(c), 0)),
        any_spec] + [any_spec] * nh
    res = pl.pallas_call(
        body, name=name, grid=(nc,), in_specs=in_specs, input_output_aliases={11: 1},
        out_specs=[pl.BlockSpec((BLK, CONV_DIM), lambda c: (rev(c), 0)),
                   pl.BlockSpec((BLK, SSD_D_INNER), lambda c: (rev(c), _OFF["z"] // SSD_D_INNER)),
                   pl.BlockSpec((SSD_HEADS, BLK), lambda c: (0, rev(c))),
                   pl.BlockSpec((SSD_HEADS, 1), whole),
                   pl.BlockSpec((SSD_HEADS, 1), whole),
                   pl.BlockSpec((1, SSD_D_INNER), whole),
                   pl.BlockSpec((1, SSD_D_INNER), whole)] + [any_spec] * nh,
        out_shape=[jax.ShapeDtypeStruct((length, CONV_DIM), BF),
                   jax.ShapeDtypeStruct((length, NP), BF),
                   jax.ShapeDtypeStruct((SSD_HEADS, length), F32),
                   jax.ShapeDtypeStruct((SSD_HEADS, 1), F32),
                   jax.ShapeDtypeStruct((SSD_HEADS, 1), F32),
                   jax.ShapeDtypeStruct((1, SSD_D_INNER), F32),
                   jax.ShapeDtypeStruct((1, SSD_D_INNER), F32)]
        + [jax.ShapeDtypeStruct(((4,) + s.shape) if h_bcast else s.shape, s.dtype) for s in h_arrays],
        scratch_shapes=[pltpu.VMEM((SSD_GROUPS, SSD_STATE, 512), F32)]
        + ([pltpu.SemaphoreType.DMA((4 * nh,)), pltpu.SemaphoreType.DMA((4 * nh,))] if nh else []),
        compiler_params=_params(1),
    )(pre, pre, pre, proj, dt_t, dtb, alog, dsk, nw, s_in, dy, dproj, *h_arrays)
    return tuple(res[:7]) + ((list(res[7:]),) if nh else ())


assert _OFF["gates"] == 0 and _OFF["z"] % SSD_D_INNER == 0 and _OFF["q"] % D_MODEL == 0 and _OFF["k"] + 512 == NP


def _attn_in_specs():
    prev = lambda n: (0, jnp.maximum(n - 1, 0), 0)
    cur = lambda n: (0, n, 0)
    return [pl.BlockSpec((BLK, D_MODEL), lambda n: (n, _OFF["q"] // D_MODEL)),
            pl.BlockSpec((2, BLK, BLK), prev),
            pl.BlockSpec((2, BLK, BLK), cur),
            pl.BlockSpec((2, BLK, BLK), prev),
            pl.BlockSpec((2, BLK, BLK), cur),
            pl.BlockSpec((1, ATT_HEADS), lambda n: (0, 0))]


def _attn_fwd(proj, k2, v2, sinks, name, hosted=None):
    length = proj.shape[0]
    nb = length // BLK
    h_arrays, h_bcast = hosted if hosted else ((), False)
    nh = len(h_arrays)

    def body(*refs):
        q, kp, kc, vp, vc, sk = refs[:6]
        o_ref = refs[6 + nh]
        n = pl.program_id(0)
        if nh:
            start, wait = _chip_exchange(refs[6:6 + nh], refs[7 + nh:7 + 2 * nh], refs[-2], refs[-1], h_bcast)

            @pl.when(n == 0)
            def _():
                start()

        for kv in range(2):
            cols = slice(512 * kv, 512 * (kv + 1))
            o = _attn_block(q[:, cols], kp[kv], kc[kv], vp[kv], vc[kv], sk[...], n, kv)
            o_ref[:, cols] = o.astype(o_ref.dtype)

        if nh:
            @pl.when(n == nb - 1)
            def _():
                wait()

    any_spec = pl.BlockSpec(memory_space=pl.ANY)
    res = pl.pallas_call(
        body, name=name, grid=(nb,), in_specs=_attn_in_specs() + [any_spec] * nh,
        out_specs=[pl.BlockSpec((BLK, D_MODEL), lambda n: (n, 0))] + [any_spec] * nh,
        out_shape=[jax.ShapeDtypeStruct((length, D_MODEL), BF)]
        + [jax.ShapeDtypeStruct(((4,) + s.shape) if h_bcast else s.shape, s.dtype) for s in h_arrays],
        scratch_shapes=[pltpu.SemaphoreType.DMA((4 * nh,)), pltpu.SemaphoreType.DMA((4 * nh,))] if nh else [],
        compiler_params=_params(1),
    )(proj, k2, k2, v2, v2, sinks, *h_arrays)
    return (res[0], list(res[1:])) if nh else res[0]


def _attn_bwd(proj, k2, v2, sinks, datt, dproj, name):
    length = proj.shape[0]

    def body(q, kp, kc, vp, vc, sk, do, _, dq, dkp, dkc, dvp, dvc, dsk):
        n = pl.program_id(0)
        dsinks = None
        for kv in range(2):
            cols = slice(512 * kv, 512 * (kv + 1))
            _, vjp = jax.vjp(lambda *a: _attn_block(*a, n, kv), q[:, cols], kp[kv], kc[kv], vp[kv], vc[kv], sk[...])
            g = vjp(do[:, cols].astype(F32))
            dq[:, cols] = g[0].astype(dq.dtype)
            dkp[kv] = g[1]
            dkc[kv] = g[2]
            dvp[kv] = g[3]
            dvc[kv] = g[4]
            dsinks = g[5] if dsinks is None else dsinks + g[5]

        @pl.when(n == 0)
        def _():
            dsk[...] = dsinks

        @pl.when(n > 0)
        def _():
            dsk[...] += dsinks

    blk3 = pl.BlockSpec((2, BLK, BLK), lambda n: (0, n, 0))
    kv_shape = jax.ShapeDtypeStruct((2, length, BLK), F32)
    return pl.pallas_call(
        body, name=name, grid=(length // BLK,),
        in_specs=_attn_in_specs() + [pl.BlockSpec((BLK, D_MODEL), lambda n: (n, 0)), pl.BlockSpec(memory_space=pl.ANY)],
        out_specs=[pl.BlockSpec((BLK, D_MODEL), lambda n: (n, _OFF["q"] // D_MODEL)), blk3, blk3, blk3, blk3,
                   pl.BlockSpec((1, ATT_HEADS), lambda n: (0, 0))],
        out_shape=[jax.ShapeDtypeStruct((length, NP), BF), kv_shape, kv_shape, kv_shape, kv_shape,
                   jax.ShapeDtypeStruct((1, ATT_HEADS), F32)],
        input_output_aliases={7: 0},
        compiler_params=_params(1),
    )(proj, k2, k2, v2, v2, sinks, datt, dproj)


DMA_CHUNK_BYTES = 1 << 20
N_STAGE = 12
LOOKAHEAD = 5


def _piece_chunks(shape, itemsize):
    if len(shape) < 2 or shape[-2] % 16 != 0:
        return [()]
    rows, cols = shape[-2:]
    step = min(rows, max(16, DMA_CHUNK_BYTES // (cols * itemsize) // 16 * 16))
    out = []
    for lead in itertools.product(*[range(d) for d in shape[:-2]]):
        for r0 in range(0, rows, step):
            out.append(lead + (pl.ds(r0, min(step, rows - r0)),))
    return out


def _exchange(srcs, group, bcast, name):
    size = {"c": 2, "xy": 4, "all": 8}[group]
    npeer = size - 1
    n = len(srcs)
    for s in srcs:
        assert bcast or s.shape[0] == size
    pieces = [s.shape if bcast else s.shape[1:] for s in srcs]
    chunks = [_piece_chunks(p, s.dtype.itemsize) for p, s in zip(pieces, srcs)]

    def body(*refs):
        src, out = refs[:n], refs[n:2 * n]
        load_sems, send_sems, recv_sems = refs[2 * n:]
        x, y, c = lax.axis_index("x"), lax.axis_index("y"), lax.axis_index("c")
        if group == "c":
            me = c
        elif group == "xy":
            me = 2 * x + y
        else:
            me = 4 * x + 2 * y + c

        def device(j):
            if group == "c":
                return (x, y, j)
            if group == "xy":
                return (j // 2, j % 2, c)
            return (j // 4, (j // 2) % 2, j % 2)

        def part(ref, idx):
            return ref.at[idx] if idx else ref

        def piece(a, j):
            return src[a] if bcast else src[a].at[j]

        for a in range(n):
            jobs = [(None, idx) for idx in chunks[a]] if bcast else [(d, idx) for idx in chunks[a] for d in range(1, size)]
            full_rows = chunks[a][0][-1].size if chunks[a][0] else None
            slot_shape = (full_rows, pieces[a][-1]) if chunks[a][0] else pieces[a]

            def stream(stage, a=a, jobs=jobs, full_rows=full_rows):
                def slot(q):
                    idx = jobs[q][1]
                    view = stage.at[q % N_STAGE]
                    if idx and idx[-1].size != full_rows:
                        view = view.at[pl.ds(0, idx[-1].size)]
                    return view

                def load(q):
                    d, idx = jobs[q]
                    j = me if d is None else jnp.bitwise_xor(me, d)
                    return pltpu.make_async_copy(part(piece(a, j), idx), slot(q), load_sems.at[q % N_STAGE])

                def sends(q):
                    d, idx = jobs[q]
                    return [pltpu.make_async_remote_copy(
                        src_ref=slot(q), dst_ref=part(out[a].at[me], idx),
                        send_sem=send_sems.at[(q % N_STAGE) * npeer + dd - 1], recv_sem=recv_sems.at[a * size + dd],
                        device_id=device(jnp.bitwise_xor(me, dd)), device_id_type=pl.DeviceIdType.MESH)
                        for dd in (range(1, size) if d is None else (d,))]

                nq = len(jobs)
                for q in range(nq + LOOKAHEAD):
                    if q < nq:
                        if q >= N_STAGE:
                            for cp in sends(q - N_STAGE):
                                cp.wait_send()
                        load(q).start()
                    if q >= LOOKAHEAD:
                        load(q - LOOKAHEAD).wait()
                        for cp in sends(q - LOOKAHEAD):
                            cp.start()
                for q in range(max(0, nq - N_STAGE), nq):
                    for cp in sends(q):
                        cp.wait_send()

            pl.run_scoped(stream, pltpu.VMEM((N_STAGE,) + tuple(slot_shape), srcs[a].dtype))

        for a in range(n):
            for d in range(1, size):
                j = jnp.bitwise_xor(me, d)
                pltpu.make_async_remote_copy(
                    src_ref=piece(a, j), dst_ref=out[a].at[j], send_sem=send_sems.at[0], recv_sem=recv_sems.at[a * size + d],
                    device_id=device(j), device_id_type=pl.DeviceIdType.MESH).wait_recv()

    any_spec = pl.BlockSpec(memory_space=pl.ANY)
    return pl.pallas_call(
        body, name=name, in_specs=[any_spec] * n, out_specs=[any_spec] * n,
        out_shape=[jax.ShapeDtypeStruct((size,) + tuple(p), s.dtype) for p, s in zip(pieces, srcs)],
        scratch_shapes=[pltpu.SemaphoreType.DMA((N_STAGE,)), pltpu.SemaphoreType.DMA((N_STAGE * npeer,)),
                        pltpu.SemaphoreType.DMA((n * size,))],
        compiler_params=pltpu.CompilerParams(vmem_limit_bytes=VMEM_LIMIT_BYTES),
    )(*srcs)


def _with_own(outs, owns, me):
    return [lax.dynamic_update_index_in_dim(o, w, me, 0) for o, w in zip(outs, owns)]


def _sum_slots(arr, name):
    k = arr.shape[0]
    rest = arr.shape[1:]
    width = rest[-1]
    rows_per = math.prod(rest[:-1])
    a2 = arr.reshape(k * rows_per, width)
    tile = rows_per
    for cand in (256, 128, 64, 32, 16, 8):
        if rows_per % cand == 0:
            tile = cand
            break
    nt = rows_per // tile

    def body(*refs):
        acc = refs[0][...]
        for r in refs[1:k]:
            acc = acc + r[...]
        refs[k][...] = acc

    in_specs = [pl.BlockSpec((tile, width), functools.partial(lambda i, s: (s * nt + i, 0), s=s)) for s in range(k)]
    out = pl.pallas_call(
        body, name=name, grid=(nt,), in_specs=in_specs, out_specs=pl.BlockSpec((tile, width), lambda i: (i, 0)),
        out_shape=jax.ShapeDtypeStruct((rows_per, width), arr.dtype), compiler_params=_params(1),
    )(*([a2] * k))
    return out.reshape(rest)


def _sum_pieces(recv, own, own_slotted, me, name):
    k = recv.shape[0]
    rest = recv.shape[1:]
    width = rest[-1]
    rows_per = math.prod(rest[:-1])
    r2 = recv.reshape(k * rows_per, width)
    o2 = own.reshape(-1, width)
    tile = rows_per
    for cand in (256, 128, 64, 32, 16, 8):
        if rows_per % cand == 0:
            tile = cand
            break
    nt = rows_per // tile

    def body(me_ref, *refs):
        mine = refs[k][...].astype(F32)
        acc = None
        for j in range(k):
            term = jnp.where(me_ref[0] == j, mine, refs[j][...].astype(F32))
            acc = term if acc is None else acc + term
        refs[k + 1][...] = acc
        refs[k + 2][...] = acc.astype(BF)

    in_specs = [pl.BlockSpec((tile, width), functools.partial(lambda i, m, s: (s * nt + i, 0), s=s)) for s in range(k)]
    in_specs.append(pl.BlockSpec((tile, width), (lambda i, m: (m[0] * nt + i, 0)) if own_slotted else (lambda i, m: (i, 0))))
    out_spec = pl.BlockSpec((tile, width), lambda i, m: (i, 0))
    out, out_bf = pl.pallas_call(
        body, name=name,
        grid_spec=pltpu.PrefetchScalarGridSpec(num_scalar_prefetch=1, grid=(nt,), in_specs=in_specs, out_specs=[out_spec, out_spec]),
        out_shape=[jax.ShapeDtypeStruct((rows_per, width), F32), jax.ShapeDtypeStruct((rows_per, width), BF)],
        compiler_params=_params(1),
    )(jnp.reshape(me, (1,)).astype(jnp.int32), *([r2] * k), o2)
    return out.reshape(rest), out_bf.reshape(rest)


def _adamw(w, g, m, v, name):
    def fn(w_, g_, m_, v_):
        m1 = ADAM_B1 * m_ + (1.0 - ADAM_B1) * g_
        v1 = ADAM_B2 * v_ + (1.0 - ADAM_B2) * (g_ * g_)
        m_hat = m1 / (1.0 - ADAM_B1 ** ADAM_STEP)
        v_hat = v1 / (1.0 - ADAM_B2 ** ADAM_STEP)
        delta = -ADAM_LR * (m_hat / (jnp.sqrt(v_hat) + ADAM_EPS) + ADAM_WD * w_)
        return (delta, m1, v1), ()

    rows, width = w.shape
    tile = rows
    for cand in (256, 128, 64, 32, 16, 8):
        if rows % cand == 0:
            tile = cand
            break
    o, _ = _rows(fn, [(w, 0, width), (g, 0, width), (m, 0, width), (v, 0, width)], [],
                 [(width, F32)] * 3, [], tile, name)
    return o


def _col_pieces(start, width, shard):
    out = []
    while width:
        chip, off = divmod(start, shard)
        take = min(width, shard - off)
        out.append((chip, off, take))
        start, width = start + take, width - take
    return out


def _gather_cols(parts, ranges, pad=0):
    shard = parts.shape[3]
    halves = []
    for h in range(2):
        cols = [parts[h, chip, :, off:off + w] for s, wd in ranges for chip, off, w in _col_pieces(s, wd, shard)]
        if pad:
            cols.append(jnp.zeros((parts.shape[2], pad), parts.dtype))
        halves.append(jnp.concatenate(cols, axis=1))
    return jnp.concatenate(halves, axis=0)


def _scatter_cols(g, ranges, shard):
    starts, pos = {}, 0
    for s, wd in ranges:
        starts[s] = (pos, wd)
        pos += wd
    order = sorted(starts)
    hr = g.shape[0] // 2
    out = []
    for h in range(2):
        per_chip = []
        for chip in range(4):
            cols = []
            for s in order:
                p0, wd = starts[s]
                lo, hi = max(s, chip * shard), min(s + wd, (chip + 1) * shard)
                if lo < hi:
                    cols.append(g[h * hr:(h + 1) * hr, p0 + lo - s:p0 + hi - s])
            per_chip.append(jnp.concatenate(cols, axis=1))
        out.append(jnp.stack(per_chip))
    return jnp.stack(out)


_IN_RANGES = tuple((o, wd) for _, o, wd in _PACK)


def _double_heads(t):
    length = t.shape[0]
    h = jnp.transpose(t.reshape(length, 2, 64), (1, 0, 2))
    return jnp.concatenate([h, h], axis=-1)


def _fold_heads(d_cur, d_prev):
    length = d_cur.shape[1]
    d = d_cur + jnp.concatenate([d_prev[:, BLK:], jnp.zeros((2, BLK, BLK), F32)], axis=1)
    d = d[..., :64] + d[..., 64:]
    return jnp.transpose(d, (1, 0, 2)).reshape(length, 128)


def _ln_fwd(a, r, g, b, name):
    def fn(a_, r_, g_, b_):
        y = _layer_norm(ALPHA * a_ + r_, g_, b_)
        return (y, y), ()
    o, _ = _rows(fn, [(a, 0, D_MODEL), (r, 0, D_MODEL)], [g, b], [(D_MODEL, F32), (D_MODEL, BF)], [], ROW_TILE, name)
    return o


def _ln_bwd(a, r, d1, d2, g, b, name):
    def fn(a_, r_, d1_, d2_, g_, b_):
        _, vjp = jax.vjp(_layer_norm, ALPHA * a_ + r_, g_, b_)
        ds, dg, db = vjp(ALPHA * d1_ + d2_)
        return (ds, ds), (dg, db)
    o, acc = _rows(fn, [(a, 0, D_MODEL), (r, 0, D_MODEL), (d1, 0, D_MODEL), (d2, 0, D_MODEL)], [g, b],
                   [(D_MODEL, F32), (D_MODEL, BF)], [(1, D_MODEL), (1, D_MODEL)], ROW_TILE, name)
    return o[0], o[1], acc[0], acc[1]


def _gate_mix(proj, gcb, ya, yb, wmo, name):
    length = proj.shape[0]
    tile = min(ROW_TILE, length)

    def body(ga, gb, ya_r, yb_r, w_ref, ub_ref, mix_ref):
        u = (_sigmoid(ga[...]) * ya_r[...].astype(F32) + _sigmoid(gb[...]) * yb_r[...].astype(F32)).astype(BF)
        ub_ref[...] = u
        mix_ref[...] = _dot(u, w_ref[...], 1, 0)

    def rows(cb):
        return pl.BlockSpec((tile, D_MODEL), lambda i: (i, cb))

    return pl.pallas_call(
        body, name=name, grid=(length // tile,),
        in_specs=[rows(gcb), rows(gcb + 1), rows(0), rows(0), pl.BlockSpec((D_MODEL, D_MODEL), lambda i: (0, 0))],
        out_specs=[rows(0), rows(0)],
        out_shape=[jax.ShapeDtypeStruct((length, D_MODEL), BF), jax.ShapeDtypeStruct((length, D_MODEL), F32)],
        compiler_params=_params(1),
    )(proj, proj, ya, yb, wmo)


def _layer_fwd(l, h, hb, w, p, hosted=None, hosted_attn=None, hosted_conv=None, late_weights=None):
    sv = {"h": h, "hb": hb}
    proj = _mm(hb, w["win"], "nn", F32, 4096, 512, 1024, f"proj{l}")
    sv["proj"] = proj
    pre = _conv_fwd(proj, p["conv_w"], p["conv_b"], f"conv{l}", hosted_conv)
    if hosted_conv:
        pre, got = pre
        w.update(late_weights(got))
    sv["pre"] = pre
    dt_t = jnp.transpose(proj[:, _OFF["dt"]:_OFF["dt"] + SSD_HEADS])
    sv["dt_t"] = dt_t
    res = _ssd_fwd(pre, proj, dt_t, p["dt_bias"], p["a_log"], p["d_skip_c"], p["ssd_norm_w"], f"ssd{l}", hosted)
    yn, s_in = res[:2]
    if hosted:
        sv["hosted"] = res[2]
    sv["yn"], sv["s_in"] = yn, s_in
    ya = _mm(yn, w["wso"], "nn", BF, 2048, 1024, 2048, f"ssdout{l}")
    k2 = _double_heads(proj[:, _OFF["k"]:_OFF["k"] + 128])
    v2 = _double_heads(proj[:, _OFF["v"]:_OFF["v"] + 128])
    sv["k2"], sv["v2"] = k2, v2
    att = _attn_fwd(proj, k2, v2, p["att_sinks"], f"attn{l}", hosted_attn)
    if hosted_attn:
        att, sv["hosted_attn"] = att
    sv["att"] = att
    yb = _mm(att, w["wao"], "nn", BF, 2048, 1024, 1024, f"attout{l}")
    sv["ya"], sv["yb"] = ya, yb
    gcb = _OFF["gates"] // 1024

    ub, mix = _gate_mix(proj, gcb, ya, yb, w["wmo"], f"gatemix{l}")
    sv["ub"] = ub
    sv["mix"] = mix
    h1, h1b = _ln_fwd(h, mix, p["ln_mix_g"], p["ln_mix_b"], f"lnmix{l}")
    sv["h1"], sv["h1b"] = h1, h1b
    gu = _mm(h1b, w["wgu"], "nn", BF, 4096, 512, 1024, f"ffnin{l}")
    sv["gu"] = gu

    def act_fn(g_, u_):
        return (_silu(g_.astype(F32)) * u_.astype(F32),), ()
    (act,), _ = _rows(act_fn, [(gu, 0, FFN_HIDDEN), (gu, 1, FFN_HIDDEN)], [], [(FFN_HIDDEN, BF)], [], ROW_TILE, f"swiglu{l}")
    sv["act"] = act
    ffn = _mm(act, w["wd"], "nn", F32, 1024, 1024, FFN_HIDDEN, f"ffnout{l}")
    sv["ffn"] = ffn
    return sv


def _layer_bwd(l, sv, w, p, ds_f, ds_b, hosted=(), early_hook=None, late_hook=None):
    g = {}
    g["w_ffn_down"] = _mm(sv["act"], ds_b, "tn", F32, 1408, 1024, 2048, f"dwd{l}")
    dact = _mm(ds_b, w["wd"], "nt", BF, 1024, FFN_HIDDEN, 1024, f"dact{l}")

    def act_bwd(g_, u_, d_):
        _, vjp = jax.vjp(lambda a, b: _silu(a) * b, g_.astype(F32), u_.astype(F32))
        dg_, du_ = vjp(d_.astype(F32))
        return (jnp.concatenate([dg_, du_], axis=1),), ()
    (dgu,), _ = _rows(act_bwd, [(sv["gu"], 0, FFN_HIDDEN), (sv["gu"], 1, FFN_HIDDEN), (dact, 0, FFN_HIDDEN)], [],
                      [(2 * FFN_HIDDEN, BF)], [], ROW_TILE // 2, f"dswiglu{l}")
    dwgu = _mm(sv["h1b"], dgu, "tn", F32, 1024, 1408, 2048, f"dwgu{l}")
    g["w_ffn_gate"], g["w_ffn_up"] = dwgu[:, :FFN_HIDDEN], dwgu[:, FFN_HIDDEN:]
    dh1 = _mm(dgu, w["wgu"], "nt", F32, 1024, 1024, 2816, f"dh1{l}")
    ds2_f, ds2_b, g["ln_mix_g"], g["ln_mix_b"] = _ln_bwd(sv["h"], sv["mix"], ds_f, dh1, p["ln_mix_g"], p["ln_mix_b"], f"dlnmix{l}")
    g["w_mix_out"] = _mm(sv["ub"], ds2_b, "tn", F32, 1024, 1024, 2048, f"dwmo{l}")
    du = _mm(ds2_b, w["wmo"], "nt", BF, 1024, 1024, 1024, f"du{l}")
    proj = sv["proj"]
    gcb = _OFF["gates"] // 1024

    def gate_bwd(ga, gb, ya_, yb_, du_):
        _, vjp = jax.vjp(lambda a, b, c, d: _sigmoid(a) * c + _sigmoid(b) * d, ga, gb, ya_.astype(F32), yb_.astype(F32))
        dga, dgb, dya, dyb = vjp(du_.astype(F32))
        return (jnp.concatenate([dga, dgb], axis=1), dya, dyb), ()
    (dproj, dya, dyb), _ = _rows(
        gate_bwd, [(proj, gcb, 1024), (proj, gcb + 1, 1024), (sv["ya"], 0, 1024), (sv["yb"], 0, 1024), (du, 0, 1024)], [],
        [(2048, BF, NP), (D_MODEL, BF), (D_MODEL, BF)], [], ROW_TILE, f"dgate{l}")
    g["w_att_out"] = _mm(sv["att"], dyb, "tn", F32, 1024, 1024, 2048, f"dwao{l}")
    datt = _mm(dyb, w["wao"], "nt", BF, 1024, 1024, 1024, f"datt{l}")
    g["w_ssd_out"] = _mm(sv["yn"], dya, "tn", F32, 1024, 1024, 2048, f"dwso{l}")
    dyn = _mm(dya, w["wso"], "nt", F32, 1024, 1024, 1024, f"dyn{l}")
    hosted = list(hosted) + (list(early_hook(g)) if early_hook else [])
    dproj, dkp, dkc, dvp, dvc, g["att_sinks"] = _attn_bwd(proj, sv["k2"], sv["v2"], p["att_sinks"], datt, dproj, f"dattn{l}")
    res = _ssd_bwd(sv["pre"], proj, sv["dt_t"], p["dt_bias"], p["a_log"], p["d_skip_c"], p["ssd_norm_w"], sv["s_in"], dyn,
                   dproj, f"dssd{l}", (hosted, False) if hosted else None)
    (dpre, dproj, ddt_t, g["dt_bias"], g["a_log"], ddsk, g["ssd_norm_w"]) = res[:7]
    if hosted:
        g["hosted"] = res[7]
    g["d_skip"] = jnp.sum(ddsk.reshape(SSD_HEADS, 64), axis=1)
    dproj, g["conv_w"], g["conv_b"] = _conv_bwd(dpre, proj, p["conv_w"], dproj, 0, f"dconv{l}")
    length = proj.shape[0]
    tail = jnp.concatenate([_fold_heads(dkc, dkp).astype(BF), _fold_heads(dvc, dvp).astype(BF),
                            jnp.transpose(ddt_t).astype(BF), jnp.zeros((length, _PAD), BF)], axis=1)
    dproj = lax.dynamic_update_slice(dproj, tail, (0, _OFF["k"]))
    g["w_in"] = _mm(sv["hb"], dproj, "tn", F32, 1024, 2176, 1024, f"dwin{l}")
    late = list(late_hook(g)) if late_hook else []
    dh = _mm(dproj, w["win"], "nt", F32, 1024, 1024, 4352, f"dh{l}", (late, False) if late else None)
    if late:
        dh, g["late_hosted"] = dh
    return g, ds2_f, dh


def kernel(x, ln_in_g, ln_in_b, w_in, conv_w, conv_b, dt_bias, a_log, d_skip, ssd_norm_w, att_sinks, w_ssd_out, w_att_out, w_mix_out, ln_mix_g, ln_mix_b, w_ffn_gate, w_ffn_up, w_ffn_down, ln_ffn_g, ln_ffn_b, loss_target, m_ln_in_g, m_ln_in_b, m_w_in, m_conv_w, m_conv_b, m_dt_bias, m_a_log, m_d_skip, m_ssd_norm_w, m_att_sinks, m_w_ssd_out, m_w_att_out, m_w_mix_out, m_ln_mix_g, m_ln_mix_b, m_w_ffn_gate, m_w_ffn_up, m_w_ffn_down, m_ln_ffn_g, m_ln_ffn_b, v_ln_in_g, v_ln_in_b, v_w_in, v_conv_w, v_conv_b, v_dt_bias, v_a_log, v_d_skip, v_ssd_norm_w, v_att_sinks, v_w_ssd_out, v_w_att_out, v_w_mix_out, v_ln_mix_g, v_ln_mix_b, v_w_ffn_gate, v_w_ffn_up, v_w_ffn_down, v_ln_ffn_g, v_ln_ffn_b):
    env = dict(locals())
    wts = {n: env[n] for n in WEIGHTS}
    mom1 = {n: env["m_" + n] for n in WEIGHTS}
    mom2 = {n: env["v_" + n] for n in WEIGHTS}
    ci = lax.axis_index("c")
    chip = 2 * lax.axis_index("x") + lax.axis_index("y")
    xs_ = x[0]
    tgt = loss_target[0]

    COLS = ("w_in", "w_ffn_gate", "w_ffn_up")

    def half(t):
        return lax.dynamic_slice_in_dim(t, ci * (t.shape[0] // 2), t.shape[0] // 2, axis=0)

    def assemble(name, parts):
        if name == "w_in":
            return _gather_cols(parts, _IN_RANGES, _PAD)
        if name in COLS:
            return _gather_cols(parts, ((0, 4 * parts.shape[3]),))
        return jnp.concatenate([parts[h, s] for s in range(4) for h in range(2)], axis=0)

    def split(name, gfull):
        if name == "w_in":
            return _scatter_cols(gfull, _IN_RANGES, IN_DIM // 4)
        if name in COLS:
            return _scatter_cols(gfull, ((0, gfull.shape[1]),), gfull.shape[1] // 4)
        hr = gfull.shape[0] // 8
        return jnp.stack([jnp.stack([gfull[(2 * s + h) * hr:(2 * s + h + 1) * hr] for s in range(4)]) for h in range(2)])

    mine = [[half(wts[n][l].astype(BF)) for n in BIG] for l in range(DEPTH)]
    conv_all, = _with_own(_exchange([conv_w], "xy", True, "gather_conv"), [conv_w], chip)
    conv_full = jnp.transpose(conv_all, (1, 2, 0, 3)).reshape(DEPTH, SSD_CONV, CONV_DIM)

    KEYS = {"w_in": "win", "w_ssd_out": "wso", "w_att_out": "wao", "w_mix_out": "wmo", "w_ffn_down": "wd"}

    def layer_weights(l, names, by_chip):
        own = [mine[l][BIG.index(n)] for n in names]
        by_chip = _with_own(by_chip, own, chip)
        parts = _with_own(_exchange(by_chip, "c", True, f"gather_cores{l}_{names[0]}"), by_chip, ci)
        fw = {n: assemble(n, part) for n, part in zip(names, parts)}
        w = {KEYS[n]: fw[n] for n in names if n in KEYS}
        if "w_ffn_gate" in fw:
            w["wgu"] = jnp.concatenate([fw["w_ffn_gate"], fw["w_ffn_up"]], axis=1)
        return w

    layer_p = []
    for l in range(DEPTH):
        layer_p.append({
            "conv_w": conv_full[l], "conv_b": conv_b[l][None],
            "dt_bias": dt_bias[l][:, None], "a_log": a_log[l][:, None],
            "d_skip_c": jnp.repeat(d_skip[l], 64)[None], "ssd_norm_w": ssd_norm_w[l][None],
            "att_sinks": att_sinks[l][None], "ln_mix_g": ln_mix_g[l][None], "ln_mix_b": ln_mix_b[l][None],
            "ln_ffn_g": ln_ffn_g[l][None], "ln_ffn_b": ln_ffn_b[l][None],
        })

    def ln_in_fn(x_, g_, b_):
        y = _layer_norm(x_, g_, b_)
        return (y, y), ()
    (h, hb), _ = _rows(ln_in_fn, [(xs_, 0, D_MODEL)], [ln_in_g[None], ln_in_b[None]], [(D_MODEL, F32), (D_MODEL, BF)], [], ROW_TILE, "ln_in")
    rest = tuple(n for n in BIG if n != "w_in")
    layer_w = [layer_weights(0, ("w_in",), _exchange([mine[0][BIG.index("w_in")]], "xy", True, "gather_chips0"))]
    behind_ssd = [i for i, n in enumerate(BIG) if n in COLS]
    behind_attn = [i for i, n in enumerate(BIG) if n not in COLS]
    saved = []
    for l in range(DEPTH):
        nxt = l + 1 < DEPTH
        sv = _layer_fwd(l, h, hb, layer_w[l], layer_p[l], ([mine[l + 1][i] for i in behind_ssd], True) if nxt else None,
                        ([mine[l + 1][i] for i in behind_attn], True) if nxt else None,
                        ([mine[l][BIG.index(n)] for n in rest], True) if l == 0 else None,
                        functools.partial(layer_weights, l, rest))
        saved.append(sv)
        if nxt:
            got = dict(zip(behind_ssd + behind_attn, sv["hosted"] + sv["hosted_attn"]))
            layer_w.append(layer_weights(l + 1, BIG, [got[i] for i in range(len(BIG))]))
            h, hb = _ln_fwd(sv["h1"], sv["ffn"], layer_p[l]["ln_ffn_g"], layer_p[l]["ln_ffn_b"], f"lnffn{l}")

    def loss_fn(a_, r_, t_, g_, b_):
        y, vjp = jax.vjp(_layer_norm, ALPHA * a_ + r_, g_, b_)
        err = y - t_
        ds, dg, db = vjp(err * (1.0 / D_MODEL))
        part = 0.5 * jnp.sum(jnp.mean(err * err, axis=-1, keepdims=True), axis=0, keepdims=True)
        return (ds, ds), (dg, db, jnp.broadcast_to(part, (1, BLK)))
    sv = saved[-1]
    (ds_f, ds_b), (dg_last, db_last, loss_part) = _rows(
        loss_fn, [(sv["h1"], 0, D_MODEL), (sv["ffn"], 0, D_MODEL), (tgt, 0, D_MODEL)],
        [layer_p[-1]["ln_ffn_g"], layer_p[-1]["ln_ffn_b"]], [(D_MODEL, F32), (D_MODEL, BF)],
        [(1, D_MODEL), (1, D_MODEL), (1, BLK)], ROW_TILE, "loss")
    loss = lax.psum(loss_part[0, 0], ("x", "y", "c"))

    def reduce_cores(l, g, names):
        src = [split(n, g[n]) for n in names]
        recv = _exchange([s.astype(BF) for s in src], "c", False, f"reduce_cores{l}_{names[0]}")
        return {n: _sum_pieces(r, o, True, ci, f"sum_cores{l}_{n}") for n, r, o in zip(names, recv, src)}

    def bf16_of(sums, names):
        return [sums[n][1] for n in names]

    def add_chips(l, sums, from_chips):
        return [_sum_pieces(from_chips[n], sums[n][0], True, chip, f"sum_chips{l}_{n}")[0] for n in BIG]

    EARLY = tuple(n for n in BIG if n != "w_in")
    grads = [None] * DEPTH
    chip_sums = [None] * DEPTH
    for l in reversed(range(DEPTH)):
        sv = saved[l]
        if l == DEPTH - 1:
            lg, lb = dg_last, db_last
        else:
            ds_f, ds_b, lg, lb = _ln_bwd(sv["h1"], sv["ffn"], d1, d2, layer_p[l]["ln_ffn_g"], layer_p[l]["ln_ffn_b"], f"dlnffn{l}")
        sums = {}

        def early_hook(g, l=l, sums=sums):
            sums.update(reduce_cores(l, g, EARLY))
            return bf16_of(sums, EARLY)

        def late_hook(g, l=l, sums=sums):
            sums.update(reduce_cores(l, g, ("w_in",)))
            return bf16_of(sums, ("w_in",))
        if l == 0:
            g, d1, d2 = _layer_bwd(l, sv, layer_w[l], layer_p[l], ds_f, ds_b, bf16_of(chip_sums[l + 1], BIG), early_hook, late_hook)
        else:
            g, d1, d2 = _layer_bwd(l, sv, layer_w[l], layer_p[l], ds_f, ds_b)
            sums.update(reduce_cores(l, g, BIG))
        g["ln_ffn_g"], g["ln_ffn_b"] = lg, lb
        grads[l] = g
        chip_sums[l] = sums

    def ln_in_bwd(x_, d1_, d2_, g_, b_):
        _, vjp = jax.vjp(_layer_norm, x_, g_, b_)
        dx, dg, db = vjp(ALPHA * d1_ + d2_)
        return (dx,), (dg, db)
    (grad_x,), (g_ln_in_g, g_ln_in_b) = _rows(
        ln_in_bwd, [(xs_, 0, D_MODEL), (d1, 0, D_MODEL), (d2, 0, D_MODEL)], [ln_in_g[None], ln_in_b[None]],
        [(D_MODEL, F32)], [(1, D_MODEL), (1, D_MODEL)], ROW_TILE, "dln_in")

    carried = grads[0]["hosted"]
    from_chips = dict(zip(EARLY, carried[len(BIG):]))
    from_chips["w_in"], = grads[0]["late_hosted"]
    reduced = add_chips(0, chip_sums[0], from_chips) + add_chips(1, chip_sums[1], dict(zip(BIG, carried[:len(BIG)])))
    halves = _with_own(_exchange(reduced, "c", True, "swap_halves"), reduced, ci)
    shards = [t.reshape(2 * t.shape[1], t.shape[2]) for t in halves]
    big_grad = {n: jnp.stack([shards[l * len(BIG) + i] for l in range(DEPTH)]).reshape(wts[n].shape) for i, n in enumerate(BIG)}

    small_g = {"ln_in_g": g_ln_in_g[0], "ln_in_b": g_ln_in_b[0]}
    for n in SMALL[2:]:
        small_g[n] = jnp.stack([grads[l][n].reshape(wts[n].shape[1:] if n != "conv_w" else (SSD_CONV, CONV_DIM)) for l in range(DEPTH)])
    flat = jnp.concatenate([small_g[n].reshape(-1) for n in SMALL])
    n_small = flat.shape[0]
    width = -(-n_small // 1024) * 1024
    flat = jnp.pad(flat, (0, width - n_small)).reshape(8, width // 8)
    gathered, = _with_own(_exchange([flat], "all", True, "gather_small"), [flat], 2 * chip + ci)
    total = _sum_slots(gathered, "sum_small").reshape(-1)
    small_grad, off = {}, 0
    for n in SMALL:
        shp = small_g[n].shape
        cnt = math.prod(shp)
        small_grad[n] = total[off:off + cnt].reshape(shp)
        off += cnt
    small_grad["conv_w"] = lax.dynamic_slice_in_dim(small_grad["conv_w"], chip * (CONV_DIM // 4), CONV_DIM // 4, axis=2)

    out_g, out_d, out_m, out_v = {}, {}, {}, {}
    for n in BIG:
        shp = wts[n].shape
        two_d = lambda t: t.reshape(shp[0] * shp[1], shp[2])
        d_, m_, v_ = _adamw(two_d(wts[n]), two_d(big_grad[n]), two_d(mom1[n]), two_d(mom2[n]), f"adamw_{n}")
        out_g[n], out_d[n], out_m[n], out_v[n] = big_grad[n], d_.reshape(shp), m_.reshape(shp), v_.reshape(shp)

    def flat_small(d):
        f = jnp.concatenate([d[n].reshape(-1) for n in SMALL])
        return jnp.pad(f, (0, swidth - f.shape[0])).reshape(8, swidth // 8)
    n_sw = sum(math.prod(wts[n].shape) for n in SMALL)
    swidth = -(-n_sw // 1024) * 1024
    d_, m_, v_ = _adamw(flat_small(wts), flat_small(small_grad), flat_small(mom1), flat_small(mom2), "adamw_small")
    off = 0
    for n in SMALL:
        shp = wts[n].shape
        cnt = math.prod(shp)
        out_g[n] = small_grad[n]
        out_d[n] = d_.reshape(-1)[off:off + cnt].reshape(shp)
        out_m[n] = m_.reshape(-1)[off:off + cnt].reshape(shp)
        out_v[n] = v_.reshape(-1)[off:off + cnt].reshape(shp)
        off += cnt

    return (loss, grad_x[None], *[out_g[n] for n in WEIGHTS], *[out_d[n] for n in WEIGHTS],
            *[out_m[n] for n in WEIGHTS], *[out_v[n] for n in WEIGHTS])
```
